```python
import math
import jax
import jax.numpy as jnp
from jax import lax
import numpy as np


D_MODEL = 1024
BATCH = 4
SEQ = 8192
DEPTH = 2

DEEPNORM_ALPHA = (2 * DEPTH) ** 0.25
DEEPNORM_BETA = (8 * DEPTH) ** -0.25
LN_EPS = 1e-5
RMS_EPS = 1e-6
NUM_BUCKETS = 32
REL_MAX_DIST = 2048
REL_HEADS = 4
GDN_HEADS = 6
GDN_DK = 128
GDN_DV = 128
GDN_CONV = 4
GDN_CHUNK = 64
GDN_QK = GDN_HEADS * GDN_DK
GDN_V = GDN_HEADS * GDN_DV
SWA_CONFIGS = ((128, 1), (512, 4), (2048, 16))
SWA_GROUPS = len(SWA_CONFIGS)
SWA_HEADS = 4
SWA_HEAD_DIM = 64
SWA_BLOCK = 128
SWA_QKV = SWA_GROUPS * SWA_HEADS * SWA_HEAD_DIM
SWA_OUT = SWA_HEADS * SWA_HEAD_DIM
DIFF_HEADS = 4
DIFF_HEAD_DIM = 64
DIFF_QK = DIFF_HEADS * 2 * DIFF_HEAD_DIM
DIFF_V = DIFF_HEADS * 2 * DIFF_HEAD_DIM
Q_BLOCK = 128
GLA_HEADS = 4
GLA_DK = 64
GLA_DV = 128
GLA_GATE_RANK = 16
GLA_GATE_TAU = 16.0
GLA_CHUNK = 64
GLA_QK = GLA_HEADS * GLA_DK
GLA_V = GLA_HEADS * GLA_DV
D_FF = 2816
FFN_CONV = 3

EVEN_SIZES = (GDN_QK, GDN_QK, GDN_V, GDN_V, GDN_HEADS, GDN_HEADS, SWA_QKV, SWA_QKV, SWA_QKV)
EVEN_IN = sum(EVEN_SIZES)
EVEN_SPLITS = tuple(int(v) for v in np.cumsum(EVEN_SIZES)[:-1])
EVEN_OUT = GDN_V + SWA_OUT
ODD_SIZES = (DIFF_QK, DIFF_QK, DIFF_V, GLA_QK, GLA_QK, GLA_V, GLA_V, GLA_GATE_RANK)
ODD_IN = sum(ODD_SIZES)
ODD_SPLITS = tuple(int(v) for v in np.cumsum(ODD_SIZES)[:-1])
ODD_OUT = DIFF_V + GLA_V
N_EVEN = (DEPTH + 1) // 2
N_ODD = DEPTH // 2

kernel_name = 'hybrid_gdn_dilated_diff_gla_trunk'


def layer_norm(x, g, b):
    xf = x.astype(jnp.float32)
    mu = jnp.mean(xf, -1, keepdims=True)
    var = jnp.mean(jnp.square(xf - mu), -1, keepdims=True)
    y = (xf - mu) * lax.rsqrt(var + LN_EPS) * g.astype(jnp.float32) + b.astype(jnp.float32)
    return y.astype(x.dtype)


def rms_norm(x, w):
    xf = x.astype(jnp.float32)
    return xf * lax.rsqrt(jnp.mean(xf * xf, -1, keepdims=True) + RMS_EPS) * w.astype(jnp.float32)


def l2_normalize(x):
    return x * lax.rsqrt(jnp.sum(x * x, -1, keepdims=True) + RMS_EPS)


def causal_depthwise_conv(x, w):
    width, seq = w.shape[0], x.shape[1]
    xp = jnp.pad(x, ((0, 0), (width - 1, 0), (0, 0)))
    y = w[0] * xp[:, :seq]
    for j in range(1, width):
        y = y + w[j] * xp[:, j:j + seq]
    return y


def rel_bucket(dist):
    max_exact = NUM_BUCKETS // 2
    d = jnp.maximum(dist, 1).astype(jnp.float32)
    large = max_exact + (jnp.log(d / max_exact) / math.log(REL_MAX_DIST / max_exact)
                         * (NUM_BUCKETS - max_exact)).astype(jnp.int32)
    large = jnp.minimum(large, NUM_BUCKETS - 1)
    return jnp.where(dist < max_exact, dist, large)


def gated_delta_rule(q, k, v, g, beta):
    bsz, seq, nh, dk = q.shape
    dv = v.shape[-1]
    c = GDN_CHUNK
    n = seq // c

    def chunk(t):
        return jnp.moveaxis(t.reshape((bsz, n, c, nh) + t.shape[3:]), 3, 1)

    qc, kc, vc, gc, bc = chunk(q), chunk(k), chunk(v), chunk(g), chunk(beta)
    gam = jnp.cumsum(gc, axis=-1)
    idx = jnp.arange(c)
    incl = idx[:, None] >= idx[None, :]
    strict = idx[:, None] > idx[None, :]
    decay = jnp.exp(jnp.where(incl, gam[..., :, None] - gam[..., None, :], -jnp.inf))
    kk = jnp.einsum('bhnid,bhnjd->bhnij', kc, kc)
    lower = jnp.eye(c, dtype=jnp.float32) + jnp.where(strict, bc[..., :, None] * kk * decay, 0.0)
    rhs = jnp.concatenate([vc * bc[..., None], kc * (bc * jnp.exp(gam))[..., None]], axis=-1)
    sol = lax.linalg.triangular_solve(lower, rhs, left_side=True, lower=True, unit_diagonal=True)
    u, w = sol[..., :dv], sol[..., dv:]
    qk = jnp.einsum('bhnid,bhnjd->bhnij', qc, kc) * decay
    q_dec = qc * jnp.exp(gam)[..., None]
    k_dec = kc * jnp.exp(gam[..., -1:] - gam)[..., None]
    g_last = jnp.exp(gam[..., -1])

    def step(state, inp):
        u_n, w_n, qk_n, qd_n, kd_n, gl_n = inp
        delta = u_n - jnp.einsum('bhck,bhkv->bhcv', w_n, state)
        o_n = jnp.einsum('bhck,bhkv->bhcv', qd_n, state) + jnp.einsum('bhij,bhjv->bhiv', qk_n, delta)
        state = gl_n[..., None, None] * state + jnp.einsum('bhck,bhcv->bhkv', kd_n, delta)
        return state, o_n

    xs = tuple(jnp.moveaxis(t, 2, 0) for t in (u, w, qk, q_dec, k_dec, g_last))
    state0 = jnp.zeros((bsz, nh, dk, dv), jnp.float32)
    _, o = lax.scan(step, state0, xs)
    return o.transpose(1, 0, 3, 2, 4).reshape(bsz, seq, nh, dv)


def dilated_branch(q, k, v, window, dilation, rel_bias):
    bsz, seq, nh, dh = q.shape
    span = window // dilation
    sub = seq // dilation
    nb = -(-sub // SWA_BLOCK)
    padded = nb * SWA_BLOCK

    def to_blocks(t):
        t = t.reshape(bsz, sub, dilation, nh, dh).transpose(0, 2, 1, 3, 4)
        t = jnp.pad(t, ((0, 0), (0, 0), (0, padded - sub), (0, 0), (0, 0)))
        return t.reshape(bsz, dilation, nb, SWA_BLOCK, nh, dh)

    def with_prev(t):
        prev = jnp.pad(t, ((0, 0), (0, 0), (1, 0), (0, 0), (0, 0), (0, 0)))[:, :, :-1]
        return jnp.concatenate([prev, t], axis=3)

    qb = to_blocks(q)
    kb, vb = with_prev(to_blocks(k)), with_prev(to_blocks(v))
    qi = jnp.arange(SWA_BLOCK)[:, None] + SWA_BLOCK
    kj = jnp.arange(2 * SWA_BLOCK)[None, :]
    rel = qi - kj
    key_sub = jnp.arange(nb)[:, None, None] * SWA_BLOCK + kj[None] - SWA_BLOCK
    mask = (rel >= 0) & (rel <= span) & (key_sub >= 0)
    bias = jnp.moveaxis(rel_bias[rel_bucket(jnp.maximum(rel, 0) * dilation)], -1, 0)
    s = jnp.einsum('bgnqhd,bgnkhd->bgnhqk', qb, kb).astype(jnp.float32) * dh ** -0.5
    s = s + bias.astype(jnp.float32)
    s = jnp.where(mask[None, None, :, None], s, -jnp.inf)
    m = jnp.max(s, axis=-1, keepdims=True)
    p = jnp.exp(s - m)
    l = jnp.sum(p, axis=-1, keepdims=True)
    o = jnp.einsum('bgnhqk,bgnkhd->bgnqhd', (p / l).astype(v.dtype), vb)
    lse = (m + jnp.log(l))[..., 0]
    o = o.reshape(bsz, dilation, padded, nh, dh)[:, :, :sub]
    o = o.transpose(0, 2, 1, 3, 4).reshape(bsz, seq, nh, dh)
    lse = lse.transpose(0, 1, 2, 4, 3).reshape(bsz, dilation, padded, nh)[:, :, :sub]
    lse = lse.transpose(0, 2, 1, 3).reshape(bsz, seq, nh)
    return o, lse


def dilated_attention(q, k, v, rel_bias):
    bsz, seq = q.shape[0], q.shape[1]
    outs, lses = [], []
    for g, (window, dilation) in enumerate(SWA_CONFIGS):
        o, lse = dilated_branch(q[:, :, g], k[:, :, g], v[:, :, g], window, dilation, rel_bias)
        outs.append(o)
        lses.append(lse)
    wts = jax.nn.softmax(jnp.stack(lses, 0), axis=0)
    o = jnp.einsum('gbsh,gbshd->bshd', wts, jnp.stack(outs, 0).astype(jnp.float32))
    return o.reshape(bsz, seq, SWA_OUT)


def even_mixer(x, rel_bias, w_in, conv_w, a_log, dt_bias, norm_w, w_out):
    bsz, seq, _ = x.shape
    f32 = jnp.float32
    q_a, k_a, v_a, z_a, b_a, a_a, q_b, k_b, v_b = jnp.split(x @ w_in, EVEN_SPLITS, axis=-1)
    qkv = jax.nn.silu(causal_depthwise_conv(jnp.concatenate([q_a, k_a, v_a], -1), conv_w)).astype(f32)
    q_a, k_a, v_a = jnp.split(qkv, (GDN_QK, 2 * GDN_QK), axis=-1)
    hk = (bsz, seq, GDN_HEADS, GDN_DK)
    q_a = l2_normalize(q_a.reshape(hk)) * GDN_DK ** -0.5
    k_a = l2_normalize(k_a.reshape(hk))
    v_a = v_a.reshape(bsz, seq, GDN_HEADS, GDN_DV)
    beta = jax.nn.sigmoid(b_a.astype(f32))
    g = -jnp.exp(a_log.astype(f32)) * jax.nn.softplus(a_a.astype(f32) + dt_bias.astype(f32))
    o_a = gated_delta_rule(q_a, k_a, v_a, g, beta)
    o_a = rms_norm(o_a, norm_w) * jax.nn.silu(z_a.astype(f32).reshape(bsz, seq, GDN_HEADS, GDN_DV))
    o_a = o_a.reshape(bsz, seq, GDN_V).astype(x.dtype)
    shp = (bsz, seq, SWA_GROUPS, SWA_HEADS, SWA_HEAD_DIM)
    o_b = dilated_attention(q_b.reshape(shp), k_b.reshape(shp), v_b.reshape(shp), rel_bias).astype(x.dtype)
    return jnp.concatenate([o_a, o_b], axis=-1) @ w_out


def diff_attention(q, k, v, lam, rel_bias):
    bsz, seq, nh, _, dh = q.shape
    nq = seq // Q_BLOCK
    qb = q.reshape(bsz, nq, Q_BLOCK, nh, 2, dh).transpose(1, 0, 3, 4, 2, 5)
    kt = k.transpose(0, 2, 3, 1, 4)
    vt = v.transpose(0, 2, 1, 3)
    k_pos = jnp.arange(seq)

    def block(args):
        q_blk, start = args
        dist = (start + jnp.arange(Q_BLOCK))[:, None] - k_pos[None, :]
        bias = jnp.moveaxis(rel_bias[rel_bucket(jnp.maximum(dist, 0))], -1, 0).astype(jnp.float32)
        s = jnp.einsum('bhcqd,bhckd->bhcqk', q_blk, kt).astype(jnp.float32) * dh ** -0.5
        s = jnp.where(dist >= 0, s + bias[None, :, None], -jnp.inf)
        p = jax.nn.softmax(s, axis=-1)
        attn = p[:, :, 0] - lam * p[:, :, 1]
        return jnp.einsum('bhqk,bhkv->bhqv', attn.astype(v.dtype), vt)

    o = lax.map(block, (qb, jnp.arange(nq) * Q_BLOCK))
    return o.transpose(1, 0, 3, 2, 4).reshape(bsz, seq, nh, 2 * dh)


def gla_chunked(q, k, v, log_a):
    bsz, seq, nh, dk = q.shape
    dv = v.shape[-1]
    c = GLA_CHUNK
    n = seq // c

    def chunk(t):
        return t.reshape(bsz, n, c, nh, t.shape[-1]).transpose(1, 0, 3, 2, 4)

    qc, kc, vc = chunk(q), chunk(k), chunk(v)
    bc = jnp.cumsum(chunk(log_a), axis=3)
    idx = jnp.arange(c)
    incl = (idx[:, None] >= idx[None, :])[:, :, None]

    def step(state, inp):
        q_n, k_n, v_n, b_n = inp
        dec = jnp.exp(jnp.where(incl, b_n[..., :, None, :] - b_n[..., None, :, :], -jnp.inf))
        attn = jnp.einsum('bhik,bhjk,bhijk->bhij', q_n, k_n, dec)
        o_n = jnp.einsum('bhik,bhkv->bhiv', q_n * jnp.exp(b_n), state) + jnp.einsum('bhij,bhjv->bhiv', attn, v_n)
        b_last = b_n[..., -1:, :]
        state = jnp.exp(b_last[..., 0, :])[..., None] * state + jnp.einsum(
            'bhck,bhcv->bhkv', k_n * jnp.exp(b_last - b_n), v_n)
        return state, o_n

    state0 = jnp.zeros((bsz, nh, dk, dv), jnp.float32)
    _, o = lax.scan(step, state0, (qc, kc, vc, bc))
    return o.transpose(1, 0, 3, 2, 4).reshape(bsz, seq, nh, dv)


def odd_mixer(x, rel_bias, w_in, lam_params, diff_norm_w, gla_w_gate, gla_b_gate, gla_norm_w, w_out, lam_init):
    bsz, seq, _ = x.shape
    f32 = jnp.float32
    q_c, k_c, v_c, q_d, k_d, v_d, r_d, g_d = jnp.split(x @ w_in, ODD_SPLITS, axis=-1)
    lp = lam_params.astype(f32)
    lam = jnp.exp(jnp.sum(lp[0] * lp[1])) - jnp.exp(jnp.sum(lp[2] * lp[3])) + lam_init
    shp = (bsz, seq, DIFF_HEADS, 2, DIFF_HEAD_DIM)
    o_c = diff_attention(q_c.reshape(shp), k_c.reshape(shp),
                         v_c.reshape(bsz, seq, DIFF_HEADS, 2 * DIFF_HEAD_DIM), lam, rel_bias)
    o_c = (rms_norm(o_c, diff_norm_w) * (1.0 - lam_init)).reshape(bsz, seq, DIFF_V).astype(x.dtype)
    log_a = jax.nn.log_sigmoid((g_d @ gla_w_gate + gla_b_gate).astype(f32)) / GLA_GATE_TAU
    hk = (bsz, seq, GLA_HEADS, GLA_DK)
    o_d = gla_chunked(q_d.astype(f32).reshape(hk) * GLA_DK ** -0.5, k_d.astype(f32).reshape(hk),
                      v_d.astype(f32).reshape(bsz, seq, GLA_HEADS, GLA_DV), log_a.reshape(hk))
    o_d = rms_norm(o_d, gla_norm_w) * jax.nn.silu(r_d.astype(f32).reshape(bsz, seq, GLA_HEADS, GLA_DV))
    o_d = o_d.reshape(bsz, seq, GLA_V).astype(x.dtype)
    return jnp.concatenate([o_c, o_d], axis=-1) @ w_out


def conv_ffn(x, w_up, conv_w, conv_b, w_down):
    h = causal_depthwise_conv(x @ w_up, conv_w) + conv_b
    gate, val = jnp.split(h, 2, axis=-1)
    return (jax.nn.silu(gate) * val) @ w_down


def setup_inputs(seed: int = 0) -> dict:
    key = jax.random.key(seed)
    ks = jax.random.split(key, 24)
    f32 = jnp.float32

    def normal(k, shape, scale):
        return jax.random.normal(k, shape, f32) * scale

    dt = jnp.exp(jax.random.uniform(ks[5], (N_EVEN, GDN_HEADS), f32, math.log(1e-3), math.log(1e-1)))
    return {
        'x': normal(ks[0], (BATCH, SEQ, D_MODEL), 1.0),
        'rel_bias': normal(ks[1], (NUM_BUCKETS, REL_HEADS), 0.1),
        'w_in_even': normal(ks[2], (N_EVEN, D_MODEL, EVEN_IN), D_MODEL ** -0.5),
        'gdn_conv_w': normal(ks[3], (N_EVEN, GDN_CONV, 2 * GDN_QK + GDN_V), GDN_CONV ** -0.5),
        'gdn_a_log': jnp.log(jax.random.uniform(ks[4], (N_EVEN, GDN_HEADS), f32, 1.0, 16.0)),
        'gdn_dt_bias': dt + jnp.log(-jnp.expm1(-dt)),
        'gdn_norm_w': 1.0 + normal(ks[6], (N_EVEN, GDN_DV), 0.05),
        'w_out_even': normal(ks[7], (N_EVEN, EVEN_OUT, D_MODEL), EVEN_OUT ** -0.5 * DEEPNORM_BETA),
        'w_in_odd': normal(ks[8], (N_ODD, D_MODEL, ODD_IN), D_MODEL ** -0.5),
        'diff_lambda': normal(ks[9], (N_ODD, 4, DIFF_HEAD_DIM), 0.1),
        'diff_norm_w': 1.0 + normal(ks[10], (N_ODD, 2 * DIFF_HEAD_DIM), 0.05),
        'gla_w_gate': normal(ks[11], (N_ODD, GLA_GATE_RANK, GLA_QK), GLA_GATE_RANK ** -0.5),
        'gla_b_gate': normal(ks[12], (N_ODD, GLA_QK), 0.1),
        'gla_norm_w': 1.0 + normal(ks[13], (N_ODD, GLA_DV), 0.05),
        'w_out_odd': normal(ks[14], (N_ODD, ODD_OUT, D_MODEL), ODD_OUT ** -0.5 * DEEPNORM_BETA),
        'ffn_w_up': normal(ks[15], (DEPTH, D_MODEL, 2 * D_FF), D_MODEL ** -0.5),
        'ffn_conv_w': normal(ks[16], (DEPTH, FFN_CONV, 2 * D_FF), FFN_CONV ** -0.5),
        'ffn_conv_b': normal(ks[17], (DEPTH, 2 * D_FF), 0.02),
        'ffn_w_down': normal(ks[18], (DEPTH, D_FF, D_MODEL), D_FF ** -0.5 * DEEPNORM_BETA),
        'ln_g': 1.0 + normal(ks[19], (DEPTH, 2, D_MODEL), 0.05),
        'ln_b': normal(ks[20], (DEPTH, 2, D_MODEL), 0.02),
    }


def reference(x, rel_bias, w_in_even, gdn_conv_w, gdn_a_log, gdn_dt_bias, gdn_norm_w, w_out_even,
              w_in_odd, diff_lambda, diff_norm_w, gla_w_gate, gla_b_gate, gla_norm_w, w_out_odd,
              ffn_w_up, ffn_conv_w, ffn_conv_b, ffn_w_down, ln_g, ln_b):
    h = x
    for layer in range(DEPTH):
        i = layer // 2
        if layer % 2 == 0:
            mix = even_mixer(h, rel_bias, w_in_even[i], gdn_conv_w[i], gdn_a_log[i], gdn_dt_bias[i],
                             gdn_norm_w[i], w_out_even[i])
        else:
            lam_init = 0.8 - 0.6 * math.exp(-0.3 * layer)
            mix = odd_mixer(h, rel_bias, w_in_odd[i], diff_lambda[i], diff_norm_w[i], gla_w_gate[i],
                            gla_b_gate[i], gla_norm_w[i], w_out_odd[i], lam_init)
        h = layer_norm(DEEPNORM_ALPHA * h + mix, ln_g[layer, 0], ln_b[layer, 0])
        ffn = conv_ffn(h, ffn_w_up[layer], ffn_conv_w[layer], ffn_conv_b[layer], ffn_w_down[layer])
        h = layer_norm(DEEPNORM_ALPHA * h + ffn, ln_g[layer, 1], ln_b[layer, 1])
    return h
```

```python
import functools
import math

import numpy as np
import jax
import jax.numpy as jnp
from jax import lax
from jax.experimental import pallas as pl
from jax.experimental.pallas import tpu as pltpu

F32 = jnp.float32
BF16 = jnp.bfloat16
HI = lax.Precision.HIGHEST

D_MODEL = 1024
DEPTH = 2
DEEPNORM_ALPHA = (2 * DEPTH) ** 0.25
LN_EPS = 1e-5
RMS_EPS = 1e-6
NUM_BUCKETS = 32
REL_MAX_DIST = 2048
GDN_HEADS = 6
GDN_D = 128
GDN_CONV = 4
CHUNK = 64
SWA_CONFIGS = ((128, 1), (512, 4), (2048, 16))
SWA_HEADS = 4
SWA_DH = 64
SWA_BLOCK = 128
DIFF_HEADS = 4
DIFF_DH = 64
GLA_HEADS = 4
GLA_DK = 64
GLA_DV = 128
GLA_RANK = 16
GLA_TAU = 16.0
GLA_SUB = 16
D_FF = 2816
FFN_CONV = 3

LANES = 128
SUBLANES = 8
VMEM_LIMIT = 56 * 1024 * 1024
NEG_BIG = -1e30

EV_QKV_A = 0
EV_Z = 2304
EV_QKV_B = 3072
EV_BA = 5376
EV_COLS = 5632
OD_QC, OD_KC, OD_VC = 0, 512, 1024
OD_QD, OD_KD, OD_VD, OD_RD, OD_GD = 1536, 1792, 2048, 2560, 3072
OD_COLS = 3200


def _cparams(sem):
    return pltpu.CompilerParams(dimension_semantics=sem, vmem_limit_bytes=VMEM_LIMIT)


def _bdot(a, b):
    return jnp.dot(a.astype(BF16), b.astype(BF16), preferred_element_type=F32)


def _bdot_nt(a, b):
    return lax.dot_general(a.astype(BF16), b.astype(BF16), (((1,), (1,)), ((), ())),
                           preferred_element_type=F32)


def _bdot_tn(a, b):
    return lax.dot_general(a.astype(BF16), b.astype(BF16), (((0,), (0,)), ((), ())),
                           preferred_element_type=F32)


def _sigmoid(x):
    return 1.0 / (1.0 + jnp.exp(-x))


def _silu(x):
    return x * _sigmoid(x)


def _softplus(x):
    return jnp.maximum(x, 0.0) + jnp.log1p(jnp.exp(-jnp.abs(x)))


def _log_sigmoid(x):
    return -_softplus(-x)


def _mm_body(x_ref, w_ref, o_ref):
    o_ref[...] = jnp.dot(x_ref[...].astype(BF16), w_ref[...], preferred_element_type=F32)


def _matmul(x, w, tm, tn):
    m, k = x.shape
    n = w.shape[1]
    return pl.pallas_call(
        _mm_body,
        grid=(m // tm, n // tn),
        in_specs=[pl.BlockSpec((tm, k), lambda i, j: (i, 0)),
                  pl.BlockSpec((k, tn), lambda i, j: (0, j))],
        out_specs=pl.BlockSpec((tm, tn), lambda i, j: (i, j)),
        out_shape=jax.ShapeDtypeStruct((m, n), F32),
        compiler_params=_cparams(("parallel", "arbitrary")),
        name="in_proj",
    )(x, w)


def _rel_bucket_np(dist):
    max_exact = NUM_BUCKETS // 2
    d = np.maximum(dist, 1).astype(np.float32)
    large = max_exact + (np.log(d / max_exact) / math.log(REL_MAX_DIST / max_exact)
                         * (NUM_BUCKETS - max_exact)).astype(np.int32)
    large = np.minimum(large, NUM_BUCKETS - 1)
    return np.where(dist < max_exact, dist, large).astype(np.int32)


def _bias_body(rb_ref, bucket_ref, neg_ref, o_ref):
    h = pl.program_id(0)
    bucket = bucket_ref[0]
    acc = neg_ref[0]
    for b in range(NUM_BUCKETS):
        acc = acc + jnp.where(bucket == b, rb_ref[b, h], 0.0)
    o_ref[0, 0] = acc


def _bias_tiles(rel_bias, bucket, neg):
    nt, r, c = bucket.shape
    nh = rel_bias.shape[1]
    return pl.pallas_call(
        _bias_body,
        grid=(nh, nt),
        in_specs=[pl.BlockSpec(memory_space=pltpu.SMEM),
                  pl.BlockSpec((1, r, c), lambda h, t: (t, 0, 0)),
                  pl.BlockSpec((1, r, c), lambda h, t: (t, 0, 0))],
        out_specs=pl.BlockSpec((1, 1, r, c), lambda h, t: (h, t, 0, 0)),
        out_shape=jax.ShapeDtypeStruct((nh, nt, r, c), F32),
        compiler_params=_cparams(("parallel", "parallel")),
        name="rel_bias_tiles",
    )(rel_bias, jnp.asarray(bucket), jnp.asarray(neg))


def _causal_conv(halo, cur, w_ref, scratch):
    width = w_ref.shape[0]
    rows = cur.shape[0]
    scratch[0:SUBLANES, :] = halo
    scratch[SUBLANES:SUBLANES + rows, :] = cur
    y = w_ref[width - 1:width, :] * cur
    for j in range(width - 1):
        back = width - 1 - j
        y = y + w_ref[j:j + 1, :] * scratch[SUBLANES - back:SUBLANES - back + rows, :]
    return y


def _gdn_prep_body(tiles_per_seq, x_ref, halo_ref, w_ref, o_ref, scratch):
    i = pl.program_id(0)
    c = pl.program_id(1)
    halo = jnp.where(i % tiles_per_seq == 0, 0.0, halo_ref[...])
    y = _silu(_causal_conv(halo, x_ref[...], w_ref, scratch))
    inv = lax.rsqrt(jnp.sum(y * y, axis=-1, keepdims=True) + RMS_EPS)
    scale = jnp.where(c < GDN_HEADS, inv * GDN_D ** -0.5, jnp.where(c < 2 * GDN_HEADS, inv, 1.0))
    o_ref[...] = y * scale


def _gdn_prep(proj, conv_w, seq, tm):
    t = proj.shape[0]
    ncol = 3 * GDN_HEADS
    hb = tm // SUBLANES
    return pl.pallas_call(
        functools.partial(_gdn_prep_body, seq // tm),
        grid=(t // tm, ncol),
        in_specs=[pl.BlockSpec((tm, LANES), lambda i, c: (i, c)),
                  pl.BlockSpec((SUBLANES, LANES), lambda i, c: (jnp.maximum(i * hb - 1, 0), c)),
                  pl.BlockSpec((GDN_CONV, LANES), lambda i, c: (0, c))],
        out_specs=pl.BlockSpec((tm, LANES), lambda i, c: (i, c)),
        out_shape=jax.ShapeDtypeStruct((t, ncol * LANES), F32),
        scratch_shapes=[pltpu.VMEM((tm + SUBLANES, LANES), F32)],
        compiler_params=_cparams(("parallel", "parallel")),
        name="gdn_prep",
    )(proj, proj, conv_w)


def _gdn_body(nc, q_ref, k_ref, v_ref, ba_ref, z_ref, alog_ref, dtb_ref, nw_ref, o_ref, state):
    h = pl.program_id(1)
    n = pl.program_id(2)

    @pl.when(n == 0)
    def _():
        state[...] = jnp.zeros_like(state)

    rows = nc * CHUNK
    lane = lax.broadcasted_iota(jnp.int32, (rows, LANES), 1)
    ba = ba_ref[...]
    beta_col = jnp.sum(jnp.where(lane == h, _sigmoid(ba), 0.0), axis=-1, keepdims=True)
    g_all = -jnp.exp(alog_ref[...]) * _softplus(ba + dtb_ref[...])
    g_col = jnp.sum(jnp.where(lane == h + GDN_HEADS, g_all, 0.0), axis=-1, keepdims=True)

    ri = lax.broadcasted_iota(jnp.int32, (CHUNK, CHUNK), 0)
    ci = lax.broadcasted_iota(jnp.int32, (CHUNK, CHUNK), 1)
    incl = ri >= ci
    strict = ri > ci
    tri = incl.astype(F32)
    eye = (ri == ci).astype(F32)
    lane0 = (lax.broadcasted_iota(jnp.int32, (CHUNK, LANES), 1) == 0).astype(F32)

    pre = []
    for c in range(nc):
        sl = slice(c * CHUNK, (c + 1) * CHUNK)
        qc, kc, vc = q_ref[sl, :], k_ref[sl, :], v_ref[sl, :]
        bcol = beta_col[sl]
        g_b = jnp.broadcast_to(g_col[sl], (CHUNK, LANES))
        gam = jnp.dot(tri, g_b, precision=HI, preferred_element_type=F32)
        gam_t = lax.dot_general(lane0, gam, (((1,), (1,)), ((), ())), precision=HI,
                                preferred_element_type=F32)
        diff = gam[:, :CHUNK] - gam_t
        decay = jnp.where(incl, jnp.exp(jnp.where(incl, diff, 0.0)), 0.0)
        kk = _bdot_nt(kc, kc)
        x = -jnp.where(strict, bcol * kk * decay, 0.0)
        inv = eye + x
        for _ in range(5):
            x = _bdot(x, x)
            inv = inv + _bdot(inv, x)
        eg = jnp.exp(gam)
        u = _bdot(inv, vc * bcol)
        w = _bdot(inv, kc * (bcol * eg))
        qk = _bdot_nt(qc, kc) * decay
        gl = gam[CHUNK - 1:CHUNK, :]
        pre.append((u, w, qk, qc * eg, kc * jnp.exp(gl - gam), jnp.exp(gl)))

    s = state[...]
    outs = []
    for u, w, qk, q_dec, k_dec, g_last in pre:
        delta = u - _bdot(w, s)
        outs.append(_bdot(q_dec, s) + _bdot(qk, delta))
        s = g_last * s + _bdot_tn(k_dec, delta)
    state[...] = s

    o = jnp.concatenate(outs, axis=0)
    o = o * lax.rsqrt(jnp.mean(o * o, axis=-1, keepdims=True) + RMS_EPS) * nw_ref[...]
    o_ref[...] = o * _silu(z_ref[...])


def _gdn(qkv, proj, alog_pad, dtb_pad, norm_w, bsz, seq, nc):
    t = qkv.shape[0]
    rows = nc * CHUNK
    spb = seq // rows
    nh = GDN_HEADS

    def row(b, h, n):
        return b * spb + n

    return pl.pallas_call(
        functools.partial(_gdn_body, nc),
        grid=(bsz, nh, spb),
        in_specs=[pl.BlockSpec((rows, LANES), lambda b, h, n: (row(b, h, n), h)),
                  pl.BlockSpec((rows, LANES), lambda b, h, n: (row(b, h, n), nh + h)),
                  pl.BlockSpec((rows, LANES), lambda b, h, n: (row(b, h, n), 2 * nh + h)),
                  pl.BlockSpec((rows, LANES), lambda b, h, n: (row(b, h, n), EV_BA // LANES)),
                  pl.BlockSpec((rows, LANES), lambda b, h, n: (row(b, h, n), EV_Z // LANES + h)),
                  pl.BlockSpec((1, LANES), lambda b, h, n: (0, 0)),
                  pl.BlockSpec((1, LANES), lambda b, h, n: (0, 0)),
                  pl.BlockSpec((1, LANES), lambda b, h, n: (0, 0))],
        out_specs=pl.BlockSpec((rows, LANES), lambda b, h, n: (row(b, h, n), h)),
        out_shape=jax.ShapeDtypeStruct((t, nh * GDN_D), F32),
        scratch_shapes=[pltpu.VMEM((GDN_D, GDN_D), F32)],
        compiler_params=_cparams(("parallel", "parallel", "arbitrary")),
        name="gdn",
    )(qkv, qkv, qkv, proj, proj, alog_pad, dtb_pad, norm_w)


def _swa_bias_tables(dilation):
    qi = np.arange(SWA_BLOCK)[:, None] + SWA_BLOCK
    kj = np.arange(2 * SWA_BLOCK)[None, :]
    rel = qi - kj
    bucket = _rel_bucket_np(np.maximum(rel, 0) * dilation)
    return bucket[None].astype(np.int32)


def _swa_body(span, q_ref, kp_ref, kc_ref, vp_ref, vc_ref, bias_ref, o_ref, lse_ref):
    j = pl.program_id(2)
    width = SWA_HEADS * SWA_DH
    q = q_ref[0] * SWA_DH ** -0.5
    k = jnp.concatenate([kp_ref[0], kc_ref[0]], axis=0).astype(BF16)
    v = jnp.concatenate([vp_ref[0], vc_ref[0]], axis=0).astype(BF16)
    qi = lax.broadcasted_iota(jnp.int32, (SWA_BLOCK, 2 * SWA_BLOCK), 0) + SWA_BLOCK
    kj = lax.broadcasted_iota(jnp.int32, (SWA_BLOCK, 2 * SWA_BLOCK), 1)
    rel = qi - kj
    valid = (rel >= 0) & (rel <= span) & ((kj >= SWA_BLOCK) | (j > 0))
    lane = lax.broadcasted_iota(jnp.int32, (SWA_BLOCK, width), 1)
    o = jnp.zeros((SWA_BLOCK, width), F32)
    lse = jnp.zeros((SWA_BLOCK, width), F32)
    for h in range(SWA_HEADS):
        in_head = (lane >= h * SWA_DH) & (lane < (h + 1) * SWA_DH)
        s = _bdot_nt(jnp.where(in_head, q, 0.0), k) + bias_ref[h, 0]
        s = jnp.where(valid, s, NEG_BIG)
        m = jnp.max(s, axis=-1, keepdims=True)
        p = jnp.exp(s - m)
        l = jnp.sum(p, axis=-1, keepdims=True)
        o = jnp.where(in_head, _bdot(p / l, v), o)
        lse = jnp.where(in_head, m + jnp.log(l), lse)
    o_ref[0] = o
    lse_ref[0] = lse


def _swa_group(proj, bias, group, bsz, seq):
    window, dilation = SWA_CONFIGS[group]
    span = window // dilation
    sub = seq // dilation
    nb = sub // SWA_BLOCK
    width = SWA_HEADS * SWA_DH
    cols = proj.shape[1]
    pv = proj.reshape(bsz, sub, dilation * cols)
    cpb = cols // width
    qo = EV_QKV_B // width + group
    ko = qo + len(SWA_CONFIGS)
    vo = ko + len(SWA_CONFIGS)
    blk = (1, SWA_BLOCK, width)

    def cur(off):
        return lambda b, r, j: (b, j, r * cpb + off)

    def prev(off):
        return lambda b, r, j: (b, jnp.maximum(j - 1, 0), r * cpb + off)

    o, lse = pl.pallas_call(
        functools.partial(_swa_body, span),
        grid=(bsz, dilation, nb),
        in_specs=[pl.BlockSpec(blk, cur(qo)),
                  pl.BlockSpec(blk, prev(ko)), pl.BlockSpec(blk, cur(ko)),
                  pl.BlockSpec(blk, prev(vo)), pl.BlockSpec(blk, cur(vo)),
                  pl.BlockSpec((SWA_HEADS, 1, SWA_BLOCK, 2 * SWA_BLOCK), lambda b, r, j: (0, 0, 0, 0))],
        out_specs=[pl.BlockSpec(blk, lambda b, r, j: (b, j, r)),
                   pl.BlockSpec(blk, lambda b, r, j: (b, j, r))],
        out_shape=[jax.ShapeDtypeStruct((bsz, sub, dilation * width), F32)] * 2,
        compiler_params=_cparams(("parallel", "parallel", "parallel")),
        name=f"swa_d{dilation}",
    )(pv, pv, pv, pv, pv, bias)
    return o.reshape(bsz * seq, width), lse.reshape(bsz * seq, width)


def _layer_norm(y, g, b):
    mu = jnp.mean(y, axis=-1, keepdims=True)
    yc = y - mu
    var = jnp.mean(yc * yc, axis=-1, keepdims=True)
    return yc * lax.rsqrt(var + LN_EPS) * g + b


def _even_out_body(oa_ref, o0_ref, o1_ref, o2_ref, l0_ref, l1_ref, l2_ref, wa_ref, wb_ref,
                   h_ref, g_ref, b_ref, y_ref):
    l0, l1, l2 = l0_ref[...], l1_ref[...], l2_ref[...]
    m = jnp.maximum(jnp.maximum(l0, l1), l2)
    e0, e1, e2 = jnp.exp(l0 - m), jnp.exp(l1 - m), jnp.exp(l2 - m)
    den = e0 + e1 + e2
    ob = (e0 / den) * o0_ref[...] + (e1 / den) * o1_ref[...] + (e2 / den) * o2_ref[...]
    mix = _bdot(oa_ref[...], wa_ref[...]) + _bdot(ob, wb_ref[...])
    y_ref[...] = _layer_norm(DEEPNORM_ALPHA * h_ref[...] + mix, g_ref[...], b_ref[...])


def _odd_out_body(oc_ref, od_ref, wc_ref, wd_ref, h_ref, g_ref, b_ref, y_ref):
    mix = _bdot(oc_ref[...], wc_ref[...]) + _bdot(od_ref[...], wd_ref[...])
    y_ref[...] = _layer_norm(DEEPNORM_ALPHA * h_ref[...] + mix, g_ref[...], b_ref[...])


def _out_proj(body, acts, weights, h, g, b, tm, name):
    t = h.shape[0]

    def rowblk(a):
        return pl.BlockSpec((tm, a.shape[1]), lambda i: (i, 0))

    def full(a):
        return pl.BlockSpec(a.shape, lambda i: (0, 0))

    return pl.pallas_call(
        body,
        grid=(t // tm,),
        in_specs=[rowblk(a) for a in acts] + [full(w) for w in weights] + [rowblk(h), full(g), full(b)],
        out_specs=pl.BlockSpec((tm, D_MODEL), lambda i: (i, 0)),
        out_shape=jax.ShapeDtypeStruct((t, D_MODEL), F32),
        compiler_params=_cparams(("parallel",)),
        name=name,
    )(*acts, *weights, h, g, b)


def _diff_bias_tables(blk):
    buckets = _rel_bucket_np(np.arange(2 * REL_MAX_DIST))
    far = int(np.max(np.nonzero(buckets != NUM_BUCKETS - 1)[0])) + 1
    nt = -(-(far + blk - 1) // blk) + 1
    i = np.arange(blk)[:, None]
    j = np.arange(blk)[None, :]
    dist = np.stack([t * blk + i - j for t in range(nt)])
    bucket = _rel_bucket_np(np.maximum(dist, 0))
    neg = np.where(dist >= 0, 0.0, NEG_BIG).astype(np.float32)
    return bucket, neg


def _diff_body(blk, nt, lam_init, q_ref, k_ref, v_ref, bias_ref, lam_ref, nw_ref, o_ref, kb, vb):
    qi = pl.program_id(2)

    @pl.when(qi == 0)
    def _():
        kb[...] = k_ref[...].astype(BF16)
        vb[...] = v_ref[...].astype(BF16)

    lane = lax.broadcasted_iota(jnp.int32, (blk, 2 * DIFF_DH), 1)
    q = q_ref[...] * DIFF_DH ** -0.5
    q1 = jnp.where(lane < DIFF_DH, q, 0.0).astype(BF16)
    q2 = jnp.where(lane >= DIFF_DH, q, 0.0).astype(BF16)

    def step(kj, carry):
        m1, l1, a1, m2, l2, a2 = carry
        start = pl.multiple_of(kj * blk, blk)
        k = kb[pl.ds(start, blk), :]
        v = vb[pl.ds(start, blk), :]
        bias = bias_ref[0, jnp.minimum(qi - kj, nt - 1)]

        def one(qc, m, l, a):
            s = lax.dot_general(qc, k, (((1,), (1,)), ((), ())), preferred_element_type=F32) + bias
            m_new = jnp.maximum(m, jnp.max(s, axis=-1, keepdims=True))
            alpha = jnp.exp(m - m_new)
            p = jnp.exp(s - m_new)
            l = alpha * l + jnp.sum(p, axis=-1, keepdims=True)
            a = alpha * a + jnp.dot(p.astype(BF16), v, preferred_element_type=F32)
            return m_new, l, a

        m1, l1, a1 = one(q1, m1, l1, a1)
        m2, l2, a2 = one(q2, m2, l2, a2)
        return m1, l1, a1, m2, l2, a2

    col = jnp.full((blk, 1), NEG_BIG, F32)
    zc = jnp.zeros((blk, 1), F32)
    za = jnp.zeros((blk, 2 * DIFF_DH), F32)
    m1, l1, a1, m2, l2, a2 = lax.fori_loop(0, qi + 1, step, (col, zc, za, col, zc, za))

    lp = lam_ref[...]
    lam = (jnp.exp(jnp.sum(lp[0:1] * lp[1:2], axis=-1, keepdims=True))
           - jnp.exp(jnp.sum(lp[2:3] * lp[3:4], axis=-1, keepdims=True)) + lam_init)
    o = a1 / l1 - lam * (a2 / l2)
    o = o * lax.rsqrt(jnp.mean(o * o, axis=-1, keepdims=True) + RMS_EPS) * nw_ref[...]
    o_ref[...] = o * (1.0 - lam_init)


def _diff_attention(proj, bias, lam_params, norm_w, lam_init, bsz, seq, blk):
    t = proj.shape[0]
    nq = seq // blk
    nt = bias.shape[1]
    width = 2 * DIFF_DH
    return pl.pallas_call(
        functools.partial(_diff_body, blk, nt, lam_init),
        grid=(bsz, DIFF_HEADS, nq),
        in_specs=[pl.BlockSpec((blk, width), lambda b, h, i: (b * nq + i, OD_QC // width + h)),
                  pl.BlockSpec((seq, width), lambda b, h, i: (b, OD_KC // width + h)),
                  pl.BlockSpec((seq, width), lambda b, h, i: (b, OD_VC // width + h)),
                  pl.BlockSpec((1, nt, blk, blk), lambda b, h, i: (h, 0, 0, 0)),
                  pl.BlockSpec((4, DIFF_DH), lambda b, h, i: (0, 0)),
                  pl.BlockSpec((1, width), lambda b, h, i: (0, 0))],
        out_specs=pl.BlockSpec((blk, width), lambda b, h, i: (b * nq + i, h)),
        out_shape=jax.ShapeDtypeStruct((t, DIFF_HEADS * width), F32),
        scratch_shapes=[pltpu.VMEM((seq, width), BF16), pltpu.VMEM((seq, width), BF16)],
        compiler_params=_cparams(("parallel", "parallel", "arbitrary")),
        name="diff_attn",
    )(proj, proj, proj, bias, lam_params, norm_w)


def _gla_body(nc, q_ref, k_ref, gd_ref, va_ref, vb_ref, ra_ref, rb_ref, wg_ref, bg_ref, nw_ref,
              o_ref, state):
    n = pl.program_id(2)

    @pl.when(n == 0)
    def _():
        state[...] = jnp.zeros_like(state)

    rows = nc * CHUNK
    nsub = rows // GLA_SUB
    per_chunk = CHUNK // GLA_SUB
    gate = jnp.dot(gd_ref[...], wg_ref[...], precision=HI, preferred_element_type=F32) + bg_ref[...]
    log_a = _log_sigmoid(gate) * (1.0 / GLA_TAU)
    q = q_ref[...] * GLA_DK ** -0.5
    k = k_ref[...]
    vs = (va_ref[...], vb_ref[...])
    lane_s = lax.broadcasted_iota(jnp.int32, (GLA_SUB, LANES), 1)
    lane_c = lax.broadcasted_iota(jnp.int32, (CHUNK, LANES), 1)
    sub_mask = (lane_s < GLA_DK, lane_s >= GLA_DK)
    chunk_mask = (lane_c < GLA_DK, lane_c >= GLA_DK)

    ri = lax.broadcasted_iota(jnp.int32, (CHUNK, CHUNK), 0)
    ci = lax.broadcasted_iota(jnp.int32, (CHUNK, CHUNK), 1)
    tri = (ri >= ci).astype(F32)
    b = jnp.concatenate(
        [jnp.dot(tri, log_a[c * CHUNK:(c + 1) * CHUNK], precision=HI, preferred_element_type=F32)
         for c in range(nc)], axis=0)

    b3 = b.reshape(nsub, GLA_SUB, LANES)
    q3 = q.reshape(nsub, GLA_SUB, LANES)
    k3 = k.reshape(nsub, GLA_SUB, LANES)
    v3 = [v.reshape(nsub, GLA_SUB, LANES) for v in vs]
    row3 = lax.broadcasted_iota(jnp.int32, (nsub, GLA_SUB, LANES), 1)
    lane3 = lax.broadcasted_iota(jnp.int32, (nsub, GLA_SUB, LANES), 2)
    rowc = lax.broadcasted_iota(jnp.int32, (nsub, GLA_SUB, 1), 1)
    o3 = [jnp.zeros((nsub, GLA_SUB, LANES), F32) for _ in vs]
    for jj in range(GLA_SUB):
        e = jnp.exp(jnp.where(row3 >= jj, b3 - b3[:, jj:jj + 1, :], 0.0))
        t = q3 * k3[:, jj:jj + 1, :] * e
        w_all = jnp.sum(t, axis=-1, keepdims=True)
        w_a = jnp.sum(jnp.where(lane3 < GLA_DK, t, 0.0), axis=-1, keepdims=True)
        for hd, w in enumerate((w_a, w_all - w_a)):
            o3[hd] = o3[hd] + jnp.where(rowc >= jj, w, 0.0) * v3[hd][:, jj:jj + 1, :]
    o_diag = [o.reshape(rows, LANES) for o in o3]

    kcol = lax.broadcasted_iota(jnp.int32, (GLA_SUB, CHUNK), 1)
    outs = ([], [])
    st = [state[0], state[1]]
    for c in range(nc):
        sl = slice(c * CHUNK, (c + 1) * CHUNK)
        bc, qc, kc = b[sl], q[sl], k[sl]
        a_rows = [[jnp.zeros((GLA_SUB, CHUNK), F32)] for _ in vs]
        for blk in range(1, per_chunk):
            r0 = blk * GLA_SUB
            bref = bc[r0:r0 + 1]
            qs = qc[r0:r0 + GLA_SUB] * jnp.exp(bc[r0:r0 + GLA_SUB] - bref)
            ks = kc * jnp.exp(jnp.minimum(bref - bc, 0.0))
            for hd in range(2):
                a = _bdot_nt(jnp.where(sub_mask[hd], qs, 0.0), ks)
                a_rows[hd].append(jnp.where(kcol < r0, a, 0.0))
        eb = jnp.exp(bc)
        b_last = bc[CHUNK - 1:CHUNK]
        k_dec = kc * jnp.exp(b_last - bc)
        e_last = jnp.exp(b_last)
        for hd in range(2):
            vc = vs[hd][sl]
            a_mat = jnp.concatenate(a_rows[hd], axis=0)
            q_dec = jnp.where(chunk_mask[hd], qc * eb, 0.0)
            outs[hd].append(_bdot(a_mat, vc) + _bdot_nt(q_dec, st[hd]))
            st[hd] = st[hd] * e_last + _bdot_tn(vc, k_dec)
    state[0] = st[0]
    state[1] = st[1]

    res = []
    for hd, r_ref in enumerate((ra_ref, rb_ref)):
        o = jnp.concatenate(outs[hd], axis=0) + o_diag[hd]
        o = o * lax.rsqrt(jnp.mean(o * o, axis=-1, keepdims=True) + RMS_EPS) * nw_ref[...]
        res.append(o * _silu(r_ref[...]))
    o_ref[...] = jnp.concatenate(res, axis=-1)


def _gla(proj, w_gate_pad, b_gate, norm_w, bsz, seq, nc):
    t = proj.shape[0]
    rows = nc * CHUNK
    spb = seq // rows
    npair = GLA_HEADS // 2

    def col(base, mul=0, add=0):
        return lambda b, p, n: (b * spb + n, base // LANES + mul * p + add)

    blk = (rows, LANES)
    return pl.pallas_call(
        functools.partial(_gla_body, nc),
        grid=(bsz, npair, spb),
        in_specs=[pl.BlockSpec(blk, col(OD_QD, 1)),
                  pl.BlockSpec(blk, col(OD_KD, 1)),
                  pl.BlockSpec(blk, col(OD_GD)),
                  pl.BlockSpec(blk, col(OD_VD, 2, 0)), pl.BlockSpec(blk, col(OD_VD, 2, 1)),
                  pl.BlockSpec(blk, col(OD_RD, 2, 0)), pl.BlockSpec(blk, col(OD_RD, 2, 1)),
                  pl.BlockSpec((LANES, LANES), lambda b, p, n: (0, p)),
                  pl.BlockSpec((1, LANES), lambda b, p, n: (0, p)),
                  pl.BlockSpec((1, GLA_DV), lambda b, p, n: (0, 0))],
        out_specs=pl.BlockSpec((rows, 2 * GLA_DV), lambda b, p, n: (b * spb + n, p)),
        out_shape=jax.ShapeDtypeStruct((t, GLA_HEADS * GLA_DV), F32),
        scratch_shapes=[pltpu.VMEM((2, GLA_DV, LANES), F32)],
        compiler_params=_cparams(("parallel", "parallel", "arbitrary")),
        name="gla",
    )(proj, proj, proj, proj, proj, proj, proj, w_gate_pad, b_gate, norm_w)


def _ffn_body(tiles_per_seq, x_ref, halo_ref, wg_ref, wv_ref, cwg_ref, cwv_ref, cbg_ref, cbv_ref,
              wd_ref, g_ref, b_ref, y_ref, xb, hb, acc, scratch):
    i = pl.program_id(0)
    j = pl.program_id(1)

    @pl.when(j == 0)
    def _():
        xb[...] = x_ref[...].astype(BF16)
        hb[...] = jnp.where(i % tiles_per_seq == 0, 0.0, halo_ref[...]).astype(BF16)
        acc[...] = jnp.zeros_like(acc)

    def branch(w_ref, cw_ref, cb_ref):
        cur = jnp.dot(xb[...], w_ref[...], preferred_element_type=F32)
        halo = jnp.dot(hb[...], w_ref[...], preferred_element_type=F32)
        return _causal_conv(halo, cur, cw_ref, scratch) + cb_ref[...]

    gate = branch(wg_ref, cwg_ref, cbg_ref)
    val = branch(wv_ref, cwv_ref, cbv_ref)
    acc[...] += _bdot(_silu(gate) * val, wd_ref[...])

    @pl.when(j == pl.num_programs(1) - 1)
    def _():
        y_ref[...] = _layer_norm(DEEPNORM_ALPHA * x_ref[...] + acc[...], g_ref[...], b_ref[...])


def _ffn(x, w_up, conv_w, conv_b, w_down, g, b, seq, tm, tc):
    t = x.shape[0]
    nj = D_FF // tc
    hb = tm // SUBLANES
    return pl.pallas_call(
        functools.partial(_ffn_body, seq // tm),
        grid=(t // tm, nj),
        in_specs=[pl.BlockSpec((tm, D_MODEL), lambda i, j: (i, 0)),
                  pl.BlockSpec((SUBLANES, D_MODEL), lambda i, j: (jnp.maximum(i * hb - 1, 0), 0)),
                  pl.BlockSpec((D_MODEL, tc), lambda i, j: (0, j)),
                  pl.BlockSpec((D_MODEL, tc), lambda i, j: (0, nj + j)),
                  pl.BlockSpec((FFN_CONV, tc), lambda i, j: (0, j)),
                  pl.BlockSpec((FFN_CONV, tc), lambda i, j: (0, nj + j)),
                  pl.BlockSpec((1, tc), lambda i, j: (0, j)),
                  pl.BlockSpec((1, tc), lambda i, j: (0, nj + j)),
                  pl.BlockSpec((tc, D_MODEL), lambda i, j: (j, 0)),
                  pl.BlockSpec((1, D_MODEL), lambda i, j: (0, 0)),
                  pl.BlockSpec((1, D_MODEL), lambda i, j: (0, 0))],
        out_specs=pl.BlockSpec((tm, D_MODEL), lambda i, j: (i, 0)),
        out_shape=jax.ShapeDtypeStruct((t, D_MODEL), F32),
        scratch_shapes=[pltpu.VMEM((tm, D_MODEL), BF16), pltpu.VMEM((SUBLANES, D_MODEL), BF16),
                        pltpu.VMEM((tm, D_MODEL), F32), pltpu.VMEM((tm + SUBLANES, tc), F32)],
        compiler_params=_cparams(("parallel", "arbitrary")),
        name="conv_ffn",
    )(x, x, w_up, w_up, conv_w, conv_w, conv_b, conv_b, w_down, g, b)


def _even_w_in(w):
    qk, v = GDN_HEADS * GDN_D, GDN_HEADS * GDN_D
    a_end = 2 * qk + 2 * v
    gates = w[:, a_end:a_end + 2 * GDN_HEADS]
    qkv_b = w[:, a_end + 2 * GDN_HEADS:]
    pad = jnp.zeros((w.shape[0], EV_COLS - EV_BA - 2 * GDN_HEADS), w.dtype)
    return jnp.concatenate([w[:, :a_end], qkv_b, gates, pad], axis=1).astype(BF16)


def _odd_w_in(w):
    pad = jnp.zeros((w.shape[0], OD_COLS - w.shape[1]), w.dtype)
    return jnp.concatenate([w, pad], axis=1).astype(BF16)


def _even_layer(h, rel_bias, w_in, conv_w, a_log, dt_bias, norm_w, w_out, ln_g, ln_b, bsz, seq):
    proj = _matmul(h, _even_w_in(w_in), 1024, 512)
    qkv = _gdn_prep(proj, conv_w, seq, 1024)
    gate_pad = jnp.zeros((1, LANES), F32)
    alog_pad = lax.dynamic_update_slice(gate_pad, a_log[None].astype(F32), (0, GDN_HEADS))
    dtb_pad = lax.dynamic_update_slice(gate_pad, dt_bias[None].astype(F32), (0, GDN_HEADS))
    o_a = _gdn(qkv, proj, alog_pad, dtb_pad, norm_w[None], bsz, seq, 4)
    os, lses = [], []
    for group, (_, dilation) in enumerate(SWA_CONFIGS):
        bucket = _swa_bias_tables(dilation)
        bias = _bias_tiles(rel_bias, bucket, np.zeros(bucket.shape, np.float32))
        o, lse = _swa_group(proj, bias, group, bsz, seq)
        os.append(o)
        lses.append(lse)
    gdn_v = GDN_HEADS * GDN_D
    w_out = w_out.astype(BF16)
    return _out_proj(_even_out_body, [o_a] + os + lses, [w_out[:gdn_v], w_out[gdn_v:]],
                     h, ln_g[None], ln_b[None], 512, "even_out")


def _odd_layer(h, rel_bias, w_in, lam_params, diff_norm_w, w_gate, b_gate, gla_norm_w, w_out,
               ln_g, ln_b, lam_init, bsz, seq):
    proj = _matmul(h, _odd_w_in(w_in), 1024, 640)
    blk = 256
    bucket, neg = _diff_bias_tables(blk)
    bias = _bias_tiles(rel_bias, bucket, neg)
    o_c = _diff_attention(proj, bias, lam_params, diff_norm_w[None], lam_init, bsz, seq, blk)
    w_gate_pad = jnp.concatenate(
        [w_gate, jnp.zeros((LANES - GLA_RANK, w_gate.shape[1]), w_gate.dtype)], axis=0)
    o_d = _gla(proj, w_gate_pad, b_gate[None], gla_norm_w[None], bsz, seq, 4)
    diff_v = DIFF_HEADS * 2 * DIFF_DH
    w_out = w_out.astype(BF16)
    return _out_proj(_odd_out_body, [o_c, o_d], [w_out[:diff_v], w_out[diff_v:]],
                     h, ln_g[None], ln_b[None], 512, "odd_out")


def kernel(x, rel_bias, w_in_even, gdn_conv_w, gdn_a_log, gdn_dt_bias, gdn_norm_w, w_out_even,
           w_in_odd, diff_lambda, diff_norm_w, gla_w_gate, gla_b_gate, gla_norm_w, w_out_odd,
           ffn_w_up, ffn_conv_w, ffn_conv_b, ffn_w_down, ln_g, ln_b):
    bsz, seq, d = x.shape
    h = x.reshape(bsz * seq, d)
    for layer in range(DEPTH):
        i = layer // 2
        if layer % 2 == 0:
            h = _even_layer(h, rel_bias, w_in_even[i], gdn_conv_w[i], gdn_a_log[i], gdn_dt_bias[i],
                            gdn_norm_w[i], w_out_even[i], ln_g[layer, 0], ln_b[layer, 0], bsz, seq)
        else:
            lam_init = 0.8 - 0.6 * math.exp(-0.3 * layer)
            h = _odd_layer(h, rel_bias, w_in_odd[i], diff_lambda[i], diff_norm_w[i], gla_w_gate[i],
                           gla_b_gate[i], gla_norm_w[i], w_out_odd[i], ln_g[layer, 0], ln_b[layer, 0],
                           lam_init, bsz, seq)
        h = _ffn(h, ffn_w_up[layer].astype(BF16), ffn_conv_w[layer], ffn_conv_b[layer][None],
                 ffn_w_down[layer].astype(BF16), ln_g[layer, 1][None], ln_b[layer, 1][None],
                 seq, 1024, 256)
    return h.reshape(bsz, seq, d)
```

```python
import functools
import math

import numpy as np
import jax
import jax.numpy as jnp
from jax import lax
from jax.experimental import pallas as pl
from jax.experimental.pallas import tpu as pltpu

F32 = jnp.float32
BF16 = jnp.bfloat16
HI = lax.Precision.HIGHEST

D_MODEL = 1024
DEPTH = 2
DEEPNORM_ALPHA = (2 * DEPTH) ** 0.25
LN_EPS = 1e-5
RMS_EPS = 1e-6
NUM_BUCKETS = 32
REL_MAX_DIST = 2048
GDN_HEADS = 6
GDN_D = 128
GDN_CONV = 4
CHUNK = 64
GDN_GROUP = 256
SWA_CONFIGS = ((128, 1), (512, 4), (2048, 16))
SWA_HEADS = 4
SWA_DH = 64
SWA_BLOCK = 128
SWA_SPAN = SWA_BLOCK * max(d for _, d in SWA_CONFIGS)
DIFF_HEADS = 4
DIFF_DH = 64
DIFF_BLOCK = 512
LOG2E = math.log2(math.e)
GLA_HEADS = 4
GLA_DK = 64
GLA_DV = 128
GLA_RANK = 16
GLA_TAU = 16.0
GLA_SUB = 16
D_FF = 2816
FFN_CONV = 3

LANES = 128
SUBLANES = 8
VMEM_LIMIT = 56 * 1024 * 1024
NEG_BIG = -1e30

EV_QKV_A = 0
EV_Z = 2304
EV_QKV_B = 3072
EV_BA = 5376
EV_COLS = 5632
OD_QC, OD_KC, OD_VC = 0, 512, 1024
OD_QD, OD_KD, OD_VD, OD_RD, OD_GD = 1536, 1792, 2048, 2560, 3072
OD_COLS = 3200


def _cparams(sem):
    return pltpu.CompilerParams(dimension_semantics=sem, vmem_limit_bytes=VMEM_LIMIT)


def _bdot(a, b):
    return jnp.dot(a.astype(BF16), b.astype(BF16), preferred_element_type=F32)


def _bdot_nt(a, b):
    return lax.dot_general(a.astype(BF16), b.astype(BF16), (((1,), (1,)), ((), ())),
                           preferred_element_type=F32)


def _bdot_tn(a, b):
    return lax.dot_general(a.astype(BF16), b.astype(BF16), (((0,), (0,)), ((), ())),
                           preferred_element_type=F32)


def _sigmoid(x):
    return 1.0 / (1.0 + jnp.exp(-x))


def _silu(x):
    return x * _sigmoid(x)


def _softplus(x):
    return jnp.maximum(x, 0.0) + jnp.log1p(jnp.exp(-jnp.abs(x)))


def _log_sigmoid(x):
    return -_softplus(-x)


def _mm_body(x_ref, w_ref, o_ref):
    o_ref[...] = jnp.dot(x_ref[...].astype(BF16), w_ref[...], preferred_element_type=F32)


def _matmul(x, w, tm, tn):
    m, k = x.shape
    n = w.shape[1]
    return pl.pallas_call(
        _mm_body,
        grid=(m // tm, n // tn),
        in_specs=[pl.BlockSpec((tm, k), lambda i, j: (i, 0)),
                  pl.BlockSpec((k, tn), lambda i, j: (0, j))],
        out_specs=pl.BlockSpec((tm, tn), lambda i, j: (i, j)),
        out_shape=jax.ShapeDtypeStruct((m, n), F32),
        compiler_params=_cparams(("parallel", "arbitrary")),
        name="in_proj",
    )(x, w)


def _rel_bucket_np(dist):
    max_exact = NUM_BUCKETS // 2
    d = np.maximum(dist, 1).astype(np.float32)
    large = max_exact + (np.log(d / max_exact) / math.log(REL_MAX_DIST / max_exact)
                         * (NUM_BUCKETS - max_exact)).astype(np.int32)
    large = np.minimum(large, NUM_BUCKETS - 1)
    return np.where(dist < max_exact, dist, large).astype(np.int32)


def _bias_body(scale, base_bucket, rb_ref, bucket_ref, neg_ref, o_ref):
    h = pl.program_id(0)
    bucket = bucket_ref[0]
    base = 0.0 if base_bucket is None else rb_ref[base_bucket, h]
    acc = neg_ref[0]
    for b in range(NUM_BUCKETS):
        acc = acc + jnp.where(bucket == b, (rb_ref[b, h] - base) * scale, 0.0)
    o_ref[0, 0] = acc


def _bias_tiles(rel_bias, bucket, neg, scale=1.0, base_bucket=None):
    nt, r, c = bucket.shape
    nh = rel_bias.shape[1]
    return pl.pallas_call(
        functools.partial(_bias_body, scale, base_bucket),
        grid=(nh, nt),
        in_specs=[pl.BlockSpec(memory_space=pltpu.SMEM),
                  pl.BlockSpec((1, r, c), lambda h, t: (t, 0, 0)),
                  pl.BlockSpec((1, r, c), lambda h, t: (t, 0, 0))],
        out_specs=pl.BlockSpec((1, 1, r, c), lambda h, t: (h, t, 0, 0)),
        out_shape=jax.ShapeDtypeStruct((nh, nt, r, c), F32),
        compiler_params=_cparams(("parallel", "parallel")),
        name="rel_bias_tiles",
    )(rel_bias, jnp.asarray(bucket), jnp.asarray(neg))


def _causal_conv(halo, cur, w_ref, scratch):
    width = w_ref.shape[0]
    rows = cur.shape[0]
    scratch[0:SUBLANES, :] = halo
    scratch[SUBLANES:SUBLANES + rows, :] = cur
    y = w_ref[width - 1:width, :] * cur
    for j in range(width - 1):
        back = width - 1 - j
        y = y + w_ref[j:j + 1, :] * scratch[SUBLANES - back:SUBLANES - back + rows, :]
    return y


def _gdn_prep_body(tiles_per_seq, x_ref, halo_ref, w_ref, o_ref, scratch):
    i = pl.program_id(0)
    c = pl.program_id(1)
    halo = jnp.where(i % tiles_per_seq == 0, 0.0, halo_ref[...])
    y = _silu(_causal_conv(halo, x_ref[...], w_ref, scratch))
    inv = lax.rsqrt(jnp.sum(y * y, axis=-1, keepdims=True) + RMS_EPS)
    scale = jnp.where(c < GDN_HEADS, inv * GDN_D ** -0.5, jnp.where(c < 2 * GDN_HEADS, inv, 1.0))
    o_ref[...] = y * scale


def _gdn_prep(proj, conv_w, seq, tm):
    t = proj.shape[0]
    ncol = 3 * GDN_HEADS
    hb = tm // SUBLANES
    return pl.pallas_call(
        functools.partial(_gdn_prep_body, seq // tm),
        grid=(t // tm, ncol),
        in_specs=[pl.BlockSpec((tm, LANES), lambda i, c: (i, c)),
                  pl.BlockSpec((SUBLANES, LANES), lambda i, c: (jnp.maximum(i * hb - 1, 0), c)),
                  pl.BlockSpec((GDN_CONV, LANES), lambda i, c: (0, c))],
        out_specs=pl.BlockSpec((tm, LANES), lambda i, c: (i, c)),
        out_shape=jax.ShapeDtypeStruct((t, ncol * LANES), F32),
        scratch_shapes=[pltpu.VMEM((tm + SUBLANES, LANES), F32)],
        compiler_params=_cparams(("parallel", "parallel")),
        name="gdn_prep",
    )(proj, proj, conv_w)


def _gdn_body(q_ref, k_ref, v_ref, ba_ref, z_ref, alog_ref, dtb_ref, nw_ref, o_ref, state):
    n = pl.program_id(1)

    @pl.when(n == 0)
    def _():
        state[...] = jnp.zeros_like(state)

    grp = GDN_GROUP
    nc = grp // CHUNK
    nh = GDN_HEADS
    ba = ba_ref[...]
    beta_all = _sigmoid(ba)
    g_all = -jnp.exp(alog_ref[...]) * _softplus(ba + dtb_ref[...])

    ri = lax.broadcasted_iota(jnp.int32, (grp, grp), 0)
    ci = lax.broadcasted_iota(jnp.int32, (grp, grp), 1)
    same = (ri // CHUNK) == (ci // CHUNK)
    incl = same & (ri >= ci)
    strict = same & (ri > ci)
    eye = (ri == ci).astype(F32)
    tri = incl.astype(BF16)
    g_hi = g_all.astype(BF16)
    rem = g_all - g_hi.astype(F32)
    g_mid = rem.astype(BF16)
    g_lo = (rem - g_mid.astype(F32)).astype(BF16)
    gam_all = (jnp.dot(tri, g_hi, preferred_element_type=F32) + jnp.dot(tri, g_mid, preferred_element_type=F32)
               + jnp.dot(tri, g_lo, preferred_element_type=F32))
    gam_rows = gam_all.T

    hs = range(nh)
    q = [q_ref[:, h * GDN_D:(h + 1) * GDN_D] for h in hs]
    k = [k_ref[:, h * GDN_D:(h + 1) * GDN_D] for h in hs]
    v = [v_ref[:, h * GDN_D:(h + 1) * GDN_D] for h in hs]
    gam = [jnp.broadcast_to(gam_all[:, nh + h:nh + h + 1], (grp, GDN_D)) for h in hs]
    bcol = [beta_all[:, h:h + 1] for h in hs]
    decay, x, inv = [], [], []
    for h in hs:
        diff = gam[h][:, 0:1] - gam_rows[nh + h:nh + h + 1, :]
        decay.append(jnp.where(incl, jnp.exp(jnp.where(incl, diff, 0.0)), 0.0))
        kk = _bdot_nt(k[h], k[h])
        x.append(-jnp.where(strict, bcol[h] * kk * decay[h], 0.0))
        inv.append(eye + x[h])
    for _ in range(5):
        for h in hs:
            x[h] = _bdot(x[h], x[h])
            inv[h] = inv[h] + _bdot(inv[h], x[h])
    u, w, qk, q_dec, k_dec, g_last = [], [], [], [], [], []
    for h in hs:
        eg = jnp.exp(gam[h])
        uw = _bdot(inv[h], jnp.concatenate([v[h] * bcol[h], k[h] * (bcol[h] * eg)], axis=1))
        u.append(uw[:, :GDN_D])
        w.append(uw[:, GDN_D:])
        qk.append(_bdot_nt(q[h], k[h]) * decay[h])
        q_dec.append(q[h] * eg)
        kd, gl = [], []
        for c in range(nc):
            last = gam[h][(c + 1) * CHUNK - 1:(c + 1) * CHUNK, :]
            kd.append(k[h][c * CHUNK:(c + 1) * CHUNK] * jnp.exp(last - gam[h][c * CHUNK:(c + 1) * CHUNK]))
            gl.append(jnp.exp(last))
        k_dec.append(kd)
        g_last.append(gl)

    s = [state[h] for h in hs]
    q_s = [[] for _ in hs]
    delta = [[] for _ in hs]
    for c in range(nc):
        sl = slice(c * CHUNK, (c + 1) * CHUNK)
        for h in hs:
            r = _bdot(jnp.concatenate([w[h][sl], q_dec[h][sl]], axis=0), s[h])
            d = u[h][sl] - r[:CHUNK]
            q_s[h].append(r[CHUNK:])
            delta[h].append(d)
            s[h] = g_last[h][c] * s[h] + _bdot_tn(k_dec[h][c], d)
    for h in hs:
        state[h] = s[h]
        o = jnp.concatenate(q_s[h], axis=0) + _bdot(qk[h], jnp.concatenate(delta[h], axis=0))
        o = o * lax.rsqrt(jnp.mean(o * o, axis=-1, keepdims=True) + RMS_EPS) * nw_ref[...]
        o_ref[:, h * GDN_D:(h + 1) * GDN_D] = o * _silu(z_ref[:, h * GDN_D:(h + 1) * GDN_D])


def _gdn(qkv, proj, alog_pad, dtb_pad, norm_w, bsz, seq):
    t = qkv.shape[0]
    rows = GDN_GROUP
    spb = seq // rows
    width = GDN_HEADS * GDN_D

    def at(col):
        return lambda b, n: (b * spb + n, col)

    return pl.pallas_call(
        _gdn_body,
        grid=(bsz, spb),
        in_specs=[pl.BlockSpec((rows, width), at(0)),
                  pl.BlockSpec((rows, width), at(1)),
                  pl.BlockSpec((rows, width), at(2)),
                  pl.BlockSpec((rows, LANES), at(EV_BA // LANES)),
                  pl.BlockSpec((rows, width), at(EV_Z // width)),
                  pl.BlockSpec((1, LANES), lambda b, n: (0, 0)),
                  pl.BlockSpec((1, LANES), lambda b, n: (0, 0)),
                  pl.BlockSpec((1, LANES), lambda b, n: (0, 0))],
        out_specs=pl.BlockSpec((rows, width), at(0)),
        out_shape=jax.ShapeDtypeStruct((t, width), F32),
        scratch_shapes=[pltpu.VMEM((GDN_HEADS, GDN_D, GDN_D), F32)],
        compiler_params=_cparams(("parallel", "arbitrary")),
        name="gdn",
    )(qkv, qkv, qkv, proj, proj, alog_pad, dtb_pad, norm_w)


def _swa_bias_tables():
    qi = np.arange(SWA_BLOCK)[:, None] + SWA_BLOCK
    kj = np.arange(2 * SWA_BLOCK)[None, :]
    rel = qi - kj
    buckets, negs = [], []
    for window, dilation in SWA_CONFIGS:
        buckets.append(_rel_bucket_np(np.maximum(rel, 0) * dilation))
        negs.append(np.where((rel >= 0) & (rel <= window // dilation), 0.0, NEG_BIG))
    return np.stack(buckets).astype(np.int32), np.stack(negs).astype(np.float32)


def _swa_body(*refs):
    ng = len(SWA_CONFIGS)
    ins, bias_ref, o_ref = refs[:5 * ng], refs[5 * ng], refs[5 * ng + 1]
    scratch = refs[5 * ng + 2:]
    kbufs, vbufs, o_scr, lse_scr = scratch[:ng], scratch[ng:2 * ng], scratch[2 * ng], scratch[2 * ng + 1]
    j = pl.program_id(1)
    blk = SWA_BLOCK
    first_head = lax.broadcasted_iota(jnp.int32, (blk, LANES), 1) < SWA_DH
    in_prev = lax.broadcasted_iota(jnp.int32, (2 * blk, 2 * blk), 1) < blk

    for g, (_, d) in enumerate(SWA_CONFIGS):
        q_ref, kc_ref, kp_ref, vc_ref, vp_ref = ins[5 * g:5 * g + 5]
        kbuf, vbuf = kbufs[g], vbufs[g]
        halo = blk * d
        kbuf[0:halo, :] = kp_ref[...]
        kbuf[halo:, :] = kc_ref[...]
        vbuf[0:halo, :] = vp_ref[...]
        vbuf[halo:, :] = vc_ref[...]

        def unit(u, carry, g=g, d=d, halo=halo, q_ref=q_ref, kbuf=kbuf, vbuf=vbuf):
            base = (u // d) * halo + u % d
            q = q_ref[pl.ds(base, blk, stride=d), :] * SWA_DH ** -0.5
            k = kbuf[pl.ds(base, 2 * blk, stride=d), :]
            v = vbuf[pl.ds(base, 2 * blk, stride=d), :]
            lhs = jnp.concatenate([jnp.where(first_head, q, 0.0), jnp.where(first_head, 0.0, q)], axis=0)
            s = _bdot_nt(lhs, k) + bias_ref[g, 0]
            no_prev = (j == 0) & (u < d)
            s = jnp.where(in_prev & no_prev, NEG_BIG, s)
            m = jnp.max(s, axis=-1, keepdims=True)
            p = jnp.exp(s - m)
            l = jnp.sum(p, axis=-1, keepdims=True)
            o2 = _bdot(p / l, v)
            lse2 = m + jnp.log(l)
            o_scr[g, pl.ds(base, blk, stride=d), :] = jnp.where(first_head, o2[:blk], o2[blk:])
            lse_scr[g, pl.ds(base, blk, stride=d), :] = jnp.where(first_head, lse2[:blk], lse2[blk:])
            return carry

        lax.fori_loop(0, SWA_SPAN // blk, unit, 0)

    def combine(c, carry):
        rows = pl.ds(pl.multiple_of(c * 2 * blk, 2 * blk), 2 * blk)
        lse = [lse_scr[g, rows, :] for g in range(ng)]
        m = functools.reduce(jnp.maximum, lse)
        e = [jnp.exp(x - m) for x in lse]
        den = functools.reduce(lambda x, y: x + y, e)
        o_ref[rows, :] = functools.reduce(lambda x, y: x + y,
                                          [(e[g] / den) * o_scr[g, rows, :] for g in range(ng)])
        return carry

    lax.fori_loop(0, SWA_SPAN // (2 * blk), combine, 0)


def _swa(proj, bias, bsz, seq):
    t = proj.shape[0]
    ng = len(SWA_CONFIGS)
    nspan = seq // SWA_SPAN
    npair = SWA_HEADS * SWA_DH // LANES
    group_cols = SWA_HEADS * SWA_DH // LANES
    in_specs, scratch_k = [], []
    for g, (_, d) in enumerate(SWA_CONFIGS):
        halo = SWA_BLOCK * d
        per_span = SWA_SPAN // halo

        def cur(which, g=g):
            col = EV_QKV_B // LANES + (which * ng + g) * group_cols
            return lambda b, j, p: (b * nspan + j, col + p)

        def prev(which, g=g, per_span=per_span):
            col = EV_QKV_B // LANES + (which * ng + g) * group_cols
            return lambda b, j, p: (jnp.maximum((b * nspan + j) * per_span - 1, 0), col + p)

        in_specs += [pl.BlockSpec((SWA_SPAN, LANES), cur(0)),
                     pl.BlockSpec((SWA_SPAN, LANES), cur(1)), pl.BlockSpec((halo, LANES), prev(1)),
                     pl.BlockSpec((SWA_SPAN, LANES), cur(2)), pl.BlockSpec((halo, LANES), prev(2))]
        scratch_k.append(pltpu.VMEM((halo + SWA_SPAN, LANES), F32))
    in_specs.append(pl.BlockSpec((ng, 1, 2 * SWA_BLOCK, 2 * SWA_BLOCK), lambda b, j, p: (0, p, 0, 0)))
    return pl.pallas_call(
        _swa_body,
        grid=(bsz, nspan, npair),
        in_specs=in_specs,
        out_specs=pl.BlockSpec((SWA_SPAN, LANES), lambda b, j, p: (b * nspan + j, p)),
        out_shape=jax.ShapeDtypeStruct((t, npair * LANES), F32),
        scratch_shapes=scratch_k + scratch_k + [pltpu.VMEM((ng, SWA_SPAN, LANES), F32)] * 2,
        compiler_params=_cparams(("parallel", "parallel", "parallel")),
        name="swa",
    )(*([proj] * (5 * ng)), bias)


def _layer_norm(y, g, b):
    mu = jnp.mean(y, axis=-1, keepdims=True)
    yc = y - mu
    var = jnp.mean(yc * yc, axis=-1, keepdims=True)
    return yc * lax.rsqrt(var + LN_EPS) * g + b


def _out_body(xa_ref, xb_ref, wa_ref, wb_ref, h_ref, g_ref, b_ref, y_ref):
    mix = _bdot(xa_ref[...], wa_ref[...]) + _bdot(xb_ref[...], wb_ref[...])
    y_ref[...] = _layer_norm(DEEPNORM_ALPHA * h_ref[...] + mix, g_ref[...], b_ref[...])


def _out_proj(body, acts, weights, h, g, b, tm, name):
    t = h.shape[0]

    def rowblk(a):
        return pl.BlockSpec((tm, a.shape[1]), lambda i: (i, 0))

    def full(a):
        return pl.BlockSpec(a.shape, lambda i: (0, 0))

    return pl.pallas_call(
        body,
        grid=(t // tm,),
        in_specs=[rowblk(a) for a in acts] + [full(w) for w in weights] + [rowblk(h), full(g), full(b)],
        out_specs=pl.BlockSpec((tm, D_MODEL), lambda i: (i, 0)),
        out_shape=jax.ShapeDtypeStruct((t, D_MODEL), F32),
        compiler_params=_cparams(("parallel",)),
        name=name,
    )(*acts, *weights, h, g, b)


def _diff_bias_tables(blk):
    buckets = _rel_bucket_np(np.arange(2 * REL_MAX_DIST))
    far = int(np.max(np.nonzero(buckets != NUM_BUCKETS - 1)[0])) + 1
    nb = -(-(far + blk - 1) // blk)
    kk = np.arange(blk)[:, None]
    qq = np.arange(blk)[None, :]
    dist = np.stack([t * blk + qq - kk for t in range(nb)])
    bucket = _rel_bucket_np(np.maximum(dist, 0))
    neg = np.where(dist >= 0, 0.0, NEG_BIG).astype(np.float32)
    return bucket, neg


def _diff_body(blk, nb, lam_init, q_ref, k_ref, v_ref, bias_ref, lam_ref, nw_ref, o_ref,
               kb, vt, acc1, acc2):
    qi = pl.program_id(2)
    dv = 2 * DIFF_DH
    seq = k_ref.shape[0]

    @pl.when(qi == 0)
    def _():
        kb[...] = k_ref[...].astype(BF16)
        vt[dv:, :] = jnp.ones((vt.shape[0] - dv, seq), BF16)

        def fill(c, carry):
            st = pl.multiple_of(c * blk, blk)
            vt[0:dv, pl.ds(st, blk)] = v_ref[pl.ds(st, blk), :].T.astype(BF16)
            return carry

        lax.fori_loop(0, seq // blk, fill, 0)

    lane = lax.broadcasted_iota(jnp.int32, (blk, dv), 1)
    q = q_ref[...] * (DIFF_DH ** -0.5 * LOG2E)
    qs = (jnp.where(lane < DIFF_DH, q, 0.0).astype(BF16), jnp.where(lane >= DIFF_DH, q, 0.0).astype(BF16))
    accs = (acc1, acc2)
    for acc in accs:
        acc[...] = jnp.zeros_like(acc)

    def step(biased, kj, ms):
        st = pl.multiple_of(kj * blk, blk)
        k = kb[pl.ds(st, blk), :]
        vtb = vt[:, pl.ds(st, blk)]
        out = []
        for qc, m, acc in zip(qs, ms, accs):
            s = lax.dot_general(k, qc, (((1,), (1,)), ((), ())), preferred_element_type=F32)
            if biased:
                s = s + bias_ref[0, qi - kj]
            m_new = jnp.maximum(m, jnp.max(s, axis=0, keepdims=True))
            p = jnp.exp2(s - m_new).astype(BF16)
            acc[...] = jnp.exp2(m - m_new) * acc[...] + jnp.dot(vtb, p, preferred_element_type=F32)
            out.append(m_new)
        return tuple(out)

    m0 = jnp.full((1, blk), NEG_BIG, F32)
    n_far = jnp.maximum(qi - (nb - 1), 0)
    ms = lax.fori_loop(0, n_far, functools.partial(step, False), (m0, m0))
    lax.fori_loop(n_far, qi + 1, functools.partial(step, True), ms)

    lp = lam_ref[...]
    lam = (jnp.exp(jnp.sum(lp[0:1] * lp[1:2], axis=-1, keepdims=True))
           - jnp.exp(jnp.sum(lp[2:3] * lp[3:4], axis=-1, keepdims=True)) + lam_init)
    a1, a2 = acc1[...], acc2[...]
    o_t = a1[:dv] / a1[dv:dv + 1] - lam * (a2[:dv] / a2[dv:dv + 1])
    o = o_t.T
    o = o * lax.rsqrt(jnp.mean(o * o, axis=-1, keepdims=True) + RMS_EPS) * nw_ref[...]
    o_ref[...] = o * (1.0 - lam_init)


def _diff_attention(proj, bias, lam_params, norm_w, lam_init, bsz, seq, blk):
    t = proj.shape[0]
    nq = seq // blk
    nb = bias.shape[1]
    width = 2 * DIFF_DH
    ones_rows = 2 * SUBLANES
    return pl.pallas_call(
        functools.partial(_diff_body, blk, nb, lam_init),
        grid=(bsz, DIFF_HEADS, nq),
        in_specs=[pl.BlockSpec((blk, width), lambda b, h, i: (b * nq + i, OD_QC // width + h)),
                  pl.BlockSpec((seq, width), lambda b, h, i: (b, OD_KC // width + h)),
                  pl.BlockSpec((seq, width), lambda b, h, i: (b, OD_VC // width + h)),
                  pl.BlockSpec((1, nb, blk, blk), lambda b, h, i: (h, 0, 0, 0)),
                  pl.BlockSpec((4, DIFF_DH), lambda b, h, i: (0, 0)),
                  pl.BlockSpec((1, width), lambda b, h, i: (0, 0))],
        out_specs=pl.BlockSpec((blk, width), lambda b, h, i: (b * nq + i, h)),
        out_shape=jax.ShapeDtypeStruct((t, DIFF_HEADS * width), F32),
        scratch_shapes=[pltpu.VMEM((seq, width), BF16), pltpu.VMEM((width + ones_rows, seq), BF16),
                        pltpu.VMEM((width + ones_rows, blk), F32), pltpu.VMEM((width + ones_rows, blk), F32)],
        compiler_params=_cparams(("parallel", "parallel", "arbitrary")),
        name="diff_attn",
    )(proj, proj, proj, bias, lam_params, norm_w)


def _gla_body(nc, q_ref, k_ref, gd_ref, va_ref, vb_ref, ra_ref, rb_ref, wg_ref, bg_ref, nw_ref,
              o_ref, state):
    n = pl.program_id(2)

    @pl.when(n == 0)
    def _():
        state[...] = jnp.zeros_like(state)

    rows = nc * CHUNK
    nsub = rows // GLA_SUB
    per_chunk = CHUNK // GLA_SUB
    gate = jnp.dot(gd_ref[...], wg_ref[...], precision=HI, preferred_element_type=F32) + bg_ref[...]
    log_a = _log_sigmoid(gate) * (1.0 / GLA_TAU)
    q = q_ref[...] * GLA_DK ** -0.5
    k = k_ref[...]
    vs = (va_ref[...], vb_ref[...])
    lane_s = lax.broadcasted_iota(jnp.int32, (GLA_SUB, LANES), 1)
    lane_c = lax.broadcasted_iota(jnp.int32, (CHUNK, LANES), 1)
    sub_mask = (lane_s < GLA_DK, lane_s >= GLA_DK)
    chunk_mask = (lane_c < GLA_DK, lane_c >= GLA_DK)

    ri = lax.broadcasted_iota(jnp.int32, (CHUNK, CHUNK), 0)
    ci = lax.broadcasted_iota(jnp.int32, (CHUNK, CHUNK), 1)
    tri = (ri >= ci).astype(F32)
    b = jnp.concatenate(
        [jnp.dot(tri, log_a[c * CHUNK:(c + 1) * CHUNK], precision=HI, preferred_element_type=F32)
         for c in range(nc)], axis=0)

    b3 = b.reshape(nsub, GLA_SUB, LANES)
    q3 = q.reshape(nsub, GLA_SUB, LANES)
    k3 = k.reshape(nsub, GLA_SUB, LANES)
    v3 = [v.reshape(nsub, GLA_SUB, LANES) for v in vs]
    row3 = lax.broadcasted_iota(jnp.int32, (nsub, GLA_SUB, LANES), 1)
    lane3 = lax.broadcasted_iota(jnp.int32, (nsub, GLA_SUB, LANES), 2)
    rowc = lax.broadcasted_iota(jnp.int32, (nsub, GLA_SUB, 1), 1)
    o3 = [jnp.zeros((nsub, GLA_SUB, LANES), F32) for _ in vs]
    for jj in range(GLA_SUB):
        e = jnp.exp(jnp.where(row3 >= jj, b3 - b3[:, jj:jj + 1, :], 0.0))
        t = q3 * k3[:, jj:jj + 1, :] * e
        w_all = jnp.sum(t, axis=-1, keepdims=True)
        w_a = jnp.sum(jnp.where(lane3 < GLA_DK, t, 0.0), axis=-1, keepdims=True)
        for hd, w in enumerate((w_a, w_all - w_a)):
            o3[hd] = o3[hd] + jnp.where(rowc >= jj, w, 0.0) * v3[hd][:, jj:jj + 1, :]
    o_diag = [o.reshape(rows, LANES) for o in o3]

    kcol = lax.broadcasted_iota(jnp.int32, (GLA_SUB, CHUNK), 1)
    outs = ([], [])
    st = [state[0], state[1]]
    for c in range(nc):
        sl = slice(c * CHUNK, (c + 1) * CHUNK)
        bc, qc, kc = b[sl], q[sl], k[sl]
        a_rows = [[jnp.zeros((GLA_SUB, CHUNK), F32)] for _ in vs]
        for blk in range(1, per_chunk):
            r0 = blk * GLA_SUB
            bref = bc[r0:r0 + 1]
            qs = qc[r0:r0 + GLA_SUB] * jnp.exp(bc[r0:r0 + GLA_SUB] - bref)
            ks = kc * jnp.exp(jnp.minimum(bref - bc, 0.0))
            for hd in range(2):
                a = _bdot_nt(jnp.where(sub_mask[hd], qs, 0.0), ks)
                a_rows[hd].append(jnp.where(kcol < r0, a, 0.0))
        eb = jnp.exp(bc)
        b_last = bc[CHUNK - 1:CHUNK]
        k_dec = kc * jnp.exp(b_last - bc)
        e_last = jnp.exp(b_last)
        for hd in range(2):
            vc = vs[hd][sl]
            a_mat = jnp.concatenate(a_rows[hd], axis=0)
            q_dec = jnp.where(chunk_mask[hd], qc * eb, 0.0)
            outs[hd].append(_bdot(a_mat, vc) + _bdot_nt(q_dec, st[hd]))
            st[hd] = st[hd] * e_last + _bdot_tn(vc, k_dec)
    state[0] = st[0]
    state[1] = st[1]

    res = []
    for hd, r_ref in enumerate((ra_ref, rb_ref)):
        o = jnp.concatenate(outs[hd], axis=0) + o_diag[hd]
        o = o * lax.rsqrt(jnp.mean(o * o, axis=-1, keepdims=True) + RMS_EPS) * nw_ref[...]
        res.append(o * _silu(r_ref[...]))
    o_ref[...] = jnp.concatenate(res, axis=-1)


def _gla(proj, w_gate_pad, b_gate, norm_w, bsz, seq, nc):
    t = proj.shape[0]
    rows = nc * CHUNK
    spb = seq // rows
    npair = GLA_HEADS // 2

    def col(base, mul=0, add=0):
        return lambda b, p, n: (b * spb + n, base // LANES + mul * p + add)

    blk = (rows, LANES)
    return pl.pallas_call(
        functools.partial(_gla_body, nc),
        grid=(bsz, npair, spb),
        in_specs=[pl.BlockSpec(blk, col(OD_QD, 1)),
                  pl.BlockSpec(blk, col(OD_KD, 1)),
                  pl.BlockSpec(blk, col(OD_GD)),
                  pl.BlockSpec(blk, col(OD_VD, 2, 0)), pl.BlockSpec(blk, col(OD_VD, 2, 1)),
                  pl.BlockSpec(blk, col(OD_RD, 2, 0)), pl.BlockSpec(blk, col(OD_RD, 2, 1)),
                  pl.BlockSpec((LANES, LANES), lambda b, p, n: (0, p)),
                  pl.BlockSpec((1, LANES), lambda b, p, n: (0, p)),
                  pl.BlockSpec((1, GLA_DV), lambda b, p, n: (0, 0))],
        out_specs=pl.BlockSpec((rows, 2 * GLA_DV), lambda b, p, n: (b * spb + n, p)),
        out_shape=jax.ShapeDtypeStruct((t, GLA_HEADS * GLA_DV), F32),
        scratch_shapes=[pltpu.VMEM((2, GLA_DV, LANES), F32)],
        compiler_params=_cparams(("parallel", "parallel", "arbitrary")),
        name="gla",
    )(proj, proj, proj, proj, proj, proj, proj, w_gate_pad, b_gate, norm_w)


def _ffn_body(tiles_per_seq, x_ref, halo_ref, wg_ref, wv_ref, cwg_ref, cwv_ref, cbg_ref, cbv_ref,
              wd_ref, g_ref, b_ref, y_ref, xb, hb, acc, scratch):
    i = pl.program_id(0)
    j = pl.program_id(1)

    @pl.when(j == 0)
    def _():
        xb[...] = x_ref[...].astype(BF16)
        hb[...] = jnp.where(i % tiles_per_seq == 0, 0.0, halo_ref[...]).astype(BF16)
        acc[...] = jnp.zeros_like(acc)

    def branch(w_ref, cw_ref, cb_ref):
        cur = jnp.dot(xb[...], w_ref[...], preferred_element_type=F32)
        halo = jnp.dot(hb[...], w_ref[...], preferred_element_type=F32)
        return _causal_conv(halo, cur, cw_ref, scratch) + cb_ref[...]

    gate = branch(wg_ref, cwg_ref, cbg_ref)
    val = branch(wv_ref, cwv_ref, cbv_ref)
    acc[...] += _bdot(_silu(gate) * val, wd_ref[...])

    @pl.when(j == pl.num_programs(1) - 1)
    def _():
        y_ref[...] = _layer_norm(DEEPNORM_ALPHA * x_ref[...] + acc[...], g_ref[...], b_ref[...])


def _ffn(x, w_up, conv_w, conv_b, w_down, g, b, seq, tm, tc):
    t = x.shape[0]
    nj = D_FF // tc
    hb = tm // SUBLANES
    return pl.pallas_call(
        functools.partial(_ffn_body, seq // tm),
        grid=(t // tm, nj),
        in_specs=[pl.BlockSpec((tm, D_MODEL), lambda i, j: (i, 0)),
                  pl.BlockSpec((SUBLANES, D_MODEL), lambda i, j: (jnp.maximum(i * hb - 1, 0), 0)),
                  pl.BlockSpec((D_MODEL, tc), lambda i, j: (0, j)),
                  pl.BlockSpec((D_MODEL, tc), lambda i, j: (0, nj + j)),
                  pl.BlockSpec((FFN_CONV, tc), lambda i, j: (0, j)),
                  pl.BlockSpec((FFN_CONV, tc), lambda i, j: (0, nj + j)),
                  pl.BlockSpec((1, tc), lambda i, j: (0, j)),
                  pl.BlockSpec((1, tc), lambda i, j: (0, nj + j)),
                  pl.BlockSpec((tc, D_MODEL), lambda i, j: (j, 0)),
                  pl.BlockSpec((1, D_MODEL), lambda i, j: (0, 0)),
                  pl.BlockSpec((1, D_MODEL), lambda i, j: (0, 0))],
        out_specs=pl.BlockSpec((tm, D_MODEL), lambda i, j: (i, 0)),
        out_shape=jax.ShapeDtypeStruct((t, D_MODEL), F32),
        scratch_shapes=[pltpu.VMEM((tm, D_MODEL), BF16), pltpu.VMEM((SUBLANES, D_MODEL), BF16),
                        pltpu.VMEM((tm, D_MODEL), F32), pltpu.VMEM((tm + SUBLANES, tc), F32)],
        compiler_params=_cparams(("parallel", "arbitrary")),
        name="conv_ffn",
    )(x, x, w_up, w_up, conv_w, conv_w, conv_b, conv_b, w_down, g, b)


def _even_w_in(w):
    qk, v = GDN_HEADS * GDN_D, GDN_HEADS * GDN_D
    a_end = 2 * qk + 2 * v
    gates = w[:, a_end:a_end + 2 * GDN_HEADS]
    qkv_b = w[:, a_end + 2 * GDN_HEADS:]
    pad = jnp.zeros((w.shape[0], EV_COLS - EV_BA - 2 * GDN_HEADS), w.dtype)
    return jnp.concatenate([w[:, :a_end], qkv_b, gates, pad], axis=1).astype(BF16)


def _odd_w_in(w):
    pad = jnp.zeros((w.shape[0], OD_COLS - w.shape[1]), w.dtype)
    return jnp.concatenate([w, pad], axis=1).astype(BF16)


def _even_layer(h, rel_bias, w_in, conv_w, a_log, dt_bias, norm_w, w_out, ln_g, ln_b, bsz, seq):
    proj = _matmul(h, _even_w_in(w_in), 1024, 512)
    qkv = _gdn_prep(proj, conv_w, seq, 1024)
    gate_pad = jnp.zeros((1, LANES), F32)
    alog_pad = lax.dynamic_update_slice(gate_pad, a_log[None].astype(F32), (0, GDN_HEADS))
    dtb_pad = lax.dynamic_update_slice(gate_pad, dt_bias[None].astype(F32), (0, GDN_HEADS))
    o_a = _gdn(qkv, proj, alog_pad, dtb_pad, norm_w[None], bsz, seq)
    tiles = _bias_tiles(rel_bias, *_swa_bias_tables())
    ng, two_blk = len(SWA_CONFIGS), 2 * SWA_BLOCK
    bias = tiles.reshape(SWA_HEADS // 2, 2, ng, SWA_BLOCK, two_blk).transpose(2, 0, 1, 3, 4)
    o_b = _swa(proj, bias.reshape(ng, SWA_HEADS // 2, two_blk, two_blk), bsz, seq)
    gdn_v = GDN_HEADS * GDN_D
    w_out = w_out.astype(BF16)
    return _out_proj(_out_body, [o_a, o_b], [w_out[:gdn_v], w_out[gdn_v:]],
                     h, ln_g[None], ln_b[None], 512, "even_out")


def _odd_layer(h, rel_bias, w_in, lam_params, diff_norm_w, w_gate, b_gate, gla_norm_w, w_out,
               ln_g, ln_b, lam_init, bsz, seq):
    proj = _matmul(h, _odd_w_in(w_in), 1024, 640)
    blk = min(DIFF_BLOCK, seq)
    bucket, neg = _diff_bias_tables(blk)
    bias = _bias_tiles(rel_bias, bucket, neg, scale=LOG2E, base_bucket=NUM_BUCKETS - 1)
    o_c = _diff_attention(proj, bias, lam_params, diff_norm_w[None], lam_init, bsz, seq, blk)
    w_gate_pad = jnp.concatenate(
        [w_gate, jnp.zeros((LANES - GLA_RANK, w_gate.shape[1]), w_gate.dtype)], axis=0)
    o_d = _gla(proj, w_gate_pad, b_gate[None], gla_norm_w[None], bsz, seq, 4)
    diff_v = DIFF_HEADS * 2 * DIFF_DH
    w_out = w_out.astype(BF16)
    return _out_proj(_out_body, [o_c, o_d], [w_out[:diff_v], w_out[diff_v:]],
                     h, ln_g[None], ln_b[None], 512, "odd_out")


def kernel(x, rel_bias, w_in_even, gdn_conv_w, gdn_a_log, gdn_dt_bias, gdn_norm_w, w_out_even,
           w_in_odd, diff_lambda, diff_norm_w, gla_w_gate, gla_b_gate, gla_norm_w, w_out_odd,
           ffn_w_up, ffn_conv_w, ffn_conv_b, ffn_w_down, ln_g, ln_b):
    bsz, seq, d = x.shape
    h = x.reshape(bsz * seq, d)
    for layer in range(DEPTH):
        i = layer // 2
        if layer % 2 == 0:
            h = _even_layer(h, rel_bias, w_in_even[i], gdn_conv_w[i], gdn_a_log[i], gdn_dt_bias[i],
                            gdn_norm_w[i], w_out_even[i], ln_g[layer, 0], ln_b[layer, 0], bsz, seq)
        else:
            lam_init = 0.8 - 0.6 * math.exp(-0.3 * layer)
            h = _odd_layer(h, rel_bias, w_in_odd[i], diff_lambda[i], diff_norm_w[i], gla_w_gate[i],
                           gla_b_gate[i], gla_norm_w[i], w_out_odd[i], ln_g[layer, 0], ln_b[layer, 0],
                           lam_init, bsz, seq)
        h = _ffn(h, ffn_w_up[layer].astype(BF16), ffn_conv_w[layer], ffn_conv_b[layer][None],
                 ffn_w_down[layer].astype(BF16), ln_g[layer, 1][None], ln_b[layer, 1][None],
                 seq, 1024, 256)
    return h.reshape(bsz, seq, d)
```

```python
import functools
import math

import numpy as np
import jax
import jax.numpy as jnp
from jax import lax
from jax.experimental import pallas as pl
from jax.experimental.pallas import tpu as pltpu

F32 = jnp.float32
BF16 = jnp.bfloat16
HI = lax.Precision.HIGHEST

D_MODEL = 1024
DEPTH = 2
DEEPNORM_ALPHA = (2 * DEPTH) ** 0.25
LN_EPS = 1e-5
RMS_EPS = 1e-6
NUM_BUCKETS = 32
REL_MAX_DIST = 2048
GDN_HEADS = 6
GDN_D = 128
GDN_CONV = 4
CHUNK = 64
GDN_GROUP = 256
SWA_CONFIGS = ((128, 1), (512, 4), (2048, 16))
SWA_HEADS = 4
SWA_DH = 64
SWA_BLOCK = 128
SWA_SPAN = SWA_BLOCK * max(d for _, d in SWA_CONFIGS)
DIFF_HEADS = 4
DIFF_DH = 64
DIFF_BLOCK = 512
LOG2E = math.log2(math.e)
GLA_HEADS = 4
GLA_DK = 64
GLA_DV = 128
GLA_RANK = 16
GLA_TAU = 16.0
GLA_SUB = 16
D_FF = 2816
FFN_CONV = 3

LANES = 128
SUBLANES = 8
VMEM_LIMIT = 56 * 1024 * 1024
NEG_BIG = -1e30

EV_QKV_A = 0
EV_Z = 2304
EV_QKV_B = 3072
EV_BA = 5376
EV_COLS = 5760
EV_TN = 384
OD_QC, OD_KC, OD_VC = 0, 512, 1024
OD_QD, OD_KD, OD_VD, OD_RD, OD_GD = 1536, 1792, 2048, 2560, 3072
OD_COLS = 3200


def _cparams(sem):
    return pltpu.CompilerParams(dimension_semantics=sem, vmem_limit_bytes=VMEM_LIMIT)


def _bdot(a, b):
    return jnp.dot(a.astype(BF16), b.astype(BF16), preferred_element_type=F32)


def _bdot_nt(a, b):
    return lax.dot_general(a.astype(BF16), b.astype(BF16), (((1,), (1,)), ((), ())),
                           preferred_element_type=F32)


def _bdot_tn(a, b):
    return lax.dot_general(a.astype(BF16), b.astype(BF16), (((0,), (0,)), ((), ())),
                           preferred_element_type=F32)


def _sigmoid(x):
    return 1.0 / (1.0 + jnp.exp(-x))


def _silu(x):
    return x * _sigmoid(x)


def _softplus(x):
    return jnp.maximum(x, 0.0) + jnp.log1p(jnp.exp(-jnp.abs(x)))


def _log_sigmoid(x):
    return -_softplus(-x)


def _mm_body(x_ref, w_ref, o_ref, xb):
    @pl.when(pl.program_id(1) == 0)
    def _():
        xb[...] = x_ref[...].astype(BF16)

    o_ref[...] = jnp.dot(xb[...], w_ref[...], preferred_element_type=F32)


def _matmul(x, w, tm, tn):
    m, k = x.shape
    n = w.shape[1]
    return pl.pallas_call(
        _mm_body,
        grid=(m // tm, n // tn),
        in_specs=[pl.BlockSpec((tm, k), lambda i, j: (i, 0)),
                  pl.BlockSpec((k, tn), lambda i, j: (0, j))],
        out_specs=pl.BlockSpec((tm, tn), lambda i, j: (i, j)),
        out_shape=jax.ShapeDtypeStruct((m, n), F32),
        scratch_shapes=[pltpu.VMEM((tm, k), BF16)],
        compiler_params=_cparams(("parallel", "arbitrary")),
        name="in_proj",
    )(x, w)


def _rel_bucket_np(dist):
    max_exact = NUM_BUCKETS // 2
    d = np.maximum(dist, 1).astype(np.float32)
    large = max_exact + (np.log(d / max_exact) / math.log(REL_MAX_DIST / max_exact)
                         * (NUM_BUCKETS - max_exact)).astype(np.int32)
    large = np.minimum(large, NUM_BUCKETS - 1)
    return np.where(dist < max_exact, dist, large).astype(np.int32)


def _bias_body(scale, base_bucket, rb_ref, bucket_ref, neg_ref, o_ref):
    h = pl.program_id(0)
    bucket = bucket_ref[0]
    base = 0.0 if base_bucket is None else rb_ref[base_bucket, h]
    acc = neg_ref[0]
    for b in range(NUM_BUCKETS):
        acc = acc + jnp.where(bucket == b, (rb_ref[b, h] - base) * scale, 0.0)
    o_ref[0, 0] = acc


def _bias_tiles(rel_bias, bucket, neg, scale=1.0, base_bucket=None):
    nt, r, c = bucket.shape
    nh = rel_bias.shape[1]
    return pl.pallas_call(
        functools.partial(_bias_body, scale, base_bucket),
        grid=(nh, nt),
        in_specs=[pl.BlockSpec(memory_space=pltpu.SMEM),
                  pl.BlockSpec((1, r, c), lambda h, t: (t, 0, 0)),
                  pl.BlockSpec((1, r, c), lambda h, t: (t, 0, 0))],
        out_specs=pl.BlockSpec((1, 1, r, c), lambda h, t: (h, t, 0, 0)),
        out_shape=jax.ShapeDtypeStruct((nh, nt, r, c), F32),
        compiler_params=_cparams(("parallel", "parallel")),
        name="rel_bias_tiles",
    )(rel_bias, jnp.asarray(bucket), jnp.asarray(neg))


def _causal_conv(halo, cur, w, scratch):
    width = w.shape[0]
    rows = cur.shape[0]
    scratch[0:SUBLANES, :] = halo
    scratch[SUBLANES:SUBLANES + rows, :] = cur
    y = w[width - 1:width, :] * cur
    for j in range(width - 1):
        back = width - 1 - j
        y = y + w[j:j + 1, :] * scratch[SUBLANES - back:SUBLANES - back + rows, :]
    return y


def _even_in_body(tiles_per_seq, x_ref, halo_ref, w_ref, cw_ref, o_ref, xb, hb, scratch):
    i = pl.program_id(0)
    j = pl.program_id(1)

    @pl.when(j == 0)
    def _():
        xb[...] = x_ref[...].astype(BF16)
        hb[...] = jnp.where(i % tiles_per_seq == 0, 0.0, halo_ref[...]).astype(BF16)

    y = jnp.dot(xb[...], w_ref[...], preferred_element_type=F32)
    blocks_per_part = GDN_HEADS * GDN_D // EV_TN

    @pl.when(j < 3 * blocks_per_part)
    def _():
        halo = jnp.dot(hb[...], w_ref[...], preferred_element_type=F32)
        c = _silu(_causal_conv(halo, y, cw_ref[...], scratch))
        qk_scale = jnp.where(j < blocks_per_part, GDN_D ** -0.5, 1.0)
        for hd in range(EV_TN // GDN_D):
            ch = c[:, hd * GDN_D:(hd + 1) * GDN_D]
            inv = lax.rsqrt(jnp.sum(ch * ch, axis=-1, keepdims=True) + RMS_EPS) * qk_scale
            o_ref[:, hd * GDN_D:(hd + 1) * GDN_D] = ch * jnp.where(j < 2 * blocks_per_part, inv, 1.0)

    @pl.when(j >= 3 * blocks_per_part)
    def _():
        o_ref[...] = y


def _even_in_proj(x, w, conv_w, seq, tm):
    t, k = x.shape
    hb = tm // SUBLANES
    last_conv = conv_w.shape[1] // EV_TN - 1
    return pl.pallas_call(
        functools.partial(_even_in_body, seq // tm),
        grid=(t // tm, EV_COLS // EV_TN),
        in_specs=[pl.BlockSpec((tm, k), lambda i, j: (i, 0)),
                  pl.BlockSpec((SUBLANES, k), lambda i, j: (jnp.maximum(i * hb - 1, 0), 0)),
                  pl.BlockSpec((k, EV_TN), lambda i, j: (0, j)),
                  pl.BlockSpec((GDN_CONV, EV_TN), lambda i, j: (0, jnp.minimum(j, last_conv)))],
        out_specs=pl.BlockSpec((tm, EV_TN), lambda i, j: (i, j)),
        out_shape=jax.ShapeDtypeStruct((t, EV_COLS), F32),
        scratch_shapes=[pltpu.VMEM((tm, k), BF16), pltpu.VMEM((SUBLANES, k), BF16),
                        pltpu.VMEM((tm + SUBLANES, EV_TN), F32)],
        compiler_params=_cparams(("parallel", "arbitrary")),
        name="even_in_proj",
    )(x, x, w, conv_w)


def _gdn_body(q_ref, k_ref, v_ref, ba_ref, z_ref, alog_ref, dtb_ref, nw_ref, o_ref, state):
    n = pl.program_id(1)

    @pl.when(n == 0)
    def _():
        state[...] = jnp.zeros_like(state)

    grp = GDN_GROUP
    nc = grp // CHUNK
    nh = GDN_HEADS
    ba = ba_ref[...]
    beta_all = _sigmoid(ba)
    g_all = -jnp.exp(alog_ref[...]) * _softplus(ba + dtb_ref[...])

    ri = lax.broadcasted_iota(jnp.int32, (grp, grp), 0)
    ci = lax.broadcasted_iota(jnp.int32, (grp, grp), 1)
    same = (ri // CHUNK) == (ci // CHUNK)
    incl = same & (ri >= ci)
    strict = same & (ri > ci)
    eye = (ri == ci).astype(F32)
    tri = incl.astype(BF16)
    g_hi = g_all.astype(BF16)
    rem = g_all - g_hi.astype(F32)
    g_mid = rem.astype(BF16)
    g_lo = (rem - g_mid.astype(F32)).astype(BF16)
    gam_all = (jnp.dot(tri, g_hi, preferred_element_type=F32) + jnp.dot(tri, g_mid, preferred_element_type=F32)
               + jnp.dot(tri, g_lo, preferred_element_type=F32))
    gam_rows = gam_all.T

    hs = range(nh)
    q = [q_ref[:, h * GDN_D:(h + 1) * GDN_D] for h in hs]
    k = [k_ref[:, h * GDN_D:(h + 1) * GDN_D] for h in hs]
    v = [v_ref[:, h * GDN_D:(h + 1) * GDN_D] for h in hs]
    gam = [jnp.broadcast_to(gam_all[:, nh + h:nh + h + 1], (grp, GDN_D)) for h in hs]
    bcol = [beta_all[:, h:h + 1] for h in hs]
    decay, x, inv = [], [], []
    for h in hs:
        diff = gam[h][:, 0:1] - gam_rows[nh + h:nh + h + 1, :]
        decay.append(jnp.where(incl, jnp.exp(jnp.where(incl, diff, 0.0)), 0.0))
        kk = _bdot_nt(k[h], k[h])
        x.append(-jnp.where(strict, bcol[h] * kk * decay[h], 0.0))
        inv.append(eye + x[h])
    for _ in range(5):
        for h in hs:
            x[h] = _bdot(x[h], x[h])
            inv[h] = inv[h] + _bdot(inv[h], x[h])
    u, w, qk, q_dec, k_dec, g_last = [], [], [], [], [], []
    for h in hs:
        eg = jnp.exp(gam[h])
        uw = _bdot(inv[h], jnp.concatenate([v[h] * bcol[h], k[h] * (bcol[h] * eg)], axis=1))
        u.append(uw[:, :GDN_D])
        w.append(uw[:, GDN_D:])
        qk.append(_bdot_nt(q[h], k[h]) * decay[h])
        q_dec.append(q[h] * eg)
        kd, gl = [], []
        for c in range(nc):
            last = gam[h][(c + 1) * CHUNK - 1:(c + 1) * CHUNK, :]
            kd.append(k[h][c * CHUNK:(c + 1) * CHUNK] * jnp.exp(last - gam[h][c * CHUNK:(c + 1) * CHUNK]))
            gl.append(jnp.exp(last))
        k_dec.append(kd)
        g_last.append(gl)

    s = [state[h] for h in hs]
    q_s = [[] for _ in hs]
    delta = [[] for _ in hs]
    for c in range(nc):
        sl = slice(c * CHUNK, (c + 1) * CHUNK)
        for h in hs:
            r = _bdot(jnp.concatenate([w[h][sl], q_dec[h][sl]], axis=0), s[h])
            d = u[h][sl] - r[:CHUNK]
            q_s[h].append(r[CHUNK:])
            delta[h].append(d)
            s[h] = g_last[h][c] * s[h] + _bdot_tn(k_dec[h][c], d)
    for h in hs:
        state[h] = s[h]
        o = jnp.concatenate(q_s[h], axis=0) + _bdot(qk[h], jnp.concatenate(delta[h], axis=0))
        o = o * lax.rsqrt(jnp.mean(o * o, axis=-1, keepdims=True) + RMS_EPS) * nw_ref[...]
        o_ref[:, h * GDN_D:(h + 1) * GDN_D] = o * _silu(z_ref[:, h * GDN_D:(h + 1) * GDN_D])


def _gdn(qkv, proj, alog_pad, dtb_pad, norm_w, bsz, seq):
    t = qkv.shape[0]
    rows = GDN_GROUP
    spb = seq // rows
    width = GDN_HEADS * GDN_D

    def at(col):
        return lambda b, n: (b * spb + n, col)

    return pl.pallas_call(
        _gdn_body,
        grid=(bsz, spb),
        in_specs=[pl.BlockSpec((rows, width), at(0)),
                  pl.BlockSpec((rows, width), at(1)),
                  pl.BlockSpec((rows, width), at(2)),
                  pl.BlockSpec((rows, LANES), at(EV_BA // LANES)),
                  pl.BlockSpec((rows, width), at(EV_Z // width)),
                  pl.BlockSpec((1, LANES), lambda b, n: (0, 0)),
                  pl.BlockSpec((1, LANES), lambda b, n: (0, 0)),
                  pl.BlockSpec((1, LANES), lambda b, n: (0, 0))],
        out_specs=pl.BlockSpec((rows, width), at(0)),
        out_shape=jax.ShapeDtypeStruct((t, width), F32),
        scratch_shapes=[pltpu.VMEM((GDN_HEADS, GDN_D, GDN_D), F32)],
        compiler_params=_cparams(("parallel", "arbitrary")),
        name="gdn",
    )(qkv, qkv, qkv, proj, proj, alog_pad, dtb_pad, norm_w)


def _swa_bias_tables():
    qi = np.arange(SWA_BLOCK)[:, None] + SWA_BLOCK
    kj = np.arange(2 * SWA_BLOCK)[None, :]
    rel = qi - kj
    buckets, negs = [], []
    for window, dilation in SWA_CONFIGS:
        buckets.append(_rel_bucket_np(np.maximum(rel, 0) * dilation))
        negs.append(np.where((rel >= 0) & (rel <= window // dilation), 0.0, NEG_BIG))
    return np.stack(buckets).astype(np.int32), np.stack(negs).astype(np.float32)


def _swa_body(*refs):
    ng = len(SWA_CONFIGS)
    ins, bias_ref, o_ref = refs[:5 * ng], refs[5 * ng], refs[5 * ng + 1]
    scratch = refs[5 * ng + 2:]
    kbufs, vbufs, o_scr, lse_scr = scratch[:ng], scratch[ng:2 * ng], scratch[2 * ng], scratch[2 * ng + 1]
    j = pl.program_id(1)
    blk = SWA_BLOCK
    first_head = lax.broadcasted_iota(jnp.int32, (blk, LANES), 1) < SWA_DH
    in_prev = lax.broadcasted_iota(jnp.int32, (2 * blk, 2 * blk), 1) < blk

    for g, (_, d) in enumerate(SWA_CONFIGS):
        q_ref, kc_ref, kp_ref, vc_ref, vp_ref = ins[5 * g:5 * g + 5]
        kbuf, vbuf = kbufs[g], vbufs[g]
        halo = blk * d
        kbuf[0:halo, :] = kp_ref[...]
        kbuf[halo:, :] = kc_ref[...]
        vbuf[0:halo, :] = vp_ref[...]
        vbuf[halo:, :] = vc_ref[...]

        def unit(u, carry, g=g, d=d, halo=halo, q_ref=q_ref, kbuf=kbuf, vbuf=vbuf):
            base = (u // d) * halo + u % d
            q = q_ref[pl.ds(base, blk, stride=d), :] * SWA_DH ** -0.5
            k = kbuf[pl.ds(base, 2 * blk, stride=d), :]
            v = vbuf[pl.ds(base, 2 * blk, stride=d), :]
            lhs = jnp.concatenate([jnp.where(first_head, q, 0.0), jnp.where(first_head, 0.0, q)], axis=0)
            s = _bdot_nt(lhs, k) + bias_ref[g, 0]
            no_prev = (j == 0) & (u < d)
            s = jnp.where(in_prev & no_prev, NEG_BIG, s)
            m = jnp.max(s, axis=-1, keepdims=True)
            p = jnp.exp(s - m)
            l = jnp.sum(p, axis=-1, keepdims=True)
            o2 = _bdot(p / l, v)
            lse2 = m + jnp.log(l)
            o_scr[g, pl.ds(base, blk, stride=d), :] = jnp.where(first_head, o2[:blk], o2[blk:])
            lse_scr[g, pl.ds(base, blk, stride=d), :] = jnp.where(first_head, lse2[:blk], lse2[blk:])
            return carry

        lax.fori_loop(0, SWA_SPAN // blk, unit, 0)

    def combine(c, carry):
        rows = pl.ds(pl.multiple_of(c * 2 * blk, 2 * blk), 2 * blk)
        lse = [lse_scr[g, rows, :] for g in range(ng)]
        m = functools.reduce(jnp.maximum, lse)
        e = [jnp.exp(x - m) for x in lse]
        den = functools.reduce(lambda x, y: x + y, e)
        o_ref[rows, :] = functools.reduce(lambda x, y: x + y,
                                          [(e[g] / den) * o_scr[g, rows, :] for g in range(ng)])
        return carry

    lax.fori_loop(0, SWA_SPAN // (2 * blk), combine, 0)


def _swa(proj, bias, bsz, seq):
    t = proj.shape[0]
    ng = len(SWA_CONFIGS)
    nspan = seq // SWA_SPAN
    npair = SWA_HEADS * SWA_DH // LANES
    group_cols = SWA_HEADS * SWA_DH // LANES
    in_specs, scratch_k = [], []
    for g, (_, d) in enumerate(SWA_CONFIGS):
        halo = SWA_BLOCK * d
        per_span = SWA_SPAN // halo

        def cur(which, g=g):
            col = EV_QKV_B // LANES + (which * ng + g) * group_cols
            return lambda b, j, p: (b * nspan + j, col + p)

        def prev(which, g=g, per_span=per_span):
            col = EV_QKV_B // LANES + (which * ng + g) * group_cols
            return lambda b, j, p: (jnp.maximum((b * nspan + j) * per_span - 1, 0), col + p)

        in_specs += [pl.BlockSpec((SWA_SPAN, LANES), cur(0)),
                     pl.BlockSpec((SWA_SPAN, LANES), cur(1)), pl.BlockSpec((halo, LANES), prev(1)),
                     pl.BlockSpec((SWA_SPAN, LANES), cur(2)), pl.BlockSpec((halo, LANES), prev(2))]
        scratch_k.append(pltpu.VMEM((halo + SWA_SPAN, LANES), F32))
    in_specs.append(pl.BlockSpec((ng, 1, 2 * SWA_BLOCK, 2 * SWA_BLOCK), lambda b, j, p: (0, p, 0, 0)))
    return pl.pallas_call(
        _swa_body,
        grid=(bsz, nspan, npair),
        in_specs=in_specs,
        out_specs=pl.BlockSpec((SWA_SPAN, LANES), lambda b, j, p: (b * nspan + j, p)),
        out_shape=jax.ShapeDtypeStruct((t, npair * LANES), F32),
        scratch_shapes=scratch_k + scratch_k + [pltpu.VMEM((ng, SWA_SPAN, LANES), F32)] * 2,
        compiler_params=_cparams(("parallel", "parallel", "parallel")),
        name="swa",
    )(*([proj] * (5 * ng)), bias)


def _layer_norm(y, g, b):
    mu = jnp.mean(y, axis=-1, keepdims=True)
    yc = y - mu
    var = jnp.mean(yc * yc, axis=-1, keepdims=True)
    return yc * lax.rsqrt(var + LN_EPS) * g + b


def _out_body(xa_ref, xb_ref, wa_ref, wb_ref, h_ref, g_ref, b_ref, y_ref):
    mix = _bdot(xa_ref[...], wa_ref[...]) + _bdot(xb_ref[...], wb_ref[...])
    y_ref[...] = _layer_norm(DEEPNORM_ALPHA * h_ref[...] + mix, g_ref[...], b_ref[...])


def _out_proj(body, acts, weights, h, g, b, tm, name):
    t = h.shape[0]

    def rowblk(a):
        return pl.BlockSpec((tm, a.shape[1]), lambda i: (i, 0))

    def full(a):
        return pl.BlockSpec(a.shape, lambda i: (0, 0))

    return pl.pallas_call(
        body,
        grid=(t // tm,),
        in_specs=[rowblk(a) for a in acts] + [full(w) for w in weights] + [rowblk(h), full(g), full(b)],
        out_specs=pl.BlockSpec((tm, D_MODEL), lambda i: (i, 0)),
        out_shape=jax.ShapeDtypeStruct((t, D_MODEL), F32),
        compiler_params=_cparams(("parallel",)),
        name=name,
    )(*acts, *weights, h, g, b)


def _diff_bias_tables(blk):
    buckets = _rel_bucket_np(np.arange(2 * REL_MAX_DIST))
    far = int(np.max(np.nonzero(buckets != NUM_BUCKETS - 1)[0])) + 1
    nb = -(-(far + blk - 1) // blk)
    kk = np.arange(blk)[:, None]
    qq = np.arange(blk)[None, :]
    dist = np.stack([t * blk + qq - kk for t in range(-1, nb + 1)])
    bucket = _rel_bucket_np(np.maximum(dist, 0))
    neg = np.where(dist >= 0, 0.0, NEG_BIG).astype(np.float32)
    return bucket, neg


def _diff_body(blk, nb, lam_init, q_ref, k_ref, v_ref, bias_ref, lam_ref, nw_ref, o_ref,
               kb, vt, acc1, acc2, s_a, s_b):
    qi = pl.program_id(2)
    dv = 2 * DIFF_DH
    seq = k_ref.shape[0]

    @pl.when(qi == 0)
    def _():
        kb[...] = k_ref[...].astype(BF16)
        vt[dv:, :] = jnp.ones((vt.shape[0] - dv, seq), BF16)

        def fill(c, carry):
            st = pl.multiple_of(c * blk, blk)
            vt[0:dv, pl.ds(st, blk)] = v_ref[pl.ds(st, blk), :].T.astype(BF16)
            return carry

        lax.fori_loop(0, seq // blk, fill, 0)

    lane = lax.broadcasted_iota(jnp.int32, (blk, dv), 1)
    q = q_ref[...] * (DIFF_DH ** -0.5 * LOG2E)
    qs = (jnp.where(lane < DIFF_DH, q, 0.0).astype(BF16), jnp.where(lane >= DIFF_DH, q, 0.0).astype(BF16))
    accs = (acc1, acc2)
    for acc in accs:
        acc[...] = jnp.zeros_like(acc)

    last = pl.num_programs(2) - 1

    def key_rows(kj):
        return pl.ds(pl.multiple_of(jnp.minimum(kj, last) * blk, blk), blk)

    def scores(kj, dst):
        k = kb[key_rows(kj), :]
        for c, qc in enumerate(qs):
            dst[c] = lax.dot_general(k, qc, (((1,), (1,)), ((), ())), preferred_element_type=F32)

    def consume(kj, src, ms):
        vtb = vt[:, key_rows(kj)]
        bias = bias_ref[0, jnp.minimum(qi - kj, nb) + 1]
        out = []
        for c, (m, acc) in enumerate(zip(ms, accs)):
            s = src[c] + bias
            m_new = jnp.maximum(m, jnp.max(s, axis=0, keepdims=True))
            p = jnp.exp2(s - m_new).astype(BF16)
            acc[...] = jnp.exp2(m - m_new) * acc[...] + jnp.dot(vtb, p, preferred_element_type=F32)
            out.append(m_new)
        return tuple(out)

    def pair(t, ms):
        kj = 2 * t
        scores(kj + 1, s_b)
        ms = consume(kj, s_a, ms)
        scores(kj + 2, s_a)
        return consume(kj + 1, s_b, ms)

    m0 = jnp.full((1, blk), NEG_BIG, F32)
    scores(0, s_a)
    lax.fori_loop(0, (qi + 2) // 2, pair, (m0, m0))

    lp = lam_ref[...]
    lam = (jnp.exp(jnp.sum(lp[0:1] * lp[1:2], axis=-1, keepdims=True))
           - jnp.exp(jnp.sum(lp[2:3] * lp[3:4], axis=-1, keepdims=True)) + lam_init)
    a1, a2 = acc1[...], acc2[...]
    o_t = a1[:dv] / a1[dv:dv + 1] - lam * (a2[:dv] / a2[dv:dv + 1])
    o = o_t.T
    o = o * lax.rsqrt(jnp.mean(o * o, axis=-1, keepdims=True) + RMS_EPS) * nw_ref[...]
    o_ref[...] = o * (1.0 - lam_init)


def _diff_attention(proj, bias, lam_params, norm_w, lam_init, bsz, seq, blk):
    t = proj.shape[0]
    nq = seq // blk
    nt = bias.shape[1]
    nb = nt - 2
    width = 2 * DIFF_DH
    ones_rows = 2 * SUBLANES
    return pl.pallas_call(
        functools.partial(_diff_body, blk, nb, lam_init),
        grid=(bsz, DIFF_HEADS, nq),
        in_specs=[pl.BlockSpec((blk, width), lambda b, h, i: (b * nq + i, OD_QC // width + h)),
                  pl.BlockSpec((seq, width), lambda b, h, i: (b, OD_KC // width + h)),
                  pl.BlockSpec((seq, width), lambda b, h, i: (b, OD_VC // width + h)),
                  pl.BlockSpec((1, nt, blk, blk), lambda b, h, i: (h, 0, 0, 0)),
                  pl.BlockSpec((4, DIFF_DH), lambda b, h, i: (0, 0)),
                  pl.BlockSpec((1, width), lambda b, h, i: (0, 0))],
        out_specs=pl.BlockSpec((blk, width), lambda b, h, i: (b * nq + i, h)),
        out_shape=jax.ShapeDtypeStruct((t, DIFF_HEADS * width), F32),
        scratch_shapes=[pltpu.VMEM((seq, width), BF16), pltpu.VMEM((width + ones_rows, seq), BF16),
                        pltpu.VMEM((width + ones_rows, blk), F32), pltpu.VMEM((width + ones_rows, blk), F32),
                        pltpu.VMEM((2, blk, blk), F32), pltpu.VMEM((2, blk, blk), F32)],
        compiler_params=_cparams(("parallel", "parallel", "arbitrary")),
        name="diff_attn",
    )(proj, proj, proj, bias, lam_params, norm_w)


def _gla_body(nc, q_ref, k_ref, gd_ref, va_ref, vb_ref, ra_ref, rb_ref, wg_ref, bg_ref, nw_ref,
              o_ref, state):
    n = pl.program_id(2)

    @pl.when(n == 0)
    def _():
        state[...] = jnp.zeros_like(state)

    rows = nc * CHUNK
    nsub = rows // GLA_SUB
    per_chunk = CHUNK // GLA_SUB
    gate = jnp.dot(gd_ref[...], wg_ref[...], precision=HI, preferred_element_type=F32) + bg_ref[...]
    log_a = _log_sigmoid(gate) * (1.0 / GLA_TAU)
    q = q_ref[...] * GLA_DK ** -0.5
    k = k_ref[...]
    vs = (va_ref[...], vb_ref[...])
    lane_s = lax.broadcasted_iota(jnp.int32, (GLA_SUB, LANES), 1)
    lane_c = lax.broadcasted_iota(jnp.int32, (CHUNK, LANES), 1)
    sub_mask = (lane_s < GLA_DK, lane_s >= GLA_DK)
    chunk_mask = (lane_c < GLA_DK, lane_c >= GLA_DK)

    ri = lax.broadcasted_iota(jnp.int32, (CHUNK, CHUNK), 0)
    ci = lax.broadcasted_iota(jnp.int32, (CHUNK, CHUNK), 1)
    tri = (ri >= ci).astype(F32)
    b = jnp.concatenate(
        [jnp.dot(tri, log_a[c * CHUNK:(c + 1) * CHUNK], precision=HI, preferred_element_type=F32)
         for c in range(nc)], axis=0)

    b3 = b.reshape(nsub, GLA_SUB, LANES)
    q3 = q.reshape(nsub, GLA_SUB, LANES)
    k3 = k.reshape(nsub, GLA_SUB, LANES)
    v3 = [v.reshape(nsub, GLA_SUB, LANES) for v in vs]
    row3 = lax.broadcasted_iota(jnp.int32, (nsub, GLA_SUB, LANES), 1)
    lane3 = lax.broadcasted_iota(jnp.int32, (nsub, GLA_SUB, LANES), 2)
    rowc = lax.broadcasted_iota(jnp.int32, (nsub, GLA_SUB, 1), 1)
    o3 = [jnp.zeros((nsub, GLA_SUB, LANES), F32) for _ in vs]
    for jj in range(GLA_SUB):
        e = jnp.exp(jnp.where(row3 >= jj, b3 - b3[:, jj:jj + 1, :], 0.0))
        t = q3 * k3[:, jj:jj + 1, :] * e
        w_all = jnp.sum(t, axis=-1, keepdims=True)
        w_a = jnp.sum(jnp.where(lane3 < GLA_DK, t, 0.0), axis=-1, keepdims=True)
        for hd, w in enumerate((w_a, w_all - w_a)):
            o3[hd] = o3[hd] + jnp.where(rowc >= jj, w, 0.0) * v3[hd][:, jj:jj + 1, :]
    o_diag = [o.reshape(rows, LANES) for o in o3]

    kcol = lax.broadcasted_iota(jnp.int32, (GLA_SUB, CHUNK), 1)
    outs = ([], [])
    st = [state[0], state[1]]
    for c in range(nc):
        sl = slice(c * CHUNK, (c + 1) * CHUNK)
        bc, qc, kc = b[sl], q[sl], k[sl]
        a_rows = [[jnp.zeros((GLA_SUB, CHUNK), F32)] for _ in vs]
        for blk in range(1, per_chunk):
            r0 = blk * GLA_SUB
            bref = bc[r0:r0 + 1]
            qs = qc[r0:r0 + GLA_SUB] * jnp.exp(bc[r0:r0 + GLA_SUB] - bref)
            ks = kc * jnp.exp(jnp.minimum(bref - bc, 0.0))
            for hd in range(2):
                a = _bdot_nt(jnp.where(sub_mask[hd], qs, 0.0), ks)
                a_rows[hd].append(jnp.where(kcol < r0, a, 0.0))
        eb = jnp.exp(bc)
        b_last = bc[CHUNK - 1:CHUNK]
        k_dec = kc * jnp.exp(b_last - bc)
        e_last = jnp.exp(b_last)
        for hd in range(2):
            vc = vs[hd][sl]
            a_mat = jnp.concatenate(a_rows[hd], axis=0)
            q_dec = jnp.where(chunk_mask[hd], qc * eb, 0.0)
            outs[hd].append(_bdot(a_mat, vc) + _bdot_nt(q_dec, st[hd]))
            st[hd] = st[hd] * e_last + _bdot_tn(vc, k_dec)
    state[0] = st[0]
    state[1] = st[1]

    res = []
    for hd, r_ref in enumerate((ra_ref, rb_ref)):
        o = jnp.concatenate(outs[hd], axis=0) + o_diag[hd]
        o = o * lax.rsqrt(jnp.mean(o * o, axis=-1, keepdims=True) + RMS_EPS) * nw_ref[...]
        res.append(o * _silu(r_ref[...]))
    o_ref[...] = jnp.concatenate(res, axis=-1)


def _gla(proj, w_gate_pad, b_gate, norm_w, bsz, seq, nc):
    t = proj.shape[0]
    rows = nc * CHUNK
    spb = seq // rows
    npair = GLA_HEADS // 2

    def col(base, mul=0, add=0):
        return lambda b, p, n: (b * spb + n, base // LANES + mul * p + add)

    blk = (rows, LANES)
    return pl.pallas_call(
        functools.partial(_gla_body, nc),
        grid=(bsz, npair, spb),
        in_specs=[pl.BlockSpec(blk, col(OD_QD, 1)),
                  pl.BlockSpec(blk, col(OD_KD, 1)),
                  pl.BlockSpec(blk, col(OD_GD)),
                  pl.BlockSpec(blk, col(OD_VD, 2, 0)), pl.BlockSpec(blk, col(OD_VD, 2, 1)),
                  pl.BlockSpec(blk, col(OD_RD, 2, 0)), pl.BlockSpec(blk, col(OD_RD, 2, 1)),
                  pl.BlockSpec((LANES, LANES), lambda b, p, n: (0, p)),
                  pl.BlockSpec((1, LANES), lambda b, p, n: (0, p)),
                  pl.BlockSpec((1, GLA_DV), lambda b, p, n: (0, 0))],
        out_specs=pl.BlockSpec((rows, 2 * GLA_DV), lambda b, p, n: (b * spb + n, p)),
        out_shape=jax.ShapeDtypeStruct((t, GLA_HEADS * GLA_DV), F32),
        scratch_shapes=[pltpu.VMEM((2, GLA_DV, LANES), F32)],
        compiler_params=_cparams(("parallel", "parallel", "arbitrary")),
        name="gla",
    )(proj, proj, proj, proj, proj, proj, proj, w_gate_pad, b_gate, norm_w)


def _ffn_body(tiles_per_seq, tc, x_ref, halo_ref, wu_ref, cw_ref, cb_ref, wd_ref, g_ref, b_ref, y_ref,
              act, scratch_g, scratch_v):
    i = pl.program_id(0)
    x = x_ref[...]
    xb = x.astype(BF16)
    hb = jnp.where(i % tiles_per_seq == 0, 0.0, halo_ref[...]).astype(BF16)

    def branch(lo, scratch):
        w = wu_ref[:, lo:lo + tc]
        cur = jnp.dot(xb, w, preferred_element_type=F32)
        halo = jnp.dot(hb, w, preferred_element_type=F32)
        return _causal_conv(halo, cur, cw_ref[:, lo:lo + tc], scratch) + cb_ref[:, lo:lo + tc]

    for c in range(D_FF // tc):
        gate = branch(c * tc, scratch_g)
        val = branch(D_FF + c * tc, scratch_v)
        act[:, c * tc:(c + 1) * tc] = (_silu(gate) * val).astype(BF16)
    ffn = jnp.dot(act[...], wd_ref[...], preferred_element_type=F32)
    y_ref[...] = _layer_norm(DEEPNORM_ALPHA * x + ffn, g_ref[...], b_ref[...])


def _ffn(x, w_up, conv_w, conv_b, w_down, g, b, seq, tm, tc):
    t = x.shape[0]
    hb = tm // SUBLANES

    def resident(a):
        return pl.BlockSpec(a.shape, lambda i: (0, 0), pipeline_mode=pl.Buffered(1))

    return pl.pallas_call(
        functools.partial(_ffn_body, seq // tm, tc),
        grid=(t // tm,),
        in_specs=[pl.BlockSpec((tm, D_MODEL), lambda i: (i, 0)),
                  pl.BlockSpec((SUBLANES, D_MODEL), lambda i: (jnp.maximum(i * hb - 1, 0), 0)),
                  resident(w_up), resident(conv_w), resident(conv_b), resident(w_down),
                  resident(g), resident(b)],
        out_specs=pl.BlockSpec((tm, D_MODEL), lambda i: (i, 0)),
        out_shape=jax.ShapeDtypeStruct((t, D_MODEL), F32),
        scratch_shapes=[pltpu.VMEM((tm, D_FF), BF16), pltpu.VMEM((tm + SUBLANES, tc), F32),
                        pltpu.VMEM((tm + SUBLANES, tc), F32)],
        compiler_params=_cparams(("parallel",)),
        name="conv_ffn",
    )(x, x, w_up, conv_w, conv_b, w_down, g, b)


def _even_w_in(w):
    qk, v = GDN_HEADS * GDN_D, GDN_HEADS * GDN_D
    a_end = 2 * qk + 2 * v
    gates = w[:, a_end:a_end + 2 * GDN_HEADS]
    qkv_b = w[:, a_end + 2 * GDN_HEADS:]
    pad = jnp.zeros((w.shape[0], EV_COLS - EV_BA - 2 * GDN_HEADS), w.dtype)
    return jnp.concatenate([w[:, :a_end], qkv_b, gates, pad], axis=1).astype(BF16)


def _odd_w_in(w):
    pad = jnp.zeros((w.shape[0], OD_COLS - w.shape[1]), w.dtype)
    return jnp.concatenate([w, pad], axis=1).astype(BF16)


def _even_layer(h, rel_bias, w_in, conv_w, a_log, dt_bias, norm_w, w_out, ln_g, ln_b, bsz, seq):
    proj = _even_in_proj(h, _even_w_in(w_in), conv_w, seq, 1024)
    qkv = proj
    gate_pad = jnp.zeros((1, LANES), F32)
    alog_pad = lax.dynamic_update_slice(gate_pad, a_log[None].astype(F32), (0, GDN_HEADS))
    dtb_pad = lax.dynamic_update_slice(gate_pad, dt_bias[None].astype(F32), (0, GDN_HEADS))
    o_a = _gdn(qkv, proj, alog_pad, dtb_pad, norm_w[None], bsz, seq)
    tiles = _bias_tiles(rel_bias, *_swa_bias_tables())
    ng, two_blk = len(SWA_CONFIGS), 2 * SWA_BLOCK
    bias = tiles.reshape(SWA_HEADS // 2, 2, ng, SWA_BLOCK, two_blk).transpose(2, 0, 1, 3, 4)
    o_b = _swa(proj, bias.reshape(ng, SWA_HEADS // 2, two_blk, two_blk), bsz, seq)
    gdn_v = GDN_HEADS * GDN_D
    w_out = w_out.astype(BF16)
    return _out_proj(_out_body, [o_a, o_b], [w_out[:gdn_v], w_out[gdn_v:]],
                     h, ln_g[None], ln_b[None], 512, "even_out")


def _odd_layer(h, rel_bias, w_in, lam_params, diff_norm_w, w_gate, b_gate, gla_norm_w, w_out,
               ln_g, ln_b, lam_init, bsz, seq):
    proj = _matmul(h, _odd_w_in(w_in), 1024, 640)
    blk = min(DIFF_BLOCK, seq)
    bucket, neg = _diff_bias_tables(blk)
    bias = _bias_tiles(rel_bias, bucket, neg, scale=LOG2E, base_bucket=NUM_BUCKETS - 1)
    o_c = _diff_attention(proj, bias, lam_params, diff_norm_w[None], lam_init, bsz, seq, blk)
    w_gate_pad = jnp.concatenate(
        [w_gate, jnp.zeros((LANES - GLA_RANK, w_gate.shape[1]), w_gate.dtype)], axis=0)
    o_d = _gla(proj, w_gate_pad, b_gate[None], gla_norm_w[None], bsz, seq, 4)
    diff_v = DIFF_HEADS * 2 * DIFF_DH
    w_out = w_out.astype(BF16)
    return _out_proj(_out_body, [o_c, o_d], [w_out[:diff_v], w_out[diff_v:]],
                     h, ln_g[None], ln_b[None], 512, "odd_out")


def kernel(x, rel_bias, w_in_even, gdn_conv_w, gdn_a_log, gdn_dt_bias, gdn_norm_w, w_out_even,
           w_in_odd, diff_lambda, diff_norm_w, gla_w_gate, gla_b_gate, gla_norm_w, w_out_odd,
           ffn_w_up, ffn_conv_w, ffn_conv_b, ffn_w_down, ln_g, ln_b):
    bsz, seq, d = x.shape
    h = x.reshape(bsz * seq, d)
    for layer in range(DEPTH):
        i = layer // 2
        if layer % 2 == 0:
            h = _even_layer(h, rel_bias, w_in_even[i], gdn_conv_w[i], gdn_a_log[i], gdn_dt_bias[i],
                            gdn_norm_w[i], w_out_even[i], ln_g[layer, 0], ln_b[layer, 0], bsz, seq)
        else:
            lam_init = 0.8 - 0.6 * math.exp(-0.3 * layer)
            h = _odd_layer(h, rel_bias, w_in_odd[i], diff_lambda[i], diff_norm_w[i], gla_w_gate[i],
                           gla_b_gate[i], gla_norm_w[i], w_out_odd[i], ln_g[layer, 0], ln_b[layer, 0],
                           lam_init, bsz, seq)
        h = _ffn(h, ffn_w_up[layer].astype(BF16), ffn_conv_w[layer], ffn_conv_b[layer][None],
                 ffn_w_down[layer].astype(BF16), ln_g[layer, 1][None], ln_b[layer, 1][None],
                 seq, 512, 256)
    return h.reshape(bsz, seq, d)
```

```python
import functools
import math

import numpy as np
import jax
import jax.numpy as jnp
from jax import lax
from jax.experimental import pallas as pl
from jax.experimental.pallas import tpu as pltpu

F32 = jnp.float32
BF16 = jnp.bfloat16
HI = lax.Precision.HIGHEST

D_MODEL = 1024
DEPTH = 2
DEEPNORM_ALPHA = (2 * DEPTH) ** 0.25
LN_EPS = 1e-5
RMS_EPS = 1e-6
NUM_BUCKETS = 32
REL_MAX_DIST = 2048
GDN_HEADS = 6
GDN_D = 128
GDN_CONV = 4
CHUNK = 64
GDN_GROUP = 256
SWA_CONFIGS = ((128, 1), (512, 4), (2048, 16))
SWA_HEADS = 4
SWA_DH = 64
SWA_BLOCK = 128
SWA_UNROLL = 4
SWA_SPAN = SWA_BLOCK * max(d for _, d in SWA_CONFIGS)
DIFF_HEADS = 4
DIFF_DH = 64
DIFF_BLOCK = 512
LOG2E = math.log2(math.e)
GLA_HEADS = 4
GLA_DK = 64
GLA_DV = 128
GLA_RANK = 16
GLA_TAU = 16.0
GLA_SUB = 16
D_FF = 2816
FFN_CONV = 3

LANES = 128
SUBLANES = 8
VMEM_LIMIT = 56 * 1024 * 1024
NEG_BIG = -1e30

EVA_PART = GDN_HEADS * GDN_D
EVB_BA = 2304
EVB_COLS = 2560
EVB_TN = 512
OD_QC, OD_KC, OD_VC = 0, 512, 1024
OD_QD, OD_KD, OD_VD, OD_RD = 1536, 1792, 2048, 2560
OD_COLS = 3072
OD_TN = 768
PROJ_TM = 1024


def _cparams(sem):
    return pltpu.CompilerParams(dimension_semantics=sem, vmem_limit_bytes=VMEM_LIMIT)


def _bdot(a, b):
    return jnp.dot(a.astype(BF16), b.astype(BF16), preferred_element_type=F32)


def _bdot_nt(a, b):
    return lax.dot_general(a.astype(BF16), b.astype(BF16), (((1,), (1,)), ((), ())),
                           preferred_element_type=F32)


def _bdot_tn(a, b):
    return lax.dot_general(a.astype(BF16), b.astype(BF16), (((0,), (0,)), ((), ())),
                           preferred_element_type=F32)


def _sigmoid(x):
    return 1.0 / (1.0 + jnp.exp(-x))


def _silu(x):
    return x * _sigmoid(x)


def _softplus(x):
    return jnp.maximum(x, 0.0) + jnp.log1p(jnp.exp(-jnp.abs(x)))


def _log_sigmoid(x):
    return -_softplus(-x)


def _mm_body(x_ref, w_ref, o_ref, xb):
    @pl.when(pl.program_id(1) == 0)
    def _():
        xb[...] = x_ref[...].astype(BF16)

    o_ref[...] = jnp.dot(xb[...], w_ref[...], preferred_element_type=F32).astype(o_ref.dtype)


def _matmul(x, w, tm, tn, out_dtype, name):
    m, k = x.shape
    n = w.shape[1]
    return pl.pallas_call(
        _mm_body,
        grid=(m // tm, n // tn),
        in_specs=[pl.BlockSpec((tm, k), lambda i, j: (i, 0)),
                  pl.BlockSpec((k, tn), lambda i, j: (0, j))],
        out_specs=pl.BlockSpec((tm, tn), lambda i, j: (i, j)),
        out_shape=jax.ShapeDtypeStruct((m, n), out_dtype),
        scratch_shapes=[pltpu.VMEM((tm, k), BF16)],
        compiler_params=_cparams(("parallel", "arbitrary")),
        name=name,
    )(x, w)


def _rel_bucket_np(dist):
    max_exact = NUM_BUCKETS // 2
    d = np.maximum(dist, 1).astype(np.float32)
    large = max_exact + (np.log(d / max_exact) / math.log(REL_MAX_DIST / max_exact)
                         * (NUM_BUCKETS - max_exact)).astype(np.int32)
    large = np.minimum(large, NUM_BUCKETS - 1)
    return np.where(dist < max_exact, dist, large).astype(np.int32)


def _bias_body(scale, base_bucket, rb_ref, bucket_ref, neg_ref, o_ref):
    h = pl.program_id(0)
    bucket = bucket_ref[0]
    base = 0.0 if base_bucket is None else rb_ref[base_bucket, h]
    acc = neg_ref[0]
    for b in range(NUM_BUCKETS):
        acc = acc + jnp.where(bucket == b, (rb_ref[b, h] - base) * scale, 0.0)
    o_ref[0, 0] = acc


def _bias_tiles(rel_bias, bucket, neg, scale=1.0, base_bucket=None):
    nt, r, c = bucket.shape
    nh = rel_bias.shape[1]
    return pl.pallas_call(
        functools.partial(_bias_body, scale, base_bucket),
        grid=(nh, nt),
        in_specs=[pl.BlockSpec(memory_space=pltpu.SMEM),
                  pl.BlockSpec((1, r, c), lambda h, t: (t, 0, 0)),
                  pl.BlockSpec((1, r, c), lambda h, t: (t, 0, 0))],
        out_specs=pl.BlockSpec((1, 1, r, c), lambda h, t: (h, t, 0, 0)),
        out_shape=jax.ShapeDtypeStruct((nh, nt, r, c), F32),
        compiler_params=_cparams(("parallel", "parallel")),
        name="rel_bias_tiles",
    )(rel_bias, jnp.asarray(bucket), jnp.asarray(neg))


def _causal_conv(halo, cur, w, scratch):
    width = w.shape[0]
    rows = cur.shape[0]
    scratch[0:SUBLANES, :] = halo
    scratch[SUBLANES:SUBLANES + rows, :] = cur
    y = w[width - 1:width, :] * cur
    for j in range(width - 1):
        back = width - 1 - j
        y = y + w[j:j + 1, :] * scratch[SUBLANES - back:SUBLANES - back + rows, :]
    return y


def _even_in_body(tiles_per_seq, x_ref, halo_ref, w_ref, cw_ref, o_ref, xb, hb, scratch):
    i = pl.program_id(0)
    j = pl.program_id(1)

    @pl.when(j == 0)
    def _():
        xb[...] = x_ref[...].astype(BF16)
        hb[...] = jnp.where(i % tiles_per_seq == 0, 0.0, halo_ref[...]).astype(BF16)

    y = jnp.dot(xb[...], w_ref[...], preferred_element_type=F32)

    @pl.when(j < 3)
    def _():
        halo = jnp.dot(hb[...], w_ref[...], preferred_element_type=F32)
        c = _silu(_causal_conv(halo, y, cw_ref[...], scratch))
        qk_scale = jnp.where(j == 0, GDN_D ** -0.5, 1.0)
        for hd in range(GDN_HEADS):
            ch = c[:, hd * GDN_D:(hd + 1) * GDN_D]
            inv = lax.rsqrt(jnp.sum(ch * ch, axis=-1, keepdims=True) + RMS_EPS) * qk_scale
            o_ref[:, hd * GDN_D:(hd + 1) * GDN_D] = (ch * jnp.where(j < 2, inv, 1.0)).astype(o_ref.dtype)

    @pl.when(j >= 3)
    def _():
        o_ref[...] = y.astype(o_ref.dtype)


def _even_in_proj(x, w, conv_w, seq, tm):
    t, k = x.shape
    n = w.shape[1]
    tn = EVA_PART
    hb = tm // SUBLANES
    return pl.pallas_call(
        functools.partial(_even_in_body, seq // tm),
        grid=(t // tm, n // tn),
        in_specs=[pl.BlockSpec((tm, k), lambda i, j: (i, 0)),
                  pl.BlockSpec((SUBLANES, k), lambda i, j: (jnp.maximum(i * hb - 1, 0), 0)),
                  pl.BlockSpec((k, tn), lambda i, j: (0, j)),
                  pl.BlockSpec((GDN_CONV, tn), lambda i, j: (0, jnp.minimum(j, 2)))],
        out_specs=pl.BlockSpec((tm, tn), lambda i, j: (i, j)),
        out_shape=jax.ShapeDtypeStruct((t, n), BF16),
        scratch_shapes=[pltpu.VMEM((tm, k), BF16), pltpu.VMEM((SUBLANES, k), BF16),
                        pltpu.VMEM((tm + SUBLANES, tn), F32)],
        compiler_params=_cparams(("parallel", "arbitrary")),
        name="even_in_proj",
    )(x, x, w, conv_w)


def _gdn_body(q_ref, k_ref, v_ref, z_ref, ba_ref, alog_ref, dtb_ref, nw_ref, o_ref, state):
    n = pl.program_id(1)

    @pl.when(n == 0)
    def _():
        state[...] = jnp.zeros_like(state)

    grp = GDN_GROUP
    nc = grp // CHUNK
    nh = GDN_HEADS
    ba = ba_ref[...]
    beta_all = _sigmoid(ba)
    g_all = -jnp.exp(alog_ref[...]) * _softplus(ba + dtb_ref[...])

    ri = lax.broadcasted_iota(jnp.int32, (grp, grp), 0)
    ci = lax.broadcasted_iota(jnp.int32, (grp, grp), 1)
    same = (ri // CHUNK) == (ci // CHUNK)
    incl = same & (ri >= ci)
    strict = same & (ri > ci)
    eye = (ri == ci).astype(F32)
    tri = incl.astype(BF16)
    g_hi = g_all.astype(BF16)
    rem = g_all - g_hi.astype(F32)
    g_mid = rem.astype(BF16)
    g_lo = (rem - g_mid.astype(F32)).astype(BF16)
    gam_all = (jnp.dot(tri, g_hi, preferred_element_type=F32) + jnp.dot(tri, g_mid, preferred_element_type=F32)
               + jnp.dot(tri, g_lo, preferred_element_type=F32))
    gam_rows = gam_all.T

    hs = range(nh)
    q = [q_ref[:, h * GDN_D:(h + 1) * GDN_D] for h in hs]
    k = [k_ref[:, h * GDN_D:(h + 1) * GDN_D] for h in hs]
    v = [v_ref[:, h * GDN_D:(h + 1) * GDN_D] for h in hs]
    gam = [jnp.broadcast_to(gam_all[:, nh + h:nh + h + 1], (grp, GDN_D)) for h in hs]
    bcol = [beta_all[:, h:h + 1] for h in hs]
    decay, x, inv = [], [], []
    for h in hs:
        diff = gam[h][:, 0:1] - gam_rows[nh + h:nh + h + 1, :]
        decay.append(jnp.where(incl, jnp.exp(jnp.where(incl, diff, 0.0)), 0.0))
        kk = _bdot_nt(k[h], k[h])
        x.append(-jnp.where(strict, bcol[h] * kk * decay[h], 0.0))
        inv.append(eye + x[h])
    for _ in range(5):
        for h in hs:
            x[h] = _bdot(x[h], x[h])
            inv[h] = inv[h] + _bdot(inv[h], x[h])
    u, w, qk, q_dec, k_dec, g_last = [], [], [], [], [], []
    for h in hs:
        eg = jnp.exp(gam[h])
        uw = _bdot(inv[h], jnp.concatenate([v[h] * bcol[h], k[h] * (bcol[h] * eg)], axis=1))
        u.append(uw[:, :GDN_D])
        w.append(uw[:, GDN_D:])
        qk.append(_bdot_nt(q[h], k[h]) * decay[h])
        q_dec.append(q[h] * eg)
        kd, gl = [], []
        for c in range(nc):
            last = gam[h][(c + 1) * CHUNK - 1:(c + 1) * CHUNK, :]
            kd.append(k[h][c * CHUNK:(c + 1) * CHUNK] * jnp.exp(last - gam[h][c * CHUNK:(c + 1) * CHUNK]))
            gl.append(jnp.exp(last))
        k_dec.append(kd)
        g_last.append(gl)

    s = [state[h] for h in hs]
    q_s = [[] for _ in hs]
    delta = [[] for _ in hs]
    for c in range(nc):
        sl = slice(c * CHUNK, (c + 1) * CHUNK)
        for h in hs:
            r = _bdot(jnp.concatenate([w[h][sl], q_dec[h][sl]], axis=0), s[h])
            d = u[h][sl] - r[:CHUNK]
            q_s[h].append(r[CHUNK:])
            delta[h].append(d)
            s[h] = g_last[h][c] * s[h] + _bdot_tn(k_dec[h][c], d)
    for h in hs:
        state[h] = s[h]
        o = jnp.concatenate(q_s[h], axis=0) + _bdot(qk[h], jnp.concatenate(delta[h], axis=0))
        o = o * lax.rsqrt(jnp.mean(o * o, axis=-1, keepdims=True) + RMS_EPS) * nw_ref[...]
        z = z_ref[:, h * GDN_D:(h + 1) * GDN_D].astype(F32)
        o_ref[:, h * GDN_D:(h + 1) * GDN_D] = (o * _silu(z)).astype(o_ref.dtype)


def _gdn(proj_a, proj_b, alog_pad, dtb_pad, norm_w, bsz, seq):
    t = proj_a.shape[0]
    rows = GDN_GROUP
    spb = seq // rows
    width = GDN_HEADS * GDN_D

    def at(col):
        return lambda b, n: (b * spb + n, col)

    return pl.pallas_call(
        _gdn_body,
        grid=(bsz, spb),
        in_specs=[pl.BlockSpec((rows, width), at(0)),
                  pl.BlockSpec((rows, width), at(1)),
                  pl.BlockSpec((rows, width), at(2)),
                  pl.BlockSpec((rows, width), at(3)),
                  pl.BlockSpec((rows, LANES), at(EVB_BA // LANES)),
                  pl.BlockSpec((1, LANES), lambda b, n: (0, 0)),
                  pl.BlockSpec((1, LANES), lambda b, n: (0, 0)),
                  pl.BlockSpec((1, LANES), lambda b, n: (0, 0))],
        out_specs=pl.BlockSpec((rows, width), at(0)),
        out_shape=jax.ShapeDtypeStruct((t, width), BF16),
        scratch_shapes=[pltpu.VMEM((GDN_HEADS, GDN_D, GDN_D), F32)],
        compiler_params=_cparams(("parallel", "arbitrary")),
        name="gdn",
    )(proj_a, proj_a, proj_a, proj_a, proj_b, alog_pad, dtb_pad, norm_w)


def _swa_bias_tables():
    qi = np.arange(SWA_BLOCK)[:, None] + SWA_BLOCK
    kj = np.arange(2 * SWA_BLOCK)[None, :]
    rel = qi - kj
    buckets, negs = [], []
    for window, dilation in SWA_CONFIGS:
        buckets.append(_rel_bucket_np(np.maximum(rel, 0) * dilation))
        negs.append(np.where((rel >= 0) & (rel <= window // dilation), 0.0, NEG_BIG))
    return np.stack(buckets).astype(np.int32), np.stack(negs).astype(np.float32)


def _swa_body(*refs):
    ng = len(SWA_CONFIGS)
    ins, bias_ref, o_ref = refs[:5 * ng], refs[5 * ng], refs[5 * ng + 1]
    scratch = refs[5 * ng + 2:]
    kbufs, vbufs, o_scr, lse_scr = scratch[:ng], scratch[ng:2 * ng], scratch[2 * ng], scratch[2 * ng + 1]
    j = pl.program_id(1)
    blk = SWA_BLOCK
    first_head = lax.broadcasted_iota(jnp.int32, (blk, LANES), 1) < SWA_DH
    in_prev = lax.broadcasted_iota(jnp.int32, (2 * blk, 2 * blk), 1) < blk

    for g, (_, d) in enumerate(SWA_CONFIGS):
        q_ref, kc_ref, kp_ref, vc_ref, vp_ref = ins[5 * g:5 * g + 5]
        kbuf, vbuf = kbufs[g], vbufs[g]
        halo = blk * d
        kbuf[0:halo, :] = kp_ref[...]
        kbuf[halo:, :] = kc_ref[...]
        vbuf[0:halo, :] = vp_ref[...]
        vbuf[halo:, :] = vc_ref[...]

        def unit(u, carry, g=g, d=d, halo=halo, q_ref=q_ref, kbuf=kbuf, vbuf=vbuf):
            base = (u // d) * halo + u % d
            q = q_ref[pl.ds(base, blk, stride=d), :] * SWA_DH ** -0.5
            k = kbuf[pl.ds(base, 2 * blk, stride=d), :]
            v = vbuf[pl.ds(base, 2 * blk, stride=d), :]
            lhs = jnp.concatenate([jnp.where(first_head, q, 0.0), jnp.where(first_head, 0.0, q)], axis=0)
            s = _bdot_nt(lhs, k) + bias_ref[g, 0]
            no_prev = (j == 0) & (u < d)
            s = jnp.where(in_prev & no_prev, NEG_BIG, s)
            m = jnp.max(s, axis=-1, keepdims=True)
            p = jnp.exp(s - m)
            l = jnp.sum(p, axis=-1, keepdims=True)
            o2 = _bdot(p / l, v)
            lse2 = m + jnp.log(l)
            o_scr[g, pl.ds(base, blk, stride=d), :] = jnp.where(first_head, o2[:blk], o2[blk:])
            lse_scr[g, pl.ds(base, blk, stride=d), :] = jnp.where(first_head, lse2[:blk], lse2[blk:])
            return carry

        lax.fori_loop(0, SWA_SPAN // blk, unit, 0, unroll=SWA_UNROLL)

    def combine(c, carry):
        rows = pl.ds(pl.multiple_of(c * 2 * blk, 2 * blk), 2 * blk)
        lse = [lse_scr[g, rows, :] for g in range(ng)]
        m = functools.reduce(jnp.maximum, lse)
        e = [jnp.exp(x - m) for x in lse]
        den = functools.reduce(lambda x, y: x + y, e)
        o_ref[rows, :] = functools.reduce(
            lambda x, y: x + y, [(e[g] / den) * o_scr[g, rows, :] for g in range(ng)]).astype(o_ref.dtype)
        return carry

    lax.fori_loop(0, SWA_SPAN // (2 * blk), combine, 0)


def _swa(proj, bias, bsz, seq):
    t = proj.shape[0]
    ng = len(SWA_CONFIGS)
    nspan = seq // SWA_SPAN
    npair = SWA_HEADS * SWA_DH // LANES
    group_cols = SWA_HEADS * SWA_DH // LANES
    in_specs, scratch_k = [], []
    for g, (_, d) in enumerate(SWA_CONFIGS):
        halo = SWA_BLOCK * d
        per_span = SWA_SPAN // halo

        def cur(which, g=g):
            col = (which * ng + g) * group_cols
            return lambda b, j, p: (b * nspan + j, col + p)

        def prev(which, g=g, per_span=per_span):
            col = (which * ng + g) * group_cols
            return lambda b, j, p: (jnp.maximum((b * nspan + j) * per_span - 1, 0), col + p)

        in_specs += [pl.BlockSpec((SWA_SPAN, LANES), cur(0)),
                     pl.BlockSpec((SWA_SPAN, LANES), cur(1)), pl.BlockSpec((halo, LANES), prev(1)),
                     pl.BlockSpec((SWA_SPAN, LANES), cur(2)), pl.BlockSpec((halo, LANES), prev(2))]
        scratch_k.append(pltpu.VMEM((halo + SWA_SPAN, LANES), F32))
    in_specs.append(pl.BlockSpec((ng, 1, 2 * SWA_BLOCK, 2 * SWA_BLOCK), lambda b, j, p: (0, p, 0, 0)))
    return pl.pallas_call(
        _swa_body,
        grid=(bsz, nspan, npair),
        in_specs=in_specs,
        out_specs=pl.BlockSpec((SWA_SPAN, LANES), lambda b, j, p: (b * nspan + j, p)),
        out_shape=jax.ShapeDtypeStruct((t, npair * LANES), BF16),
        scratch_shapes=scratch_k + scratch_k + [pltpu.VMEM((ng, SWA_SPAN, LANES), F32)] * 2,
        compiler_params=_cparams(("parallel", "parallel", "parallel")),
        name="swa",
    )(*([proj] * (5 * ng)), bias)


def _layer_norm(y, g, b):
    mu = jnp.mean(y, axis=-1, keepdims=True)
    yc = y - mu
    var = jnp.mean(yc * yc, axis=-1, keepdims=True)
    return yc * lax.rsqrt(var + LN_EPS) * g + b


def _out_body(xa_ref, xb_ref, wa_ref, wb_ref, h_ref, g_ref, b_ref, y_ref):
    mix = _bdot(xa_ref[...], wa_ref[...]) + _bdot(xb_ref[...], wb_ref[...])
    y_ref[...] = _layer_norm(DEEPNORM_ALPHA * h_ref[...] + mix, g_ref[...], b_ref[...])


def _out_proj(body, acts, weights, h, g, b, tm, name):
    t = h.shape[0]

    def rowblk(a):
        return pl.BlockSpec((tm, a.shape[1]), lambda i: (i, 0))

    def full(a):
        return pl.BlockSpec(a.shape, lambda i: (0, 0))

    return pl.pallas_call(
        body,
        grid=(t // tm,),
        in_specs=[rowblk(a) for a in acts] + [full(w) for w in weights] + [rowblk(h), full(g), full(b)],
        out_specs=pl.BlockSpec((tm, D_MODEL), lambda i: (i, 0)),
        out_shape=jax.ShapeDtypeStruct((t, D_MODEL), F32),
        compiler_params=_cparams(("parallel",)),
        name=name,
    )(*acts, *weights, h, g, b)


def _diff_bias_tables(blk):
    buckets = _rel_bucket_np(np.arange(2 * REL_MAX_DIST))
    far = int(np.max(np.nonzero(buckets != NUM_BUCKETS - 1)[0])) + 1
    nb = -(-(far + blk - 1) // blk)
    kk = np.arange(blk)[:, None]
    qq = np.arange(blk)[None, :]
    dist = np.stack([t * blk + qq - kk for t in range(-1, nb + 1)])
    bucket = _rel_bucket_np(np.maximum(dist, 0))
    neg = np.where(dist >= 0, 0.0, NEG_BIG).astype(np.float32)
    return bucket, neg


def _diff_body(blk, nb, lam_init, q_ref, k_ref, v_ref, bias_ref, lam_ref, nw_ref, o_ref,
               vt, acc1, acc2, s_a, s_b):
    qi = pl.program_id(2)
    dv = 2 * DIFF_DH
    seq = k_ref.shape[0]

    @pl.when(qi == 0)
    def _():
        vt[dv:, :] = jnp.ones((vt.shape[0] - dv, seq), BF16)

        def fill(c, carry):
            st = pl.multiple_of(c * blk, blk)
            vt[0:dv, pl.ds(st, blk)] = v_ref[pl.ds(st, blk), :].astype(F32).T.astype(BF16)
            return carry

        lax.fori_loop(0, seq // blk, fill, 0)

    lane = lax.broadcasted_iota(jnp.int32, (blk, dv), 1)
    q = q_ref[...].astype(F32) * (DIFF_DH ** -0.5 * LOG2E)
    qs = (jnp.where(lane < DIFF_DH, q, 0.0).astype(BF16), jnp.where(lane >= DIFF_DH, q, 0.0).astype(BF16))
    accs = (acc1, acc2)
    for acc in accs:
        acc[...] = jnp.zeros_like(acc)

    last = pl.num_programs(2) - 1

    def key_rows(kj):
        return pl.ds(pl.multiple_of(jnp.minimum(kj, last) * blk, blk), blk)

    def scores(kj, dst):
        k = k_ref[key_rows(kj), :]
        for c, qc in enumerate(qs):
            dst[c] = lax.dot_general(k, qc, (((1,), (1,)), ((), ())), preferred_element_type=F32)

    def consume(kj, src, ms):
        vtb = vt[:, key_rows(kj)]
        bias = bias_ref[0, jnp.minimum(qi - kj, nb) + 1]
        out = []
        for c, (m, acc) in enumerate(zip(ms, accs)):
            s = src[c] + bias
            m_new = jnp.maximum(m, jnp.max(s, axis=0, keepdims=True))
            p = jnp.exp2(s - m_new).astype(BF16)
            acc[...] = jnp.exp2(m - m_new) * acc[...] + jnp.dot(vtb, p, preferred_element_type=F32)
            out.append(m_new)
        return tuple(out)

    def pair(t, ms):
        kj = 2 * t
        scores(kj + 1, s_b)
        ms = consume(kj, s_a, ms)
        scores(kj + 2, s_a)
        return consume(kj + 1, s_b, ms)

    m0 = jnp.full((1, blk), NEG_BIG, F32)
    scores(0, s_a)
    lax.fori_loop(0, (qi + 2) // 2, pair, (m0, m0))

    lp = lam_ref[...]
    lam = (jnp.exp(jnp.sum(lp[0:1] * lp[1:2], axis=-1, keepdims=True))
           - jnp.exp(jnp.sum(lp[2:3] * lp[3:4], axis=-1, keepdims=True)) + lam_init)
    a1, a2 = acc1[...], acc2[...]
    o_t = a1[:dv] / a1[dv:dv + 1] - lam * (a2[:dv] / a2[dv:dv + 1])
    o = o_t.T
    o = o * lax.rsqrt(jnp.mean(o * o, axis=-1, keepdims=True) + RMS_EPS) * nw_ref[...]
    o_ref[...] = (o * (1.0 - lam_init)).astype(o_ref.dtype)


def _diff_attention(proj, bias, lam_params, norm_w, lam_init, bsz, seq, blk):
    t = proj.shape[0]
    nq = seq // blk
    nt = bias.shape[1]
    nb = nt - 2
    width = 2 * DIFF_DH
    ones_rows = 2 * SUBLANES
    return pl.pallas_call(
        functools.partial(_diff_body, blk, nb, lam_init),
        grid=(bsz, DIFF_HEADS, nq),
        in_specs=[pl.BlockSpec((blk, width), lambda b, h, i: (b * nq + i, OD_QC // width + h)),
                  pl.BlockSpec((seq, width), lambda b, h, i: (b, OD_KC // width + h)),
                  pl.BlockSpec((seq, width), lambda b, h, i: (b, OD_VC // width + h)),
                  pl.BlockSpec((1, nt, blk, blk), lambda b, h, i: (h, 0, 0, 0)),
                  pl.BlockSpec((4, DIFF_DH), lambda b, h, i: (0, 0)),
                  pl.BlockSpec((1, width), lambda b, h, i: (0, 0))],
        out_specs=pl.BlockSpec((blk, width), lambda b, h, i: (b * nq + i, h)),
        out_shape=jax.ShapeDtypeStruct((t, DIFF_HEADS * width), BF16),
        scratch_shapes=[pltpu.VMEM((width + ones_rows, seq), BF16),
                        pltpu.VMEM((width + ones_rows, blk), F32), pltpu.VMEM((width + ones_rows, blk), F32),
                        pltpu.VMEM((2, blk, blk), F32), pltpu.VMEM((2, blk, blk), F32)],
        compiler_params=_cparams(("parallel", "parallel", "arbitrary")),
        name="diff_attn",
    )(proj, proj, proj, bias, lam_params, norm_w)


def _gla_body(nc, q_ref, k_ref, gd_ref, va_ref, vb_ref, ra_ref, rb_ref, wg_ref, bg_ref, nw_ref,
              o_ref, state):
    n = pl.program_id(2)

    @pl.when(n == 0)
    def _():
        state[...] = jnp.zeros_like(state)

    rows = nc * CHUNK
    nsub = rows // GLA_SUB
    per_chunk = CHUNK // GLA_SUB
    gate = jnp.dot(gd_ref[...], wg_ref[...], precision=HI, preferred_element_type=F32) + bg_ref[...]
    log_a = _log_sigmoid(gate) * (1.0 / GLA_TAU)
    q = q_ref[...].astype(F32) * GLA_DK ** -0.5
    k = k_ref[...].astype(F32)
    vs = (va_ref[...].astype(F32), vb_ref[...].astype(F32))
    lane_s = lax.broadcasted_iota(jnp.int32, (GLA_SUB, LANES), 1)
    lane_c = lax.broadcasted_iota(jnp.int32, (CHUNK, LANES), 1)
    sub_mask = (lane_s < GLA_DK, lane_s >= GLA_DK)
    chunk_mask = (lane_c < GLA_DK, lane_c >= GLA_DK)

    ri = lax.broadcasted_iota(jnp.int32, (CHUNK, CHUNK), 0)
    ci = lax.broadcasted_iota(jnp.int32, (CHUNK, CHUNK), 1)
    tri = (ri >= ci).astype(F32)
    b = jnp.concatenate(
        [jnp.dot(tri, log_a[c * CHUNK:(c + 1) * CHUNK], precision=HI, preferred_element_type=F32)
         for c in range(nc)], axis=0)

    b3 = b.reshape(nsub, GLA_SUB, LANES)
    q3 = q.reshape(nsub, GLA_SUB, LANES)
    k3 = k.reshape(nsub, GLA_SUB, LANES)
    v3 = [v.reshape(nsub, GLA_SUB, LANES) for v in vs]
    row3 = lax.broadcasted_iota(jnp.int32, (nsub, GLA_SUB, LANES), 1)
    lane3 = lax.broadcasted_iota(jnp.int32, (nsub, GLA_SUB, LANES), 2)
    rowc = lax.broadcasted_iota(jnp.int32, (nsub, GLA_SUB, 1), 1)
    o3 = [jnp.zeros((nsub, GLA_SUB, LANES), F32) for _ in vs]
    for jj in range(GLA_SUB):
        e = jnp.exp(jnp.where(row3 >= jj, b3 - b3[:, jj:jj + 1, :], 0.0))
        t = q3 * k3[:, jj:jj + 1, :] * e
        w_all = jnp.sum(t, axis=-1, keepdims=True)
        w_a = jnp.sum(jnp.where(lane3 < GLA_DK, t, 0.0), axis=-1, keepdims=True)
        for hd, w in enumerate((w_a, w_all - w_a)):
            o3[hd] = o3[hd] + jnp.where(rowc >= jj, w, 0.0) * v3[hd][:, jj:jj + 1, :]
    o_diag = [o.reshape(rows, LANES) for o in o3]

    kcol = lax.broadcasted_iota(jnp.int32, (GLA_SUB, CHUNK), 1)
    outs = ([], [])
    st = [state[0], state[1]]
    for c in range(nc):
        sl = slice(c * CHUNK, (c + 1) * CHUNK)
        bc, qc, kc = b[sl], q[sl], k[sl]
        a_rows = [[jnp.zeros((GLA_SUB, CHUNK), F32)] for _ in vs]
        for blk in range(1, per_chunk):
            r0 = blk * GLA_SUB
            bref = bc[r0:r0 + 1]
            qs = qc[r0:r0 + GLA_SUB] * jnp.exp(bc[r0:r0 + GLA_SUB] - bref)
            ks = kc * jnp.exp(jnp.minimum(bref - bc, 0.0))
            for hd in range(2):
                a = _bdot_nt(jnp.where(sub_mask[hd], qs, 0.0), ks)
                a_rows[hd].append(jnp.where(kcol < r0, a, 0.0))
        eb = jnp.exp(bc)
        b_last = bc[CHUNK - 1:CHUNK]
        k_dec = kc * jnp.exp(b_last - bc)
        e_last = jnp.exp(b_last)
        for hd in range(2):
            vc = vs[hd][sl]
            a_mat = jnp.concatenate(a_rows[hd], axis=0)
            q_dec = jnp.where(chunk_mask[hd], qc * eb, 0.0)
            outs[hd].append(_bdot(a_mat, vc) + _bdot_nt(q_dec, st[hd]))
            st[hd] = st[hd] * e_last + _bdot_tn(vc, k_dec)
    state[0] = st[0]
    state[1] = st[1]

    res = []
    for hd, r_ref in enumerate((ra_ref, rb_ref)):
        o = jnp.concatenate(outs[hd], axis=0) + o_diag[hd]
        o = o * lax.rsqrt(jnp.mean(o * o, axis=-1, keepdims=True) + RMS_EPS) * nw_ref[...]
        res.append(o * _silu(r_ref[...].astype(F32)))
    o_ref[...] = jnp.concatenate(res, axis=-1).astype(o_ref.dtype)


def _gla(proj, gate_in, w_gate_pad, b_gate, norm_w, bsz, seq, nc):
    t = proj.shape[0]
    rows = nc * CHUNK
    spb = seq // rows
    npair = GLA_HEADS // 2

    def col(base, mul=0, add=0):
        return lambda b, p, n: (b * spb + n, base // LANES + mul * p + add)

    blk = (rows, LANES)
    return pl.pallas_call(
        functools.partial(_gla_body, nc),
        grid=(bsz, npair, spb),
        in_specs=[pl.BlockSpec(blk, col(OD_QD, 1)),
                  pl.BlockSpec(blk, col(OD_KD, 1)),
                  pl.BlockSpec(blk, lambda b, p, n: (b * spb + n, 0)),
                  pl.BlockSpec(blk, col(OD_VD, 2, 0)), pl.BlockSpec(blk, col(OD_VD, 2, 1)),
                  pl.BlockSpec(blk, col(OD_RD, 2, 0)), pl.BlockSpec(blk, col(OD_RD, 2, 1)),
                  pl.BlockSpec((LANES, LANES), lambda b, p, n: (0, p)),
                  pl.BlockSpec((1, LANES), lambda b, p, n: (0, p)),
                  pl.BlockSpec((1, GLA_DV), lambda b, p, n: (0, 0))],
        out_specs=pl.BlockSpec((rows, 2 * GLA_DV), lambda b, p, n: (b * spb + n, p)),
        out_shape=jax.ShapeDtypeStruct((t, GLA_HEADS * GLA_DV), BF16),
        scratch_shapes=[pltpu.VMEM((2, GLA_DV, LANES), F32)],
        compiler_params=_cparams(("parallel", "parallel", "arbitrary")),
        name="gla",
    )(proj, proj, gate_in, proj, proj, proj, proj, w_gate_pad, b_gate, norm_w)


def _ffn_body(tiles_per_seq, tc, x_ref, halo_ref, wu_ref, cw_ref, cb_ref, wd_ref, g_ref, b_ref, y_ref,
              act, scratch_g, scratch_v):
    i = pl.program_id(0)
    x = x_ref[...]
    xb = x.astype(BF16)
    hb = jnp.where(i % tiles_per_seq == 0, 0.0, halo_ref[...]).astype(BF16)

    def branch(lo, scratch):
        w = wu_ref[:, lo:lo + tc]
        cur = jnp.dot(xb, w, preferred_element_type=F32)
        halo = jnp.dot(hb, w, preferred_element_type=F32)
        return _causal_conv(halo, cur, cw_ref[:, lo:lo + tc], scratch) + cb_ref[:, lo:lo + tc]

    for c in range(D_FF // tc):
        gate = branch(c * tc, scratch_g)
        val = branch(D_FF + c * tc, scratch_v)
        act[:, c * tc:(c + 1) * tc] = (_silu(gate) * val).astype(BF16)
    ffn = jnp.dot(act[...], wd_ref[...], preferred_element_type=F32)
    y_ref[...] = _layer_norm(DEEPNORM_ALPHA * x + ffn, g_ref[...], b_ref[...])


def _ffn(x, w_up, conv_w, conv_b, w_down, g, b, seq, tm, tc):
    t = x.shape[0]
    hb = tm // SUBLANES

    def resident(a):
        return pl.BlockSpec(a.shape, lambda i: (0, 0), pipeline_mode=pl.Buffered(1))

    return pl.pallas_call(
        functools.partial(_ffn_body, seq // tm, tc),
        grid=(t // tm,),
        in_specs=[pl.BlockSpec((tm, D_MODEL), lambda i: (i, 0)),
                  pl.BlockSpec((SUBLANES, D_MODEL), lambda i: (jnp.maximum(i * hb - 1, 0), 0)),
                  resident(w_up), resident(conv_w), resident(conv_b), resident(w_down),
                  resident(g), resident(b)],
        out_specs=pl.BlockSpec((tm, D_MODEL), lambda i: (i, 0)),
        out_shape=jax.ShapeDtypeStruct((t, D_MODEL), F32),
        scratch_shapes=[pltpu.VMEM((tm, D_FF), BF16), pltpu.VMEM((tm + SUBLANES, tc), F32),
                        pltpu.VMEM((tm + SUBLANES, tc), F32)],
        compiler_params=_cparams(("parallel",)),
        name="conv_ffn",
    )(x, x, w_up, conv_w, conv_b, w_down, g, b)


def _even_w_in(w):
    a_end = 4 * EVA_PART
    gates = w[:, a_end:a_end + 2 * GDN_HEADS]
    qkv_b = w[:, a_end + 2 * GDN_HEADS:]
    pad = jnp.zeros((w.shape[0], EVB_COLS - EVB_BA - 2 * GDN_HEADS), w.dtype)
    return w[:, :a_end].astype(BF16), jnp.concatenate([qkv_b, gates, pad], axis=1).astype(BF16)


def _odd_w_in(w):
    pad = jnp.zeros((w.shape[0], LANES - GLA_RANK), w.dtype)
    return w[:, :OD_COLS].astype(BF16), jnp.concatenate([w[:, OD_COLS:], pad], axis=1).astype(BF16)


def _even_layer(h, rel_bias, w_in, conv_w, a_log, dt_bias, norm_w, w_out, ln_g, ln_b, bsz, seq):
    w_a, w_b = _even_w_in(w_in)
    proj_a = _even_in_proj(h, w_a, conv_w, seq, PROJ_TM)
    proj_b = _matmul(h, w_b, PROJ_TM, EVB_TN, F32, "even_in_proj_f32")
    gate_pad = jnp.zeros((1, LANES), F32)
    alog_pad = lax.dynamic_update_slice(gate_pad, a_log[None].astype(F32), (0, GDN_HEADS))
    dtb_pad = lax.dynamic_update_slice(gate_pad, dt_bias[None].astype(F32), (0, GDN_HEADS))
    o_a = _gdn(proj_a, proj_b, alog_pad, dtb_pad, norm_w[None], bsz, seq)
    tiles = _bias_tiles(rel_bias, *_swa_bias_tables())
    ng, two_blk = len(SWA_CONFIGS), 2 * SWA_BLOCK
    bias = tiles.reshape(SWA_HEADS // 2, 2, ng, SWA_BLOCK, two_blk).transpose(2, 0, 1, 3, 4)
    o_b = _swa(proj_b, bias.reshape(ng, SWA_HEADS // 2, two_blk, two_blk), bsz, seq)
    w_out = w_out.astype(BF16)
    return _out_proj(_out_body, [o_a, o_b], [w_out[:EVA_PART], w_out[EVA_PART:]],
                     h, ln_g[None], ln_b[None], 512, "even_out")


def _odd_layer(h, rel_bias, w_in, lam_params, diff_norm_w, w_gate, b_gate, gla_norm_w, w_out,
               ln_g, ln_b, lam_init, bsz, seq):
    w_main, w_gd = _odd_w_in(w_in)
    proj = _matmul(h, w_main, PROJ_TM, OD_TN, BF16, "odd_in_proj")
    gate_in = _matmul(h, w_gd, PROJ_TM, LANES, F32, "odd_gate_proj")
    blk = min(DIFF_BLOCK, seq)
    bucket, neg = _diff_bias_tables(blk)
    bias = _bias_tiles(rel_bias, bucket, neg, scale=LOG2E, base_bucket=NUM_BUCKETS - 1)
    o_c = _diff_attention(proj, bias, lam_params, diff_norm_w[None], lam_init, bsz, seq, blk)
    w_gate_pad = jnp.concatenate(
        [w_gate, jnp.zeros((LANES - GLA_RANK, w_gate.shape[1]), w_gate.dtype)], axis=0)
    o_d = _gla(proj, gate_in, w_gate_pad, b_gate[None], gla_norm_w[None], bsz, seq, 4)
    diff_v = DIFF_HEADS * 2 * DIFF_DH
    w_out = w_out.astype(BF16)
    return _out_proj(_out_body, [o_c, o_d], [w_out[:diff_v], w_out[diff_v:]],
                     h, ln_g[None], ln_b[None], 512, "odd_out")


def kernel(x, rel_bias, w_in_even, gdn_conv_w, gdn_a_log, gdn_dt_bias, gdn_norm_w, w_out_even,
           w_in_odd, diff_lambda, diff_norm_w, gla_w_gate, gla_b_gate, gla_norm_w, w_out_odd,
           ffn_w_up, ffn_conv_w, ffn_conv_b, ffn_w_down, ln_g, ln_b):
    bsz, seq, d = x.shape
    h = x.reshape(bsz * seq, d)
    for layer in range(DEPTH):
        i = layer // 2
        if layer % 2 == 0:
            h = _even_layer(h, rel_bias, w_in_even[i], gdn_conv_w[i], gdn_a_log[i], gdn_dt_bias[i],
                            gdn_norm_w[i], w_out_even[i], ln_g[layer, 0], ln_b[layer, 0], bsz, seq)
        else:
            lam_init = 0.8 - 0.6 * math.exp(-0.3 * layer)
            h = _odd_layer(h, rel_bias, w_in_odd[i], diff_lambda[i], diff_norm_w[i], gla_w_gate[i],
                           gla_b_gate[i], gla_norm_w[i], w_out_odd[i], ln_g[layer, 0], ln_b[layer, 0],
                           lam_init, bsz, seq)
        h = _ffn(h, ffn_w_up[layer].astype(BF16), ffn_conv_w[layer], ffn_conv_b[layer][None],
                 ffn_w_down[layer].astype(BF16), ln_g[layer, 1][None], ln_b[layer, 1][None],
                 seq, 512, 256)
    return h.reshape(bsz, seq, d)
```

```python
import functools
import math

import numpy as np
import jax
import jax.numpy as jnp
from jax import lax
from jax.experimental import pallas as pl
from jax.experimental.pallas import tpu as pltpu

F32 = jnp.float32
BF16 = jnp.bfloat16
HI = lax.Precision.HIGHEST

D_MODEL = 1024
DEPTH = 2
DEEPNORM_ALPHA = (2 * DEPTH) ** 0.25
LN_EPS = 1e-5
RMS_EPS = 1e-6
NUM_BUCKETS = 32
REL_MAX_DIST = 2048
GDN_HEADS = 6
GDN_D = 128
GDN_CONV = 4
CHUNK = 64
GDN_GROUP = 256
SWA_CONFIGS = ((128, 1), (512, 4), (2048, 16))
SWA_HEADS = 4
SWA_DH = 64
SWA_BLOCK = 128
SWA_UNROLL = 8
SWA_SPAN = SWA_BLOCK * max(d for _, d in SWA_CONFIGS)
DIFF_HEADS = 4
DIFF_DH = 64
DIFF_BLOCK = 512
LOG2E = math.log2(math.e)
GLA_HEADS = 4
GLA_DK = 64
GLA_DV = 128
GLA_RANK = 16
GLA_TAU = 16.0
GLA_SUB = 16
D_FF = 2816
FFN_CONV = 3

LANES = 128
SUBLANES = 8
HALO = 16
VMEM_LIMIT = 56 * 1024 * 1024
NEG_BIG = -1e30

EVA_PART = GDN_HEADS * GDN_D
EVB_BA = 2304
EVB_COLS = 2560
EVB_TN = 512
OD_QC, OD_KC, OD_VC = 0, 512, 1024
OD_QD, OD_KD, OD_VD, OD_RD = 1536, 1792, 2048, 2560
OD_COLS = 3072
OD_TN = 768
PROJ_TM = 1024


def _cparams(sem):
    return pltpu.CompilerParams(dimension_semantics=sem, vmem_limit_bytes=VMEM_LIMIT)


def _bdot(a, b):
    return jnp.dot(a.astype(BF16), b.astype(BF16), preferred_element_type=F32)


def _bdot_nt(a, b):
    return lax.dot_general(a.astype(BF16), b.astype(BF16), (((1,), (1,)), ((), ())),
                           preferred_element_type=F32)


def _bdot_tn(a, b):
    return lax.dot_general(a.astype(BF16), b.astype(BF16), (((0,), (0,)), ((), ())),
                           preferred_element_type=F32)


def _sigmoid(x):
    return 1.0 / (1.0 + jnp.exp(-x))


def _silu(x):
    return x * _sigmoid(x)


def _softplus(x):
    return jnp.maximum(x, 0.0) + jnp.log1p(jnp.exp(-jnp.abs(x)))


def _log_sigmoid(x):
    return -_softplus(-x)


def _mm_body(x_ref, w_ref, o_ref, xb):
    @pl.when(pl.program_id(1) == 0)
    def _():
        xb[...] = x_ref[...].astype(BF16)

    o_ref[...] = jnp.dot(xb[...], w_ref[...], preferred_element_type=F32).astype(o_ref.dtype)


def _matmul(x, w, tm, tn, out_dtype, name):
    m, k = x.shape
    n = w.shape[1]
    return pl.pallas_call(
        _mm_body,
        grid=(m // tm, n // tn),
        in_specs=[pl.BlockSpec((tm, k), lambda i, j: (i, 0)),
                  pl.BlockSpec((k, tn), lambda i, j: (0, j))],
        out_specs=pl.BlockSpec((tm, tn), lambda i, j: (i, j)),
        out_shape=jax.ShapeDtypeStruct((m, n), out_dtype),
        scratch_shapes=[pltpu.VMEM((tm, k), BF16)],
        compiler_params=_cparams(("parallel", "arbitrary")),
        name=name,
    )(x, w)


def _rel_bucket_np(dist):
    max_exact = NUM_BUCKETS // 2
    d = np.maximum(dist, 1).astype(np.float32)
    large = max_exact + (np.log(d / max_exact) / math.log(REL_MAX_DIST / max_exact)
                         * (NUM_BUCKETS - max_exact)).astype(np.int32)
    large = np.minimum(large, NUM_BUCKETS - 1)
    return np.where(dist < max_exact, dist, large).astype(np.int32)


def _bias_body(scale, base_bucket, rb_ref, bucket_ref, neg_ref, o_ref):
    h = pl.program_id(0)
    bucket = bucket_ref[0]
    base = 0.0 if base_bucket is None else rb_ref[base_bucket, h]
    acc = neg_ref[0]
    for b in range(NUM_BUCKETS):
        acc = acc + jnp.where(bucket == b, (rb_ref[b, h] - base) * scale, 0.0)
    o_ref[0, 0] = acc


def _bias_tiles(rel_bias, bucket, neg, scale=1.0, base_bucket=None):
    nt, r, c = bucket.shape
    nh = rel_bias.shape[1]
    return pl.pallas_call(
        functools.partial(_bias_body, scale, base_bucket),
        grid=(nh, nt),
        in_specs=[pl.BlockSpec(memory_space=pltpu.SMEM),
                  pl.BlockSpec((1, r, c), lambda h, t: (t, 0, 0)),
                  pl.BlockSpec((1, r, c), lambda h, t: (t, 0, 0))],
        out_specs=pl.BlockSpec((1, 1, r, c), lambda h, t: (h, t, 0, 0)),
        out_shape=jax.ShapeDtypeStruct((nh, nt, r, c), F32),
        compiler_params=_cparams(("parallel", "parallel")),
        name="rel_bias_tiles",
    )(rel_bias, jnp.asarray(bucket), jnp.asarray(neg))


def _causal_conv(full, w, scratch):
    width = w.shape[0]
    rows = full.shape[0] - HALO
    scratch[...] = full
    y = w[width - 1:width, :] * full[HALO:]
    for j in range(width - 1):
        back = width - 1 - j
        y = y + w[j:j + 1, :] * scratch[HALO - back:HALO - back + rows, :]
    return y


def _even_in_body(tiles_per_seq, x_ref, halo_ref, w_ref, cw_ref, o_ref, xb, scratch):
    i = pl.program_id(0)
    j = pl.program_id(1)

    @pl.when(j == 0)
    def _():
        xb[0:HALO, :] = jnp.where(i % tiles_per_seq == 0, 0.0, halo_ref[...]).astype(BF16)
        xb[HALO:, :] = x_ref[...].astype(BF16)

    y = jnp.dot(xb[...], w_ref[...], preferred_element_type=F32)

    @pl.when(j < 3)
    def _():
        c = _silu(_causal_conv(y, cw_ref[...], scratch))
        qk_scale = jnp.where(j == 0, GDN_D ** -0.5, 1.0)
        for hd in range(GDN_HEADS):
            ch = c[:, hd * GDN_D:(hd + 1) * GDN_D]
            inv = lax.rsqrt(jnp.sum(ch * ch, axis=-1, keepdims=True) + RMS_EPS) * qk_scale
            o_ref[:, hd * GDN_D:(hd + 1) * GDN_D] = (ch * jnp.where(j < 2, inv, 1.0)).astype(o_ref.dtype)

    @pl.when(j >= 3)
    def _():
        o_ref[...] = y[HALO:].astype(o_ref.dtype)


def _even_in_proj(x, w, conv_w, seq, tm):
    t, k = x.shape
    n = w.shape[1]
    tn = EVA_PART
    hb = tm // HALO
    return pl.pallas_call(
        functools.partial(_even_in_body, seq // tm),
        grid=(t // tm, n // tn),
        in_specs=[pl.BlockSpec((tm, k), lambda i, j: (i, 0)),
                  pl.BlockSpec((HALO, k), lambda i, j: (jnp.maximum(i * hb - 1, 0), 0)),
                  pl.BlockSpec((k, tn), lambda i, j: (0, j)),
                  pl.BlockSpec((GDN_CONV, tn), lambda i, j: (0, jnp.minimum(j, 2)))],
        out_specs=pl.BlockSpec((tm, tn), lambda i, j: (i, j)),
        out_shape=jax.ShapeDtypeStruct((t, n), BF16),
        scratch_shapes=[pltpu.VMEM((HALO + tm, k), BF16), pltpu.VMEM((HALO + tm, tn), F32)],
        compiler_params=_cparams(("parallel", "arbitrary")),
        name="even_in_proj",
    )(x, x, w, conv_w)


def _gdn_body(q_ref, k_ref, v_ref, z_ref, ba_ref, alog_ref, dtb_ref, nw_ref, o_ref, state):
    n = pl.program_id(1)

    @pl.when(n == 0)
    def _():
        state[...] = jnp.zeros_like(state)

    grp = GDN_GROUP
    nc = grp // CHUNK
    nh = GDN_HEADS
    ba = ba_ref[...]
    beta_all = _sigmoid(ba)
    g_all = -jnp.exp(alog_ref[...]) * _softplus(ba + dtb_ref[...])

    ri = lax.broadcasted_iota(jnp.int32, (grp, grp), 0)
    ci = lax.broadcasted_iota(jnp.int32, (grp, grp), 1)
    same = (ri // CHUNK) == (ci // CHUNK)
    incl = same & (ri >= ci)
    strict = same & (ri > ci)
    eye = (ri == ci).astype(F32)
    tri = incl.astype(BF16)
    g_hi = g_all.astype(BF16)
    rem = g_all - g_hi.astype(F32)
    g_mid = rem.astype(BF16)
    g_lo = (rem - g_mid.astype(F32)).astype(BF16)
    gam_all = (jnp.dot(tri, g_hi, preferred_element_type=F32) + jnp.dot(tri, g_mid, preferred_element_type=F32)
               + jnp.dot(tri, g_lo, preferred_element_type=F32))
    gam_rows = gam_all.T

    hs = range(nh)
    q = [q_ref[:, h * GDN_D:(h + 1) * GDN_D] for h in hs]
    k = [k_ref[:, h * GDN_D:(h + 1) * GDN_D] for h in hs]
    v = [v_ref[:, h * GDN_D:(h + 1) * GDN_D] for h in hs]
    gam = [jnp.broadcast_to(gam_all[:, nh + h:nh + h + 1], (grp, GDN_D)) for h in hs]
    bcol = [beta_all[:, h:h + 1] for h in hs]
    decay, x, inv = [], [], []
    for h in hs:
        diff = gam[h][:, 0:1] - gam_rows[nh + h:nh + h + 1, :]
        decay.append(jnp.where(incl, jnp.exp(jnp.where(incl, diff, 0.0)), 0.0))
        kk = _bdot_nt(k[h], k[h])
        x.append(-jnp.where(strict, bcol[h] * kk * decay[h], 0.0))
        inv.append(eye + x[h])
    for _ in range(5):
        for h in hs:
            x[h] = _bdot(x[h], x[h])
            inv[h] = inv[h] + _bdot(inv[h], x[h])
    u, w, qk, q_dec, k_dec, g_last = [], [], [], [], [], []
    for h in hs:
        eg = jnp.exp(gam[h])
        uw = _bdot(inv[h], jnp.concatenate([v[h] * bcol[h], k[h] * (bcol[h] * eg)], axis=1))
        u.append(uw[:, :GDN_D])
        w.append(uw[:, GDN_D:])
        qk.append(_bdot_nt(q[h], k[h]) * decay[h])
        q_dec.append(q[h] * eg)
        kd, gl = [], []
        for c in range(nc):
            last = gam[h][(c + 1) * CHUNK - 1:(c + 1) * CHUNK, :]
            kd.append(k[h][c * CHUNK:(c + 1) * CHUNK] * jnp.exp(last - gam[h][c * CHUNK:(c + 1) * CHUNK]))
            gl.append(jnp.exp(last))
        k_dec.append(kd)
        g_last.append(gl)

    s = [state[h] for h in hs]
    q_s = [[] for _ in hs]
    delta = [[] for _ in hs]
    for c in range(nc):
        sl = slice(c * CHUNK, (c + 1) * CHUNK)
        for h in hs:
            r = _bdot(jnp.concatenate([w[h][sl], q_dec[h][sl]], axis=0), s[h])
            d = u[h][sl] - r[:CHUNK]
            q_s[h].append(r[CHUNK:])
            delta[h].append(d)
            s[h] = g_last[h][c] * s[h] + _bdot_tn(k_dec[h][c], d)
    for h in hs:
        state[h] = s[h]
        o = jnp.concatenate(q_s[h], axis=0) + _bdot(qk[h], jnp.concatenate(delta[h], axis=0))
        o = o * lax.rsqrt(jnp.mean(o * o, axis=-1, keepdims=True) + RMS_EPS) * nw_ref[...]
        z = z_ref[:, h * GDN_D:(h + 1) * GDN_D].astype(F32)
        o_ref[:, h * GDN_D:(h + 1) * GDN_D] = (o * _silu(z)).astype(o_ref.dtype)


def _gdn(proj_a, proj_b, alog_pad, dtb_pad, norm_w, bsz, seq):
    t = proj_a.shape[0]
    rows = GDN_GROUP
    spb = seq // rows
    width = GDN_HEADS * GDN_D

    def at(col):
        return lambda b, n: (b * spb + n, col)

    return pl.pallas_call(
        _gdn_body,
        grid=(bsz, spb),
        in_specs=[pl.BlockSpec((rows, width), at(0)),
                  pl.BlockSpec((rows, width), at(1)),
                  pl.BlockSpec((rows, width), at(2)),
                  pl.BlockSpec((rows, width), at(3)),
                  pl.BlockSpec((rows, LANES), at(EVB_BA // LANES)),
                  pl.BlockSpec((1, LANES), lambda b, n: (0, 0)),
                  pl.BlockSpec((1, LANES), lambda b, n: (0, 0)),
                  pl.BlockSpec((1, LANES), lambda b, n: (0, 0))],
        out_specs=pl.BlockSpec((rows, width), at(0)),
        out_shape=jax.ShapeDtypeStruct((t, width), BF16),
        scratch_shapes=[pltpu.VMEM((GDN_HEADS, GDN_D, GDN_D), F32)],
        compiler_params=_cparams(("parallel", "arbitrary")),
        name="gdn",
    )(proj_a, proj_a, proj_a, proj_a, proj_b, alog_pad, dtb_pad, norm_w)


def _swa_bias_tables():
    qi = np.arange(SWA_BLOCK)[:, None] + SWA_BLOCK
    kj = np.arange(2 * SWA_BLOCK)[None, :]
    rel = qi - kj
    buckets, negs = [], []
    for window, dilation in SWA_CONFIGS:
        buckets.append(_rel_bucket_np(np.maximum(rel, 0) * dilation))
        negs.append(np.where((rel >= 0) & (rel <= window // dilation), 0.0, NEG_BIG))
    return np.stack(buckets).astype(np.int32), np.stack(negs).astype(np.float32)


def _swa_body(*refs):
    ng = len(SWA_CONFIGS)
    ins, bias_ref, o_ref = refs[:5 * ng], refs[5 * ng], refs[5 * ng + 1]
    scratch = refs[5 * ng + 2:]
    kbufs, vbufs, o_scr, lse_scr = scratch[:ng], scratch[ng:2 * ng], scratch[2 * ng], scratch[2 * ng + 1]
    j = pl.program_id(1)
    blk = SWA_BLOCK
    first_head = lax.broadcasted_iota(jnp.int32, (blk, LANES), 1) < SWA_DH
    in_prev = lax.broadcasted_iota(jnp.int32, (2 * blk, 2 * blk), 1) < blk

    for g, (_, d) in enumerate(SWA_CONFIGS):
        q_ref, kc_ref, kp_ref, vc_ref, vp_ref = ins[5 * g:5 * g + 5]
        kbuf, vbuf = kbufs[g], vbufs[g]
        halo = blk * d
        kbuf[0:halo, :] = kp_ref[...]
        kbuf[halo:, :] = kc_ref[...]
        vbuf[0:halo, :] = vp_ref[...]
        vbuf[halo:, :] = vc_ref[...]

        def unit(u, carry, g=g, d=d, halo=halo, q_ref=q_ref, kbuf=kbuf, vbuf=vbuf):
            base = (u // d) * halo + u % d
            q = q_ref[pl.ds(base, blk, stride=d), :] * SWA_DH ** -0.5
            k = kbuf[pl.ds(base, 2 * blk, stride=d), :]
            v = vbuf[pl.ds(base, 2 * blk, stride=d), :]
            lhs = jnp.concatenate([jnp.where(first_head, q, 0.0), jnp.where(first_head, 0.0, q)], axis=0)
            s = _bdot_nt(lhs, k) + bias_ref[g, 0]
            no_prev = (j == 0) & (u < d)
            s = jnp.where(in_prev & no_prev, NEG_BIG, s)
            m = jnp.max(s, axis=-1, keepdims=True)
            p = jnp.exp(s - m)
            l = jnp.sum(p, axis=-1, keepdims=True)
            o2 = _bdot(p / l, v)
            lse2 = m + jnp.log(l)
            o_scr[g, pl.ds(base, blk, stride=d), :] = jnp.where(first_head, o2[:blk], o2[blk:])
            lse_scr[g, pl.ds(base, blk, stride=d), :] = jnp.where(first_head, lse2[:blk], lse2[blk:])
            return carry

        lax.fori_loop(0, SWA_SPAN // blk, unit, 0, unroll=SWA_UNROLL)

    def combine(c, carry):
        rows = pl.ds(pl.multiple_of(c * 2 * blk, 2 * blk), 2 * blk)
        lse = [lse_scr[g, rows, :] for g in range(ng)]
        m = functools.reduce(jnp.maximum, lse)
        e = [jnp.exp(x - m) for x in lse]
        den = functools.reduce(lambda x, y: x + y, e)
        o_ref[rows, :] = functools.reduce(
            lambda x, y: x + y, [(e[g] / den) * o_scr[g, rows, :] for g in range(ng)]).astype(o_ref.dtype)
        return carry

    lax.fori_loop(0, SWA_SPAN // (2 * blk), combine, 0)


def _swa(proj, bias, bsz, seq):
    t = proj.shape[0]
    ng = len(SWA_CONFIGS)
    nspan = seq // SWA_SPAN
    npair = SWA_HEADS * SWA_DH // LANES
    group_cols = SWA_HEADS * SWA_DH // LANES
    in_specs, scratch_k = [], []
    for g, (_, d) in enumerate(SWA_CONFIGS):
        halo = SWA_BLOCK * d
        per_span = SWA_SPAN // halo

        def cur(which, g=g):
            col = (which * ng + g) * group_cols
            return lambda b, j, p: (b * nspan + j, col + p)

        def prev(which, g=g, per_span=per_span):
            col = (which * ng + g) * group_cols
            return lambda b, j, p: (jnp.maximum((b * nspan + j) * per_span - 1, 0), col + p)

        in_specs += [pl.BlockSpec((SWA_SPAN, LANES), cur(0)),
                     pl.BlockSpec((SWA_SPAN, LANES), cur(1)), pl.BlockSpec((halo, LANES), prev(1)),
                     pl.BlockSpec((SWA_SPAN, LANES), cur(2)), pl.BlockSpec((halo, LANES), prev(2))]
        scratch_k.append(pltpu.VMEM((halo + SWA_SPAN, LANES), F32))
    in_specs.append(pl.BlockSpec((ng, 1, 2 * SWA_BLOCK, 2 * SWA_BLOCK), lambda b, j, p: (0, p, 0, 0)))
    return pl.pallas_call(
        _swa_body,
        grid=(bsz, nspan, npair),
        in_specs=in_specs,
        out_specs=pl.BlockSpec((SWA_SPAN, LANES), lambda b, j, p: (b * nspan + j, p)),
        out_shape=jax.ShapeDtypeStruct((t, npair * LANES), BF16),
        scratch_shapes=scratch_k + scratch_k + [pltpu.VMEM((ng, SWA_SPAN, LANES), F32)] * 2,
        compiler_params=_cparams(("parallel", "parallel", "parallel")),
        name="swa",
    )(*([proj] * (5 * ng)), bias)


def _layer_norm(y, g, b):
    mu = jnp.mean(y, axis=-1, keepdims=True)
    yc = y - mu
    var = jnp.mean(yc * yc, axis=-1, keepdims=True)
    return yc * lax.rsqrt(var + LN_EPS) * g + b


def _out_body(xa_ref, xb_ref, wa_ref, wb_ref, h_ref, g_ref, b_ref, y_ref):
    mix = _bdot(xa_ref[...], wa_ref[...]) + _bdot(xb_ref[...], wb_ref[...])
    y_ref[...] = _layer_norm(DEEPNORM_ALPHA * h_ref[...] + mix, g_ref[...], b_ref[...])


def _out_proj(body, acts, weights, h, g, b, tm, name):
    t = h.shape[0]

    def rowblk(a):
        return pl.BlockSpec((tm, a.shape[1]), lambda i: (i, 0))

    def full(a):
        return pl.BlockSpec(a.shape, lambda i: (0, 0))

    return pl.pallas_call(
        body,
        grid=(t // tm,),
        in_specs=[rowblk(a) for a in acts] + [full(w) for w in weights] + [rowblk(h), full(g), full(b)],
        out_specs=pl.BlockSpec((tm, D_MODEL), lambda i: (i, 0)),
        out_shape=jax.ShapeDtypeStruct((t, D_MODEL), F32),
        compiler_params=_cparams(("parallel",)),
        name=name,
    )(*acts, *weights, h, g, b)


def _diff_bias_tables(blk):
    buckets = _rel_bucket_np(np.arange(2 * REL_MAX_DIST))
    far = int(np.max(np.nonzero(buckets != NUM_BUCKETS - 1)[0])) + 1
    nb = -(-(far + blk - 1) // blk)
    kk = np.arange(blk)[:, None]
    qq = np.arange(blk)[None, :]
    dist = np.stack([t * blk + qq - kk for t in range(-1, nb + 1)])
    bucket = _rel_bucket_np(np.maximum(dist, 0))
    neg = np.where(dist >= 0, 0.0, NEG_BIG).astype(np.float32)
    return bucket, neg


def _diff_body(blk, nb, lam_init, q_ref, k_ref, v_ref, bias_ref, lam_ref, nw_ref, o_ref,
               vt, acc1, acc2, s_a, s_b):
    qi = pl.program_id(2)
    dv = 2 * DIFF_DH
    seq = k_ref.shape[0]

    @pl.when(qi == 0)
    def _():
        vt[dv:, :] = jnp.ones((vt.shape[0] - dv, seq), BF16)

        def fill(c, carry):
            st = pl.multiple_of(c * blk, blk)
            vt[0:dv, pl.ds(st, blk)] = v_ref[pl.ds(st, blk), :].astype(F32).T.astype(BF16)
            return carry

        lax.fori_loop(0, seq // blk, fill, 0)

    lane = lax.broadcasted_iota(jnp.int32, (blk, dv), 1)
    q = q_ref[...].astype(F32) * (DIFF_DH ** -0.5 * LOG2E)
    qs = (jnp.where(lane < DIFF_DH, q, 0.0).astype(BF16), jnp.where(lane >= DIFF_DH, q, 0.0).astype(BF16))
    accs = (acc1, acc2)
    for acc in accs:
        acc[...] = jnp.zeros_like(acc)

    last = pl.num_programs(2) - 1

    def key_rows(kj):
        return pl.ds(pl.multiple_of(jnp.minimum(kj, last) * blk, blk), blk)

    def scores(kj, dst):
        k = k_ref[key_rows(kj), :]
        for c, qc in enumerate(qs):
            dst[c] = lax.dot_general(k, qc, (((1,), (1,)), ((), ())), preferred_element_type=F32)

    def consume(kj, src, ms):
        vtb = vt[:, key_rows(kj)]
        bias = bias_ref[0, jnp.minimum(qi - kj, nb) + 1]
        out = []
        for c, (m, acc) in enumerate(zip(ms, accs)):
            s = src[c] + bias
            m_new = jnp.maximum(m, jnp.max(s, axis=0, keepdims=True))
            p = jnp.exp2(s - m_new).astype(BF16)
            acc[...] = jnp.exp2(m - m_new) * acc[...] + jnp.dot(vtb, p, preferred_element_type=F32)
            out.append(m_new)
        return tuple(out)

    def pair(t, ms):
        kj = 2 * t
        scores(kj + 1, s_b)
        ms = consume(kj, s_a, ms)
        scores(kj + 2, s_a)
        return consume(kj + 1, s_b, ms)

    m0 = jnp.full((1, blk), NEG_BIG, F32)
    scores(0, s_a)
    lax.fori_loop(0, (qi + 2) // 2, pair, (m0, m0))

    lp = lam_ref[...]
    lam = (jnp.exp(jnp.sum(lp[0:1] * lp[1:2], axis=-1, keepdims=True))
           - jnp.exp(jnp.sum(lp[2:3] * lp[3:4], axis=-1, keepdims=True)) + lam_init)
    a1, a2 = acc1[...], acc2[...]
    o_t = a1[:dv] / a1[dv:dv + 1] - lam * (a2[:dv] / a2[dv:dv + 1])
    o = o_t.T
    o = o * lax.rsqrt(jnp.mean(o * o, axis=-1, keepdims=True) + RMS_EPS) * nw_ref[...]
    o_ref[...] = (o * (1.0 - lam_init)).astype(o_ref.dtype)


def _diff_attention(proj, bias, lam_params, norm_w, lam_init, bsz, seq, blk):
    t = proj.shape[0]
    nq = seq // blk
    nt = bias.shape[1]
    nb = nt - 2
    width = 2 * DIFF_DH
    ones_rows = 2 * SUBLANES
    return pl.pallas_call(
        functools.partial(_diff_body, blk, nb, lam_init),
        grid=(bsz, DIFF_HEADS, nq),
        in_specs=[pl.BlockSpec((blk, width), lambda b, h, i: (b * nq + i, OD_QC // width + h)),
                  pl.BlockSpec((seq, width), lambda b, h, i: (b, OD_KC // width + h)),
                  pl.BlockSpec((seq, width), lambda b, h, i: (b, OD_VC // width + h)),
                  pl.BlockSpec((1, nt, blk, blk), lambda b, h, i: (h, 0, 0, 0)),
                  pl.BlockSpec((4, DIFF_DH), lambda b, h, i: (0, 0)),
                  pl.BlockSpec((1, width), lambda b, h, i: (0, 0))],
        out_specs=pl.BlockSpec((blk, width), lambda b, h, i: (b * nq + i, h)),
        out_shape=jax.ShapeDtypeStruct((t, DIFF_HEADS * width), BF16),
        scratch_shapes=[pltpu.VMEM((width + ones_rows, seq), BF16),
                        pltpu.VMEM((width + ones_rows, blk), F32), pltpu.VMEM((width + ones_rows, blk), F32),
                        pltpu.VMEM((2, blk, blk), F32), pltpu.VMEM((2, blk, blk), F32)],
        compiler_params=_cparams(("parallel", "parallel", "arbitrary")),
        name="diff_attn",
    )(proj, proj, proj, bias, lam_params, norm_w)


def _gla_body(nc, q_ref, k_ref, gd_ref, va_ref, vb_ref, ra_ref, rb_ref, wg_ref, bg_ref, nw_ref,
              o_ref, state):
    n = pl.program_id(2)

    @pl.when(n == 0)
    def _():
        state[...] = jnp.zeros_like(state)

    rows = nc * CHUNK
    nsub = rows // GLA_SUB
    per_chunk = CHUNK // GLA_SUB
    gate = jnp.dot(gd_ref[...], wg_ref[...], precision=HI, preferred_element_type=F32) + bg_ref[...]
    log_a = _log_sigmoid(gate) * (1.0 / GLA_TAU)
    q = q_ref[...].astype(F32) * GLA_DK ** -0.5
    k = k_ref[...].astype(F32)
    vs = (va_ref[...].astype(F32), vb_ref[...].astype(F32))
    lane_s = lax.broadcasted_iota(jnp.int32, (GLA_SUB, LANES), 1)
    lane_c = lax.broadcasted_iota(jnp.int32, (CHUNK, LANES), 1)
    sub_mask = (lane_s < GLA_DK, lane_s >= GLA_DK)
    chunk_mask = (lane_c < GLA_DK, lane_c >= GLA_DK)

    ri = lax.broadcasted_iota(jnp.int32, (CHUNK, CHUNK), 0)
    ci = lax.broadcasted_iota(jnp.int32, (CHUNK, CHUNK), 1)
    tri = (ri >= ci).astype(F32)
    b = jnp.concatenate(
        [jnp.dot(tri, log_a[c * CHUNK:(c + 1) * CHUNK], precision=HI, preferred_element_type=F32)
         for c in range(nc)], axis=0)

    b3 = b.reshape(nsub, GLA_SUB, LANES)
    q3 = q.reshape(nsub, GLA_SUB, LANES)
    k3 = k.reshape(nsub, GLA_SUB, LANES)
    v3 = [v.reshape(nsub, GLA_SUB, LANES) for v in vs]
    row3 = lax.broadcasted_iota(jnp.int32, (nsub, GLA_SUB, LANES), 1)
    lane3 = lax.broadcasted_iota(jnp.int32, (nsub, GLA_SUB, LANES), 2)
    rowc = lax.broadcasted_iota(jnp.int32, (nsub, GLA_SUB, 1), 1)
    o3 = [jnp.zeros((nsub, GLA_SUB, LANES), F32) for _ in vs]
    for jj in range(GLA_SUB):
        e = jnp.exp(jnp.where(row3 >= jj, b3 - b3[:, jj:jj + 1, :], 0.0))
        t = q3 * k3[:, jj:jj + 1, :] * e
        w_all = jnp.sum(t, axis=-1, keepdims=True)
        w_a = jnp.sum(jnp.where(lane3 < GLA_DK, t, 0.0), axis=-1, keepdims=True)
        for hd, w in enumerate((w_a, w_all - w_a)):
            o3[hd] = o3[hd] + jnp.where(rowc >= jj, w, 0.0) * v3[hd][:, jj:jj + 1, :]
    o_diag = [o.reshape(rows, LANES) for o in o3]

    kcol = lax.broadcasted_iota(jnp.int32, (GLA_SUB, CHUNK), 1)
    outs = ([], [])
    st = [state[0], state[1]]
    for c in range(nc):
        sl = slice(c * CHUNK, (c + 1) * CHUNK)
        bc, qc, kc = b[sl], q[sl], k[sl]
        a_rows = [[jnp.zeros((GLA_SUB, CHUNK), F32)] for _ in vs]
        for blk in range(1, per_chunk):
            r0 = blk * GLA_SUB
            bref = bc[r0:r0 + 1]
            qs = qc[r0:r0 + GLA_SUB] * jnp.exp(bc[r0:r0 + GLA_SUB] - bref)
            ks = kc * jnp.exp(jnp.minimum(bref - bc, 0.0))
            for hd in range(2):
                a = _bdot_nt(jnp.where(sub_mask[hd], qs, 0.0), ks)
                a_rows[hd].append(jnp.where(kcol < r0, a, 0.0))
        eb = jnp.exp(bc)
        b_last = bc[CHUNK - 1:CHUNK]
        k_dec = kc * jnp.exp(b_last - bc)
        e_last = jnp.exp(b_last)
        for hd in range(2):
            vc = vs[hd][sl]
            a_mat = jnp.concatenate(a_rows[hd], axis=0)
            q_dec = jnp.where(chunk_mask[hd], qc * eb, 0.0)
            outs[hd].append(_bdot(a_mat, vc) + _bdot_nt(q_dec, st[hd]))
            st[hd] = st[hd] * e_last + _bdot_tn(vc, k_dec)
    state[0] = st[0]
    state[1] = st[1]

    res = []
    for hd, r_ref in enumerate((ra_ref, rb_ref)):
        o = jnp.concatenate(outs[hd], axis=0) + o_diag[hd]
        o = o * lax.rsqrt(jnp.mean(o * o, axis=-1, keepdims=True) + RMS_EPS) * nw_ref[...]
        res.append(o * _silu(r_ref[...].astype(F32)))
    o_ref[...] = jnp.concatenate(res, axis=-1).astype(o_ref.dtype)


def _gla(proj, gate_in, w_gate_pad, b_gate, norm_w, bsz, seq, nc):
    t = proj.shape[0]
    rows = nc * CHUNK
    spb = seq // rows
    npair = GLA_HEADS // 2

    def col(base, mul=0, add=0):
        return lambda b, p, n: (b * spb + n, base // LANES + mul * p + add)

    blk = (rows, LANES)
    return pl.pallas_call(
        functools.partial(_gla_body, nc),
        grid=(bsz, npair, spb),
        in_specs=[pl.BlockSpec(blk, col(OD_QD, 1)),
                  pl.BlockSpec(blk, col(OD_KD, 1)),
                  pl.BlockSpec(blk, lambda b, p, n: (b * spb + n, 0)),
                  pl.BlockSpec(blk, col(OD_VD, 2, 0)), pl.BlockSpec(blk, col(OD_VD, 2, 1)),
                  pl.BlockSpec(blk, col(OD_RD, 2, 0)), pl.BlockSpec(blk, col(OD_RD, 2, 1)),
                  pl.BlockSpec((LANES, LANES), lambda b, p, n: (0, p)),
                  pl.BlockSpec((1, LANES), lambda b, p, n: (0, p)),
                  pl.BlockSpec((1, GLA_DV), lambda b, p, n: (0, 0))],
        out_specs=pl.BlockSpec((rows, 2 * GLA_DV), lambda b, p, n: (b * spb + n, p)),
        out_shape=jax.ShapeDtypeStruct((t, GLA_HEADS * GLA_DV), BF16),
        scratch_shapes=[pltpu.VMEM((2, GLA_DV, LANES), F32)],
        compiler_params=_cparams(("parallel", "parallel", "arbitrary")),
        name="gla",
    )(proj, proj, gate_in, proj, proj, proj, proj, w_gate_pad, b_gate, norm_w)


def _ffn_body(tiles_per_seq, tc, x_ref, halo_ref, wu_ref, cw_ref, cb_ref, wd_ref, g_ref, b_ref, y_ref,
              act, scratch_g, scratch_v):
    i = pl.program_id(0)
    x = x_ref[...]
    halo = jnp.where(i % tiles_per_seq == 0, 0.0, halo_ref[...])
    xb = jnp.concatenate([halo.astype(BF16), x.astype(BF16)], axis=0)

    def branch(lo, scratch):
        full = jnp.dot(xb, wu_ref[:, lo:lo + tc], preferred_element_type=F32)
        return _causal_conv(full, cw_ref[:, lo:lo + tc], scratch) + cb_ref[:, lo:lo + tc]

    for c in range(D_FF // tc):
        gate = branch(c * tc, scratch_g)
        val = branch(D_FF + c * tc, scratch_v)
        act[:, c * tc:(c + 1) * tc] = (_silu(gate) * val).astype(BF16)
    ffn = jnp.dot(act[...], wd_ref[...], preferred_element_type=F32)
    y_ref[...] = _layer_norm(DEEPNORM_ALPHA * x + ffn, g_ref[...], b_ref[...])


def _ffn(x, w_up, conv_w, conv_b, w_down, g, b, seq, tm, tc):
    t = x.shape[0]
    hb = tm // HALO

    def resident(a):
        return pl.BlockSpec(a.shape, lambda i: (0, 0), pipeline_mode=pl.Buffered(1))

    return pl.pallas_call(
        functools.partial(_ffn_body, seq // tm, tc),
        grid=(t // tm,),
        in_specs=[pl.BlockSpec((tm, D_MODEL), lambda i: (i, 0)),
                  pl.BlockSpec((HALO, D_MODEL), lambda i: (jnp.maximum(i * hb - 1, 0), 0)),
                  resident(w_up), resident(conv_w), resident(conv_b), resident(w_down),
                  resident(g), resident(b)],
        out_specs=pl.BlockSpec((tm, D_MODEL), lambda i: (i, 0)),
        out_shape=jax.ShapeDtypeStruct((t, D_MODEL), F32),
        scratch_shapes=[pltpu.VMEM((tm, D_FF), BF16), pltpu.VMEM((HALO + tm, tc), F32),
                        pltpu.VMEM((HALO + tm, tc), F32)],
        compiler_params=_cparams(("parallel",)),
        name="conv_ffn",
    )(x, x, w_up, conv_w, conv_b, w_down, g, b)


def _even_w_in(w):
    a_end = 4 * EVA_PART
    gates = w[:, a_end:a_end + 2 * GDN_HEADS]
    qkv_b = w[:, a_end + 2 * GDN_HEADS:]
    pad = jnp.zeros((w.shape[0], EVB_COLS - EVB_BA - 2 * GDN_HEADS), w.dtype)
    return w[:, :a_end].astype(BF16), jnp.concatenate([qkv_b, gates, pad], axis=1).astype(BF16)


def _odd_w_in(w):
    pad = jnp.zeros((w.shape[0], LANES - GLA_RANK), w.dtype)
    return w[:, :OD_COLS].astype(BF16), jnp.concatenate([w[:, OD_COLS:], pad], axis=1).astype(BF16)


def _even_layer(h, rel_bias, w_in, conv_w, a_log, dt_bias, norm_w, w_out, ln_g, ln_b, bsz, seq):
    w_a, w_b = _even_w_in(w_in)
    proj_a = _even_in_proj(h, w_a, conv_w, seq, PROJ_TM)
    proj_b = _matmul(h, w_b, PROJ_TM, EVB_TN, F32, "even_in_proj_f32")
    gate_pad = jnp.zeros((1, LANES), F32)
    alog_pad = lax.dynamic_update_slice(gate_pad, a_log[None].astype(F32), (0, GDN_HEADS))
    dtb_pad = lax.dynamic_update_slice(gate_pad, dt_bias[None].astype(F32), (0, GDN_HEADS))
    o_a = _gdn(proj_a, proj_b, alog_pad, dtb_pad, norm_w[None], bsz, seq)
    tiles = _bias_tiles(rel_bias, *_swa_bias_tables())
    ng, two_blk = len(SWA_CONFIGS), 2 * SWA_BLOCK
    bias = tiles.reshape(SWA_HEADS // 2, 2, ng, SWA_BLOCK, two_blk).transpose(2, 0, 1, 3, 4)
    o_b = _swa(proj_b, bias.reshape(ng, SWA_HEADS // 2, two_blk, two_blk), bsz, seq)
    w_out = w_out.astype(BF16)
    return _out_proj(_out_body, [o_a, o_b], [w_out[:EVA_PART], w_out[EVA_PART:]],
                     h, ln_g[None], ln_b[None], 512, "even_out")


def _odd_layer(h, rel_bias, w_in, lam_params, diff_norm_w, w_gate, b_gate, gla_norm_w, w_out,
               ln_g, ln_b, lam_init, bsz, seq):
    w_main, w_gd = _odd_w_in(w_in)
    proj = _matmul(h, w_main, PROJ_TM, OD_TN, BF16, "odd_in_proj")
    gate_in = _matmul(h, w_gd, PROJ_TM, LANES, F32, "odd_gate_proj")
    blk = min(DIFF_BLOCK, seq)
    bucket, neg = _diff_bias_tables(blk)
    bias = _bias_tiles(rel_bias, bucket, neg, scale=LOG2E, base_bucket=NUM_BUCKETS - 1)
    o_c = _diff_attention(proj, bias, lam_params, diff_norm_w[None], lam_init, bsz, seq, blk)
    w_gate_pad = jnp.concatenate(
        [w_gate, jnp.zeros((LANES - GLA_RANK, w_gate.shape[1]), w_gate.dtype)], axis=0)
    o_d = _gla(proj, gate_in, w_gate_pad, b_gate[None], gla_norm_w[None], bsz, seq, 4)
    diff_v = DIFF_HEADS * 2 * DIFF_DH
    w_out = w_out.astype(BF16)
    return _out_proj(_out_body, [o_c, o_d], [w_out[:diff_v], w_out[diff_v:]],
                     h, ln_g[None], ln_b[None], 512, "odd_out")


def kernel(x, rel_bias, w_in_even, gdn_conv_w, gdn_a_log, gdn_dt_bias, gdn_norm_w, w_out_even,
           w_in_odd, diff_lambda, diff_norm_w, gla_w_gate, gla_b_gate, gla_norm_w, w_out_odd,
           ffn_w_up, ffn_conv_w, ffn_conv_b, ffn_w_down, ln_g, ln_b):
    bsz, seq, d = x.shape
    h = x.reshape(bsz * seq, d)
    for layer in range(DEPTH):
        i = layer // 2
        if layer % 2 == 0:
            h = _even_layer(h, rel_bias, w_in_even[i], gdn_conv_w[i], gdn_a_log[i], gdn_dt_bias[i],
                            gdn_norm_w[i], w_out_even[i], ln_g[layer, 0], ln_b[layer, 0], bsz, seq)
        else:
            lam_init = 0.8 - 0.6 * math.exp(-0.3 * layer)
            h = _odd_layer(h, rel_bias, w_in_odd[i], diff_lambda[i], diff_norm_w[i], gla_w_gate[i],
                           gla_b_gate[i], gla_norm_w[i], w_out_odd[i], ln_g[layer, 0], ln_b[layer, 0],
                           lam_init, bsz, seq)
        h = _ffn(h, ffn_w_up[layer].astype(BF16), ffn_conv_w[layer], ffn_conv_b[layer][None],
                 ffn_w_down[layer].astype(BF16), ln_g[layer, 1][None], ln_b[layer, 1][None],
                 seq, 512, 256)
    return h.reshape(bsz, seq, d)
```

```python
import functools
import math

import numpy as np
import jax
import jax.numpy as jnp
from jax import lax
from jax.experimental import pallas as pl
from jax.experimental.pallas import tpu as pltpu

F32 = jnp.float32
BF16 = jnp.bfloat16
HI = lax.Precision.HIGHEST

D_MODEL = 1024
DEPTH = 2
DEEPNORM_ALPHA = (2 * DEPTH) ** 0.25
LN_EPS = 1e-5
RMS_EPS = 1e-6
NUM_BUCKETS = 32
REL_MAX_DIST = 2048
GDN_HEADS = 6
GDN_D = 128
GDN_CONV = 4
CHUNK = 64
GDN_GROUP = 256
SWA_CONFIGS = ((128, 1), (512, 4), (2048, 16))
SWA_HEADS = 4
SWA_DH = 64
SWA_BLOCK = 128
SWA_UNROLL = 8
SWA_SPAN = SWA_BLOCK * max(d for _, d in SWA_CONFIGS)
DIFF_HEADS = 4
DIFF_DH = 64
DIFF_BLOCK = 512
LOG2E = math.log2(math.e)
GLA_HEADS = 4
GLA_DK = 64
GLA_DV = 128
GLA_RANK = 16
GLA_TAU = 16.0
GLA_SUB = 16
GLA_GROUP = 256
GLA_MAX_DECAY = 60.0
D_FF = 2816
FFN_CONV = 3

LANES = 128
SUBLANES = 8
HALO = 16
VMEM_LIMIT = 56 * 1024 * 1024
NEG_BIG = -1e30

EVA_PART = GDN_HEADS * GDN_D
EVB_BA = 2304
EVB_COLS = 2560
EVB_TN = 512
OD_QC, OD_KC, OD_VC = 0, 512, 1024
OD_QD, OD_KD, OD_VD, OD_RD = 1536, 1792, 2048, 2560
OD_COLS = 3072
OD_TN = 768
PROJ_TM = 1024


def _cparams(sem):
    return pltpu.CompilerParams(dimension_semantics=sem, vmem_limit_bytes=VMEM_LIMIT)


def _bdot(a, b):
    return jnp.dot(a.astype(BF16), b.astype(BF16), preferred_element_type=F32)


def _bdot_nt(a, b):
    return lax.dot_general(a.astype(BF16), b.astype(BF16), (((1,), (1,)), ((), ())),
                           preferred_element_type=F32)


def _bdot_tn(a, b):
    return lax.dot_general(a.astype(BF16), b.astype(BF16), (((0,), (0,)), ((), ())),
                           preferred_element_type=F32)


def _sigmoid(x):
    return 1.0 / (1.0 + jnp.exp(-x))


def _silu(x):
    return x * _sigmoid(x)


def _softplus(x):
    return jnp.maximum(x, 0.0) + jnp.log1p(jnp.exp(-jnp.abs(x)))


def _log_sigmoid(x):
    return -_softplus(-x)


def _mm_body(x_ref, w_ref, o_ref, xb):
    @pl.when(pl.program_id(1) == 0)
    def _():
        xb[...] = x_ref[...].astype(BF16)

    o_ref[...] = jnp.dot(xb[...], w_ref[...], preferred_element_type=F32).astype(o_ref.dtype)


def _matmul(x, w, tm, tn, out_dtype, name):
    m, k = x.shape
    n = w.shape[1]
    return pl.pallas_call(
        _mm_body,
        grid=(m // tm, n // tn),
        in_specs=[pl.BlockSpec((tm, k), lambda i, j: (i, 0)),
                  pl.BlockSpec((k, tn), lambda i, j: (0, j))],
        out_specs=pl.BlockSpec((tm, tn), lambda i, j: (i, j)),
        out_shape=jax.ShapeDtypeStruct((m, n), out_dtype),
        scratch_shapes=[pltpu.VMEM((tm, k), BF16)],
        compiler_params=_cparams(("parallel", "arbitrary")),
        name=name,
    )(x, w)


def _rel_bucket_np(dist):
    max_exact = NUM_BUCKETS // 2
    d = np.maximum(dist, 1).astype(np.float32)
    large = max_exact + (np.log(d / max_exact) / math.log(REL_MAX_DIST / max_exact)
                         * (NUM_BUCKETS - max_exact)).astype(np.int32)
    large = np.minimum(large, NUM_BUCKETS - 1)
    return np.where(dist < max_exact, dist, large).astype(np.int32)


def _bias_body(scale, base_bucket, rb_ref, bucket_ref, neg_ref, o_ref):
    h = pl.program_id(0)
    bucket = bucket_ref[0]
    base = 0.0 if base_bucket is None else rb_ref[base_bucket, h]
    acc = neg_ref[0]
    for b in range(NUM_BUCKETS):
        acc = acc + jnp.where(bucket == b, (rb_ref[b, h] - base) * scale, 0.0)
    o_ref[0, 0] = acc


def _bias_tiles(rel_bias, bucket, neg, scale=1.0, base_bucket=None):
    nt, r, c = bucket.shape
    nh = rel_bias.shape[1]
    return pl.pallas_call(
        functools.partial(_bias_body, scale, base_bucket),
        grid=(nh, nt),
        in_specs=[pl.BlockSpec(memory_space=pltpu.SMEM),
                  pl.BlockSpec((1, r, c), lambda h, t: (t, 0, 0)),
                  pl.BlockSpec((1, r, c), lambda h, t: (t, 0, 0))],
        out_specs=pl.BlockSpec((1, 1, r, c), lambda h, t: (h, t, 0, 0)),
        out_shape=jax.ShapeDtypeStruct((nh, nt, r, c), F32),
        compiler_params=_cparams(("parallel", "parallel")),
        name="rel_bias_tiles",
    )(rel_bias, jnp.asarray(bucket), jnp.asarray(neg))


def _causal_conv(full, w, scratch):
    width = w.shape[0]
    rows = full.shape[0] - HALO
    scratch[...] = full
    y = w[width - 1:width, :] * full[HALO:]
    for j in range(width - 1):
        back = width - 1 - j
        y = y + w[j:j + 1, :] * scratch[HALO - back:HALO - back + rows, :]
    return y


def _even_in_body(tiles_per_seq, x_ref, halo_ref, w_ref, cw_ref, o_ref, xb, scratch):
    i = pl.program_id(0)
    j = pl.program_id(1)

    @pl.when(j == 0)
    def _():
        xb[0:HALO, :] = jnp.where(i % tiles_per_seq == 0, 0.0, halo_ref[...]).astype(BF16)
        xb[HALO:, :] = x_ref[...].astype(BF16)

    y = jnp.dot(xb[...], w_ref[...], preferred_element_type=F32)

    @pl.when(j < 3)
    def _():
        c = _silu(_causal_conv(y, cw_ref[...], scratch))
        qk_scale = jnp.where(j == 0, GDN_D ** -0.5, 1.0)
        for hd in range(GDN_HEADS):
            ch = c[:, hd * GDN_D:(hd + 1) * GDN_D]
            inv = lax.rsqrt(jnp.sum(ch * ch, axis=-1, keepdims=True) + RMS_EPS) * qk_scale
            o_ref[:, hd * GDN_D:(hd + 1) * GDN_D] = (ch * jnp.where(j < 2, inv, 1.0)).astype(o_ref.dtype)

    @pl.when(j >= 3)
    def _():
        o_ref[...] = y[HALO:].astype(o_ref.dtype)


def _even_in_proj(x, w, conv_w, seq, tm):
    t, k = x.shape
    n = w.shape[1]
    tn = EVA_PART
    hb = tm // HALO
    return pl.pallas_call(
        functools.partial(_even_in_body, seq // tm),
        grid=(t // tm, n // tn),
        in_specs=[pl.BlockSpec((tm, k), lambda i, j: (i, 0)),
                  pl.BlockSpec((HALO, k), lambda i, j: (jnp.maximum(i * hb - 1, 0), 0)),
                  pl.BlockSpec((k, tn), lambda i, j: (0, j)),
                  pl.BlockSpec((GDN_CONV, tn), lambda i, j: (0, jnp.minimum(j, 2)))],
        out_specs=pl.BlockSpec((tm, tn), lambda i, j: (i, j)),
        out_shape=jax.ShapeDtypeStruct((t, n), BF16),
        scratch_shapes=[pltpu.VMEM((HALO + tm, k), BF16), pltpu.VMEM((HALO + tm, tn), F32)],
        compiler_params=_cparams(("parallel", "arbitrary")),
        name="even_in_proj",
    )(x, x, w, conv_w)


def _gdn_body(q_ref, k_ref, v_ref, z_ref, ba_ref, alog_ref, dtb_ref, nw_ref, o_ref, state):
    n = pl.program_id(1)

    @pl.when(n == 0)
    def _():
        state[...] = jnp.zeros_like(state)

    grp = GDN_GROUP
    nc = grp // CHUNK
    nh = GDN_HEADS
    ba = ba_ref[...]
    beta_all = _sigmoid(ba)
    g_all = -jnp.exp(alog_ref[...]) * _softplus(ba + dtb_ref[...])

    ri = lax.broadcasted_iota(jnp.int32, (grp, grp), 0)
    ci = lax.broadcasted_iota(jnp.int32, (grp, grp), 1)
    same = (ri // CHUNK) == (ci // CHUNK)
    incl = same & (ri >= ci)
    strict = same & (ri > ci)
    eye = (ri == ci).astype(F32)
    tri = incl.astype(BF16)
    g_hi = g_all.astype(BF16)
    rem = g_all - g_hi.astype(F32)
    g_mid = rem.astype(BF16)
    g_lo = (rem - g_mid.astype(F32)).astype(BF16)
    gam_all = (jnp.dot(tri, g_hi, preferred_element_type=F32) + jnp.dot(tri, g_mid, preferred_element_type=F32)
               + jnp.dot(tri, g_lo, preferred_element_type=F32))
    gam_rows = gam_all.T

    hs = range(nh)
    q = [q_ref[:, h * GDN_D:(h + 1) * GDN_D] for h in hs]
    k = [k_ref[:, h * GDN_D:(h + 1) * GDN_D] for h in hs]
    v = [v_ref[:, h * GDN_D:(h + 1) * GDN_D] for h in hs]
    gam = [jnp.broadcast_to(gam_all[:, nh + h:nh + h + 1], (grp, GDN_D)) for h in hs]
    bcol = [beta_all[:, h:h + 1] for h in hs]
    decay, x, inv = [], [], []
    for h in hs:
        diff = gam[h][:, 0:1] - gam_rows[nh + h:nh + h + 1, :]
        decay.append(jnp.where(incl, jnp.exp(jnp.where(incl, diff, 0.0)), 0.0))
        kk = _bdot_nt(k[h], k[h])
        x.append(-jnp.where(strict, bcol[h] * kk * decay[h], 0.0))
        inv.append(eye + x[h])
    for _ in range(5):
        for h in hs:
            x[h] = _bdot(x[h], x[h])
            inv[h] = inv[h] + _bdot(inv[h], x[h])
    u, w, qk, q_dec, k_dec, g_last = [], [], [], [], [], []
    for h in hs:
        eg = jnp.exp(gam[h])
        uw = _bdot(inv[h], jnp.concatenate([v[h] * bcol[h], k[h] * (bcol[h] * eg)], axis=1))
        u.append(uw[:, :GDN_D])
        w.append(uw[:, GDN_D:])
        qk.append(_bdot_nt(q[h], k[h]) * decay[h])
        q_dec.append(q[h] * eg)
        kd, gl = [], []
        for c in range(nc):
            last = gam[h][(c + 1) * CHUNK - 1:(c + 1) * CHUNK, :]
            kd.append(k[h][c * CHUNK:(c + 1) * CHUNK] * jnp.exp(last - gam[h][c * CHUNK:(c + 1) * CHUNK]))
            gl.append(jnp.exp(last))
        k_dec.append(kd)
        g_last.append(gl)

    s = [state[h] for h in hs]
    q_s = [[] for _ in hs]
    delta = [[] for _ in hs]
    for c in range(nc):
        sl = slice(c * CHUNK, (c + 1) * CHUNK)
        for h in hs:
            r = _bdot(jnp.concatenate([w[h][sl], q_dec[h][sl]], axis=0), s[h])
            d = u[h][sl] - r[:CHUNK]
            q_s[h].append(r[CHUNK:])
            delta[h].append(d)
            s[h] = g_last[h][c] * s[h] + _bdot_tn(k_dec[h][c], d)
    for h in hs:
        state[h] = s[h]
        o = jnp.concatenate(q_s[h], axis=0) + _bdot(qk[h], jnp.concatenate(delta[h], axis=0))
        o = o * lax.rsqrt(jnp.mean(o * o, axis=-1, keepdims=True) + RMS_EPS) * nw_ref[...]
        z = z_ref[:, h * GDN_D:(h + 1) * GDN_D].astype(F32)
        o_ref[:, h * GDN_D:(h + 1) * GDN_D] = (o * _silu(z)).astype(o_ref.dtype)


def _gdn(proj_a, proj_b, alog_pad, dtb_pad, norm_w, bsz, seq):
    t = proj_a.shape[0]
    rows = GDN_GROUP
    spb = seq // rows
    width = GDN_HEADS * GDN_D

    def at(col):
        return lambda b, n: (b * spb + n, col)

    return pl.pallas_call(
        _gdn_body,
        grid=(bsz, spb),
        in_specs=[pl.BlockSpec((rows, width), at(0)),
                  pl.BlockSpec((rows, width), at(1)),
                  pl.BlockSpec((rows, width), at(2)),
                  pl.BlockSpec((rows, width), at(3)),
                  pl.BlockSpec((rows, LANES), at(EVB_BA // LANES)),
                  pl.BlockSpec((1, LANES), lambda b, n: (0, 0)),
                  pl.BlockSpec((1, LANES), lambda b, n: (0, 0)),
                  pl.BlockSpec((1, LANES), lambda b, n: (0, 0))],
        out_specs=pl.BlockSpec((rows, width), at(0)),
        out_shape=jax.ShapeDtypeStruct((t, width), BF16),
        scratch_shapes=[pltpu.VMEM((GDN_HEADS, GDN_D, GDN_D), F32)],
        compiler_params=_cparams(("parallel", "arbitrary")),
        name="gdn",
    )(proj_a, proj_a, proj_a, proj_a, proj_b, alog_pad, dtb_pad, norm_w)


def _swa_bias_tables():
    qi = np.arange(SWA_BLOCK)[:, None] + SWA_BLOCK
    kj = np.arange(2 * SWA_BLOCK)[None, :]
    rel = qi - kj
    buckets, negs = [], []
    for window, dilation in SWA_CONFIGS:
        buckets.append(_rel_bucket_np(np.maximum(rel, 0) * dilation))
        negs.append(np.where((rel >= 0) & (rel <= window // dilation), 0.0, NEG_BIG))
    return np.stack(buckets).astype(np.int32), np.stack(negs).astype(np.float32)


def _swa_body(*refs):
    ng = len(SWA_CONFIGS)
    ins, bias_ref, o_ref = refs[:5 * ng], refs[5 * ng], refs[5 * ng + 1]
    scratch = refs[5 * ng + 2:]
    kbufs, vbufs, o_scr, lse_scr = scratch[:ng], scratch[ng:2 * ng], scratch[2 * ng], scratch[2 * ng + 1]
    j = pl.program_id(1)
    blk = SWA_BLOCK
    first_head = lax.broadcasted_iota(jnp.int32, (blk, LANES), 1) < SWA_DH
    in_prev = lax.broadcasted_iota(jnp.int32, (2 * blk, 2 * blk), 1) < blk

    for g, (_, d) in enumerate(SWA_CONFIGS):
        q_ref, kc_ref, kp_ref, vc_ref, vp_ref = ins[5 * g:5 * g + 5]
        kbuf, vbuf = kbufs[g], vbufs[g]
        halo = blk * d
        kbuf[0:halo, :] = kp_ref[...]
        kbuf[halo:, :] = kc_ref[...]
        vbuf[0:halo, :] = vp_ref[...]
        vbuf[halo:, :] = vc_ref[...]

        def unit(u, carry, g=g, d=d, halo=halo, q_ref=q_ref, kbuf=kbuf, vbuf=vbuf):
            base = (u // d) * halo + u % d
            q = q_ref[pl.ds(base, blk, stride=d), :] * SWA_DH ** -0.5
            k = kbuf[pl.ds(base, 2 * blk, stride=d), :]
            v = vbuf[pl.ds(base, 2 * blk, stride=d), :]
            lhs = jnp.concatenate([jnp.where(first_head, q, 0.0), jnp.where(first_head, 0.0, q)], axis=0)
            s = _bdot_nt(lhs, k) + bias_ref[g, 0]
            no_prev = (j == 0) & (u < d)
            s = jnp.where(in_prev & no_prev, NEG_BIG, s)
            m = jnp.max(s, axis=-1, keepdims=True)
            p = jnp.exp(s - m)
            l = jnp.sum(p, axis=-1, keepdims=True)
            o2 = _bdot(p / l, v)
            lse2 = m + jnp.log(l)
            o_scr[g, pl.ds(base, blk, stride=d), :] = jnp.where(first_head, o2[:blk], o2[blk:])
            lse_scr[g, pl.ds(base, blk, stride=d), :] = jnp.where(first_head, lse2[:blk], lse2[blk:])
            return carry

        lax.fori_loop(0, SWA_SPAN // blk, unit, 0, unroll=SWA_UNROLL)

    def combine(c, carry):
        rows = pl.ds(pl.multiple_of(c * 2 * blk, 2 * blk), 2 * blk)
        lse = [lse_scr[g, rows, :] for g in range(ng)]
        m = functools.reduce(jnp.maximum, lse)
        e = [jnp.exp(x - m) for x in lse]
        den = functools.reduce(lambda x, y: x + y, e)
        o_ref[rows, :] = functools.reduce(
            lambda x, y: x + y, [(e[g] / den) * o_scr[g, rows, :] for g in range(ng)]).astype(o_ref.dtype)
        return carry

    lax.fori_loop(0, SWA_SPAN // (2 * blk), combine, 0)


def _swa(proj, bias, bsz, seq):
    t = proj.shape[0]
    ng = len(SWA_CONFIGS)
    nspan = seq // SWA_SPAN
    npair = SWA_HEADS * SWA_DH // LANES
    group_cols = SWA_HEADS * SWA_DH // LANES
    in_specs, scratch_k = [], []
    for g, (_, d) in enumerate(SWA_CONFIGS):
        halo = SWA_BLOCK * d
        per_span = SWA_SPAN // halo

        def cur(which, g=g):
            col = (which * ng + g) * group_cols
            return lambda b, j, p: (b * nspan + j, col + p)

        def prev(which, g=g, per_span=per_span):
            col = (which * ng + g) * group_cols
            return lambda b, j, p: (jnp.maximum((b * nspan + j) * per_span - 1, 0), col + p)

        in_specs += [pl.BlockSpec((SWA_SPAN, LANES), cur(0)),
                     pl.BlockSpec((SWA_SPAN, LANES), cur(1)), pl.BlockSpec((halo, LANES), prev(1)),
                     pl.BlockSpec((SWA_SPAN, LANES), cur(2)), pl.BlockSpec((halo, LANES), prev(2))]
        scratch_k.append(pltpu.VMEM((halo + SWA_SPAN, LANES), F32))
    in_specs.append(pl.BlockSpec((ng, 1, 2 * SWA_BLOCK, 2 * SWA_BLOCK), lambda b, j, p: (0, p, 0, 0)))
    return pl.pallas_call(
        _swa_body,
        grid=(bsz, nspan, npair),
        in_specs=in_specs,
        out_specs=pl.BlockSpec((SWA_SPAN, LANES), lambda b, j, p: (b * nspan + j, p)),
        out_shape=jax.ShapeDtypeStruct((t, npair * LANES), BF16),
        scratch_shapes=scratch_k + scratch_k + [pltpu.VMEM((ng, SWA_SPAN, LANES), F32)] * 2,
        compiler_params=_cparams(("parallel", "parallel", "parallel")),
        name="swa",
    )(*([proj] * (5 * ng)), bias)


def _layer_norm(y, g, b):
    mu = jnp.mean(y, axis=-1, keepdims=True)
    yc = y - mu
    var = jnp.mean(yc * yc, axis=-1, keepdims=True)
    return yc * lax.rsqrt(var + LN_EPS) * g + b


def _out_body(xa_ref, xb_ref, wa_ref, wb_ref, h_ref, g_ref, b_ref, y_ref):
    mix = _bdot(xa_ref[...], wa_ref[...]) + _bdot(xb_ref[...], wb_ref[...])
    y_ref[...] = _layer_norm(DEEPNORM_ALPHA * h_ref[...] + mix, g_ref[...], b_ref[...])


def _out_proj(body, acts, weights, h, g, b, tm, name):
    t = h.shape[0]

    def rowblk(a):
        return pl.BlockSpec((tm, a.shape[1]), lambda i: (i, 0))

    def full(a):
        return pl.BlockSpec(a.shape, lambda i: (0, 0))

    return pl.pallas_call(
        body,
        grid=(t // tm,),
        in_specs=[rowblk(a) for a in acts] + [full(w) for w in weights] + [rowblk(h), full(g), full(b)],
        out_specs=pl.BlockSpec((tm, D_MODEL), lambda i: (i, 0)),
        out_shape=jax.ShapeDtypeStruct((t, D_MODEL), F32),
        compiler_params=_cparams(("parallel",)),
        name=name,
    )(*acts, *weights, h, g, b)


def _diff_bias_tables(blk):
    buckets = _rel_bucket_np(np.arange(2 * REL_MAX_DIST))
    far = int(np.max(np.nonzero(buckets != NUM_BUCKETS - 1)[0])) + 1
    nb = -(-(far + blk - 1) // blk)
    kk = np.arange(blk)[:, None]
    qq = np.arange(blk)[None, :]
    dist = np.stack([t * blk + qq - kk for t in range(-1, nb + 1)])
    bucket = _rel_bucket_np(np.maximum(dist, 0))
    neg = np.where(dist >= 0, 0.0, NEG_BIG).astype(np.float32)
    return bucket, neg


def _diff_body(blk, nb, lam_init, q_ref, qn_ref, k_ref, v_ref, bias_ref, lam_ref, nw_ref, o_ref,
               vt, acc1, acc2, s_a, s_b):
    qi = pl.program_id(2)
    dv = 2 * DIFF_DH
    seq = k_ref.shape[0]

    @pl.when(qi == 0)
    def _():
        vt[dv:, :] = jnp.ones((vt.shape[0] - dv, seq), BF16)

        def fill(c, carry):
            st = pl.multiple_of(c * blk, blk)
            vt[0:dv, pl.ds(st, blk)] = v_ref[pl.ds(st, blk), :].astype(F32).T.astype(BF16)
            return carry

        lax.fori_loop(0, seq // blk, fill, 0)

    lane = lax.broadcasted_iota(jnp.int32, (blk, dv), 1)

    def components(ref):
        q = ref[...].astype(F32) * (DIFF_DH ** -0.5 * LOG2E)
        return (jnp.where(lane < DIFF_DH, q, 0.0).astype(BF16), jnp.where(lane >= DIFF_DH, q, 0.0).astype(BF16))

    qs = components(q_ref)
    qs_next = components(qn_ref)
    accs = (acc1, acc2)
    for acc in accs:
        acc[...] = jnp.zeros_like(acc)

    last = pl.num_programs(2) - 1

    def key_rows(kj):
        return pl.ds(pl.multiple_of(jnp.minimum(kj, last) * blk, blk), blk)

    def scores(kj, dst, queries=qs):
        k = k_ref[key_rows(kj), :]
        for c, qc in enumerate(queries):
            dst[c] = lax.dot_general(k, qc, (((1,), (1,)), ((), ())), preferred_element_type=F32)

    def consume(kj, src, ms):
        vtb = vt[:, key_rows(kj)]
        bias = bias_ref[0, jnp.minimum(qi - kj, nb) + 1]
        out = []
        for c, (m, acc) in enumerate(zip(ms, accs)):
            s = src[c] + bias
            m_new = jnp.maximum(m, jnp.max(s, axis=0, keepdims=True))
            p = jnp.exp2(s - m_new).astype(BF16)
            acc[...] = jnp.exp2(m - m_new) * acc[...] + jnp.dot(vtb, p, preferred_element_type=F32)
            out.append(m_new)
        return tuple(out)

    trips = (qi + 2) // 2

    def pair(t, ms):
        kj = 2 * t
        scores(kj + 1, s_b)
        ms = consume(kj, s_a, ms)
        is_last = t == trips - 1
        scores(jnp.where(is_last, 0, kj + 2), s_a,
               tuple(jnp.where(is_last, qn, qc) for qc, qn in zip(qs, qs_next)))
        return consume(kj + 1, s_b, ms)

    @pl.when(qi == 0)
    def _():
        scores(0, s_a)

    m0 = jnp.full((1, blk), NEG_BIG, F32)
    lax.fori_loop(0, trips, pair, (m0, m0))

    lp = lam_ref[...]
    lam = (jnp.exp(jnp.sum(lp[0:1] * lp[1:2], axis=-1, keepdims=True))
           - jnp.exp(jnp.sum(lp[2:3] * lp[3:4], axis=-1, keepdims=True)) + lam_init)
    a1, a2 = acc1[...], acc2[...]
    o_t = a1[:dv] * (1.0 / a1[dv:dv + 1]) - a2[:dv] * (lam / a2[dv:dv + 1])
    o = o_t.T
    o = o * lax.rsqrt(jnp.mean(o * o, axis=-1, keepdims=True) + RMS_EPS) * nw_ref[...]
    o_ref[...] = (o * (1.0 - lam_init)).astype(o_ref.dtype)


def _diff_attention(proj, bias, lam_params, norm_w, lam_init, bsz, seq, blk):
    t = proj.shape[0]
    nq = seq // blk
    nt = bias.shape[1]
    nb = nt - 2
    width = 2 * DIFF_DH
    ones_rows = 2 * SUBLANES
    return pl.pallas_call(
        functools.partial(_diff_body, blk, nb, lam_init),
        grid=(bsz, DIFF_HEADS, nq),
        in_specs=[pl.BlockSpec((blk, width), lambda b, h, i: (b * nq + i, OD_QC // width + h)),
                  pl.BlockSpec((blk, width),
                               lambda b, h, i: (b * nq + jnp.minimum(i + 1, nq - 1), OD_QC // width + h)),
                  pl.BlockSpec((seq, width), lambda b, h, i: (b, OD_KC // width + h)),
                  pl.BlockSpec((seq, width), lambda b, h, i: (b, OD_VC // width + h)),
                  pl.BlockSpec((1, nt, blk, blk), lambda b, h, i: (h, 0, 0, 0)),
                  pl.BlockSpec((4, DIFF_DH), lambda b, h, i: (0, 0)),
                  pl.BlockSpec((1, width), lambda b, h, i: (0, 0))],
        out_specs=pl.BlockSpec((blk, width), lambda b, h, i: (b * nq + i, h)),
        out_shape=jax.ShapeDtypeStruct((t, DIFF_HEADS * width), BF16),
        scratch_shapes=[pltpu.VMEM((width + ones_rows, seq), BF16),
                        pltpu.VMEM((width + ones_rows, blk), F32), pltpu.VMEM((width + ones_rows, blk), F32),
                        pltpu.VMEM((2, blk, blk), F32), pltpu.VMEM((2, blk, blk), F32)],
        compiler_params=_cparams(("arbitrary", "arbitrary", "arbitrary")),
        name="diff_attn",
    )(proj, proj, proj, proj, bias, lam_params, norm_w)


def _gla_body(q_ref, k_ref, gd_ref, v_ref, r_ref, wg_ref, bg_ref, nw_ref, o_ref, state, part):
    n = pl.program_id(1)

    @pl.when(n == 0)
    def _():
        state[...] = jnp.zeros_like(state)

    rows = GLA_GROUP
    nc = rows // CHUNK
    npair = GLA_HEADS // 2
    gate = jnp.dot(gd_ref[...], wg_ref[...], precision=HI, preferred_element_type=F32) + bg_ref[...]
    log_a = _log_sigmoid(gate) * (1.0 / GLA_TAU)

    ri = lax.broadcasted_iota(jnp.int32, (rows, rows), 0)
    ci = lax.broadcasted_iota(jnp.int32, (rows, rows), 1)
    causal = ((ri // CHUNK) == (ci // CHUNK)) & (ri >= ci)
    tri = causal.astype(BF16)
    hi = log_a.astype(BF16)
    rem = log_a - hi.astype(F32)
    mid = rem.astype(BF16)
    lo = (rem - mid.astype(F32)).astype(BF16)
    b_all = (jnp.dot(tri, hi, preferred_element_type=F32) + jnp.dot(tri, mid, preferred_element_type=F32)
             + jnp.dot(tri, lo, preferred_element_type=F32))

    lane = lax.broadcasted_iota(jnp.int32, (rows, LANES), 1)
    lane_c = lax.broadcasted_iota(jnp.int32, (CHUNK, LANES), 1)
    head_lanes = (lane < GLA_DK, lane >= GLA_DK)
    chunk_lanes = (lane_c < GLA_DK, lane_c >= GLA_DK)
    pairs = []
    for p in range(npair):
        cols = slice(p * LANES, (p + 1) * LANES)
        pairs.append(dict(
            b=b_all[:, cols],
            q=q_ref[:, cols].astype(F32) * GLA_DK ** -0.5,
            k=k_ref[:, cols].astype(F32),
            v=[v_ref[:, (2 * p + hd) * GLA_DV:(2 * p + hd + 1) * GLA_DV].astype(F32) for hd in range(2)]))

    for pr in pairs:
        pr["q_dec"] = pr["q"] * jnp.exp(pr["b"])
    for c in range(nc):
        sl = slice(c * CHUNK, (c + 1) * CHUNK)
        for p, pr in enumerate(pairs):
            bc = pr["b"][sl]
            b_last = bc[CHUNK - 1:CHUNK]
            k_dec = pr["k"][sl] * jnp.exp(b_last - bc)
            e_last = jnp.exp(b_last)
            for hd in range(2):
                h = 2 * p + hd
                st = state[h]
                part[h, sl, :] = _bdot_nt(jnp.where(chunk_lanes[hd], pr["q_dec"][sl], 0.0), st)
                state[h] = st * e_last + _bdot_tn(pr["v"][hd][sl], k_dec)

    def finish(intra):
        for h in range(GLA_HEADS):
            o = part[h] + intra[h]
            o = o * lax.rsqrt(jnp.mean(o * o, axis=-1, keepdims=True) + RMS_EPS) * nw_ref[...]
            gate_r = _silu(r_ref[:, h * GLA_DV:(h + 1) * GLA_DV].astype(F32))
            o_ref[:, h * GLA_DV:(h + 1) * GLA_DV] = (o * gate_r).astype(o_ref.dtype)

    def intra_whole_chunk():
        out = []
        for pr in pairs:
            k_inv = pr["k"] * jnp.exp(jnp.minimum(-pr["b"], GLA_MAX_DECAY))
            for hd in range(2):
                a = _bdot_nt(jnp.where(head_lanes[hd], pr["q_dec"], 0.0), k_inv)
                out.append(_bdot(jnp.where(causal, a, 0.0), pr["v"][hd]))
        return out

    def intra_exact():
        out = []
        for pr in pairs:
            off = _gla_intra_off_diagonal(pr["q"], pr["k"], pr["b"], pr["v"], nc)
            diag = _gla_intra_diagonal(pr["q"], pr["k"], pr["b"], pr["v"])
            out += [off[hd] + diag[hd] for hd in range(2)]
        return out

    chunk_decay = jnp.max(-b_all.reshape(nc, CHUNK, npair * LANES)[:, CHUNK - 1:CHUNK, :])

    @pl.when(chunk_decay <= GLA_MAX_DECAY)
    def _():
        finish(intra_whole_chunk())

    @pl.when(chunk_decay > GLA_MAX_DECAY)
    def _():
        finish(intra_exact())


def _gla_intra_off_diagonal(q, k, b, vs, nc):
    per_chunk = CHUNK // GLA_SUB
    lane = lax.broadcasted_iota(jnp.int32, (GLA_SUB, LANES), 1)
    sub_mask = (lane < GLA_DK, lane >= GLA_DK)
    kcol = lax.broadcasted_iota(jnp.int32, (GLA_SUB, CHUNK), 1)
    outs = ([], [])
    for c in range(nc):
        sl = slice(c * CHUNK, (c + 1) * CHUNK)
        bc, qc, kc = b[sl], q[sl], k[sl]
        a_rows = [[jnp.zeros((GLA_SUB, CHUNK), F32)] for _ in vs]
        for blk in range(1, per_chunk):
            r0 = blk * GLA_SUB
            bref = bc[r0:r0 + 1]
            qs = qc[r0:r0 + GLA_SUB] * jnp.exp(bc[r0:r0 + GLA_SUB] - bref)
            ks = kc * jnp.exp(jnp.minimum(bref - bc, 0.0))
            for hd in range(2):
                a = _bdot_nt(jnp.where(sub_mask[hd], qs, 0.0), ks)
                a_rows[hd].append(jnp.where(kcol < r0, a, 0.0))
        for hd in range(2):
            outs[hd].append(_bdot(jnp.concatenate(a_rows[hd], axis=0), vs[hd][sl]))
    return [jnp.concatenate(o, axis=0) for o in outs]


def _gla_intra_diagonal(q, k, b, vs):
    rows = q.shape[0]
    nsub = rows // GLA_SUB
    b3 = b.reshape(nsub, GLA_SUB, LANES)
    q3 = q.reshape(nsub, GLA_SUB, LANES)
    k3 = k.reshape(nsub, GLA_SUB, LANES)
    v3 = [v.reshape(nsub, GLA_SUB, LANES) for v in vs]
    row3 = lax.broadcasted_iota(jnp.int32, (nsub, GLA_SUB, LANES), 1)
    lane3 = lax.broadcasted_iota(jnp.int32, (nsub, GLA_SUB, LANES), 2)
    rowc = lax.broadcasted_iota(jnp.int32, (nsub, GLA_SUB, 1), 1)
    o3 = [jnp.zeros((nsub, GLA_SUB, LANES), F32) for _ in vs]
    for jj in range(GLA_SUB):
        e = jnp.exp(jnp.where(row3 >= jj, b3 - b3[:, jj:jj + 1, :], 0.0))
        t = q3 * k3[:, jj:jj + 1, :] * e
        w_all = jnp.sum(t, axis=-1, keepdims=True)
        w_a = jnp.sum(jnp.where(lane3 < GLA_DK, t, 0.0), axis=-1, keepdims=True)
        for hd, w in enumerate((w_a, w_all - w_a)):
            o3[hd] = o3[hd] + jnp.where(rowc >= jj, w, 0.0) * v3[hd][:, jj:jj + 1, :]
    return [o.reshape(rows, LANES) for o in o3]


def _gla(proj, gate_in, w_gate_pad, b_gate, norm_w, bsz, seq):
    t = proj.shape[0]
    rows = GLA_GROUP
    spb = seq // rows
    qk_w = GLA_HEADS * GLA_DK
    v_w = GLA_HEADS * GLA_DV

    def at(col):
        return lambda b, n: (b * spb + n, col)

    def whole(a):
        return pl.BlockSpec(a.shape, lambda b, n: (0, 0))

    return pl.pallas_call(
        _gla_body,
        grid=(bsz, spb),
        in_specs=[pl.BlockSpec((rows, qk_w), at(OD_QD // qk_w)),
                  pl.BlockSpec((rows, qk_w), at(OD_KD // qk_w)),
                  pl.BlockSpec((rows, LANES), at(0)),
                  pl.BlockSpec((rows, v_w), at(OD_VD // v_w)),
                  pl.BlockSpec((rows, v_w), at(OD_RD // v_w)),
                  whole(w_gate_pad), whole(b_gate), whole(norm_w)],
        out_specs=pl.BlockSpec((rows, v_w), at(0)),
        out_shape=jax.ShapeDtypeStruct((t, v_w), BF16),
        scratch_shapes=[pltpu.VMEM((GLA_HEADS, GLA_DV, LANES), F32), pltpu.VMEM((GLA_HEADS, rows, GLA_DV), F32)],
        compiler_params=_cparams(("parallel", "arbitrary")),
        name="gla",
    )(proj, proj, gate_in, proj, proj, w_gate_pad, b_gate, norm_w)


def _ffn_body(tiles_per_seq, tc, x_ref, halo_ref, wu_ref, cw_ref, cb_ref, wd_ref, g_ref, b_ref, y_ref,
              act, scratch_g, scratch_v):
    i = pl.program_id(0)
    x = x_ref[...]
    halo = jnp.where(i % tiles_per_seq == 0, 0.0, halo_ref[...])
    xb = jnp.concatenate([halo.astype(BF16), x.astype(BF16)], axis=0)

    def branch(lo, scratch):
        full = jnp.dot(xb, wu_ref[:, lo:lo + tc], preferred_element_type=F32)
        return _causal_conv(full, cw_ref[:, lo:lo + tc], scratch) + cb_ref[:, lo:lo + tc]

    for c in range(D_FF // tc):
        gate = branch(c * tc, scratch_g)
        val = branch(D_FF + c * tc, scratch_v)
        act[:, c * tc:(c + 1) * tc] = (_silu(gate) * val).astype(BF16)
    ffn = jnp.dot(act[...], wd_ref[...], preferred_element_type=F32)
    y_ref[...] = _layer_norm(DEEPNORM_ALPHA * x + ffn, g_ref[...], b_ref[...])


def _ffn(x, w_up, conv_w, conv_b, w_down, g, b, seq, tm, tc):
    t = x.shape[0]
    hb = tm // HALO

    def resident(a):
        return pl.BlockSpec(a.shape, lambda i: (0, 0), pipeline_mode=pl.Buffered(1))

    return pl.pallas_call(
        functools.partial(_ffn_body, seq // tm, tc),
        grid=(t // tm,),
        in_specs=[pl.BlockSpec((tm, D_MODEL), lambda i: (i, 0)),
                  pl.BlockSpec((HALO, D_MODEL), lambda i: (jnp.maximum(i * hb - 1, 0), 0)),
                  resident(w_up), resident(conv_w), resident(conv_b), resident(w_down),
                  resident(g), resident(b)],
        out_specs=pl.BlockSpec((tm, D_MODEL), lambda i: (i, 0)),
        out_shape=jax.ShapeDtypeStruct((t, D_MODEL), F32),
        scratch_shapes=[pltpu.VMEM((tm, D_FF), BF16), pltpu.VMEM((HALO + tm, tc), F32),
                        pltpu.VMEM((HALO + tm, tc), F32)],
        compiler_params=_cparams(("parallel",)),
        name="conv_ffn",
    )(x, x, w_up, conv_w, conv_b, w_down, g, b)


def _even_w_in(w):
    a_end = 4 * EVA_PART
    gates = w[:, a_end:a_end + 2 * GDN_HEADS]
    qkv_b = w[:, a_end + 2 * GDN_HEADS:]
    pad = jnp.zeros((w.shape[0], EVB_COLS - EVB_BA - 2 * GDN_HEADS), w.dtype)
    return w[:, :a_end].astype(BF16), jnp.concatenate([qkv_b, gates, pad], axis=1).astype(BF16)


def _odd_w_in(w):
    pad = jnp.zeros((w.shape[0], LANES - GLA_RANK), w.dtype)
    return w[:, :OD_COLS].astype(BF16), jnp.concatenate([w[:, OD_COLS:], pad], axis=1).astype(BF16)


def _even_layer(h, rel_bias, w_in, conv_w, a_log, dt_bias, norm_w, w_out, ln_g, ln_b, bsz, seq):
    w_a, w_b = _even_w_in(w_in)
    proj_a = _even_in_proj(h, w_a, conv_w, seq, PROJ_TM)
    proj_b = _matmul(h, w_b, PROJ_TM, EVB_TN, F32, "even_in_proj_f32")
    gate_pad = jnp.zeros((1, LANES), F32)
    alog_pad = lax.dynamic_update_slice(gate_pad, a_log[None].astype(F32), (0, GDN_HEADS))
    dtb_pad = lax.dynamic_update_slice(gate_pad, dt_bias[None].astype(F32), (0, GDN_HEADS))
    o_a = _gdn(proj_a, proj_b, alog_pad, dtb_pad, norm_w[None], bsz, seq)
    tiles = _bias_tiles(rel_bias, *_swa_bias_tables())
    ng, two_blk = len(SWA_CONFIGS), 2 * SWA_BLOCK
    bias = tiles.reshape(SWA_HEADS // 2, 2, ng, SWA_BLOCK, two_blk).transpose(2, 0, 1, 3, 4)
    o_b = _swa(proj_b, bias.reshape(ng, SWA_HEADS // 2, two_blk, two_blk), bsz, seq)
    w_out = w_out.astype(BF16)
    return _out_proj(_out_body, [o_a, o_b], [w_out[:EVA_PART], w_out[EVA_PART:]],
                     h, ln_g[None], ln_b[None], 512, "even_out")


def _odd_layer(h, rel_bias, w_in, lam_params, diff_norm_w, w_gate, b_gate, gla_norm_w, w_out,
               ln_g, ln_b, lam_init, bsz, seq):
    w_main, w_gd = _odd_w_in(w_in)
    proj = _matmul(h, w_main, PROJ_TM, OD_TN, BF16, "odd_in_proj")
    gate_in = _matmul(h, w_gd, PROJ_TM, LANES, F32, "odd_gate_proj")
    blk = min(DIFF_BLOCK, seq)
    bucket, neg = _diff_bias_tables(blk)
    bias = _bias_tiles(rel_bias, bucket, neg, scale=LOG2E, base_bucket=NUM_BUCKETS - 1)
    o_c = _diff_attention(proj, bias, lam_params, diff_norm_w[None], lam_init, bsz, seq, blk)
    w_gate_pad = jnp.concatenate(
        [w_gate, jnp.zeros((LANES - GLA_RANK, w_gate.shape[1]), w_gate.dtype)], axis=0)
    o_d = _gla(proj, gate_in, w_gate_pad, b_gate[None], gla_norm_w[None], bsz, seq)
    diff_v = DIFF_HEADS * 2 * DIFF_DH
    w_out = w_out.astype(BF16)
    return _out_proj(_out_body, [o_c, o_d], [w_out[:diff_v], w_out[diff_v:]],
                     h, ln_g[None], ln_b[None], 512, "odd_out")


def kernel(x, rel_bias, w_in_even, gdn_conv_w, gdn_a_log, gdn_dt_bias, gdn_norm_w, w_out_even,
           w_in_odd, diff_lambda, diff_norm_w, gla_w_gate, gla_b_gate, gla_norm_w, w_out_odd,
           ffn_w_up, ffn_conv_w, ffn_conv_b, ffn_w_down, ln_g, ln_b):
    bsz, seq, d = x.shape
    h = x.reshape(bsz * seq, d)
    for layer in range(DEPTH):
        i = layer // 2
        if layer % 2 == 0:
            h = _even_layer(h, rel_bias, w_in_even[i], gdn_conv_w[i], gdn_a_log[i], gdn_dt_bias[i],
                            gdn_norm_w[i], w_out_even[i], ln_g[layer, 0], ln_b[layer, 0], bsz, seq)
        else:
            lam_init = 0.8 - 0.6 * math.exp(-0.3 * layer)
            h = _odd_layer(h, rel_bias, w_in_odd[i], diff_lambda[i], diff_norm_w[i], gla_w_gate[i],
                           gla_b_gate[i], gla_norm_w[i], w_out_odd[i], ln_g[layer, 0], ln_b[layer, 0],
                           lam_init, bsz, seq)
        h = _ffn(h, ffn_w_up[layer].astype(BF16), ffn_conv_w[layer], ffn_conv_b[layer][None],
                 ffn_w_down[layer].astype(BF16), ln_g[layer, 1][None], ln_b[layer, 1][None],
                 seq, 512, 256)
    return h.reshape(bsz, seq, d)
```

```python
import functools
import math

import numpy as np
import jax
import jax.numpy as jnp
from jax import lax
from jax.experimental import pallas as pl
from jax.experimental.pallas import tpu as pltpu

F32 = jnp.float32
BF16 = jnp.bfloat16
HI = lax.Precision.HIGHEST

D_MODEL = 1024
DEPTH = 2
DEEPNORM_ALPHA = (2 * DEPTH) ** 0.25
LN_EPS = 1e-5
RMS_EPS = 1e-6
NUM_BUCKETS = 32
REL_MAX_DIST = 2048
GDN_HEADS = 6
GDN_D = 128
GDN_CONV = 4
CHUNK = 64
GDN_GROUP = 256
SWA_CONFIGS = ((128, 1), (512, 4), (2048, 16))
SWA_HEADS = 4
SWA_DH = 64
SWA_BLOCK = 128
SWA_UNROLL = 8
SWA_SPAN = SWA_BLOCK * max(d for _, d in SWA_CONFIGS)
DIFF_HEADS = 4
DIFF_DH = 64
DIFF_BLOCK = 512
LOG2E = math.log2(math.e)
GLA_HEADS = 4
GLA_DK = 64
GLA_DV = 128
GLA_RANK = 16
GLA_TAU = 16.0
GLA_SUB = 16
GLA_GROUP = 256
GLA_MAX_DECAY = 60.0
D_FF = 2816
FFN_CONV = 3

LANES = 128
SUBLANES = 8
HALO = 16
VMEM_LIMIT = 56 * 1024 * 1024
NEG_BIG = -1e30

EVA_PART = GDN_HEADS * GDN_D
EVB_BA = 2304
EVB_COLS = 2560
EVB_TN = 512
OD_QC, OD_KC, OD_VC = 0, 512, 1024
OD_QD, OD_KD, OD_VD, OD_RD = 1536, 1792, 2048, 2560
OD_COLS = 3072
OD_TN = 768
PROJ_TM = 512


def _cparams(sem):
    return pltpu.CompilerParams(dimension_semantics=sem, vmem_limit_bytes=VMEM_LIMIT)


def _bdot(a, b):
    return jnp.dot(a.astype(BF16), b.astype(BF16), preferred_element_type=F32)


def _bdot_nt(a, b):
    return lax.dot_general(a.astype(BF16), b.astype(BF16), (((1,), (1,)), ((), ())),
                           preferred_element_type=F32)


def _bdot_tn(a, b):
    return lax.dot_general(a.astype(BF16), b.astype(BF16), (((0,), (0,)), ((), ())),
                           preferred_element_type=F32)


def _sigmoid(x):
    return 1.0 / (1.0 + jnp.exp(-x))


def _silu(x):
    return x * _sigmoid(x)


def _softplus(x):
    return jnp.maximum(x, 0.0) + jnp.log1p(jnp.exp(-jnp.abs(x)))


def _log_sigmoid(x):
    return -_softplus(-x)


def _resident(a):
    return pl.BlockSpec(a.shape, lambda i: (0,) * a.ndim, pipeline_mode=pl.Buffered(1))


def _odd_in_body(x_ref, w_ref, wg_ref, o_ref, og_ref):
    xb = x_ref[...].astype(BF16)
    for c in range(OD_COLS // OD_TN):
        cols = slice(c * OD_TN, (c + 1) * OD_TN)
        o_ref[:, cols] = jnp.dot(xb, w_ref[:, cols], preferred_element_type=F32).astype(o_ref.dtype)
    og_ref[...] = jnp.dot(xb, wg_ref[...], preferred_element_type=F32)


def _odd_in_proj(x, w, w_gate_in, tm):
    t, k = x.shape
    return pl.pallas_call(
        _odd_in_body,
        grid=(t // tm,),
        in_specs=[pl.BlockSpec((tm, k), lambda i: (i, 0)), _resident(w), _resident(w_gate_in)],
        out_specs=[pl.BlockSpec((tm, OD_COLS), lambda i: (i, 0)), pl.BlockSpec((tm, LANES), lambda i: (i, 0))],
        out_shape=[jax.ShapeDtypeStruct((t, OD_COLS), BF16), jax.ShapeDtypeStruct((t, LANES), F32)],
        compiler_params=_cparams(("parallel",)),
        name="odd_in_proj",
    )(x, w, w_gate_in)


def _rel_bucket_np(dist):
    max_exact = NUM_BUCKETS // 2
    d = np.maximum(dist, 1).astype(np.float32)
    large = max_exact + (np.log(d / max_exact) / math.log(REL_MAX_DIST / max_exact)
                         * (NUM_BUCKETS - max_exact)).astype(np.int32)
    large = np.minimum(large, NUM_BUCKETS - 1)
    return np.where(dist < max_exact, dist, large).astype(np.int32)


def _bias_body(scale, base_bucket, rb_ref, bucket_ref, neg_ref, o_ref):
    h = pl.program_id(0)
    bucket = bucket_ref[0]
    base = 0.0 if base_bucket is None else rb_ref[base_bucket, h]
    acc = neg_ref[0]
    for b in range(NUM_BUCKETS):
        acc = acc + jnp.where(bucket == b, (rb_ref[b, h] - base) * scale, 0.0)
    o_ref[0, 0] = acc


def _bias_tiles(rel_bias, bucket, neg, scale=1.0, base_bucket=None):
    nt, r, c = bucket.shape
    nh = rel_bias.shape[1]
    return pl.pallas_call(
        functools.partial(_bias_body, scale, base_bucket),
        grid=(nh, nt),
        in_specs=[pl.BlockSpec(memory_space=pltpu.SMEM),
                  pl.BlockSpec((1, r, c), lambda h, t: (t, 0, 0)),
                  pl.BlockSpec((1, r, c), lambda h, t: (t, 0, 0))],
        out_specs=pl.BlockSpec((1, 1, r, c), lambda h, t: (h, t, 0, 0)),
        out_shape=jax.ShapeDtypeStruct((nh, nt, r, c), F32),
        compiler_params=_cparams(("parallel", "parallel")),
        name="rel_bias_tiles",
    )(rel_bias, jnp.asarray(bucket), jnp.asarray(neg))


def _causal_conv(full, w, scratch):
    width = w.shape[0]
    rows = full.shape[0] - HALO
    scratch[...] = full
    y = w[width - 1:width, :] * full[HALO:]
    for j in range(width - 1):
        back = width - 1 - j
        y = y + w[j:j + 1, :] * scratch[HALO - back:HALO - back + rows, :]
    return y


def _even_in_body(tiles_per_seq, x_ref, halo_ref, wa_ref, wb_ref, cw_ref, oa_ref, ob_ref, scratch):
    i = pl.program_id(0)
    halo = jnp.where(i % tiles_per_seq == 0, 0.0, halo_ref[...])
    x = x_ref[...].astype(BF16)
    x_ext = jnp.concatenate([halo.astype(BF16), x], axis=0)
    for part in range(3):
        cols = slice(part * EVA_PART, (part + 1) * EVA_PART)
        y = jnp.dot(x_ext, wa_ref[:, cols], preferred_element_type=F32)
        c = _silu(_causal_conv(y, cw_ref[:, cols], scratch))
        for hd in range(GDN_HEADS):
            ch = c[:, hd * GDN_D:(hd + 1) * GDN_D]
            if part < 2:
                inv = lax.rsqrt(jnp.sum(ch * ch, axis=-1, keepdims=True) + RMS_EPS)
                ch = ch * (inv * GDN_D ** -0.5 if part == 0 else inv)
            lo = part * EVA_PART + hd * GDN_D
            oa_ref[:, lo:lo + GDN_D] = ch.astype(oa_ref.dtype)
    z_cols = slice(3 * EVA_PART, 4 * EVA_PART)
    oa_ref[:, z_cols] = jnp.dot(x, wa_ref[:, z_cols], preferred_element_type=F32).astype(oa_ref.dtype)
    for c in range(EVB_COLS // EVB_TN):
        cols = slice(c * EVB_TN, (c + 1) * EVB_TN)
        ob_ref[:, cols] = jnp.dot(x, wb_ref[:, cols], preferred_element_type=F32)


def _even_in_proj(x, w_a, w_b, conv_w, seq, tm):
    t, k = x.shape
    hb = tm // HALO
    return pl.pallas_call(
        functools.partial(_even_in_body, seq // tm),
        grid=(t // tm,),
        in_specs=[pl.BlockSpec((tm, k), lambda i: (i, 0)),
                  pl.BlockSpec((HALO, k), lambda i: (jnp.maximum(i * hb - 1, 0), 0)),
                  _resident(w_a), _resident(w_b), _resident(conv_w)],
        out_specs=[pl.BlockSpec((tm, w_a.shape[1]), lambda i: (i, 0)),
                   pl.BlockSpec((tm, EVB_COLS), lambda i: (i, 0))],
        out_shape=[jax.ShapeDtypeStruct((t, w_a.shape[1]), BF16), jax.ShapeDtypeStruct((t, EVB_COLS), F32)],
        scratch_shapes=[pltpu.VMEM((HALO + tm, EVA_PART), F32)],
        compiler_params=_cparams(("parallel",)),
        name="even_in_proj",
    )(x, x, w_a, w_b, conv_w)


def _gdn_body(q_ref, k_ref, v_ref, z_ref, ba_ref, alog_ref, dtb_ref, nw_ref, o_ref, state):
    n = pl.program_id(1)

    @pl.when(n == 0)
    def _():
        state[...] = jnp.zeros_like(state)

    grp = GDN_GROUP
    nc = grp // CHUNK
    nh = GDN_HEADS
    ba = ba_ref[...]
    beta_all = _sigmoid(ba)
    g_all = -jnp.exp(alog_ref[...]) * _softplus(ba + dtb_ref[...])

    ri = lax.broadcasted_iota(jnp.int32, (grp, grp), 0)
    ci = lax.broadcasted_iota(jnp.int32, (grp, grp), 1)
    same = (ri // CHUNK) == (ci // CHUNK)
    incl = same & (ri >= ci)
    strict = same & (ri > ci)
    eye = (ri == ci).astype(F32)
    tri = incl.astype(BF16)
    g_hi = g_all.astype(BF16)
    rem = g_all - g_hi.astype(F32)
    g_mid = rem.astype(BF16)
    g_lo = (rem - g_mid.astype(F32)).astype(BF16)
    gam_all = (jnp.dot(tri, g_hi, preferred_element_type=F32) + jnp.dot(tri, g_mid, preferred_element_type=F32)
               + jnp.dot(tri, g_lo, preferred_element_type=F32))
    gam_rows = gam_all.T

    hs = range(nh)
    q = [q_ref[:, h * GDN_D:(h + 1) * GDN_D] for h in hs]
    k = [k_ref[:, h * GDN_D:(h + 1) * GDN_D] for h in hs]
    v = [v_ref[:, h * GDN_D:(h + 1) * GDN_D] for h in hs]
    gam = [jnp.broadcast_to(gam_all[:, nh + h:nh + h + 1], (grp, GDN_D)) for h in hs]
    bcol = [beta_all[:, h:h + 1] for h in hs]
    decay, x, inv = [], [], []
    for h in hs:
        diff = gam[h][:, 0:1] - gam_rows[nh + h:nh + h + 1, :]
        decay.append(jnp.where(incl, jnp.exp(jnp.where(incl, diff, 0.0)), 0.0))
        kk = _bdot_nt(k[h], k[h])
        x.append(-jnp.where(strict, bcol[h] * kk * decay[h], 0.0))
        inv.append(eye + x[h])
    for _ in range(5):
        for h in hs:
            x[h] = _bdot(x[h], x[h])
            inv[h] = inv[h] + _bdot(inv[h], x[h])
    u, w, qk, q_dec, k_dec, g_last = [], [], [], [], [], []
    for h in hs:
        eg = jnp.exp(gam[h])
        uw = _bdot(inv[h], jnp.concatenate([v[h] * bcol[h], k[h] * (bcol[h] * eg)], axis=1))
        u.append(uw[:, :GDN_D])
        w.append(uw[:, GDN_D:])
        qk.append(_bdot_nt(q[h], k[h]) * decay[h])
        q_dec.append(q[h] * eg)
        kd, gl = [], []
        for c in range(nc):
            last = gam[h][(c + 1) * CHUNK - 1:(c + 1) * CHUNK, :]
            kd.append(k[h][c * CHUNK:(c + 1) * CHUNK] * jnp.exp(last - gam[h][c * CHUNK:(c + 1) * CHUNK]))
            gl.append(jnp.exp(last))
        k_dec.append(kd)
        g_last.append(gl)

    s = [state[h] for h in hs]
    q_s = [[] for _ in hs]
    delta = [[] for _ in hs]
    for c in range(nc):
        sl = slice(c * CHUNK, (c + 1) * CHUNK)
        for h in hs:
            r = _bdot(jnp.concatenate([w[h][sl], q_dec[h][sl]], axis=0), s[h])
            d = u[h][sl] - r[:CHUNK]
            q_s[h].append(r[CHUNK:])
            delta[h].append(d)
            s[h] = g_last[h][c] * s[h] + _bdot_tn(k_dec[h][c], d)
    for h in hs:
        state[h] = s[h]
        o = jnp.concatenate(q_s[h], axis=0) + _bdot(qk[h], jnp.concatenate(delta[h], axis=0))
        o = o * lax.rsqrt(jnp.mean(o * o, axis=-1, keepdims=True) + RMS_EPS) * nw_ref[...]
        z = z_ref[:, h * GDN_D:(h + 1) * GDN_D].astype(F32)
        o_ref[:, h * GDN_D:(h + 1) * GDN_D] = (o * _silu(z)).astype(o_ref.dtype)


def _gdn(proj_a, proj_b, alog_pad, dtb_pad, norm_w, bsz, seq):
    t = proj_a.shape[0]
    rows = GDN_GROUP
    spb = seq // rows
    width = GDN_HEADS * GDN_D

    def at(col):
        return lambda b, n: (b * spb + n, col)

    return pl.pallas_call(
        _gdn_body,
        grid=(bsz, spb),
        in_specs=[pl.BlockSpec((rows, width), at(0)),
                  pl.BlockSpec((rows, width), at(1)),
                  pl.BlockSpec((rows, width), at(2)),
                  pl.BlockSpec((rows, width), at(3)),
                  pl.BlockSpec((rows, LANES), at(EVB_BA // LANES)),
                  pl.BlockSpec((1, LANES), lambda b, n: (0, 0)),
                  pl.BlockSpec((1, LANES), lambda b, n: (0, 0)),
                  pl.BlockSpec((1, LANES), lambda b, n: (0, 0))],
        out_specs=pl.BlockSpec((rows, width), at(0)),
        out_shape=jax.ShapeDtypeStruct((t, width), BF16),
        scratch_shapes=[pltpu.VMEM((GDN_HEADS, GDN_D, GDN_D), F32)],
        compiler_params=_cparams(("parallel", "arbitrary")),
        name="gdn",
    )(proj_a, proj_a, proj_a, proj_a, proj_b, alog_pad, dtb_pad, norm_w)


def _swa_bias_tables():
    qi = np.arange(SWA_BLOCK)[:, None] + SWA_BLOCK
    kj = np.arange(2 * SWA_BLOCK)[None, :]
    rel = qi - kj
    buckets, negs = [], []
    for window, dilation in SWA_CONFIGS:
        buckets.append(_rel_bucket_np(np.maximum(rel, 0) * dilation))
        negs.append(np.where((rel >= 0) & (rel <= window // dilation), 0.0, NEG_BIG))
    return np.stack(buckets).astype(np.int32), np.stack(negs).astype(np.float32)


def _swa_body(*refs):
    ng = len(SWA_CONFIGS)
    ins, bias_ref, o_ref = refs[:5 * ng], refs[5 * ng], refs[5 * ng + 1]
    scratch = refs[5 * ng + 2:]
    kbufs, vbufs, o_scr, lse_scr = scratch[:ng], scratch[ng:2 * ng], scratch[2 * ng], scratch[2 * ng + 1]
    j = pl.program_id(1)
    blk = SWA_BLOCK
    first_head = lax.broadcasted_iota(jnp.int32, (blk, LANES), 1) < SWA_DH
    in_prev = lax.broadcasted_iota(jnp.int32, (2 * blk, 2 * blk), 1) < blk

    for g, (_, d) in enumerate(SWA_CONFIGS):
        q_ref, kc_ref, kp_ref, vc_ref, vp_ref = ins[5 * g:5 * g + 5]
        kbuf, vbuf = kbufs[g], vbufs[g]
        halo = blk * d
        kbuf[0:halo, :] = kp_ref[...]
        kbuf[halo:, :] = kc_ref[...]
        vbuf[0:halo, :] = vp_ref[...]
        vbuf[halo:, :] = vc_ref[...]

        def unit(u, carry, g=g, d=d, halo=halo, q_ref=q_ref, kbuf=kbuf, vbuf=vbuf):
            base = (u // d) * halo + u % d
            q = q_ref[pl.ds(base, blk, stride=d), :] * SWA_DH ** -0.5
            k = kbuf[pl.ds(base, 2 * blk, stride=d), :]
            v = vbuf[pl.ds(base, 2 * blk, stride=d), :]
            lhs = jnp.concatenate([jnp.where(first_head, q, 0.0), jnp.where(first_head, 0.0, q)], axis=0)
            s = _bdot_nt(lhs, k) + bias_ref[g, 0]
            no_prev = (j == 0) & (u < d)
            s = jnp.where(in_prev & no_prev, NEG_BIG, s)
            m = jnp.max(s, axis=-1, keepdims=True)
            p = jnp.exp(s - m)
            l = jnp.sum(p, axis=-1, keepdims=True)
            o2 = _bdot(p / l, v)
            lse2 = m + jnp.log(l)
            o_scr[g, pl.ds(base, blk, stride=d), :] = jnp.where(first_head, o2[:blk], o2[blk:])
            lse_scr[g, pl.ds(base, blk, stride=d), :] = jnp.where(first_head, lse2[:blk], lse2[blk:])
            return carry

        lax.fori_loop(0, SWA_SPAN // blk, unit, 0, unroll=SWA_UNROLL)

    def combine(c, carry):
        rows = pl.ds(pl.multiple_of(c * 2 * blk, 2 * blk), 2 * blk)
        lse = [lse_scr[g, rows, :] for g in range(ng)]
        m = functools.reduce(jnp.maximum, lse)
        e = [jnp.exp(x - m) for x in lse]
        den = functools.reduce(lambda x, y: x + y, e)
        o_ref[rows, :] = functools.reduce(
            lambda x, y: x + y, [(e[g] / den) * o_scr[g, rows, :] for g in range(ng)]).astype(o_ref.dtype)
        return carry

    lax.fori_loop(0, SWA_SPAN // (2 * blk), combine, 0)


def _swa(proj, bias, bsz, seq):
    t = proj.shape[0]
    ng = len(SWA_CONFIGS)
    nspan = seq // SWA_SPAN
    npair = SWA_HEADS * SWA_DH // LANES
    group_cols = SWA_HEADS * SWA_DH // LANES
    in_specs, scratch_k = [], []
    for g, (_, d) in enumerate(SWA_CONFIGS):
        halo = SWA_BLOCK * d
        per_span = SWA_SPAN // halo

        def cur(which, g=g):
            col = (which * ng + g) * group_cols
            return lambda b, j, p: (b * nspan + j, col + p)

        def prev(which, g=g, per_span=per_span):
            col = (which * ng + g) * group_cols
            return lambda b, j, p: (jnp.maximum((b * nspan + j) * per_span - 1, 0), col + p)

        in_specs += [pl.BlockSpec((SWA_SPAN, LANES), cur(0)),
                     pl.BlockSpec((SWA_SPAN, LANES), cur(1)), pl.BlockSpec((halo, LANES), prev(1)),
                     pl.BlockSpec((SWA_SPAN, LANES), cur(2)), pl.BlockSpec((halo, LANES), prev(2))]
        scratch_k.append(pltpu.VMEM((halo + SWA_SPAN, LANES), F32))
    in_specs.append(pl.BlockSpec((ng, 1, 2 * SWA_BLOCK, 2 * SWA_BLOCK), lambda b, j, p: (0, p, 0, 0)))
    return pl.pallas_call(
        _swa_body,
        grid=(bsz, nspan, npair),
        in_specs=in_specs,
        out_specs=pl.BlockSpec((SWA_SPAN, LANES), lambda b, j, p: (b * nspan + j, p)),
        out_shape=jax.ShapeDtypeStruct((t, npair * LANES), BF16),
        scratch_shapes=scratch_k + scratch_k + [pltpu.VMEM((ng, SWA_SPAN, LANES), F32)] * 2,
        compiler_params=_cparams(("parallel", "parallel", "parallel")),
        name="swa",
    )(*([proj] * (5 * ng)), bias)


def _layer_norm(y, g, b):
    mu = jnp.mean(y, axis=-1, keepdims=True)
    yc = y - mu
    var = jnp.mean(yc * yc, axis=-1, keepdims=True)
    return yc * lax.rsqrt(var + LN_EPS) * g + b


def _out_body(xa_ref, xb_ref, wa_ref, wb_ref, h_ref, g_ref, b_ref, y_ref):
    mix = _bdot(xa_ref[...], wa_ref[...]) + _bdot(xb_ref[...], wb_ref[...])
    y_ref[...] = _layer_norm(DEEPNORM_ALPHA * h_ref[...] + mix, g_ref[...], b_ref[...])


def _out_proj(body, acts, weights, h, g, b, tm, name):
    t = h.shape[0]

    def rowblk(a):
        return pl.BlockSpec((tm, a.shape[1]), lambda i: (i, 0))

    def full(a):
        return pl.BlockSpec(a.shape, lambda i: (0, 0))

    return pl.pallas_call(
        body,
        grid=(t // tm,),
        in_specs=[rowblk(a) for a in acts] + [full(w) for w in weights] + [rowblk(h), full(g), full(b)],
        out_specs=pl.BlockSpec((tm, D_MODEL), lambda i: (i, 0)),
        out_shape=jax.ShapeDtypeStruct((t, D_MODEL), F32),
        compiler_params=_cparams(("parallel",)),
        name=name,
    )(*acts, *weights, h, g, b)


def _diff_bias_tables(blk):
    buckets = _rel_bucket_np(np.arange(2 * REL_MAX_DIST))
    far = int(np.max(np.nonzero(buckets != NUM_BUCKETS - 1)[0])) + 1
    nb = -(-(far + blk - 1) // blk)
    kk = np.arange(blk)[:, None]
    qq = np.arange(blk)[None, :]
    dist = np.stack([t * blk + qq - kk for t in range(-1, nb + 1)])
    bucket = _rel_bucket_np(np.maximum(dist, 0))
    neg = np.where(dist >= 0, 0.0, NEG_BIG).astype(np.float32)
    return bucket, neg


def _diff_body(blk, nb, lam_init, q_ref, qn_ref, k_ref, v_ref, bias_ref, lam_ref, nw_ref, o_ref,
               vt, acc1, acc2, s_a, s_b):
    qi = pl.program_id(2)
    dv = 2 * DIFF_DH
    seq = k_ref.shape[0]

    @pl.when(qi == 0)
    def _():
        vt[dv:, :] = jnp.ones((vt.shape[0] - dv, seq), BF16)

        def fill(c, carry):
            st = pl.multiple_of(c * blk, blk)
            vt[0:dv, pl.ds(st, blk)] = v_ref[pl.ds(st, blk), :].astype(F32).T.astype(BF16)
            return carry

        lax.fori_loop(0, seq // blk, fill, 0)

    lane = lax.broadcasted_iota(jnp.int32, (blk, dv), 1)

    def components(ref):
        q = ref[...].astype(F32) * (DIFF_DH ** -0.5 * LOG2E)
        return (jnp.where(lane < DIFF_DH, q, 0.0).astype(BF16), jnp.where(lane >= DIFF_DH, q, 0.0).astype(BF16))

    qs = components(q_ref)
    qs_next = components(qn_ref)
    accs = (acc1, acc2)
    for acc in accs:
        acc[...] = jnp.zeros_like(acc)

    last = pl.num_programs(2) - 1

    def key_rows(kj):
        return pl.ds(pl.multiple_of(jnp.minimum(kj, last) * blk, blk), blk)

    def scores(kj, dst, queries=qs):
        k = k_ref[key_rows(kj), :]
        for c, qc in enumerate(queries):
            dst[c] = lax.dot_general(k, qc, (((1,), (1,)), ((), ())), preferred_element_type=F32)

    def consume(biased, kj, src, ms):
        vtb = vt[:, key_rows(kj)]
        out = []
        for c, (m, acc) in enumerate(zip(ms, accs)):
            s = src[c]
            if biased:
                s = s + bias_ref[0, jnp.minimum(qi - kj, nb) + 1]
            m_new = jnp.maximum(m, jnp.max(s, axis=0, keepdims=True))
            p = jnp.exp2(s - m_new).astype(BF16)
            acc[...] = jnp.exp2(m - m_new) * acc[...] + jnp.dot(vtb, p, preferred_element_type=F32)
            out.append(m_new)
        return tuple(out)

    trips = (qi + 2) // 2
    far_trips = jnp.maximum(qi - nb + 1, 0) // 2

    def pair(biased, t, ms):
        kj = 2 * t
        scores(kj + 1, s_b)
        ms = consume(biased, kj, s_a, ms)
        if biased:
            is_last = t == trips - 1
            scores(jnp.where(is_last, 0, kj + 2), s_a,
                   tuple(jnp.where(is_last, qn, qc) for qc, qn in zip(qs, qs_next)))
        else:
            scores(kj + 2, s_a)
        return consume(biased, kj + 1, s_b, ms)

    @pl.when(qi == 0)
    def _():
        scores(0, s_a)

    m0 = jnp.full((1, blk), NEG_BIG, F32)
    ms = lax.fori_loop(0, far_trips, functools.partial(pair, False), (m0, m0))
    lax.fori_loop(far_trips, trips, functools.partial(pair, True), ms)

    lp = lam_ref[...]
    lam = (jnp.exp(jnp.sum(lp[0:1] * lp[1:2], axis=-1, keepdims=True))
           - jnp.exp(jnp.sum(lp[2:3] * lp[3:4], axis=-1, keepdims=True)) + lam_init)
    a1, a2 = acc1[...], acc2[...]
    o_t = a1[:dv] * (1.0 / a1[dv:dv + 1]) - a2[:dv] * (lam / a2[dv:dv + 1])
    o = o_t.T
    o = o * lax.rsqrt(jnp.mean(o * o, axis=-1, keepdims=True) + RMS_EPS) * nw_ref[...]
    o_ref[...] = (o * (1.0 - lam_init)).astype(o_ref.dtype)


def _diff_attention(proj, bias, lam_params, norm_w, lam_init, bsz, seq, blk):
    t = proj.shape[0]
    nq = seq // blk
    nt = bias.shape[1]
    nb = nt - 2
    width = 2 * DIFF_DH
    ones_rows = 2 * SUBLANES
    return pl.pallas_call(
        functools.partial(_diff_body, blk, nb, lam_init),
        grid=(bsz, DIFF_HEADS, nq),
        in_specs=[pl.BlockSpec((blk, width), lambda b, h, i: (b * nq + i, OD_QC // width + h)),
                  pl.BlockSpec((blk, width),
                               lambda b, h, i: (b * nq + jnp.minimum(i + 1, nq - 1), OD_QC // width + h)),
                  pl.BlockSpec((seq, width), lambda b, h, i: (b, OD_KC // width + h)),
                  pl.BlockSpec((seq, width), lambda b, h, i: (b, OD_VC // width + h)),
                  pl.BlockSpec((1, nt, blk, blk), lambda b, h, i: (h, 0, 0, 0)),
                  pl.BlockSpec((4, DIFF_DH), lambda b, h, i: (0, 0)),
                  pl.BlockSpec((1, width), lambda b, h, i: (0, 0))],
        out_specs=pl.BlockSpec((blk, width), lambda b, h, i: (b * nq + i, h)),
        out_shape=jax.ShapeDtypeStruct((t, DIFF_HEADS * width), BF16),
        scratch_shapes=[pltpu.VMEM((width + ones_rows, seq), BF16),
                        pltpu.VMEM((width + ones_rows, blk), F32), pltpu.VMEM((width + ones_rows, blk), F32),
                        pltpu.VMEM((2, blk, blk), F32), pltpu.VMEM((2, blk, blk), F32)],
        compiler_params=_cparams(("arbitrary", "arbitrary", "arbitrary")),
        name="diff_attn",
    )(proj, proj, proj, proj, bias, lam_params, norm_w)


def _gla_body(q_ref, k_ref, gd_ref, v_ref, r_ref, wg_ref, bg_ref, nw_ref, o_ref, state, part):
    n = pl.program_id(1)

    @pl.when(n == 0)
    def _():
        state[...] = jnp.zeros_like(state)

    rows = GLA_GROUP
    nc = rows // CHUNK
    npair = GLA_HEADS // 2
    gate = jnp.dot(gd_ref[...], wg_ref[...], precision=HI, preferred_element_type=F32) + bg_ref[...]
    log_a = _log_sigmoid(gate) * (1.0 / GLA_TAU)

    ri = lax.broadcasted_iota(jnp.int32, (rows, rows), 0)
    ci = lax.broadcasted_iota(jnp.int32, (rows, rows), 1)
    causal = ((ri // CHUNK) == (ci // CHUNK)) & (ri >= ci)
    tri = causal.astype(BF16)
    hi = log_a.astype(BF16)
    rem = log_a - hi.astype(F32)
    mid = rem.astype(BF16)
    lo = (rem - mid.astype(F32)).astype(BF16)
    b_all = (jnp.dot(tri, hi, preferred_element_type=F32) + jnp.dot(tri, mid, preferred_element_type=F32)
             + jnp.dot(tri, lo, preferred_element_type=F32))

    lane = lax.broadcasted_iota(jnp.int32, (rows, LANES), 1)
    lane_c = lax.broadcasted_iota(jnp.int32, (CHUNK, LANES), 1)
    head_lanes = (lane < GLA_DK, lane >= GLA_DK)
    chunk_lanes = (lane_c < GLA_DK, lane_c >= GLA_DK)
    pairs = []
    for p in range(npair):
        cols = slice(p * LANES, (p + 1) * LANES)
        pairs.append(dict(
            b=b_all[:, cols],
            q=q_ref[:, cols].astype(F32) * GLA_DK ** -0.5,
            k=k_ref[:, cols].astype(F32),
            v=[v_ref[:, (2 * p + hd) * GLA_DV:(2 * p + hd + 1) * GLA_DV].astype(F32) for hd in range(2)]))

    for pr in pairs:
        pr["q_dec"] = pr["q"] * jnp.exp(pr["b"])
    for c in range(nc):
        sl = slice(c * CHUNK, (c + 1) * CHUNK)
        for p, pr in enumerate(pairs):
            bc = pr["b"][sl]
            b_last = bc[CHUNK - 1:CHUNK]
            k_dec = pr["k"][sl] * jnp.exp(b_last - bc)
            e_last = jnp.exp(b_last)
            for hd in range(2):
                h = 2 * p + hd
                st = state[h]
                part[h, sl, :] = _bdot_nt(jnp.where(chunk_lanes[hd], pr["q_dec"][sl], 0.0), st)
                state[h] = st * e_last + _bdot_tn(pr["v"][hd][sl], k_dec)

    def finish(intra):
        for h in range(GLA_HEADS):
            o = part[h] + intra[h]
            o = o * lax.rsqrt(jnp.mean(o * o, axis=-1, keepdims=True) + RMS_EPS) * nw_ref[...]
            gate_r = _silu(r_ref[:, h * GLA_DV:(h + 1) * GLA_DV].astype(F32))
            o_ref[:, h * GLA_DV:(h + 1) * GLA_DV] = (o * gate_r).astype(o_ref.dtype)

    def intra_whole_chunk():
        out = []
        for pr in pairs:
            k_inv = pr["k"] * jnp.exp(jnp.minimum(-pr["b"], GLA_MAX_DECAY))
            for hd in range(2):
                a = _bdot_nt(jnp.where(head_lanes[hd], pr["q_dec"], 0.0), k_inv)
                out.append(_bdot(jnp.where(causal, a, 0.0), pr["v"][hd]))
        return out

    def intra_exact():
        out = []
        for pr in pairs:
            off = _gla_intra_off_diagonal(pr["q"], pr["k"], pr["b"], pr["v"], nc)
            diag = _gla_intra_diagonal(pr["q"], pr["k"], pr["b"], pr["v"])
            out += [off[hd] + diag[hd] for hd in range(2)]
        return out

    chunk_decay = jnp.max(-b_all.reshape(nc, CHUNK, npair * LANES)[:, CHUNK - 1:CHUNK, :])

    @pl.when(chunk_decay <= GLA_MAX_DECAY)
    def _():
        finish(intra_whole_chunk())

    @pl.when(chunk_decay > GLA_MAX_DECAY)
    def _():
        finish(intra_exact())


def _gla_intra_off_diagonal(q, k, b, vs, nc):
    per_chunk = CHUNK // GLA_SUB
    lane = lax.broadcasted_iota(jnp.int32, (GLA_SUB, LANES), 1)
    sub_mask = (lane < GLA_DK, lane >= GLA_DK)
    kcol = lax.broadcasted_iota(jnp.int32, (GLA_SUB, CHUNK), 1)
    outs = ([], [])
    for c in range(nc):
        sl = slice(c * CHUNK, (c + 1) * CHUNK)
        bc, qc, kc = b[sl], q[sl], k[sl]
        a_rows = [[jnp.zeros((GLA_SUB, CHUNK), F32)] for _ in vs]
        for blk in range(1, per_chunk):
            r0 = blk * GLA_SUB
            bref = bc[r0:r0 + 1]
            qs = qc[r0:r0 + GLA_SUB] * jnp.exp(bc[r0:r0 + GLA_SUB] - bref)
            ks = kc * jnp.exp(jnp.minimum(bref - bc, 0.0))
            for hd in range(2):
                a = _bdot_nt(jnp.where(sub_mask[hd], qs, 0.0), ks)
                a_rows[hd].append(jnp.where(kcol < r0, a, 0.0))
        for hd in range(2):
            outs[hd].append(_bdot(jnp.concatenate(a_rows[hd], axis=0), vs[hd][sl]))
    return [jnp.concatenate(o, axis=0) for o in outs]


def _gla_intra_diagonal(q, k, b, vs):
    rows = q.shape[0]
    nsub = rows // GLA_SUB
    b3 = b.reshape(nsub, GLA_SUB, LANES)
    q3 = q.reshape(nsub, GLA_SUB, LANES)
    k3 = k.reshape(nsub, GLA_SUB, LANES)
    v3 = [v.reshape(nsub, GLA_SUB, LANES) for v in vs]
    row3 = lax.broadcasted_iota(jnp.int32, (nsub, GLA_SUB, LANES), 1)
    lane3 = lax.broadcasted_iota(jnp.int32, (nsub, GLA_SUB, LANES), 2)
    rowc = lax.broadcasted_iota(jnp.int32, (nsub, GLA_SUB, 1), 1)
    o3 = [jnp.zeros((nsub, GLA_SUB, LANES), F32) for _ in vs]
    for jj in range(GLA_SUB):
        e = jnp.exp(jnp.where(row3 >= jj, b3 - b3[:, jj:jj + 1, :], 0.0))
        t = q3 * k3[:, jj:jj + 1, :] * e
        w_all = jnp.sum(t, axis=-1, keepdims=True)
        w_a = jnp.sum(jnp.where(lane3 < GLA_DK, t, 0.0), axis=-1, keepdims=True)
        for hd, w in enumerate((w_a, w_all - w_a)):
            o3[hd] = o3[hd] + jnp.where(rowc >= jj, w, 0.0) * v3[hd][:, jj:jj + 1, :]
    return [o.reshape(rows, LANES) for o in o3]


def _gla(proj, gate_in, w_gate_pad, b_gate, norm_w, bsz, seq):
    t = proj.shape[0]
    rows = GLA_GROUP
    spb = seq // rows
    qk_w = GLA_HEADS * GLA_DK
    v_w = GLA_HEADS * GLA_DV

    def at(col):
        return lambda b, n: (b * spb + n, col)

    def whole(a):
        return pl.BlockSpec(a.shape, lambda b, n: (0, 0))

    return pl.pallas_call(
        _gla_body,
        grid=(bsz, spb),
        in_specs=[pl.BlockSpec((rows, qk_w), at(OD_QD // qk_w)),
                  pl.BlockSpec((rows, qk_w), at(OD_KD // qk_w)),
                  pl.BlockSpec((rows, LANES), at(0)),
                  pl.BlockSpec((rows, v_w), at(OD_VD // v_w)),
                  pl.BlockSpec((rows, v_w), at(OD_RD // v_w)),
                  whole(w_gate_pad), whole(b_gate), whole(norm_w)],
        out_specs=pl.BlockSpec((rows, v_w), at(0)),
        out_shape=jax.ShapeDtypeStruct((t, v_w), BF16),
        scratch_shapes=[pltpu.VMEM((GLA_HEADS, GLA_DV, LANES), F32), pltpu.VMEM((GLA_HEADS, rows, GLA_DV), F32)],
        compiler_params=_cparams(("parallel", "arbitrary")),
        name="gla",
    )(proj, proj, gate_in, proj, proj, w_gate_pad, b_gate, norm_w)


def _ffn_body(tiles_per_seq, tc, x_ref, halo_ref, wu_ref, cw_ref, cb_ref, wd_ref, g_ref, b_ref, y_ref,
              act, scratch_g, scratch_v):
    i = pl.program_id(0)
    x = x_ref[...]
    halo = jnp.where(i % tiles_per_seq == 0, 0.0, halo_ref[...])
    xb = jnp.concatenate([halo.astype(BF16), x.astype(BF16)], axis=0)

    def branch(lo, scratch):
        full = jnp.dot(xb, wu_ref[:, lo:lo + tc], preferred_element_type=F32)
        return _causal_conv(full, cw_ref[:, lo:lo + tc], scratch) + cb_ref[:, lo:lo + tc]

    for c in range(D_FF // tc):
        gate = branch(c * tc, scratch_g)
        val = branch(D_FF + c * tc, scratch_v)
        act[:, c * tc:(c + 1) * tc] = (_silu(gate) * val).astype(BF16)
    ffn = jnp.dot(act[...], wd_ref[...], preferred_element_type=F32)
    y_ref[...] = _layer_norm(DEEPNORM_ALPHA * x + ffn, g_ref[...], b_ref[...])


def _ffn(x, w_up, conv_w, conv_b, w_down, g, b, seq, tm, tc):
    t = x.shape[0]
    hb = tm // HALO

    return pl.pallas_call(
        functools.partial(_ffn_body, seq // tm, tc),
        grid=(t // tm,),
        in_specs=[pl.BlockSpec((tm, D_MODEL), lambda i: (i, 0)),
                  pl.BlockSpec((HALO, D_MODEL), lambda i: (jnp.maximum(i * hb - 1, 0), 0)),
                  _resident(w_up), _resident(conv_w), _resident(conv_b), _resident(w_down),
                  _resident(g), _resident(b)],
        out_specs=pl.BlockSpec((tm, D_MODEL), lambda i: (i, 0)),
        out_shape=jax.ShapeDtypeStruct((t, D_MODEL), F32),
        scratch_shapes=[pltpu.VMEM((tm, D_FF), BF16), pltpu.VMEM((HALO + tm, tc), F32),
                        pltpu.VMEM((HALO + tm, tc), F32)],
        compiler_params=_cparams(("parallel",)),
        name="conv_ffn",
    )(x, x, w_up, conv_w, conv_b, w_down, g, b)


def _even_w_in(w):
    a_end = 4 * EVA_PART
    gates = w[:, a_end:a_end + 2 * GDN_HEADS]
    qkv_b = w[:, a_end + 2 * GDN_HEADS:]
    pad = jnp.zeros((w.shape[0], EVB_COLS - EVB_BA - 2 * GDN_HEADS), w.dtype)
    return w[:, :a_end].astype(BF16), jnp.concatenate([qkv_b, gates, pad], axis=1).astype(BF16)


def _odd_w_in(w):
    pad = jnp.zeros((w.shape[0], LANES - GLA_RANK), w.dtype)
    return w[:, :OD_COLS].astype(BF16), jnp.concatenate([w[:, OD_COLS:], pad], axis=1).astype(BF16)


def _even_layer(h, rel_bias, w_in, conv_w, a_log, dt_bias, norm_w, w_out, ln_g, ln_b, bsz, seq):
    w_a, w_b = _even_w_in(w_in)
    proj_a, proj_b = _even_in_proj(h, w_a, w_b, conv_w, seq, PROJ_TM)
    gate_pad = jnp.zeros((1, LANES), F32)
    alog_pad = lax.dynamic_update_slice(gate_pad, a_log[None].astype(F32), (0, GDN_HEADS))
    dtb_pad = lax.dynamic_update_slice(gate_pad, dt_bias[None].astype(F32), (0, GDN_HEADS))
    o_a = _gdn(proj_a, proj_b, alog_pad, dtb_pad, norm_w[None], bsz, seq)
    tiles = _bias_tiles(rel_bias, *_swa_bias_tables())
    ng, two_blk = len(SWA_CONFIGS), 2 * SWA_BLOCK
    bias = tiles.reshape(SWA_HEADS // 2, 2, ng, SWA_BLOCK, two_blk).transpose(2, 0, 1, 3, 4)
    o_b = _swa(proj_b, bias.reshape(ng, SWA_HEADS // 2, two_blk, two_blk), bsz, seq)
    w_out = w_out.astype(BF16)
    return _out_proj(_out_body, [o_a, o_b], [w_out[:EVA_PART], w_out[EVA_PART:]],
                     h, ln_g[None], ln_b[None], 512, "even_out")


def _odd_layer(h, rel_bias, w_in, lam_params, diff_norm_w, w_gate, b_gate, gla_norm_w, w_out,
               ln_g, ln_b, lam_init, bsz, seq):
    w_main, w_gd = _odd_w_in(w_in)
    proj, gate_in = _odd_in_proj(h, w_main, w_gd, PROJ_TM)
    blk = min(DIFF_BLOCK, seq)
    bucket, neg = _diff_bias_tables(blk)
    bias = _bias_tiles(rel_bias, bucket, neg, scale=LOG2E, base_bucket=NUM_BUCKETS - 1)
    o_c = _diff_attention(proj, bias, lam_params, diff_norm_w[None], lam_init, bsz, seq, blk)
    w_gate_pad = jnp.concatenate(
        [w_gate, jnp.zeros((LANES - GLA_RANK, w_gate.shape[1]), w_gate.dtype)], axis=0)
    o_d = _gla(proj, gate_in, w_gate_pad, b_gate[None], gla_norm_w[None], bsz, seq)
    diff_v = DIFF_HEADS * 2 * DIFF_DH
    w_out = w_out.astype(BF16)
    return _out_proj(_out_body, [o_c, o_d], [w_out[:diff_v], w_out[diff_v:]],
                     h, ln_g[None], ln_b[None], 512, "odd_out")


def kernel(x, rel_bias, w_in_even, gdn_conv_w, gdn_a_log, gdn_dt_bias, gdn_norm_w, w_out_even,
           w_in_odd, diff_lambda, diff_norm_w, gla_w_gate, gla_b_gate, gla_norm_w, w_out_odd,
           ffn_w_up, ffn_conv_w, ffn_conv_b, ffn_w_down, ln_g, ln_b):
    bsz, seq, d = x.shape
    h = x.reshape(bsz * seq, d)
    for layer in range(DEPTH):
        i = layer // 2
        if layer % 2 == 0:
            h = _even_layer(h, rel_bias, w_in_even[i], gdn_conv_w[i], gdn_a_log[i], gdn_dt_bias[i],
                            gdn_norm_w[i], w_out_even[i], ln_g[layer, 0], ln_b[layer, 0], bsz, seq)
        else:
            lam_init = 0.8 - 0.6 * math.exp(-0.3 * layer)
            h = _odd_layer(h, rel_bias, w_in_odd[i], diff_lambda[i], diff_norm_w[i], gla_w_gate[i],
                           gla_b_gate[i], gla_norm_w[i], w_out_odd[i], ln_g[layer, 0], ln_b[layer, 0],
                           lam_init, bsz, seq)
        h = _ffn(h, ffn_w_up[layer].astype(BF16), ffn_conv_w[layer], ffn_conv_b[layer][None],
                 ffn_w_down[layer].astype(BF16), ln_g[layer, 1][None], ln_b[layer, 1][None],
                 seq, 512, 256)
    return h.reshape(bsz, seq, d)
```

```python
import functools
import math

import numpy as np
import jax
import jax.numpy as jnp
from jax import lax
from jax.experimental import pallas as pl
from jax.experimental.pallas import tpu as pltpu

F32 = jnp.float32
BF16 = jnp.bfloat16
HI = lax.Precision.HIGHEST

D_MODEL = 1024
DEPTH = 2
DEEPNORM_ALPHA = (2 * DEPTH) ** 0.25
LN_EPS = 1e-5
RMS_EPS = 1e-6
NUM_BUCKETS = 32
REL_MAX_DIST = 2048
GDN_HEADS = 6
GDN_D = 128
CHUNK = 64
GDN_GROUP = 256
SWA_CONFIGS = ((128, 1), (512, 4), (2048, 16))
SWA_HEADS = 4
SWA_DH = 64
SWA_BLOCK = 128
SWA_UNROLL = 8
SWA_SPAN = SWA_BLOCK * max(d for _, d in SWA_CONFIGS)
DIFF_HEADS = 4
DIFF_DH = 64
DIFF_BLOCK = 512
LOG2E = math.log2(math.e)
GLA_HEADS = 4
GLA_DK = 64
GLA_DV = 128
GLA_RANK = 16
GLA_TAU = 16.0
GLA_SUB = 16
GLA_GROUP = 256
GLA_MAX_DECAY = 60.0
D_FF = 2816

LANES = 128
SUBLANES = 8
HALO = 16
VMEM_LIMIT = 56 * 1024 * 1024
NEG_BIG = -1e30

EVA_PART = GDN_HEADS * GDN_D
EVB_BA = 2304
EVB_COLS = 2560
EVB_TN = 512
OD_QC, OD_KC, OD_VC = 0, 512, 1024
OD_QD, OD_KD, OD_VD, OD_RD = 1536, 1792, 2048, 2560
OD_COLS = 3072
OD_TN = 768
PROJ_TM = 512
TAIL_TM = 512
FFN_TC = 256


def _cparams(sem):
    return pltpu.CompilerParams(dimension_semantics=sem, vmem_limit_bytes=VMEM_LIMIT)


def _bdot(a, b):
    return jnp.dot(a.astype(BF16), b.astype(BF16), preferred_element_type=F32)


def _bdot_nt(a, b):
    return lax.dot_general(a.astype(BF16), b.astype(BF16), (((1,), (1,)), ((), ())),
                           preferred_element_type=F32)


def _bdot_tn(a, b):
    return lax.dot_general(a.astype(BF16), b.astype(BF16), (((0,), (0,)), ((), ())),
                           preferred_element_type=F32)


def _sigmoid(x):
    return 1.0 / (1.0 + jnp.exp(-x))


def _silu(x):
    return x * _sigmoid(x)


def _softplus(x):
    return jnp.maximum(x, 0.0) + jnp.log1p(jnp.exp(-jnp.abs(x)))


def _log_sigmoid(x):
    return -_softplus(-x)


def _resident(a):
    return pl.BlockSpec(a.shape, lambda i: (0,) * a.ndim, pipeline_mode=pl.Buffered(1))


def _odd_in_body(x_ref, w_ref, wg_ref, o_ref, og_ref):
    xb = x_ref[...].astype(BF16)
    for c in range(OD_COLS // OD_TN):
        cols = slice(c * OD_TN, (c + 1) * OD_TN)
        o_ref[:, cols] = jnp.dot(xb, w_ref[:, cols], preferred_element_type=F32).astype(o_ref.dtype)
    og_ref[...] = jnp.dot(xb, wg_ref[...], preferred_element_type=F32)


def _odd_in_proj(x, w, w_gate_in, tm):
    t, k = x.shape
    return pl.pallas_call(
        _odd_in_body,
        grid=(t // tm,),
        in_specs=[pl.BlockSpec((tm, k), lambda i: (i, 0)), _resident(w), _resident(w_gate_in)],
        out_specs=[pl.BlockSpec((tm, OD_COLS), lambda i: (i, 0)), pl.BlockSpec((tm, LANES), lambda i: (i, 0))],
        out_shape=[jax.ShapeDtypeStruct((t, OD_COLS), BF16), jax.ShapeDtypeStruct((t, LANES), F32)],
        compiler_params=_cparams(("parallel",)),
        name="odd_in_proj",
    )(x, w, w_gate_in)


def _rel_bucket_np(dist):
    max_exact = NUM_BUCKETS // 2
    d = np.maximum(dist, 1).astype(np.float32)
    large = max_exact + (np.log(d / max_exact) / math.log(REL_MAX_DIST / max_exact)
                         * (NUM_BUCKETS - max_exact)).astype(np.int32)
    large = np.minimum(large, NUM_BUCKETS - 1)
    return np.where(dist < max_exact, dist, large).astype(np.int32)


def _bias_body(scale, base_bucket, tile_buckets, rb_ref, bucket_ref, neg_ref, o_ref):
    h = pl.program_id(0)
    t = pl.program_id(1)
    base = 0.0 if base_bucket is None else rb_ref[base_bucket, h]
    for tile, present in enumerate(tile_buckets):
        @pl.when(t == tile)
        def _(present=present):
            bucket = bucket_ref[0]
            acc = neg_ref[0]
            for b in present:
                acc = acc + jnp.where(bucket == b, (rb_ref[b, h] - base) * scale, 0.0)
            o_ref[0, 0] = acc


def _bias_tiles(rel_bias, bucket, neg, scale=1.0, base_bucket=None):
    nt, r, c = bucket.shape
    nh = rel_bias.shape[1]
    tile_buckets = tuple(tuple(int(b) for b in np.unique(bucket[t][neg[t] == 0]) if b != base_bucket)
                         for t in range(nt))
    return pl.pallas_call(
        functools.partial(_bias_body, scale, base_bucket, tile_buckets),
        grid=(nh, nt),
        in_specs=[pl.BlockSpec(memory_space=pltpu.SMEM),
                  pl.BlockSpec((1, r, c), lambda h, t: (t, 0, 0)),
                  pl.BlockSpec((1, r, c), lambda h, t: (t, 0, 0))],
        out_specs=pl.BlockSpec((1, 1, r, c), lambda h, t: (h, t, 0, 0)),
        out_shape=jax.ShapeDtypeStruct((nh, nt, r, c), F32),
        compiler_params=_cparams(("parallel", "parallel")),
        name="rel_bias_tiles",
    )(rel_bias, jnp.asarray(bucket), jnp.asarray(neg))


def _causal_conv(full, w, scratch):
    width = w.shape[0]
    rows = full.shape[0] - HALO
    scratch[...] = full
    y = w[width - 1:width, :] * full[HALO:]
    for j in range(width - 1):
        back = width - 1 - j
        y = y + w[j:j + 1, :] * scratch[HALO - back:HALO - back + rows, :]
    return y


def _even_in_body(tiles_per_seq, x_ref, halo_ref, wa_ref, wb_ref, cw_ref, oa_ref, ob_ref, scratch):
    i = pl.program_id(0)
    halo = jnp.where(i % tiles_per_seq == 0, 0.0, halo_ref[...])
    x = x_ref[...].astype(BF16)
    x_ext = jnp.concatenate([halo.astype(BF16), x], axis=0)
    for part in range(3):
        cols = slice(part * EVA_PART, (part + 1) * EVA_PART)
        y = jnp.dot(x_ext, wa_ref[:, cols], preferred_element_type=F32)
        c = _silu(_causal_conv(y, cw_ref[:, cols], scratch))
        for hd in range(GDN_HEADS):
            ch = c[:, hd * GDN_D:(hd + 1) * GDN_D]
            if part < 2:
                inv = lax.rsqrt(jnp.sum(ch * ch, axis=-1, keepdims=True) + RMS_EPS)
                ch = ch * (inv * GDN_D ** -0.5 if part == 0 else inv)
            lo = part * EVA_PART + hd * GDN_D
            oa_ref[:, lo:lo + GDN_D] = ch.astype(oa_ref.dtype)
    z_cols = slice(3 * EVA_PART, 4 * EVA_PART)
    oa_ref[:, z_cols] = jnp.dot(x, wa_ref[:, z_cols], preferred_element_type=F32).astype(oa_ref.dtype)
    for c in range(EVB_COLS // EVB_TN):
        cols = slice(c * EVB_TN, (c + 1) * EVB_TN)
        ob_ref[:, cols] = jnp.dot(x, wb_ref[:, cols], preferred_element_type=F32)


def _even_in_proj(x, w_a, w_b, conv_w, seq, tm):
    t, k = x.shape
    hb = tm // HALO
    return pl.pallas_call(
        functools.partial(_even_in_body, seq // tm),
        grid=(t // tm,),
        in_specs=[pl.BlockSpec((tm, k), lambda i: (i, 0)),
                  pl.BlockSpec((HALO, k), lambda i: (jnp.maximum(i * hb - 1, 0), 0)),
                  _resident(w_a), _resident(w_b), _resident(conv_w)],
        out_specs=[pl.BlockSpec((tm, w_a.shape[1]), lambda i: (i, 0)),
                   pl.BlockSpec((tm, EVB_COLS), lambda i: (i, 0))],
        out_shape=[jax.ShapeDtypeStruct((t, w_a.shape[1]), BF16), jax.ShapeDtypeStruct((t, EVB_COLS), F32)],
        scratch_shapes=[pltpu.VMEM((HALO + tm, EVA_PART), F32)],
        compiler_params=_cparams(("parallel",)),
        name="even_in_proj",
    )(x, x, w_a, w_b, conv_w)


def _gdn_body(q_ref, k_ref, v_ref, z_ref, ba_ref, alog_ref, dtb_ref, nw_ref, o_ref, state):
    n = pl.program_id(1)

    @pl.when(n == 0)
    def _():
        state[...] = jnp.zeros_like(state)

    grp = GDN_GROUP
    nc = grp // CHUNK
    nh = GDN_HEADS
    ba = ba_ref[...]
    beta_all = _sigmoid(ba)
    g_all = -jnp.exp(alog_ref[...]) * _softplus(ba + dtb_ref[...])

    ri = lax.broadcasted_iota(jnp.int32, (grp, grp), 0)
    ci = lax.broadcasted_iota(jnp.int32, (grp, grp), 1)
    same = (ri // CHUNK) == (ci // CHUNK)
    incl = same & (ri >= ci)
    strict = same & (ri > ci)
    eye = (ri == ci).astype(F32)
    tri = incl.astype(BF16)
    g_hi = g_all.astype(BF16)
    rem = g_all - g_hi.astype(F32)
    g_mid = rem.astype(BF16)
    g_lo = (rem - g_mid.astype(F32)).astype(BF16)
    gam_all = (jnp.dot(tri, g_hi, preferred_element_type=F32) + jnp.dot(tri, g_mid, preferred_element_type=F32)
               + jnp.dot(tri, g_lo, preferred_element_type=F32))
    gam_rows = gam_all.T

    hs = range(nh)
    q = [q_ref[:, h * GDN_D:(h + 1) * GDN_D] for h in hs]
    k = [k_ref[:, h * GDN_D:(h + 1) * GDN_D] for h in hs]
    v = [v_ref[:, h * GDN_D:(h + 1) * GDN_D] for h in hs]
    gam = [jnp.broadcast_to(gam_all[:, nh + h:nh + h + 1], (grp, GDN_D)) for h in hs]
    bcol = [beta_all[:, h:h + 1] for h in hs]
    decay, x, inv = [], [], []
    for h in hs:
        diff = gam[h][:, 0:1] - gam_rows[nh + h:nh + h + 1, :]
        decay.append(jnp.where(incl, jnp.exp(jnp.where(incl, diff, 0.0)), 0.0))
        kk = _bdot_nt(k[h], k[h])
        x.append(-jnp.where(strict, bcol[h] * kk * decay[h], 0.0))
        inv.append(eye + x[h])
    for _ in range(5):
        for h in hs:
            x[h] = _bdot(x[h], x[h])
            inv[h] = inv[h] + _bdot(inv[h], x[h])
    u, w, qk, q_dec, k_dec, g_last = [], [], [], [], [], []
    for h in hs:
        eg = jnp.exp(gam[h])
        uw = _bdot(inv[h], jnp.concatenate([v[h] * bcol[h], k[h] * (bcol[h] * eg)], axis=1))
        u.append(uw[:, :GDN_D])
        w.append(uw[:, GDN_D:])
        qk.append(_bdot_nt(q[h], k[h]) * decay[h])
        q_dec.append(q[h] * eg)
        kd, gl = [], []
        for c in range(nc):
            last = gam[h][(c + 1) * CHUNK - 1:(c + 1) * CHUNK, :]
            kd.append(k[h][c * CHUNK:(c + 1) * CHUNK] * jnp.exp(last - gam[h][c * CHUNK:(c + 1) * CHUNK]))
            gl.append(jnp.exp(last))
        k_dec.append(kd)
        g_last.append(gl)

    s = [state[h] for h in hs]
    q_s = [[] for _ in hs]
    delta = [[] for _ in hs]
    for c in range(nc):
        sl = slice(c * CHUNK, (c + 1) * CHUNK)
        for h in hs:
            r = _bdot(jnp.concatenate([w[h][sl], q_dec[h][sl]], axis=0), s[h])
            d = u[h][sl] - r[:CHUNK]
            q_s[h].append(r[CHUNK:])
            delta[h].append(d)
            s[h] = g_last[h][c] * s[h] + _bdot_tn(k_dec[h][c], d)
    for h in hs:
        state[h] = s[h]
        o = jnp.concatenate(q_s[h], axis=0) + _bdot(qk[h], jnp.concatenate(delta[h], axis=0))
        o = o * lax.rsqrt(jnp.mean(o * o, axis=-1, keepdims=True) + RMS_EPS) * nw_ref[...]
        z = z_ref[:, h * GDN_D:(h + 1) * GDN_D].astype(F32)
        o_ref[:, h * GDN_D:(h + 1) * GDN_D] = (o * _silu(z)).astype(o_ref.dtype)


def _gdn(proj_a, proj_b, alog_pad, dtb_pad, norm_w, bsz, seq):
    t = proj_a.shape[0]
    rows = GDN_GROUP
    spb = seq // rows
    width = GDN_HEADS * GDN_D

    def at(col):
        return lambda b, n: (b * spb + n, col)

    return pl.pallas_call(
        _gdn_body,
        grid=(bsz, spb),
        in_specs=[pl.BlockSpec((rows, width), at(0)),
                  pl.BlockSpec((rows, width), at(1)),
                  pl.BlockSpec((rows, width), at(2)),
                  pl.BlockSpec((rows, width), at(3)),
                  pl.BlockSpec((rows, LANES), at(EVB_BA // LANES)),
                  pl.BlockSpec((1, LANES), lambda b, n: (0, 0)),
                  pl.BlockSpec((1, LANES), lambda b, n: (0, 0)),
                  pl.BlockSpec((1, LANES), lambda b, n: (0, 0))],
        out_specs=pl.BlockSpec((rows, width), at(0)),
        out_shape=jax.ShapeDtypeStruct((t, width), BF16),
        scratch_shapes=[pltpu.VMEM((GDN_HEADS, GDN_D, GDN_D), F32)],
        compiler_params=_cparams(("parallel", "arbitrary")),
        name="gdn",
    )(proj_a, proj_a, proj_a, proj_a, proj_b, alog_pad, dtb_pad, norm_w)


def _swa_bias_tables():
    qi = np.arange(SWA_BLOCK)[:, None] + SWA_BLOCK
    kj = np.arange(2 * SWA_BLOCK)[None, :]
    rel = qi - kj
    buckets, negs = [], []
    for window, dilation in SWA_CONFIGS:
        buckets.append(_rel_bucket_np(np.maximum(rel, 0) * dilation))
        negs.append(np.where((rel >= 0) & (rel <= window // dilation), 0.0, NEG_BIG))
    return np.stack(buckets).astype(np.int32), np.stack(negs).astype(np.float32)


def _swa_body(*refs):
    ng = len(SWA_CONFIGS)
    ins, bias_ref, o_ref = refs[:5 * ng], refs[5 * ng], refs[5 * ng + 1]
    scratch = refs[5 * ng + 2:]
    kbufs, vbufs, o_scr, lse_scr = scratch[:ng], scratch[ng:2 * ng], scratch[2 * ng], scratch[2 * ng + 1]
    j = pl.program_id(1)
    blk = SWA_BLOCK
    first_head = lax.broadcasted_iota(jnp.int32, (blk, LANES), 1) < SWA_DH
    in_prev = lax.broadcasted_iota(jnp.int32, (2 * blk, 2 * blk), 1) < blk

    for g, (_, d) in enumerate(SWA_CONFIGS):
        q_ref, kc_ref, kp_ref, vc_ref, vp_ref = ins[5 * g:5 * g + 5]
        kbuf, vbuf = kbufs[g], vbufs[g]
        halo = blk * d
        kbuf[0:halo, :] = kp_ref[...]
        kbuf[halo:, :] = kc_ref[...]
        vbuf[0:halo, :] = vp_ref[...]
        vbuf[halo:, :] = vc_ref[...]

        def unit(u, carry, g=g, d=d, halo=halo, q_ref=q_ref, kbuf=kbuf, vbuf=vbuf):
            base = (u // d) * halo + u % d
            q = q_ref[pl.ds(base, blk, stride=d), :] * SWA_DH ** -0.5
            k = kbuf[pl.ds(base, 2 * blk, stride=d), :]
            v = vbuf[pl.ds(base, 2 * blk, stride=d), :]
            lhs = jnp.concatenate([jnp.where(first_head, q, 0.0), jnp.where(first_head, 0.0, q)], axis=0)
            s = _bdot_nt(lhs, k) + bias_ref[g, 0]
            no_prev = (j == 0) & (u < d)
            s = jnp.where(in_prev & no_prev, NEG_BIG, s)
            m = jnp.max(s, axis=-1, keepdims=True)
            p = jnp.exp(s - m)
            l = jnp.sum(p, axis=-1, keepdims=True)
            o2 = _bdot(p / l, v)
            lse2 = m + jnp.log(l)
            o_scr[g, pl.ds(base, blk, stride=d), :] = jnp.where(first_head, o2[:blk], o2[blk:])
            lse_scr[g, pl.ds(base, blk, stride=d), :] = jnp.where(first_head, lse2[:blk], lse2[blk:])
            return carry

        lax.fori_loop(0, SWA_SPAN // blk, unit, 0, unroll=SWA_UNROLL)

    def combine(c, carry):
        rows = pl.ds(pl.multiple_of(c * 2 * blk, 2 * blk), 2 * blk)
        lse = [lse_scr[g, rows, :] for g in range(ng)]
        m = functools.reduce(jnp.maximum, lse)
        e = [jnp.exp(x - m) for x in lse]
        den = functools.reduce(lambda x, y: x + y, e)
        o_ref[rows, :] = functools.reduce(
            lambda x, y: x + y, [(e[g] / den) * o_scr[g, rows, :] for g in range(ng)]).astype(o_ref.dtype)
        return carry

    lax.fori_loop(0, SWA_SPAN // (2 * blk), combine, 0)


def _swa(proj, bias, bsz, seq):
    t = proj.shape[0]
    ng = len(SWA_CONFIGS)
    nspan = seq // SWA_SPAN
    npair = SWA_HEADS * SWA_DH // LANES
    group_cols = SWA_HEADS * SWA_DH // LANES
    in_specs, scratch_k = [], []
    for g, (_, d) in enumerate(SWA_CONFIGS):
        halo = SWA_BLOCK * d
        per_span = SWA_SPAN // halo

        def cur(which, g=g):
            col = (which * ng + g) * group_cols
            return lambda b, j, p: (b * nspan + j, col + p)

        def prev(which, g=g, per_span=per_span):
            col = (which * ng + g) * group_cols
            return lambda b, j, p: (jnp.maximum((b * nspan + j) * per_span - 1, 0), col + p)

        in_specs += [pl.BlockSpec((SWA_SPAN, LANES), cur(0)),
                     pl.BlockSpec((SWA_SPAN, LANES), cur(1)), pl.BlockSpec((halo, LANES), prev(1)),
                     pl.BlockSpec((SWA_SPAN, LANES), cur(2)), pl.BlockSpec((halo, LANES), prev(2))]
        scratch_k.append(pltpu.VMEM((halo + SWA_SPAN, LANES), F32))
    in_specs.append(pl.BlockSpec((ng, 1, 2 * SWA_BLOCK, 2 * SWA_BLOCK), lambda b, j, p: (0, p, 0, 0)))
    return pl.pallas_call(
        _swa_body,
        grid=(bsz, nspan, npair),
        in_specs=in_specs,
        out_specs=pl.BlockSpec((SWA_SPAN, LANES), lambda b, j, p: (b * nspan + j, p)),
        out_shape=jax.ShapeDtypeStruct((t, npair * LANES), BF16),
        scratch_shapes=scratch_k + scratch_k + [pltpu.VMEM((ng, SWA_SPAN, LANES), F32)] * 2,
        compiler_params=_cparams(("parallel", "parallel", "parallel")),
        name="swa",
    )(*([proj] * (5 * ng)), bias)


def _layer_norm(y, g, b):
    mu = jnp.mean(y, axis=-1, keepdims=True)
    yc = y - mu
    var = jnp.mean(yc * yc, axis=-1, keepdims=True)
    return yc * lax.rsqrt(var + LN_EPS) * g + b


def _diff_bias_tables(blk):
    buckets = _rel_bucket_np(np.arange(2 * REL_MAX_DIST))
    far = int(np.max(np.nonzero(buckets != NUM_BUCKETS - 1)[0])) + 1
    nb = -(-(far + blk - 1) // blk)
    kk = np.arange(blk)[:, None]
    qq = np.arange(blk)[None, :]
    dist = np.stack([t * blk + qq - kk for t in range(-1, nb + 1)])
    bucket = _rel_bucket_np(np.maximum(dist, 0))
    neg = np.where(dist >= 0, 0.0, NEG_BIG).astype(np.float32)
    return bucket, neg


def _diff_body(blk, nb, lam_init, q_ref, qn_ref, k_ref, v_ref, bias_ref, lam_ref, nw_ref, o_ref,
               vt, acc1, acc2, s_a, s_b):
    qi = pl.program_id(2)
    dv = 2 * DIFF_DH
    seq = k_ref.shape[0]

    @pl.when(qi == 0)
    def _():
        vt[dv:, :] = jnp.ones((vt.shape[0] - dv, seq), BF16)

        def fill(c, carry):
            st = pl.multiple_of(c * blk, blk)
            vt[0:dv, pl.ds(st, blk)] = v_ref[pl.ds(st, blk), :].astype(F32).T.astype(BF16)
            return carry

        lax.fori_loop(0, seq // blk, fill, 0)

    lane = lax.broadcasted_iota(jnp.int32, (blk, dv), 1)

    def components(ref):
        q = ref[...].astype(F32) * (DIFF_DH ** -0.5 * LOG2E)
        return (jnp.where(lane < DIFF_DH, q, 0.0).astype(BF16), jnp.where(lane >= DIFF_DH, q, 0.0).astype(BF16))

    qs = components(q_ref)
    qs_next = components(qn_ref)
    accs = (acc1, acc2)
    for acc in accs:
        acc[...] = jnp.zeros_like(acc)

    last = pl.num_programs(2) - 1

    def key_rows(kj):
        return pl.ds(pl.multiple_of(jnp.minimum(kj, last) * blk, blk), blk)

    def scores(kj, dst, queries=qs):
        k = k_ref[key_rows(kj), :]
        for c, qc in enumerate(queries):
            dst[c] = lax.dot_general(k, qc, (((1,), (1,)), ((), ())), preferred_element_type=F32)

    def consume(biased, kj, src, ms):
        vtb = vt[:, key_rows(kj)]
        out = []
        for c, (m, acc) in enumerate(zip(ms, accs)):
            s = src[c]
            if biased:
                s = s + bias_ref[0, jnp.minimum(qi - kj, nb) + 1]
            m_new = jnp.maximum(m, jnp.max(s, axis=0, keepdims=True))
            p = jnp.exp2(s - m_new).astype(BF16)
            acc[...] = jnp.exp2(m - m_new) * acc[...] + jnp.dot(vtb, p, preferred_element_type=F32)
            out.append(m_new)
        return tuple(out)

    trips = (qi + 2) // 2
    far_trips = jnp.maximum(qi - nb + 1, 0) // 2

    def pair(biased, t, ms):
        kj = 2 * t
        scores(kj + 1, s_b)
        ms = consume(biased, kj, s_a, ms)
        if biased:
            is_last = t == trips - 1
            scores(jnp.where(is_last, 0, kj + 2), s_a,
                   tuple(jnp.where(is_last, qn, qc) for qc, qn in zip(qs, qs_next)))
        else:
            scores(kj + 2, s_a)
        return consume(biased, kj + 1, s_b, ms)

    @pl.when(qi == 0)
    def _():
        scores(0, s_a)

    m0 = jnp.full((1, blk), NEG_BIG, F32)
    ms = lax.fori_loop(0, far_trips, functools.partial(pair, False), (m0, m0))
    lax.fori_loop(far_trips, trips, functools.partial(pair, True), ms)

    lp = lam_ref[...]
    lam = (jnp.exp(jnp.sum(lp[0:1] * lp[1:2], axis=-1, keepdims=True))
           - jnp.exp(jnp.sum(lp[2:3] * lp[3:4], axis=-1, keepdims=True)) + lam_init)
    a1, a2 = acc1[...], acc2[...]
    o_t = a1[:dv] * (1.0 / a1[dv:dv + 1]) - a2[:dv] * (lam / a2[dv:dv + 1])
    o = o_t.T
    o = o * lax.rsqrt(jnp.mean(o * o, axis=-1, keepdims=True) + RMS_EPS) * nw_ref[...]
    o_ref[...] = (o * (1.0 - lam_init)).astype(o_ref.dtype)


def _diff_attention(proj, bias, lam_params, norm_w, lam_init, bsz, seq, blk):
    t = proj.shape[0]
    nq = seq // blk
    nt = bias.shape[1]
    nb = nt - 2
    width = 2 * DIFF_DH
    ones_rows = 2 * SUBLANES
    return pl.pallas_call(
        functools.partial(_diff_body, blk, nb, lam_init),
        grid=(bsz, DIFF_HEADS, nq),
        in_specs=[pl.BlockSpec((blk, width), lambda b, h, i: (b * nq + i, OD_QC // width + h)),
                  pl.BlockSpec((blk, width),
                               lambda b, h, i: (b * nq + jnp.minimum(i + 1, nq - 1), OD_QC // width + h)),
                  pl.BlockSpec((seq, width), lambda b, h, i: (b, OD_KC // width + h)),
                  pl.BlockSpec((seq, width), lambda b, h, i: (b, OD_VC // width + h)),
                  pl.BlockSpec((1, nt, blk, blk), lambda b, h, i: (h, 0, 0, 0)),
                  pl.BlockSpec((4, DIFF_DH), lambda b, h, i: (0, 0)),
                  pl.BlockSpec((1, width), lambda b, h, i: (0, 0))],
        out_specs=pl.BlockSpec((blk, width), lambda b, h, i: (b * nq + i, h)),
        out_shape=jax.ShapeDtypeStruct((t, DIFF_HEADS * width), BF16),
        scratch_shapes=[pltpu.VMEM((width + ones_rows, seq), BF16),
                        pltpu.VMEM((width + ones_rows, blk), F32), pltpu.VMEM((width + ones_rows, blk), F32),
                        pltpu.VMEM((2, blk, blk), F32), pltpu.VMEM((2, blk, blk), F32)],
        compiler_params=_cparams(("arbitrary", "arbitrary", "arbitrary")),
        name="diff_attn",
    )(proj, proj, proj, proj, bias, lam_params, norm_w)


def _gla_body(q_ref, k_ref, gd_ref, v_ref, r_ref, wg_ref, bg_ref, nw_ref, o_ref, state, part):
    n = pl.program_id(1)

    @pl.when(n == 0)
    def _():
        state[...] = jnp.zeros_like(state)

    rows = GLA_GROUP
    nc = rows // CHUNK
    npair = GLA_HEADS // 2
    gate = jnp.dot(gd_ref[...], wg_ref[...], precision=HI, preferred_element_type=F32) + bg_ref[...]
    log_a = _log_sigmoid(gate) * (1.0 / GLA_TAU)

    ri = lax.broadcasted_iota(jnp.int32, (rows, rows), 0)
    ci = lax.broadcasted_iota(jnp.int32, (rows, rows), 1)
    causal = ((ri // CHUNK) == (ci // CHUNK)) & (ri >= ci)
    tri = causal.astype(BF16)
    hi = log_a.astype(BF16)
    rem = log_a - hi.astype(F32)
    mid = rem.astype(BF16)
    lo = (rem - mid.astype(F32)).astype(BF16)
    b_all = (jnp.dot(tri, hi, preferred_element_type=F32) + jnp.dot(tri, mid, preferred_element_type=F32)
             + jnp.dot(tri, lo, preferred_element_type=F32))

    lane = lax.broadcasted_iota(jnp.int32, (rows, LANES), 1)
    lane_c = lax.broadcasted_iota(jnp.int32, (CHUNK, LANES), 1)
    head_lanes = (lane < GLA_DK, lane >= GLA_DK)
    chunk_lanes = (lane_c < GLA_DK, lane_c >= GLA_DK)
    pairs = []
    for p in range(npair):
        cols = slice(p * LANES, (p + 1) * LANES)
        pairs.append(dict(
            b=b_all[:, cols],
            q=q_ref[:, cols].astype(F32) * GLA_DK ** -0.5,
            k=k_ref[:, cols].astype(F32),
            v=[v_ref[:, (2 * p + hd) * GLA_DV:(2 * p + hd + 1) * GLA_DV].astype(F32) for hd in range(2)]))

    for pr in pairs:
        pr["q_dec"] = pr["q"] * jnp.exp(pr["b"])
    for c in range(nc):
        sl = slice(c * CHUNK, (c + 1) * CHUNK)
        for p, pr in enumerate(pairs):
            bc = pr["b"][sl]
            b_last = bc[CHUNK - 1:CHUNK]
            k_dec = pr["k"][sl] * jnp.exp(b_last - bc)
            e_last = jnp.exp(b_last)
            for hd in range(2):
                h = 2 * p + hd
                st = state[h]
                part[h, sl, :] = _bdot_nt(jnp.where(chunk_lanes[hd], pr["q_dec"][sl], 0.0), st)
                state[h] = st * e_last + _bdot_tn(pr["v"][hd][sl], k_dec)

    def finish(intra):
        for h in range(GLA_HEADS):
            o = part[h] + intra[h]
            o = o * lax.rsqrt(jnp.mean(o * o, axis=-1, keepdims=True) + RMS_EPS) * nw_ref[...]
            gate_r = _silu(r_ref[:, h * GLA_DV:(h + 1) * GLA_DV].astype(F32))
            o_ref[:, h * GLA_DV:(h + 1) * GLA_DV] = (o * gate_r).astype(o_ref.dtype)

    def intra_whole_chunk():
        out = []
        for pr in pairs:
            k_inv = pr["k"] * jnp.exp(jnp.minimum(-pr["b"], GLA_MAX_DECAY))
            for hd in range(2):
                a = _bdot_nt(jnp.where(head_lanes[hd], pr["q_dec"], 0.0), k_inv)
                out.append(_bdot(jnp.where(causal, a, 0.0), pr["v"][hd]))
        return out

    def intra_exact():
        out = []
        for pr in pairs:
            off = _gla_intra_off_diagonal(pr["q"], pr["k"], pr["b"], pr["v"], nc)
            diag = _gla_intra_diagonal(pr["q"], pr["k"], pr["b"], pr["v"])
            out += [off[hd] + diag[hd] for hd in range(2)]
        return out

    chunk_decay = jnp.max(-b_all.reshape(nc, CHUNK, npair * LANES)[:, CHUNK - 1:CHUNK, :])

    @pl.when(chunk_decay <= GLA_MAX_DECAY)
    def _():
        finish(intra_whole_chunk())

    @pl.when(chunk_decay > GLA_MAX_DECAY)
    def _():
        finish(intra_exact())


def _gla_intra_off_diagonal(q, k, b, vs, nc):
    per_chunk = CHUNK // GLA_SUB
    lane = lax.broadcasted_iota(jnp.int32, (GLA_SUB, LANES), 1)
    sub_mask = (lane < GLA_DK, lane >= GLA_DK)
    kcol = lax.broadcasted_iota(jnp.int32, (GLA_SUB, CHUNK), 1)
    outs = ([], [])
    for c in range(nc):
        sl = slice(c * CHUNK, (c + 1) * CHUNK)
        bc, qc, kc = b[sl], q[sl], k[sl]
        a_rows = [[jnp.zeros((GLA_SUB, CHUNK), F32)] for _ in vs]
        for blk in range(1, per_chunk):
            r0 = blk * GLA_SUB
            bref = bc[r0:r0 + 1]
            qs = qc[r0:r0 + GLA_SUB] * jnp.exp(bc[r0:r0 + GLA_SUB] - bref)
            ks = kc * jnp.exp(jnp.minimum(bref - bc, 0.0))
            for hd in range(2):
                a = _bdot_nt(jnp.where(sub_mask[hd], qs, 0.0), ks)
                a_rows[hd].append(jnp.where(kcol < r0, a, 0.0))
        for hd in range(2):
            outs[hd].append(_bdot(jnp.concatenate(a_rows[hd], axis=0), vs[hd][sl]))
    return [jnp.concatenate(o, axis=0) for o in outs]


def _gla_intra_diagonal(q, k, b, vs):
    rows = q.shape[0]
    nsub = rows // GLA_SUB
    b3 = b.reshape(nsub, GLA_SUB, LANES)
    q3 = q.reshape(nsub, GLA_SUB, LANES)
    k3 = k.reshape(nsub, GLA_SUB, LANES)
    v3 = [v.reshape(nsub, GLA_SUB, LANES) for v in vs]
    row3 = lax.broadcasted_iota(jnp.int32, (nsub, GLA_SUB, LANES), 1)
    lane3 = lax.broadcasted_iota(jnp.int32, (nsub, GLA_SUB, LANES), 2)
    rowc = lax.broadcasted_iota(jnp.int32, (nsub, GLA_SUB, 1), 1)
    o3 = [jnp.zeros((nsub, GLA_SUB, LANES), F32) for _ in vs]
    for jj in range(GLA_SUB):
        e = jnp.exp(jnp.where(row3 >= jj, b3 - b3[:, jj:jj + 1, :], 0.0))
        t = q3 * k3[:, jj:jj + 1, :] * e
        w_all = jnp.sum(t, axis=-1, keepdims=True)
        w_a = jnp.sum(jnp.where(lane3 < GLA_DK, t, 0.0), axis=-1, keepdims=True)
        for hd, w in enumerate((w_a, w_all - w_a)):
            o3[hd] = o3[hd] + jnp.where(rowc >= jj, w, 0.0) * v3[hd][:, jj:jj + 1, :]
    return [o.reshape(rows, LANES) for o in o3]


def _gla(proj, gate_in, w_gate_pad, b_gate, norm_w, bsz, seq):
    t = proj.shape[0]
    rows = GLA_GROUP
    spb = seq // rows
    qk_w = GLA_HEADS * GLA_DK
    v_w = GLA_HEADS * GLA_DV

    def at(col):
        return lambda b, n: (b * spb + n, col)

    def whole(a):
        return pl.BlockSpec(a.shape, lambda b, n: (0, 0))

    return pl.pallas_call(
        _gla_body,
        grid=(bsz, spb),
        in_specs=[pl.BlockSpec((rows, qk_w), at(OD_QD // qk_w)),
                  pl.BlockSpec((rows, qk_w), at(OD_KD // qk_w)),
                  pl.BlockSpec((rows, LANES), at(0)),
                  pl.BlockSpec((rows, v_w), at(OD_VD // v_w)),
                  pl.BlockSpec((rows, v_w), at(OD_RD // v_w)),
                  whole(w_gate_pad), whole(b_gate), whole(norm_w)],
        out_specs=pl.BlockSpec((rows, v_w), at(0)),
        out_shape=jax.ShapeDtypeStruct((t, v_w), BF16),
        scratch_shapes=[pltpu.VMEM((GLA_HEADS, GLA_DV, LANES), F32), pltpu.VMEM((GLA_HEADS, rows, GLA_DV), F32)],
        compiler_params=_cparams(("parallel", "arbitrary")),
        name="gla",
    )(proj, proj, gate_in, proj, proj, w_gate_pad, b_gate, norm_w)


def _tail_body(tiles_per_seq, tc, ma_ref, ma_halo, mb_ref, mb_halo, h_ref, h_halo, woa_ref, wob_ref,
               g1_ref, b1_ref, wu_ref, cw_ref, cb_ref, wd_ref, g2_ref, b2_ref, y_ref,
               act, scratch_g, scratch_v):
    i = pl.program_id(0)

    def ext(halo_ref, ref):
        return jnp.concatenate([halo_ref[...], ref[...]], axis=0)

    mix = (jnp.dot(ext(ma_halo, ma_ref), woa_ref[...], preferred_element_type=F32)
           + jnp.dot(ext(mb_halo, mb_ref), wob_ref[...], preferred_element_type=F32))
    x_ext = _layer_norm(DEEPNORM_ALPHA * ext(h_halo, h_ref) + mix, g1_ref[...], b1_ref[...])
    x = x_ext[HALO:]
    row = lax.broadcasted_iota(jnp.int32, x_ext.shape, 0)
    first = i % tiles_per_seq == 0
    xb = jnp.where((row < HALO) & first, 0.0, x_ext).astype(BF16)

    def branch(lo, scratch):
        full = jnp.dot(xb, wu_ref[:, lo:lo + tc], preferred_element_type=F32)
        return _causal_conv(full, cw_ref[:, lo:lo + tc], scratch) + cb_ref[:, lo:lo + tc]

    for c in range(D_FF // tc):
        gate = branch(c * tc, scratch_g)
        val = branch(D_FF + c * tc, scratch_v)
        act[:, c * tc:(c + 1) * tc] = (_silu(gate) * val).astype(BF16)
    ffn = jnp.dot(act[...], wd_ref[...], preferred_element_type=F32)
    y_ref[...] = _layer_norm(DEEPNORM_ALPHA * x + ffn, g2_ref[...], b2_ref[...])


def _layer_tail(mix_a, mix_b, h, w_out_a, w_out_b, g1, b1, w_up, conv_w, conv_b, w_down, g2, b2, seq):
    t = h.shape[0]
    tm, tc = TAIL_TM, FFN_TC
    hb = tm // HALO

    def tile(a):
        return pl.BlockSpec((tm, a.shape[1]), lambda i: (i, 0))

    def halo(a):
        return pl.BlockSpec((HALO, a.shape[1]), lambda i: (jnp.maximum(i * hb - 1, 0), 0))

    params = (w_out_a, w_out_b, g1, b1, w_up, conv_w, conv_b, w_down, g2, b2)
    return pl.pallas_call(
        functools.partial(_tail_body, seq // tm, tc),
        grid=(t // tm,),
        in_specs=[tile(mix_a), halo(mix_a), tile(mix_b), halo(mix_b), tile(h), halo(h)]
        + [_resident(p) for p in params],
        out_specs=pl.BlockSpec((tm, D_MODEL), lambda i: (i, 0)),
        out_shape=jax.ShapeDtypeStruct((t, D_MODEL), F32),
        scratch_shapes=[pltpu.VMEM((tm, D_FF), BF16), pltpu.VMEM((HALO + tm, tc), F32),
                        pltpu.VMEM((HALO + tm, tc), F32)],
        compiler_params=_cparams(("parallel",)),
        name="layer_tail",
    )(mix_a, mix_a, mix_b, mix_b, h, h, *params)


def _even_w_in(w):
    a_end = 4 * EVA_PART
    gates = w[:, a_end:a_end + 2 * GDN_HEADS]
    qkv_b = w[:, a_end + 2 * GDN_HEADS:]
    pad = jnp.zeros((w.shape[0], EVB_COLS - EVB_BA - 2 * GDN_HEADS), w.dtype)
    return w[:, :a_end].astype(BF16), jnp.concatenate([qkv_b, gates, pad], axis=1).astype(BF16)


def _odd_w_in(w):
    pad = jnp.zeros((w.shape[0], LANES - GLA_RANK), w.dtype)
    return w[:, :OD_COLS].astype(BF16), jnp.concatenate([w[:, OD_COLS:], pad], axis=1).astype(BF16)


def _even_mixer(h, rel_bias, w_in, conv_w, a_log, dt_bias, norm_w, w_out, bsz, seq):
    w_a, w_b = _even_w_in(w_in)
    proj_a, proj_b = _even_in_proj(h, w_a, w_b, conv_w, seq, PROJ_TM)
    gate_pad = jnp.zeros((1, LANES), F32)
    alog_pad = lax.dynamic_update_slice(gate_pad, a_log[None].astype(F32), (0, GDN_HEADS))
    dtb_pad = lax.dynamic_update_slice(gate_pad, dt_bias[None].astype(F32), (0, GDN_HEADS))
    o_a = _gdn(proj_a, proj_b, alog_pad, dtb_pad, norm_w[None], bsz, seq)
    tiles = _bias_tiles(rel_bias, *_swa_bias_tables())
    ng, two_blk = len(SWA_CONFIGS), 2 * SWA_BLOCK
    bias = tiles.reshape(SWA_HEADS // 2, 2, ng, SWA_BLOCK, two_blk).transpose(2, 0, 1, 3, 4)
    o_b = _swa(proj_b, bias.reshape(ng, SWA_HEADS // 2, two_blk, two_blk), bsz, seq)
    w_out = w_out.astype(BF16)
    return o_a, o_b, w_out[:EVA_PART], w_out[EVA_PART:]


def _odd_mixer(h, rel_bias, w_in, lam_params, diff_norm_w, w_gate, b_gate, gla_norm_w, w_out,
               lam_init, bsz, seq):
    w_main, w_gd = _odd_w_in(w_in)
    proj, gate_in = _odd_in_proj(h, w_main, w_gd, PROJ_TM)
    blk = min(DIFF_BLOCK, seq)
    bucket, neg = _diff_bias_tables(blk)
    bias = _bias_tiles(rel_bias, bucket, neg, scale=LOG2E, base_bucket=NUM_BUCKETS - 1)
    o_c = _diff_attention(proj, bias, lam_params, diff_norm_w[None], lam_init, bsz, seq, blk)
    w_gate_pad = jnp.concatenate(
        [w_gate, jnp.zeros((LANES - GLA_RANK, w_gate.shape[1]), w_gate.dtype)], axis=0)
    o_d = _gla(proj, gate_in, w_gate_pad, b_gate[None], gla_norm_w[None], bsz, seq)
    diff_v = DIFF_HEADS * 2 * DIFF_DH
    w_out = w_out.astype(BF16)
    return o_c, o_d, w_out[:diff_v], w_out[diff_v:]


def kernel(x, rel_bias, w_in_even, gdn_conv_w, gdn_a_log, gdn_dt_bias, gdn_norm_w, w_out_even,
           w_in_odd, diff_lambda, diff_norm_w, gla_w_gate, gla_b_gate, gla_norm_w, w_out_odd,
           ffn_w_up, ffn_conv_w, ffn_conv_b, ffn_w_down, ln_g, ln_b):
    bsz, seq, d = x.shape
    h = x.reshape(bsz * seq, d)
    for layer in range(DEPTH):
        i = layer // 2
        if layer % 2 == 0:
            mixed = _even_mixer(h, rel_bias, w_in_even[i], gdn_conv_w[i], gdn_a_log[i], gdn_dt_bias[i],
                                gdn_norm_w[i], w_out_even[i], bsz, seq)
        else:
            lam_init = 0.8 - 0.6 * math.exp(-0.3 * layer)
            mixed = _odd_mixer(h, rel_bias, w_in_odd[i], diff_lambda[i], diff_norm_w[i], gla_w_gate[i],
                               gla_b_gate[i], gla_norm_w[i], w_out_odd[i], lam_init, bsz, seq)
        h = _layer_tail(*mixed[:2], h, *mixed[2:], ln_g[layer, 0][None], ln_b[layer, 0][None],
                        ffn_w_up[layer].astype(BF16), ffn_conv_w[layer], ffn_conv_b[layer][None],
                        ffn_w_down[layer].astype(BF16), ln_g[layer, 1][None], ln_b[layer, 1][None], seq)
    return h.reshape(bsz, seq, d)
```

```python
import functools
import math

import numpy as np
import jax
import jax.numpy as jnp
from jax import lax
from jax.experimental import pallas as pl
from jax.experimental.pallas import tpu as pltpu

F32 = jnp.float32
BF16 = jnp.bfloat16
HI = lax.Precision.HIGHEST

D_MODEL = 1024
DEPTH = 2
DEEPNORM_ALPHA = (2 * DEPTH) ** 0.25
LN_EPS = 1e-5
RMS_EPS = 1e-6
NUM_BUCKETS = 32
REL_MAX_DIST = 2048
GDN_HEADS = 6
GDN_D = 128
CHUNK = 64
GDN_GROUP = 256
GDN_STEP_GROUPS = 1
SWA_CONFIGS = ((128, 1), (512, 4), (2048, 16))
SWA_HEADS = 4
SWA_DH = 64
SWA_BLOCK = 128
SWA_UNROLL = 8
SWA_SPAN = SWA_BLOCK * max(d for _, d in SWA_CONFIGS)
DIFF_HEADS = 4
DIFF_DH = 64
DIFF_BLOCK = 512
LOG2E = math.log2(math.e)
GLA_HEADS = 4
GLA_DK = 64
GLA_DV = 128
GLA_RANK = 16
GLA_TAU = 16.0
GLA_SUB = 16
GLA_GROUP = 256
GLA_MAX_DECAY = 60.0
D_FF = 2816

LANES = 128
SUBLANES = 8
HALO = 16
VMEM_LIMIT = 56 * 1024 * 1024
NEG_BIG = -1e30

EVA_PART = GDN_HEADS * GDN_D
EVB_BA = 2304
EVB_COLS = 2560
EVB_TN = 512
OD_QC, OD_KC, OD_VC = 0, 512, 1024
OD_QD, OD_KD, OD_VD, OD_RD = 1536, 1792, 2048, 2560
OD_COLS = 3072
OD_TN = 768
PROJ_TM = 512
TAIL_TM = 512
FFN_TC = 256


def _cparams(sem):
    return pltpu.CompilerParams(dimension_semantics=sem, vmem_limit_bytes=VMEM_LIMIT)


def _bdot(a, b):
    return jnp.dot(a.astype(BF16), b.astype(BF16), preferred_element_type=F32)


def _bdot_nt(a, b):
    return lax.dot_general(a.astype(BF16), b.astype(BF16), (((1,), (1,)), ((), ())),
                           preferred_element_type=F32)


def _bdot_tn(a, b):
    return lax.dot_general(a.astype(BF16), b.astype(BF16), (((0,), (0,)), ((), ())),
                           preferred_element_type=F32)


def _sigmoid(x):
    return 1.0 / (1.0 + jnp.exp(-x))


def _silu(x):
    return x * _sigmoid(x)


def _softplus(x):
    return jnp.maximum(x, 0.0) + jnp.log1p(jnp.exp(-jnp.abs(x)))


def _log_sigmoid(x):
    return -_softplus(-x)


def _resident(a, lead=None):
    if lead is None:
        return pl.BlockSpec(a.shape, lambda i: (0,) * a.ndim, pipeline_mode=pl.Buffered(1))
    return pl.BlockSpec((None,) + a.shape[1:], lambda i: (lead,) + (0,) * (a.ndim - 1),
                        pipeline_mode=pl.Buffered(1))


def _odd_in_body(x_ref, w_ref, wg_ref, o_ref, og_ref):
    xb = x_ref[...].astype(BF16)
    for c in range(OD_COLS // OD_TN):
        cols = slice(c * OD_TN, (c + 1) * OD_TN)
        o_ref[:, cols] = jnp.dot(xb, w_ref[:, cols], preferred_element_type=F32).astype(o_ref.dtype)
    og_ref[...] = jnp.dot(xb, wg_ref[...], preferred_element_type=F32)


def _odd_in_proj(x, w, w_gate_in, tm):
    t, k = x.shape
    return pl.pallas_call(
        _odd_in_body,
        grid=(t // tm,),
        in_specs=[pl.BlockSpec((tm, k), lambda i: (i, 0)), _resident(w), _resident(w_gate_in)],
        out_specs=[pl.BlockSpec((tm, OD_COLS), lambda i: (i, 0)), pl.BlockSpec((tm, LANES), lambda i: (i, 0))],
        out_shape=[jax.ShapeDtypeStruct((t, OD_COLS), BF16), jax.ShapeDtypeStruct((t, LANES), F32)],
        compiler_params=_cparams(("parallel",)),
        name="odd_in_proj",
    )(x, w, w_gate_in)


def _rel_bucket_np(dist):
    max_exact = NUM_BUCKETS // 2
    d = np.maximum(dist, 1).astype(np.float32)
    large = max_exact + (np.log(d / max_exact) / math.log(REL_MAX_DIST / max_exact)
                         * (NUM_BUCKETS - max_exact)).astype(np.int32)
    large = np.minimum(large, NUM_BUCKETS - 1)
    return np.where(dist < max_exact, dist, large).astype(np.int32)


def _bias_body(scale, base_bucket, tile_buckets, rb_ref, bucket_ref, neg_ref, o_ref):
    h = pl.program_id(0)
    t = pl.program_id(1)
    base = 0.0 if base_bucket is None else rb_ref[base_bucket, h]
    for tile, present in enumerate(tile_buckets):
        @pl.when(t == tile)
        def _(present=present):
            bucket = bucket_ref[0]
            acc = neg_ref[0]
            for b in present:
                acc = acc + jnp.where(bucket == b, (rb_ref[b, h] - base) * scale, 0.0)
            o_ref[0, 0] = acc


def _bias_tiles(rel_bias, bucket, neg, scale=1.0, base_bucket=None):
    nt, r, c = bucket.shape
    nh = rel_bias.shape[1]
    tile_buckets = tuple(tuple(int(b) for b in np.unique(bucket[t][neg[t] == 0]) if b != base_bucket)
                         for t in range(nt))
    return pl.pallas_call(
        functools.partial(_bias_body, scale, base_bucket, tile_buckets),
        grid=(nh, nt),
        in_specs=[pl.BlockSpec(memory_space=pltpu.SMEM),
                  pl.BlockSpec((1, r, c), lambda h, t: (t, 0, 0)),
                  pl.BlockSpec((1, r, c), lambda h, t: (t, 0, 0))],
        out_specs=pl.BlockSpec((1, 1, r, c), lambda h, t: (h, t, 0, 0)),
        out_shape=jax.ShapeDtypeStruct((nh, nt, r, c), F32),
        compiler_params=_cparams(("parallel", "parallel")),
        name="rel_bias_tiles",
    )(rel_bias, jnp.asarray(bucket), jnp.asarray(neg))


def _causal_conv(full, w, scratch):
    width = w.shape[0]
    rows = full.shape[0] - HALO
    scratch[...] = full
    y = w[width - 1:width, :] * full[HALO:]
    for j in range(width - 1):
        back = width - 1 - j
        y = y + w[j:j + 1, :] * scratch[HALO - back:HALO - back + rows, :]
    return y


def _even_in_body(tiles_per_seq, x_ref, halo_ref, wa_ref, wb_ref, cw_ref, oa_ref, ob_ref, scratch):
    i = pl.program_id(0)
    halo = jnp.where(i % tiles_per_seq == 0, 0.0, halo_ref[...])
    x = x_ref[...].astype(BF16)
    x_ext = jnp.concatenate([halo.astype(BF16), x], axis=0)
    for part in range(3):
        cols = slice(part * EVA_PART, (part + 1) * EVA_PART)
        y = jnp.dot(x_ext, wa_ref[:, cols], preferred_element_type=F32)
        c = _silu(_causal_conv(y, cw_ref[:, cols], scratch))
        for hd in range(GDN_HEADS):
            ch = c[:, hd * GDN_D:(hd + 1) * GDN_D]
            if part < 2:
                inv = lax.rsqrt(jnp.sum(ch * ch, axis=-1, keepdims=True) + RMS_EPS)
                ch = ch * (inv * GDN_D ** -0.5 if part == 0 else inv)
            lo = part * EVA_PART + hd * GDN_D
            oa_ref[:, lo:lo + GDN_D] = ch.astype(oa_ref.dtype)
    z_cols = slice(3 * EVA_PART, 4 * EVA_PART)
    oa_ref[:, z_cols] = jnp.dot(x, wa_ref[:, z_cols], preferred_element_type=F32).astype(oa_ref.dtype)
    for c in range(EVB_COLS // EVB_TN):
        cols = slice(c * EVB_TN, (c + 1) * EVB_TN)
        ob_ref[:, cols] = jnp.dot(x, wb_ref[:, cols], preferred_element_type=F32)


def _even_in_proj(x, w_a, w_b, conv_w, seq, tm):
    t, k = x.shape
    hb = tm // HALO
    return pl.pallas_call(
        functools.partial(_even_in_body, seq // tm),
        grid=(t // tm,),
        in_specs=[pl.BlockSpec((tm, k), lambda i: (i, 0)),
                  pl.BlockSpec((HALO, k), lambda i: (jnp.maximum(i * hb - 1, 0), 0)),
                  _resident(w_a), _resident(w_b), _resident(conv_w)],
        out_specs=[pl.BlockSpec((tm, w_a.shape[1]), lambda i: (i, 0)),
                   pl.BlockSpec((tm, EVB_COLS), lambda i: (i, 0))],
        out_shape=[jax.ShapeDtypeStruct((t, w_a.shape[1]), BF16), jax.ShapeDtypeStruct((t, EVB_COLS), F32)],
        scratch_shapes=[pltpu.VMEM((HALO + tm, EVA_PART), F32)],
        compiler_params=_cparams(("parallel",)),
        name="even_in_proj",
    )(x, x, w_a, w_b, conv_w)


def _gdn_body(q_ref, k_ref, v_ref, z_ref, ba_ref, alog_ref, dtb_ref, nw_ref, o_ref, state):
    n = pl.program_id(1)

    @pl.when(n == 0)
    def _():
        state[...] = jnp.zeros_like(state)

    grp = GDN_GROUP
    nc = grp // CHUNK
    nh = GDN_HEADS
    hs = range(nh)
    ri = lax.broadcasted_iota(jnp.int32, (grp, grp), 0)
    ci = lax.broadcasted_iota(jnp.int32, (grp, grp), 1)
    same = (ri // CHUNK) == (ci // CHUNK)
    incl = same & (ri >= ci)
    strict = same & (ri > ci)
    eye = (ri == ci).astype(F32)
    tri = incl.astype(BF16)

    def chunk_local(rows):
        ba = ba_ref[rows, :]
        beta_all = _sigmoid(ba)
        g_all = -jnp.exp(alog_ref[...]) * _softplus(ba + dtb_ref[...])
        g_hi = g_all.astype(BF16)
        rem = g_all - g_hi.astype(F32)
        g_mid = rem.astype(BF16)
        g_lo = (rem - g_mid.astype(F32)).astype(BF16)
        gam_all = (jnp.dot(tri, g_hi, preferred_element_type=F32)
                   + jnp.dot(tri, g_mid, preferred_element_type=F32)
                   + jnp.dot(tri, g_lo, preferred_element_type=F32))
        gam_rows = gam_all.T

        q = [q_ref[rows, h * GDN_D:(h + 1) * GDN_D] for h in hs]
        k = [k_ref[rows, h * GDN_D:(h + 1) * GDN_D] for h in hs]
        v = [v_ref[rows, h * GDN_D:(h + 1) * GDN_D] for h in hs]
        gam = [jnp.broadcast_to(gam_all[:, nh + h:nh + h + 1], (grp, GDN_D)) for h in hs]
        bcol = [beta_all[:, h:h + 1] for h in hs]
        decay, x, inv = [], [], []
        for h in hs:
            diff = gam[h][:, 0:1] - gam_rows[nh + h:nh + h + 1, :]
            decay.append(jnp.where(incl, jnp.exp(jnp.where(incl, diff, 0.0)), 0.0))
            kk = _bdot_nt(k[h], k[h])
            x.append(-jnp.where(strict, bcol[h] * kk * decay[h], 0.0))
            inv.append(eye + x[h])
        for _ in range(5):
            for h in hs:
                x[h] = _bdot(x[h], x[h])
                inv[h] = inv[h] + _bdot(inv[h], x[h])
        u, w, qk, q_dec, k_dec, g_last = [], [], [], [], [], []
        for h in hs:
            eg = jnp.exp(gam[h])
            uw = _bdot(inv[h], jnp.concatenate([v[h] * bcol[h], k[h] * (bcol[h] * eg)], axis=1))
            u.append(uw[:, :GDN_D])
            w.append(uw[:, GDN_D:])
            qk.append(_bdot_nt(q[h], k[h]) * decay[h])
            q_dec.append(q[h] * eg)
            kd, gl = [], []
            for c in range(nc):
                last = gam[h][(c + 1) * CHUNK - 1:(c + 1) * CHUNK, :]
                kd.append(k[h][c * CHUNK:(c + 1) * CHUNK] * jnp.exp(last - gam[h][c * CHUNK:(c + 1) * CHUNK]))
                gl.append(jnp.exp(last))
            k_dec.append(kd)
            g_last.append(gl)
        return u, w, qk, q_dec, k_dec, g_last

    def recurrence(rows, local, s):
        u, w, qk, q_dec, k_dec, g_last = local
        q_s = [[] for _ in hs]
        delta = [[] for _ in hs]
        for c in range(nc):
            sl = slice(c * CHUNK, (c + 1) * CHUNK)
            for h in hs:
                r = _bdot(jnp.concatenate([w[h][sl], q_dec[h][sl]], axis=0), s[h])
                d = u[h][sl] - r[:CHUNK]
                q_s[h].append(r[CHUNK:])
                delta[h].append(d)
                s[h] = g_last[h][c] * s[h] + _bdot_tn(k_dec[h][c], d)
        for h in hs:
            o = jnp.concatenate(q_s[h], axis=0) + _bdot(qk[h], jnp.concatenate(delta[h], axis=0))
            o = o * lax.rsqrt(jnp.mean(o * o, axis=-1, keepdims=True) + RMS_EPS) * nw_ref[...]
            z = z_ref[rows, h * GDN_D:(h + 1) * GDN_D].astype(F32)
            o_ref[rows, h * GDN_D:(h + 1) * GDN_D] = (o * _silu(z)).astype(o_ref.dtype)
        return s

    groups = [slice(g * grp, (g + 1) * grp) for g in range(GDN_STEP_GROUPS)]
    local = [chunk_local(rows) for rows in groups]
    s = [state[h] for h in hs]
    for rows, loc in zip(groups, local):
        s = recurrence(rows, loc, s)
    for h in hs:
        state[h] = s[h]


def _gdn(proj_a, proj_b, alog_pad, dtb_pad, norm_w, bsz, seq):
    t = proj_a.shape[0]
    rows = GDN_GROUP * GDN_STEP_GROUPS
    spb = seq // rows
    width = GDN_HEADS * GDN_D

    def at(col):
        return lambda b, n: (b * spb + n, col)

    return pl.pallas_call(
        _gdn_body,
        grid=(bsz, spb),
        in_specs=[pl.BlockSpec((rows, width), at(0)),
                  pl.BlockSpec((rows, width), at(1)),
                  pl.BlockSpec((rows, width), at(2)),
                  pl.BlockSpec((rows, width), at(3)),
                  pl.BlockSpec((rows, LANES), at(EVB_BA // LANES)),
                  pl.BlockSpec((1, LANES), lambda b, n: (0, 0)),
                  pl.BlockSpec((1, LANES), lambda b, n: (0, 0)),
                  pl.BlockSpec((1, LANES), lambda b, n: (0, 0))],
        out_specs=pl.BlockSpec((rows, width), at(0)),
        out_shape=jax.ShapeDtypeStruct((t, width), BF16),
        scratch_shapes=[pltpu.VMEM((GDN_HEADS, GDN_D, GDN_D), F32)],
        compiler_params=_cparams(("parallel", "arbitrary")),
        name="gdn",
    )(proj_a, proj_a, proj_a, proj_a, proj_b, alog_pad, dtb_pad, norm_w)


def _swa_bias_tables():
    qi = np.arange(SWA_BLOCK)[:, None] + SWA_BLOCK
    kj = np.arange(2 * SWA_BLOCK)[None, :]
    rel = qi - kj
    buckets, negs = [], []
    for window, dilation in SWA_CONFIGS:
        buckets.append(_rel_bucket_np(np.maximum(rel, 0) * dilation))
        negs.append(np.where((rel >= 0) & (rel <= window // dilation), 0.0, NEG_BIG))
    return np.stack(buckets).astype(np.int32), np.stack(negs).astype(np.float32)


def _swa_body(*refs):
    ng = len(SWA_CONFIGS)
    ins, bias_ref, o_ref = refs[:5 * ng], refs[5 * ng], refs[5 * ng + 1]
    scratch = refs[5 * ng + 2:]
    kbufs, vbufs, o_scr, lse_scr = scratch[:ng], scratch[ng:2 * ng], scratch[2 * ng], scratch[2 * ng + 1]
    j = pl.program_id(1)
    blk = SWA_BLOCK
    first_head = lax.broadcasted_iota(jnp.int32, (blk, LANES), 1) < SWA_DH
    in_prev = lax.broadcasted_iota(jnp.int32, (2 * blk, 2 * blk), 1) < blk

    for g, (_, d) in enumerate(SWA_CONFIGS):
        q_ref, kc_ref, kp_ref, vc_ref, vp_ref = ins[5 * g:5 * g + 5]
        kbuf, vbuf = kbufs[g], vbufs[g]
        halo = blk * d
        kbuf[0:halo, :] = kp_ref[...]
        kbuf[halo:, :] = kc_ref[...]
        vbuf[0:halo, :] = vp_ref[...]
        vbuf[halo:, :] = vc_ref[...]

        def unit(u, carry, g=g, d=d, halo=halo, q_ref=q_ref, kbuf=kbuf, vbuf=vbuf):
            base = (u // d) * halo + u % d
            q = q_ref[pl.ds(base, blk, stride=d), :] * SWA_DH ** -0.5
            k = kbuf[pl.ds(base, 2 * blk, stride=d), :]
            v = vbuf[pl.ds(base, 2 * blk, stride=d), :]
            lhs = jnp.concatenate([jnp.where(first_head, q, 0.0), jnp.where(first_head, 0.0, q)], axis=0)
            s = _bdot_nt(lhs, k) + bias_ref[g, 0]
            no_prev = (j == 0) & (u < d)
            s = jnp.where(in_prev & no_prev, NEG_BIG, s)
            m = jnp.max(s, axis=-1, keepdims=True)
            p = jnp.exp(s - m)
            l = jnp.sum(p, axis=-1, keepdims=True)
            o2 = _bdot(p / l, v)
            lse2 = m + jnp.log(l)
            o_scr[g, pl.ds(base, blk, stride=d), :] = jnp.where(first_head, o2[:blk], o2[blk:])
            lse_scr[g, pl.ds(base, blk, stride=d), :] = jnp.where(first_head, lse2[:blk], lse2[blk:])
            return carry

        lax.fori_loop(0, SWA_SPAN // blk, unit, 0, unroll=SWA_UNROLL)

    def combine(c, carry):
        rows = pl.ds(pl.multiple_of(c * 2 * blk, 2 * blk), 2 * blk)
        lse = [lse_scr[g, rows, :] for g in range(ng)]
        m = functools.reduce(jnp.maximum, lse)
        e = [jnp.exp(x - m) for x in lse]
        den = functools.reduce(lambda x, y: x + y, e)
        o_ref[rows, :] = functools.reduce(
            lambda x, y: x + y, [(e[g] / den) * o_scr[g, rows, :] for g in range(ng)]).astype(o_ref.dtype)
        return carry

    lax.fori_loop(0, SWA_SPAN // (2 * blk), combine, 0)


def _swa(proj, bias, bsz, seq):
    t = proj.shape[0]
    ng = len(SWA_CONFIGS)
    nspan = seq // SWA_SPAN
    npair = SWA_HEADS * SWA_DH // LANES
    group_cols = SWA_HEADS * SWA_DH // LANES
    in_specs, scratch_k = [], []
    for g, (_, d) in enumerate(SWA_CONFIGS):
        halo = SWA_BLOCK * d
        per_span = SWA_SPAN // halo

        def cur(which, g=g):
            col = (which * ng + g) * group_cols
            return lambda b, j, p: (b * nspan + j, col + p)

        def prev(which, g=g, per_span=per_span):
            col = (which * ng + g) * group_cols
            return lambda b, j, p: (jnp.maximum((b * nspan + j) * per_span - 1, 0), col + p)

        in_specs += [pl.BlockSpec((SWA_SPAN, LANES), cur(0)),
                     pl.BlockSpec((SWA_SPAN, LANES), cur(1)), pl.BlockSpec((halo, LANES), prev(1)),
                     pl.BlockSpec((SWA_SPAN, LANES), cur(2)), pl.BlockSpec((halo, LANES), prev(2))]
        scratch_k.append(pltpu.VMEM((halo + SWA_SPAN, LANES), F32))
    in_specs.append(pl.BlockSpec((ng, 1, 2 * SWA_BLOCK, 2 * SWA_BLOCK), lambda b, j, p: (0, p, 0, 0)))
    return pl.pallas_call(
        _swa_body,
        grid=(bsz, nspan, npair),
        in_specs=in_specs,
        out_specs=pl.BlockSpec((SWA_SPAN, LANES), lambda b, j, p: (b * nspan + j, p)),
        out_shape=jax.ShapeDtypeStruct((t, npair * LANES), BF16),
        scratch_shapes=scratch_k + scratch_k + [pltpu.VMEM((ng, SWA_SPAN, LANES), F32)] * 2,
        compiler_params=_cparams(("parallel", "parallel", "parallel")),
        name="swa",
    )(*([proj] * (5 * ng)), bias)


def _layer_norm(y, g, b):
    mu = jnp.mean(y, axis=-1, keepdims=True)
    yc = y - mu
    var = jnp.mean(yc * yc, axis=-1, keepdims=True)
    return yc * lax.rsqrt(var + LN_EPS) * g + b


def _diff_bias_tables(blk):
    buckets = _rel_bucket_np(np.arange(2 * REL_MAX_DIST))
    far = int(np.max(np.nonzero(buckets != NUM_BUCKETS - 1)[0])) + 1
    nb = -(-(far + blk - 1) // blk)
    kk = np.arange(blk)[:, None]
    qq = np.arange(blk)[None, :]
    dist = np.stack([t * blk + qq - kk for t in range(nb + 1)])
    bucket = _rel_bucket_np(np.maximum(dist, 0))
    neg = np.where(dist >= 0, 0.0, NEG_BIG).astype(np.float32)
    return bucket, neg


def _diff_body(blk, nb, lam_init, q_ref, qn_ref, k_ref, v_ref, bias_ref, lam_ref, nw_ref, o_ref,
               vt, acc1, acc2, s_a, s_b):
    qi = pl.program_id(2)
    dv = 2 * DIFF_DH
    seq = k_ref.shape[0]

    @pl.when(qi == 0)
    def _():
        vt[dv:, :] = jnp.ones((vt.shape[0] - dv, seq), BF16)

        def fill(c, carry):
            st = pl.multiple_of(c * blk, blk)
            vt[0:dv, pl.ds(st, blk)] = v_ref[pl.ds(st, blk), :].astype(F32).T.astype(BF16)
            return carry

        lax.fori_loop(0, seq // blk, fill, 0)

    lane = lax.broadcasted_iota(jnp.int32, (blk, dv), 1)

    def components(ref):
        q = ref[...].astype(F32) * (DIFF_DH ** -0.5 * LOG2E)
        return (jnp.where(lane < DIFF_DH, q, 0.0).astype(BF16), jnp.where(lane >= DIFF_DH, q, 0.0).astype(BF16))

    qs = components(q_ref)
    qs_next = components(qn_ref)
    accs = (acc1, acc2)
    for acc in accs:
        acc[...] = jnp.zeros_like(acc)

    last = pl.num_programs(2) - 1

    def key_rows(kj):
        return pl.ds(pl.multiple_of(jnp.minimum(kj, last) * blk, blk), blk)

    def scores(kj, dst, queries=qs):
        k = k_ref[key_rows(kj), :]
        for c, qc in enumerate(queries):
            dst[c] = lax.dot_general(k, qc, (((1,), (1,)), ((), ())), preferred_element_type=F32)

    def consume(biased, kj, src, ms):
        vtb = vt[:, key_rows(kj)]
        out = []
        for c, (m, acc) in enumerate(zip(ms, accs)):
            s = src[c]
            if biased:
                s = s + bias_ref[0, jnp.minimum(qi - kj, nb)]
            m_new = jnp.maximum(m, jnp.max(s, axis=0, keepdims=True))
            p = jnp.exp2(s - m_new).astype(BF16)
            acc[...] = jnp.exp2(m - m_new) * acc[...] + jnp.dot(vtb, p, preferred_element_type=F32)
            out.append(m_new)
        return tuple(out)

    nblocks = qi + 1
    trips = nblocks // 2
    odd = nblocks % 2 == 1
    far_trips = jnp.maximum(qi - nb + 1, 0) // 2

    def pair(biased, t, ms):
        kj = 2 * t
        scores(kj + 1, s_b)
        ms = consume(biased, kj, s_a, ms)
        if biased:
            hand_off = (t == trips - 1) & jnp.logical_not(odd)
            scores(jnp.where(hand_off, 0, kj + 2), s_a,
                   tuple(jnp.where(hand_off, qn, qc) for qc, qn in zip(qs, qs_next)))
        else:
            scores(kj + 2, s_a)
        return consume(biased, kj + 1, s_b, ms)

    @pl.when(qi == 0)
    def _():
        scores(0, s_a)

    m0 = jnp.full((1, blk), NEG_BIG, F32)
    ms = lax.fori_loop(0, far_trips, functools.partial(pair, False), (m0, m0))
    ms = lax.fori_loop(far_trips, trips, functools.partial(pair, True), ms)

    @pl.when(odd)
    def _():
        consume(True, qi, s_a, ms)
        scores(0, s_a, qs_next)

    lp = lam_ref[...]
    lam = (jnp.exp(jnp.sum(lp[0:1] * lp[1:2], axis=-1, keepdims=True))
           - jnp.exp(jnp.sum(lp[2:3] * lp[3:4], axis=-1, keepdims=True)) + lam_init)
    a1, a2 = acc1[...], acc2[...]
    o_t = a1[:dv] * (1.0 / a1[dv:dv + 1]) - a2[:dv] * (lam / a2[dv:dv + 1])
    o = o_t.T
    o = o * lax.rsqrt(jnp.mean(o * o, axis=-1, keepdims=True) + RMS_EPS) * nw_ref[...]
    o_ref[...] = (o * (1.0 - lam_init)).astype(o_ref.dtype)


def _diff_attention(proj, bias, lam_params, norm_w, lam_init, bsz, seq, blk):
    t = proj.shape[0]
    nq = seq // blk
    nt = bias.shape[1]
    nb = nt - 1
    width = 2 * DIFF_DH
    ones_rows = 2 * SUBLANES
    return pl.pallas_call(
        functools.partial(_diff_body, blk, nb, lam_init),
        grid=(bsz, DIFF_HEADS, nq),
        in_specs=[pl.BlockSpec((blk, width), lambda b, h, i: (b * nq + i, OD_QC // width + h)),
                  pl.BlockSpec((blk, width),
                               lambda b, h, i: (b * nq + jnp.minimum(i + 1, nq - 1), OD_QC // width + h)),
                  pl.BlockSpec((seq, width), lambda b, h, i: (b, OD_KC // width + h)),
                  pl.BlockSpec((seq, width), lambda b, h, i: (b, OD_VC // width + h)),
                  pl.BlockSpec((1, nt, blk, blk), lambda b, h, i: (h, 0, 0, 0)),
                  pl.BlockSpec((4, DIFF_DH), lambda b, h, i: (0, 0)),
                  pl.BlockSpec((1, width), lambda b, h, i: (0, 0))],
        out_specs=pl.BlockSpec((blk, width), lambda b, h, i: (b * nq + i, h)),
        out_shape=jax.ShapeDtypeStruct((t, DIFF_HEADS * width), BF16),
        scratch_shapes=[pltpu.VMEM((width + ones_rows, seq), BF16),
                        pltpu.VMEM((width + ones_rows, blk), F32), pltpu.VMEM((width + ones_rows, blk), F32),
                        pltpu.VMEM((2, blk, blk), F32), pltpu.VMEM((2, blk, blk), F32)],
        compiler_params=_cparams(("arbitrary", "arbitrary", "arbitrary")),
        name="diff_attn",
    )(proj, proj, proj, proj, bias, lam_params, norm_w)


def _gla_body(q_ref, k_ref, gd_ref, v_ref, r_ref, wg_ref, bg_ref, nw_ref, o_ref, state, part):
    n = pl.program_id(1)

    @pl.when(n == 0)
    def _():
        state[...] = jnp.zeros_like(state)

    rows = GLA_GROUP
    nc = rows // CHUNK
    npair = GLA_HEADS // 2
    gate = jnp.dot(gd_ref[...], wg_ref[...], precision=HI, preferred_element_type=F32) + bg_ref[...]
    log_a = _log_sigmoid(gate) * (1.0 / GLA_TAU)

    ri = lax.broadcasted_iota(jnp.int32, (rows, rows), 0)
    ci = lax.broadcasted_iota(jnp.int32, (rows, rows), 1)
    causal = ((ri // CHUNK) == (ci // CHUNK)) & (ri >= ci)
    tri = causal.astype(BF16)
    hi = log_a.astype(BF16)
    rem = log_a - hi.astype(F32)
    mid = rem.astype(BF16)
    lo = (rem - mid.astype(F32)).astype(BF16)
    b_all = (jnp.dot(tri, hi, preferred_element_type=F32) + jnp.dot(tri, mid, preferred_element_type=F32)
             + jnp.dot(tri, lo, preferred_element_type=F32))

    lane = lax.broadcasted_iota(jnp.int32, (rows, LANES), 1)
    lane_c = lax.broadcasted_iota(jnp.int32, (CHUNK, LANES), 1)
    head_lanes = (lane < GLA_DK, lane >= GLA_DK)
    chunk_lanes = (lane_c < GLA_DK, lane_c >= GLA_DK)
    pairs = []
    for p in range(npair):
        cols = slice(p * LANES, (p + 1) * LANES)
        pairs.append(dict(
            b=b_all[:, cols],
            q=q_ref[:, cols].astype(F32) * GLA_DK ** -0.5,
            k=k_ref[:, cols].astype(F32),
            v=[v_ref[:, (2 * p + hd) * GLA_DV:(2 * p + hd + 1) * GLA_DV].astype(F32) for hd in range(2)]))

    for pr in pairs:
        pr["q_dec"] = pr["q"] * jnp.exp(pr["b"])
    for c in range(nc):
        sl = slice(c * CHUNK, (c + 1) * CHUNK)
        for p, pr in enumerate(pairs):
            bc = pr["b"][sl]
            b_last = bc[CHUNK - 1:CHUNK]
            k_dec = pr["k"][sl] * jnp.exp(b_last - bc)
            e_last = jnp.exp(b_last)
            for hd in range(2):
                h = 2 * p + hd
                st = state[h]
                part[h, sl, :] = _bdot_nt(jnp.where(chunk_lanes[hd], pr["q_dec"][sl], 0.0), st)
                state[h] = st * e_last + _bdot_tn(pr["v"][hd][sl], k_dec)

    def finish(intra):
        for h in range(GLA_HEADS):
            o = part[h] + intra[h]
            o = o * lax.rsqrt(jnp.mean(o * o, axis=-1, keepdims=True) + RMS_EPS) * nw_ref[...]
            gate_r = _silu(r_ref[:, h * GLA_DV:(h + 1) * GLA_DV].astype(F32))
            o_ref[:, h * GLA_DV:(h + 1) * GLA_DV] = (o * gate_r).astype(o_ref.dtype)

    def intra_whole_chunk():
        out = []
        for pr in pairs:
            k_inv = pr["k"] * jnp.exp(jnp.minimum(-pr["b"], GLA_MAX_DECAY))
            for hd in range(2):
                a = _bdot_nt(jnp.where(head_lanes[hd], pr["q_dec"], 0.0), k_inv)
                out.append(_bdot(jnp.where(causal, a, 0.0), pr["v"][hd]))
        return out

    def intra_exact():
        out = []
        for pr in pairs:
            off = _gla_intra_off_diagonal(pr["q"], pr["k"], pr["b"], pr["v"], nc)
            diag = _gla_intra_diagonal(pr["q"], pr["k"], pr["b"], pr["v"])
            out += [off[hd] + diag[hd] for hd in range(2)]
        return out

    chunk_decay = jnp.max(-b_all.reshape(nc, CHUNK, npair * LANES)[:, CHUNK - 1:CHUNK, :])

    @pl.when(chunk_decay <= GLA_MAX_DECAY)
    def _():
        finish(intra_whole_chunk())

    @pl.when(chunk_decay > GLA_MAX_DECAY)
    def _():
        finish(intra_exact())


def _gla_intra_off_diagonal(q, k, b, vs, nc):
    per_chunk = CHUNK // GLA_SUB
    lane = lax.broadcasted_iota(jnp.int32, (GLA_SUB, LANES), 1)
    sub_mask = (lane < GLA_DK, lane >= GLA_DK)
    kcol = lax.broadcasted_iota(jnp.int32, (GLA_SUB, CHUNK), 1)
    outs = ([], [])
    for c in range(nc):
        sl = slice(c * CHUNK, (c + 1) * CHUNK)
        bc, qc, kc = b[sl], q[sl], k[sl]
        a_rows = [[jnp.zeros((GLA_SUB, CHUNK), F32)] for _ in vs]
        for blk in range(1, per_chunk):
            r0 = blk * GLA_SUB
            bref = bc[r0:r0 + 1]
            qs = qc[r0:r0 + GLA_SUB] * jnp.exp(bc[r0:r0 + GLA_SUB] - bref)
            ks = kc * jnp.exp(jnp.minimum(bref - bc, 0.0))
            for hd in range(2):
                a = _bdot_nt(jnp.where(sub_mask[hd], qs, 0.0), ks)
                a_rows[hd].append(jnp.where(kcol < r0, a, 0.0))
        for hd in range(2):
            outs[hd].append(_bdot(jnp.concatenate(a_rows[hd], axis=0), vs[hd][sl]))
    return [jnp.concatenate(o, axis=0) for o in outs]


def _gla_intra_diagonal(q, k, b, vs):
    rows = q.shape[0]
    nsub = rows // GLA_SUB
    b3 = b.reshape(nsub, GLA_SUB, LANES)
    q3 = q.reshape(nsub, GLA_SUB, LANES)
    k3 = k.reshape(nsub, GLA_SUB, LANES)
    v3 = [v.reshape(nsub, GLA_SUB, LANES) for v in vs]
    row3 = lax.broadcasted_iota(jnp.int32, (nsub, GLA_SUB, LANES), 1)
    lane3 = lax.broadcasted_iota(jnp.int32, (nsub, GLA_SUB, LANES), 2)
    rowc = lax.broadcasted_iota(jnp.int32, (nsub, GLA_SUB, 1), 1)
    o3 = [jnp.zeros((nsub, GLA_SUB, LANES), F32) for _ in vs]
    for jj in range(GLA_SUB):
        e = jnp.exp(jnp.where(row3 >= jj, b3 - b3[:, jj:jj + 1, :], 0.0))
        t = q3 * k3[:, jj:jj + 1, :] * e
        w_all = jnp.sum(t, axis=-1, keepdims=True)
        w_a = jnp.sum(jnp.where(lane3 < GLA_DK, t, 0.0), axis=-1, keepdims=True)
        for hd, w in enumerate((w_a, w_all - w_a)):
            o3[hd] = o3[hd] + jnp.where(rowc >= jj, w, 0.0) * v3[hd][:, jj:jj + 1, :]
    return [o.reshape(rows, LANES) for o in o3]


def _gla(proj, gate_in, w_gate_pad, b_gate, norm_w, bsz, seq):
    t = proj.shape[0]
    rows = GLA_GROUP
    spb = seq // rows
    qk_w = GLA_HEADS * GLA_DK
    v_w = GLA_HEADS * GLA_DV

    def at(col):
        return lambda b, n: (b * spb + n, col)

    def whole(a):
        return pl.BlockSpec(a.shape, lambda b, n: (0, 0))

    return pl.pallas_call(
        _gla_body,
        grid=(bsz, spb),
        in_specs=[pl.BlockSpec((rows, qk_w), at(OD_QD // qk_w)),
                  pl.BlockSpec((rows, qk_w), at(OD_KD // qk_w)),
                  pl.BlockSpec((rows, LANES), at(0)),
                  pl.BlockSpec((rows, v_w), at(OD_VD // v_w)),
                  pl.BlockSpec((rows, v_w), at(OD_RD // v_w)),
                  whole(w_gate_pad), whole(b_gate), whole(norm_w)],
        out_specs=pl.BlockSpec((rows, v_w), at(0)),
        out_shape=jax.ShapeDtypeStruct((t, v_w), BF16),
        scratch_shapes=[pltpu.VMEM((GLA_HEADS, GLA_DV, LANES), F32), pltpu.VMEM((GLA_HEADS, rows, GLA_DV), F32)],
        compiler_params=_cparams(("parallel", "arbitrary")),
        name="gla",
    )(proj, proj, gate_in, proj, proj, w_gate_pad, b_gate, norm_w)


def _tail_body(tiles_per_seq, tc, ma_ref, ma_halo, mb_ref, mb_halo, h_ref, h_halo, woa_ref, wob_ref,
               g1_ref, b1_ref, wu_ref, cw_ref, cb_ref, wd_ref, g2_ref, b2_ref, y_ref,
               act, scratch_g, scratch_v):
    i = pl.program_id(0)

    def ext(halo_ref, ref):
        return jnp.concatenate([halo_ref[...], ref[...]], axis=0)

    mix = (jnp.dot(ext(ma_halo, ma_ref), woa_ref[...], preferred_element_type=F32)
           + jnp.dot(ext(mb_halo, mb_ref), wob_ref[...], preferred_element_type=F32))
    x_ext = _layer_norm(DEEPNORM_ALPHA * ext(h_halo, h_ref) + mix, g1_ref[...], b1_ref[...])
    x = x_ext[HALO:]
    row = lax.broadcasted_iota(jnp.int32, x_ext.shape, 0)
    first = i % tiles_per_seq == 0
    xb = jnp.where((row < HALO) & first, 0.0, x_ext).astype(BF16)

    def branch(lo, scratch):
        full = jnp.dot(xb, wu_ref[:, lo:lo + tc], preferred_element_type=F32)
        return _causal_conv(full, cw_ref[:, lo:lo + tc], scratch) + cb_ref[:, lo:lo + tc]

    for c in range(D_FF // tc):
        gate = branch(c * tc, scratch_g)
        val = branch(D_FF + c * tc, scratch_v)
        act[:, c * tc:(c + 1) * tc] = (_silu(gate) * val).astype(BF16)
    ffn = jnp.dot(act[...], wd_ref[...], preferred_element_type=F32)
    y_ref[...] = _layer_norm(DEEPNORM_ALPHA * x + ffn, g2_ref[...], b2_ref[...])


def _layer_tail(mix_a, mix_b, h, w_out_a, w_out_b, g1, b1, w_up, conv_w, conv_b, w_down, g2, b2,
                layer, seq):
    t = h.shape[0]
    tm, tc = TAIL_TM, FFN_TC
    hb = tm // HALO

    def tile(a):
        return pl.BlockSpec((tm, a.shape[1]), lambda i: (i, 0))

    def halo(a):
        return pl.BlockSpec((HALO, a.shape[1]), lambda i: (jnp.maximum(i * hb - 1, 0), 0))

    params = (w_out_a, w_out_b, g1, b1, w_up, conv_w, conv_b, w_down, g2, b2)
    stacked = (w_up, w_down)
    return pl.pallas_call(
        functools.partial(_tail_body, seq // tm, tc),
        grid=(t // tm,),
        in_specs=[tile(mix_a), halo(mix_a), tile(mix_b), halo(mix_b), tile(h), halo(h)]
        + [_resident(p, layer if any(p is s for s in stacked) else None) for p in params],
        out_specs=pl.BlockSpec((tm, D_MODEL), lambda i: (i, 0)),
        out_shape=jax.ShapeDtypeStruct((t, D_MODEL), F32),
        scratch_shapes=[pltpu.VMEM((tm, D_FF), BF16), pltpu.VMEM((HALO + tm, tc), F32),
                        pltpu.VMEM((HALO + tm, tc), F32)],
        compiler_params=_cparams(("parallel",)),
        name="layer_tail",
    )(mix_a, mix_a, mix_b, mix_b, h, h, *params)


def _even_w_in(w):
    a_end = 4 * EVA_PART
    gates = w[:, a_end:a_end + 2 * GDN_HEADS]
    qkv_b = w[:, a_end + 2 * GDN_HEADS:]
    pad = jnp.zeros((w.shape[0], EVB_COLS - EVB_BA - 2 * GDN_HEADS), w.dtype)
    return w[:, :a_end].astype(BF16), jnp.concatenate([qkv_b, gates, pad], axis=1).astype(BF16)


def _odd_w_in(w):
    pad = jnp.zeros((w.shape[0], LANES - GLA_RANK), w.dtype)
    return w[:, :OD_COLS].astype(BF16), jnp.concatenate([w[:, OD_COLS:], pad], axis=1).astype(BF16)


def _even_mixer(h, rel_bias, w_in, conv_w, a_log, dt_bias, norm_w, w_out, bsz, seq):
    w_a, w_b = _even_w_in(w_in)
    proj_a, proj_b = _even_in_proj(h, w_a, w_b, conv_w, seq, PROJ_TM)
    gate_pad = jnp.zeros((1, LANES), F32)
    alog_pad = lax.dynamic_update_slice(gate_pad, a_log[None].astype(F32), (0, GDN_HEADS))
    dtb_pad = lax.dynamic_update_slice(gate_pad, dt_bias[None].astype(F32), (0, GDN_HEADS))
    o_a = _gdn(proj_a, proj_b, alog_pad, dtb_pad, norm_w[None], bsz, seq)
    tiles = _bias_tiles(rel_bias, *_swa_bias_tables())
    ng, two_blk = len(SWA_CONFIGS), 2 * SWA_BLOCK
    bias = tiles.reshape(SWA_HEADS // 2, 2, ng, SWA_BLOCK, two_blk).transpose(2, 0, 1, 3, 4)
    o_b = _swa(proj_b, bias.reshape(ng, SWA_HEADS // 2, two_blk, two_blk), bsz, seq)
    w_out = w_out.astype(BF16)
    return o_a, o_b, w_out[:EVA_PART], w_out[EVA_PART:]


def _odd_mixer(h, rel_bias, w_in, lam_params, diff_norm_w, w_gate, b_gate, gla_norm_w, w_out,
               lam_init, bsz, seq):
    w_main, w_gd = _odd_w_in(w_in)
    proj, gate_in = _odd_in_proj(h, w_main, w_gd, PROJ_TM)
    blk = min(DIFF_BLOCK, seq)
    bucket, neg = _diff_bias_tables(blk)
    bias = _bias_tiles(rel_bias, bucket, neg, scale=LOG2E, base_bucket=NUM_BUCKETS - 1)
    o_c = _diff_attention(proj, bias, lam_params, diff_norm_w[None], lam_init, bsz, seq, blk)
    w_gate_pad = jnp.concatenate(
        [w_gate, jnp.zeros((LANES - GLA_RANK, w_gate.shape[1]), w_gate.dtype)], axis=0)
    o_d = _gla(proj, gate_in, w_gate_pad, b_gate[None], gla_norm_w[None], bsz, seq)
    diff_v = DIFF_HEADS * 2 * DIFF_DH
    w_out = w_out.astype(BF16)
    return o_c, o_d, w_out[:diff_v], w_out[diff_v:]


def kernel(x, rel_bias, w_in_even, gdn_conv_w, gdn_a_log, gdn_dt_bias, gdn_norm_w, w_out_even,
           w_in_odd, diff_lambda, diff_norm_w, gla_w_gate, gla_b_gate, gla_norm_w, w_out_odd,
           ffn_w_up, ffn_conv_w, ffn_conv_b, ffn_w_down, ln_g, ln_b):
    bsz, seq, d = x.shape
    h = x.reshape(bsz * seq, d)
    w_up, w_down = ffn_w_up.astype(BF16), ffn_w_down.astype(BF16)
    for layer in range(DEPTH):
        i = layer // 2
        if layer % 2 == 0:
            mixed = _even_mixer(h, rel_bias, w_in_even[i], gdn_conv_w[i], gdn_a_log[i], gdn_dt_bias[i],
                                gdn_norm_w[i], w_out_even[i], bsz, seq)
        else:
            lam_init = 0.8 - 0.6 * math.exp(-0.3 * layer)
            mixed = _odd_mixer(h, rel_bias, w_in_odd[i], diff_lambda[i], diff_norm_w[i], gla_w_gate[i],
                               gla_b_gate[i], gla_norm_w[i], w_out_odd[i], lam_init, bsz, seq)
        h = _layer_tail(*mixed[:2], h, *mixed[2:], ln_g[layer, 0][None], ln_b[layer, 0][None],
                        w_up, ffn_conv_w[layer], ffn_conv_b[layer][None], w_down,
                        ln_g[layer, 1][None], ln_b[layer, 1][None], layer, seq)
    return h.reshape(bsz, seq, d)
```

```python
import functools
import math

import numpy as np
import jax
import jax.numpy as jnp
from jax import lax
from jax.experimental import pallas as pl
from jax.experimental.pallas import tpu as pltpu

F32 = jnp.float32
BF16 = jnp.bfloat16
HI = lax.Precision.HIGHEST

D_MODEL = 1024
DEPTH = 2
DEEPNORM_ALPHA = (2 * DEPTH) ** 0.25
LN_EPS = 1e-5
RMS_EPS = 1e-6
NUM_BUCKETS = 32
REL_MAX_DIST = 2048
GDN_HEADS = 6
GDN_D = 128
CHUNK = 64
GDN_GROUP = 256
GDN_STEP_GROUPS = 1
SWA_CONFIGS = ((128, 1), (512, 4), (2048, 16))
SWA_HEADS = 4
SWA_DH = 64
SWA_BLOCK = 128
SWA_UNROLL = 8
SWA_SPAN = SWA_BLOCK * max(d for _, d in SWA_CONFIGS)
DIFF_HEADS = 4
DIFF_DH = 64
DIFF_BLOCK = 512
LOG2E = math.log2(math.e)
GLA_HEADS = 4
GLA_DK = 64
GLA_DV = 128
GLA_RANK = 16
GLA_TAU = 16.0
GLA_SUB = 16
GLA_GROUP = 256
GLA_MAX_DECAY = 60.0
D_FF = 2816

LANES = 128
SUBLANES = 8
HALO = 16
VMEM_LIMIT = 56 * 1024 * 1024
NEG_BIG = -1e30

EVA_PART = GDN_HEADS * GDN_D
EVB_BA = 2304
EVB_COLS = 2560
EVB_TN = 512
OD_QC, OD_KC, OD_VC = 0, 512, 1024
OD_QD, OD_KD, OD_VD, OD_RD = 1536, 1792, 2048, 2560
OD_COLS = 3072
OD_TN = 768
PROJ_TM = 512
TAIL_TM = 1024
FFN_TC = 256


def _cparams(sem):
    return pltpu.CompilerParams(dimension_semantics=sem, vmem_limit_bytes=VMEM_LIMIT)


def _bdot(a, b):
    return jnp.dot(a.astype(BF16), b.astype(BF16), preferred_element_type=F32)


def _bdot_nt(a, b):
    return lax.dot_general(a.astype(BF16), b.astype(BF16), (((1,), (1,)), ((), ())),
                           preferred_element_type=F32)


def _bdot_tn(a, b):
    return lax.dot_general(a.astype(BF16), b.astype(BF16), (((0,), (0,)), ((), ())),
                           preferred_element_type=F32)


def _sigmoid(x):
    return 1.0 / (1.0 + jnp.exp(-x))


def _silu(x):
    return x * _sigmoid(x)


def _softplus(x):
    return jnp.maximum(x, 0.0) + jnp.log1p(jnp.exp(-jnp.abs(x)))


def _log_sigmoid(x):
    return -_softplus(-x)


def _resident(a, lead=None):
    if lead is None:
        return pl.BlockSpec(a.shape, lambda i: (0,) * a.ndim, pipeline_mode=pl.Buffered(1))
    return pl.BlockSpec((None,) + a.shape[1:], lambda i: (lead,) + (0,) * (a.ndim - 1),
                        pipeline_mode=pl.Buffered(1))


def _odd_in_body(x_ref, w_ref, wg_ref, o_ref, og_ref):
    xb = x_ref[...].astype(BF16)
    for c in range(OD_COLS // OD_TN):
        cols = slice(c * OD_TN, (c + 1) * OD_TN)
        o_ref[:, cols] = jnp.dot(xb, w_ref[:, cols], preferred_element_type=F32).astype(o_ref.dtype)
    og_ref[...] = jnp.dot(xb, wg_ref[...], preferred_element_type=F32)


def _odd_in_proj(x, w, w_gate_in, tm):
    t, k = x.shape
    return pl.pallas_call(
        _odd_in_body,
        grid=(t // tm,),
        in_specs=[pl.BlockSpec((tm, k), lambda i: (i, 0)), _resident(w), _resident(w_gate_in)],
        out_specs=[pl.BlockSpec((tm, OD_COLS), lambda i: (i, 0)), pl.BlockSpec((tm, LANES), lambda i: (i, 0))],
        out_shape=[jax.ShapeDtypeStruct((t, OD_COLS), BF16), jax.ShapeDtypeStruct((t, LANES), F32)],
        compiler_params=_cparams(("parallel",)),
        name="odd_in_proj",
    )(x, w, w_gate_in)


def _rel_bucket_np(dist):
    max_exact = NUM_BUCKETS // 2
    d = np.maximum(dist, 1).astype(np.float32)
    large = max_exact + (np.log(d / max_exact) / math.log(REL_MAX_DIST / max_exact)
                         * (NUM_BUCKETS - max_exact)).astype(np.int32)
    large = np.minimum(large, NUM_BUCKETS - 1)
    return np.where(dist < max_exact, dist, large).astype(np.int32)


def _bias_body(scale, base_bucket, tile_buckets, rb_ref, bucket_ref, neg_ref, o_ref):
    h = pl.program_id(0)
    t = pl.program_id(1)
    base = 0.0 if base_bucket is None else rb_ref[base_bucket, h]
    for tile, present in enumerate(tile_buckets):
        @pl.when(t == tile)
        def _(present=present):
            bucket = bucket_ref[0]
            acc = neg_ref[0]
            for b in present:
                acc = acc + jnp.where(bucket == b, (rb_ref[b, h] - base) * scale, 0.0)
            o_ref[0, 0] = acc


def _bias_tiles(rel_bias, bucket, neg, scale=1.0, base_bucket=None):
    nt, r, c = bucket.shape
    nh = rel_bias.shape[1]
    tile_buckets = tuple(tuple(int(b) for b in np.unique(bucket[t][neg[t] == 0]) if b != base_bucket)
                         for t in range(nt))
    return pl.pallas_call(
        functools.partial(_bias_body, scale, base_bucket, tile_buckets),
        grid=(nh, nt),
        in_specs=[pl.BlockSpec(memory_space=pltpu.SMEM),
                  pl.BlockSpec((1, r, c), lambda h, t: (t, 0, 0)),
                  pl.BlockSpec((1, r, c), lambda h, t: (t, 0, 0))],
        out_specs=pl.BlockSpec((1, 1, r, c), lambda h, t: (h, t, 0, 0)),
        out_shape=jax.ShapeDtypeStruct((nh, nt, r, c), F32),
        compiler_params=_cparams(("parallel", "parallel")),
        name="rel_bias_tiles",
    )(rel_bias, jnp.asarray(bucket), jnp.asarray(neg))


def _causal_conv(full, w, scratch):
    width = w.shape[0]
    rows = full.shape[0] - HALO
    scratch[...] = full
    y = w[width - 1:width, :] * full[HALO:]
    for j in range(width - 1):
        back = width - 1 - j
        y = y + w[j:j + 1, :] * scratch[HALO - back:HALO - back + rows, :]
    return y


def _even_in_body(tiles_per_seq, x_ref, halo_ref, wa_ref, wb_ref, cw_ref, oa_ref, ob_ref, scratch):
    i = pl.program_id(0)
    halo = jnp.where(i % tiles_per_seq == 0, 0.0, halo_ref[...])
    x = x_ref[...].astype(BF16)
    x_ext = jnp.concatenate([halo.astype(BF16), x], axis=0)
    for part in range(3):
        cols = slice(part * EVA_PART, (part + 1) * EVA_PART)
        y = jnp.dot(x_ext, wa_ref[:, cols], preferred_element_type=F32)
        c = _silu(_causal_conv(y, cw_ref[:, cols], scratch))
        for hd in range(GDN_HEADS):
            ch = c[:, hd * GDN_D:(hd + 1) * GDN_D]
            if part < 2:
                inv = lax.rsqrt(jnp.sum(ch * ch, axis=-1, keepdims=True) + RMS_EPS)
                ch = ch * (inv * GDN_D ** -0.5 if part == 0 else inv)
            lo = part * EVA_PART + hd * GDN_D
            oa_ref[:, lo:lo + GDN_D] = ch.astype(oa_ref.dtype)
    z_cols = slice(3 * EVA_PART, 4 * EVA_PART)
    oa_ref[:, z_cols] = jnp.dot(x, wa_ref[:, z_cols], preferred_element_type=F32).astype(oa_ref.dtype)
    for c in range(EVB_COLS // EVB_TN):
        cols = slice(c * EVB_TN, (c + 1) * EVB_TN)
        ob_ref[:, cols] = jnp.dot(x, wb_ref[:, cols], preferred_element_type=F32)


def _even_in_proj(x, w_a, w_b, conv_w, seq, tm):
    t, k = x.shape
    hb = tm // HALO
    return pl.pallas_call(
        functools.partial(_even_in_body, seq // tm),
        grid=(t // tm,),
        in_specs=[pl.BlockSpec((tm, k), lambda i: (i, 0)),
                  pl.BlockSpec((HALO, k), lambda i: (jnp.maximum(i * hb - 1, 0), 0)),
                  _resident(w_a), _resident(w_b), _resident(conv_w)],
        out_specs=[pl.BlockSpec((tm, w_a.shape[1]), lambda i: (i, 0)),
                   pl.BlockSpec((tm, EVB_COLS), lambda i: (i, 0))],
        out_shape=[jax.ShapeDtypeStruct((t, w_a.shape[1]), BF16), jax.ShapeDtypeStruct((t, EVB_COLS), F32)],
        scratch_shapes=[pltpu.VMEM((HALO + tm, EVA_PART), F32)],
        compiler_params=_cparams(("parallel",)),
        name="even_in_proj",
    )(x, x, w_a, w_b, conv_w)


def _gdn_body(q_ref, k_ref, v_ref, z_ref, ba_ref, alog_ref, dtb_ref, nw_ref, o_ref, state):
    n = pl.program_id(1)

    @pl.when(n == 0)
    def _():
        state[...] = jnp.zeros_like(state)

    grp = GDN_GROUP
    nc = grp // CHUNK
    nh = GDN_HEADS
    hs = range(nh)
    ri = lax.broadcasted_iota(jnp.int32, (grp, grp), 0)
    ci = lax.broadcasted_iota(jnp.int32, (grp, grp), 1)
    same = (ri // CHUNK) == (ci // CHUNK)
    incl = same & (ri >= ci)
    strict = same & (ri > ci)
    eye = (ri == ci).astype(F32)
    tri = incl.astype(BF16)

    def chunk_local(rows):
        ba = ba_ref[rows, :]
        beta_all = _sigmoid(ba)
        g_all = -jnp.exp(alog_ref[...]) * _softplus(ba + dtb_ref[...])
        g_hi = g_all.astype(BF16)
        rem = g_all - g_hi.astype(F32)
        g_mid = rem.astype(BF16)
        g_lo = (rem - g_mid.astype(F32)).astype(BF16)
        gam_all = (jnp.dot(tri, g_hi, preferred_element_type=F32)
                   + jnp.dot(tri, g_mid, preferred_element_type=F32)
                   + jnp.dot(tri, g_lo, preferred_element_type=F32))
        gam_rows = gam_all.T

        q = [q_ref[rows, h * GDN_D:(h + 1) * GDN_D] for h in hs]
        k = [k_ref[rows, h * GDN_D:(h + 1) * GDN_D] for h in hs]
        v = [v_ref[rows, h * GDN_D:(h + 1) * GDN_D] for h in hs]
        gam = [jnp.broadcast_to(gam_all[:, nh + h:nh + h + 1], (grp, GDN_D)) for h in hs]
        bcol = [beta_all[:, h:h + 1] for h in hs]
        decay, x, inv = [], [], []
        for h in hs:
            diff = gam[h][:, 0:1] - gam_rows[nh + h:nh + h + 1, :]
            decay.append(jnp.where(incl, jnp.exp(jnp.where(incl, diff, 0.0)), 0.0))
            kk = _bdot_nt(k[h], k[h])
            x.append(-jnp.where(strict, bcol[h] * kk * decay[h], 0.0))
            inv.append(eye + x[h])
        for _ in range(5):
            for h in hs:
                x[h] = _bdot(x[h], x[h])
                inv[h] = inv[h] + _bdot(inv[h], x[h])
        u, w, qk, q_dec, k_dec, g_last = [], [], [], [], [], []
        for h in hs:
            eg = jnp.exp(gam[h])
            uw = _bdot(inv[h], jnp.concatenate([v[h] * bcol[h], k[h] * (bcol[h] * eg)], axis=1))
            u.append(uw[:, :GDN_D])
            w.append(uw[:, GDN_D:])
            qk.append(_bdot_nt(q[h], k[h]) * decay[h])
            q_dec.append(q[h] * eg)
            kd, gl = [], []
            for c in range(nc):
                last = gam[h][(c + 1) * CHUNK - 1:(c + 1) * CHUNK, :]
                kd.append(k[h][c * CHUNK:(c + 1) * CHUNK] * jnp.exp(last - gam[h][c * CHUNK:(c + 1) * CHUNK]))
                gl.append(jnp.exp(last))
            k_dec.append(kd)
            g_last.append(gl)
        return u, w, qk, q_dec, k_dec, g_last

    def recurrence(rows, local, s):
        u, w, qk, q_dec, k_dec, g_last = local
        q_s = [[] for _ in hs]
        delta = [[] for _ in hs]
        for c in range(nc):
            sl = slice(c * CHUNK, (c + 1) * CHUNK)
            for h in hs:
                r = _bdot(jnp.concatenate([w[h][sl], q_dec[h][sl]], axis=0), s[h])
                d = u[h][sl] - r[:CHUNK]
                q_s[h].append(r[CHUNK:])
                delta[h].append(d)
                s[h] = g_last[h][c] * s[h] + _bdot_tn(k_dec[h][c], d)
        for h in hs:
            o = jnp.concatenate(q_s[h], axis=0) + _bdot(qk[h], jnp.concatenate(delta[h], axis=0))
            o = o * lax.rsqrt(jnp.mean(o * o, axis=-1, keepdims=True) + RMS_EPS) * nw_ref[...]
            z = z_ref[rows, h * GDN_D:(h + 1) * GDN_D].astype(F32)
            o_ref[rows, h * GDN_D:(h + 1) * GDN_D] = (o * _silu(z)).astype(o_ref.dtype)
        return s

    groups = [slice(g * grp, (g + 1) * grp) for g in range(GDN_STEP_GROUPS)]
    local = [chunk_local(rows) for rows in groups]
    s = [state[h] for h in hs]
    for rows, loc in zip(groups, local):
        s = recurrence(rows, loc, s)
    for h in hs:
        state[h] = s[h]


def _gdn(proj_a, proj_b, alog_pad, dtb_pad, norm_w, bsz, seq):
    t = proj_a.shape[0]
    rows = GDN_GROUP * GDN_STEP_GROUPS
    spb = seq // rows
    width = GDN_HEADS * GDN_D

    def at(col):
        return lambda b, n: (b * spb + n, col)

    return pl.pallas_call(
        _gdn_body,
        grid=(bsz, spb),
        in_specs=[pl.BlockSpec((rows, width), at(0)),
                  pl.BlockSpec((rows, width), at(1)),
                  pl.BlockSpec((rows, width), at(2)),
                  pl.BlockSpec((rows, width), at(3)),
                  pl.BlockSpec((rows, LANES), at(EVB_BA // LANES)),
                  pl.BlockSpec((1, LANES), lambda b, n: (0, 0)),
                  pl.BlockSpec((1, LANES), lambda b, n: (0, 0)),
                  pl.BlockSpec((1, LANES), lambda b, n: (0, 0))],
        out_specs=pl.BlockSpec((rows, width), at(0)),
        out_shape=jax.ShapeDtypeStruct((t, width), BF16),
        scratch_shapes=[pltpu.VMEM((GDN_HEADS, GDN_D, GDN_D), F32)],
        compiler_params=_cparams(("parallel", "arbitrary")),
        name="gdn",
    )(proj_a, proj_a, proj_a, proj_a, proj_b, alog_pad, dtb_pad, norm_w)


def _swa_bias_tables():
    qi = np.arange(SWA_BLOCK)[:, None] + SWA_BLOCK
    kj = np.arange(2 * SWA_BLOCK)[None, :]
    rel = qi - kj
    buckets, negs = [], []
    for window, dilation in SWA_CONFIGS:
        buckets.append(_rel_bucket_np(np.maximum(rel, 0) * dilation))
        negs.append(np.where((rel >= 0) & (rel <= window // dilation), 0.0, NEG_BIG))
    return np.stack(buckets).astype(np.int32), np.stack(negs).astype(np.float32)


def _swa_body(*refs):
    ng = len(SWA_CONFIGS)
    ins, bias_ref, o_ref = refs[:5 * ng], refs[5 * ng], refs[5 * ng + 1]
    scratch = refs[5 * ng + 2:]
    kbufs, vbufs, o_scr, lse_scr = scratch[:ng], scratch[ng:2 * ng], scratch[2 * ng], scratch[2 * ng + 1]
    j = pl.program_id(1)
    blk = SWA_BLOCK
    first_head = lax.broadcasted_iota(jnp.int32, (blk, LANES), 1) < SWA_DH
    in_prev = lax.broadcasted_iota(jnp.int32, (2 * blk, 2 * blk), 1) < blk

    for g, (_, d) in enumerate(SWA_CONFIGS):
        q_ref, kc_ref, kp_ref, vc_ref, vp_ref = ins[5 * g:5 * g + 5]
        kbuf, vbuf = kbufs[g], vbufs[g]
        halo = blk * d
        kbuf[0:halo, :] = kp_ref[...]
        kbuf[halo:, :] = kc_ref[...]
        vbuf[0:halo, :] = vp_ref[...]
        vbuf[halo:, :] = vc_ref[...]

        def unit(u, carry, g=g, d=d, halo=halo, q_ref=q_ref, kbuf=kbuf, vbuf=vbuf):
            base = (u // d) * halo + u % d
            q = q_ref[pl.ds(base, blk, stride=d), :] * SWA_DH ** -0.5
            k = kbuf[pl.ds(base, 2 * blk, stride=d), :]
            v = vbuf[pl.ds(base, 2 * blk, stride=d), :]
            lhs = jnp.concatenate([jnp.where(first_head, q, 0.0), jnp.where(first_head, 0.0, q)], axis=0)
            s = _bdot_nt(lhs, k) + bias_ref[g, 0]
            no_prev = (j == 0) & (u < d)
            s = jnp.where(in_prev & no_prev, NEG_BIG, s)
            m = jnp.max(s, axis=-1, keepdims=True)
            p = jnp.exp(s - m)
            l = jnp.sum(p, axis=-1, keepdims=True)
            o2 = _bdot(p / l, v)
            lse2 = m + jnp.log(l)
            o_scr[g, pl.ds(base, blk, stride=d), :] = jnp.where(first_head, o2[:blk], o2[blk:])
            lse_scr[g, pl.ds(base, blk, stride=d), :] = jnp.where(first_head, lse2[:blk], lse2[blk:])
            return carry

        lax.fori_loop(0, SWA_SPAN // blk, unit, 0, unroll=SWA_UNROLL)

    def combine(c, carry):
        rows = pl.ds(pl.multiple_of(c * 2 * blk, 2 * blk), 2 * blk)
        lse = [lse_scr[g, rows, :] for g in range(ng)]
        m = functools.reduce(jnp.maximum, lse)
        e = [jnp.exp(x - m) for x in lse]
        den = functools.reduce(lambda x, y: x + y, e)
        o_ref[rows, :] = functools.reduce(
            lambda x, y: x + y, [(e[g] / den) * o_scr[g, rows, :] for g in range(ng)]).astype(o_ref.dtype)
        return carry

    lax.fori_loop(0, SWA_SPAN // (2 * blk), combine, 0)


def _swa(proj, bias, bsz, seq):
    t = proj.shape[0]
    ng = len(SWA_CONFIGS)
    nspan = seq // SWA_SPAN
    npair = SWA_HEADS * SWA_DH // LANES
    group_cols = SWA_HEADS * SWA_DH // LANES
    in_specs, scratch_k = [], []
    for g, (_, d) in enumerate(SWA_CONFIGS):
        halo = SWA_BLOCK * d
        per_span = SWA_SPAN // halo

        def cur(which, g=g):
            col = (which * ng + g) * group_cols
            return lambda b, j, p: (b * nspan + j, col + p)

        def prev(which, g=g, per_span=per_span):
            col = (which * ng + g) * group_cols
            return lambda b, j, p: (jnp.maximum((b * nspan + j) * per_span - 1, 0), col + p)

        in_specs += [pl.BlockSpec((SWA_SPAN, LANES), cur(0)),
                     pl.BlockSpec((SWA_SPAN, LANES), cur(1)), pl.BlockSpec((halo, LANES), prev(1)),
                     pl.BlockSpec((SWA_SPAN, LANES), cur(2)), pl.BlockSpec((halo, LANES), prev(2))]
        scratch_k.append(pltpu.VMEM((halo + SWA_SPAN, LANES), F32))
    in_specs.append(pl.BlockSpec((ng, 1, 2 * SWA_BLOCK, 2 * SWA_BLOCK), lambda b, j, p: (0, p, 0, 0)))
    return pl.pallas_call(
        _swa_body,
        grid=(bsz, nspan, npair),
        in_specs=in_specs,
        out_specs=pl.BlockSpec((SWA_SPAN, LANES), lambda b, j, p: (b * nspan + j, p)),
        out_shape=jax.ShapeDtypeStruct((t, npair * LANES), BF16),
        scratch_shapes=scratch_k + scratch_k + [pltpu.VMEM((ng, SWA_SPAN, LANES), F32)] * 2,
        compiler_params=_cparams(("parallel", "parallel", "parallel")),
        name="swa",
    )(*([proj] * (5 * ng)), bias)


def _layer_norm(y, g, b):
    mu = jnp.mean(y, axis=-1, keepdims=True)
    yc = y - mu
    var = jnp.mean(yc * yc, axis=-1, keepdims=True)
    return yc * lax.rsqrt(var + LN_EPS) * g + b


def _diff_bias_tables(blk):
    buckets = _rel_bucket_np(np.arange(2 * REL_MAX_DIST))
    far = int(np.max(np.nonzero(buckets != NUM_BUCKETS - 1)[0])) + 1
    nb = -(-(far + blk - 1) // blk)
    kk = np.arange(blk)[:, None]
    qq = np.arange(blk)[None, :]
    dist = np.stack([t * blk + qq - kk for t in range(nb + 1)])
    bucket = _rel_bucket_np(np.maximum(dist, 0))
    neg = np.where(dist >= 0, 0.0, NEG_BIG).astype(np.float32)
    return bucket, neg


def _diff_body(blk, nb, lam_init, q_ref, qn_ref, k_ref, v_ref, bias_ref, lam_ref, nw_ref, o_ref,
               vt, acc1, acc2, s_a, s_b):
    qi = pl.program_id(2)
    dv = 2 * DIFF_DH
    seq = k_ref.shape[0]

    @pl.when(qi == 0)
    def _():
        vt[dv:, :] = jnp.ones((vt.shape[0] - dv, seq), BF16)

        def fill(c, carry):
            st = pl.multiple_of(c * blk, blk)
            vt[0:dv, pl.ds(st, blk)] = v_ref[pl.ds(st, blk), :].astype(F32).T.astype(BF16)
            return carry

        lax.fori_loop(0, seq // blk, fill, 0)

    feature = lax.broadcasted_iota(jnp.int32, (dv, blk), 0)

    def components(ref):
        q_t = (ref[...].astype(F32) * (DIFF_DH ** -0.5 * LOG2E)).T
        return (jnp.where(feature < DIFF_DH, q_t, 0.0).astype(BF16),
                jnp.where(feature >= DIFF_DH, q_t, 0.0).astype(BF16))

    qs = components(q_ref)
    qs_next = components(qn_ref)
    accs = (acc1, acc2)
    for acc in accs:
        acc[...] = jnp.zeros_like(acc)

    last = pl.num_programs(2) - 1

    def key_rows(kj):
        return pl.ds(pl.multiple_of(jnp.minimum(kj, last) * blk, blk), blk)

    def scores(kj, dst, queries=qs):
        k = k_ref[key_rows(kj), :]
        for c, qc in enumerate(queries):
            dst[c] = jnp.dot(k, qc, preferred_element_type=F32)

    def consume(biased, kj, src, ms):
        vtb = vt[:, key_rows(kj)]
        out = []
        for c, (m, acc) in enumerate(zip(ms, accs)):
            s = src[c]
            if biased:
                s = s + bias_ref[0, jnp.minimum(qi - kj, nb)]
            m_new = jnp.maximum(m, jnp.max(s, axis=0, keepdims=True))
            p = jnp.exp2(s - m_new).astype(BF16)
            acc[...] = jnp.exp2(m - m_new) * acc[...] + jnp.dot(vtb, p, preferred_element_type=F32)
            out.append(m_new)
        return tuple(out)

    nblocks = qi + 1
    trips = nblocks // 2
    odd = nblocks % 2 == 1
    far_trips = jnp.maximum(qi - nb + 1, 0) // 2

    def pair(biased, t, ms):
        kj = 2 * t
        scores(kj + 1, s_b)
        ms = consume(biased, kj, s_a, ms)
        if biased:
            hand_off = (t == trips - 1) & jnp.logical_not(odd)
            scores(jnp.where(hand_off, 0, kj + 2), s_a,
                   tuple(jnp.where(hand_off, qn, qc) for qc, qn in zip(qs, qs_next)))
        else:
            scores(kj + 2, s_a)
        return consume(biased, kj + 1, s_b, ms)

    @pl.when(qi == 0)
    def _():
        scores(0, s_a)

    m0 = jnp.full((1, blk), NEG_BIG, F32)
    ms = lax.fori_loop(0, far_trips, functools.partial(pair, False), (m0, m0))
    ms = lax.fori_loop(far_trips, trips, functools.partial(pair, True), ms)

    @pl.when(odd)
    def _():
        consume(True, qi, s_a, ms)
        scores(0, s_a, qs_next)

    lp = lam_ref[...]
    lam = (jnp.exp(jnp.sum(lp[0:1] * lp[1:2], axis=-1, keepdims=True))
           - jnp.exp(jnp.sum(lp[2:3] * lp[3:4], axis=-1, keepdims=True)) + lam_init)
    a1, a2 = acc1[...], acc2[...]
    o_t = a1[:dv] * (1.0 / a1[dv:dv + 1]) - a2[:dv] * (lam / a2[dv:dv + 1])
    o = o_t.T
    o = o * lax.rsqrt(jnp.mean(o * o, axis=-1, keepdims=True) + RMS_EPS) * nw_ref[...]
    o_ref[...] = (o * (1.0 - lam_init)).astype(o_ref.dtype)


def _diff_attention(proj, bias, lam_params, norm_w, lam_init, bsz, seq, blk):
    t = proj.shape[0]
    nq = seq // blk
    nt = bias.shape[1]
    nb = nt - 1
    width = 2 * DIFF_DH
    ones_rows = 2 * SUBLANES
    return pl.pallas_call(
        functools.partial(_diff_body, blk, nb, lam_init),
        grid=(bsz, DIFF_HEADS, nq),
        in_specs=[pl.BlockSpec((blk, width), lambda b, h, i: (b * nq + i, OD_QC // width + h)),
                  pl.BlockSpec((blk, width),
                               lambda b, h, i: (b * nq + jnp.minimum(i + 1, nq - 1), OD_QC // width + h)),
                  pl.BlockSpec((seq, width), lambda b, h, i: (b, OD_KC // width + h)),
                  pl.BlockSpec((seq, width), lambda b, h, i: (b, OD_VC // width + h)),
                  pl.BlockSpec((1, nt, blk, blk), lambda b, h, i: (h, 0, 0, 0)),
                  pl.BlockSpec((4, DIFF_DH), lambda b, h, i: (0, 0)),
                  pl.BlockSpec((1, width), lambda b, h, i: (0, 0))],
        out_specs=pl.BlockSpec((blk, width), lambda b, h, i: (b * nq + i, h)),
        out_shape=jax.ShapeDtypeStruct((t, DIFF_HEADS * width), BF16),
        scratch_shapes=[pltpu.VMEM((width + ones_rows, seq), BF16),
                        pltpu.VMEM((width + ones_rows, blk), F32), pltpu.VMEM((width + ones_rows, blk), F32),
                        pltpu.VMEM((2, blk, blk), F32), pltpu.VMEM((2, blk, blk), F32)],
        compiler_params=_cparams(("arbitrary", "arbitrary", "arbitrary")),
        name="diff_attn",
    )(proj, proj, proj, proj, bias, lam_params, norm_w)


def _gla_body(q_ref, k_ref, gd_ref, v_ref, r_ref, wg_ref, bg_ref, nw_ref, o_ref, state, part):
    n = pl.program_id(1)

    @pl.when(n == 0)
    def _():
        state[...] = jnp.zeros_like(state)

    rows = GLA_GROUP
    nc = rows // CHUNK
    npair = GLA_HEADS // 2
    gate = jnp.dot(gd_ref[...], wg_ref[...], precision=HI, preferred_element_type=F32) + bg_ref[...]
    log_a = _log_sigmoid(gate) * (1.0 / GLA_TAU)

    ri = lax.broadcasted_iota(jnp.int32, (rows, rows), 0)
    ci = lax.broadcasted_iota(jnp.int32, (rows, rows), 1)
    causal = ((ri // CHUNK) == (ci // CHUNK)) & (ri >= ci)
    tri = causal.astype(BF16)
    hi = log_a.astype(BF16)
    rem = log_a - hi.astype(F32)
    mid = rem.astype(BF16)
    lo = (rem - mid.astype(F32)).astype(BF16)
    b_all = (jnp.dot(tri, hi, preferred_element_type=F32) + jnp.dot(tri, mid, preferred_element_type=F32)
             + jnp.dot(tri, lo, preferred_element_type=F32))

    lane = lax.broadcasted_iota(jnp.int32, (rows, LANES), 1)
    lane_c = lax.broadcasted_iota(jnp.int32, (CHUNK, LANES), 1)
    head_lanes = (lane < GLA_DK, lane >= GLA_DK)
    chunk_lanes = (lane_c < GLA_DK, lane_c >= GLA_DK)
    pairs = []
    for p in range(npair):
        cols = slice(p * LANES, (p + 1) * LANES)
        pairs.append(dict(
            b=b_all[:, cols],
            q=q_ref[:, cols].astype(F32) * GLA_DK ** -0.5,
            k=k_ref[:, cols].astype(F32),
            v=[v_ref[:, (2 * p + hd) * GLA_DV:(2 * p + hd + 1) * GLA_DV].astype(F32) for hd in range(2)]))

    for pr in pairs:
        pr["q_dec"] = pr["q"] * jnp.exp(pr["b"])
    for c in range(nc):
        sl = slice(c * CHUNK, (c + 1) * CHUNK)
        for p, pr in enumerate(pairs):
            bc = pr["b"][sl]
            b_last = bc[CHUNK - 1:CHUNK]
            k_dec = pr["k"][sl] * jnp.exp(b_last - bc)
            e_last = jnp.exp(b_last)
            for hd in range(2):
                h = 2 * p + hd
                st = state[h]
                part[h, sl, :] = _bdot_nt(jnp.where(chunk_lanes[hd], pr["q_dec"][sl], 0.0), st)
                state[h] = st * e_last + _bdot_tn(pr["v"][hd][sl], k_dec)

    def finish(intra):
        for h in range(GLA_HEADS):
            o = part[h] + intra[h]
            o = o * lax.rsqrt(jnp.mean(o * o, axis=-1, keepdims=True) + RMS_EPS) * nw_ref[...]
            gate_r = _silu(r_ref[:, h * GLA_DV:(h + 1) * GLA_DV].astype(F32))
            o_ref[:, h * GLA_DV:(h + 1) * GLA_DV] = (o * gate_r).astype(o_ref.dtype)

    def intra_whole_chunk():
        out = []
        for pr in pairs:
            k_inv = pr["k"] * jnp.exp(jnp.minimum(-pr["b"], GLA_MAX_DECAY))
            for hd in range(2):
                a = _bdot_nt(jnp.where(head_lanes[hd], pr["q_dec"], 0.0), k_inv)
                out.append(_bdot(jnp.where(causal, a, 0.0), pr["v"][hd]))
        return out

    def intra_exact():
        out = []
        for pr in pairs:
            off = _gla_intra_off_diagonal(pr["q"], pr["k"], pr["b"], pr["v"], nc)
            diag = _gla_intra_diagonal(pr["q"], pr["k"], pr["b"], pr["v"])
            out += [off[hd] + diag[hd] for hd in range(2)]
        return out

    chunk_decay = jnp.max(-b_all.reshape(nc, CHUNK, npair * LANES)[:, CHUNK - 1:CHUNK, :])

    @pl.when(chunk_decay <= GLA_MAX_DECAY)
    def _():
        finish(intra_whole_chunk())

    @pl.when(chunk_decay > GLA_MAX_DECAY)
    def _():
        finish(intra_exact())


def _gla_intra_off_diagonal(q, k, b, vs, nc):
    per_chunk = CHUNK // GLA_SUB
    lane = lax.broadcasted_iota(jnp.int32, (GLA_SUB, LANES), 1)
    sub_mask = (lane < GLA_DK, lane >= GLA_DK)
    kcol = lax.broadcasted_iota(jnp.int32, (GLA_SUB, CHUNK), 1)
    outs = ([], [])
    for c in range(nc):
        sl = slice(c * CHUNK, (c + 1) * CHUNK)
        bc, qc, kc = b[sl], q[sl], k[sl]
        a_rows = [[jnp.zeros((GLA_SUB, CHUNK), F32)] for _ in vs]
        for blk in range(1, per_chunk):
            r0 = blk * GLA_SUB
            bref = bc[r0:r0 + 1]
            qs = qc[r0:r0 + GLA_SUB] * jnp.exp(bc[r0:r0 + GLA_SUB] - bref)
            ks = kc * jnp.exp(jnp.minimum(bref - bc, 0.0))
            for hd in range(2):
                a = _bdot_nt(jnp.where(sub_mask[hd], qs, 0.0), ks)
                a_rows[hd].append(jnp.where(kcol < r0, a, 0.0))
        for hd in range(2):
            outs[hd].append(_bdot(jnp.concatenate(a_rows[hd], axis=0), vs[hd][sl]))
    return [jnp.concatenate(o, axis=0) for o in outs]


def _gla_intra_diagonal(q, k, b, vs):
    rows = q.shape[0]
    nsub = rows // GLA_SUB
    b3 = b.reshape(nsub, GLA_SUB, LANES)
    q3 = q.reshape(nsub, GLA_SUB, LANES)
    k3 = k.reshape(nsub, GLA_SUB, LANES)
    v3 = [v.reshape(nsub, GLA_SUB, LANES) for v in vs]
    row3 = lax.broadcasted_iota(jnp.int32, (nsub, GLA_SUB, LANES), 1)
    lane3 = lax.broadcasted_iota(jnp.int32, (nsub, GLA_SUB, LANES), 2)
    rowc = lax.broadcasted_iota(jnp.int32, (nsub, GLA_SUB, 1), 1)
    o3 = [jnp.zeros((nsub, GLA_SUB, LANES), F32) for _ in vs]
    for jj in range(GLA_SUB):
        e = jnp.exp(jnp.where(row3 >= jj, b3 - b3[:, jj:jj + 1, :], 0.0))
        t = q3 * k3[:, jj:jj + 1, :] * e
        w_all = jnp.sum(t, axis=-1, keepdims=True)
        w_a = jnp.sum(jnp.where(lane3 < GLA_DK, t, 0.0), axis=-1, keepdims=True)
        for hd, w in enumerate((w_a, w_all - w_a)):
            o3[hd] = o3[hd] + jnp.where(rowc >= jj, w, 0.0) * v3[hd][:, jj:jj + 1, :]
    return [o.reshape(rows, LANES) for o in o3]


def _gla(proj, gate_in, w_gate_pad, b_gate, norm_w, bsz, seq):
    t = proj.shape[0]
    rows = GLA_GROUP
    spb = seq // rows
    qk_w = GLA_HEADS * GLA_DK
    v_w = GLA_HEADS * GLA_DV

    def at(col):
        return lambda b, n: (b * spb + n, col)

    def whole(a):
        return pl.BlockSpec(a.shape, lambda b, n: (0, 0))

    return pl.pallas_call(
        _gla_body,
        grid=(bsz, spb),
        in_specs=[pl.BlockSpec((rows, qk_w), at(OD_QD // qk_w)),
                  pl.BlockSpec((rows, qk_w), at(OD_KD // qk_w)),
                  pl.BlockSpec((rows, LANES), at(0)),
                  pl.BlockSpec((rows, v_w), at(OD_VD // v_w)),
                  pl.BlockSpec((rows, v_w), at(OD_RD // v_w)),
                  whole(w_gate_pad), whole(b_gate), whole(norm_w)],
        out_specs=pl.BlockSpec((rows, v_w), at(0)),
        out_shape=jax.ShapeDtypeStruct((t, v_w), BF16),
        scratch_shapes=[pltpu.VMEM((GLA_HEADS, GLA_DV, LANES), F32), pltpu.VMEM((GLA_HEADS, rows, GLA_DV), F32)],
        compiler_params=_cparams(("parallel", "arbitrary")),
        name="gla",
    )(proj, proj, gate_in, proj, proj, w_gate_pad, b_gate, norm_w)


def _tail_body(tiles_per_seq, tc, ma_ref, ma_halo, mb_ref, mb_halo, h_ref, h_halo, woa_ref, wob_ref,
               g1_ref, b1_ref, wu_ref, cw_ref, cb_ref, wd_ref, g2_ref, b2_ref, y_ref,
               act, scratch_g, scratch_v):
    i = pl.program_id(0)

    def ext(halo_ref, ref):
        return jnp.concatenate([halo_ref[...], ref[...]], axis=0)

    mix = (jnp.dot(ext(ma_halo, ma_ref), woa_ref[...], preferred_element_type=F32)
           + jnp.dot(ext(mb_halo, mb_ref), wob_ref[...], preferred_element_type=F32))
    x_ext = _layer_norm(DEEPNORM_ALPHA * ext(h_halo, h_ref) + mix, g1_ref[...], b1_ref[...])
    x = x_ext[HALO:]
    row = lax.broadcasted_iota(jnp.int32, x_ext.shape, 0)
    first = i % tiles_per_seq == 0
    xb = jnp.where((row < HALO) & first, 0.0, x_ext).astype(BF16)

    def branch(lo, scratch):
        full = jnp.dot(xb, wu_ref[:, lo:lo + tc], preferred_element_type=F32)
        return _causal_conv(full, cw_ref[:, lo:lo + tc], scratch) + cb_ref[:, lo:lo + tc]

    for c in range(D_FF // tc):
        gate = branch(c * tc, scratch_g)
        val = branch(D_FF + c * tc, scratch_v)
        act[:, c * tc:(c + 1) * tc] = (_silu(gate) * val).astype(BF16)
    ffn = jnp.dot(act[...], wd_ref[...], preferred_element_type=F32)
    y_ref[...] = _layer_norm(DEEPNORM_ALPHA * x + ffn, g2_ref[...], b2_ref[...])


def _layer_tail(mix_a, mix_b, h, w_out_a, w_out_b, g1, b1, w_up, conv_w, conv_b, w_down, g2, b2,
                layer, seq):
    t = h.shape[0]
    tm, tc = TAIL_TM, FFN_TC
    hb = tm // HALO

    def tile(a):
        return pl.BlockSpec((tm, a.shape[1]), lambda i: (i, 0))

    def halo(a):
        return pl.BlockSpec((HALO, a.shape[1]), lambda i: (jnp.maximum(i * hb - 1, 0), 0))

    params = (w_out_a, w_out_b, g1, b1, w_up, conv_w, conv_b, w_down, g2, b2)
    stacked = (w_up, w_down)
    return pl.pallas_call(
        functools.partial(_tail_body, seq // tm, tc),
        grid=(t // tm,),
        in_specs=[tile(mix_a), halo(mix_a), tile(mix_b), halo(mix_b), tile(h), halo(h)]
        + [_resident(p, layer if any(p is s for s in stacked) else None) for p in params],
        out_specs=pl.BlockSpec((tm, D_MODEL), lambda i: (i, 0)),
        out_shape=jax.ShapeDtypeStruct((t, D_MODEL), F32),
        scratch_shapes=[pltpu.VMEM((tm, D_FF), BF16), pltpu.VMEM((HALO + tm, tc), F32),
                        pltpu.VMEM((HALO + tm, tc), F32)],
        compiler_params=_cparams(("parallel",)),
        name="layer_tail",
    )(mix_a, mix_a, mix_b, mix_b, h, h, *params)


def _even_w_in(w):
    a_end = 4 * EVA_PART
    gates = w[:, a_end:a_end + 2 * GDN_HEADS]
    qkv_b = w[:, a_end + 2 * GDN_HEADS:]
    pad = jnp.zeros((w.shape[0], EVB_COLS - EVB_BA - 2 * GDN_HEADS), w.dtype)
    return w[:, :a_end].astype(BF16), jnp.concatenate([qkv_b, gates, pad], axis=1).astype(BF16)


def _odd_w_in(w):
    pad = jnp.zeros((w.shape[0], LANES - GLA_RANK), w.dtype)
    return w[:, :OD_COLS].astype(BF16), jnp.concatenate([w[:, OD_COLS:], pad], axis=1).astype(BF16)


def _even_mixer(h, rel_bias, w_in, conv_w, a_log, dt_bias, norm_w, w_out, bsz, seq):
    w_a, w_b = _even_w_in(w_in)
    proj_a, proj_b = _even_in_proj(h, w_a, w_b, conv_w, seq, PROJ_TM)
    gate_pad = jnp.zeros((1, LANES), F32)
    alog_pad = lax.dynamic_update_slice(gate_pad, a_log[None].astype(F32), (0, GDN_HEADS))
    dtb_pad = lax.dynamic_update_slice(gate_pad, dt_bias[None].astype(F32), (0, GDN_HEADS))
    o_a = _gdn(proj_a, proj_b, alog_pad, dtb_pad, norm_w[None], bsz, seq)
    tiles = _bias_tiles(rel_bias, *_swa_bias_tables())
    ng, two_blk = len(SWA_CONFIGS), 2 * SWA_BLOCK
    bias = tiles.reshape(SWA_HEADS // 2, 2, ng, SWA_BLOCK, two_blk).transpose(2, 0, 1, 3, 4)
    o_b = _swa(proj_b, bias.reshape(ng, SWA_HEADS // 2, two_blk, two_blk), bsz, seq)
    w_out = w_out.astype(BF16)
    return o_a, o_b, w_out[:EVA_PART], w_out[EVA_PART:]


def _odd_mixer(h, rel_bias, w_in, lam_params, diff_norm_w, w_gate, b_gate, gla_norm_w, w_out,
               lam_init, bsz, seq):
    w_main, w_gd = _odd_w_in(w_in)
    proj, gate_in = _odd_in_proj(h, w_main, w_gd, PROJ_TM)
    blk = min(DIFF_BLOCK, seq)
    bucket, neg = _diff_bias_tables(blk)
    bias = _bias_tiles(rel_bias, bucket, neg, scale=LOG2E, base_bucket=NUM_BUCKETS - 1)
    o_c = _diff_attention(proj, bias, lam_params, diff_norm_w[None], lam_init, bsz, seq, blk)
    w_gate_pad = jnp.concatenate(
        [w_gate, jnp.zeros((LANES - GLA_RANK, w_gate.shape[1]), w_gate.dtype)], axis=0)
    o_d = _gla(proj, gate_in, w_gate_pad, b_gate[None], gla_norm_w[None], bsz, seq)
    diff_v = DIFF_HEADS * 2 * DIFF_DH
    w_out = w_out.astype(BF16)
    return o_c, o_d, w_out[:diff_v], w_out[diff_v:]


def kernel(x, rel_bias, w_in_even, gdn_conv_w, gdn_a_log, gdn_dt_bias, gdn_norm_w, w_out_even,
           w_in_odd, diff_lambda, diff_norm_w, gla_w_gate, gla_b_gate, gla_norm_w, w_out_odd,
           ffn_w_up, ffn_conv_w, ffn_conv_b, ffn_w_down, ln_g, ln_b):
    bsz, seq, d = x.shape
    h = x.reshape(bsz * seq, d)
    w_up, w_down = ffn_w_up.astype(BF16), ffn_w_down.astype(BF16)
    for layer in range(DEPTH):
        i = layer // 2
        if layer % 2 == 0:
            mixed = _even_mixer(h, rel_bias, w_in_even[i], gdn_conv_w[i], gdn_a_log[i], gdn_dt_bias[i],
                                gdn_norm_w[i], w_out_even[i], bsz, seq)
        else:
            lam_init = 0.8 - 0.6 * math.exp(-0.3 * layer)
            mixed = _odd_mixer(h, rel_bias, w_in_odd[i], diff_lambda[i], diff_norm_w[i], gla_w_gate[i],
                               gla_b_gate[i], gla_norm_w[i], w_out_odd[i], lam_init, bsz, seq)
        h = _layer_tail(*mixed[:2], h, *mixed[2:], ln_g[layer, 0][None], ln_b[layer, 0][None],
                        w_up, ffn_conv_w[layer], ffn_conv_b[layer][None], w_down,
                        ln_g[layer, 1][None], ln_b[layer, 1][None], layer, seq)
    return h.reshape(bsz, seq, d)
```

```python
import functools
import math

import numpy as np
import jax
import jax.numpy as jnp
from jax import lax
from jax.experimental import pallas as pl
from jax.experimental.pallas import tpu as pltpu

F32 = jnp.float32
BF16 = jnp.bfloat16
HI = lax.Precision.HIGHEST

D_MODEL = 1024
DEPTH = 2
DEEPNORM_ALPHA = (2 * DEPTH) ** 0.25
LN_EPS = 1e-5
RMS_EPS = 1e-6
NUM_BUCKETS = 32
REL_MAX_DIST = 2048
GDN_HEADS = 6
GDN_D = 128
CHUNK = 64
GDN_GROUP = 256
GDN_STEP_GROUPS = 1
SWA_CONFIGS = ((128, 1), (512, 4), (2048, 16))
SWA_HEADS = 4
SWA_DH = 64
SWA_BLOCK = 128
SWA_UNROLL = 8
SWA_SPAN = SWA_BLOCK * max(d for _, d in SWA_CONFIGS)
DIFF_HEADS = 4
DIFF_DH = 64
DIFF_BLOCK = 512
LOG2E = math.log2(math.e)
GLA_HEADS = 4
GLA_DK = 64
GLA_DV = 128
GLA_RANK = 16
GLA_TAU = 16.0
GLA_SUB = 16
GLA_GROUP = 256
GLA_MAX_DECAY = 60.0
D_FF = 2816

LANES = 128
SUBLANES = 8
HALO = 16
VMEM_LIMIT = 56 * 1024 * 1024
NEG_BIG = -1e30

EVA_PART = GDN_HEADS * GDN_D
EVB_BA = 2304
EVB_COLS = 2560
EVB_TN = 512
OD_QC, OD_KC, OD_VC = 0, 512, 1024
OD_QD, OD_KD, OD_VD, OD_RD = 1536, 1792, 2048, 2560
OD_COLS = 3072
OD_TN = 768
PROJ_TM = 512
TAIL_TM = 512
FFN_TC = 256


def _cparams(sem):
    return pltpu.CompilerParams(dimension_semantics=sem, vmem_limit_bytes=VMEM_LIMIT)


def _bdot(a, b):
    return jnp.dot(a.astype(BF16), b.astype(BF16), preferred_element_type=F32)


def _bdot_nt(a, b):
    return lax.dot_general(a.astype(BF16), b.astype(BF16), (((1,), (1,)), ((), ())),
                           preferred_element_type=F32)


def _bdot_tn(a, b):
    return lax.dot_general(a.astype(BF16), b.astype(BF16), (((0,), (0,)), ((), ())),
                           preferred_element_type=F32)


def _sigmoid(x):
    return 1.0 / (1.0 + jnp.exp(-x))


def _silu(x):
    return x * _sigmoid(x)


def _softplus(x):
    return jnp.maximum(x, 0.0) + jnp.log1p(jnp.exp(-jnp.abs(x)))


def _log_sigmoid(x):
    return -_softplus(-x)


def _resident(a, lead=None):
    if lead is None:
        return pl.BlockSpec(a.shape, lambda i: (0,) * a.ndim, pipeline_mode=pl.Buffered(1))
    return pl.BlockSpec((None,) + a.shape[1:], lambda i: (lead,) + (0,) * (a.ndim - 1),
                        pipeline_mode=pl.Buffered(1))


def _odd_in_body(x_ref, w_ref, wg_ref, o_ref, og_ref):
    xb = x_ref[...].astype(BF16)
    for c in range(OD_COLS // OD_TN):
        cols = slice(c * OD_TN, (c + 1) * OD_TN)
        o_ref[:, cols] = jnp.dot(xb, w_ref[:, cols], preferred_element_type=F32).astype(o_ref.dtype)
    og_ref[...] = jnp.dot(xb, wg_ref[...], preferred_element_type=F32)


def _odd_in_proj(x, w, w_gate_in, tm):
    t, k = x.shape
    return pl.pallas_call(
        _odd_in_body,
        grid=(t // tm,),
        in_specs=[pl.BlockSpec((tm, k), lambda i: (i, 0)), _resident(w), _resident(w_gate_in)],
        out_specs=[pl.BlockSpec((tm, OD_COLS), lambda i: (i, 0)), pl.BlockSpec((tm, LANES), lambda i: (i, 0))],
        out_shape=[jax.ShapeDtypeStruct((t, OD_COLS), BF16), jax.ShapeDtypeStruct((t, LANES), F32)],
        compiler_params=_cparams(("parallel",)),
        name="odd_in_proj",
    )(x, w, w_gate_in)


def _rel_bucket_np(dist):
    max_exact = NUM_BUCKETS // 2
    d = np.maximum(dist, 1).astype(np.float32)
    large = max_exact + (np.log(d / max_exact) / math.log(REL_MAX_DIST / max_exact)
                         * (NUM_BUCKETS - max_exact)).astype(np.int32)
    large = np.minimum(large, NUM_BUCKETS - 1)
    return np.where(dist < max_exact, dist, large).astype(np.int32)


def _bias_body(scale, base_bucket, tile_buckets, rb_ref, bucket_ref, neg_ref, o_ref):
    h = pl.program_id(0)
    t = pl.program_id(1)
    base = 0.0 if base_bucket is None else rb_ref[base_bucket, h]
    for tile, present in enumerate(tile_buckets):
        @pl.when(t == tile)
        def _(present=present):
            bucket = bucket_ref[0]
            acc = neg_ref[0]
            for b in present:
                acc = acc + jnp.where(bucket == b, (rb_ref[b, h] - base) * scale, 0.0)
            o_ref[0, 0] = acc


def _bias_tiles(rel_bias, bucket, neg, scale=1.0, base_bucket=None):
    nt, r, c = bucket.shape
    nh = rel_bias.shape[1]
    tile_buckets = tuple(tuple(int(b) for b in np.unique(bucket[t][neg[t] == 0]) if b != base_bucket)
                         for t in range(nt))
    return pl.pallas_call(
        functools.partial(_bias_body, scale, base_bucket, tile_buckets),
        grid=(nh, nt),
        in_specs=[pl.BlockSpec(memory_space=pltpu.SMEM),
                  pl.BlockSpec((1, r, c), lambda h, t: (t, 0, 0)),
                  pl.BlockSpec((1, r, c), lambda h, t: (t, 0, 0))],
        out_specs=pl.BlockSpec((1, 1, r, c), lambda h, t: (h, t, 0, 0)),
        out_shape=jax.ShapeDtypeStruct((nh, nt, r, c), F32),
        compiler_params=_cparams(("parallel", "parallel")),
        name="rel_bias_tiles",
    )(rel_bias, jnp.asarray(bucket), jnp.asarray(neg))


def _causal_conv(full, w, scratch):
    width = w.shape[0]
    rows = full.shape[0] - HALO
    scratch[...] = full
    y = w[width - 1:width, :] * full[HALO:]
    for j in range(width - 1):
        back = width - 1 - j
        y = y + w[j:j + 1, :] * scratch[HALO - back:HALO - back + rows, :]
    return y


def _even_in_body(tiles_per_seq, x_ref, halo_ref, wa_ref, wb_ref, cw_ref, oa_ref, ob_ref, scratch):
    i = pl.program_id(0)
    halo = jnp.where(i % tiles_per_seq == 0, 0.0, halo_ref[...])
    x = x_ref[...].astype(BF16)
    x_ext = jnp.concatenate([halo.astype(BF16), x], axis=0)
    for part in range(3):
        cols = slice(part * EVA_PART, (part + 1) * EVA_PART)
        y = jnp.dot(x_ext, wa_ref[:, cols], preferred_element_type=F32)
        c = _silu(_causal_conv(y, cw_ref[:, cols], scratch))
        for hd in range(GDN_HEADS):
            ch = c[:, hd * GDN_D:(hd + 1) * GDN_D]
            if part < 2:
                inv = lax.rsqrt(jnp.sum(ch * ch, axis=-1, keepdims=True) + RMS_EPS)
                ch = ch * (inv * GDN_D ** -0.5 if part == 0 else inv)
            lo = part * EVA_PART + hd * GDN_D
            oa_ref[:, lo:lo + GDN_D] = ch.astype(oa_ref.dtype)
    z_cols = slice(3 * EVA_PART, 4 * EVA_PART)
    oa_ref[:, z_cols] = jnp.dot(x, wa_ref[:, z_cols], preferred_element_type=F32).astype(oa_ref.dtype)
    for c in range(EVB_COLS // EVB_TN):
        cols = slice(c * EVB_TN, (c + 1) * EVB_TN)
        ob_ref[:, cols] = jnp.dot(x, wb_ref[:, cols], preferred_element_type=F32)


def _even_in_proj(x, w_a, w_b, conv_w, seq, tm):
    t, k = x.shape
    hb = tm // HALO
    return pl.pallas_call(
        functools.partial(_even_in_body, seq // tm),
        grid=(t // tm,),
        in_specs=[pl.BlockSpec((tm, k), lambda i: (i, 0)),
                  pl.BlockSpec((HALO, k), lambda i: (jnp.maximum(i * hb - 1, 0), 0)),
                  _resident(w_a), _resident(w_b), _resident(conv_w)],
        out_specs=[pl.BlockSpec((tm, w_a.shape[1]), lambda i: (i, 0)),
                   pl.BlockSpec((tm, EVB_COLS), lambda i: (i, 0))],
        out_shape=[jax.ShapeDtypeStruct((t, w_a.shape[1]), BF16), jax.ShapeDtypeStruct((t, EVB_COLS), F32)],
        scratch_shapes=[pltpu.VMEM((HALO + tm, EVA_PART), F32)],
        compiler_params=_cparams(("parallel",)),
        name="even_in_proj",
    )(x, x, w_a, w_b, conv_w)


def _gdn_body(q_ref, k_ref, v_ref, z_ref, ba_ref, alog_ref, dtb_ref, nw_ref, o_ref, state):
    n = pl.program_id(1)

    @pl.when(n == 0)
    def _():
        state[...] = jnp.zeros_like(state)

    grp = GDN_GROUP
    nc = grp // CHUNK
    nh = GDN_HEADS
    hs = range(nh)
    ri = lax.broadcasted_iota(jnp.int32, (grp, grp), 0)
    ci = lax.broadcasted_iota(jnp.int32, (grp, grp), 1)
    same = (ri // CHUNK) == (ci // CHUNK)
    incl = same & (ri >= ci)
    strict = same & (ri > ci)
    eye = (ri == ci).astype(F32)
    tri = incl.astype(BF16)

    def chunk_local(rows):
        ba = ba_ref[rows, :]
        beta_all = _sigmoid(ba)
        g_all = -jnp.exp(alog_ref[...]) * _softplus(ba + dtb_ref[...])
        g_hi = g_all.astype(BF16)
        rem = g_all - g_hi.astype(F32)
        g_mid = rem.astype(BF16)
        g_lo = (rem - g_mid.astype(F32)).astype(BF16)
        gam_all = (jnp.dot(tri, g_hi, preferred_element_type=F32)
                   + jnp.dot(tri, g_mid, preferred_element_type=F32)
                   + jnp.dot(tri, g_lo, preferred_element_type=F32))
        gam_rows = gam_all.T

        q = [q_ref[rows, h * GDN_D:(h + 1) * GDN_D] for h in hs]
        k = [k_ref[rows, h * GDN_D:(h + 1) * GDN_D] for h in hs]
        v = [v_ref[rows, h * GDN_D:(h + 1) * GDN_D] for h in hs]
        gam = [jnp.broadcast_to(gam_all[:, nh + h:nh + h + 1], (grp, GDN_D)) for h in hs]
        bcol = [beta_all[:, h:h + 1] for h in hs]
        decay, x, inv = [], [], []
        for h in hs:
            diff = gam[h][:, 0:1] - gam_rows[nh + h:nh + h + 1, :]
            decay.append(jnp.where(incl, jnp.exp(jnp.where(incl, diff, 0.0)), 0.0))
            kk = _bdot_nt(k[h], k[h])
            x.append(-jnp.where(strict, bcol[h] * kk * decay[h], 0.0))
            inv.append(eye + x[h])
        for _ in range(5):
            for h in hs:
                x[h] = _bdot(x[h], x[h])
                inv[h] = inv[h] + _bdot(inv[h], x[h])
        u, w, qk, q_dec, k_dec, g_last = [], [], [], [], [], []
        for h in hs:
            eg = jnp.exp(gam[h])
            uw = _bdot(inv[h], jnp.concatenate([v[h] * bcol[h], k[h] * (bcol[h] * eg)], axis=1))
            u.append(uw[:, :GDN_D])
            w.append(uw[:, GDN_D:])
            qk.append(_bdot_nt(q[h], k[h]) * decay[h])
            q_dec.append(q[h] * eg)
            kd, gl = [], []
            for c in range(nc):
                last = gam[h][(c + 1) * CHUNK - 1:(c + 1) * CHUNK, :]
                kd.append(k[h][c * CHUNK:(c + 1) * CHUNK] * jnp.exp(last - gam[h][c * CHUNK:(c + 1) * CHUNK]))
                gl.append(jnp.exp(last))
            k_dec.append(kd)
            g_last.append(gl)
        return u, w, qk, q_dec, k_dec, g_last

    def recurrence(rows, local, s):
        u, w, qk, q_dec, k_dec, g_last = local
        q_s = [[] for _ in hs]
        delta = [[] for _ in hs]
        for c in range(nc):
            sl = slice(c * CHUNK, (c + 1) * CHUNK)
            for h in hs:
                r = _bdot(jnp.concatenate([w[h][sl], q_dec[h][sl]], axis=0), s[h])
                d = u[h][sl] - r[:CHUNK]
                q_s[h].append(r[CHUNK:])
                delta[h].append(d)
                s[h] = g_last[h][c] * s[h] + _bdot_tn(k_dec[h][c], d)
        for h in hs:
            o = jnp.concatenate(q_s[h], axis=0) + _bdot(qk[h], jnp.concatenate(delta[h], axis=0))
            o = o * lax.rsqrt(jnp.mean(o * o, axis=-1, keepdims=True) + RMS_EPS) * nw_ref[...]
            z = z_ref[rows, h * GDN_D:(h + 1) * GDN_D].astype(F32)
            o_ref[rows, h * GDN_D:(h + 1) * GDN_D] = (o * _silu(z)).astype(o_ref.dtype)
        return s

    groups = [slice(g * grp, (g + 1) * grp) for g in range(GDN_STEP_GROUPS)]
    local = [chunk_local(rows) for rows in groups]
    s = [state[h] for h in hs]
    for rows, loc in zip(groups, local):
        s = recurrence(rows, loc, s)
    for h in hs:
        state[h] = s[h]


def _gdn(proj_a, proj_b, alog_pad, dtb_pad, norm_w, bsz, seq):
    t = proj_a.shape[0]
    rows = GDN_GROUP * GDN_STEP_GROUPS
    spb = seq // rows
    width = GDN_HEADS * GDN_D

    def at(col):
        return lambda b, n: (b * spb + n, col)

    return pl.pallas_call(
        _gdn_body,
        grid=(bsz, spb),
        in_specs=[pl.BlockSpec((rows, width), at(0)),
                  pl.BlockSpec((rows, width), at(1)),
                  pl.BlockSpec((rows, width), at(2)),
                  pl.BlockSpec((rows, width), at(3)),
                  pl.BlockSpec((rows, LANES), at(EVB_BA // LANES)),
                  pl.BlockSpec((1, LANES), lambda b, n: (0, 0)),
                  pl.BlockSpec((1, LANES), lambda b, n: (0, 0)),
                  pl.BlockSpec((1, LANES), lambda b, n: (0, 0))],
        out_specs=pl.BlockSpec((rows, width), at(0)),
        out_shape=jax.ShapeDtypeStruct((t, width), BF16),
        scratch_shapes=[pltpu.VMEM((GDN_HEADS, GDN_D, GDN_D), F32)],
        compiler_params=_cparams(("parallel", "arbitrary")),
        name="gdn",
    )(proj_a, proj_a, proj_a, proj_a, proj_b, alog_pad, dtb_pad, norm_w)


def _swa_bias_tables():
    qi = np.arange(SWA_BLOCK)[:, None] + SWA_BLOCK
    kj = np.arange(2 * SWA_BLOCK)[None, :]
    rel = qi - kj
    buckets, negs = [], []
    for has_prev in (True, False):
        for window, dilation in SWA_CONFIGS:
            valid = (rel >= 0) & (rel <= window // dilation) & (has_prev | (kj >= SWA_BLOCK))
            buckets.append(_rel_bucket_np(np.maximum(rel, 0) * dilation))
            negs.append(np.where(valid, 0.0, NEG_BIG))
    return np.stack(buckets).astype(np.int32), np.stack(negs).astype(np.float32)


def _swa_body(*refs):
    ng = len(SWA_CONFIGS)
    ins, bias_ref, o_ref = refs[:5 * ng], refs[5 * ng], refs[5 * ng + 1]
    scratch = refs[5 * ng + 2:]
    kbufs, vbufs, o_scr, lse_scr = scratch[:ng], scratch[ng:2 * ng], scratch[2 * ng], scratch[2 * ng + 1]
    j = pl.program_id(1)
    blk = SWA_BLOCK
    first_head = lax.broadcasted_iota(jnp.int32, (blk, LANES), 1) < SWA_DH

    for g, (_, d) in enumerate(SWA_CONFIGS):
        q_ref, kc_ref, kp_ref, vc_ref, vp_ref = ins[5 * g:5 * g + 5]
        kbuf, vbuf = kbufs[g], vbufs[g]
        halo = blk * d
        kbuf[0:halo, :] = kp_ref[...]
        kbuf[halo:, :] = kc_ref[...]
        vbuf[0:halo, :] = vp_ref[...]
        vbuf[halo:, :] = vc_ref[...]

        def unit(u, carry, g=g, d=d, halo=halo, q_ref=q_ref, kbuf=kbuf, vbuf=vbuf):
            base = (u // d) * halo + u % d
            q = q_ref[pl.ds(base, blk, stride=d), :] * (SWA_DH ** -0.5 * LOG2E)
            k = kbuf[pl.ds(base, 2 * blk, stride=d), :]
            v = vbuf[pl.ds(base, 2 * blk, stride=d), :]
            lhs = jnp.concatenate([jnp.where(first_head, q, 0.0), jnp.where(first_head, 0.0, q)], axis=0)
            no_prev = jnp.where((j == 0) & (u < d), 1, 0)
            s = _bdot_nt(lhs, k) + bias_ref[g, 0, no_prev]
            m = jnp.max(s, axis=-1, keepdims=True)
            p = jnp.exp2(s - m)
            l = jnp.sum(p, axis=-1, keepdims=True)
            o2 = _bdot(p, v) * (1.0 / l)
            lse2 = m + jnp.log2(l)
            o_scr[g, pl.ds(base, blk, stride=d), :] = jnp.where(first_head, o2[:blk], o2[blk:])
            lse_scr[g, pl.ds(base, blk, stride=d), :] = jnp.where(first_head, lse2[:blk], lse2[blk:])
            return carry

        lax.fori_loop(0, SWA_SPAN // blk, unit, 0, unroll=SWA_UNROLL)

    def combine(c, carry):
        rows = pl.ds(pl.multiple_of(c * 2 * blk, 2 * blk), 2 * blk)
        lse = [lse_scr[g, rows, :] for g in range(ng)]
        m = functools.reduce(jnp.maximum, lse)
        e = [jnp.exp2(x - m) for x in lse]
        den = functools.reduce(lambda x, y: x + y, e)
        o_ref[rows, :] = functools.reduce(
            lambda x, y: x + y, [(e[g] / den) * o_scr[g, rows, :] for g in range(ng)]).astype(o_ref.dtype)
        return carry

    lax.fori_loop(0, SWA_SPAN // (2 * blk), combine, 0)


def _swa(proj, bias, bsz, seq):
    t = proj.shape[0]
    ng = len(SWA_CONFIGS)
    nspan = seq // SWA_SPAN
    npair = SWA_HEADS * SWA_DH // LANES
    group_cols = SWA_HEADS * SWA_DH // LANES
    in_specs, scratch_k = [], []
    for g, (_, d) in enumerate(SWA_CONFIGS):
        halo = SWA_BLOCK * d
        per_span = SWA_SPAN // halo

        def cur(which, g=g):
            col = (which * ng + g) * group_cols
            return lambda b, j, p: (b * nspan + j, col + p)

        def prev(which, g=g, per_span=per_span):
            col = (which * ng + g) * group_cols
            return lambda b, j, p: (jnp.maximum((b * nspan + j) * per_span - 1, 0), col + p)

        in_specs += [pl.BlockSpec((SWA_SPAN, LANES), cur(0)),
                     pl.BlockSpec((SWA_SPAN, LANES), cur(1)), pl.BlockSpec((halo, LANES), prev(1)),
                     pl.BlockSpec((SWA_SPAN, LANES), cur(2)), pl.BlockSpec((halo, LANES), prev(2))]
        scratch_k.append(pltpu.VMEM((halo + SWA_SPAN, LANES), F32))
    in_specs.append(pl.BlockSpec((ng, 1, 2, 2 * SWA_BLOCK, 2 * SWA_BLOCK), lambda b, j, p: (0, p, 0, 0, 0)))
    return pl.pallas_call(
        _swa_body,
        grid=(bsz, nspan, npair),
        in_specs=in_specs,
        out_specs=pl.BlockSpec((SWA_SPAN, LANES), lambda b, j, p: (b * nspan + j, p)),
        out_shape=jax.ShapeDtypeStruct((t, npair * LANES), BF16),
        scratch_shapes=scratch_k + scratch_k + [pltpu.VMEM((ng, SWA_SPAN, LANES), F32)] * 2,
        compiler_params=_cparams(("parallel", "parallel", "parallel")),
        name="swa",
    )(*([proj] * (5 * ng)), bias)


def _layer_norm(y, g, b):
    mu = jnp.mean(y, axis=-1, keepdims=True)
    yc = y - mu
    var = jnp.mean(yc * yc, axis=-1, keepdims=True)
    return yc * lax.rsqrt(var + LN_EPS) * g + b


def _diff_bias_tables(blk):
    buckets = _rel_bucket_np(np.arange(2 * REL_MAX_DIST))
    far = int(np.max(np.nonzero(buckets != NUM_BUCKETS - 1)[0])) + 1
    nb = -(-(far + blk - 1) // blk)
    kk = np.arange(blk)[:, None]
    qq = np.arange(blk)[None, :]
    dist = np.stack([t * blk + qq - kk for t in range(nb + 1)])
    bucket = _rel_bucket_np(np.maximum(dist, 0))
    neg = np.where(dist >= 0, 0.0, NEG_BIG).astype(np.float32)
    return bucket, neg


def _diff_body(blk, nb, lam_init, q_ref, qn_ref, k_ref, v_ref, bias_ref, lam_ref, nw_ref, o_ref,
               vt, acc1, acc2, s_a, s_b):
    qi = pl.program_id(2)
    dv = 2 * DIFF_DH
    seq = k_ref.shape[0]

    @pl.when(qi == 0)
    def _():
        vt[dv:, :] = jnp.ones((vt.shape[0] - dv, seq), BF16)

        def fill(c, carry):
            st = pl.multiple_of(c * blk, blk)
            vt[0:dv, pl.ds(st, blk)] = v_ref[pl.ds(st, blk), :].astype(F32).T.astype(BF16)
            return carry

        lax.fori_loop(0, seq // blk, fill, 0)

    feature = lax.broadcasted_iota(jnp.int32, (dv, blk), 0)

    def components(ref):
        q_t = (ref[...].astype(F32) * (DIFF_DH ** -0.5 * LOG2E)).T
        return (jnp.where(feature < DIFF_DH, q_t, 0.0).astype(BF16),
                jnp.where(feature >= DIFF_DH, q_t, 0.0).astype(BF16))

    qs = components(q_ref)
    qs_next = components(qn_ref)
    accs = (acc1, acc2)
    for acc in accs:
        acc[...] = jnp.zeros_like(acc)

    last = pl.num_programs(2) - 1

    def key_rows(kj):
        return pl.ds(pl.multiple_of(jnp.minimum(kj, last) * blk, blk), blk)

    def scores(kj, dst, queries=qs):
        k = k_ref[key_rows(kj), :]
        for c, qc in enumerate(queries):
            dst[c] = jnp.dot(k, qc, preferred_element_type=F32)

    def consume(biased, kj, src, ms):
        vtb = vt[:, key_rows(kj)]
        out = []
        for c, (m, acc) in enumerate(zip(ms, accs)):
            s = src[c]
            if biased:
                s = s + bias_ref[0, jnp.minimum(qi - kj, nb)]
            m_new = jnp.maximum(m, jnp.max(s, axis=0, keepdims=True))
            p = jnp.exp2(s - m_new).astype(BF16)
            acc[...] = jnp.exp2(m - m_new) * acc[...] + jnp.dot(vtb, p, preferred_element_type=F32)
            out.append(m_new)
        return tuple(out)

    nblocks = qi + 1
    trips = nblocks // 2
    odd = nblocks % 2 == 1
    far_trips = jnp.maximum(qi - nb + 1, 0) // 2

    def pair(biased, t, ms):
        kj = 2 * t
        scores(kj + 1, s_b)
        ms = consume(biased, kj, s_a, ms)
        if biased:
            hand_off = (t == trips - 1) & jnp.logical_not(odd)
            scores(jnp.where(hand_off, 0, kj + 2), s_a,
                   tuple(jnp.where(hand_off, qn, qc) for qc, qn in zip(qs, qs_next)))
        else:
            scores(kj + 2, s_a)
        return consume(biased, kj + 1, s_b, ms)

    @pl.when(qi == 0)
    def _():
        scores(0, s_a)

    m0 = jnp.full((1, blk), NEG_BIG, F32)
    ms = lax.fori_loop(0, far_trips, functools.partial(pair, False), (m0, m0))
    ms = lax.fori_loop(far_trips, trips, functools.partial(pair, True), ms)

    @pl.when(odd)
    def _():
        consume(True, qi, s_a, ms)
        scores(0, s_a, qs_next)

    lp = lam_ref[...]
    lam = (jnp.exp(jnp.sum(lp[0:1] * lp[1:2], axis=-1, keepdims=True))
           - jnp.exp(jnp.sum(lp[2:3] * lp[3:4], axis=-1, keepdims=True)) + lam_init)
    a1, a2 = acc1[...], acc2[...]
    o_t = a1[:dv] * (1.0 / a1[dv:dv + 1]) - a2[:dv] * (lam / a2[dv:dv + 1])
    o = o_t.T
    o = o * lax.rsqrt(jnp.mean(o * o, axis=-1, keepdims=True) + RMS_EPS) * nw_ref[...]
    o_ref[...] = (o * (1.0 - lam_init)).astype(o_ref.dtype)


def _diff_attention(proj, bias, lam_params, norm_w, lam_init, bsz, seq, blk):
    t = proj.shape[0]
    nq = seq // blk
    nt = bias.shape[1]
    nb = nt - 1
    width = 2 * DIFF_DH
    ones_rows = 2 * SUBLANES
    return pl.pallas_call(
        functools.partial(_diff_body, blk, nb, lam_init),
        grid=(bsz, DIFF_HEADS, nq),
        in_specs=[pl.BlockSpec((blk, width), lambda b, h, i: (b * nq + i, OD_QC // width + h)),
                  pl.BlockSpec((blk, width),
                               lambda b, h, i: (b * nq + jnp.minimum(i + 1, nq - 1), OD_QC // width + h)),
                  pl.BlockSpec((seq, width), lambda b, h, i: (b, OD_KC // width + h)),
                  pl.BlockSpec((seq, width), lambda b, h, i: (b, OD_VC // width + h)),
                  pl.BlockSpec((1, nt, blk, blk), lambda b, h, i: (h, 0, 0, 0)),
                  pl.BlockSpec((4, DIFF_DH), lambda b, h, i: (0, 0)),
                  pl.BlockSpec((1, width), lambda b, h, i: (0, 0))],
        out_specs=pl.BlockSpec((blk, width), lambda b, h, i: (b * nq + i, h)),
        out_shape=jax.ShapeDtypeStruct((t, DIFF_HEADS * width), BF16),
        scratch_shapes=[pltpu.VMEM((width + ones_rows, seq), BF16),
                        pltpu.VMEM((width + ones_rows, blk), F32), pltpu.VMEM((width + ones_rows, blk), F32),
                        pltpu.VMEM((2, blk, blk), F32), pltpu.VMEM((2, blk, blk), F32)],
        compiler_params=_cparams(("arbitrary", "arbitrary", "arbitrary")),
        name="diff_attn",
    )(proj, proj, proj, proj, bias, lam_params, norm_w)


def _gla_body(q_ref, k_ref, gd_ref, v_ref, r_ref, wg_ref, bg_ref, nw_ref, o_ref, state, part):
    n = pl.program_id(1)

    @pl.when(n == 0)
    def _():
        state[...] = jnp.zeros_like(state)

    rows = GLA_GROUP
    nc = rows // CHUNK
    npair = GLA_HEADS // 2
    gate = jnp.dot(gd_ref[...], wg_ref[...], precision=HI, preferred_element_type=F32) + bg_ref[...]
    log_a = _log_sigmoid(gate) * (1.0 / GLA_TAU)

    ri = lax.broadcasted_iota(jnp.int32, (rows, rows), 0)
    ci = lax.broadcasted_iota(jnp.int32, (rows, rows), 1)
    causal = ((ri // CHUNK) == (ci // CHUNK)) & (ri >= ci)
    tri = causal.astype(BF16)
    hi = log_a.astype(BF16)
    rem = log_a - hi.astype(F32)
    mid = rem.astype(BF16)
    lo = (rem - mid.astype(F32)).astype(BF16)
    b_all = (jnp.dot(tri, hi, preferred_element_type=F32) + jnp.dot(tri, mid, preferred_element_type=F32)
             + jnp.dot(tri, lo, preferred_element_type=F32))

    lane = lax.broadcasted_iota(jnp.int32, (rows, LANES), 1)
    lane_c = lax.broadcasted_iota(jnp.int32, (CHUNK, LANES), 1)
    head_lanes = (lane < GLA_DK, lane >= GLA_DK)
    chunk_lanes = (lane_c < GLA_DK, lane_c >= GLA_DK)
    pairs = []
    for p in range(npair):
        cols = slice(p * LANES, (p + 1) * LANES)
        pairs.append(dict(
            b=b_all[:, cols],
            q=q_ref[:, cols].astype(F32) * GLA_DK ** -0.5,
            k=k_ref[:, cols].astype(F32),
            v=[v_ref[:, (2 * p + hd) * GLA_DV:(2 * p + hd + 1) * GLA_DV].astype(F32) for hd in range(2)]))

    for pr in pairs:
        pr["q_dec"] = pr["q"] * jnp.exp(pr["b"])
    for c in range(nc):
        sl = slice(c * CHUNK, (c + 1) * CHUNK)
        for p, pr in enumerate(pairs):
            bc = pr["b"][sl]
            b_last = bc[CHUNK - 1:CHUNK]
            k_dec = pr["k"][sl] * jnp.exp(b_last - bc)
            e_last = jnp.exp(b_last)
            for hd in range(2):
                h = 2 * p + hd
                st = state[h]
                part[h, sl, :] = _bdot_nt(jnp.where(chunk_lanes[hd], pr["q_dec"][sl], 0.0), st)
                state[h] = st * e_last + _bdot_tn(pr["v"][hd][sl], k_dec)

    def finish(intra):
        for h in range(GLA_HEADS):
            o = part[h] + intra[h]
            o = o * lax.rsqrt(jnp.mean(o * o, axis=-1, keepdims=True) + RMS_EPS) * nw_ref[...]
            gate_r = _silu(r_ref[:, h * GLA_DV:(h + 1) * GLA_DV].astype(F32))
            o_ref[:, h * GLA_DV:(h + 1) * GLA_DV] = (o * gate_r).astype(o_ref.dtype)

    def intra_whole_chunk():
        out = []
        for pr in pairs:
            k_inv = pr["k"] * jnp.exp(jnp.minimum(-pr["b"], GLA_MAX_DECAY))
            for hd in range(2):
                a = _bdot_nt(jnp.where(head_lanes[hd], pr["q_dec"], 0.0), k_inv)
                out.append(_bdot(jnp.where(causal, a, 0.0), pr["v"][hd]))
        return out

    def intra_exact():
        out = []
        for pr in pairs:
            off = _gla_intra_off_diagonal(pr["q"], pr["k"], pr["b"], pr["v"], nc)
            diag = _gla_intra_diagonal(pr["q"], pr["k"], pr["b"], pr["v"])
            out += [off[hd] + diag[hd] for hd in range(2)]
        return out

    chunk_decay = jnp.max(-b_all.reshape(nc, CHUNK, npair * LANES)[:, CHUNK - 1:CHUNK, :])

    @pl.when(chunk_decay <= GLA_MAX_DECAY)
    def _():
        finish(intra_whole_chunk())

    @pl.when(chunk_decay > GLA_MAX_DECAY)
    def _():
        finish(intra_exact())


def _gla_intra_off_diagonal(q, k, b, vs, nc):
    per_chunk = CHUNK // GLA_SUB
    lane = lax.broadcasted_iota(jnp.int32, (GLA_SUB, LANES), 1)
    sub_mask = (lane < GLA_DK, lane >= GLA_DK)
    kcol = lax.broadcasted_iota(jnp.int32, (GLA_SUB, CHUNK), 1)
    outs = ([], [])
    for c in range(nc):
        sl = slice(c * CHUNK, (c + 1) * CHUNK)
        bc, qc, kc = b[sl], q[sl], k[sl]
        a_rows = [[jnp.zeros((GLA_SUB, CHUNK), F32)] for _ in vs]
        for blk in range(1, per_chunk):
            r0 = blk * GLA_SUB
            bref = bc[r0:r0 + 1]
            qs = qc[r0:r0 + GLA_SUB] * jnp.exp(bc[r0:r0 + GLA_SUB] - bref)
            ks = kc * jnp.exp(jnp.minimum(bref - bc, 0.0))
            for hd in range(2):
                a = _bdot_nt(jnp.where(sub_mask[hd], qs, 0.0), ks)
                a_rows[hd].append(jnp.where(kcol < r0, a, 0.0))
        for hd in range(2):
            outs[hd].append(_bdot(jnp.concatenate(a_rows[hd], axis=0), vs[hd][sl]))
    return [jnp.concatenate(o, axis=0) for o in outs]


def _gla_intra_diagonal(q, k, b, vs):
    rows = q.shape[0]
    nsub = rows // GLA_SUB
    b3 = b.reshape(nsub, GLA_SUB, LANES)
    q3 = q.reshape(nsub, GLA_SUB, LANES)
    k3 = k.reshape(nsub, GLA_SUB, LANES)
    v3 = [v.reshape(nsub, GLA_SUB, LANES) for v in vs]
    row3 = lax.broadcasted_iota(jnp.int32, (nsub, GLA_SUB, LANES), 1)
    lane3 = lax.broadcasted_iota(jnp.int32, (nsub, GLA_SUB, LANES), 2)
    rowc = lax.broadcasted_iota(jnp.int32, (nsub, GLA_SUB, 1), 1)
    o3 = [jnp.zeros((nsub, GLA_SUB, LANES), F32) for _ in vs]
    for jj in range(GLA_SUB):
        e = jnp.exp(jnp.where(row3 >= jj, b3 - b3[:, jj:jj + 1, :], 0.0))
        t = q3 * k3[:, jj:jj + 1, :] * e
        w_all = jnp.sum(t, axis=-1, keepdims=True)
        w_a = jnp.sum(jnp.where(lane3 < GLA_DK, t, 0.0), axis=-1, keepdims=True)
        for hd, w in enumerate((w_a, w_all - w_a)):
            o3[hd] = o3[hd] + jnp.where(rowc >= jj, w, 0.0) * v3[hd][:, jj:jj + 1, :]
    return [o.reshape(rows, LANES) for o in o3]


def _gla(proj, gate_in, w_gate_pad, b_gate, norm_w, bsz, seq):
    t = proj.shape[0]
    rows = GLA_GROUP
    spb = seq // rows
    qk_w = GLA_HEADS * GLA_DK
    v_w = GLA_HEADS * GLA_DV

    def at(col):
        return lambda b, n: (b * spb + n, col)

    def whole(a):
        return pl.BlockSpec(a.shape, lambda b, n: (0, 0))

    return pl.pallas_call(
        _gla_body,
        grid=(bsz, spb),
        in_specs=[pl.BlockSpec((rows, qk_w), at(OD_QD // qk_w)),
                  pl.BlockSpec((rows, qk_w), at(OD_KD // qk_w)),
                  pl.BlockSpec((rows, LANES), at(0)),
                  pl.BlockSpec((rows, v_w), at(OD_VD // v_w)),
                  pl.BlockSpec((rows, v_w), at(OD_RD // v_w)),
                  whole(w_gate_pad), whole(b_gate), whole(norm_w)],
        out_specs=pl.BlockSpec((rows, v_w), at(0)),
        out_shape=jax.ShapeDtypeStruct((t, v_w), BF16),
        scratch_shapes=[pltpu.VMEM((GLA_HEADS, GLA_DV, LANES), F32), pltpu.VMEM((GLA_HEADS, rows, GLA_DV), F32)],
        compiler_params=_cparams(("parallel", "arbitrary")),
        name="gla",
    )(proj, proj, gate_in, proj, proj, w_gate_pad, b_gate, norm_w)


def _tail_body(tiles_per_seq, tc, ma_ref, ma_halo, mb_ref, mb_halo, h_ref, h_halo, woa_ref, wob_ref,
               g1_ref, b1_ref, wu_ref, cw_ref, cb_ref, wd_ref, g2_ref, b2_ref, y_ref,
               act, scratch_g, scratch_v):
    i = pl.program_id(0)

    def ext(halo_ref, ref):
        return jnp.concatenate([halo_ref[...], ref[...]], axis=0)

    mix = (jnp.dot(ext(ma_halo, ma_ref), woa_ref[...], preferred_element_type=F32)
           + jnp.dot(ext(mb_halo, mb_ref), wob_ref[...], preferred_element_type=F32))
    x_ext = _layer_norm(DEEPNORM_ALPHA * ext(h_halo, h_ref) + mix, g1_ref[...], b1_ref[...])
    x = x_ext[HALO:]
    row = lax.broadcasted_iota(jnp.int32, x_ext.shape, 0)
    first = i % tiles_per_seq == 0
    xb = jnp.where((row < HALO) & first, 0.0, x_ext).astype(BF16)

    def branch(lo, scratch):
        full = jnp.dot(xb, wu_ref[:, lo:lo + tc], preferred_element_type=F32)
        return _causal_conv(full, cw_ref[:, lo:lo + tc], scratch) + cb_ref[:, lo:lo + tc]

    for c in range(D_FF // tc):
        gate = branch(c * tc, scratch_g)
        val = branch(D_FF + c * tc, scratch_v)
        act[:, c * tc:(c + 1) * tc] = (_silu(gate) * val).astype(BF16)
    ffn = jnp.dot(act[...], wd_ref[...], preferred_element_type=F32)
    y_ref[...] = _layer_norm(DEEPNORM_ALPHA * x + ffn, g2_ref[...], b2_ref[...])


def _layer_tail(mix_a, mix_b, h, w_out_a, w_out_b, g1, b1, w_up, conv_w, conv_b, w_down, g2, b2,
                layer, seq):
    t = h.shape[0]
    tm, tc = TAIL_TM, FFN_TC
    hb = tm // HALO

    def tile(a):
        return pl.BlockSpec((tm, a.shape[1]), lambda i: (i, 0))

    def halo(a):
        return pl.BlockSpec((HALO, a.shape[1]), lambda i: (jnp.maximum(i * hb - 1, 0), 0))

    params = (w_out_a, w_out_b, g1, b1, w_up, conv_w, conv_b, w_down, g2, b2)
    param_specs = [_resident(w_out_a), _resident(w_out_b), _resident(g1), _resident(b1),
                   _resident(w_up, layer), _resident(conv_w), _resident(conv_b), _resident(w_down, layer),
                   _resident(g2), _resident(b2)]
    return pl.pallas_call(
        functools.partial(_tail_body, seq // tm, tc),
        grid=(t // tm,),
        in_specs=[tile(mix_a), halo(mix_a), tile(mix_b), halo(mix_b), tile(h), halo(h)] + param_specs,
        out_specs=pl.BlockSpec((tm, D_MODEL), lambda i: (i, 0)),
        out_shape=jax.ShapeDtypeStruct((t, D_MODEL), F32),
        scratch_shapes=[pltpu.VMEM((tm, D_FF), BF16), pltpu.VMEM((HALO + tm, tc), F32),
                        pltpu.VMEM((HALO + tm, tc), F32)],
        compiler_params=_cparams(("parallel",)),
        name="layer_tail",
    )(mix_a, mix_a, mix_b, mix_b, h, h, *params)


def _even_w_in(w):
    a_end = 4 * EVA_PART
    gates = w[:, a_end:a_end + 2 * GDN_HEADS]
    qkv_b = w[:, a_end + 2 * GDN_HEADS:]
    pad = jnp.zeros((w.shape[0], EVB_COLS - EVB_BA - 2 * GDN_HEADS), w.dtype)
    return w[:, :a_end].astype(BF16), jnp.concatenate([qkv_b, gates, pad], axis=1).astype(BF16)


def _odd_w_in(w):
    pad = jnp.zeros((w.shape[0], LANES - GLA_RANK), w.dtype)
    return w[:, :OD_COLS].astype(BF16), jnp.concatenate([w[:, OD_COLS:], pad], axis=1).astype(BF16)


def _even_mixer(h, rel_bias, w_in, conv_w, a_log, dt_bias, norm_w, w_out, bsz, seq):
    w_a, w_b = _even_w_in(w_in)
    proj_a, proj_b = _even_in_proj(h, w_a, w_b, conv_w, seq, PROJ_TM)
    gate_pad = jnp.zeros((1, LANES), F32)
    alog_pad = lax.dynamic_update_slice(gate_pad, a_log[None].astype(F32), (0, GDN_HEADS))
    dtb_pad = lax.dynamic_update_slice(gate_pad, dt_bias[None].astype(F32), (0, GDN_HEADS))
    o_a = _gdn(proj_a, proj_b, alog_pad, dtb_pad, norm_w[None], bsz, seq)
    tiles = _bias_tiles(rel_bias, *_swa_bias_tables(), scale=LOG2E)
    ng, two_blk = len(SWA_CONFIGS), 2 * SWA_BLOCK
    bias = tiles.reshape(SWA_HEADS // 2, 2, 2, ng, SWA_BLOCK, two_blk).transpose(3, 0, 2, 1, 4, 5)
    o_b = _swa(proj_b, bias.reshape(ng, SWA_HEADS // 2, 2, two_blk, two_blk), bsz, seq)
    w_out = w_out.astype(BF16)
    return o_a, o_b, w_out[:EVA_PART], w_out[EVA_PART:]


def _odd_mixer(h, rel_bias, w_in, lam_params, diff_norm_w, w_gate, b_gate, gla_norm_w, w_out,
               lam_init, bsz, seq):
    w_main, w_gd = _odd_w_in(w_in)
    proj, gate_in = _odd_in_proj(h, w_main, w_gd, PROJ_TM)
    blk = min(DIFF_BLOCK, seq)
    bucket, neg = _diff_bias_tables(blk)
    bias = _bias_tiles(rel_bias, bucket, neg, scale=LOG2E, base_bucket=NUM_BUCKETS - 1)
    o_c = _diff_attention(proj, bias, lam_params, diff_norm_w[None], lam_init, bsz, seq, blk)
    w_gate_pad = jnp.concatenate(
        [w_gate, jnp.zeros((LANES - GLA_RANK, w_gate.shape[1]), w_gate.dtype)], axis=0)
    o_d = _gla(proj, gate_in, w_gate_pad, b_gate[None], gla_norm_w[None], bsz, seq)
    diff_v = DIFF_HEADS * 2 * DIFF_DH
    w_out = w_out.astype(BF16)
    return o_c, o_d, w_out[:diff_v], w_out[diff_v:]


def kernel(x, rel_bias, w_in_even, gdn_conv_w, gdn_a_log, gdn_dt_bias, gdn_norm_w, w_out_even,
           w_in_odd, diff_lambda, diff_norm_w, gla_w_gate, gla_b_gate, gla_norm_w, w_out_odd,
           ffn_w_up, ffn_conv_w, ffn_conv_b, ffn_w_down, ln_g, ln_b):
    bsz, seq, d = x.shape
    h = x.reshape(bsz * seq, d)
    w_up, w_down = ffn_w_up.astype(BF16), ffn_w_down.astype(BF16)
    for layer in range(DEPTH):
        i = layer // 2
        if layer % 2 == 0:
            mixed = _even_mixer(h, rel_bias, w_in_even[i], gdn_conv_w[i], gdn_a_log[i], gdn_dt_bias[i],
                                gdn_norm_w[i], w_out_even[i], bsz, seq)
        else:
            lam_init = 0.8 - 0.6 * math.exp(-0.3 * layer)
            mixed = _odd_mixer(h, rel_bias, w_in_odd[i], diff_lambda[i], diff_norm_w[i], gla_w_gate[i],
                               gla_b_gate[i], gla_norm_w[i], w_out_odd[i], lam_init, bsz, seq)
        h = _layer_tail(*mixed[:2], h, *mixed[2:], ln_g[layer, 0][None], ln_b[layer, 0][None],
                        w_up, ffn_conv_w[layer], ffn_conv_b[layer][None], w_down,
                        ln_g[layer, 1][None], ln_b[layer, 1][None], layer, seq)
    return h.reshape(bsz, seq, d)
```

```python
import functools
import math

import numpy as np
import jax
import jax.numpy as jnp
from jax import lax
from jax.experimental import pallas as pl
from jax.experimental.pallas import tpu as pltpu

F32 = jnp.float32
BF16 = jnp.bfloat16
HI = lax.Precision.HIGHEST

D_MODEL = 1024
DEPTH = 2
DEEPNORM_ALPHA = (2 * DEPTH) ** 0.25
LN_EPS = 1e-5
RMS_EPS = 1e-6
NUM_BUCKETS = 32
REL_MAX_DIST = 2048
GDN_HEADS = 6
GDN_D = 128
CHUNK = 64
GDN_GROUP = 256
GDN_STEP_GROUPS = 1
SWA_CONFIGS = ((128, 1), (512, 4), (2048, 16))
SWA_HEADS = 4
SWA_DH = 64
SWA_BLOCK = 128
SWA_UNROLL = 8
SWA_SPAN = SWA_BLOCK * max(d for _, d in SWA_CONFIGS)
DIFF_HEADS = 4
DIFF_DH = 64
DIFF_BLOCK = 512
LOG2E = math.log2(math.e)
GLA_HEADS = 4
GLA_DK = 64
GLA_DV = 128
GLA_RANK = 16
GLA_TAU = 16.0
GLA_SUB = 16
GLA_GROUP = 256
GLA_MAX_DECAY = 60.0
D_FF = 2816

LANES = 128
SUBLANES = 8
HALO = 16
VMEM_LIMIT = 56 * 1024 * 1024
NEG_BIG = -1e30

EVA_PART = GDN_HEADS * GDN_D
EVB_BA = 2304
EVB_COLS = 2560
EVB_TN = 512
OD_QC, OD_KC, OD_VC = 0, 512, 1024
OD_QD, OD_KD, OD_VD, OD_RD = 1536, 1792, 2048, 2560
OD_COLS = 3072
OD_TN = 768
PROJ_TM = 512
TAIL_TM = 512
FFN_TC = 256


def _cparams(sem):
    return pltpu.CompilerParams(dimension_semantics=sem, vmem_limit_bytes=VMEM_LIMIT)


def _bdot(a, b):
    return jnp.dot(a.astype(BF16), b.astype(BF16), preferred_element_type=F32)


def _bdot_nt(a, b):
    return lax.dot_general(a.astype(BF16), b.astype(BF16), (((1,), (1,)), ((), ())),
                           preferred_element_type=F32)


def _bdot_tn(a, b):
    return lax.dot_general(a.astype(BF16), b.astype(BF16), (((0,), (0,)), ((), ())),
                           preferred_element_type=F32)


def _sigmoid(x):
    return 1.0 / (1.0 + jnp.exp(-x))


def _silu(x):
    return x * _sigmoid(x)


def _softplus(x):
    return jnp.maximum(x, 0.0) + jnp.log1p(jnp.exp(-jnp.abs(x)))


def _log_sigmoid(x):
    return -_softplus(-x)


def _resident(a, lead=None):
    if lead is None:
        return pl.BlockSpec(a.shape, lambda i: (0,) * a.ndim, pipeline_mode=pl.Buffered(1))
    return pl.BlockSpec((None,) + a.shape[1:], lambda i: (lead,) + (0,) * (a.ndim - 1),
                        pipeline_mode=pl.Buffered(1))


def _odd_in_body(x_ref, w_ref, wg_ref, o_ref, og_ref):
    xb = x_ref[...].astype(BF16)
    for c in range(OD_COLS // OD_TN):
        cols = slice(c * OD_TN, (c + 1) * OD_TN)
        o_ref[:, cols] = jnp.dot(xb, w_ref[:, cols], preferred_element_type=F32).astype(o_ref.dtype)
    og_ref[...] = jnp.dot(xb, wg_ref[...], preferred_element_type=F32)


def _odd_in_proj(x, w, w_gate_in, tm):
    t, k = x.shape
    return pl.pallas_call(
        _odd_in_body,
        grid=(t // tm,),
        in_specs=[pl.BlockSpec((tm, k), lambda i: (i, 0)), _resident(w), _resident(w_gate_in)],
        out_specs=[pl.BlockSpec((tm, OD_COLS), lambda i: (i, 0)), pl.BlockSpec((tm, LANES), lambda i: (i, 0))],
        out_shape=[jax.ShapeDtypeStruct((t, OD_COLS), BF16), jax.ShapeDtypeStruct((t, LANES), F32)],
        compiler_params=_cparams(("parallel",)),
        name="odd_in_proj",
    )(x, w, w_gate_in)


def _rel_bucket_np(dist):
    max_exact = NUM_BUCKETS // 2
    d = np.maximum(dist, 1).astype(np.float32)
    large = max_exact + (np.log(d / max_exact) / math.log(REL_MAX_DIST / max_exact)
                         * (NUM_BUCKETS - max_exact)).astype(np.int32)
    large = np.minimum(large, NUM_BUCKETS - 1)
    return np.where(dist < max_exact, dist, large).astype(np.int32)


def _bias_body(scale, base_bucket, tile_buckets, rb_ref, bucket_ref, neg_ref, o_ref):
    h = pl.program_id(0)
    t = pl.program_id(1)
    base = 0.0 if base_bucket is None else rb_ref[base_bucket, h]
    for tile, present in enumerate(tile_buckets):
        @pl.when(t == tile)
        def _(present=present):
            bucket = bucket_ref[0]
            acc = neg_ref[0]
            for b in present:
                acc = acc + jnp.where(bucket == b, (rb_ref[b, h] - base) * scale, 0.0)
            o_ref[0, 0] = acc


def _bias_tiles(rel_bias, bucket, neg, scale=1.0, base_bucket=None):
    nt, r, c = bucket.shape
    nh = rel_bias.shape[1]
    tile_buckets = tuple(tuple(int(b) for b in np.unique(bucket[t][neg[t] == 0]) if b != base_bucket)
                         for t in range(nt))
    return pl.pallas_call(
        functools.partial(_bias_body, scale, base_bucket, tile_buckets),
        grid=(nh, nt),
        in_specs=[pl.BlockSpec(memory_space=pltpu.SMEM),
                  pl.BlockSpec((1, r, c), lambda h, t: (t, 0, 0)),
                  pl.BlockSpec((1, r, c), lambda h, t: (t, 0, 0))],
        out_specs=pl.BlockSpec((1, 1, r, c), lambda h, t: (h, t, 0, 0)),
        out_shape=jax.ShapeDtypeStruct((nh, nt, r, c), F32),
        compiler_params=_cparams(("parallel", "parallel")),
        name="rel_bias_tiles",
    )(rel_bias, jnp.asarray(bucket), jnp.asarray(neg))


def _causal_conv(full, w, scratch):
    width = w.shape[0]
    rows = full.shape[0] - HALO
    scratch[...] = full
    y = w[width - 1:width, :] * full[HALO:]
    for j in range(width - 1):
        back = width - 1 - j
        y = y + w[j:j + 1, :] * scratch[HALO - back:HALO - back + rows, :]
    return y


def _even_in_body(tiles_per_seq, x_ref, halo_ref, wa_ref, wb_ref, cw_ref, oa_ref, ob_ref, scratch):
    i = pl.program_id(0)
    halo = jnp.where(i % tiles_per_seq == 0, 0.0, halo_ref[...])
    x = x_ref[...].astype(BF16)
    x_ext = jnp.concatenate([halo.astype(BF16), x], axis=0)
    for part in range(3):
        cols = slice(part * EVA_PART, (part + 1) * EVA_PART)
        y = jnp.dot(x_ext, wa_ref[:, cols], preferred_element_type=F32)
        c = _silu(_causal_conv(y, cw_ref[:, cols], scratch))
        for hd in range(GDN_HEADS):
            ch = c[:, hd * GDN_D:(hd + 1) * GDN_D]
            if part < 2:
                inv = lax.rsqrt(jnp.sum(ch * ch, axis=-1, keepdims=True) + RMS_EPS)
                ch = ch * (inv * GDN_D ** -0.5 if part == 0 else inv)
            lo = part * EVA_PART + hd * GDN_D
            oa_ref[:, lo:lo + GDN_D] = ch.astype(oa_ref.dtype)
    z_cols = slice(3 * EVA_PART, 4 * EVA_PART)
    oa_ref[:, z_cols] = jnp.dot(x, wa_ref[:, z_cols], preferred_element_type=F32).astype(oa_ref.dtype)
    for c in range(EVB_COLS // EVB_TN):
        cols = slice(c * EVB_TN, (c + 1) * EVB_TN)
        ob_ref[:, cols] = jnp.dot(x, wb_ref[:, cols], preferred_element_type=F32)


def _even_in_proj(x, w_a, w_b, conv_w, seq, tm):
    t, k = x.shape
    hb = tm // HALO
    return pl.pallas_call(
        functools.partial(_even_in_body, seq // tm),
        grid=(t // tm,),
        in_specs=[pl.BlockSpec((tm, k), lambda i: (i, 0)),
                  pl.BlockSpec((HALO, k), lambda i: (jnp.maximum(i * hb - 1, 0), 0)),
                  _resident(w_a), _resident(w_b), _resident(conv_w)],
        out_specs=[pl.BlockSpec((tm, w_a.shape[1]), lambda i: (i, 0)),
                   pl.BlockSpec((tm, EVB_COLS), lambda i: (i, 0))],
        out_shape=[jax.ShapeDtypeStruct((t, w_a.shape[1]), BF16), jax.ShapeDtypeStruct((t, EVB_COLS), F32)],
        scratch_shapes=[pltpu.VMEM((HALO + tm, EVA_PART), F32)],
        compiler_params=_cparams(("parallel",)),
        name="even_in_proj",
    )(x, x, w_a, w_b, conv_w)


def _gdn_body(q_ref, k_ref, v_ref, z_ref, ba_ref, alog_ref, dtb_ref, nw_ref, o_ref, state):
    n = pl.program_id(1)

    @pl.when(n == 0)
    def _():
        state[...] = jnp.zeros_like(state)

    grp = GDN_GROUP
    nc = grp // CHUNK
    nh = GDN_HEADS
    hs = range(nh)
    ri = lax.broadcasted_iota(jnp.int32, (grp, grp), 0)
    ci = lax.broadcasted_iota(jnp.int32, (grp, grp), 1)
    same = (ri // CHUNK) == (ci // CHUNK)
    incl = same & (ri >= ci)
    strict = same & (ri > ci)
    eye = (ri == ci).astype(F32)
    tri = incl.astype(BF16)

    def chunk_local(rows):
        ba = ba_ref[rows, :]
        beta_all = _sigmoid(ba)
        g_all = -jnp.exp(alog_ref[...]) * _softplus(ba + dtb_ref[...])
        g_hi = g_all.astype(BF16)
        rem = g_all - g_hi.astype(F32)
        g_mid = rem.astype(BF16)
        g_lo = (rem - g_mid.astype(F32)).astype(BF16)
        gam_all = (jnp.dot(tri, g_hi, preferred_element_type=F32)
                   + jnp.dot(tri, g_mid, preferred_element_type=F32)
                   + jnp.dot(tri, g_lo, preferred_element_type=F32))
        gam_rows = gam_all.T

        q = [q_ref[rows, h * GDN_D:(h + 1) * GDN_D] for h in hs]
        k = [k_ref[rows, h * GDN_D:(h + 1) * GDN_D] for h in hs]
        v = [v_ref[rows, h * GDN_D:(h + 1) * GDN_D] for h in hs]
        gam = [jnp.broadcast_to(gam_all[:, nh + h:nh + h + 1], (grp, GDN_D)) for h in hs]
        bcol = [beta_all[:, h:h + 1] for h in hs]
        decay, x, inv = [], [], []
        for h in hs:
            diff = gam[h][:, 0:1] - gam_rows[nh + h:nh + h + 1, :]
            decay.append(jnp.where(incl, jnp.exp(jnp.where(incl, diff, 0.0)), 0.0))
            kk = _bdot_nt(k[h], k[h])
            x.append(-jnp.where(strict, bcol[h] * kk * decay[h], 0.0))
            inv.append(eye + x[h])
        for _ in range(5):
            for h in hs:
                x[h] = _bdot(x[h], x[h])
                inv[h] = inv[h] + _bdot(inv[h], x[h])
        u, w, qk, q_dec, k_dec, g_last = [], [], [], [], [], []
        for h in hs:
            eg = jnp.exp(gam[h])
            uw = _bdot(inv[h], jnp.concatenate([v[h] * bcol[h], k[h] * (bcol[h] * eg)], axis=1))
            u.append(uw[:, :GDN_D])
            w.append(uw[:, GDN_D:])
            qk.append(_bdot_nt(q[h], k[h]) * decay[h])
            q_dec.append(q[h] * eg)
            kd, gl = [], []
            for c in range(nc):
                last = gam[h][(c + 1) * CHUNK - 1:(c + 1) * CHUNK, :]
                kd.append(k[h][c * CHUNK:(c + 1) * CHUNK] * jnp.exp(last - gam[h][c * CHUNK:(c + 1) * CHUNK]))
                gl.append(jnp.exp(last))
            k_dec.append(kd)
            g_last.append(gl)
        return u, w, qk, q_dec, k_dec, g_last

    def recurrence(rows, local, s):
        u, w, qk, q_dec, k_dec, g_last = local
        q_s = [[] for _ in hs]
        delta = [[] for _ in hs]
        for c in range(nc):
            sl = slice(c * CHUNK, (c + 1) * CHUNK)
            for h in hs:
                r = _bdot(jnp.concatenate([w[h][sl], q_dec[h][sl]], axis=0), s[h])
                d = u[h][sl] - r[:CHUNK]
                q_s[h].append(r[CHUNK:])
                delta[h].append(d)
                s[h] = g_last[h][c] * s[h] + _bdot_tn(k_dec[h][c], d)
        for h in hs:
            o = jnp.concatenate(q_s[h], axis=0) + _bdot(qk[h], jnp.concatenate(delta[h], axis=0))
            o = o * lax.rsqrt(jnp.mean(o * o, axis=-1, keepdims=True) + RMS_EPS) * nw_ref[...]
            z = z_ref[rows, h * GDN_D:(h + 1) * GDN_D].astype(F32)
            o_ref[rows, h * GDN_D:(h + 1) * GDN_D] = (o * _silu(z)).astype(o_ref.dtype)
        return s

    groups = [slice(g * grp, (g + 1) * grp) for g in range(GDN_STEP_GROUPS)]
    local = [chunk_local(rows) for rows in groups]
    s = [state[h] for h in hs]
    for rows, loc in zip(groups, local):
        s = recurrence(rows, loc, s)
    for h in hs:
        state[h] = s[h]


def _gdn(proj_a, proj_b, alog_pad, dtb_pad, norm_w, bsz, seq):
    t = proj_a.shape[0]
    rows = GDN_GROUP * GDN_STEP_GROUPS
    spb = seq // rows
    width = GDN_HEADS * GDN_D

    def at(col):
        return lambda b, n: (b * spb + n, col)

    return pl.pallas_call(
        _gdn_body,
        grid=(bsz, spb),
        in_specs=[pl.BlockSpec((rows, width), at(0)),
                  pl.BlockSpec((rows, width), at(1)),
                  pl.BlockSpec((rows, width), at(2)),
                  pl.BlockSpec((rows, width), at(3)),
                  pl.BlockSpec((rows, LANES), at(EVB_BA // LANES)),
                  pl.BlockSpec((1, LANES), lambda b, n: (0, 0)),
                  pl.BlockSpec((1, LANES), lambda b, n: (0, 0)),
                  pl.BlockSpec((1, LANES), lambda b, n: (0, 0))],
        out_specs=pl.BlockSpec((rows, width), at(0)),
        out_shape=jax.ShapeDtypeStruct((t, width), BF16),
        scratch_shapes=[pltpu.VMEM((GDN_HEADS, GDN_D, GDN_D), F32)],
        compiler_params=_cparams(("parallel", "arbitrary")),
        name="gdn",
    )(proj_a, proj_a, proj_a, proj_a, proj_b, alog_pad, dtb_pad, norm_w)


def _swa_bias_tables():
    qi = np.arange(SWA_BLOCK)[:, None] + SWA_BLOCK
    kj = np.arange(2 * SWA_BLOCK)[None, :]
    rel = qi - kj
    buckets, negs = [], []
    for has_prev in (True, False):
        for window, dilation in SWA_CONFIGS:
            valid = (rel >= 0) & (rel <= window // dilation) & (has_prev | (kj >= SWA_BLOCK))
            buckets.append(_rel_bucket_np(np.maximum(rel, 0) * dilation))
            negs.append(np.where(valid, 0.0, NEG_BIG))
    return np.stack(buckets).astype(np.int32), np.stack(negs).astype(np.float32)


def _swa_body(*refs):
    ng = len(SWA_CONFIGS)
    ins, bias_ref, o_ref = refs[:5 * ng], refs[5 * ng], refs[5 * ng + 1]
    scratch = refs[5 * ng + 2:]
    kbufs, vbufs, o_scr, lse_scr = scratch[:ng], scratch[ng:2 * ng], scratch[2 * ng], scratch[2 * ng + 1]
    j = pl.program_id(1)
    blk = SWA_BLOCK
    first_head = lax.broadcasted_iota(jnp.int32, (blk, LANES), 1) < SWA_DH

    for g, (_, d) in enumerate(SWA_CONFIGS):
        q_ref, kc_ref, kp_ref, vc_ref, vp_ref = ins[5 * g:5 * g + 5]
        kbuf, vbuf = kbufs[g], vbufs[g]
        halo = blk * d
        kbuf[0:halo, :] = kp_ref[...]
        kbuf[halo:, :] = kc_ref[...]
        vbuf[0:halo, :] = vp_ref[...]
        vbuf[halo:, :] = vc_ref[...]

        def unit(u, carry, g=g, d=d, halo=halo, q_ref=q_ref, kbuf=kbuf, vbuf=vbuf):
            base = (u // d) * halo + u % d
            q = q_ref[pl.ds(base, blk, stride=d), :] * (SWA_DH ** -0.5 * LOG2E)
            k = kbuf[pl.ds(base, 2 * blk, stride=d), :]
            v = vbuf[pl.ds(base, 2 * blk, stride=d), :]
            lhs = jnp.concatenate([jnp.where(first_head, q, 0.0), jnp.where(first_head, 0.0, q)], axis=0)
            no_prev = jnp.where((j == 0) & (u < d), 1, 0)
            s = _bdot_nt(lhs, k) + bias_ref[g, 0, no_prev]
            m = jnp.max(s, axis=-1, keepdims=True)
            p = jnp.exp2(s - m)
            l = jnp.sum(p, axis=-1, keepdims=True)
            o2 = _bdot(p, v) * (1.0 / l)
            lse2 = m + jnp.log2(l)
            o_scr[g, pl.ds(base, blk, stride=d), :] = jnp.where(first_head, o2[:blk], o2[blk:])
            lse_scr[g, pl.ds(base, blk, stride=d), :] = jnp.where(first_head, lse2[:blk], lse2[blk:])
            return carry

        lax.fori_loop(0, SWA_SPAN // blk, unit, 0, unroll=SWA_UNROLL)

    def combine(c, carry):
        rows = pl.ds(pl.multiple_of(c * 2 * blk, 2 * blk), 2 * blk)
        lse = [lse_scr[g, rows, :] for g in range(ng)]
        m = functools.reduce(jnp.maximum, lse)
        e = [jnp.exp2(x - m) for x in lse]
        den = functools.reduce(lambda x, y: x + y, e)
        o_ref[rows, :] = functools.reduce(
            lambda x, y: x + y, [(e[g] / den) * o_scr[g, rows, :] for g in range(ng)]).astype(o_ref.dtype)
        return carry

    lax.fori_loop(0, SWA_SPAN // (2 * blk), combine, 0)


def _swa(proj, bias, bsz, seq):
    t = proj.shape[0]
    ng = len(SWA_CONFIGS)
    nspan = seq // SWA_SPAN
    npair = SWA_HEADS * SWA_DH // LANES
    group_cols = SWA_HEADS * SWA_DH // LANES
    in_specs, scratch_k = [], []
    for g, (_, d) in enumerate(SWA_CONFIGS):
        halo = SWA_BLOCK * d
        per_span = SWA_SPAN // halo

        def cur(which, g=g):
            col = (which * ng + g) * group_cols
            return lambda b, j, p: (b * nspan + j, col + p)

        def prev(which, g=g, per_span=per_span):
            col = (which * ng + g) * group_cols
            return lambda b, j, p: (jnp.maximum((b * nspan + j) * per_span - 1, 0), col + p)

        in_specs += [pl.BlockSpec((SWA_SPAN, LANES), cur(0)),
                     pl.BlockSpec((SWA_SPAN, LANES), cur(1)), pl.BlockSpec((halo, LANES), prev(1)),
                     pl.BlockSpec((SWA_SPAN, LANES), cur(2)), pl.BlockSpec((halo, LANES), prev(2))]
        scratch_k.append(pltpu.VMEM((halo + SWA_SPAN, LANES), F32))
    in_specs.append(pl.BlockSpec((ng, 1, 2, 2 * SWA_BLOCK, 2 * SWA_BLOCK), lambda b, j, p: (0, p, 0, 0, 0)))
    return pl.pallas_call(
        _swa_body,
        grid=(bsz, nspan, npair),
        in_specs=in_specs,
        out_specs=pl.BlockSpec((SWA_SPAN, LANES), lambda b, j, p: (b * nspan + j, p)),
        out_shape=jax.ShapeDtypeStruct((t, npair * LANES), BF16),
        scratch_shapes=scratch_k + scratch_k + [pltpu.VMEM((ng, SWA_SPAN, LANES), F32)] * 2,
        compiler_params=_cparams(("parallel", "parallel", "parallel")),
        name="swa",
    )(*([proj] * (5 * ng)), bias)


def _layer_norm(y, g, b):
    mu = jnp.mean(y, axis=-1, keepdims=True)
    yc = y - mu
    var = jnp.mean(yc * yc, axis=-1, keepdims=True)
    return yc * lax.rsqrt(var + LN_EPS) * g + b


def _diff_bias_tables(blk):
    buckets = _rel_bucket_np(np.arange(2 * REL_MAX_DIST))
    far = int(np.max(np.nonzero(buckets != NUM_BUCKETS - 1)[0])) + 1
    nb = -(-(far + blk - 1) // blk)
    kk = np.arange(blk)[:, None]
    qq = np.arange(blk)[None, :]
    dist = np.stack([t * blk + qq - kk for t in range(nb + 1)])
    bucket = _rel_bucket_np(np.maximum(dist, 0))
    neg = np.where(dist >= 0, 0.0, NEG_BIG).astype(np.float32)
    return bucket, neg


def _diff_body(blk, nb, lam_init, q_ref, qn_ref, k_ref, v_ref, bias_ref, lam_ref, nw_ref, o_ref,
               vt, acc1, acc2, s_a, s_b):
    qi = pl.program_id(2)
    dv = 2 * DIFF_DH
    seq = k_ref.shape[0]

    @pl.when(qi == 0)
    def _():
        vt[dv:, :] = jnp.ones((vt.shape[0] - dv, seq), BF16)

        def fill(c, carry):
            st = pl.multiple_of(c * blk, blk)
            vt[0:dv, pl.ds(st, blk)] = v_ref[pl.ds(st, blk), :].astype(F32).T.astype(BF16)
            return carry

        lax.fori_loop(0, seq // blk, fill, 0)

    feature = lax.broadcasted_iota(jnp.int32, (dv, blk), 0)

    def components(ref):
        q_t = (ref[...].astype(F32) * (DIFF_DH ** -0.5 * LOG2E)).T
        return (jnp.where(feature < DIFF_DH, q_t, 0.0).astype(BF16),
                jnp.where(feature >= DIFF_DH, q_t, 0.0).astype(BF16))

    qs = components(q_ref)
    qs_next = components(qn_ref)
    accs = (acc1, acc2)
    for acc in accs:
        acc[...] = jnp.zeros_like(acc)

    last = pl.num_programs(2) - 1

    def key_rows(kj):
        return pl.ds(pl.multiple_of(jnp.minimum(kj, last) * blk, blk), blk)

    def scores(kj, dst, queries=qs):
        k = k_ref[key_rows(kj), :]
        for c, qc in enumerate(queries):
            dst[c] = jnp.dot(k, qc, preferred_element_type=F32)

    def consume(biased, kj, src, ms):
        vtb = vt[:, key_rows(kj)]
        out = []
        for c, (m, acc) in enumerate(zip(ms, accs)):
            s = src[c]
            if biased:
                s = s + bias_ref[0, jnp.minimum(qi - kj, nb)]
            m_new = jnp.maximum(m, jnp.max(s, axis=0, keepdims=True))
            p = jnp.exp2(s - m_new).astype(BF16)
            acc[...] = jnp.exp2(m - m_new) * acc[...] + jnp.dot(vtb, p, preferred_element_type=F32)
            out.append(m_new)
        return tuple(out)

    nblocks = qi + 1
    trips = nblocks // 2
    odd = nblocks % 2 == 1
    far_trips = jnp.maximum(qi - nb + 1, 0) // 2

    def pair(biased, t, ms):
        kj = 2 * t
        scores(kj + 1, s_b)
        ms = consume(biased, kj, s_a, ms)
        scores(kj + 2, s_a)
        return consume(biased, kj + 1, s_b, ms)

    @pl.when(qi == 0)
    def _():
        scores(0, s_a)

    m0 = jnp.full((1, blk), NEG_BIG, F32)
    ms = lax.fori_loop(0, far_trips, functools.partial(pair, False), (m0, m0))
    ms = lax.fori_loop(far_trips, trips, functools.partial(pair, True), ms)

    @pl.when(odd)
    def _():
        consume(True, qi, s_a, ms)

    scores(0, s_a, qs_next)
    lp = lam_ref[...]
    lam = (jnp.exp(jnp.sum(lp[0:1] * lp[1:2], axis=-1, keepdims=True))
           - jnp.exp(jnp.sum(lp[2:3] * lp[3:4], axis=-1, keepdims=True)) + lam_init)
    a1, a2 = acc1[...], acc2[...]
    o_t = a1[:dv] * (1.0 / a1[dv:dv + 1]) - a2[:dv] * (lam / a2[dv:dv + 1])
    o = o_t.T
    o = o * lax.rsqrt(jnp.mean(o * o, axis=-1, keepdims=True) + RMS_EPS) * nw_ref[...]
    o_ref[...] = (o * (1.0 - lam_init)).astype(o_ref.dtype)


def _diff_attention(proj, bias, lam_params, norm_w, lam_init, bsz, seq, blk):
    t = proj.shape[0]
    nq = seq // blk
    nt = bias.shape[1]
    nb = nt - 1
    width = 2 * DIFF_DH
    ones_rows = 2 * SUBLANES
    return pl.pallas_call(
        functools.partial(_diff_body, blk, nb, lam_init),
        grid=(bsz, DIFF_HEADS, nq),
        in_specs=[pl.BlockSpec((blk, width), lambda b, h, i: (b * nq + i, OD_QC // width + h)),
                  pl.BlockSpec((blk, width),
                               lambda b, h, i: (b * nq + jnp.minimum(i + 1, nq - 1), OD_QC // width + h)),
                  pl.BlockSpec((seq, width), lambda b, h, i: (b, OD_KC // width + h)),
                  pl.BlockSpec((seq, width), lambda b, h, i: (b, OD_VC // width + h)),
                  pl.BlockSpec((1, nt, blk, blk), lambda b, h, i: (h, 0, 0, 0)),
                  pl.BlockSpec((4, DIFF_DH), lambda b, h, i: (0, 0)),
                  pl.BlockSpec((1, width), lambda b, h, i: (0, 0))],
        out_specs=pl.BlockSpec((blk, width), lambda b, h, i: (b * nq + i, h)),
        out_shape=jax.ShapeDtypeStruct((t, DIFF_HEADS * width), BF16),
        scratch_shapes=[pltpu.VMEM((width + ones_rows, seq), BF16),
                        pltpu.VMEM((width + ones_rows, blk), F32), pltpu.VMEM((width + ones_rows, blk), F32),
                        pltpu.VMEM((2, blk, blk), F32), pltpu.VMEM((2, blk, blk), F32)],
        compiler_params=_cparams(("arbitrary", "arbitrary", "arbitrary")),
        name="diff_attn",
    )(proj, proj, proj, proj, bias, lam_params, norm_w)


def _gla_body(q_ref, k_ref, gd_ref, v_ref, r_ref, wg_ref, bg_ref, nw_ref, o_ref, state, part):
    n = pl.program_id(1)

    @pl.when(n == 0)
    def _():
        state[...] = jnp.zeros_like(state)

    rows = GLA_GROUP
    nc = rows // CHUNK
    npair = GLA_HEADS // 2
    gate = jnp.dot(gd_ref[...], wg_ref[...], precision=HI, preferred_element_type=F32) + bg_ref[...]
    log_a = _log_sigmoid(gate) * (1.0 / GLA_TAU)

    ri = lax.broadcasted_iota(jnp.int32, (rows, rows), 0)
    ci = lax.broadcasted_iota(jnp.int32, (rows, rows), 1)
    causal = ((ri // CHUNK) == (ci // CHUNK)) & (ri >= ci)
    tri = causal.astype(BF16)
    hi = log_a.astype(BF16)
    rem = log_a - hi.astype(F32)
    mid = rem.astype(BF16)
    lo = (rem - mid.astype(F32)).astype(BF16)
    b_all = (jnp.dot(tri, hi, preferred_element_type=F32) + jnp.dot(tri, mid, preferred_element_type=F32)
             + jnp.dot(tri, lo, preferred_element_type=F32))

    lane = lax.broadcasted_iota(jnp.int32, (rows, LANES), 1)
    lane_c = lax.broadcasted_iota(jnp.int32, (CHUNK, LANES), 1)
    head_lanes = (lane < GLA_DK, lane >= GLA_DK)
    chunk_lanes = (lane_c < GLA_DK, lane_c >= GLA_DK)
    pairs = []
    for p in range(npair):
        cols = slice(p * LANES, (p + 1) * LANES)
        pairs.append(dict(
            b=b_all[:, cols],
            q=q_ref[:, cols].astype(F32) * GLA_DK ** -0.5,
            k=k_ref[:, cols].astype(F32),
            v=[v_ref[:, (2 * p + hd) * GLA_DV:(2 * p + hd + 1) * GLA_DV].astype(F32) for hd in range(2)]))

    for pr in pairs:
        pr["q_dec"] = pr["q"] * jnp.exp(pr["b"])
    for c in range(nc):
        sl = slice(c * CHUNK, (c + 1) * CHUNK)
        for p, pr in enumerate(pairs):
            bc = pr["b"][sl]
            b_last = bc[CHUNK - 1:CHUNK]
            k_dec = pr["k"][sl] * jnp.exp(b_last - bc)
            e_last = jnp.exp(b_last)
            for hd in range(2):
                h = 2 * p + hd
                st = state[h]
                part[h, sl, :] = _bdot_nt(jnp.where(chunk_lanes[hd], pr["q_dec"][sl], 0.0), st)
                state[h] = st * e_last + _bdot_tn(pr["v"][hd][sl], k_dec)

    def finish(intra):
        for h in range(GLA_HEADS):
            o = part[h] + intra[h]
            o = o * lax.rsqrt(jnp.mean(o * o, axis=-1, keepdims=True) + RMS_EPS) * nw_ref[...]
            gate_r = _silu(r_ref[:, h * GLA_DV:(h + 1) * GLA_DV].astype(F32))
            o_ref[:, h * GLA_DV:(h + 1) * GLA_DV] = (o * gate_r).astype(o_ref.dtype)

    def intra_whole_chunk():
        out = []
        for pr in pairs:
            k_inv = pr["k"] * jnp.exp(jnp.minimum(-pr["b"], GLA_MAX_DECAY))
            for hd in range(2):
                a = _bdot_nt(jnp.where(head_lanes[hd], pr["q_dec"], 0.0), k_inv)
                out.append(_bdot(jnp.where(causal, a, 0.0), pr["v"][hd]))
        return out

    def intra_exact():
        out = []
        for pr in pairs:
            off = _gla_intra_off_diagonal(pr["q"], pr["k"], pr["b"], pr["v"], nc)
            diag = _gla_intra_diagonal(pr["q"], pr["k"], pr["b"], pr["v"])
            out += [off[hd] + diag[hd] for hd in range(2)]
        return out

    chunk_decay = jnp.max(-b_all.reshape(nc, CHUNK, npair * LANES)[:, CHUNK - 1:CHUNK, :])

    @pl.when(chunk_decay <= GLA_MAX_DECAY)
    def _():
        finish(intra_whole_chunk())

    @pl.when(chunk_decay > GLA_MAX_DECAY)
    def _():
        finish(intra_exact())


def _gla_intra_off_diagonal(q, k, b, vs, nc):
    per_chunk = CHUNK // GLA_SUB
    lane = lax.broadcasted_iota(jnp.int32, (GLA_SUB, LANES), 1)
    sub_mask = (lane < GLA_DK, lane >= GLA_DK)
    kcol = lax.broadcasted_iota(jnp.int32, (GLA_SUB, CHUNK), 1)
    outs = ([], [])
    for c in range(nc):
        sl = slice(c * CHUNK, (c + 1) * CHUNK)
        bc, qc, kc = b[sl], q[sl], k[sl]
        a_rows = [[jnp.zeros((GLA_SUB, CHUNK), F32)] for _ in vs]
        for blk in range(1, per_chunk):
            r0 = blk * GLA_SUB
            bref = bc[r0:r0 + 1]
            qs = qc[r0:r0 + GLA_SUB] * jnp.exp(bc[r0:r0 + GLA_SUB] - bref)
            ks = kc * jnp.exp(jnp.minimum(bref - bc, 0.0))
            for hd in range(2):
                a = _bdot_nt(jnp.where(sub_mask[hd], qs, 0.0), ks)
                a_rows[hd].append(jnp.where(kcol < r0, a, 0.0))
        for hd in range(2):
            outs[hd].append(_bdot(jnp.concatenate(a_rows[hd], axis=0), vs[hd][sl]))
    return [jnp.concatenate(o, axis=0) for o in outs]


def _gla_intra_diagonal(q, k, b, vs):
    rows = q.shape[0]
    nsub = rows // GLA_SUB
    b3 = b.reshape(nsub, GLA_SUB, LANES)
    q3 = q.reshape(nsub, GLA_SUB, LANES)
    k3 = k.reshape(nsub, GLA_SUB, LANES)
    v3 = [v.reshape(nsub, GLA_SUB, LANES) for v in vs]
    row3 = lax.broadcasted_iota(jnp.int32, (nsub, GLA_SUB, LANES), 1)
    lane3 = lax.broadcasted_iota(jnp.int32, (nsub, GLA_SUB, LANES), 2)
    rowc = lax.broadcasted_iota(jnp.int32, (nsub, GLA_SUB, 1), 1)
    o3 = [jnp.zeros((nsub, GLA_SUB, LANES), F32) for _ in vs]
    for jj in range(GLA_SUB):
        e = jnp.exp(jnp.where(row3 >= jj, b3 - b3[:, jj:jj + 1, :], 0.0))
        t = q3 * k3[:, jj:jj + 1, :] * e
        w_all = jnp.sum(t, axis=-1, keepdims=True)
        w_a = jnp.sum(jnp.where(lane3 < GLA_DK, t, 0.0), axis=-1, keepdims=True)
        for hd, w in enumerate((w_a, w_all - w_a)):
            o3[hd] = o3[hd] + jnp.where(rowc >= jj, w, 0.0) * v3[hd][:, jj:jj + 1, :]
    return [o.reshape(rows, LANES) for o in o3]


def _gla(proj, gate_in, w_gate_pad, b_gate, norm_w, bsz, seq):
    t = proj.shape[0]
    rows = GLA_GROUP
    spb = seq // rows
    qk_w = GLA_HEADS * GLA_DK
    v_w = GLA_HEADS * GLA_DV

    def at(col):
        return lambda b, n: (b * spb + n, col)

    def whole(a):
        return pl.BlockSpec(a.shape, lambda b, n: (0, 0))

    return pl.pallas_call(
        _gla_body,
        grid=(bsz, spb),
        in_specs=[pl.BlockSpec((rows, qk_w), at(OD_QD // qk_w)),
                  pl.BlockSpec((rows, qk_w), at(OD_KD // qk_w)),
                  pl.BlockSpec((rows, LANES), at(0)),
                  pl.BlockSpec((rows, v_w), at(OD_VD // v_w)),
                  pl.BlockSpec((rows, v_w), at(OD_RD // v_w)),
                  whole(w_gate_pad), whole(b_gate), whole(norm_w)],
        out_specs=pl.BlockSpec((rows, v_w), at(0)),
        out_shape=jax.ShapeDtypeStruct((t, v_w), BF16),
        scratch_shapes=[pltpu.VMEM((GLA_HEADS, GLA_DV, LANES), F32), pltpu.VMEM((GLA_HEADS, rows, GLA_DV), F32)],
        compiler_params=_cparams(("parallel", "arbitrary")),
        name="gla",
    )(proj, proj, gate_in, proj, proj, w_gate_pad, b_gate, norm_w)


def _tail_body(tiles_per_seq, tc, ma_ref, ma_halo, mb_ref, mb_halo, h_ref, h_halo, woa_ref, wob_ref,
               g1_ref, b1_ref, wu_ref, cw_ref, cb_ref, wd_ref, g2_ref, b2_ref, y_ref,
               act, scratch_g, scratch_v):
    i = pl.program_id(0)

    def ext(halo_ref, ref):
        return jnp.concatenate([halo_ref[...], ref[...]], axis=0)

    mix = (jnp.dot(ext(ma_halo, ma_ref), woa_ref[...], preferred_element_type=F32)
           + jnp.dot(ext(mb_halo, mb_ref), wob_ref[...], preferred_element_type=F32))
    x_ext = _layer_norm(DEEPNORM_ALPHA * ext(h_halo, h_ref) + mix, g1_ref[...], b1_ref[...])
    x = x_ext[HALO:]
    row = lax.broadcasted_iota(jnp.int32, x_ext.shape, 0)
    first = i % tiles_per_seq == 0
    xb = jnp.where((row < HALO) & first, 0.0, x_ext).astype(BF16)

    def branch(lo, scratch):
        full = jnp.dot(xb, wu_ref[:, lo:lo + tc], preferred_element_type=F32)
        return _causal_conv(full, cw_ref[:, lo:lo + tc], scratch) + cb_ref[:, lo:lo + tc]

    for c in range(D_FF // tc):
        gate = branch(c * tc, scratch_g)
        val = branch(D_FF + c * tc, scratch_v)
        act[:, c * tc:(c + 1) * tc] = (_silu(gate) * val).astype(BF16)
    ffn = jnp.dot(act[...], wd_ref[...], preferred_element_type=F32)
    y_ref[...] = _layer_norm(DEEPNORM_ALPHA * x + ffn, g2_ref[...], b2_ref[...])


def _layer_tail(mix_a, mix_b, h, w_out_a, w_out_b, g1, b1, w_up, conv_w, conv_b, w_down, g2, b2,
                layer, seq):
    t = h.shape[0]
    tm, tc = TAIL_TM, FFN_TC
    hb = tm // HALO

    def tile(a):
        return pl.BlockSpec((tm, a.shape[1]), lambda i: (i, 0))

    def halo(a):
        return pl.BlockSpec((HALO, a.shape[1]), lambda i: (jnp.maximum(i * hb - 1, 0), 0))

    params = (w_out_a, w_out_b, g1, b1, w_up, conv_w, conv_b, w_down, g2, b2)
    param_specs = [_resident(w_out_a), _resident(w_out_b), _resident(g1), _resident(b1),
                   _resident(w_up, layer), _resident(conv_w), _resident(conv_b), _resident(w_down, layer),
                   _resident(g2), _resident(b2)]
    return pl.pallas_call(
        functools.partial(_tail_body, seq // tm, tc),
        grid=(t // tm,),
        in_specs=[tile(mix_a), halo(mix_a), tile(mix_b), halo(mix_b), tile(h), halo(h)] + param_specs,
        out_specs=pl.BlockSpec((tm, D_MODEL), lambda i: (i, 0)),
        out_shape=jax.ShapeDtypeStruct((t, D_MODEL), F32),
        scratch_shapes=[pltpu.VMEM((tm, D_FF), BF16), pltpu.VMEM((HALO + tm, tc), F32),
                        pltpu.VMEM((HALO + tm, tc), F32)],
        compiler_params=_cparams(("parallel",)),
        name="layer_tail",
    )(mix_a, mix_a, mix_b, mix_b, h, h, *params)


def _even_w_in(w):
    a_end = 4 * EVA_PART
    gates = w[:, a_end:a_end + 2 * GDN_HEADS]
    qkv_b = w[:, a_end + 2 * GDN_HEADS:]
    pad = jnp.zeros((w.shape[0], EVB_COLS - EVB_BA - 2 * GDN_HEADS), w.dtype)
    return w[:, :a_end].astype(BF16), jnp.concatenate([qkv_b, gates, pad], axis=1).astype(BF16)


def _odd_w_in(w):
    pad = jnp.zeros((w.shape[0], LANES - GLA_RANK), w.dtype)
    return w[:, :OD_COLS].astype(BF16), jnp.concatenate([w[:, OD_COLS:], pad], axis=1).astype(BF16)


def _even_mixer(h, rel_bias, w_in, conv_w, a_log, dt_bias, norm_w, w_out, bsz, seq):
    w_a, w_b = _even_w_in(w_in)
    proj_a, proj_b = _even_in_proj(h, w_a, w_b, conv_w, seq, PROJ_TM)
    gate_pad = jnp.zeros((1, LANES), F32)
    alog_pad = lax.dynamic_update_slice(gate_pad, a_log[None].astype(F32), (0, GDN_HEADS))
    dtb_pad = lax.dynamic_update_slice(gate_pad, dt_bias[None].astype(F32), (0, GDN_HEADS))
    o_a = _gdn(proj_a, proj_b, alog_pad, dtb_pad, norm_w[None], bsz, seq)
    tiles = _bias_tiles(rel_bias, *_swa_bias_tables(), scale=LOG2E)
    ng, two_blk = len(SWA_CONFIGS), 2 * SWA_BLOCK
    bias = tiles.reshape(SWA_HEADS // 2, 2, 2, ng, SWA_BLOCK, two_blk).transpose(3, 0, 2, 1, 4, 5)
    o_b = _swa(proj_b, bias.reshape(ng, SWA_HEADS // 2, 2, two_blk, two_blk), bsz, seq)
    w_out = w_out.astype(BF16)
    return o_a, o_b, w_out[:EVA_PART], w_out[EVA_PART:]


def _odd_mixer(h, rel_bias, w_in, lam_params, diff_norm_w, w_gate, b_gate, gla_norm_w, w_out,
               lam_init, bsz, seq):
    w_main, w_gd = _odd_w_in(w_in)
    proj, gate_in = _odd_in_proj(h, w_main, w_gd, PROJ_TM)
    blk = min(DIFF_BLOCK, seq)
    bucket, neg = _diff_bias_tables(blk)
    bias = _bias_tiles(rel_bias, bucket, neg, scale=LOG2E, base_bucket=NUM_BUCKETS - 1)
    o_c = _diff_attention(proj, bias, lam_params, diff_norm_w[None], lam_init, bsz, seq, blk)
    w_gate_pad = jnp.concatenate(
        [w_gate, jnp.zeros((LANES - GLA_RANK, w_gate.shape[1]), w_gate.dtype)], axis=0)
    o_d = _gla(proj, gate_in, w_gate_pad, b_gate[None], gla_norm_w[None], bsz, seq)
    diff_v = DIFF_HEADS * 2 * DIFF_DH
    w_out = w_out.astype(BF16)
    return o_c, o_d, w_out[:diff_v], w_out[diff_v:]


def kernel(x, rel_bias, w_in_even, gdn_conv_w, gdn_a_log, gdn_dt_bias, gdn_norm_w, w_out_even,
           w_in_odd, diff_lambda, diff_norm_w, gla_w_gate, gla_b_gate, gla_norm_w, w_out_odd,
           ffn_w_up, ffn_conv_w, ffn_conv_b, ffn_w_down, ln_g, ln_b):
    bsz, seq, d = x.shape
    h = x.reshape(bsz * seq, d)
    w_up, w_down = ffn_w_up.astype(BF16), ffn_w_down.astype(BF16)
    for layer in range(DEPTH):
        i = layer // 2
        if layer % 2 == 0:
            mixed = _even_mixer(h, rel_bias, w_in_even[i], gdn_conv_w[i], gdn_a_log[i], gdn_dt_bias[i],
                                gdn_norm_w[i], w_out_even[i], bsz, seq)
        else:
            lam_init = 0.8 - 0.6 * math.exp(-0.3 * layer)
            mixed = _odd_mixer(h, rel_bias, w_in_odd[i], diff_lambda[i], diff_norm_w[i], gla_w_gate[i],
                               gla_b_gate[i], gla_norm_w[i], w_out_odd[i], lam_init, bsz, seq)
        h = _layer_tail(*mixed[:2], h, *mixed[2:], ln_g[layer, 0][None], ln_b[layer, 0][None],
                        w_up, ffn_conv_w[layer], ffn_conv_b[layer][None], w_down,
                        ln_g[layer, 1][None], ln_b[layer, 1][None], layer, seq)
    return h.reshape(bsz, seq, d)
```

```python
import functools
import math

import numpy as np
import jax
import jax.numpy as jnp
from jax import lax
from jax.experimental import pallas as pl
from jax.experimental.pallas import tpu as pltpu

F32 = jnp.float32
BF16 = jnp.bfloat16
HI = lax.Precision.HIGHEST

D_MODEL = 1024
DEPTH = 2
DEEPNORM_ALPHA = (2 * DEPTH) ** 0.25
LN_EPS = 1e-5
RMS_EPS = 1e-6
NUM_BUCKETS = 32
REL_MAX_DIST = 2048
GDN_HEADS = 6
GDN_D = 128
CHUNK = 64
GDN_GROUP = 256
GDN_STEP_GROUPS = 1
SWA_CONFIGS = ((128, 1), (512, 4), (2048, 16))
SWA_HEADS = 4
SWA_DH = 64
SWA_BLOCK = 128
SWA_UNROLL = 8
SWA_SPAN = SWA_BLOCK * max(d for _, d in SWA_CONFIGS)
DIFF_HEADS = 4
DIFF_DH = 64
DIFF_BLOCK = 512
LOG2E = math.log2(math.e)
GLA_HEADS = 4
GLA_DK = 64
GLA_DV = 128
GLA_RANK = 16
GLA_TAU = 16.0
GLA_SUB = 16
GLA_GROUP = 256
GLA_MAX_DECAY = 60.0
D_FF = 2816

LANES = 128
SUBLANES = 8
HALO = 8
VMEM_LIMIT = 56 * 1024 * 1024
NEG_BIG = -1e30

EVA_PART = GDN_HEADS * GDN_D
EVB_BA = 2304
EVB_COLS = 2560
EVB_TN = 512
OD_QC, OD_KC, OD_VC = 0, 512, 1024
OD_QD, OD_KD, OD_VD, OD_RD = 1536, 1792, 2048, 2560
OD_COLS = 3072
OD_TN = 768
PROJ_TM = 512
TAIL_TM = 512
FFN_TC = 256


def _cparams(sem):
    return pltpu.CompilerParams(dimension_semantics=sem, vmem_limit_bytes=VMEM_LIMIT)


def _bdot(a, b):
    return jnp.dot(a.astype(BF16), b.astype(BF16), preferred_element_type=F32)


def _bdot_nt(a, b):
    return lax.dot_general(a.astype(BF16), b.astype(BF16), (((1,), (1,)), ((), ())),
                           preferred_element_type=F32)


def _bdot_tn(a, b):
    return lax.dot_general(a.astype(BF16), b.astype(BF16), (((0,), (0,)), ((), ())),
                           preferred_element_type=F32)


def _sigmoid(x):
    return 1.0 / (1.0 + jnp.exp(-x))


def _silu(x):
    return x * _sigmoid(x)


def _softplus(x):
    return jnp.maximum(x, 0.0) + jnp.log1p(jnp.exp(-jnp.abs(x)))


def _log_sigmoid(x):
    return -_softplus(-x)


def _resident(a, lead=None):
    if lead is None:
        return pl.BlockSpec(a.shape, lambda i: (0,) * a.ndim, pipeline_mode=pl.Buffered(1))
    return pl.BlockSpec((None,) + a.shape[1:], lambda i: (lead,) + (0,) * (a.ndim - 1),
                        pipeline_mode=pl.Buffered(1))


def _odd_in_body(x_ref, w_ref, wg_ref, o_ref, og_ref):
    xb = x_ref[...].astype(BF16)
    for c in range(OD_COLS // OD_TN):
        cols = slice(c * OD_TN, (c + 1) * OD_TN)
        o_ref[:, cols] = jnp.dot(xb, w_ref[:, cols], preferred_element_type=F32).astype(o_ref.dtype)
    og_ref[...] = jnp.dot(xb, wg_ref[...], preferred_element_type=F32)


def _odd_in_proj(x, w, w_gate_in, tm):
    t, k = x.shape
    return pl.pallas_call(
        _odd_in_body,
        grid=(t // tm,),
        in_specs=[pl.BlockSpec((tm, k), lambda i: (i, 0)), _resident(w), _resident(w_gate_in)],
        out_specs=[pl.BlockSpec((tm, OD_COLS), lambda i: (i, 0)), pl.BlockSpec((tm, LANES), lambda i: (i, 0))],
        out_shape=[jax.ShapeDtypeStruct((t, OD_COLS), BF16), jax.ShapeDtypeStruct((t, LANES), F32)],
        compiler_params=_cparams(("parallel",)),
        name="odd_in_proj",
    )(x, w, w_gate_in)


def _rel_bucket_np(dist):
    max_exact = NUM_BUCKETS // 2
    d = np.maximum(dist, 1).astype(np.float32)
    large = max_exact + (np.log(d / max_exact) / math.log(REL_MAX_DIST / max_exact)
                         * (NUM_BUCKETS - max_exact)).astype(np.int32)
    large = np.minimum(large, NUM_BUCKETS - 1)
    return np.where(dist < max_exact, dist, large).astype(np.int32)


def _bias_body(scale, base_bucket, tile_buckets, rb_ref, bucket_ref, neg_ref, o_ref):
    h = pl.program_id(0)
    t = pl.program_id(1)
    base = 0.0 if base_bucket is None else rb_ref[base_bucket, h]
    for tile, present in enumerate(tile_buckets):
        @pl.when(t == tile)
        def _(present=present):
            bucket = bucket_ref[0]
            acc = neg_ref[0]
            for b in present:
                acc = acc + jnp.where(bucket == b, (rb_ref[b, h] - base) * scale, 0.0)
            o_ref[0, 0] = acc


def _bias_tiles(rel_bias, bucket, neg, scale=1.0, base_bucket=None):
    nt, r, c = bucket.shape
    nh = rel_bias.shape[1]
    tile_buckets = tuple(tuple(int(b) for b in np.unique(bucket[t][neg[t] == 0]) if b != base_bucket)
                         for t in range(nt))
    return pl.pallas_call(
        functools.partial(_bias_body, scale, base_bucket, tile_buckets),
        grid=(nh, nt),
        in_specs=[pl.BlockSpec(memory_space=pltpu.SMEM),
                  pl.BlockSpec((1, r, c), lambda h, t: (t, 0, 0)),
                  pl.BlockSpec((1, r, c), lambda h, t: (t, 0, 0))],
        out_specs=pl.BlockSpec((1, 1, r, c), lambda h, t: (h, t, 0, 0)),
        out_shape=jax.ShapeDtypeStruct((nh, nt, r, c), F32),
        compiler_params=_cparams(("parallel", "parallel")),
        name="rel_bias_tiles",
    )(rel_bias, jnp.asarray(bucket), jnp.asarray(neg))


def _causal_conv(cur, w, scratch, carry, at_start):
    width = w.shape[0]
    rows = cur.shape[0]
    scratch[0:HALO, :] = jnp.where(at_start, 0.0, carry[...])
    scratch[HALO:, :] = cur
    carry[...] = cur[rows - HALO:]
    y = w[width - 1:width, :] * cur
    for j in range(width - 1):
        back = width - 1 - j
        y = y + w[j:j + 1, :] * scratch[HALO - back:HALO - back + rows, :]
    return y


def _even_in_body(tiles_per_seq, x_ref, wa_ref, wb_ref, cw_ref, oa_ref, ob_ref, scratch, carry):
    at_start = pl.program_id(0) % tiles_per_seq == 0
    x = x_ref[...].astype(BF16)
    for part in range(3):
        cols = slice(part * EVA_PART, (part + 1) * EVA_PART)
        y = jnp.dot(x, wa_ref[:, cols], preferred_element_type=F32)
        c = _silu(_causal_conv(y, cw_ref[:, cols], scratch, carry.at[part], at_start))
        for hd in range(GDN_HEADS):
            ch = c[:, hd * GDN_D:(hd + 1) * GDN_D]
            if part < 2:
                inv = lax.rsqrt(jnp.sum(ch * ch, axis=-1, keepdims=True) + RMS_EPS)
                ch = ch * (inv * GDN_D ** -0.5 if part == 0 else inv)
            lo = part * EVA_PART + hd * GDN_D
            oa_ref[:, lo:lo + GDN_D] = ch.astype(oa_ref.dtype)
    z_cols = slice(3 * EVA_PART, 4 * EVA_PART)
    oa_ref[:, z_cols] = jnp.dot(x, wa_ref[:, z_cols], preferred_element_type=F32).astype(oa_ref.dtype)
    for c in range(EVB_COLS // EVB_TN):
        cols = slice(c * EVB_TN, (c + 1) * EVB_TN)
        ob_ref[:, cols] = jnp.dot(x, wb_ref[:, cols], preferred_element_type=F32)


def _even_in_proj(x, w_a, w_b, conv_w, seq, tm):
    t, k = x.shape
    return pl.pallas_call(
        functools.partial(_even_in_body, seq // tm),
        grid=(t // tm,),
        in_specs=[pl.BlockSpec((tm, k), lambda i: (i, 0)),
                  _resident(w_a), _resident(w_b), _resident(conv_w)],
        out_specs=[pl.BlockSpec((tm, w_a.shape[1]), lambda i: (i, 0)),
                   pl.BlockSpec((tm, EVB_COLS), lambda i: (i, 0))],
        out_shape=[jax.ShapeDtypeStruct((t, w_a.shape[1]), BF16), jax.ShapeDtypeStruct((t, EVB_COLS), F32)],
        scratch_shapes=[pltpu.VMEM((HALO + tm, EVA_PART), F32), pltpu.VMEM((3, HALO, EVA_PART), F32)],
        compiler_params=_cparams(("arbitrary",)),
        name="even_in_proj",
    )(x, w_a, w_b, conv_w)


def _gdn_body(q_ref, k_ref, v_ref, z_ref, ba_ref, alog_ref, dtb_ref, nw_ref, o_ref, state):
    n = pl.program_id(1)

    @pl.when(n == 0)
    def _():
        state[...] = jnp.zeros_like(state)

    grp = GDN_GROUP
    nc = grp // CHUNK
    nh = GDN_HEADS
    hs = range(nh)
    ri = lax.broadcasted_iota(jnp.int32, (grp, grp), 0)
    ci = lax.broadcasted_iota(jnp.int32, (grp, grp), 1)
    same = (ri // CHUNK) == (ci // CHUNK)
    incl = same & (ri >= ci)
    strict = same & (ri > ci)
    eye = (ri == ci).astype(F32)
    tri = incl.astype(BF16)

    def chunk_local(rows):
        ba = ba_ref[rows, :]
        beta_all = _sigmoid(ba)
        g_all = -jnp.exp(alog_ref[...]) * _softplus(ba + dtb_ref[...])
        g_hi = g_all.astype(BF16)
        rem = g_all - g_hi.astype(F32)
        g_mid = rem.astype(BF16)
        g_lo = (rem - g_mid.astype(F32)).astype(BF16)
        gam_all = (jnp.dot(tri, g_hi, preferred_element_type=F32)
                   + jnp.dot(tri, g_mid, preferred_element_type=F32)
                   + jnp.dot(tri, g_lo, preferred_element_type=F32))
        gam_rows = gam_all.T

        q = [q_ref[rows, h * GDN_D:(h + 1) * GDN_D] for h in hs]
        k = [k_ref[rows, h * GDN_D:(h + 1) * GDN_D] for h in hs]
        v = [v_ref[rows, h * GDN_D:(h + 1) * GDN_D] for h in hs]
        gam = [jnp.broadcast_to(gam_all[:, nh + h:nh + h + 1], (grp, GDN_D)) for h in hs]
        bcol = [beta_all[:, h:h + 1] for h in hs]
        decay, x, inv = [], [], []
        for h in hs:
            diff = gam[h][:, 0:1] - gam_rows[nh + h:nh + h + 1, :]
            decay.append(jnp.where(incl, jnp.exp(jnp.where(incl, diff, 0.0)), 0.0))
            kk = _bdot_nt(k[h], k[h])
            x.append(-jnp.where(strict, bcol[h] * kk * decay[h], 0.0))
            inv.append(eye + x[h])
        for _ in range(5):
            for h in hs:
                x[h] = _bdot(x[h], x[h])
                inv[h] = inv[h] + _bdot(inv[h], x[h])
        u, w, qk, q_dec, k_dec, g_last = [], [], [], [], [], []
        for h in hs:
            eg = jnp.exp(gam[h])
            uw = _bdot(inv[h], jnp.concatenate([v[h] * bcol[h], k[h] * (bcol[h] * eg)], axis=1))
            u.append(uw[:, :GDN_D])
            w.append(uw[:, GDN_D:])
            qk.append(_bdot_nt(q[h], k[h]) * decay[h])
            q_dec.append(q[h] * eg)
            kd, gl = [], []
            for c in range(nc):
                last = gam[h][(c + 1) * CHUNK - 1:(c + 1) * CHUNK, :]
                kd.append(k[h][c * CHUNK:(c + 1) * CHUNK] * jnp.exp(last - gam[h][c * CHUNK:(c + 1) * CHUNK]))
                gl.append(jnp.exp(last))
            k_dec.append(kd)
            g_last.append(gl)
        return u, w, qk, q_dec, k_dec, g_last

    def recurrence(rows, local, s):
        u, w, qk, q_dec, k_dec, g_last = local
        q_s = [[] for _ in hs]
        delta = [[] for _ in hs]
        for c in range(nc):
            sl = slice(c * CHUNK, (c + 1) * CHUNK)
            for h in hs:
                r = _bdot(jnp.concatenate([w[h][sl], q_dec[h][sl]], axis=0), s[h])
                d = u[h][sl] - r[:CHUNK]
                q_s[h].append(r[CHUNK:])
                delta[h].append(d)
                s[h] = g_last[h][c] * s[h] + _bdot_tn(k_dec[h][c], d)
        for h in hs:
            o = jnp.concatenate(q_s[h], axis=0) + _bdot(qk[h], jnp.concatenate(delta[h], axis=0))
            o = o * lax.rsqrt(jnp.mean(o * o, axis=-1, keepdims=True) + RMS_EPS) * nw_ref[...]
            z = z_ref[rows, h * GDN_D:(h + 1) * GDN_D].astype(F32)
            o_ref[rows, h * GDN_D:(h + 1) * GDN_D] = (o * _silu(z)).astype(o_ref.dtype)
        return s

    groups = [slice(g * grp, (g + 1) * grp) for g in range(GDN_STEP_GROUPS)]
    local = [chunk_local(rows) for rows in groups]
    s = [state[h] for h in hs]
    for rows, loc in zip(groups, local):
        s = recurrence(rows, loc, s)
    for h in hs:
        state[h] = s[h]


def _gdn(proj_a, proj_b, alog_pad, dtb_pad, norm_w, bsz, seq):
    t = proj_a.shape[0]
    rows = GDN_GROUP * GDN_STEP_GROUPS
    spb = seq // rows
    width = GDN_HEADS * GDN_D

    def at(col):
        return lambda b, n: (b * spb + n, col)

    return pl.pallas_call(
        _gdn_body,
        grid=(bsz, spb),
        in_specs=[pl.BlockSpec((rows, width), at(0)),
                  pl.BlockSpec((rows, width), at(1)),
                  pl.BlockSpec((rows, width), at(2)),
                  pl.BlockSpec((rows, width), at(3)),
                  pl.BlockSpec((rows, LANES), at(EVB_BA // LANES)),
                  pl.BlockSpec((1, LANES), lambda b, n: (0, 0)),
                  pl.BlockSpec((1, LANES), lambda b, n: (0, 0)),
                  pl.BlockSpec((1, LANES), lambda b, n: (0, 0))],
        out_specs=pl.BlockSpec((rows, width), at(0)),
        out_shape=jax.ShapeDtypeStruct((t, width), BF16),
        scratch_shapes=[pltpu.VMEM((GDN_HEADS, GDN_D, GDN_D), F32)],
        compiler_params=_cparams(("parallel", "arbitrary")),
        name="gdn",
    )(proj_a, proj_a, proj_a, proj_a, proj_b, alog_pad, dtb_pad, norm_w)


def _swa_bias_tables():
    qi = np.arange(SWA_BLOCK)[:, None] + SWA_BLOCK
    kj = np.arange(2 * SWA_BLOCK)[None, :]
    rel = qi - kj
    buckets, negs = [], []
    for has_prev in (True, False):
        for window, dilation in SWA_CONFIGS:
            valid = (rel >= 0) & (rel <= window // dilation) & (has_prev | (kj >= SWA_BLOCK))
            buckets.append(_rel_bucket_np(np.maximum(rel, 0) * dilation))
            negs.append(np.where(valid, 0.0, NEG_BIG))
    return np.stack(buckets).astype(np.int32), np.stack(negs).astype(np.float32)


def _swa_body(*refs):
    ng = len(SWA_CONFIGS)
    ins, bias_ref, o_ref = refs[:5 * ng], refs[5 * ng], refs[5 * ng + 1]
    scratch = refs[5 * ng + 2:]
    kbufs, vbufs, o_scr, lse_scr = scratch[:ng], scratch[ng:2 * ng], scratch[2 * ng], scratch[2 * ng + 1]
    j = pl.program_id(1)
    blk = SWA_BLOCK
    first_head = lax.broadcasted_iota(jnp.int32, (blk, LANES), 1) < SWA_DH

    for g, (_, d) in enumerate(SWA_CONFIGS):
        q_ref, kc_ref, kp_ref, vc_ref, vp_ref = ins[5 * g:5 * g + 5]
        kbuf, vbuf = kbufs[g], vbufs[g]
        halo = blk * d
        kbuf[0:halo, :] = kp_ref[...]
        kbuf[halo:, :] = kc_ref[...]
        vbuf[0:halo, :] = vp_ref[...]
        vbuf[halo:, :] = vc_ref[...]

        def unit(u, carry, g=g, d=d, halo=halo, q_ref=q_ref, kbuf=kbuf, vbuf=vbuf):
            base = (u // d) * halo + u % d
            q = q_ref[pl.ds(base, blk, stride=d), :] * (SWA_DH ** -0.5 * LOG2E)
            k = kbuf[pl.ds(base, 2 * blk, stride=d), :]
            v = vbuf[pl.ds(base, 2 * blk, stride=d), :]
            lhs = jnp.concatenate([jnp.where(first_head, q, 0.0), jnp.where(first_head, 0.0, q)], axis=0)
            no_prev = jnp.where((j == 0) & (u < d), 1, 0)
            s = _bdot_nt(lhs, k) + bias_ref[g, 0, no_prev]
            m = jnp.max(s, axis=-1, keepdims=True)
            p = jnp.exp2(s - m)
            l = jnp.sum(p, axis=-1, keepdims=True)
            o2 = _bdot(p, v) * (1.0 / l)
            lse2 = m + jnp.log2(l)
            o_scr[g, pl.ds(base, blk, stride=d), :] = jnp.where(first_head, o2[:blk], o2[blk:])
            lse_scr[g, pl.ds(base, blk, stride=d), :] = jnp.where(first_head, lse2[:blk], lse2[blk:])
            return carry

        lax.fori_loop(0, SWA_SPAN // blk, unit, 0, unroll=SWA_UNROLL)

    def combine(c, carry):
        rows = pl.ds(pl.multiple_of(c * 2 * blk, 2 * blk), 2 * blk)
        lse = [lse_scr[g, rows, :] for g in range(ng)]
        m = functools.reduce(jnp.maximum, lse)
        e = [jnp.exp2(x - m) for x in lse]
        den = functools.reduce(lambda x, y: x + y, e)
        o_ref[rows, :] = functools.reduce(
            lambda x, y: x + y, [(e[g] / den) * o_scr[g, rows, :] for g in range(ng)]).astype(o_ref.dtype)
        return carry

    lax.fori_loop(0, SWA_SPAN // (2 * blk), combine, 0)


def _swa(proj, bias, bsz, seq):
    t = proj.shape[0]
    ng = len(SWA_CONFIGS)
    nspan = seq // SWA_SPAN
    npair = SWA_HEADS * SWA_DH // LANES
    group_cols = SWA_HEADS * SWA_DH // LANES
    in_specs, scratch_k = [], []
    for g, (_, d) in enumerate(SWA_CONFIGS):
        halo = SWA_BLOCK * d
        per_span = SWA_SPAN // halo

        def cur(which, g=g):
            col = (which * ng + g) * group_cols
            return lambda b, j, p: (b * nspan + j, col + p)

        def prev(which, g=g, per_span=per_span):
            col = (which * ng + g) * group_cols
            return lambda b, j, p: (jnp.maximum((b * nspan + j) * per_span - 1, 0), col + p)

        in_specs += [pl.BlockSpec((SWA_SPAN, LANES), cur(0)),
                     pl.BlockSpec((SWA_SPAN, LANES), cur(1)), pl.BlockSpec((halo, LANES), prev(1)),
                     pl.BlockSpec((SWA_SPAN, LANES), cur(2)), pl.BlockSpec((halo, LANES), prev(2))]
        scratch_k.append(pltpu.VMEM((halo + SWA_SPAN, LANES), F32))
    in_specs.append(pl.BlockSpec((ng, 1, 2, 2 * SWA_BLOCK, 2 * SWA_BLOCK), lambda b, j, p: (0, p, 0, 0, 0)))
    return pl.pallas_call(
        _swa_body,
        grid=(bsz, nspan, npair),
        in_specs=in_specs,
        out_specs=pl.BlockSpec((SWA_SPAN, LANES), lambda b, j, p: (b * nspan + j, p)),
        out_shape=jax.ShapeDtypeStruct((t, npair * LANES), BF16),
        scratch_shapes=scratch_k + scratch_k + [pltpu.VMEM((ng, SWA_SPAN, LANES), F32)] * 2,
        compiler_params=_cparams(("parallel", "parallel", "parallel")),
        name="swa",
    )(*([proj] * (5 * ng)), bias)


def _layer_norm(y, g, b):
    mu = jnp.mean(y, axis=-1, keepdims=True)
    yc = y - mu
    var = jnp.mean(yc * yc, axis=-1, keepdims=True)
    return yc * lax.rsqrt(var + LN_EPS) * g + b


def _diff_bias_tables(blk):
    buckets = _rel_bucket_np(np.arange(2 * REL_MAX_DIST))
    far = int(np.max(np.nonzero(buckets != NUM_BUCKETS - 1)[0])) + 1
    nb = -(-(far + blk - 1) // blk)
    kk = np.arange(blk)[:, None]
    qq = np.arange(blk)[None, :]
    dist = np.stack([t * blk + qq - kk for t in range(nb + 1)])
    bucket = _rel_bucket_np(np.maximum(dist, 0))
    neg = np.where(dist >= 0, 0.0, NEG_BIG).astype(np.float32)
    return bucket, neg


def _diff_body(blk, nb, lam_init, q_ref, qn_ref, k_ref, v_ref, bias_ref, lam_ref, nw_ref, o_ref,
               vt, acc1, acc2, s_a, s_b):
    qi = pl.program_id(2)
    dv = 2 * DIFF_DH
    seq = k_ref.shape[0]

    @pl.when(qi == 0)
    def _():
        vt[dv:, :] = jnp.ones((vt.shape[0] - dv, seq), BF16)

        def fill(c, carry):
            st = pl.multiple_of(c * blk, blk)
            vt[0:dv, pl.ds(st, blk)] = v_ref[pl.ds(st, blk), :].astype(F32).T.astype(BF16)
            return carry

        lax.fori_loop(0, seq // blk, fill, 0)

    feature = lax.broadcasted_iota(jnp.int32, (dv, blk), 0)

    def components(ref):
        q_t = (ref[...].astype(F32) * (DIFF_DH ** -0.5 * LOG2E)).T
        return (jnp.where(feature < DIFF_DH, q_t, 0.0).astype(BF16),
                jnp.where(feature >= DIFF_DH, q_t, 0.0).astype(BF16))

    qs = components(q_ref)
    qs_next = components(qn_ref)
    accs = (acc1, acc2)
    for acc in accs:
        acc[...] = jnp.zeros_like(acc)

    last = pl.num_programs(2) - 1

    def key_rows(kj):
        return pl.ds(pl.multiple_of(jnp.minimum(kj, last) * blk, blk), blk)

    def scores(kj, dst, queries=qs):
        k = k_ref[key_rows(kj), :]
        for c, qc in enumerate(queries):
            dst[c] = jnp.dot(k, qc, preferred_element_type=F32)

    def consume(biased, kj, src, ms):
        vtb = vt[:, key_rows(kj)]
        out = []
        for c, (m, acc) in enumerate(zip(ms, accs)):
            s = src[c]
            if biased:
                s = s + bias_ref[0, jnp.minimum(qi - kj, nb)]
            m_new = jnp.maximum(m, jnp.max(s, axis=0, keepdims=True))
            p = jnp.exp2(s - m_new).astype(BF16)
            acc[...] = jnp.exp2(m - m_new) * acc[...] + jnp.dot(vtb, p, preferred_element_type=F32)
            out.append(m_new)
        return tuple(out)

    nblocks = qi + 1
    trips = nblocks // 2
    odd = nblocks % 2 == 1
    far_trips = jnp.maximum(qi - nb + 1, 0) // 2

    def pair(biased, t, ms):
        kj = 2 * t
        scores(kj + 1, s_b)
        ms = consume(biased, kj, s_a, ms)
        scores(kj + 2, s_a)
        return consume(biased, kj + 1, s_b, ms)

    @pl.when(qi == 0)
    def _():
        scores(0, s_a)

    m0 = jnp.full((1, blk), NEG_BIG, F32)
    ms = lax.fori_loop(0, far_trips, functools.partial(pair, False), (m0, m0))
    ms = lax.fori_loop(far_trips, trips, functools.partial(pair, True), ms)

    @pl.when(odd)
    def _():
        consume(True, qi, s_a, ms)

    scores(0, s_a, qs_next)
    lp = lam_ref[...]
    lam = (jnp.exp(jnp.sum(lp[0:1] * lp[1:2], axis=-1, keepdims=True))
           - jnp.exp(jnp.sum(lp[2:3] * lp[3:4], axis=-1, keepdims=True)) + lam_init)
    a1, a2 = acc1[...], acc2[...]
    o_t = a1[:dv] * (1.0 / a1[dv:dv + 1]) - a2[:dv] * (lam / a2[dv:dv + 1])
    o = o_t.T
    o = o * lax.rsqrt(jnp.mean(o * o, axis=-1, keepdims=True) + RMS_EPS) * nw_ref[...]
    o_ref[...] = (o * (1.0 - lam_init)).astype(o_ref.dtype)


def _diff_attention(proj, bias, lam_params, norm_w, lam_init, bsz, seq, blk):
    t = proj.shape[0]
    nq = seq // blk
    nt = bias.shape[1]
    nb = nt - 1
    width = 2 * DIFF_DH
    ones_rows = 2 * SUBLANES
    return pl.pallas_call(
        functools.partial(_diff_body, blk, nb, lam_init),
        grid=(bsz, DIFF_HEADS, nq),
        in_specs=[pl.BlockSpec((blk, width), lambda b, h, i: (b * nq + i, OD_QC // width + h)),
                  pl.BlockSpec((blk, width),
                               lambda b, h, i: (b * nq + jnp.minimum(i + 1, nq - 1), OD_QC // width + h)),
                  pl.BlockSpec((seq, width), lambda b, h, i: (b, OD_KC // width + h)),
                  pl.BlockSpec((seq, width), lambda b, h, i: (b, OD_VC // width + h)),
                  pl.BlockSpec((1, nt, blk, blk), lambda b, h, i: (h, 0, 0, 0)),
                  pl.BlockSpec((4, DIFF_DH), lambda b, h, i: (0, 0)),
                  pl.BlockSpec((1, width), lambda b, h, i: (0, 0))],
        out_specs=pl.BlockSpec((blk, width), lambda b, h, i: (b * nq + i, h)),
        out_shape=jax.ShapeDtypeStruct((t, DIFF_HEADS * width), BF16),
        scratch_shapes=[pltpu.VMEM((width + ones_rows, seq), BF16),
                        pltpu.VMEM((width + ones_rows, blk), F32), pltpu.VMEM((width + ones_rows, blk), F32),
                        pltpu.VMEM((2, blk, blk), F32), pltpu.VMEM((2, blk, blk), F32)],
        compiler_params=_cparams(("arbitrary", "arbitrary", "arbitrary")),
        name="diff_attn",
    )(proj, proj, proj, proj, bias, lam_params, norm_w)


def _gla_body(q_ref, k_ref, gd_ref, v_ref, r_ref, wg_ref, bg_ref, nw_ref, o_ref, state, part):
    n = pl.program_id(1)

    @pl.when(n == 0)
    def _():
        state[...] = jnp.zeros_like(state)

    rows = GLA_GROUP
    nc = rows // CHUNK
    npair = GLA_HEADS // 2
    gate = jnp.dot(gd_ref[...], wg_ref[...], precision=HI, preferred_element_type=F32) + bg_ref[...]
    log_a = _log_sigmoid(gate) * (1.0 / GLA_TAU)

    ri = lax.broadcasted_iota(jnp.int32, (rows, rows), 0)
    ci = lax.broadcasted_iota(jnp.int32, (rows, rows), 1)
    causal = ((ri // CHUNK) == (ci // CHUNK)) & (ri >= ci)
    tri = causal.astype(BF16)
    hi = log_a.astype(BF16)
    rem = log_a - hi.astype(F32)
    mid = rem.astype(BF16)
    lo = (rem - mid.astype(F32)).astype(BF16)
    b_all = (jnp.dot(tri, hi, preferred_element_type=F32) + jnp.dot(tri, mid, preferred_element_type=F32)
             + jnp.dot(tri, lo, preferred_element_type=F32))

    lane = lax.broadcasted_iota(jnp.int32, (rows, LANES), 1)
    lane_c = lax.broadcasted_iota(jnp.int32, (CHUNK, LANES), 1)
    head_lanes = (lane < GLA_DK, lane >= GLA_DK)
    chunk_lanes = (lane_c < GLA_DK, lane_c >= GLA_DK)
    pairs = []
    for p in range(npair):
        cols = slice(p * LANES, (p + 1) * LANES)
        pairs.append(dict(
            b=b_all[:, cols],
            q=q_ref[:, cols].astype(F32) * GLA_DK ** -0.5,
            k=k_ref[:, cols].astype(F32),
            v=[v_ref[:, (2 * p + hd) * GLA_DV:(2 * p + hd + 1) * GLA_DV].astype(F32) for hd in range(2)]))

    for pr in pairs:
        pr["q_dec"] = pr["q"] * jnp.exp(pr["b"])
    for c in range(nc):
        sl = slice(c * CHUNK, (c + 1) * CHUNK)
        for p, pr in enumerate(pairs):
            bc = pr["b"][sl]
            b_last = bc[CHUNK - 1:CHUNK]
            k_dec = pr["k"][sl] * jnp.exp(b_last - bc)
            e_last = jnp.exp(b_last)
            for hd in range(2):
                h = 2 * p + hd
                st = state[h]
                part[h, sl, :] = _bdot_nt(jnp.where(chunk_lanes[hd], pr["q_dec"][sl], 0.0), st)
                state[h] = st * e_last + _bdot_tn(pr["v"][hd][sl], k_dec)

    def finish(intra):
        for h in range(GLA_HEADS):
            o = part[h] + intra[h]
            o = o * lax.rsqrt(jnp.mean(o * o, axis=-1, keepdims=True) + RMS_EPS) * nw_ref[...]
            gate_r = _silu(r_ref[:, h * GLA_DV:(h + 1) * GLA_DV].astype(F32))
            o_ref[:, h * GLA_DV:(h + 1) * GLA_DV] = (o * gate_r).astype(o_ref.dtype)

    def intra_whole_chunk():
        out = []
        for pr in pairs:
            k_inv = pr["k"] * jnp.exp(jnp.minimum(-pr["b"], GLA_MAX_DECAY))
            for hd in range(2):
                a = _bdot_nt(jnp.where(head_lanes[hd], pr["q_dec"], 0.0), k_inv)
                out.append(_bdot(jnp.where(causal, a, 0.0), pr["v"][hd]))
        return out

    def intra_exact():
        out = []
        for pr in pairs:
            off = _gla_intra_off_diagonal(pr["q"], pr["k"], pr["b"], pr["v"], nc)
            diag = _gla_intra_diagonal(pr["q"], pr["k"], pr["b"], pr["v"])
            out += [off[hd] + diag[hd] for hd in range(2)]
        return out

    chunk_decay = jnp.max(-b_all.reshape(nc, CHUNK, npair * LANES)[:, CHUNK - 1:CHUNK, :])

    @pl.when(chunk_decay <= GLA_MAX_DECAY)
    def _():
        finish(intra_whole_chunk())

    @pl.when(chunk_decay > GLA_MAX_DECAY)
    def _():
        finish(intra_exact())


def _gla_intra_off_diagonal(q, k, b, vs, nc):
    per_chunk = CHUNK // GLA_SUB
    lane = lax.broadcasted_iota(jnp.int32, (GLA_SUB, LANES), 1)
    sub_mask = (lane < GLA_DK, lane >= GLA_DK)
    kcol = lax.broadcasted_iota(jnp.int32, (GLA_SUB, CHUNK), 1)
    outs = ([], [])
    for c in range(nc):
        sl = slice(c * CHUNK, (c + 1) * CHUNK)
        bc, qc, kc = b[sl], q[sl], k[sl]
        a_rows = [[jnp.zeros((GLA_SUB, CHUNK), F32)] for _ in vs]
        for blk in range(1, per_chunk):
            r0 = blk * GLA_SUB
            bref = bc[r0:r0 + 1]
            qs = qc[r0:r0 + GLA_SUB] * jnp.exp(bc[r0:r0 + GLA_SUB] - bref)
            ks = kc * jnp.exp(jnp.minimum(bref - bc, 0.0))
            for hd in range(2):
                a = _bdot_nt(jnp.where(sub_mask[hd], qs, 0.0), ks)
                a_rows[hd].append(jnp.where(kcol < r0, a, 0.0))
        for hd in range(2):
            outs[hd].append(_bdot(jnp.concatenate(a_rows[hd], axis=0), vs[hd][sl]))
    return [jnp.concatenate(o, axis=0) for o in outs]


def _gla_intra_diagonal(q, k, b, vs):
    rows = q.shape[0]
    nsub = rows // GLA_SUB
    b3 = b.reshape(nsub, GLA_SUB, LANES)
    q3 = q.reshape(nsub, GLA_SUB, LANES)
    k3 = k.reshape(nsub, GLA_SUB, LANES)
    v3 = [v.reshape(nsub, GLA_SUB, LANES) for v in vs]
    row3 = lax.broadcasted_iota(jnp.int32, (nsub, GLA_SUB, LANES), 1)
    lane3 = lax.broadcasted_iota(jnp.int32, (nsub, GLA_SUB, LANES), 2)
    rowc = lax.broadcasted_iota(jnp.int32, (nsub, GLA_SUB, 1), 1)
    o3 = [jnp.zeros((nsub, GLA_SUB, LANES), F32) for _ in vs]
    for jj in range(GLA_SUB):
        e = jnp.exp(jnp.where(row3 >= jj, b3 - b3[:, jj:jj + 1, :], 0.0))
        t = q3 * k3[:, jj:jj + 1, :] * e
        w_all = jnp.sum(t, axis=-1, keepdims=True)
        w_a = jnp.sum(jnp.where(lane3 < GLA_DK, t, 0.0), axis=-1, keepdims=True)
        for hd, w in enumerate((w_a, w_all - w_a)):
            o3[hd] = o3[hd] + jnp.where(rowc >= jj, w, 0.0) * v3[hd][:, jj:jj + 1, :]
    return [o.reshape(rows, LANES) for o in o3]


def _gla(proj, gate_in, w_gate_pad, b_gate, norm_w, bsz, seq):
    t = proj.shape[0]
    rows = GLA_GROUP
    spb = seq // rows
    qk_w = GLA_HEADS * GLA_DK
    v_w = GLA_HEADS * GLA_DV

    def at(col):
        return lambda b, n: (b * spb + n, col)

    def whole(a):
        return pl.BlockSpec(a.shape, lambda b, n: (0, 0))

    return pl.pallas_call(
        _gla_body,
        grid=(bsz, spb),
        in_specs=[pl.BlockSpec((rows, qk_w), at(OD_QD // qk_w)),
                  pl.BlockSpec((rows, qk_w), at(OD_KD // qk_w)),
                  pl.BlockSpec((rows, LANES), at(0)),
                  pl.BlockSpec((rows, v_w), at(OD_VD // v_w)),
                  pl.BlockSpec((rows, v_w), at(OD_RD // v_w)),
                  whole(w_gate_pad), whole(b_gate), whole(norm_w)],
        out_specs=pl.BlockSpec((rows, v_w), at(0)),
        out_shape=jax.ShapeDtypeStruct((t, v_w), BF16),
        scratch_shapes=[pltpu.VMEM((GLA_HEADS, GLA_DV, LANES), F32), pltpu.VMEM((GLA_HEADS, rows, GLA_DV), F32)],
        compiler_params=_cparams(("parallel", "arbitrary")),
        name="gla",
    )(proj, proj, gate_in, proj, proj, w_gate_pad, b_gate, norm_w)


def _tail_body(tiles_per_seq, tc, ma_ref, mb_ref, h_ref, woa_ref, wob_ref,
               g1_ref, b1_ref, wu_ref, cw_ref, cb_ref, wd_ref, g2_ref, b2_ref, y_ref,
               act, scratch_g, scratch_v, carry):
    at_start = pl.program_id(0) % tiles_per_seq == 0
    mix = (jnp.dot(ma_ref[...], woa_ref[...], preferred_element_type=F32)
           + jnp.dot(mb_ref[...], wob_ref[...], preferred_element_type=F32))
    x = _layer_norm(DEEPNORM_ALPHA * h_ref[...] + mix, g1_ref[...], b1_ref[...])
    xb = x.astype(BF16)
    nchunk = D_FF // tc

    def branch(idx, scratch):
        lo = idx * tc
        cur = jnp.dot(xb, wu_ref[:, lo:lo + tc], preferred_element_type=F32)
        return _causal_conv(cur, cw_ref[:, lo:lo + tc], scratch, carry.at[idx], at_start) + cb_ref[:, lo:lo + tc]

    for c in range(nchunk):
        gate = branch(c, scratch_g)
        val = branch(nchunk + c, scratch_v)
        act[:, c * tc:(c + 1) * tc] = (_silu(gate) * val).astype(BF16)
    ffn = jnp.dot(act[...], wd_ref[...], preferred_element_type=F32)
    y_ref[...] = _layer_norm(DEEPNORM_ALPHA * x + ffn, g2_ref[...], b2_ref[...])


def _layer_tail(mix_a, mix_b, h, w_out_a, w_out_b, g1, b1, w_up, conv_w, conv_b, w_down, g2, b2,
                layer, seq):
    t = h.shape[0]
    tm, tc = TAIL_TM, FFN_TC

    def tile(a):
        return pl.BlockSpec((tm, a.shape[1]), lambda i: (i, 0))

    params = (w_out_a, w_out_b, g1, b1, w_up, conv_w, conv_b, w_down, g2, b2)
    param_specs = [_resident(w_out_a), _resident(w_out_b), _resident(g1), _resident(b1),
                   _resident(w_up, layer), _resident(conv_w), _resident(conv_b), _resident(w_down, layer),
                   _resident(g2), _resident(b2)]
    return pl.pallas_call(
        functools.partial(_tail_body, seq // tm, tc),
        grid=(t // tm,),
        in_specs=[tile(mix_a), tile(mix_b), tile(h)] + param_specs,
        out_specs=pl.BlockSpec((tm, D_MODEL), lambda i: (i, 0)),
        out_shape=jax.ShapeDtypeStruct((t, D_MODEL), F32),
        scratch_shapes=[pltpu.VMEM((tm, D_FF), BF16), pltpu.VMEM((HALO + tm, tc), F32),
                        pltpu.VMEM((HALO + tm, tc), F32), pltpu.VMEM((2 * D_FF // tc, HALO, tc), F32)],
        compiler_params=_cparams(("arbitrary",)),
        name="layer_tail",
    )(mix_a, mix_b, h, *params)


def _even_w_in(w):
    a_end = 4 * EVA_PART
    gates = w[:, a_end:a_end + 2 * GDN_HEADS]
    qkv_b = w[:, a_end + 2 * GDN_HEADS:]
    pad = jnp.zeros((w.shape[0], EVB_COLS - EVB_BA - 2 * GDN_HEADS), w.dtype)
    return w[:, :a_end].astype(BF16), jnp.concatenate([qkv_b, gates, pad], axis=1).astype(BF16)


def _odd_w_in(w):
    pad = jnp.zeros((w.shape[0], LANES - GLA_RANK), w.dtype)
    return w[:, :OD_COLS].astype(BF16), jnp.concatenate([w[:, OD_COLS:], pad], axis=1).astype(BF16)


def _even_mixer(h, rel_bias, w_in, conv_w, a_log, dt_bias, norm_w, w_out, bsz, seq):
    w_a, w_b = _even_w_in(w_in)
    proj_a, proj_b = _even_in_proj(h, w_a, w_b, conv_w, seq, PROJ_TM)
    gate_pad = jnp.zeros((1, LANES), F32)
    alog_pad = lax.dynamic_update_slice(gate_pad, a_log[None].astype(F32), (0, GDN_HEADS))
    dtb_pad = lax.dynamic_update_slice(gate_pad, dt_bias[None].astype(F32), (0, GDN_HEADS))
    o_a = _gdn(proj_a, proj_b, alog_pad, dtb_pad, norm_w[None], bsz, seq)
    tiles = _bias_tiles(rel_bias, *_swa_bias_tables(), scale=LOG2E)
    ng, two_blk = len(SWA_CONFIGS), 2 * SWA_BLOCK
    bias = tiles.reshape(SWA_HEADS // 2, 2, 2, ng, SWA_BLOCK, two_blk).transpose(3, 0, 2, 1, 4, 5)
    o_b = _swa(proj_b, bias.reshape(ng, SWA_HEADS // 2, 2, two_blk, two_blk), bsz, seq)
    w_out = w_out.astype(BF16)
    return o_a, o_b, w_out[:EVA_PART], w_out[EVA_PART:]


def _odd_mixer(h, rel_bias, w_in, lam_params, diff_norm_w, w_gate, b_gate, gla_norm_w, w_out,
               lam_init, bsz, seq):
    w_main, w_gd = _odd_w_in(w_in)
    proj, gate_in = _odd_in_proj(h, w_main, w_gd, PROJ_TM)
    blk = min(DIFF_BLOCK, seq)
    bucket, neg = _diff_bias_tables(blk)
    bias = _bias_tiles(rel_bias, bucket, neg, scale=LOG2E, base_bucket=NUM_BUCKETS - 1)
    o_c = _diff_attention(proj, bias, lam_params, diff_norm_w[None], lam_init, bsz, seq, blk)
    w_gate_pad = jnp.concatenate(
        [w_gate, jnp.zeros((LANES - GLA_RANK, w_gate.shape[1]), w_gate.dtype)], axis=0)
    o_d = _gla(proj, gate_in, w_gate_pad, b_gate[None], gla_norm_w[None], bsz, seq)
    diff_v = DIFF_HEADS * 2 * DIFF_DH
    w_out = w_out.astype(BF16)
    return o_c, o_d, w_out[:diff_v], w_out[diff_v:]


def kernel(x, rel_bias, w_in_even, gdn_conv_w, gdn_a_log, gdn_dt_bias, gdn_norm_w, w_out_even,
           w_in_odd, diff_lambda, diff_norm_w, gla_w_gate, gla_b_gate, gla_norm_w, w_out_odd,
           ffn_w_up, ffn_conv_w, ffn_conv_b, ffn_w_down, ln_g, ln_b):
    bsz, seq, d = x.shape
    h = x.reshape(bsz * seq, d)
    w_up, w_down = ffn_w_up.astype(BF16), ffn_w_down.astype(BF16)
    for layer in range(DEPTH):
        i = layer // 2
        if layer % 2 == 0:
            mixed = _even_mixer(h, rel_bias, w_in_even[i], gdn_conv_w[i], gdn_a_log[i], gdn_dt_bias[i],
                                gdn_norm_w[i], w_out_even[i], bsz, seq)
        else:
            lam_init = 0.8 - 0.6 * math.exp(-0.3 * layer)
            mixed = _odd_mixer(h, rel_bias, w_in_odd[i], diff_lambda[i], diff_norm_w[i], gla_w_gate[i],
                               gla_b_gate[i], gla_norm_w[i], w_out_odd[i], lam_init, bsz, seq)
        h = _layer_tail(*mixed[:2], h, *mixed[2:], ln_g[layer, 0][None], ln_b[layer, 0][None],
                        w_up, ffn_conv_w[layer], ffn_conv_b[layer][None], w_down,
                        ln_g[layer, 1][None], ln_b[layer, 1][None], layer, seq)
    return h.reshape(bsz, seq, d)
```

```python
import functools
import math

import numpy as np
import jax
import jax.numpy as jnp
from jax import lax
from jax.experimental import pallas as pl
from jax.experimental.pallas import tpu as pltpu

F32 = jnp.float32
BF16 = jnp.bfloat16
HI = lax.Precision.HIGHEST

D_MODEL = 1024
DEPTH = 2
DEEPNORM_ALPHA = (2 * DEPTH) ** 0.25
LN_EPS = 1e-5
RMS_EPS = 1e-6
NUM_BUCKETS = 32
REL_MAX_DIST = 2048
GDN_HEADS = 6
GDN_D = 128
CHUNK = 64
GDN_GROUP = 256
SWA_CONFIGS = ((128, 1), (512, 4), (2048, 16))
SWA_HEADS = 4
SWA_DH = 64
SWA_BLOCK = 128
SWA_UNROLL = 8
SWA_SPAN = SWA_BLOCK * max(d for _, d in SWA_CONFIGS)
DIFF_HEADS = 4
DIFF_DH = 64
DIFF_BLOCK = 512
LOG2E = math.log2(math.e)
GLA_HEADS = 4
GLA_DK = 64
GLA_DV = 128
GLA_RANK = 16
GLA_TAU = 16.0
GLA_SUB = 16
GLA_GROUP = 256
GLA_MAX_DECAY = 60.0
D_FF = 2816

LANES = 128
SUBLANES = 8
HALO = 8
VMEM_LIMIT = 56 * 1024 * 1024
NEG_BIG = -1e30

EVA_PART = GDN_HEADS * GDN_D
EVB_BA = 2304
EVB_COLS = 2560
EVB_TN = 512
OD_QC, OD_KC, OD_VC = 0, 512, 1024
OD_QD, OD_KD, OD_VD, OD_RD = 1536, 1792, 2048, 2560
OD_COLS = 3072
OD_TN = 768
PROJ_TM = 512
TAIL_TM = 512
FFN_TC = 256


def _cparams(sem):
    return pltpu.CompilerParams(dimension_semantics=sem, vmem_limit_bytes=VMEM_LIMIT)


def _bdot(a, b):
    return jnp.dot(a.astype(BF16), b.astype(BF16), preferred_element_type=F32)


def _bdot_nt(a, b):
    return lax.dot_general(a.astype(BF16), b.astype(BF16), (((1,), (1,)), ((), ())),
                           preferred_element_type=F32)


def _bdot_tn(a, b):
    return lax.dot_general(a.astype(BF16), b.astype(BF16), (((0,), (0,)), ((), ())),
                           preferred_element_type=F32)


def _sigmoid(x):
    return 1.0 / (1.0 + jnp.exp(-x))


def _silu(x):
    return x * _sigmoid(x)


def _softplus(x):
    return jnp.maximum(x, 0.0) + jnp.log1p(jnp.exp(-jnp.abs(x)))


def _log_sigmoid(x):
    return -_softplus(-x)


def _resident(a, lead=None):
    if lead is None:
        return pl.BlockSpec(a.shape, lambda i: (0,) * a.ndim, pipeline_mode=pl.Buffered(1))
    return pl.BlockSpec((None,) + a.shape[1:], lambda i: (lead,) + (0,) * (a.ndim - 1),
                        pipeline_mode=pl.Buffered(1))


def _odd_in_body(x_ref, w_ref, wg_ref, o_ref, og_ref):
    xb = x_ref[...].astype(BF16)
    for c in range(OD_COLS // OD_TN):
        cols = slice(c * OD_TN, (c + 1) * OD_TN)
        o_ref[:, cols] = jnp.dot(xb, w_ref[:, cols], preferred_element_type=F32).astype(o_ref.dtype)
    og_ref[...] = jnp.dot(xb, wg_ref[...], preferred_element_type=F32)


def _odd_in_proj(x, w, w_gate_in, tm):
    t, k = x.shape
    return pl.pallas_call(
        _odd_in_body,
        grid=(t // tm,),
        in_specs=[pl.BlockSpec((tm, k), lambda i: (i, 0)), _resident(w), _resident(w_gate_in)],
        out_specs=[pl.BlockSpec((tm, OD_COLS), lambda i: (i, 0)), pl.BlockSpec((tm, LANES), lambda i: (i, 0))],
        out_shape=[jax.ShapeDtypeStruct((t, OD_COLS), BF16), jax.ShapeDtypeStruct((t, LANES), F32)],
        compiler_params=_cparams(("parallel",)),
        name="odd_in_proj",
    )(x, w, w_gate_in)


def _rel_bucket_np(dist):
    max_exact = NUM_BUCKETS // 2
    d = np.maximum(dist, 1).astype(np.float32)
    large = max_exact + (np.log(d / max_exact) / math.log(REL_MAX_DIST / max_exact)
                         * (NUM_BUCKETS - max_exact)).astype(np.int32)
    large = np.minimum(large, NUM_BUCKETS - 1)
    return np.where(dist < max_exact, dist, large).astype(np.int32)


def _bias_body(scale, base_bucket, tile_buckets, rb_ref, bucket_ref, neg_ref, o_ref):
    h = pl.program_id(0)
    t = pl.program_id(1)
    base = 0.0 if base_bucket is None else rb_ref[base_bucket, h]
    for tile, present in enumerate(tile_buckets):
        @pl.when(t == tile)
        def _(present=present):
            bucket = bucket_ref[0]
            acc = neg_ref[0]
            for b in present:
                acc = acc + jnp.where(bucket == b, (rb_ref[b, h] - base) * scale, 0.0)
            o_ref[0, 0] = acc


def _bias_tiles(rel_bias, bucket, neg, scale=1.0, base_bucket=None):
    nt, r, c = bucket.shape
    nh = rel_bias.shape[1]
    tile_buckets = tuple(tuple(int(b) for b in np.unique(bucket[t][neg[t] == 0]) if b != base_bucket)
                         for t in range(nt))
    return pl.pallas_call(
        functools.partial(_bias_body, scale, base_bucket, tile_buckets),
        grid=(nh, nt),
        in_specs=[pl.BlockSpec(memory_space=pltpu.SMEM),
                  pl.BlockSpec((1, r, c), lambda h, t: (t, 0, 0)),
                  pl.BlockSpec((1, r, c), lambda h, t: (t, 0, 0))],
        out_specs=pl.BlockSpec((1, 1, r, c), lambda h, t: (h, t, 0, 0)),
        out_shape=jax.ShapeDtypeStruct((nh, nt, r, c), F32),
        compiler_params=_cparams(("parallel", "parallel")),
        name="rel_bias_tiles",
    )(rel_bias, jnp.asarray(bucket), jnp.asarray(neg))


def _causal_conv(cur, w, scratch, carry, at_start):
    width = w.shape[0]
    rows = cur.shape[0]
    scratch[0:HALO, :] = jnp.where(at_start, 0.0, carry[...])
    scratch[HALO:, :] = cur
    carry[...] = cur[rows - HALO:]
    y = w[width - 1:width, :] * cur
    for j in range(width - 1):
        back = width - 1 - j
        y = y + w[j:j + 1, :] * scratch[HALO - back:HALO - back + rows, :]
    return y


def _even_in_body(tiles_per_seq, x_ref, wa_ref, wb_ref, cw_ref, oa_ref, ob_ref, scratch, carry):
    at_start = pl.program_id(0) % tiles_per_seq == 0
    x = x_ref[...].astype(BF16)
    for part in range(3):
        cols = slice(part * EVA_PART, (part + 1) * EVA_PART)
        y = jnp.dot(x, wa_ref[:, cols], preferred_element_type=F32)
        c = _silu(_causal_conv(y, cw_ref[:, cols], scratch, carry.at[part], at_start))
        for hd in range(GDN_HEADS):
            ch = c[:, hd * GDN_D:(hd + 1) * GDN_D]
            if part < 2:
                inv = lax.rsqrt(jnp.sum(ch * ch, axis=-1, keepdims=True) + RMS_EPS)
                ch = ch * (inv * GDN_D ** -0.5 if part == 0 else inv)
            lo = part * EVA_PART + hd * GDN_D
            oa_ref[:, lo:lo + GDN_D] = ch.astype(oa_ref.dtype)
    z_cols = slice(3 * EVA_PART, 4 * EVA_PART)
    oa_ref[:, z_cols] = jnp.dot(x, wa_ref[:, z_cols], preferred_element_type=F32).astype(oa_ref.dtype)
    for c in range(EVB_COLS // EVB_TN):
        cols = slice(c * EVB_TN, (c + 1) * EVB_TN)
        ob_ref[:, cols] = jnp.dot(x, wb_ref[:, cols], preferred_element_type=F32)


def _even_in_proj(x, w_a, w_b, conv_w, seq, tm):
    t, k = x.shape
    return pl.pallas_call(
        functools.partial(_even_in_body, seq // tm),
        grid=(t // tm,),
        in_specs=[pl.BlockSpec((tm, k), lambda i: (i, 0)),
                  _resident(w_a), _resident(w_b), _resident(conv_w)],
        out_specs=[pl.BlockSpec((tm, w_a.shape[1]), lambda i: (i, 0)),
                   pl.BlockSpec((tm, EVB_COLS), lambda i: (i, 0))],
        out_shape=[jax.ShapeDtypeStruct((t, w_a.shape[1]), BF16), jax.ShapeDtypeStruct((t, EVB_COLS), F32)],
        scratch_shapes=[pltpu.VMEM((HALO + tm, EVA_PART), F32), pltpu.VMEM((3, HALO, EVA_PART), F32)],
        compiler_params=_cparams(("arbitrary",)),
        name="even_in_proj",
    )(x, w_a, w_b, conv_w)


def _gdn_body(q_ref, k_ref, v_ref, z_ref, ba_ref, alog_ref, dtb_ref, nw_ref, o_ref, state):
    n = pl.program_id(1)

    @pl.when(n == 0)
    def _():
        state[...] = jnp.zeros_like(state)

    grp = GDN_GROUP
    nc = grp // CHUNK
    nh = GDN_HEADS
    hs = range(nh)
    ri = lax.broadcasted_iota(jnp.int32, (grp, grp), 0)
    ci = lax.broadcasted_iota(jnp.int32, (grp, grp), 1)
    same = (ri // CHUNK) == (ci // CHUNK)
    incl = same & (ri >= ci)
    strict = same & (ri > ci)
    eye = (ri == ci).astype(F32)
    tri = incl.astype(BF16)

    def chunk_local(rows):
        ba = ba_ref[rows, :]
        beta_all = _sigmoid(ba)
        g_all = -jnp.exp(alog_ref[...]) * _softplus(ba + dtb_ref[...])
        g_hi = g_all.astype(BF16)
        rem = g_all - g_hi.astype(F32)
        g_mid = rem.astype(BF16)
        g_lo = (rem - g_mid.astype(F32)).astype(BF16)
        gam_all = (jnp.dot(tri, g_hi, preferred_element_type=F32)
                   + jnp.dot(tri, g_mid, preferred_element_type=F32)
                   + jnp.dot(tri, g_lo, preferred_element_type=F32))
        gam_rows = gam_all.T

        q = [q_ref[rows, h * GDN_D:(h + 1) * GDN_D] for h in hs]
        k = [k_ref[rows, h * GDN_D:(h + 1) * GDN_D] for h in hs]
        v = [v_ref[rows, h * GDN_D:(h + 1) * GDN_D] for h in hs]
        gam = [jnp.broadcast_to(gam_all[:, nh + h:nh + h + 1], (grp, GDN_D)) for h in hs]
        bcol = [beta_all[:, h:h + 1] for h in hs]
        decay, x, inv = [], [], []
        for h in hs:
            diff = gam[h][:, 0:1] - gam_rows[nh + h:nh + h + 1, :]
            decay.append(jnp.where(incl, jnp.exp(jnp.where(incl, diff, 0.0)), 0.0))
            kk = _bdot_nt(k[h], k[h])
            x.append(-jnp.where(strict, bcol[h] * kk * decay[h], 0.0))
            inv.append(eye + x[h])
        for _ in range(5):
            for h in hs:
                x[h] = _bdot(x[h], x[h])
                inv[h] = inv[h] + _bdot(inv[h], x[h])
        u, w, qk, q_dec, k_dec, g_last = [], [], [], [], [], []
        for h in hs:
            eg = jnp.exp(gam[h])
            uw = _bdot(inv[h], jnp.concatenate([v[h] * bcol[h], k[h] * (bcol[h] * eg)], axis=1))
            u.append(uw[:, :GDN_D])
            w.append(uw[:, GDN_D:])
            qk.append(_bdot_nt(q[h], k[h]) * decay[h])
            q_dec.append(q[h] * eg)
            kd, gl = [], []
            for c in range(nc):
                last = gam[h][(c + 1) * CHUNK - 1:(c + 1) * CHUNK, :]
                kd.append(k[h][c * CHUNK:(c + 1) * CHUNK] * jnp.exp(last - gam[h][c * CHUNK:(c + 1) * CHUNK]))
                gl.append(jnp.exp(last))
            k_dec.append(kd)
            g_last.append(gl)
        return u, w, qk, q_dec, k_dec, g_last

    def recurrence(rows, local, s):
        u, w, qk, q_dec, k_dec, g_last = local
        q_s = [[] for _ in hs]
        delta = [[] for _ in hs]
        for c in range(nc):
            sl = slice(c * CHUNK, (c + 1) * CHUNK)
            for h in hs:
                r = _bdot(jnp.concatenate([w[h][sl], q_dec[h][sl]], axis=0), s[h])
                d = u[h][sl] - r[:CHUNK]
                q_s[h].append(r[CHUNK:])
                delta[h].append(d)
                s[h] = g_last[h][c] * s[h] + _bdot_tn(k_dec[h][c], d)
        for h in hs:
            o = jnp.concatenate(q_s[h], axis=0) + _bdot(qk[h], jnp.concatenate(delta[h], axis=0))
            o = o * lax.rsqrt(jnp.mean(o * o, axis=-1, keepdims=True) + RMS_EPS) * nw_ref[...]
            z = z_ref[rows, h * GDN_D:(h + 1) * GDN_D].astype(F32)
            o_ref[rows, h * GDN_D:(h + 1) * GDN_D] = (o * _silu(z)).astype(o_ref.dtype)
        return s

    rows = slice(0, grp)
    s = recurrence(rows, chunk_local(rows), [state[h] for h in hs])
    for h in hs:
        state[h] = s[h]


def _gdn(proj_a, proj_b, alog_pad, dtb_pad, norm_w, bsz, seq):
    t = proj_a.shape[0]
    rows = GDN_GROUP
    spb = seq // rows
    width = GDN_HEADS * GDN_D

    def at(col):
        return lambda b, n: (b * spb + n, col)

    return pl.pallas_call(
        _gdn_body,
        grid=(bsz, spb),
        in_specs=[pl.BlockSpec((rows, width), at(0)),
                  pl.BlockSpec((rows, width), at(1)),
                  pl.BlockSpec((rows, width), at(2)),
                  pl.BlockSpec((rows, width), at(3)),
                  pl.BlockSpec((rows, LANES), at(EVB_BA // LANES)),
                  pl.BlockSpec((1, LANES), lambda b, n: (0, 0)),
                  pl.BlockSpec((1, LANES), lambda b, n: (0, 0)),
                  pl.BlockSpec((1, LANES), lambda b, n: (0, 0))],
        out_specs=pl.BlockSpec((rows, width), at(0)),
        out_shape=jax.ShapeDtypeStruct((t, width), BF16),
        scratch_shapes=[pltpu.VMEM((GDN_HEADS, GDN_D, GDN_D), F32)],
        compiler_params=_cparams(("parallel", "arbitrary")),
        name="gdn",
    )(proj_a, proj_a, proj_a, proj_a, proj_b, alog_pad, dtb_pad, norm_w)


def _swa_bias_tables():
    qi = np.arange(SWA_BLOCK)[:, None] + SWA_BLOCK
    kj = np.arange(2 * SWA_BLOCK)[None, :]
    rel = qi - kj
    buckets, negs = [], []
    for has_prev in (True, False):
        for window, dilation in SWA_CONFIGS:
            valid = (rel >= 0) & (rel <= window // dilation) & (has_prev | (kj >= SWA_BLOCK))
            buckets.append(_rel_bucket_np(np.maximum(rel, 0) * dilation))
            negs.append(np.where(valid, 0.0, NEG_BIG))
    return np.stack(buckets).astype(np.int32), np.stack(negs).astype(np.float32)


def _swa_body(*refs):
    ng = len(SWA_CONFIGS)
    ins, bias_ref, o_ref = refs[:5 * ng], refs[5 * ng], refs[5 * ng + 1]
    scratch = refs[5 * ng + 2:]
    kbufs, vbufs, o_scr, lse_scr = scratch[:ng], scratch[ng:2 * ng], scratch[2 * ng], scratch[2 * ng + 1]
    j = pl.program_id(1)
    blk = SWA_BLOCK
    first_head = lax.broadcasted_iota(jnp.int32, (blk, LANES), 1) < SWA_DH

    for g, (_, d) in enumerate(SWA_CONFIGS):
        q_ref, kc_ref, kp_ref, vc_ref, vp_ref = ins[5 * g:5 * g + 5]
        kbuf, vbuf = kbufs[g], vbufs[g]
        halo = blk * d
        kbuf[0:halo, :] = kp_ref[...]
        kbuf[halo:, :] = kc_ref[...]
        vbuf[0:halo, :] = vp_ref[...]
        vbuf[halo:, :] = vc_ref[...]

        def unit(u, carry, g=g, d=d, halo=halo, q_ref=q_ref, kbuf=kbuf, vbuf=vbuf):
            base = (u // d) * halo + u % d
            q = q_ref[pl.ds(base, blk, stride=d), :] * (SWA_DH ** -0.5 * LOG2E)
            k = kbuf[pl.ds(base, 2 * blk, stride=d), :]
            v = vbuf[pl.ds(base, 2 * blk, stride=d), :]
            lhs = jnp.concatenate([jnp.where(first_head, q, 0.0), jnp.where(first_head, 0.0, q)], axis=0)
            no_prev = jnp.where((j == 0) & (u < d), 1, 0)
            s = _bdot_nt(lhs, k) + bias_ref[g, 0, no_prev]
            m = jnp.max(s, axis=-1, keepdims=True)
            p = jnp.exp2(s - m)
            l = jnp.sum(p, axis=-1, keepdims=True)
            o2 = _bdot(p, v) * (1.0 / l)
            lse2 = m + jnp.log2(l)
            o_scr[g, pl.ds(base, blk, stride=d), :] = jnp.where(first_head, o2[:blk], o2[blk:])
            lse_scr[g, pl.ds(base, blk, stride=d), :] = jnp.where(first_head, lse2[:blk], lse2[blk:])
            return carry

        lax.fori_loop(0, SWA_SPAN // blk, unit, 0, unroll=SWA_UNROLL)

    def combine(c, carry):
        rows = pl.ds(pl.multiple_of(c * 2 * blk, 2 * blk), 2 * blk)
        lse = [lse_scr[g, rows, :] for g in range(ng)]
        m = functools.reduce(jnp.maximum, lse)
        e = [jnp.exp2(x - m) for x in lse]
        den = functools.reduce(lambda x, y: x + y, e)
        o_ref[rows, :] = functools.reduce(
            lambda x, y: x + y, [(e[g] / den) * o_scr[g, rows, :] for g in range(ng)]).astype(o_ref.dtype)
        return carry

    lax.fori_loop(0, SWA_SPAN // (2 * blk), combine, 0)


def _swa(proj, bias, bsz, seq):
    t = proj.shape[0]
    ng = len(SWA_CONFIGS)
    nspan = seq // SWA_SPAN
    npair = SWA_HEADS * SWA_DH // LANES
    group_cols = SWA_HEADS * SWA_DH // LANES
    in_specs, scratch_k = [], []
    for g, (_, d) in enumerate(SWA_CONFIGS):
        halo = SWA_BLOCK * d
        per_span = SWA_SPAN // halo

        def cur(which, g=g):
            col = (which * ng + g) * group_cols
            return lambda b, j, p: (b * nspan + j, col + p)

        def prev(which, g=g, per_span=per_span):
            col = (which * ng + g) * group_cols
            return lambda b, j, p: (jnp.maximum((b * nspan + j) * per_span - 1, 0), col + p)

        in_specs += [pl.BlockSpec((SWA_SPAN, LANES), cur(0)),
                     pl.BlockSpec((SWA_SPAN, LANES), cur(1)), pl.BlockSpec((halo, LANES), prev(1)),
                     pl.BlockSpec((SWA_SPAN, LANES), cur(2)), pl.BlockSpec((halo, LANES), prev(2))]
        scratch_k.append(pltpu.VMEM((halo + SWA_SPAN, LANES), F32))
    in_specs.append(pl.BlockSpec((ng, 1, 2, 2 * SWA_BLOCK, 2 * SWA_BLOCK), lambda b, j, p: (0, p, 0, 0, 0)))
    return pl.pallas_call(
        _swa_body,
        grid=(bsz, nspan, npair),
        in_specs=in_specs,
        out_specs=pl.BlockSpec((SWA_SPAN, LANES), lambda b, j, p: (b * nspan + j, p)),
        out_shape=jax.ShapeDtypeStruct((t, npair * LANES), BF16),
        scratch_shapes=scratch_k + scratch_k + [pltpu.VMEM((ng, SWA_SPAN, LANES), F32)] * 2,
        compiler_params=_cparams(("parallel", "parallel", "parallel")),
        name="swa",
    )(*([proj] * (5 * ng)), bias)


def _layer_norm(y, g, b):
    mu = jnp.mean(y, axis=-1, keepdims=True)
    yc = y - mu
    var = jnp.mean(yc * yc, axis=-1, keepdims=True)
    return yc * lax.rsqrt(var + LN_EPS) * g + b


def _diff_bias_tables(blk):
    buckets = _rel_bucket_np(np.arange(2 * REL_MAX_DIST))
    far = int(np.max(np.nonzero(buckets != NUM_BUCKETS - 1)[0])) + 1
    nb = -(-(far + blk - 1) // blk)
    kk = np.arange(blk)[:, None]
    qq = np.arange(blk)[None, :]
    dist = np.stack([t * blk + qq - kk for t in range(nb + 1)])
    bucket = _rel_bucket_np(np.maximum(dist, 0))
    neg = np.where(dist >= 0, 0.0, NEG_BIG).astype(np.float32)
    return bucket, neg


def _diff_body(blk, nb, lam_init, q_ref, qn_ref, k_ref, v_ref, bias_ref, lam_ref, nw_ref, o_ref,
               vt, acc1, acc2, s_a, s_b):
    qi = pl.program_id(2)
    dv = 2 * DIFF_DH
    seq = k_ref.shape[0]

    @pl.when(qi == 0)
    def _():
        vt[dv:, :] = jnp.ones((vt.shape[0] - dv, seq), BF16)

        def fill(c, carry):
            st = pl.multiple_of(c * blk, blk)
            vt[0:dv, pl.ds(st, blk)] = v_ref[pl.ds(st, blk), :].astype(F32).T.astype(BF16)
            return carry

        lax.fori_loop(0, seq // blk, fill, 0)

    feature = lax.broadcasted_iota(jnp.int32, (dv, blk), 0)

    def components(ref):
        q_t = (ref[...].astype(F32) * (DIFF_DH ** -0.5 * LOG2E)).T
        return (jnp.where(feature < DIFF_DH, q_t, 0.0).astype(BF16),
                jnp.where(feature >= DIFF_DH, q_t, 0.0).astype(BF16))

    qs = components(q_ref)
    qs_next = components(qn_ref)
    accs = (acc1, acc2)
    for acc in accs:
        acc[...] = jnp.zeros_like(acc)

    last = pl.num_programs(2) - 1

    def key_rows(kj):
        return pl.ds(pl.multiple_of(jnp.minimum(kj, last) * blk, blk), blk)

    def scores(kj, dst, queries=qs):
        k = k_ref[key_rows(kj), :]
        for c, qc in enumerate(queries):
            dst[c] = jnp.dot(k, qc, preferred_element_type=F32)

    def consume(biased, kj, src, ms):
        vtb = vt[:, key_rows(kj)]
        out = []
        for c, (m, acc) in enumerate(zip(ms, accs)):
            s = src[c]
            if biased:
                s = s + bias_ref[0, jnp.minimum(qi - kj, nb)]
            m_new = jnp.maximum(m, jnp.max(s, axis=0, keepdims=True))
            p = jnp.exp2(s - m_new).astype(BF16)
            acc[...] = jnp.exp2(m - m_new) * acc[...] + jnp.dot(vtb, p, preferred_element_type=F32)
            out.append(m_new)
        return tuple(out)

    nblocks = qi + 1
    trips = nblocks // 2
    odd = nblocks % 2 == 1
    far_trips = jnp.maximum(qi - nb + 1, 0) // 2

    def pair(biased, t, ms):
        kj = 2 * t
        scores(kj + 1, s_b)
        ms = consume(biased, kj, s_a, ms)
        if biased:
            hand_off = (t == trips - 1) & jnp.logical_not(odd)
            scores(jnp.where(hand_off, 0, kj + 2), s_a,
                   tuple(jnp.where(hand_off, qn, qc) for qc, qn in zip(qs, qs_next)))
        else:
            scores(kj + 2, s_a)
        return consume(biased, kj + 1, s_b, ms)

    @pl.when(qi == 0)
    def _():
        scores(0, s_a)

    m0 = jnp.full((1, blk), NEG_BIG, F32)
    ms = lax.fori_loop(0, far_trips, functools.partial(pair, False), (m0, m0))
    ms = lax.fori_loop(far_trips, trips, functools.partial(pair, True), ms)

    def finish():
        lp = lam_ref[...]
        lam = (jnp.exp(jnp.sum(lp[0:1] * lp[1:2], axis=-1, keepdims=True))
               - jnp.exp(jnp.sum(lp[2:3] * lp[3:4], axis=-1, keepdims=True)) + lam_init)
        a1, a2 = acc1[...], acc2[...]
        o_t = a1[:dv] * (1.0 / a1[dv:dv + 1]) - a2[:dv] * (lam / a2[dv:dv + 1])
        o = o_t.T
        o = o * lax.rsqrt(jnp.mean(o * o, axis=-1, keepdims=True) + RMS_EPS) * nw_ref[...]
        o_ref[...] = (o * (1.0 - lam_init)).astype(o_ref.dtype)

    @pl.when(odd)
    def _():
        consume(True, qi, s_a, ms)
        scores(0, s_a, qs_next)
        finish()

    @pl.when(jnp.logical_not(odd))
    def _():
        finish()


def _diff_attention(proj, bias, lam_params, norm_w, lam_init, bsz, seq, blk):
    t = proj.shape[0]
    nq = seq // blk
    nt = bias.shape[1]
    nb = nt - 1
    width = 2 * DIFF_DH
    ones_rows = 2 * SUBLANES
    return pl.pallas_call(
        functools.partial(_diff_body, blk, nb, lam_init),
        grid=(bsz, DIFF_HEADS, nq),
        in_specs=[pl.BlockSpec((blk, width), lambda b, h, i: (b * nq + i, OD_QC // width + h)),
                  pl.BlockSpec((blk, width),
                               lambda b, h, i: (b * nq + jnp.minimum(i + 1, nq - 1), OD_QC // width + h)),
                  pl.BlockSpec((seq, width), lambda b, h, i: (b, OD_KC // width + h)),
                  pl.BlockSpec((seq, width), lambda b, h, i: (b, OD_VC // width + h)),
                  pl.BlockSpec((1, nt, blk, blk), lambda b, h, i: (h, 0, 0, 0)),
                  pl.BlockSpec((4, DIFF_DH), lambda b, h, i: (0, 0)),
                  pl.BlockSpec((1, width), lambda b, h, i: (0, 0))],
        out_specs=pl.BlockSpec((blk, width), lambda b, h, i: (b * nq + i, h)),
        out_shape=jax.ShapeDtypeStruct((t, DIFF_HEADS * width), BF16),
        scratch_shapes=[pltpu.VMEM((width + ones_rows, seq), BF16),
                        pltpu.VMEM((width + ones_rows, blk), F32), pltpu.VMEM((width + ones_rows, blk), F32),
                        pltpu.VMEM((2, blk, blk), F32), pltpu.VMEM((2, blk, blk), F32)],
        compiler_params=_cparams(("arbitrary", "arbitrary", "arbitrary")),
        name="diff_attn",
    )(proj, proj, proj, proj, bias, lam_params, norm_w)


def _gla_body(q_ref, k_ref, gd_ref, v_ref, r_ref, wg_ref, bg_ref, nw_ref, o_ref, state, part):
    n = pl.program_id(1)

    @pl.when(n == 0)
    def _():
        state[...] = jnp.zeros_like(state)

    rows = GLA_GROUP
    nc = rows // CHUNK
    npair = GLA_HEADS // 2
    gate = jnp.dot(gd_ref[...], wg_ref[...], precision=HI, preferred_element_type=F32) + bg_ref[...]
    log_a = _log_sigmoid(gate) * (1.0 / GLA_TAU)

    ri = lax.broadcasted_iota(jnp.int32, (rows, rows), 0)
    ci = lax.broadcasted_iota(jnp.int32, (rows, rows), 1)
    causal = ((ri // CHUNK) == (ci // CHUNK)) & (ri >= ci)
    tri = causal.astype(BF16)
    hi = log_a.astype(BF16)
    rem = log_a - hi.astype(F32)
    mid = rem.astype(BF16)
    lo = (rem - mid.astype(F32)).astype(BF16)
    b_all = (jnp.dot(tri, hi, preferred_element_type=F32) + jnp.dot(tri, mid, preferred_element_type=F32)
             + jnp.dot(tri, lo, preferred_element_type=F32))

    lane = lax.broadcasted_iota(jnp.int32, (rows, LANES), 1)
    lane_c = lax.broadcasted_iota(jnp.int32, (CHUNK, LANES), 1)
    head_lanes = (lane < GLA_DK, lane >= GLA_DK)
    chunk_lanes = (lane_c < GLA_DK, lane_c >= GLA_DK)
    pairs = []
    for p in range(npair):
        cols = slice(p * LANES, (p + 1) * LANES)
        pairs.append(dict(
            b=b_all[:, cols],
            q=q_ref[:, cols].astype(F32) * GLA_DK ** -0.5,
            k=k_ref[:, cols].astype(F32),
            v=[v_ref[:, (2 * p + hd) * GLA_DV:(2 * p + hd + 1) * GLA_DV].astype(F32) for hd in range(2)]))

    for pr in pairs:
        pr["q_dec"] = pr["q"] * jnp.exp(pr["b"])
    for c in range(nc):
        sl = slice(c * CHUNK, (c + 1) * CHUNK)
        for p, pr in enumerate(pairs):
            bc = pr["b"][sl]
            b_last = bc[CHUNK - 1:CHUNK]
            k_dec = pr["k"][sl] * jnp.exp(b_last - bc)
            e_last = jnp.exp(b_last)
            for hd in range(2):
                h = 2 * p + hd
                st = state[h]
                part[h, sl, :] = _bdot_nt(jnp.where(chunk_lanes[hd], pr["q_dec"][sl], 0.0), st)
                state[h] = st * e_last + _bdot_tn(pr["v"][hd][sl], k_dec)

    def finish(intra):
        for h in range(GLA_HEADS):
            o = part[h] + intra[h]
            o = o * lax.rsqrt(jnp.mean(o * o, axis=-1, keepdims=True) + RMS_EPS) * nw_ref[...]
            gate_r = _silu(r_ref[:, h * GLA_DV:(h + 1) * GLA_DV].astype(F32))
            o_ref[:, h * GLA_DV:(h + 1) * GLA_DV] = (o * gate_r).astype(o_ref.dtype)

    def intra_whole_chunk():
        out = []
        for pr in pairs:
            k_inv = pr["k"] * jnp.exp(jnp.minimum(-pr["b"], GLA_MAX_DECAY))
            for hd in range(2):
                a = _bdot_nt(jnp.where(head_lanes[hd], pr["q_dec"], 0.0), k_inv)
                out.append(_bdot(jnp.where(causal, a, 0.0), pr["v"][hd]))
        return out

    def intra_exact():
        out = []
        for pr in pairs:
            off = _gla_intra_off_diagonal(pr["q"], pr["k"], pr["b"], pr["v"], nc)
            diag = _gla_intra_diagonal(pr["q"], pr["k"], pr["b"], pr["v"])
            out += [off[hd] + diag[hd] for hd in range(2)]
        return out

    chunk_decay = jnp.max(-b_all.reshape(nc, CHUNK, npair * LANES)[:, CHUNK - 1:CHUNK, :])

    @pl.when(chunk_decay <= GLA_MAX_DECAY)
    def _():
        finish(intra_whole_chunk())

    @pl.when(chunk_decay > GLA_MAX_DECAY)
    def _():
        finish(intra_exact())


def _gla_intra_off_diagonal(q, k, b, vs, nc):
    per_chunk = CHUNK // GLA_SUB
    lane = lax.broadcasted_iota(jnp.int32, (GLA_SUB, LANES), 1)
    sub_mask = (lane < GLA_DK, lane >= GLA_DK)
    kcol = lax.broadcasted_iota(jnp.int32, (GLA_SUB, CHUNK), 1)
    outs = ([], [])
    for c in range(nc):
        sl = slice(c * CHUNK, (c + 1) * CHUNK)
        bc, qc, kc = b[sl], q[sl], k[sl]
        a_rows = [[jnp.zeros((GLA_SUB, CHUNK), F32)] for _ in vs]
        for blk in range(1, per_chunk):
            r0 = blk * GLA_SUB
            bref = bc[r0:r0 + 1]
            qs = qc[r0:r0 + GLA_SUB] * jnp.exp(bc[r0:r0 + GLA_SUB] - bref)
            ks = kc * jnp.exp(jnp.minimum(bref - bc, 0.0))
            for hd in range(2):
                a = _bdot_nt(jnp.where(sub_mask[hd], qs, 0.0), ks)
                a_rows[hd].append(jnp.where(kcol < r0, a, 0.0))
        for hd in range(2):
            outs[hd].append(_bdot(jnp.concatenate(a_rows[hd], axis=0), vs[hd][sl]))
    return [jnp.concatenate(o, axis=0) for o in outs]


def _gla_intra_diagonal(q, k, b, vs):
    rows = q.shape[0]
    nsub = rows // GLA_SUB
    b3 = b.reshape(nsub, GLA_SUB, LANES)
    q3 = q.reshape(nsub, GLA_SUB, LANES)
    k3 = k.reshape(nsub, GLA_SUB, LANES)
    v3 = [v.reshape(nsub, GLA_SUB, LANES) for v in vs]
    row3 = lax.broadcasted_iota(jnp.int32, (nsub, GLA_SUB, LANES), 1)
    lane3 = lax.broadcasted_iota(jnp.int32, (nsub, GLA_SUB, LANES), 2)
    rowc = lax.broadcasted_iota(jnp.int32, (nsub, GLA_SUB, 1), 1)
    o3 = [jnp.zeros((nsub, GLA_SUB, LANES), F32) for _ in vs]
    for jj in range(GLA_SUB):
        e = jnp.exp(jnp.where(row3 >= jj, b3 - b3[:, jj:jj + 1, :], 0.0))
        t = q3 * k3[:, jj:jj + 1, :] * e
        w_all = jnp.sum(t, axis=-1, keepdims=True)
        w_a = jnp.sum(jnp.where(lane3 < GLA_DK, t, 0.0), axis=-1, keepdims=True)
        for hd, w in enumerate((w_a, w_all - w_a)):
            o3[hd] = o3[hd] + jnp.where(rowc >= jj, w, 0.0) * v3[hd][:, jj:jj + 1, :]
    return [o.reshape(rows, LANES) for o in o3]


def _gla(proj, gate_in, w_gate_pad, b_gate, norm_w, bsz, seq):
    t = proj.shape[0]
    rows = GLA_GROUP
    spb = seq // rows
    qk_w = GLA_HEADS * GLA_DK
    v_w = GLA_HEADS * GLA_DV

    def at(col):
        return lambda b, n: (b * spb + n, col)

    def whole(a):
        return pl.BlockSpec(a.shape, lambda b, n: (0, 0))

    return pl.pallas_call(
        _gla_body,
        grid=(bsz, spb),
        in_specs=[pl.BlockSpec((rows, qk_w), at(OD_QD // qk_w)),
                  pl.BlockSpec((rows, qk_w), at(OD_KD // qk_w)),
                  pl.BlockSpec((rows, LANES), at(0)),
                  pl.BlockSpec((rows, v_w), at(OD_VD // v_w)),
                  pl.BlockSpec((rows, v_w), at(OD_RD // v_w)),
                  whole(w_gate_pad), whole(b_gate), whole(norm_w)],
        out_specs=pl.BlockSpec((rows, v_w), at(0)),
        out_shape=jax.ShapeDtypeStruct((t, v_w), BF16),
        scratch_shapes=[pltpu.VMEM((GLA_HEADS, GLA_DV, LANES), F32), pltpu.VMEM((GLA_HEADS, rows, GLA_DV), F32)],
        compiler_params=_cparams(("parallel", "arbitrary")),
        name="gla",
    )(proj, proj, gate_in, proj, proj, w_gate_pad, b_gate, norm_w)


def _tail_body(tiles_per_seq, tc, ma_ref, mb_ref, h_ref, woa_ref, wob_ref,
               g1_ref, b1_ref, wu_ref, cw_ref, cb_ref, wd_ref, g2_ref, b2_ref, y_ref,
               act, scratch_g, scratch_v, carry):
    at_start = pl.program_id(0) % tiles_per_seq == 0
    mix = (jnp.dot(ma_ref[...], woa_ref[...], preferred_element_type=F32)
           + jnp.dot(mb_ref[...], wob_ref[...], preferred_element_type=F32))
    x = _layer_norm(DEEPNORM_ALPHA * h_ref[...] + mix, g1_ref[...], b1_ref[...])
    xb = x.astype(BF16)
    nchunk = D_FF // tc

    def branch(idx, scratch):
        lo = idx * tc
        cur = jnp.dot(xb, wu_ref[:, lo:lo + tc], preferred_element_type=F32)
        return _causal_conv(cur, cw_ref[:, lo:lo + tc], scratch, carry.at[idx], at_start) + cb_ref[:, lo:lo + tc]

    for c in range(nchunk):
        gate = branch(c, scratch_g)
        val = branch(nchunk + c, scratch_v)
        act[:, c * tc:(c + 1) * tc] = (_silu(gate) * val).astype(BF16)
    ffn = jnp.dot(act[...], wd_ref[...], preferred_element_type=F32)
    y_ref[...] = _layer_norm(DEEPNORM_ALPHA * x + ffn, g2_ref[...], b2_ref[...])


def _layer_tail(mix_a, mix_b, h, w_out_a, w_out_b, g1, b1, w_up, conv_w, conv_b, w_down, g2, b2,
                layer, seq):
    t = h.shape[0]
    tm, tc = TAIL_TM, FFN_TC

    def tile(a):
        return pl.BlockSpec((tm, a.shape[1]), lambda i: (i, 0))

    params = (w_out_a, w_out_b, g1, b1, w_up, conv_w, conv_b, w_down, g2, b2)
    param_specs = [_resident(w_out_a), _resident(w_out_b), _resident(g1), _resident(b1),
                   _resident(w_up, layer), _resident(conv_w), _resident(conv_b), _resident(w_down, layer),
                   _resident(g2), _resident(b2)]
    return pl.pallas_call(
        functools.partial(_tail_body, seq // tm, tc),
        grid=(t // tm,),
        in_specs=[tile(mix_a), tile(mix_b), tile(h)] + param_specs,
        out_specs=pl.BlockSpec((tm, D_MODEL), lambda i: (i, 0)),
        out_shape=jax.ShapeDtypeStruct((t, D_MODEL), F32),
        scratch_shapes=[pltpu.VMEM((tm, D_FF), BF16), pltpu.VMEM((HALO + tm, tc), F32),
                        pltpu.VMEM((HALO + tm, tc), F32), pltpu.VMEM((2 * D_FF // tc, HALO, tc), F32)],
        compiler_params=_cparams(("arbitrary",)),
        name="layer_tail",
    )(mix_a, mix_b, h, *params)


def _even_w_in(w):
    a_end = 4 * EVA_PART
    gates = w[:, a_end:a_end + 2 * GDN_HEADS]
    qkv_b = w[:, a_end + 2 * GDN_HEADS:]
    pad = jnp.zeros((w.shape[0], EVB_COLS - EVB_BA - 2 * GDN_HEADS), w.dtype)
    return w[:, :a_end].astype(BF16), jnp.concatenate([qkv_b, gates, pad], axis=1).astype(BF16)


def _odd_w_in(w):
    pad = jnp.zeros((w.shape[0], LANES - GLA_RANK), w.dtype)
    return w[:, :OD_COLS].astype(BF16), jnp.concatenate([w[:, OD_COLS:], pad], axis=1).astype(BF16)


def _even_mixer(h, rel_bias, w_in, conv_w, a_log, dt_bias, norm_w, w_out, bsz, seq):
    w_a, w_b = _even_w_in(w_in)
    proj_a, proj_b = _even_in_proj(h, w_a, w_b, conv_w, seq, PROJ_TM)
    gate_pad = jnp.zeros((1, LANES), F32)
    alog_pad = lax.dynamic_update_slice(gate_pad, a_log[None].astype(F32), (0, GDN_HEADS))
    dtb_pad = lax.dynamic_update_slice(gate_pad, dt_bias[None].astype(F32), (0, GDN_HEADS))
    o_a = _gdn(proj_a, proj_b, alog_pad, dtb_pad, norm_w[None], bsz, seq)
    tiles = _bias_tiles(rel_bias, *_swa_bias_tables(), scale=LOG2E)
    ng, two_blk = len(SWA_CONFIGS), 2 * SWA_BLOCK
    bias = tiles.reshape(SWA_HEADS // 2, 2, 2, ng, SWA_BLOCK, two_blk).transpose(3, 0, 2, 1, 4, 5)
    o_b = _swa(proj_b, bias.reshape(ng, SWA_HEADS // 2, 2, two_blk, two_blk), bsz, seq)
    w_out = w_out.astype(BF16)
    return o_a, o_b, w_out[:EVA_PART], w_out[EVA_PART:]


def _odd_mixer(h, rel_bias, w_in, lam_params, diff_norm_w, w_gate, b_gate, gla_norm_w, w_out,
               lam_init, bsz, seq):
    w_main, w_gd = _odd_w_in(w_in)
    proj, gate_in = _odd_in_proj(h, w_main, w_gd, PROJ_TM)
    blk = min(DIFF_BLOCK, seq)
    bucket, neg = _diff_bias_tables(blk)
    bias = _bias_tiles(rel_bias, bucket, neg, scale=LOG2E, base_bucket=NUM_BUCKETS - 1)
    o_c = _diff_attention(proj, bias, lam_params, diff_norm_w[None], lam_init, bsz, seq, blk)
    w_gate_pad = jnp.concatenate(
        [w_gate, jnp.zeros((LANES - GLA_RANK, w_gate.shape[1]), w_gate.dtype)], axis=0)
    o_d = _gla(proj, gate_in, w_gate_pad, b_gate[None], gla_norm_w[None], bsz, seq)
    diff_v = DIFF_HEADS * 2 * DIFF_DH
    w_out = w_out.astype(BF16)
    return o_c, o_d, w_out[:diff_v], w_out[diff_v:]


def kernel(x, rel_bias, w_in_even, gdn_conv_w, gdn_a_log, gdn_dt_bias, gdn_norm_w, w_out_even,
           w_in_odd, diff_lambda, diff_norm_w, gla_w_gate, gla_b_gate, gla_norm_w, w_out_odd,
           ffn_w_up, ffn_conv_w, ffn_conv_b, ffn_w_down, ln_g, ln_b):
    bsz, seq, d = x.shape
    h = x.reshape(bsz * seq, d)
    w_up, w_down = ffn_w_up.astype(BF16), ffn_w_down.astype(BF16)
    for layer in range(DEPTH):
        i = layer // 2
        if layer % 2 == 0:
            mixed = _even_mixer(h, rel_bias, w_in_even[i], gdn_conv_w[i], gdn_a_log[i], gdn_dt_bias[i],
                                gdn_norm_w[i], w_out_even[i], bsz, seq)
        else:
            lam_init = 0.8 - 0.6 * math.exp(-0.3 * layer)
            mixed = _odd_mixer(h, rel_bias, w_in_odd[i], diff_lambda[i], diff_norm_w[i], gla_w_gate[i],
                               gla_b_gate[i], gla_norm_w[i], w_out_odd[i], lam_init, bsz, seq)
        h = _layer_tail(*mixed[:2], h, *mixed[2:], ln_g[layer, 0][None], ln_b[layer, 0][None],
                        w_up, ffn_conv_w[layer], ffn_conv_b[layer][None], w_down,
                        ln_g[layer, 1][None], ln_b[layer, 1][None], layer, seq)
    return h.reshape(bsz, seq, d)
```

```python
import functools
import math

import numpy as np
import jax
import jax.numpy as jnp
from jax import lax
from jax.experimental import pallas as pl
from jax.experimental.pallas import tpu as pltpu

F32 = jnp.float32
BF16 = jnp.bfloat16
HI = lax.Precision.HIGHEST

D_MODEL = 1024
DEPTH = 2
DEEPNORM_ALPHA = (2 * DEPTH) ** 0.25
LN_EPS = 1e-5
RMS_EPS = 1e-6
NUM_BUCKETS = 32
REL_MAX_DIST = 2048
GDN_HEADS = 6
GDN_D = 128
CHUNK = 64
GDN_GROUP = 256
SWA_CONFIGS = ((128, 1), (512, 4), (2048, 16))
SWA_HEADS = 4
SWA_DH = 64
SWA_BLOCK = 128
SWA_UNROLL = 8
SWA_SPAN = SWA_BLOCK * max(d for _, d in SWA_CONFIGS)
DIFF_HEADS = 4
DIFF_DH = 64
DIFF_BLOCK = 512
LOG2E = math.log2(math.e)
GLA_HEADS = 4
GLA_DK = 64
GLA_DV = 128
GLA_RANK = 16
GLA_TAU = 16.0
GLA_SUB = 16
GLA_GROUP = 256
GLA_MAX_DECAY = 60.0
D_FF = 2816

LANES = 128
SUBLANES = 8
HALO = 8
VMEM_LIMIT = 56 * 1024 * 1024
NEG_BIG = -1e30

EVA_PART = GDN_HEADS * GDN_D
EVB_BA = 2304
EVB_COLS = 2560
EVB_TN = 512
OD_QC, OD_KC, OD_VC = 0, 512, 1024
OD_QD, OD_KD, OD_VD, OD_RD = 1536, 1792, 2048, 2560
OD_COLS = 3072
OD_TN = 768
PROJ_TM = 512
TAIL_TM = 512
FFN_TC = 256


def _cparams(sem):
    return pltpu.CompilerParams(dimension_semantics=sem, vmem_limit_bytes=VMEM_LIMIT)


def _bdot(a, b):
    return jnp.dot(a.astype(BF16), b.astype(BF16), preferred_element_type=F32)


def _dot_nt(a, b):
    return lax.dot_general(a, b, (((1,), (1,)), ((), ())), preferred_element_type=F32)


def _bdot_nt(a, b):
    return lax.dot_general(a.astype(BF16), b.astype(BF16), (((1,), (1,)), ((), ())),
                           preferred_element_type=F32)


def _bdot_tn(a, b):
    return lax.dot_general(a.astype(BF16), b.astype(BF16), (((0,), (0,)), ((), ())),
                           preferred_element_type=F32)


def _sigmoid(x):
    return 1.0 / (1.0 + jnp.exp(-x))


def _silu(x):
    return x * _sigmoid(x)


def _softplus(x):
    return jnp.maximum(x, 0.0) + jnp.log1p(jnp.exp(-jnp.abs(x)))


def _log_sigmoid(x):
    return -_softplus(-x)


def _resident(a, lead=None):
    if lead is None:
        return pl.BlockSpec(a.shape, lambda i: (0,) * a.ndim, pipeline_mode=pl.Buffered(1))
    return pl.BlockSpec((None,) + a.shape[1:], lambda i: (lead,) + (0,) * (a.ndim - 1),
                        pipeline_mode=pl.Buffered(1))


def _odd_in_body(x_ref, w_ref, wg_ref, o_ref, og_ref):
    xb = x_ref[...].astype(BF16)
    for c in range(OD_COLS // OD_TN):
        cols = slice(c * OD_TN, (c + 1) * OD_TN)
        o_ref[:, cols] = _dot_nt(xb, w_ref[cols, :]).astype(o_ref.dtype)
    og_ref[...] = _dot_nt(xb, wg_ref[...])


def _odd_in_proj(x, w, w_gate_in, tm):
    t, k = x.shape
    return pl.pallas_call(
        _odd_in_body,
        grid=(t // tm,),
        in_specs=[pl.BlockSpec((tm, k), lambda i: (i, 0)), _resident(w), _resident(w_gate_in)],
        out_specs=[pl.BlockSpec((tm, OD_COLS), lambda i: (i, 0)), pl.BlockSpec((tm, LANES), lambda i: (i, 0))],
        out_shape=[jax.ShapeDtypeStruct((t, OD_COLS), BF16), jax.ShapeDtypeStruct((t, LANES), F32)],
        compiler_params=_cparams(("parallel",)),
        name="odd_in_proj",
    )(x, w, w_gate_in)


def _rel_bucket_np(dist):
    max_exact = NUM_BUCKETS // 2
    d = np.maximum(dist, 1).astype(np.float32)
    large = max_exact + (np.log(d / max_exact) / math.log(REL_MAX_DIST / max_exact)
                         * (NUM_BUCKETS - max_exact)).astype(np.int32)
    large = np.minimum(large, NUM_BUCKETS - 1)
    return np.where(dist < max_exact, dist, large).astype(np.int32)


def _bias_body(scale, base_bucket, tile_buckets, rb_ref, bucket_ref, neg_ref, o_ref):
    h = pl.program_id(0)
    t = pl.program_id(1)
    base = 0.0 if base_bucket is None else rb_ref[base_bucket, h]
    for tile, present in enumerate(tile_buckets):
        @pl.when(t == tile)
        def _(present=present):
            bucket = bucket_ref[0]
            acc = neg_ref[0]
            for b in present:
                acc = acc + jnp.where(bucket == b, (rb_ref[b, h] - base) * scale, 0.0)
            o_ref[0, 0] = acc


def _bias_tiles(rel_bias, bucket, neg, scale=1.0, base_bucket=None):
    nt, r, c = bucket.shape
    nh = rel_bias.shape[1]
    tile_buckets = tuple(tuple(int(b) for b in np.unique(bucket[t][neg[t] == 0]) if b != base_bucket)
                         for t in range(nt))
    return pl.pallas_call(
        functools.partial(_bias_body, scale, base_bucket, tile_buckets),
        grid=(nh, nt),
        in_specs=[pl.BlockSpec(memory_space=pltpu.SMEM),
                  pl.BlockSpec((1, r, c), lambda h, t: (t, 0, 0)),
                  pl.BlockSpec((1, r, c), lambda h, t: (t, 0, 0))],
        out_specs=pl.BlockSpec((1, 1, r, c), lambda h, t: (h, t, 0, 0)),
        out_shape=jax.ShapeDtypeStruct((nh, nt, r, c), F32),
        compiler_params=_cparams(("parallel", "parallel")),
        name="rel_bias_tiles",
    )(rel_bias, jnp.asarray(bucket), jnp.asarray(neg))


def _causal_conv(cur, w, scratch, carry, at_start):
    width = w.shape[0]
    rows = cur.shape[0]
    scratch[0:HALO, :] = jnp.where(at_start, 0.0, carry[...])
    scratch[HALO:, :] = cur
    carry[...] = cur[rows - HALO:]
    y = w[width - 1:width, :] * cur
    for j in range(width - 1):
        back = width - 1 - j
        y = y + w[j:j + 1, :] * scratch[HALO - back:HALO - back + rows, :]
    return y


def _even_in_body(tiles_per_seq, x_ref, wa_ref, wb_ref, cw_ref, oa_ref, ob_ref, scratch, carry):
    at_start = pl.program_id(0) % tiles_per_seq == 0
    x = x_ref[...].astype(BF16)
    for part in range(3):
        cols = slice(part * EVA_PART, (part + 1) * EVA_PART)
        y = _dot_nt(x, wa_ref[cols, :])
        c = _silu(_causal_conv(y, cw_ref[:, cols], scratch, carry.at[part], at_start))
        for hd in range(GDN_HEADS):
            ch = c[:, hd * GDN_D:(hd + 1) * GDN_D]
            if part < 2:
                inv = lax.rsqrt(jnp.sum(ch * ch, axis=-1, keepdims=True) + RMS_EPS)
                ch = ch * (inv * GDN_D ** -0.5 if part == 0 else inv)
            lo = part * EVA_PART + hd * GDN_D
            oa_ref[:, lo:lo + GDN_D] = ch.astype(oa_ref.dtype)
    z_cols = slice(3 * EVA_PART, 4 * EVA_PART)
    oa_ref[:, z_cols] = _dot_nt(x, wa_ref[z_cols, :]).astype(oa_ref.dtype)
    for c in range(EVB_COLS // EVB_TN):
        cols = slice(c * EVB_TN, (c + 1) * EVB_TN)
        ob_ref[:, cols] = _dot_nt(x, wb_ref[cols, :])


def _even_in_proj(x, w_a, w_b, conv_w, seq, tm):
    t, k = x.shape
    return pl.pallas_call(
        functools.partial(_even_in_body, seq // tm),
        grid=(t // tm,),
        in_specs=[pl.BlockSpec((tm, k), lambda i: (i, 0)),
                  _resident(w_a), _resident(w_b), _resident(conv_w)],
        out_specs=[pl.BlockSpec((tm, w_a.shape[0]), lambda i: (i, 0)),
                   pl.BlockSpec((tm, EVB_COLS), lambda i: (i, 0))],
        out_shape=[jax.ShapeDtypeStruct((t, w_a.shape[0]), BF16), jax.ShapeDtypeStruct((t, EVB_COLS), F32)],
        scratch_shapes=[pltpu.VMEM((HALO + tm, EVA_PART), F32), pltpu.VMEM((3, HALO, EVA_PART), F32)],
        compiler_params=_cparams(("arbitrary",)),
        name="even_in_proj",
    )(x, w_a, w_b, conv_w)


def _gdn_body(q_ref, k_ref, v_ref, z_ref, ba_ref, alog_ref, dtb_ref, nw_ref, o_ref, state):
    n = pl.program_id(1)

    @pl.when(n == 0)
    def _():
        state[...] = jnp.zeros_like(state)

    grp = GDN_GROUP
    nc = grp // CHUNK
    nh = GDN_HEADS
    hs = range(nh)
    ri = lax.broadcasted_iota(jnp.int32, (grp, grp), 0)
    ci = lax.broadcasted_iota(jnp.int32, (grp, grp), 1)
    same = (ri // CHUNK) == (ci // CHUNK)
    incl = same & (ri >= ci)
    strict = same & (ri > ci)
    eye = (ri == ci).astype(F32)
    tri = incl.astype(BF16)

    def chunk_local(rows):
        ba = ba_ref[rows, :]
        beta_all = _sigmoid(ba)
        g_all = -jnp.exp(alog_ref[...]) * _softplus(ba + dtb_ref[...])
        g_hi = g_all.astype(BF16)
        rem = g_all - g_hi.astype(F32)
        g_mid = rem.astype(BF16)
        g_lo = (rem - g_mid.astype(F32)).astype(BF16)
        gam_all = (jnp.dot(tri, g_hi, preferred_element_type=F32)
                   + jnp.dot(tri, g_mid, preferred_element_type=F32)
                   + jnp.dot(tri, g_lo, preferred_element_type=F32))
        gam_rows = gam_all.T

        q = [q_ref[rows, h * GDN_D:(h + 1) * GDN_D] for h in hs]
        k = [k_ref[rows, h * GDN_D:(h + 1) * GDN_D] for h in hs]
        v = [v_ref[rows, h * GDN_D:(h + 1) * GDN_D] for h in hs]
        gam = [jnp.broadcast_to(gam_all[:, nh + h:nh + h + 1], (grp, GDN_D)) for h in hs]
        bcol = [beta_all[:, h:h + 1] for h in hs]
        decay, x, inv = [], [], []
        for h in hs:
            diff = gam[h][:, 0:1] - gam_rows[nh + h:nh + h + 1, :]
            decay.append(jnp.where(incl, jnp.exp(jnp.where(incl, diff, 0.0)), 0.0))
            kk = _bdot_nt(k[h], k[h])
            x.append(-jnp.where(strict, bcol[h] * kk * decay[h], 0.0))
            inv.append(eye + x[h])
        for _ in range(5):
            for h in hs:
                x[h] = _bdot(x[h], x[h])
                inv[h] = inv[h] + _bdot(inv[h], x[h])
        u, w, qk, q_dec, k_dec, g_last = [], [], [], [], [], []
        for h in hs:
            eg = jnp.exp(gam[h])
            uw = _bdot(inv[h], jnp.concatenate([v[h] * bcol[h], k[h] * (bcol[h] * eg)], axis=1))
            u.append(uw[:, :GDN_D])
            w.append(uw[:, GDN_D:])
            qk.append(_bdot_nt(q[h], k[h]) * decay[h])
            q_dec.append(q[h] * eg)
            kd, gl = [], []
            for c in range(nc):
                last = gam[h][(c + 1) * CHUNK - 1:(c + 1) * CHUNK, :]
                kd.append(k[h][c * CHUNK:(c + 1) * CHUNK] * jnp.exp(last - gam[h][c * CHUNK:(c + 1) * CHUNK]))
                gl.append(jnp.exp(last))
            k_dec.append(kd)
            g_last.append(gl)
        return u, w, qk, q_dec, k_dec, g_last

    def recurrence(rows, local, s):
        u, w, qk, q_dec, k_dec, g_last = local
        q_s = [[] for _ in hs]
        delta = [[] for _ in hs]
        for c in range(nc):
            sl = slice(c * CHUNK, (c + 1) * CHUNK)
            for h in hs:
                r = _bdot(jnp.concatenate([w[h][sl], q_dec[h][sl]], axis=0), s[h])
                d = u[h][sl] - r[:CHUNK]
                q_s[h].append(r[CHUNK:])
                delta[h].append(d)
                s[h] = g_last[h][c] * s[h] + _bdot_tn(k_dec[h][c], d)
        for h in hs:
            o = jnp.concatenate(q_s[h], axis=0) + _bdot(qk[h], jnp.concatenate(delta[h], axis=0))
            o = o * lax.rsqrt(jnp.mean(o * o, axis=-1, keepdims=True) + RMS_EPS) * nw_ref[...]
            z = z_ref[rows, h * GDN_D:(h + 1) * GDN_D].astype(F32)
            o_ref[rows, h * GDN_D:(h + 1) * GDN_D] = (o * _silu(z)).astype(o_ref.dtype)
        return s

    rows = slice(0, grp)
    s = recurrence(rows, chunk_local(rows), [state[h] for h in hs])
    for h in hs:
        state[h] = s[h]


def _gdn(proj_a, proj_b, alog_pad, dtb_pad, norm_w, bsz, seq):
    t = proj_a.shape[0]
    rows = GDN_GROUP
    spb = seq // rows
    width = GDN_HEADS * GDN_D

    def at(col):
        return lambda b, n: (b * spb + n, col)

    return pl.pallas_call(
        _gdn_body,
        grid=(bsz, spb),
        in_specs=[pl.BlockSpec((rows, width), at(0)),
                  pl.BlockSpec((rows, width), at(1)),
                  pl.BlockSpec((rows, width), at(2)),
                  pl.BlockSpec((rows, width), at(3)),
                  pl.BlockSpec((rows, LANES), at(EVB_BA // LANES)),
                  pl.BlockSpec((1, LANES), lambda b, n: (0, 0)),
                  pl.BlockSpec((1, LANES), lambda b, n: (0, 0)),
                  pl.BlockSpec((1, LANES), lambda b, n: (0, 0))],
        out_specs=pl.BlockSpec((rows, width), at(0)),
        out_shape=jax.ShapeDtypeStruct((t, width), BF16),
        scratch_shapes=[pltpu.VMEM((GDN_HEADS, GDN_D, GDN_D), F32)],
        compiler_params=_cparams(("parallel", "arbitrary")),
        name="gdn",
    )(proj_a, proj_a, proj_a, proj_a, proj_b, alog_pad, dtb_pad, norm_w)


def _swa_bias_tables():
    qi = np.arange(SWA_BLOCK)[:, None] + SWA_BLOCK
    kj = np.arange(2 * SWA_BLOCK)[None, :]
    rel = qi - kj
    buckets, negs = [], []
    for has_prev in (True, False):
        for window, dilation in SWA_CONFIGS:
            valid = (rel >= 0) & (rel <= window // dilation) & (has_prev | (kj >= SWA_BLOCK))
            buckets.append(_rel_bucket_np(np.maximum(rel, 0) * dilation))
            negs.append(np.where(valid, 0.0, NEG_BIG))
    return np.stack(buckets).astype(np.int32), np.stack(negs).astype(np.float32)


def _swa_body(*refs):
    ng = len(SWA_CONFIGS)
    ins, bias_ref, o_ref = refs[:5 * ng], refs[5 * ng], refs[5 * ng + 1]
    scratch = refs[5 * ng + 2:]
    kbufs, vbufs, o_scr, lse_scr = scratch[:ng], scratch[ng:2 * ng], scratch[2 * ng], scratch[2 * ng + 1]
    j = pl.program_id(1)
    blk = SWA_BLOCK
    first_head = lax.broadcasted_iota(jnp.int32, (blk, LANES), 1) < SWA_DH

    for g, (_, d) in enumerate(SWA_CONFIGS):
        q_ref, kc_ref, kp_ref, vc_ref, vp_ref = ins[5 * g:5 * g + 5]
        kbuf, vbuf = kbufs[g], vbufs[g]
        halo = blk * d
        kbuf[0:halo, :] = kp_ref[...]
        kbuf[halo:, :] = kc_ref[...]
        vbuf[0:halo, :] = vp_ref[...]
        vbuf[halo:, :] = vc_ref[...]

        def unit(u, carry, g=g, d=d, halo=halo, q_ref=q_ref, kbuf=kbuf, vbuf=vbuf):
            base = (u // d) * halo + u % d
            q = q_ref[pl.ds(base, blk, stride=d), :] * (SWA_DH ** -0.5 * LOG2E)
            k = kbuf[pl.ds(base, 2 * blk, stride=d), :]
            v = vbuf[pl.ds(base, 2 * blk, stride=d), :]
            lhs = jnp.concatenate([jnp.where(first_head, q, 0.0), jnp.where(first_head, 0.0, q)], axis=0)
            no_prev = jnp.where((j == 0) & (u < d), 1, 0)
            s = _bdot_nt(lhs, k) + bias_ref[g, 0, no_prev]
            m = jnp.max(s, axis=-1, keepdims=True)
            p = jnp.exp2(s - m)
            l = jnp.sum(p, axis=-1, keepdims=True)
            o2 = _bdot(p, v) * (1.0 / l)
            lse2 = m + jnp.log2(l)
            o_scr[g, pl.ds(base, blk, stride=d), :] = jnp.where(first_head, o2[:blk], o2[blk:])
            lse_scr[g, pl.ds(base, blk, stride=d), :] = jnp.where(first_head, lse2[:blk], lse2[blk:])
            return carry

        lax.fori_loop(0, SWA_SPAN // blk, unit, 0, unroll=SWA_UNROLL)

    def combine(c, carry):
        rows = pl.ds(pl.multiple_of(c * 2 * blk, 2 * blk), 2 * blk)
        lse = [lse_scr[g, rows, :] for g in range(ng)]
        m = functools.reduce(jnp.maximum, lse)
        e = [jnp.exp2(x - m) for x in lse]
        den = functools.reduce(lambda x, y: x + y, e)
        o_ref[rows, :] = functools.reduce(
            lambda x, y: x + y, [(e[g] / den) * o_scr[g, rows, :] for g in range(ng)]).astype(o_ref.dtype)
        return carry

    lax.fori_loop(0, SWA_SPAN // (2 * blk), combine, 0)


def _swa(proj, bias, bsz, seq):
    t = proj.shape[0]
    ng = len(SWA_CONFIGS)
    nspan = seq // SWA_SPAN
    npair = SWA_HEADS * SWA_DH // LANES
    group_cols = SWA_HEADS * SWA_DH // LANES
    in_specs, scratch_k = [], []
    for g, (_, d) in enumerate(SWA_CONFIGS):
        halo = SWA_BLOCK * d
        per_span = SWA_SPAN // halo

        def cur(which, g=g):
            col = (which * ng + g) * group_cols
            return lambda b, j, p: (b * nspan + j, col + p)

        def prev(which, g=g, per_span=per_span):
            col = (which * ng + g) * group_cols
            return lambda b, j, p: (jnp.maximum((b * nspan + j) * per_span - 1, 0), col + p)

        in_specs += [pl.BlockSpec((SWA_SPAN, LANES), cur(0)),
                     pl.BlockSpec((SWA_SPAN, LANES), cur(1)), pl.BlockSpec((halo, LANES), prev(1)),
                     pl.BlockSpec((SWA_SPAN, LANES), cur(2)), pl.BlockSpec((halo, LANES), prev(2))]
        scratch_k.append(pltpu.VMEM((halo + SWA_SPAN, LANES), F32))
    in_specs.append(pl.BlockSpec((ng, 1, 2, 2 * SWA_BLOCK, 2 * SWA_BLOCK), lambda b, j, p: (0, p, 0, 0, 0)))
    return pl.pallas_call(
        _swa_body,
        grid=(bsz, nspan, npair),
        in_specs=in_specs,
        out_specs=pl.BlockSpec((SWA_SPAN, LANES), lambda b, j, p: (b * nspan + j, p)),
        out_shape=jax.ShapeDtypeStruct((t, npair * LANES), BF16),
        scratch_shapes=scratch_k + scratch_k + [pltpu.VMEM((ng, SWA_SPAN, LANES), F32)] * 2,
        compiler_params=_cparams(("parallel", "parallel", "parallel")),
        name="swa",
    )(*([proj] * (5 * ng)), bias)


def _layer_norm(y, g, b):
    mu = jnp.mean(y, axis=-1, keepdims=True)
    yc = y - mu
    var = jnp.mean(yc * yc, axis=-1, keepdims=True)
    return yc * lax.rsqrt(var + LN_EPS) * g + b


def _diff_bias_tables(blk):
    buckets = _rel_bucket_np(np.arange(2 * REL_MAX_DIST))
    far = int(np.max(np.nonzero(buckets != NUM_BUCKETS - 1)[0])) + 1
    nb = -(-(far + blk - 1) // blk)
    kk = np.arange(blk)[:, None]
    qq = np.arange(blk)[None, :]
    dist = np.stack([t * blk + qq - kk for t in range(nb + 1)])
    bucket = _rel_bucket_np(np.maximum(dist, 0))
    neg = np.where(dist >= 0, 0.0, NEG_BIG).astype(np.float32)
    return bucket, neg


def _diff_body(blk, nb, lam_init, q_ref, qn_ref, k_ref, v_ref, bias_ref, lam_ref, nw_ref, o_ref,
               vt, acc1, acc2, s_a, s_b):
    qi = pl.program_id(2)
    dv = 2 * DIFF_DH
    seq = k_ref.shape[0]

    @pl.when(qi == 0)
    def _():
        vt[dv:, :] = jnp.ones((vt.shape[0] - dv, seq), BF16)

        def fill(c, carry):
            st = pl.multiple_of(c * blk, blk)
            vt[0:dv, pl.ds(st, blk)] = v_ref[pl.ds(st, blk), :].astype(F32).T.astype(BF16)
            return carry

        lax.fori_loop(0, seq // blk, fill, 0)

    feature = lax.broadcasted_iota(jnp.int32, (dv, blk), 0)

    def components(ref):
        q_t = (ref[...].astype(F32) * (DIFF_DH ** -0.5 * LOG2E)).T
        return (jnp.where(feature < DIFF_DH, q_t, 0.0).astype(BF16),
                jnp.where(feature >= DIFF_DH, q_t, 0.0).astype(BF16))

    qs = components(q_ref)
    qs_next = components(qn_ref)
    accs = (acc1, acc2)
    for acc in accs:
        acc[...] = jnp.zeros_like(acc)

    last = pl.num_programs(2) - 1

    def key_rows(kj):
        return pl.ds(pl.multiple_of(jnp.minimum(kj, last) * blk, blk), blk)

    def scores(kj, dst, queries=qs):
        k = k_ref[key_rows(kj), :]
        for c, qc in enumerate(queries):
            dst[c] = jnp.dot(k, qc, preferred_element_type=F32)

    def consume(biased, kj, src, ms):
        vtb = vt[:, key_rows(kj)]
        out = []
        for c, (m, acc) in enumerate(zip(ms, accs)):
            s = src[c]
            if biased:
                s = s + bias_ref[0, jnp.minimum(qi - kj, nb)]
            m_new = jnp.maximum(m, jnp.max(s, axis=0, keepdims=True))
            p = jnp.exp2(s - m_new).astype(BF16)
            acc[...] = jnp.exp2(m - m_new) * acc[...] + jnp.dot(vtb, p, preferred_element_type=F32)
            out.append(m_new)
        return tuple(out)

    nblocks = qi + 1
    trips = nblocks // 2
    odd = nblocks % 2 == 1
    far_trips = jnp.maximum(qi - nb + 1, 0) // 2

    def pair(biased, t, ms):
        kj = 2 * t
        scores(kj + 1, s_b)
        ms = consume(biased, kj, s_a, ms)
        if biased:
            hand_off = (t == trips - 1) & jnp.logical_not(odd)
            scores(jnp.where(hand_off, 0, kj + 2), s_a,
                   tuple(jnp.where(hand_off, qn, qc) for qc, qn in zip(qs, qs_next)))
        else:
            scores(kj + 2, s_a)
        return consume(biased, kj + 1, s_b, ms)

    @pl.when(qi == 0)
    def _():
        scores(0, s_a)

    m0 = jnp.full((1, blk), NEG_BIG, F32)
    ms = lax.fori_loop(0, far_trips, functools.partial(pair, False), (m0, m0))
    ms = lax.fori_loop(far_trips, trips, functools.partial(pair, True), ms)

    def finish():
        lp = lam_ref[...]
        lam = (jnp.exp(jnp.sum(lp[0:1] * lp[1:2], axis=-1, keepdims=True))
               - jnp.exp(jnp.sum(lp[2:3] * lp[3:4], axis=-1, keepdims=True)) + lam_init)
        a1, a2 = acc1[...], acc2[...]
        o_t = a1[:dv] * (1.0 / a1[dv:dv + 1]) - a2[:dv] * (lam / a2[dv:dv + 1])
        o = o_t.T
        o = o * lax.rsqrt(jnp.mean(o * o, axis=-1, keepdims=True) + RMS_EPS) * nw_ref[...]
        o_ref[...] = (o * (1.0 - lam_init)).astype(o_ref.dtype)

    @pl.when(odd)
    def _():
        consume(True, qi, s_a, ms)
        scores(0, s_a, qs_next)
        finish()

    @pl.when(jnp.logical_not(odd))
    def _():
        finish()


def _diff_attention(proj, bias, lam_params, norm_w, lam_init, bsz, seq, blk):
    t = proj.shape[0]
    nq = seq // blk
    nt = bias.shape[1]
    nb = nt - 1
    width = 2 * DIFF_DH
    ones_rows = 2 * SUBLANES
    return pl.pallas_call(
        functools.partial(_diff_body, blk, nb, lam_init),
        grid=(bsz, DIFF_HEADS, nq),
        in_specs=[pl.BlockSpec((blk, width), lambda b, h, i: (b * nq + i, OD_QC // width + h)),
                  pl.BlockSpec((blk, width),
                               lambda b, h, i: (b * nq + jnp.minimum(i + 1, nq - 1), OD_QC // width + h)),
                  pl.BlockSpec((seq, width), lambda b, h, i: (b, OD_KC // width + h)),
                  pl.BlockSpec((seq, width), lambda b, h, i: (b, OD_VC // width + h)),
                  pl.BlockSpec((1, nt, blk, blk), lambda b, h, i: (h, 0, 0, 0)),
                  pl.BlockSpec((4, DIFF_DH), lambda b, h, i: (0, 0)),
                  pl.BlockSpec((1, width), lambda b, h, i: (0, 0))],
        out_specs=pl.BlockSpec((blk, width), lambda b, h, i: (b * nq + i, h)),
        out_shape=jax.ShapeDtypeStruct((t, DIFF_HEADS * width), BF16),
        scratch_shapes=[pltpu.VMEM((width + ones_rows, seq), BF16),
                        pltpu.VMEM((width + ones_rows, blk), F32), pltpu.VMEM((width + ones_rows, blk), F32),
                        pltpu.VMEM((2, blk, blk), F32), pltpu.VMEM((2, blk, blk), F32)],
        compiler_params=_cparams(("arbitrary", "arbitrary", "arbitrary")),
        name="diff_attn",
    )(proj, proj, proj, proj, bias, lam_params, norm_w)


def _gla_body(q_ref, k_ref, gd_ref, v_ref, r_ref, wg_ref, bg_ref, nw_ref, o_ref, state, part):
    n = pl.program_id(1)

    @pl.when(n == 0)
    def _():
        state[...] = jnp.zeros_like(state)

    rows = GLA_GROUP
    nc = rows // CHUNK
    npair = GLA_HEADS // 2
    gate = jnp.dot(gd_ref[...], wg_ref[...], precision=HI, preferred_element_type=F32) + bg_ref[...]
    log_a = _log_sigmoid(gate) * (1.0 / GLA_TAU)

    ri = lax.broadcasted_iota(jnp.int32, (rows, rows), 0)
    ci = lax.broadcasted_iota(jnp.int32, (rows, rows), 1)
    causal = ((ri // CHUNK) == (ci // CHUNK)) & (ri >= ci)
    tri = causal.astype(BF16)
    hi = log_a.astype(BF16)
    rem = log_a - hi.astype(F32)
    mid = rem.astype(BF16)
    lo = (rem - mid.astype(F32)).astype(BF16)
    b_all = (jnp.dot(tri, hi, preferred_element_type=F32) + jnp.dot(tri, mid, preferred_element_type=F32)
             + jnp.dot(tri, lo, preferred_element_type=F32))

    lane = lax.broadcasted_iota(jnp.int32, (rows, LANES), 1)
    lane_c = lax.broadcasted_iota(jnp.int32, (CHUNK, LANES), 1)
    head_lanes = (lane < GLA_DK, lane >= GLA_DK)
    chunk_lanes = (lane_c < GLA_DK, lane_c >= GLA_DK)
    pairs = []
    for p in range(npair):
        cols = slice(p * LANES, (p + 1) * LANES)
        pairs.append(dict(
            b=b_all[:, cols],
            q=q_ref[:, cols].astype(F32) * GLA_DK ** -0.5,
            k=k_ref[:, cols].astype(F32),
            v=[v_ref[:, (2 * p + hd) * GLA_DV:(2 * p + hd + 1) * GLA_DV].astype(F32) for hd in range(2)]))

    for pr in pairs:
        pr["q_dec"] = pr["q"] * jnp.exp(pr["b"])
    for c in range(nc):
        sl = slice(c * CHUNK, (c + 1) * CHUNK)
        for p, pr in enumerate(pairs):
            bc = pr["b"][sl]
            b_last = bc[CHUNK - 1:CHUNK]
            k_dec = pr["k"][sl] * jnp.exp(b_last - bc)
            e_last = jnp.exp(b_last)
            for hd in range(2):
                h = 2 * p + hd
                st = state[h]
                part[h, sl, :] = _bdot_nt(jnp.where(chunk_lanes[hd], pr["q_dec"][sl], 0.0), st)
                state[h] = st * e_last + _bdot_tn(pr["v"][hd][sl], k_dec)

    def finish(intra):
        for h in range(GLA_HEADS):
            o = part[h] + intra[h]
            o = o * lax.rsqrt(jnp.mean(o * o, axis=-1, keepdims=True) + RMS_EPS) * nw_ref[...]
            gate_r = _silu(r_ref[:, h * GLA_DV:(h + 1) * GLA_DV].astype(F32))
            o_ref[:, h * GLA_DV:(h + 1) * GLA_DV] = (o * gate_r).astype(o_ref.dtype)

    def intra_whole_chunk():
        out = []
        for pr in pairs:
            k_inv = pr["k"] * jnp.exp(jnp.minimum(-pr["b"], GLA_MAX_DECAY))
            for hd in range(2):
                a = _bdot_nt(jnp.where(head_lanes[hd], pr["q_dec"], 0.0), k_inv)
                out.append(_bdot(jnp.where(causal, a, 0.0), pr["v"][hd]))
        return out

    def intra_exact():
        out = []
        for pr in pairs:
            off = _gla_intra_off_diagonal(pr["q"], pr["k"], pr["b"], pr["v"], nc)
            diag = _gla_intra_diagonal(pr["q"], pr["k"], pr["b"], pr["v"])
            out += [off[hd] + diag[hd] for hd in range(2)]
        return out

    chunk_decay = jnp.max(-b_all.reshape(nc, CHUNK, npair * LANES)[:, CHUNK - 1:CHUNK, :])

    @pl.when(chunk_decay <= GLA_MAX_DECAY)
    def _():
        finish(intra_whole_chunk())

    @pl.when(chunk_decay > GLA_MAX_DECAY)
    def _():
        finish(intra_exact())


def _gla_intra_off_diagonal(q, k, b, vs, nc):
    per_chunk = CHUNK // GLA_SUB
    lane = lax.broadcasted_iota(jnp.int32, (GLA_SUB, LANES), 1)
    sub_mask = (lane < GLA_DK, lane >= GLA_DK)
    kcol = lax.broadcasted_iota(jnp.int32, (GLA_SUB, CHUNK), 1)
    outs = ([], [])
    for c in range(nc):
        sl = slice(c * CHUNK, (c + 1) * CHUNK)
        bc, qc, kc = b[sl], q[sl], k[sl]
        a_rows = [[jnp.zeros((GLA_SUB, CHUNK), F32)] for _ in vs]
        for blk in range(1, per_chunk):
            r0 = blk * GLA_SUB
            bref = bc[r0:r0 + 1]
            qs = qc[r0:r0 + GLA_SUB] * jnp.exp(bc[r0:r0 + GLA_SUB] - bref)
            ks = kc * jnp.exp(jnp.minimum(bref - bc, 0.0))
            for hd in range(2):
                a = _bdot_nt(jnp.where(sub_mask[hd], qs, 0.0), ks)
                a_rows[hd].append(jnp.where(kcol < r0, a, 0.0))
        for hd in range(2):
            outs[hd].append(_bdot(jnp.concatenate(a_rows[hd], axis=0), vs[hd][sl]))
    return [jnp.concatenate(o, axis=0) for o in outs]


def _gla_intra_diagonal(q, k, b, vs):
    rows = q.shape[0]
    nsub = rows // GLA_SUB
    b3 = b.reshape(nsub, GLA_SUB, LANES)
    q3 = q.reshape(nsub, GLA_SUB, LANES)
    k3 = k.reshape(nsub, GLA_SUB, LANES)
    v3 = [v.reshape(nsub, GLA_SUB, LANES) for v in vs]
    row3 = lax.broadcasted_iota(jnp.int32, (nsub, GLA_SUB, LANES), 1)
    lane3 = lax.broadcasted_iota(jnp.int32, (nsub, GLA_SUB, LANES), 2)
    rowc = lax.broadcasted_iota(jnp.int32, (nsub, GLA_SUB, 1), 1)
    o3 = [jnp.zeros((nsub, GLA_SUB, LANES), F32) for _ in vs]
    for jj in range(GLA_SUB):
        e = jnp.exp(jnp.where(row3 >= jj, b3 - b3[:, jj:jj + 1, :], 0.0))
        t = q3 * k3[:, jj:jj + 1, :] * e
        w_all = jnp.sum(t, axis=-1, keepdims=True)
        w_a = jnp.sum(jnp.where(lane3 < GLA_DK, t, 0.0), axis=-1, keepdims=True)
        for hd, w in enumerate((w_a, w_all - w_a)):
            o3[hd] = o3[hd] + jnp.where(rowc >= jj, w, 0.0) * v3[hd][:, jj:jj + 1, :]
    return [o.reshape(rows, LANES) for o in o3]


def _gla(proj, gate_in, w_gate_pad, b_gate, norm_w, bsz, seq):
    t = proj.shape[0]
    rows = GLA_GROUP
    spb = seq // rows
    qk_w = GLA_HEADS * GLA_DK
    v_w = GLA_HEADS * GLA_DV

    def at(col):
        return lambda b, n: (b * spb + n, col)

    def whole(a):
        return pl.BlockSpec(a.shape, lambda b, n: (0, 0))

    return pl.pallas_call(
        _gla_body,
        grid=(bsz, spb),
        in_specs=[pl.BlockSpec((rows, qk_w), at(OD_QD // qk_w)),
                  pl.BlockSpec((rows, qk_w), at(OD_KD // qk_w)),
                  pl.BlockSpec((rows, LANES), at(0)),
                  pl.BlockSpec((rows, v_w), at(OD_VD // v_w)),
                  pl.BlockSpec((rows, v_w), at(OD_RD // v_w)),
                  whole(w_gate_pad), whole(b_gate), whole(norm_w)],
        out_specs=pl.BlockSpec((rows, v_w), at(0)),
        out_shape=jax.ShapeDtypeStruct((t, v_w), BF16),
        scratch_shapes=[pltpu.VMEM((GLA_HEADS, GLA_DV, LANES), F32), pltpu.VMEM((GLA_HEADS, rows, GLA_DV), F32)],
        compiler_params=_cparams(("parallel", "arbitrary")),
        name="gla",
    )(proj, proj, gate_in, proj, proj, w_gate_pad, b_gate, norm_w)


def _tail_body(tiles_per_seq, tc, ma_ref, mb_ref, h_ref, woa_ref, wob_ref,
               g1_ref, b1_ref, wu_ref, cw_ref, cb_ref, wd_ref, g2_ref, b2_ref, y_ref,
               act, scratch_g, scratch_v, carry):
    at_start = pl.program_id(0) % tiles_per_seq == 0
    mix = (jnp.dot(ma_ref[...], woa_ref[...], preferred_element_type=F32)
           + jnp.dot(mb_ref[...], wob_ref[...], preferred_element_type=F32))
    x = _layer_norm(DEEPNORM_ALPHA * h_ref[...] + mix, g1_ref[...], b1_ref[...])
    xb = x.astype(BF16)
    nchunk = D_FF // tc

    def branch(idx, scratch):
        lo = idx * tc
        cur = jnp.dot(xb, wu_ref[:, lo:lo + tc], preferred_element_type=F32)
        return _causal_conv(cur, cw_ref[:, lo:lo + tc], scratch, carry.at[idx], at_start) + cb_ref[:, lo:lo + tc]

    for c in range(nchunk):
        gate = branch(c, scratch_g)
        val = branch(nchunk + c, scratch_v)
        act[:, c * tc:(c + 1) * tc] = (_silu(gate) * val).astype(BF16)
    ffn = jnp.dot(act[...], wd_ref[...], preferred_element_type=F32)
    y_ref[...] = _layer_norm(DEEPNORM_ALPHA * x + ffn, g2_ref[...], b2_ref[...])


def _layer_tail(mix_a, mix_b, h, w_out_a, w_out_b, g1, b1, w_up, conv_w, conv_b, w_down, g2, b2,
                layer, seq):
    t = h.shape[0]
    tm, tc = TAIL_TM, FFN_TC

    def tile(a):
        return pl.BlockSpec((tm, a.shape[1]), lambda i: (i, 0))

    params = (w_out_a, w_out_b, g1, b1, w_up, conv_w, conv_b, w_down, g2, b2)
    param_specs = [_resident(w_out_a), _resident(w_out_b), _resident(g1), _resident(b1),
                   _resident(w_up, layer), _resident(conv_w), _resident(conv_b), _resident(w_down, layer),
                   _resident(g2), _resident(b2)]
    return pl.pallas_call(
        functools.partial(_tail_body, seq // tm, tc),
        grid=(t // tm,),
        in_specs=[tile(mix_a), tile(mix_b), tile(h)] + param_specs,
        out_specs=pl.BlockSpec((tm, D_MODEL), lambda i: (i, 0)),
        out_shape=jax.ShapeDtypeStruct((t, D_MODEL), F32),
        scratch_shapes=[pltpu.VMEM((tm, D_FF), BF16), pltpu.VMEM((HALO + tm, tc), F32),
                        pltpu.VMEM((HALO + tm, tc), F32), pltpu.VMEM((2 * D_FF // tc, HALO, tc), F32)],
        compiler_params=_cparams(("arbitrary",)),
        name="layer_tail",
    )(mix_a, mix_b, h, *params)


def _even_w_in(w):
    w_t = w.T
    a_end = 4 * EVA_PART
    gates = w_t[a_end:a_end + 2 * GDN_HEADS]
    qkv_b = w_t[a_end + 2 * GDN_HEADS:]
    pad = jnp.zeros((EVB_COLS - EVB_BA - 2 * GDN_HEADS, w.shape[0]), w.dtype)
    return w_t[:a_end].astype(BF16), jnp.concatenate([qkv_b, gates, pad], axis=0).astype(BF16)


def _odd_w_in(w):
    w_t = w.T
    pad = jnp.zeros((LANES - GLA_RANK, w.shape[0]), w.dtype)
    return w_t[:OD_COLS].astype(BF16), jnp.concatenate([w_t[OD_COLS:], pad], axis=0).astype(BF16)


def _even_mixer(h, rel_bias, w_in, conv_w, a_log, dt_bias, norm_w, w_out, bsz, seq):
    w_a, w_b = _even_w_in(w_in)
    proj_a, proj_b = _even_in_proj(h, w_a, w_b, conv_w, seq, PROJ_TM)
    gate_pad = jnp.zeros((1, LANES), F32)
    alog_pad = lax.dynamic_update_slice(gate_pad, a_log[None].astype(F32), (0, GDN_HEADS))
    dtb_pad = lax.dynamic_update_slice(gate_pad, dt_bias[None].astype(F32), (0, GDN_HEADS))
    o_a = _gdn(proj_a, proj_b, alog_pad, dtb_pad, norm_w[None], bsz, seq)
    tiles = _bias_tiles(rel_bias, *_swa_bias_tables(), scale=LOG2E)
    ng, two_blk = len(SWA_CONFIGS), 2 * SWA_BLOCK
    bias = tiles.reshape(SWA_HEADS // 2, 2, 2, ng, SWA_BLOCK, two_blk).transpose(3, 0, 2, 1, 4, 5)
    o_b = _swa(proj_b, bias.reshape(ng, SWA_HEADS // 2, 2, two_blk, two_blk), bsz, seq)
    w_out = w_out.astype(BF16)
    return o_a, o_b, w_out[:EVA_PART], w_out[EVA_PART:]


def _odd_mixer(h, rel_bias, w_in, lam_params, diff_norm_w, w_gate, b_gate, gla_norm_w, w_out,
               lam_init, bsz, seq):
    w_main, w_gd = _odd_w_in(w_in)
    proj, gate_in = _odd_in_proj(h, w_main, w_gd, PROJ_TM)
    blk = min(DIFF_BLOCK, seq)
    bucket, neg = _diff_bias_tables(blk)
    bias = _bias_tiles(rel_bias, bucket, neg, scale=LOG2E, base_bucket=NUM_BUCKETS - 1)
    o_c = _diff_attention(proj, bias, lam_params, diff_norm_w[None], lam_init, bsz, seq, blk)
    w_gate_pad = jnp.concatenate(
        [w_gate, jnp.zeros((LANES - GLA_RANK, w_gate.shape[1]), w_gate.dtype)], axis=0)
    o_d = _gla(proj, gate_in, w_gate_pad, b_gate[None], gla_norm_w[None], bsz, seq)
    diff_v = DIFF_HEADS * 2 * DIFF_DH
    w_out = w_out.astype(BF16)
    return o_c, o_d, w_out[:diff_v], w_out[diff_v:]


def kernel(x, rel_bias, w_in_even, gdn_conv_w, gdn_a_log, gdn_dt_bias, gdn_norm_w, w_out_even,
           w_in_odd, diff_lambda, diff_norm_w, gla_w_gate, gla_b_gate, gla_norm_w, w_out_odd,
           ffn_w_up, ffn_conv_w, ffn_conv_b, ffn_w_down, ln_g, ln_b):
    bsz, seq, d = x.shape
    h = x.reshape(bsz * seq, d)
    w_up, w_down = ffn_w_up.astype(BF16), ffn_w_down.astype(BF16)
    for layer in range(DEPTH):
        i = layer // 2
        if layer % 2 == 0:
            mixed = _even_mixer(h, rel_bias, w_in_even[i], gdn_conv_w[i], gdn_a_log[i], gdn_dt_bias[i],
                                gdn_norm_w[i], w_out_even[i], bsz, seq)
        else:
            lam_init = 0.8 - 0.6 * math.exp(-0.3 * layer)
            mixed = _odd_mixer(h, rel_bias, w_in_odd[i], diff_lambda[i], diff_norm_w[i], gla_w_gate[i],
                               gla_b_gate[i], gla_norm_w[i], w_out_odd[i], lam_init, bsz, seq)
        h = _layer_tail(*mixed[:2], h, *mixed[2:], ln_g[layer, 0][None], ln_b[layer, 0][None],
                        w_up, ffn_conv_w[layer], ffn_conv_b[layer][None], w_down,
                        ln_g[layer, 1][None], ln_b[layer, 1][None], layer, seq)
    return h.reshape(bsz, seq, d)
```

```python
import functools
import math

import numpy as np
import jax
import jax.numpy as jnp
from jax import lax
from jax.experimental import pallas as pl
from jax.experimental.pallas import tpu as pltpu

F32 = jnp.float32
BF16 = jnp.bfloat16
HI = lax.Precision.HIGHEST

D_MODEL = 1024
DEPTH = 2
DEEPNORM_ALPHA = (2 * DEPTH) ** 0.25
LN_EPS = 1e-5
RMS_EPS = 1e-6
NUM_BUCKETS = 32
REL_MAX_DIST = 2048
GDN_HEADS = 6
GDN_D = 128
CHUNK = 64
GDN_GROUP = 256
SWA_CONFIGS = ((128, 1), (512, 4), (2048, 16))
SWA_HEADS = 4
SWA_DH = 64
SWA_BLOCK = 128
SWA_UNROLL = 8
SWA_SPAN = SWA_BLOCK * max(d for _, d in SWA_CONFIGS)
DIFF_HEADS = 4
DIFF_DH = 64
DIFF_BLOCK = 512
LOG2E = math.log2(math.e)
GLA_HEADS = 4
GLA_DK = 64
GLA_DV = 128
GLA_RANK = 16
GLA_TAU = 16.0
GLA_SUB = 16
GLA_GROUP = 256
GLA_MAX_DECAY = 60.0
D_FF = 2816

LANES = 128
SUBLANES = 8
HALO = 8
VMEM_LIMIT = 56 * 1024 * 1024
NEG_BIG = -1e30

EVA_PART = GDN_HEADS * GDN_D
EVB_BA = 2304
EVB_COLS = 2560
EVB_TN = 512
OD_QC, OD_KC, OD_VC = 0, 512, 1024
OD_QD, OD_KD, OD_VD, OD_RD = 1536, 1792, 2048, 2560
OD_COLS = 3072
OD_TN = 768
PROJ_TM = 512
TAIL_TM = 512
FFN_TC = 256


def _cparams(sem):
    return pltpu.CompilerParams(dimension_semantics=sem, vmem_limit_bytes=VMEM_LIMIT)


def _bdot(a, b):
    return jnp.dot(a.astype(BF16), b.astype(BF16), preferred_element_type=F32)


def _dot_nt(a, b):
    return lax.dot_general(a, b, (((1,), (1,)), ((), ())), preferred_element_type=F32)


def _bdot_nt(a, b):
    return lax.dot_general(a.astype(BF16), b.astype(BF16), (((1,), (1,)), ((), ())),
                           preferred_element_type=F32)


def _bdot_tn(a, b):
    return lax.dot_general(a.astype(BF16), b.astype(BF16), (((0,), (0,)), ((), ())),
                           preferred_element_type=F32)


def _sigmoid(x):
    return 1.0 / (1.0 + jnp.exp(-x))


def _silu(x):
    return x * _sigmoid(x)


def _softplus(x):
    return jnp.maximum(x, 0.0) + jnp.log1p(jnp.exp(-jnp.abs(x)))


def _log_sigmoid(x):
    return -_softplus(-x)


def _resident(a, lead=None):
    if lead is None:
        return pl.BlockSpec(a.shape, lambda i: (0,) * a.ndim, pipeline_mode=pl.Buffered(1))
    return pl.BlockSpec((None,) + a.shape[1:], lambda i: (lead,) + (0,) * (a.ndim - 1),
                        pipeline_mode=pl.Buffered(1))


def _odd_in_body(x_ref, w_ref, wg_ref, o_ref, og_ref):
    xb = x_ref[...].astype(BF16)
    for c in range(OD_COLS // OD_TN):
        cols = slice(c * OD_TN, (c + 1) * OD_TN)
        o_ref[:, cols] = _dot_nt(xb, w_ref[cols, :]).astype(o_ref.dtype)
    og_ref[...] = _dot_nt(xb, wg_ref[...])


def _odd_in_proj(x, w, w_gate_in, tm):
    t, k = x.shape
    return pl.pallas_call(
        _odd_in_body,
        grid=(t // tm,),
        in_specs=[pl.BlockSpec((tm, k), lambda i: (i, 0)), _resident(w), _resident(w_gate_in)],
        out_specs=[pl.BlockSpec((tm, OD_COLS), lambda i: (i, 0)), pl.BlockSpec((tm, LANES), lambda i: (i, 0))],
        out_shape=[jax.ShapeDtypeStruct((t, OD_COLS), BF16), jax.ShapeDtypeStruct((t, LANES), F32)],
        compiler_params=_cparams(("parallel",)),
        name="odd_in_proj",
    )(x, w, w_gate_in)


def _rel_bucket_np(dist):
    max_exact = NUM_BUCKETS // 2
    d = np.maximum(dist, 1).astype(np.float32)
    large = max_exact + (np.log(d / max_exact) / math.log(REL_MAX_DIST / max_exact)
                         * (NUM_BUCKETS - max_exact)).astype(np.int32)
    large = np.minimum(large, NUM_BUCKETS - 1)
    return np.where(dist < max_exact, dist, large).astype(np.int32)


def _bias_body(scale, base_bucket, tile_buckets, toeplitz, rb_ref, bucket_ref, neg_ref, o_ref):
    h = pl.program_id(0)
    t = pl.program_id(1)
    base = 0.0 if base_bucket is None else rb_ref[base_bucket, h]
    for tile, present in enumerate(tile_buckets):
        @pl.when(t == tile)
        def _(present=present):
            bucket = bucket_ref[0]
            acc = neg_ref[0]
            for b in present:
                acc = acc + jnp.where(bucket == b, (rb_ref[b, h] - base) * scale, 0.0)
            if toeplitz:
                n = o_ref.shape[2]
                rows = jnp.broadcast_to(acc[0:1], (n, 2 * n))
                acc = pltpu.roll(rows, n, 1, stride=1, stride_axis=0)[:, :n]
            o_ref[0, 0] = acc


def _bias_tiles(rel_bias, bucket, neg, scale=1.0, base_bucket=None, toeplitz=False):
    nt, r, c = bucket.shape
    nh = rel_bias.shape[1]
    out_r, out_c = (c // 2, c // 2) if toeplitz else (r, c)
    tile_buckets = tuple(tuple(int(b) for b in np.unique(bucket[t][neg[t] == 0]) if b != base_bucket)
                         for t in range(nt))
    return pl.pallas_call(
        functools.partial(_bias_body, scale, base_bucket, tile_buckets, toeplitz),
        grid=(nh, nt),
        in_specs=[pl.BlockSpec(memory_space=pltpu.SMEM),
                  pl.BlockSpec((1, r, c), lambda h, t: (t, 0, 0)),
                  pl.BlockSpec((1, r, c), lambda h, t: (t, 0, 0))],
        out_specs=pl.BlockSpec((1, 1, out_r, out_c), lambda h, t: (h, t, 0, 0)),
        out_shape=jax.ShapeDtypeStruct((nh, nt, out_r, out_c), F32),
        compiler_params=_cparams(("parallel", "parallel")),
        name="rel_bias_tiles",
    )(rel_bias, jnp.asarray(bucket), jnp.asarray(neg))


def _causal_conv(cur, w, scratch, carry, at_start):
    width = w.shape[0]
    rows = cur.shape[0]
    scratch[0:HALO, :] = jnp.where(at_start, 0.0, carry[...])
    scratch[HALO:, :] = cur
    carry[...] = cur[rows - HALO:]
    y = w[width - 1:width, :] * cur
    for j in range(width - 1):
        back = width - 1 - j
        y = y + w[j:j + 1, :] * scratch[HALO - back:HALO - back + rows, :]
    return y


def _even_in_body(tiles_per_seq, x_ref, wa_ref, wb_ref, cw_ref, oa_ref, ob_ref, scratch, carry):
    at_start = pl.program_id(0) % tiles_per_seq == 0
    x = x_ref[...].astype(BF16)
    for part in range(3):
        cols = slice(part * EVA_PART, (part + 1) * EVA_PART)
        y = _dot_nt(x, wa_ref[cols, :])
        c = _silu(_causal_conv(y, cw_ref[:, cols], scratch, carry.at[part], at_start))
        for hd in range(GDN_HEADS):
            ch = c[:, hd * GDN_D:(hd + 1) * GDN_D]
            if part < 2:
                inv = lax.rsqrt(jnp.sum(ch * ch, axis=-1, keepdims=True) + RMS_EPS)
                ch = ch * (inv * GDN_D ** -0.5 if part == 0 else inv)
            lo = part * EVA_PART + hd * GDN_D
            oa_ref[:, lo:lo + GDN_D] = ch.astype(oa_ref.dtype)
    z_cols = slice(3 * EVA_PART, 4 * EVA_PART)
    oa_ref[:, z_cols] = _dot_nt(x, wa_ref[z_cols, :]).astype(oa_ref.dtype)
    for c in range(EVB_COLS // EVB_TN):
        cols = slice(c * EVB_TN, (c + 1) * EVB_TN)
        ob_ref[:, cols] = _dot_nt(x, wb_ref[cols, :])


def _even_in_proj(x, w_a, w_b, conv_w, seq, tm):
    t, k = x.shape
    return pl.pallas_call(
        functools.partial(_even_in_body, seq // tm),
        grid=(t // tm,),
        in_specs=[pl.BlockSpec((tm, k), lambda i: (i, 0)),
                  _resident(w_a), _resident(w_b), _resident(conv_w)],
        out_specs=[pl.BlockSpec((tm, w_a.shape[0]), lambda i: (i, 0)),
                   pl.BlockSpec((tm, EVB_COLS), lambda i: (i, 0))],
        out_shape=[jax.ShapeDtypeStruct((t, w_a.shape[0]), BF16), jax.ShapeDtypeStruct((t, EVB_COLS), F32)],
        scratch_shapes=[pltpu.VMEM((HALO + tm, EVA_PART), F32), pltpu.VMEM((3, HALO, EVA_PART), F32)],
        compiler_params=_cparams(("arbitrary",)),
        name="even_in_proj",
    )(x, w_a, w_b, conv_w)


def _gdn_body(q_ref, k_ref, v_ref, z_ref, ba_ref, alog_ref, dtb_ref, nw_ref, o_ref, state):
    n = pl.program_id(1)

    @pl.when(n == 0)
    def _():
        state[...] = jnp.zeros_like(state)

    grp = GDN_GROUP
    nc = grp // CHUNK
    nh = GDN_HEADS
    hs = range(nh)
    ri = lax.broadcasted_iota(jnp.int32, (grp, grp), 0)
    ci = lax.broadcasted_iota(jnp.int32, (grp, grp), 1)
    same = (ri // CHUNK) == (ci // CHUNK)
    incl = same & (ri >= ci)
    strict = same & (ri > ci)
    eye = (ri == ci).astype(F32)
    tri = incl.astype(BF16)

    def chunk_local(rows):
        ba = ba_ref[rows, :]
        beta_all = _sigmoid(ba)
        g_all = -jnp.exp(alog_ref[...]) * _softplus(ba + dtb_ref[...])
        g_hi = g_all.astype(BF16)
        rem = g_all - g_hi.astype(F32)
        g_mid = rem.astype(BF16)
        g_lo = (rem - g_mid.astype(F32)).astype(BF16)
        gam_all = (jnp.dot(tri, g_hi, preferred_element_type=F32)
                   + jnp.dot(tri, g_mid, preferred_element_type=F32)
                   + jnp.dot(tri, g_lo, preferred_element_type=F32))
        gam_rows = gam_all.T

        q = [q_ref[rows, h * GDN_D:(h + 1) * GDN_D] for h in hs]
        k = [k_ref[rows, h * GDN_D:(h + 1) * GDN_D] for h in hs]
        v = [v_ref[rows, h * GDN_D:(h + 1) * GDN_D] for h in hs]
        gam = [jnp.broadcast_to(gam_all[:, nh + h:nh + h + 1], (grp, GDN_D)) for h in hs]
        bcol = [beta_all[:, h:h + 1] for h in hs]
        decay, x, inv = [], [], []
        for h in hs:
            diff = gam[h][:, 0:1] - gam_rows[nh + h:nh + h + 1, :]
            decay.append(jnp.where(incl, jnp.exp(jnp.where(incl, diff, 0.0)), 0.0))
            kk = _bdot_nt(k[h], k[h])
            x.append(-jnp.where(strict, bcol[h] * kk * decay[h], 0.0))
            inv.append(eye + x[h])
        for _ in range(5):
            for h in hs:
                x[h] = _bdot(x[h], x[h])
                inv[h] = inv[h] + _bdot(inv[h], x[h])
        u, w, qk, q_dec, k_dec, g_last = [], [], [], [], [], []
        for h in hs:
            eg = jnp.exp(gam[h])
            uw = _bdot(inv[h], jnp.concatenate([v[h] * bcol[h], k[h] * (bcol[h] * eg)], axis=1))
            u.append(uw[:, :GDN_D])
            w.append(uw[:, GDN_D:])
            qk.append(_bdot_nt(q[h], k[h]) * decay[h])
            q_dec.append(q[h] * eg)
            kd, gl = [], []
            for c in range(nc):
                last = gam[h][(c + 1) * CHUNK - 1:(c + 1) * CHUNK, :]
                kd.append(k[h][c * CHUNK:(c + 1) * CHUNK] * jnp.exp(last - gam[h][c * CHUNK:(c + 1) * CHUNK]))
                gl.append(jnp.exp(last))
            k_dec.append(kd)
            g_last.append(gl)
        return u, w, qk, q_dec, k_dec, g_last

    def recurrence(rows, local, s):
        u, w, qk, q_dec, k_dec, g_last = local
        q_s = [[] for _ in hs]
        delta = [[] for _ in hs]
        for c in range(nc):
            sl = slice(c * CHUNK, (c + 1) * CHUNK)
            for h in hs:
                r = _bdot(jnp.concatenate([w[h][sl], q_dec[h][sl]], axis=0), s[h])
                d = u[h][sl] - r[:CHUNK]
                q_s[h].append(r[CHUNK:])
                delta[h].append(d)
                s[h] = g_last[h][c] * s[h] + _bdot_tn(k_dec[h][c], d)
        for h in hs:
            o = jnp.concatenate(q_s[h], axis=0) + _bdot(qk[h], jnp.concatenate(delta[h], axis=0))
            o = o * lax.rsqrt(jnp.mean(o * o, axis=-1, keepdims=True) + RMS_EPS) * nw_ref[...]
            z = z_ref[rows, h * GDN_D:(h + 1) * GDN_D].astype(F32)
            o_ref[rows, h * GDN_D:(h + 1) * GDN_D] = (o * _silu(z)).astype(o_ref.dtype)
        return s

    rows = slice(0, grp)
    s = recurrence(rows, chunk_local(rows), [state[h] for h in hs])
    for h in hs:
        state[h] = s[h]


def _gdn(proj_a, proj_b, alog_pad, dtb_pad, norm_w, bsz, seq):
    t = proj_a.shape[0]
    rows = GDN_GROUP
    spb = seq // rows
    width = GDN_HEADS * GDN_D

    def at(col):
        return lambda b, n: (b * spb + n, col)

    return pl.pallas_call(
        _gdn_body,
        grid=(bsz, spb),
        in_specs=[pl.BlockSpec((rows, width), at(0)),
                  pl.BlockSpec((rows, width), at(1)),
                  pl.BlockSpec((rows, width), at(2)),
                  pl.BlockSpec((rows, width), at(3)),
                  pl.BlockSpec((rows, LANES), at(EVB_BA // LANES)),
                  pl.BlockSpec((1, LANES), lambda b, n: (0, 0)),
                  pl.BlockSpec((1, LANES), lambda b, n: (0, 0)),
                  pl.BlockSpec((1, LANES), lambda b, n: (0, 0))],
        out_specs=pl.BlockSpec((rows, width), at(0)),
        out_shape=jax.ShapeDtypeStruct((t, width), BF16),
        scratch_shapes=[pltpu.VMEM((GDN_HEADS, GDN_D, GDN_D), F32)],
        compiler_params=_cparams(("parallel", "arbitrary")),
        name="gdn",
    )(proj_a, proj_a, proj_a, proj_a, proj_b, alog_pad, dtb_pad, norm_w)


def _swa_bias_tables():
    qi = np.arange(SWA_BLOCK)[:, None] + SWA_BLOCK
    kj = np.arange(2 * SWA_BLOCK)[None, :]
    rel = qi - kj
    buckets, negs = [], []
    for has_prev in (True, False):
        for window, dilation in SWA_CONFIGS:
            valid = (rel >= 0) & (rel <= window // dilation) & (has_prev | (kj >= SWA_BLOCK))
            buckets.append(_rel_bucket_np(np.maximum(rel, 0) * dilation))
            negs.append(np.where(valid, 0.0, NEG_BIG))
    return np.stack(buckets).astype(np.int32), np.stack(negs).astype(np.float32)


def _swa_body(*refs):
    ng = len(SWA_CONFIGS)
    ins, bias_ref, o_ref = refs[:5 * ng], refs[5 * ng], refs[5 * ng + 1]
    scratch = refs[5 * ng + 2:]
    kbufs, vbufs, o_scr, lse_scr = scratch[:ng], scratch[ng:2 * ng], scratch[2 * ng], scratch[2 * ng + 1]
    j = pl.program_id(1)
    blk = SWA_BLOCK
    first_head = lax.broadcasted_iota(jnp.int32, (blk, LANES), 1) < SWA_DH

    for g, (_, d) in enumerate(SWA_CONFIGS):
        q_ref, kc_ref, kp_ref, vc_ref, vp_ref = ins[5 * g:5 * g + 5]
        kbuf, vbuf = kbufs[g], vbufs[g]
        halo = blk * d
        kbuf[0:halo, :] = kp_ref[...]
        kbuf[halo:, :] = kc_ref[...]
        vbuf[0:halo, :] = vp_ref[...]
        vbuf[halo:, :] = vc_ref[...]

        def unit(u, carry, g=g, d=d, halo=halo, q_ref=q_ref, kbuf=kbuf, vbuf=vbuf):
            base = (u // d) * halo + u % d
            q = q_ref[pl.ds(base, blk, stride=d), :] * (SWA_DH ** -0.5 * LOG2E)
            k = kbuf[pl.ds(base, 2 * blk, stride=d), :]
            v = vbuf[pl.ds(base, 2 * blk, stride=d), :]
            lhs = jnp.concatenate([jnp.where(first_head, q, 0.0), jnp.where(first_head, 0.0, q)], axis=0)
            no_prev = jnp.where((j == 0) & (u < d), 1, 0)
            s = _bdot_nt(lhs, k) + bias_ref[g, 0, no_prev]
            m = jnp.max(s, axis=-1, keepdims=True)
            p = jnp.exp2(s - m)
            l = jnp.sum(p, axis=-1, keepdims=True)
            o2 = _bdot(p, v) * (1.0 / l)
            lse2 = m + jnp.log2(l)
            o_scr[g, pl.ds(base, blk, stride=d), :] = jnp.where(first_head, o2[:blk], o2[blk:])
            lse_scr[g, pl.ds(base, blk, stride=d), :] = jnp.where(first_head, lse2[:blk], lse2[blk:])
            return carry

        lax.fori_loop(0, SWA_SPAN // blk, unit, 0, unroll=SWA_UNROLL)

    def combine(c, carry):
        rows = pl.ds(pl.multiple_of(c * 2 * blk, 2 * blk), 2 * blk)
        lse = [lse_scr[g, rows, :] for g in range(ng)]
        m = functools.reduce(jnp.maximum, lse)
        e = [jnp.exp2(x - m) for x in lse]
        den = functools.reduce(lambda x, y: x + y, e)
        o_ref[rows, :] = functools.reduce(
            lambda x, y: x + y, [(e[g] / den) * o_scr[g, rows, :] for g in range(ng)]).astype(o_ref.dtype)
        return carry

    lax.fori_loop(0, SWA_SPAN // (2 * blk), combine, 0)


def _swa(proj, bias, bsz, seq):
    t = proj.shape[0]
    ng = len(SWA_CONFIGS)
    nspan = seq // SWA_SPAN
    npair = SWA_HEADS * SWA_DH // LANES
    group_cols = SWA_HEADS * SWA_DH // LANES
    in_specs, scratch_k = [], []
    for g, (_, d) in enumerate(SWA_CONFIGS):
        halo = SWA_BLOCK * d
        per_span = SWA_SPAN // halo

        def cur(which, g=g):
            col = (which * ng + g) * group_cols
            return lambda b, j, p: (b * nspan + j, col + p)

        def prev(which, g=g, per_span=per_span):
            col = (which * ng + g) * group_cols
            return lambda b, j, p: (jnp.maximum((b * nspan + j) * per_span - 1, 0), col + p)

        in_specs += [pl.BlockSpec((SWA_SPAN, LANES), cur(0)),
                     pl.BlockSpec((SWA_SPAN, LANES), cur(1)), pl.BlockSpec((halo, LANES), prev(1)),
                     pl.BlockSpec((SWA_SPAN, LANES), cur(2)), pl.BlockSpec((halo, LANES), prev(2))]
        scratch_k.append(pltpu.VMEM((halo + SWA_SPAN, LANES), F32))
    in_specs.append(pl.BlockSpec((ng, 1, 2, 2 * SWA_BLOCK, 2 * SWA_BLOCK), lambda b, j, p: (0, p, 0, 0, 0)))
    return pl.pallas_call(
        _swa_body,
        grid=(bsz, nspan, npair),
        in_specs=in_specs,
        out_specs=pl.BlockSpec((SWA_SPAN, LANES), lambda b, j, p: (b * nspan + j, p)),
        out_shape=jax.ShapeDtypeStruct((t, npair * LANES), BF16),
        scratch_shapes=scratch_k + scratch_k + [pltpu.VMEM((ng, SWA_SPAN, LANES), F32)] * 2,
        compiler_params=_cparams(("parallel", "parallel", "parallel")),
        name="swa",
    )(*([proj] * (5 * ng)), bias)


def _layer_norm(y, g, b):
    mu = jnp.mean(y, axis=-1, keepdims=True)
    yc = y - mu
    var = jnp.mean(yc * yc, axis=-1, keepdims=True)
    return yc * lax.rsqrt(var + LN_EPS) * g + b


def _diff_bias_tables(blk):
    buckets = _rel_bucket_np(np.arange(2 * REL_MAX_DIST))
    far = int(np.max(np.nonzero(buckets != NUM_BUCKETS - 1)[0])) + 1
    nb = -(-(far + blk - 1) // blk)
    dist = np.stack([t * blk - blk + np.arange(2 * blk) for t in range(nb + 1)])[:, None, :]
    dist = np.broadcast_to(dist, (nb + 1, SUBLANES, 2 * blk))
    bucket = _rel_bucket_np(np.maximum(dist, 0))
    neg = np.where(dist >= 0, 0.0, NEG_BIG).astype(np.float32)
    return bucket, neg


def _diff_body(blk, nb, lam_init, q_ref, qn_ref, k_ref, v_ref, bias_ref, lam_ref, nw_ref, o_ref,
               vt, acc1, acc2, s_a, s_b):
    qi = pl.program_id(2)
    dv = 2 * DIFF_DH
    seq = k_ref.shape[0]

    @pl.when(qi == 0)
    def _():
        vt[dv:, :] = jnp.ones((vt.shape[0] - dv, seq), BF16)

        def fill(c, carry):
            st = pl.multiple_of(c * blk, blk)
            vt[0:dv, pl.ds(st, blk)] = v_ref[pl.ds(st, blk), :].astype(F32).T.astype(BF16)
            return carry

        lax.fori_loop(0, seq // blk, fill, 0)

    feature = lax.broadcasted_iota(jnp.int32, (dv, blk), 0)

    def components(ref):
        q_t = (ref[...].astype(F32) * (DIFF_DH ** -0.5 * LOG2E)).T
        return (jnp.where(feature < DIFF_DH, q_t, 0.0).astype(BF16),
                jnp.where(feature >= DIFF_DH, q_t, 0.0).astype(BF16))

    qs = components(q_ref)
    qs_next = components(qn_ref)
    accs = (acc1, acc2)
    for acc in accs:
        acc[...] = jnp.zeros_like(acc)

    last = pl.num_programs(2) - 1

    def key_rows(kj):
        return pl.ds(pl.multiple_of(jnp.minimum(kj, last) * blk, blk), blk)

    def scores(kj, dst, queries=qs):
        k = k_ref[key_rows(kj), :]
        for c, qc in enumerate(queries):
            dst[c] = jnp.dot(k, qc, preferred_element_type=F32)

    def consume(biased, kj, src, ms):
        vtb = vt[:, key_rows(kj)]
        out = []
        for c, (m, acc) in enumerate(zip(ms, accs)):
            s = src[c]
            if biased:
                s = s + bias_ref[0, jnp.minimum(qi - kj, nb)]
            m_new = jnp.maximum(m, jnp.max(s, axis=0, keepdims=True))
            p = jnp.exp2(s - m_new).astype(BF16)
            acc[...] = jnp.exp2(m - m_new) * acc[...] + jnp.dot(vtb, p, preferred_element_type=F32)
            out.append(m_new)
        return tuple(out)

    nblocks = qi + 1
    trips = nblocks // 2
    odd = nblocks % 2 == 1
    far_trips = jnp.maximum(qi - nb + 1, 0) // 2

    def pair(biased, t, ms):
        kj = 2 * t
        scores(kj + 1, s_b)
        ms = consume(biased, kj, s_a, ms)
        if biased:
            hand_off = (t == trips - 1) & jnp.logical_not(odd)
            scores(jnp.where(hand_off, 0, kj + 2), s_a,
                   tuple(jnp.where(hand_off, qn, qc) for qc, qn in zip(qs, qs_next)))
        else:
            scores(kj + 2, s_a)
        return consume(biased, kj + 1, s_b, ms)

    @pl.when(qi == 0)
    def _():
        scores(0, s_a)

    m0 = jnp.full((1, blk), NEG_BIG, F32)
    ms = lax.fori_loop(0, far_trips, functools.partial(pair, False), (m0, m0))
    ms = lax.fori_loop(far_trips, trips, functools.partial(pair, True), ms)

    def finish():
        lp = lam_ref[...]
        lam = (jnp.exp(jnp.sum(lp[0:1] * lp[1:2], axis=-1, keepdims=True))
               - jnp.exp(jnp.sum(lp[2:3] * lp[3:4], axis=-1, keepdims=True)) + lam_init)
        a1, a2 = acc1[...], acc2[...]
        o_t = a1[:dv] * (1.0 / a1[dv:dv + 1]) - a2[:dv] * (lam / a2[dv:dv + 1])
        o = o_t.T
        o = o * lax.rsqrt(jnp.mean(o * o, axis=-1, keepdims=True) + RMS_EPS) * nw_ref[...]
        o_ref[...] = (o * (1.0 - lam_init)).astype(o_ref.dtype)

    @pl.when(odd)
    def _():
        consume(True, qi, s_a, ms)
        scores(0, s_a, qs_next)
        finish()

    @pl.when(jnp.logical_not(odd))
    def _():
        finish()


def _diff_attention(proj, bias, lam_params, norm_w, lam_init, bsz, seq, blk):
    t = proj.shape[0]
    nq = seq // blk
    nt = bias.shape[1]
    nb = nt - 1
    width = 2 * DIFF_DH
    ones_rows = 2 * SUBLANES
    return pl.pallas_call(
        functools.partial(_diff_body, blk, nb, lam_init),
        grid=(bsz, DIFF_HEADS, nq),
        in_specs=[pl.BlockSpec((blk, width), lambda b, h, i: (b * nq + i, OD_QC // width + h)),
                  pl.BlockSpec((blk, width),
                               lambda b, h, i: (b * nq + jnp.minimum(i + 1, nq - 1), OD_QC // width + h)),
                  pl.BlockSpec((seq, width), lambda b, h, i: (b, OD_KC // width + h)),
                  pl.BlockSpec((seq, width), lambda b, h, i: (b, OD_VC // width + h)),
                  pl.BlockSpec((1, nt, blk, blk), lambda b, h, i: (h, 0, 0, 0)),
                  pl.BlockSpec((4, DIFF_DH), lambda b, h, i: (0, 0)),
                  pl.BlockSpec((1, width), lambda b, h, i: (0, 0))],
        out_specs=pl.BlockSpec((blk, width), lambda b, h, i: (b * nq + i, h)),
        out_shape=jax.ShapeDtypeStruct((t, DIFF_HEADS * width), BF16),
        scratch_shapes=[pltpu.VMEM((width + ones_rows, seq), BF16),
                        pltpu.VMEM((width + ones_rows, blk), F32), pltpu.VMEM((width + ones_rows, blk), F32),
                        pltpu.VMEM((2, blk, blk), F32), pltpu.VMEM((2, blk, blk), F32)],
        compiler_params=_cparams(("arbitrary", "arbitrary", "arbitrary")),
        name="diff_attn",
    )(proj, proj, proj, proj, bias, lam_params, norm_w)


def _gla_body(q_ref, k_ref, gd_ref, v_ref, r_ref, wg_ref, bg_ref, nw_ref, o_ref, state, part):
    n = pl.program_id(1)

    @pl.when(n == 0)
    def _():
        state[...] = jnp.zeros_like(state)

    rows = GLA_GROUP
    nc = rows // CHUNK
    npair = GLA_HEADS // 2
    gate = jnp.dot(gd_ref[...], wg_ref[...], precision=HI, preferred_element_type=F32) + bg_ref[...]
    log_a = _log_sigmoid(gate) * (1.0 / GLA_TAU)

    ri = lax.broadcasted_iota(jnp.int32, (rows, rows), 0)
    ci = lax.broadcasted_iota(jnp.int32, (rows, rows), 1)
    causal = ((ri // CHUNK) == (ci // CHUNK)) & (ri >= ci)
    tri = causal.astype(BF16)
    hi = log_a.astype(BF16)
    rem = log_a - hi.astype(F32)
    mid = rem.astype(BF16)
    lo = (rem - mid.astype(F32)).astype(BF16)
    b_all = (jnp.dot(tri, hi, preferred_element_type=F32) + jnp.dot(tri, mid, preferred_element_type=F32)
             + jnp.dot(tri, lo, preferred_element_type=F32))

    lane = lax.broadcasted_iota(jnp.int32, (rows, LANES), 1)
    lane_c = lax.broadcasted_iota(jnp.int32, (CHUNK, LANES), 1)
    head_lanes = (lane < GLA_DK, lane >= GLA_DK)
    chunk_lanes = (lane_c < GLA_DK, lane_c >= GLA_DK)
    pairs = []
    for p in range(npair):
        cols = slice(p * LANES, (p + 1) * LANES)
        pairs.append(dict(
            b=b_all[:, cols],
            q=q_ref[:, cols].astype(F32) * GLA_DK ** -0.5,
            k=k_ref[:, cols].astype(F32),
            v=[v_ref[:, (2 * p + hd) * GLA_DV:(2 * p + hd + 1) * GLA_DV].astype(F32) for hd in range(2)]))

    for pr in pairs:
        pr["q_dec"] = pr["q"] * jnp.exp(pr["b"])
    for c in range(nc):
        sl = slice(c * CHUNK, (c + 1) * CHUNK)
        for p, pr in enumerate(pairs):
            bc = pr["b"][sl]
            b_last = bc[CHUNK - 1:CHUNK]
            k_dec = pr["k"][sl] * jnp.exp(b_last - bc)
            e_last = jnp.exp(b_last)
            for hd in range(2):
                h = 2 * p + hd
                st = state[h]
                part[h, sl, :] = _bdot_nt(jnp.where(chunk_lanes[hd], pr["q_dec"][sl], 0.0), st)
                state[h] = st * e_last + _bdot_tn(pr["v"][hd][sl], k_dec)

    def finish(intra):
        for h in range(GLA_HEADS):
            o = part[h] + intra[h]
            o = o * lax.rsqrt(jnp.mean(o * o, axis=-1, keepdims=True) + RMS_EPS) * nw_ref[...]
            gate_r = _silu(r_ref[:, h * GLA_DV:(h + 1) * GLA_DV].astype(F32))
            o_ref[:, h * GLA_DV:(h + 1) * GLA_DV] = (o * gate_r).astype(o_ref.dtype)

    def intra_whole_chunk():
        out = []
        for pr in pairs:
            k_inv = pr["k"] * jnp.exp(jnp.minimum(-pr["b"], GLA_MAX_DECAY))
            for hd in range(2):
                a = _bdot_nt(jnp.where(head_lanes[hd], pr["q_dec"], 0.0), k_inv)
                out.append(_bdot(jnp.where(causal, a, 0.0), pr["v"][hd]))
        return out

    def intra_exact():
        out = []
        for pr in pairs:
            off = _gla_intra_off_diagonal(pr["q"], pr["k"], pr["b"], pr["v"], nc)
            diag = _gla_intra_diagonal(pr["q"], pr["k"], pr["b"], pr["v"])
            out += [off[hd] + diag[hd] for hd in range(2)]
        return out

    chunk_decay = jnp.max(-b_all.reshape(nc, CHUNK, npair * LANES)[:, CHUNK - 1:CHUNK, :])

    @pl.when(chunk_decay <= GLA_MAX_DECAY)
    def _():
        finish(intra_whole_chunk())

    @pl.when(chunk_decay > GLA_MAX_DECAY)
    def _():
        finish(intra_exact())


def _gla_intra_off_diagonal(q, k, b, vs, nc):
    per_chunk = CHUNK // GLA_SUB
    lane = lax.broadcasted_iota(jnp.int32, (GLA_SUB, LANES), 1)
    sub_mask = (lane < GLA_DK, lane >= GLA_DK)
    kcol = lax.broadcasted_iota(jnp.int32, (GLA_SUB, CHUNK), 1)
    outs = ([], [])
    for c in range(nc):
        sl = slice(c * CHUNK, (c + 1) * CHUNK)
        bc, qc, kc = b[sl], q[sl], k[sl]
        a_rows = [[jnp.zeros((GLA_SUB, CHUNK), F32)] for _ in vs]
        for blk in range(1, per_chunk):
            r0 = blk * GLA_SUB
            bref = bc[r0:r0 + 1]
            qs = qc[r0:r0 + GLA_SUB] * jnp.exp(bc[r0:r0 + GLA_SUB] - bref)
            ks = kc * jnp.exp(jnp.minimum(bref - bc, 0.0))
            for hd in range(2):
                a = _bdot_nt(jnp.where(sub_mask[hd], qs, 0.0), ks)
                a_rows[hd].append(jnp.where(kcol < r0, a, 0.0))
        for hd in range(2):
            outs[hd].append(_bdot(jnp.concatenate(a_rows[hd], axis=0), vs[hd][sl]))
    return [jnp.concatenate(o, axis=0) for o in outs]


def _gla_intra_diagonal(q, k, b, vs):
    rows = q.shape[0]
    nsub = rows // GLA_SUB
    b3 = b.reshape(nsub, GLA_SUB, LANES)
    q3 = q.reshape(nsub, GLA_SUB, LANES)
    k3 = k.reshape(nsub, GLA_SUB, LANES)
    v3 = [v.reshape(nsub, GLA_SUB, LANES) for v in vs]
    row3 = lax.broadcasted_iota(jnp.int32, (nsub, GLA_SUB, LANES), 1)
    lane3 = lax.broadcasted_iota(jnp.int32, (nsub, GLA_SUB, LANES), 2)
    rowc = lax.broadcasted_iota(jnp.int32, (nsub, GLA_SUB, 1), 1)
    o3 = [jnp.zeros((nsub, GLA_SUB, LANES), F32) for _ in vs]
    for jj in range(GLA_SUB):
        e = jnp.exp(jnp.where(row3 >= jj, b3 - b3[:, jj:jj + 1, :], 0.0))
        t = q3 * k3[:, jj:jj + 1, :] * e
        w_all = jnp.sum(t, axis=-1, keepdims=True)
        w_a = jnp.sum(jnp.where(lane3 < GLA_DK, t, 0.0), axis=-1, keepdims=True)
        for hd, w in enumerate((w_a, w_all - w_a)):
            o3[hd] = o3[hd] + jnp.where(rowc >= jj, w, 0.0) * v3[hd][:, jj:jj + 1, :]
    return [o.reshape(rows, LANES) for o in o3]


def _gla(proj, gate_in, w_gate_pad, b_gate, norm_w, bsz, seq):
    t = proj.shape[0]
    rows = GLA_GROUP
    spb = seq // rows
    qk_w = GLA_HEADS * GLA_DK
    v_w = GLA_HEADS * GLA_DV

    def at(col):
        return lambda b, n: (b * spb + n, col)

    def whole(a):
        return pl.BlockSpec(a.shape, lambda b, n: (0, 0))

    return pl.pallas_call(
        _gla_body,
        grid=(bsz, spb),
        in_specs=[pl.BlockSpec((rows, qk_w), at(OD_QD // qk_w)),
                  pl.BlockSpec((rows, qk_w), at(OD_KD // qk_w)),
                  pl.BlockSpec((rows, LANES), at(0)),
                  pl.BlockSpec((rows, v_w), at(OD_VD // v_w)),
                  pl.BlockSpec((rows, v_w), at(OD_RD // v_w)),
                  whole(w_gate_pad), whole(b_gate), whole(norm_w)],
        out_specs=pl.BlockSpec((rows, v_w), at(0)),
        out_shape=jax.ShapeDtypeStruct((t, v_w), BF16),
        scratch_shapes=[pltpu.VMEM((GLA_HEADS, GLA_DV, LANES), F32), pltpu.VMEM((GLA_HEADS, rows, GLA_DV), F32)],
        compiler_params=_cparams(("parallel", "arbitrary")),
        name="gla",
    )(proj, proj, gate_in, proj, proj, w_gate_pad, b_gate, norm_w)


def _tail_body(tiles_per_seq, tc, ma_ref, mb_ref, h_ref, woa_ref, wob_ref,
               g1_ref, b1_ref, wu_ref, cw_ref, cb_ref, wd_ref, g2_ref, b2_ref, y_ref,
               act, scratch_g, scratch_v, carry):
    at_start = pl.program_id(0) % tiles_per_seq == 0
    mix = (jnp.dot(ma_ref[...], woa_ref[...], preferred_element_type=F32)
           + jnp.dot(mb_ref[...], wob_ref[...], preferred_element_type=F32))
    x = _layer_norm(DEEPNORM_ALPHA * h_ref[...] + mix, g1_ref[...], b1_ref[...])
    xb = x.astype(BF16)
    nchunk = D_FF // tc

    def branch(idx, scratch):
        lo = idx * tc
        cur = jnp.dot(xb, wu_ref[:, lo:lo + tc], preferred_element_type=F32)
        return _causal_conv(cur, cw_ref[:, lo:lo + tc], scratch, carry.at[idx], at_start) + cb_ref[:, lo:lo + tc]

    for c in range(nchunk):
        gate = branch(c, scratch_g)
        val = branch(nchunk + c, scratch_v)
        act[:, c * tc:(c + 1) * tc] = (_silu(gate) * val).astype(BF16)
    ffn = jnp.dot(act[...], wd_ref[...], preferred_element_type=F32)
    y_ref[...] = _layer_norm(DEEPNORM_ALPHA * x + ffn, g2_ref[...], b2_ref[...])


def _layer_tail(mix_a, mix_b, h, w_out_a, w_out_b, g1, b1, w_up, conv_w, conv_b, w_down, g2, b2,
                layer, seq):
    t = h.shape[0]
    tm, tc = TAIL_TM, FFN_TC

    def tile(a):
        return pl.BlockSpec((tm, a.shape[1]), lambda i: (i, 0))

    params = (w_out_a, w_out_b, g1, b1, w_up, conv_w, conv_b, w_down, g2, b2)
    param_specs = [_resident(w_out_a), _resident(w_out_b), _resident(g1), _resident(b1),
                   _resident(w_up, layer), _resident(conv_w), _resident(conv_b), _resident(w_down, layer),
                   _resident(g2), _resident(b2)]
    return pl.pallas_call(
        functools.partial(_tail_body, seq // tm, tc),
        grid=(t // tm,),
        in_specs=[tile(mix_a), tile(mix_b), tile(h)] + param_specs,
        out_specs=pl.BlockSpec((tm, D_MODEL), lambda i: (i, 0)),
        out_shape=jax.ShapeDtypeStruct((t, D_MODEL), F32),
        scratch_shapes=[pltpu.VMEM((tm, D_FF), BF16), pltpu.VMEM((HALO + tm, tc), F32),
                        pltpu.VMEM((HALO + tm, tc), F32), pltpu.VMEM((2 * D_FF // tc, HALO, tc), F32)],
        compiler_params=_cparams(("arbitrary",)),
        name="layer_tail",
    )(mix_a, mix_b, h, *params)


def _even_w_in(w):
    w_t = w.T
    a_end = 4 * EVA_PART
    gates = w_t[a_end:a_end + 2 * GDN_HEADS]
    qkv_b = w_t[a_end + 2 * GDN_HEADS:]
    pad = jnp.zeros((EVB_COLS - EVB_BA - 2 * GDN_HEADS, w.shape[0]), w.dtype)
    return w_t[:a_end].astype(BF16), jnp.concatenate([qkv_b, gates, pad], axis=0).astype(BF16)


def _odd_w_in(w):
    w_t = w.T
    pad = jnp.zeros((LANES - GLA_RANK, w.shape[0]), w.dtype)
    return w_t[:OD_COLS].astype(BF16), jnp.concatenate([w_t[OD_COLS:], pad], axis=0).astype(BF16)


def _even_mixer(h, rel_bias, w_in, conv_w, a_log, dt_bias, norm_w, w_out, bsz, seq):
    w_a, w_b = _even_w_in(w_in)
    proj_a, proj_b = _even_in_proj(h, w_a, w_b, conv_w, seq, PROJ_TM)
    gate_pad = jnp.zeros((1, LANES), F32)
    alog_pad = lax.dynamic_update_slice(gate_pad, a_log[None].astype(F32), (0, GDN_HEADS))
    dtb_pad = lax.dynamic_update_slice(gate_pad, dt_bias[None].astype(F32), (0, GDN_HEADS))
    o_a = _gdn(proj_a, proj_b, alog_pad, dtb_pad, norm_w[None], bsz, seq)
    tiles = _bias_tiles(rel_bias, *_swa_bias_tables(), scale=LOG2E)
    ng, two_blk = len(SWA_CONFIGS), 2 * SWA_BLOCK
    bias = tiles.reshape(SWA_HEADS // 2, 2, 2, ng, SWA_BLOCK, two_blk).transpose(3, 0, 2, 1, 4, 5)
    o_b = _swa(proj_b, bias.reshape(ng, SWA_HEADS // 2, 2, two_blk, two_blk), bsz, seq)
    w_out = w_out.astype(BF16)
    return o_a, o_b, w_out[:EVA_PART], w_out[EVA_PART:]


def _odd_mixer(h, rel_bias, w_in, lam_params, diff_norm_w, w_gate, b_gate, gla_norm_w, w_out,
               lam_init, bsz, seq):
    w_main, w_gd = _odd_w_in(w_in)
    proj, gate_in = _odd_in_proj(h, w_main, w_gd, PROJ_TM)
    blk = min(DIFF_BLOCK, seq)
    bucket, neg = _diff_bias_tables(blk)
    bias = _bias_tiles(rel_bias, bucket, neg, scale=LOG2E, base_bucket=NUM_BUCKETS - 1, toeplitz=True)
    o_c = _diff_attention(proj, bias, lam_params, diff_norm_w[None], lam_init, bsz, seq, blk)
    w_gate_pad = jnp.concatenate(
        [w_gate, jnp.zeros((LANES - GLA_RANK, w_gate.shape[1]), w_gate.dtype)], axis=0)
    o_d = _gla(proj, gate_in, w_gate_pad, b_gate[None], gla_norm_w[None], bsz, seq)
    diff_v = DIFF_HEADS * 2 * DIFF_DH
    w_out = w_out.astype(BF16)
    return o_c, o_d, w_out[:diff_v], w_out[diff_v:]


def kernel(x, rel_bias, w_in_even, gdn_conv_w, gdn_a_log, gdn_dt_bias, gdn_norm_w, w_out_even,
           w_in_odd, diff_lambda, diff_norm_w, gla_w_gate, gla_b_gate, gla_norm_w, w_out_odd,
           ffn_w_up, ffn_conv_w, ffn_conv_b, ffn_w_down, ln_g, ln_b):
    bsz, seq, d = x.shape
    h = x.reshape(bsz * seq, d)
    w_up, w_down = ffn_w_up.astype(BF16), ffn_w_down.astype(BF16)
    for layer in range(DEPTH):
        i = layer // 2
        if layer % 2 == 0:
            mixed = _even_mixer(h, rel_bias, w_in_even[i], gdn_conv_w[i], gdn_a_log[i], gdn_dt_bias[i],
                                gdn_norm_w[i], w_out_even[i], bsz, seq)
        else:
            lam_init = 0.8 - 0.6 * math.exp(-0.3 * layer)
            mixed = _odd_mixer(h, rel_bias, w_in_odd[i], diff_lambda[i], diff_norm_w[i], gla_w_gate[i],
                               gla_b_gate[i], gla_norm_w[i], w_out_odd[i], lam_init, bsz, seq)
        h = _layer_tail(*mixed[:2], h, *mixed[2:], ln_g[layer, 0][None], ln_b[layer, 0][None],
                        w_up, ffn_conv_w[layer], ffn_conv_b[layer][None], w_down,
                        ln_g[layer, 1][None], ln_b[layer, 1][None], layer, seq)
    return h.reshape(bsz, seq, d)
```

```python
import functools
import math

import numpy as np
import jax
import jax.numpy as jnp
from jax import lax
from jax.experimental import pallas as pl
from jax.experimental.pallas import tpu as pltpu

F32 = jnp.float32
BF16 = jnp.bfloat16
HI = lax.Precision.HIGHEST

D_MODEL = 1024
DEPTH = 2
DEEPNORM_ALPHA = (2 * DEPTH) ** 0.25
LN_EPS = 1e-5
RMS_EPS = 1e-6
NUM_BUCKETS = 32
REL_MAX_DIST = 2048
GDN_HEADS = 6
GDN_D = 128
CHUNK = 64
GDN_GROUP = 256
GDN_STEP_GROUPS = 2
SWA_CONFIGS = ((128, 1), (512, 4), (2048, 16))
SWA_HEADS = 4
SWA_DH = 64
SWA_BLOCK = 128
SWA_UNROLL = 8
SWA_SPAN = SWA_BLOCK * max(d for _, d in SWA_CONFIGS)
DIFF_HEADS = 4
DIFF_DH = 64
DIFF_BLOCK = 512
LOG2E = math.log2(math.e)
GLA_HEADS = 4
GLA_DK = 64
GLA_DV = 128
GLA_RANK = 16
GLA_TAU = 16.0
GLA_SUB = 16
GLA_GROUP = 256
GLA_MAX_DECAY = 60.0
D_FF = 2816

LANES = 128
SUBLANES = 8
HALO = 8
VMEM_LIMIT = 56 * 1024 * 1024
NEG_BIG = -1e30

EVA_PART = GDN_HEADS * GDN_D
EVB_BA = 2304
EVB_COLS = 2560
EVB_TN = 512
OD_QC, OD_KC, OD_VC = 0, 512, 1024
OD_QD, OD_KD, OD_VD, OD_RD = 1536, 1792, 2048, 2560
OD_COLS = 3072
OD_TN = 768
PROJ_TM = 512
TAIL_TM = 512
FFN_TC = 256


def _cparams(sem):
    return pltpu.CompilerParams(dimension_semantics=sem, vmem_limit_bytes=VMEM_LIMIT)


def _bdot(a, b):
    return jnp.dot(a.astype(BF16), b.astype(BF16), preferred_element_type=F32)


def _dot_nt(a, b):
    return lax.dot_general(a, b, (((1,), (1,)), ((), ())), preferred_element_type=F32)


def _bdot_nt(a, b):
    return lax.dot_general(a.astype(BF16), b.astype(BF16), (((1,), (1,)), ((), ())),
                           preferred_element_type=F32)


def _bdot_tn(a, b):
    return lax.dot_general(a.astype(BF16), b.astype(BF16), (((0,), (0,)), ((), ())),
                           preferred_element_type=F32)


def _sigmoid(x):
    return 1.0 / (1.0 + jnp.exp(-x))


def _silu(x):
    return x * _sigmoid(x)


def _softplus(x):
    return jnp.maximum(x, 0.0) + jnp.log1p(jnp.exp(-jnp.abs(x)))


def _log_sigmoid(x):
    return -_softplus(-x)


def _resident(a, lead=None):
    if lead is None:
        return pl.BlockSpec(a.shape, lambda i: (0,) * a.ndim, pipeline_mode=pl.Buffered(1))
    return pl.BlockSpec((None,) + a.shape[1:], lambda i: (lead,) + (0,) * (a.ndim - 1),
                        pipeline_mode=pl.Buffered(1))


def _odd_in_body(x_ref, w_ref, wg_ref, o_ref, og_ref):
    xb = x_ref[...].astype(BF16)
    for c in range(OD_COLS // OD_TN):
        cols = slice(c * OD_TN, (c + 1) * OD_TN)
        o_ref[:, cols] = _dot_nt(xb, w_ref[cols, :]).astype(o_ref.dtype)
    og_ref[...] = _dot_nt(xb, wg_ref[...])


def _odd_in_proj(x, w, w_gate_in, tm):
    t, k = x.shape
    return pl.pallas_call(
        _odd_in_body,
        grid=(t // tm,),
        in_specs=[pl.BlockSpec((tm, k), lambda i: (i, 0)), _resident(w), _resident(w_gate_in)],
        out_specs=[pl.BlockSpec((tm, OD_COLS), lambda i: (i, 0)), pl.BlockSpec((tm, LANES), lambda i: (i, 0))],
        out_shape=[jax.ShapeDtypeStruct((t, OD_COLS), BF16), jax.ShapeDtypeStruct((t, LANES), F32)],
        compiler_params=_cparams(("parallel",)),
        name="odd_in_proj",
    )(x, w, w_gate_in)


def _rel_bucket_np(dist):
    max_exact = NUM_BUCKETS // 2
    d = np.maximum(dist, 1).astype(np.float32)
    large = max_exact + (np.log(d / max_exact) / math.log(REL_MAX_DIST / max_exact)
                         * (NUM_BUCKETS - max_exact)).astype(np.int32)
    large = np.minimum(large, NUM_BUCKETS - 1)
    return np.where(dist < max_exact, dist, large).astype(np.int32)


def _bias_body(scale, base_bucket, tile_buckets, toeplitz, rb_ref, bucket_ref, neg_ref, o_ref):
    h = pl.program_id(0)
    t = pl.program_id(1)
    base = 0.0 if base_bucket is None else rb_ref[base_bucket, h]
    for tile, present in enumerate(tile_buckets):
        @pl.when(t == tile)
        def _(present=present):
            bucket = bucket_ref[0]
            acc = neg_ref[0]
            for b in present:
                acc = acc + jnp.where(bucket == b, (rb_ref[b, h] - base) * scale, 0.0)
            if toeplitz:
                n = o_ref.shape[2]
                rows = jnp.broadcast_to(acc[0:1], (n, 2 * n))
                acc = pltpu.roll(rows, n, 1, stride=1, stride_axis=0)[:, :n]
            o_ref[0, 0] = acc


def _bias_tiles(rel_bias, bucket, neg, scale=1.0, base_bucket=None, toeplitz=False):
    nt, r, c = bucket.shape
    nh = rel_bias.shape[1]
    out_r, out_c = (c // 2, c // 2) if toeplitz else (r, c)
    tile_buckets = tuple(tuple(int(b) for b in np.unique(bucket[t][neg[t] == 0]) if b != base_bucket)
                         for t in range(nt))
    return pl.pallas_call(
        functools.partial(_bias_body, scale, base_bucket, tile_buckets, toeplitz),
        grid=(nh, nt),
        in_specs=[pl.BlockSpec(memory_space=pltpu.SMEM),
                  pl.BlockSpec((1, r, c), lambda h, t: (t, 0, 0)),
                  pl.BlockSpec((1, r, c), lambda h, t: (t, 0, 0))],
        out_specs=pl.BlockSpec((1, 1, out_r, out_c), lambda h, t: (h, t, 0, 0)),
        out_shape=jax.ShapeDtypeStruct((nh, nt, out_r, out_c), F32),
        compiler_params=_cparams(("parallel", "parallel")),
        name="rel_bias_tiles",
    )(rel_bias, jnp.asarray(bucket), jnp.asarray(neg))


def _causal_conv(cur, w, scratch, carry, at_start):
    width = w.shape[0]
    rows = cur.shape[0]
    scratch[0:HALO, :] = jnp.where(at_start, 0.0, carry[...])
    scratch[HALO:, :] = cur
    carry[...] = cur[rows - HALO:]
    y = w[width - 1:width, :] * cur
    for j in range(width - 1):
        back = width - 1 - j
        y = y + w[j:j + 1, :] * scratch[HALO - back:HALO - back + rows, :]
    return y


def _even_in_body(tiles_per_seq, x_ref, wa_ref, wb_ref, cw_ref, oa_ref, ob_ref, scratch, carry):
    at_start = pl.program_id(0) % tiles_per_seq == 0
    x = x_ref[...].astype(BF16)
    for part in range(3):
        cols = slice(part * EVA_PART, (part + 1) * EVA_PART)
        y = _dot_nt(x, wa_ref[cols, :])
        c = _silu(_causal_conv(y, cw_ref[:, cols], scratch, carry.at[part], at_start))
        for hd in range(GDN_HEADS):
            ch = c[:, hd * GDN_D:(hd + 1) * GDN_D]
            if part < 2:
                inv = lax.rsqrt(jnp.sum(ch * ch, axis=-1, keepdims=True) + RMS_EPS)
                ch = ch * (inv * GDN_D ** -0.5 if part == 0 else inv)
            lo = part * EVA_PART + hd * GDN_D
            oa_ref[:, lo:lo + GDN_D] = ch.astype(oa_ref.dtype)
    z_cols = slice(3 * EVA_PART, 4 * EVA_PART)
    oa_ref[:, z_cols] = _dot_nt(x, wa_ref[z_cols, :]).astype(oa_ref.dtype)
    for c in range(EVB_COLS // EVB_TN):
        cols = slice(c * EVB_TN, (c + 1) * EVB_TN)
        ob_ref[:, cols] = _dot_nt(x, wb_ref[cols, :])


def _even_in_proj(x, w_a, w_b, conv_w, seq, tm):
    t, k = x.shape
    return pl.pallas_call(
        functools.partial(_even_in_body, seq // tm),
        grid=(t // tm,),
        in_specs=[pl.BlockSpec((tm, k), lambda i: (i, 0)),
                  _resident(w_a), _resident(w_b), _resident(conv_w)],
        out_specs=[pl.BlockSpec((tm, w_a.shape[0]), lambda i: (i, 0)),
                   pl.BlockSpec((tm, EVB_COLS), lambda i: (i, 0))],
        out_shape=[jax.ShapeDtypeStruct((t, w_a.shape[0]), BF16), jax.ShapeDtypeStruct((t, EVB_COLS), F32)],
        scratch_shapes=[pltpu.VMEM((HALO + tm, EVA_PART), F32), pltpu.VMEM((3, HALO, EVA_PART), F32)],
        compiler_params=_cparams(("arbitrary",)),
        name="even_in_proj",
    )(x, w_a, w_b, conv_w)


def _gdn_body(q_ref, k_ref, v_ref, z_ref, ba_ref, alog_ref, dtb_ref, nw_ref, o_ref, state):
    n = pl.program_id(1)

    @pl.when(n == 0)
    def _():
        state[...] = jnp.zeros_like(state)

    grp = GDN_GROUP
    nc = grp // CHUNK
    nh = GDN_HEADS
    hs = range(nh)
    ri = lax.broadcasted_iota(jnp.int32, (grp, grp), 0)
    ci = lax.broadcasted_iota(jnp.int32, (grp, grp), 1)
    same = (ri // CHUNK) == (ci // CHUNK)
    incl = same & (ri >= ci)
    strict = same & (ri > ci)
    eye = (ri == ci).astype(F32)
    tri = incl.astype(BF16)

    def chunk_local(groups):
        units = [(g, h) for g in range(len(groups)) for h in hs]
        gam_all, gam_rows, beta_all = [], [], []
        for rows in groups:
            ba = ba_ref[rows, :]
            beta_all.append(_sigmoid(ba))
            g_all = -jnp.exp(alog_ref[...]) * _softplus(ba + dtb_ref[...])
            g_hi = g_all.astype(BF16)
            rem = g_all - g_hi.astype(F32)
            g_mid = rem.astype(BF16)
            g_lo = (rem - g_mid.astype(F32)).astype(BF16)
            gam_all.append(jnp.dot(tri, g_hi, preferred_element_type=F32)
                           + jnp.dot(tri, g_mid, preferred_element_type=F32)
                           + jnp.dot(tri, g_lo, preferred_element_type=F32))
            gam_rows.append(gam_all[-1].T)

        def head(ref, g, h):
            return ref[groups[g], h * GDN_D:(h + 1) * GDN_D]

        q = [head(q_ref, g, h) for g, h in units]
        k = [head(k_ref, g, h) for g, h in units]
        v = [head(v_ref, g, h) for g, h in units]
        gam = [jnp.broadcast_to(gam_all[g][:, nh + h:nh + h + 1], (grp, GDN_D)) for g, h in units]
        bcol = [beta_all[g][:, h:h + 1] for g, h in units]
        us = range(len(units))
        decay, x, inv = [], [], []
        for i, (g, h) in enumerate(units):
            diff = gam[i][:, 0:1] - gam_rows[g][nh + h:nh + h + 1, :]
            decay.append(jnp.where(incl, jnp.exp(jnp.where(incl, diff, 0.0)), 0.0))
            kk = _bdot_nt(k[i], k[i])
            x.append(-jnp.where(strict, bcol[i] * kk * decay[i], 0.0))
            inv.append(eye + x[i])
        for _ in range(5):
            for i in us:
                x[i] = _bdot(x[i], x[i])
                inv[i] = inv[i] + _bdot(inv[i], x[i])
        out = [tuple([] for _ in range(6)) for _ in groups]
        for i, (g, h) in enumerate(units):
            eg = jnp.exp(gam[i])
            uw = _bdot(inv[i], jnp.concatenate([v[i] * bcol[i], k[i] * (bcol[i] * eg)], axis=1))
            kd, gl = [], []
            for c in range(nc):
                last = gam[i][(c + 1) * CHUNK - 1:(c + 1) * CHUNK, :]
                kd.append(k[i][c * CHUNK:(c + 1) * CHUNK] * jnp.exp(last - gam[i][c * CHUNK:(c + 1) * CHUNK]))
                gl.append(jnp.exp(last))
            for lst, val in zip(out[g], (uw[:, :GDN_D], uw[:, GDN_D:], _bdot_nt(q[i], k[i]) * decay[i],
                                         q[i] * eg, kd, gl)):
                lst.append(val)
        return out

    def recurrence(rows, local, s):
        u, w, qk, q_dec, k_dec, g_last = local
        q_s = [[] for _ in hs]
        delta = [[] for _ in hs]
        for c in range(nc):
            sl = slice(c * CHUNK, (c + 1) * CHUNK)
            for h in hs:
                r = _bdot(jnp.concatenate([w[h][sl], q_dec[h][sl]], axis=0), s[h])
                d = u[h][sl] - r[:CHUNK]
                q_s[h].append(r[CHUNK:])
                delta[h].append(d)
                s[h] = g_last[h][c] * s[h] + _bdot_tn(k_dec[h][c], d)
        for h in hs:
            o = jnp.concatenate(q_s[h], axis=0) + _bdot(qk[h], jnp.concatenate(delta[h], axis=0))
            o = o * lax.rsqrt(jnp.mean(o * o, axis=-1, keepdims=True) + RMS_EPS) * nw_ref[...]
            z = z_ref[rows, h * GDN_D:(h + 1) * GDN_D].astype(F32)
            o_ref[rows, h * GDN_D:(h + 1) * GDN_D] = (o * _silu(z)).astype(o_ref.dtype)
        return s

    groups = [slice(g * grp, (g + 1) * grp) for g in range(GDN_STEP_GROUPS)]
    s = [state[h] for h in hs]
    for rows, local in zip(groups, chunk_local(groups)):
        s = recurrence(rows, local, s)
    for h in hs:
        state[h] = s[h]


def _gdn(proj_a, proj_b, alog_pad, dtb_pad, norm_w, bsz, seq):
    t = proj_a.shape[0]
    rows = GDN_GROUP * GDN_STEP_GROUPS
    spb = seq // rows
    width = GDN_HEADS * GDN_D

    def at(col):
        return lambda b, n: (b * spb + n, col)

    return pl.pallas_call(
        _gdn_body,
        grid=(bsz, spb),
        in_specs=[pl.BlockSpec((rows, width), at(0)),
                  pl.BlockSpec((rows, width), at(1)),
                  pl.BlockSpec((rows, width), at(2)),
                  pl.BlockSpec((rows, width), at(3)),
                  pl.BlockSpec((rows, LANES), at(EVB_BA // LANES)),
                  pl.BlockSpec((1, LANES), lambda b, n: (0, 0)),
                  pl.BlockSpec((1, LANES), lambda b, n: (0, 0)),
                  pl.BlockSpec((1, LANES), lambda b, n: (0, 0))],
        out_specs=pl.BlockSpec((rows, width), at(0)),
        out_shape=jax.ShapeDtypeStruct((t, width), BF16),
        scratch_shapes=[pltpu.VMEM((GDN_HEADS, GDN_D, GDN_D), F32)],
        compiler_params=_cparams(("parallel", "arbitrary")),
        name="gdn",
    )(proj_a, proj_a, proj_a, proj_a, proj_b, alog_pad, dtb_pad, norm_w)


def _swa_bias_tables():
    qi = np.arange(SWA_BLOCK)[:, None] + SWA_BLOCK
    kj = np.arange(2 * SWA_BLOCK)[None, :]
    rel = qi - kj
    buckets, negs = [], []
    for has_prev in (True, False):
        for window, dilation in SWA_CONFIGS:
            valid = (rel >= 0) & (rel <= window // dilation) & (has_prev | (kj >= SWA_BLOCK))
            buckets.append(_rel_bucket_np(np.maximum(rel, 0) * dilation))
            negs.append(np.where(valid, 0.0, NEG_BIG))
    return np.stack(buckets).astype(np.int32), np.stack(negs).astype(np.float32)


def _swa_body(*refs):
    ng = len(SWA_CONFIGS)
    ins, bias_ref, o_ref = refs[:5 * ng], refs[5 * ng], refs[5 * ng + 1]
    scratch = refs[5 * ng + 2:]
    kbufs, vbufs, o_scr, lse_scr = scratch[:ng], scratch[ng:2 * ng], scratch[2 * ng], scratch[2 * ng + 1]
    j = pl.program_id(1)
    blk = SWA_BLOCK
    first_head = lax.broadcasted_iota(jnp.int32, (blk, LANES), 1) < SWA_DH

    for g, (_, d) in enumerate(SWA_CONFIGS):
        q_ref, kc_ref, kp_ref, vc_ref, vp_ref = ins[5 * g:5 * g + 5]
        kbuf, vbuf = kbufs[g], vbufs[g]
        halo = blk * d
        kbuf[0:halo, :] = kp_ref[...]
        kbuf[halo:, :] = kc_ref[...]
        vbuf[0:halo, :] = vp_ref[...]
        vbuf[halo:, :] = vc_ref[...]

        def unit(u, carry, g=g, d=d, halo=halo, q_ref=q_ref, kbuf=kbuf, vbuf=vbuf):
            base = (u // d) * halo + u % d
            q = q_ref[pl.ds(base, blk, stride=d), :] * (SWA_DH ** -0.5 * LOG2E)
            k = kbuf[pl.ds(base, 2 * blk, stride=d), :]
            v = vbuf[pl.ds(base, 2 * blk, stride=d), :]
            lhs = jnp.concatenate([jnp.where(first_head, q, 0.0), jnp.where(first_head, 0.0, q)], axis=0)
            no_prev = jnp.where((j == 0) & (u < d), 1, 0)
            s = _bdot_nt(lhs, k) + bias_ref[g, 0, no_prev]
            m = jnp.max(s, axis=-1, keepdims=True)
            p = jnp.exp2(s - m)
            l = jnp.sum(p, axis=-1, keepdims=True)
            o2 = _bdot(p, v) * (1.0 / l)
            lse2 = m + jnp.log2(l)
            o_scr[g, pl.ds(base, blk, stride=d), :] = jnp.where(first_head, o2[:blk], o2[blk:])
            lse_scr[g, pl.ds(base, blk, stride=d), :] = jnp.where(first_head, lse2[:blk], lse2[blk:])
            return carry

        lax.fori_loop(0, SWA_SPAN // blk, unit, 0, unroll=SWA_UNROLL)

    def combine(c, carry):
        rows = pl.ds(pl.multiple_of(c * 2 * blk, 2 * blk), 2 * blk)
        lse = [lse_scr[g, rows, :] for g in range(ng)]
        m = functools.reduce(jnp.maximum, lse)
        e = [jnp.exp2(x - m) for x in lse]
        den = functools.reduce(lambda x, y: x + y, e)
        o_ref[rows, :] = functools.reduce(
            lambda x, y: x + y, [(e[g] / den) * o_scr[g, rows, :] for g in range(ng)]).astype(o_ref.dtype)
        return carry

    lax.fori_loop(0, SWA_SPAN // (2 * blk), combine, 0)


def _swa(proj, bias, bsz, seq):
    t = proj.shape[0]
    ng = len(SWA_CONFIGS)
    nspan = seq // SWA_SPAN
    npair = SWA_HEADS * SWA_DH // LANES
    group_cols = SWA_HEADS * SWA_DH // LANES
    in_specs, scratch_k = [], []
    for g, (_, d) in enumerate(SWA_CONFIGS):
        halo = SWA_BLOCK * d
        per_span = SWA_SPAN // halo

        def cur(which, g=g):
            col = (which * ng + g) * group_cols
            return lambda b, j, p: (b * nspan + j, col + p)

        def prev(which, g=g, per_span=per_span):
            col = (which * ng + g) * group_cols
            return lambda b, j, p: (jnp.maximum((b * nspan + j) * per_span - 1, 0), col + p)

        in_specs += [pl.BlockSpec((SWA_SPAN, LANES), cur(0)),
                     pl.BlockSpec((SWA_SPAN, LANES), cur(1)), pl.BlockSpec((halo, LANES), prev(1)),
                     pl.BlockSpec((SWA_SPAN, LANES), cur(2)), pl.BlockSpec((halo, LANES), prev(2))]
        scratch_k.append(pltpu.VMEM((halo + SWA_SPAN, LANES), F32))
    in_specs.append(pl.BlockSpec((ng, 1, 2, 2 * SWA_BLOCK, 2 * SWA_BLOCK), lambda b, j, p: (0, p, 0, 0, 0)))
    return pl.pallas_call(
        _swa_body,
        grid=(bsz, nspan, npair),
        in_specs=in_specs,
        out_specs=pl.BlockSpec((SWA_SPAN, LANES), lambda b, j, p: (b * nspan + j, p)),
        out_shape=jax.ShapeDtypeStruct((t, npair * LANES), BF16),
        scratch_shapes=scratch_k + scratch_k + [pltpu.VMEM((ng, SWA_SPAN, LANES), F32)] * 2,
        compiler_params=_cparams(("parallel", "parallel", "parallel")),
        name="swa",
    )(*([proj] * (5 * ng)), bias)


def _layer_norm(y, g, b):
    mu = jnp.mean(y, axis=-1, keepdims=True)
    yc = y - mu
    var = jnp.mean(yc * yc, axis=-1, keepdims=True)
    return yc * lax.rsqrt(var + LN_EPS) * g + b


def _diff_bias_tables(blk):
    buckets = _rel_bucket_np(np.arange(2 * REL_MAX_DIST))
    far = int(np.max(np.nonzero(buckets != NUM_BUCKETS - 1)[0])) + 1
    nb = -(-(far + blk - 1) // blk)
    dist = np.stack([t * blk - blk + np.arange(2 * blk) for t in range(nb + 1)])[:, None, :]
    dist = np.broadcast_to(dist, (nb + 1, SUBLANES, 2 * blk))
    bucket = _rel_bucket_np(np.maximum(dist, 0))
    neg = np.where(dist >= 0, 0.0, NEG_BIG).astype(np.float32)
    return bucket, neg


def _diff_body(blk, nb, lam_init, q_ref, qn_ref, k_ref, v_ref, bias_ref, lam_ref, nw_ref, o_ref,
               vt, acc1, acc2, s_a, s_b):
    qi = pl.program_id(2)
    dv = 2 * DIFF_DH
    seq = k_ref.shape[0]

    @pl.when(qi == 0)
    def _():
        vt[dv:, :] = jnp.ones((vt.shape[0] - dv, seq), BF16)

        def fill(c, carry):
            st = pl.multiple_of(c * blk, blk)
            vt[0:dv, pl.ds(st, blk)] = v_ref[pl.ds(st, blk), :].astype(F32).T.astype(BF16)
            return carry

        lax.fori_loop(0, seq // blk, fill, 0)

    feature = lax.broadcasted_iota(jnp.int32, (dv, blk), 0)

    def components(ref):
        q_t = (ref[...].astype(F32) * (DIFF_DH ** -0.5 * LOG2E)).T
        return (jnp.where(feature < DIFF_DH, q_t, 0.0).astype(BF16),
                jnp.where(feature >= DIFF_DH, q_t, 0.0).astype(BF16))

    qs = components(q_ref)
    qs_next = components(qn_ref)
    accs = (acc1, acc2)
    for acc in accs:
        acc[...] = jnp.zeros_like(acc)

    last = pl.num_programs(2) - 1

    def key_rows(kj):
        return pl.ds(pl.multiple_of(jnp.minimum(kj, last) * blk, blk), blk)

    def scores(kj, dst, queries=qs):
        k = k_ref[key_rows(kj), :]
        for c, qc in enumerate(queries):
            dst[c] = jnp.dot(k, qc, preferred_element_type=F32)

    def consume(biased, kj, src, ms):
        vtb = vt[:, key_rows(kj)]
        out = []
        for c, (m, acc) in enumerate(zip(ms, accs)):
            s = src[c]
            if biased:
                s = s + bias_ref[0, jnp.minimum(qi - kj, nb)]
            m_new = jnp.maximum(m, jnp.max(s, axis=0, keepdims=True))
            p = jnp.exp2(s - m_new).astype(BF16)
            acc[...] = jnp.exp2(m - m_new) * acc[...] + jnp.dot(vtb, p, preferred_element_type=F32)
            out.append(m_new)
        return tuple(out)

    nblocks = qi + 1
    trips = nblocks // 2
    odd = nblocks % 2 == 1
    far_trips = jnp.maximum(qi - nb + 1, 0) // 2

    def pair(biased, t, ms):
        kj = 2 * t
        scores(kj + 1, s_b)
        ms = consume(biased, kj, s_a, ms)
        if biased:
            hand_off = (t == trips - 1) & jnp.logical_not(odd)
            scores(jnp.where(hand_off, 0, kj + 2), s_a,
                   tuple(jnp.where(hand_off, qn, qc) for qc, qn in zip(qs, qs_next)))
        else:
            scores(kj + 2, s_a)
        return consume(biased, kj + 1, s_b, ms)

    @pl.when(qi == 0)
    def _():
        scores(0, s_a)

    m0 = jnp.full((1, blk), NEG_BIG, F32)
    ms = lax.fori_loop(0, far_trips, functools.partial(pair, False), (m0, m0))
    ms = lax.fori_loop(far_trips, trips, functools.partial(pair, True), ms)

    def finish():
        lp = lam_ref[...]
        lam = (jnp.exp(jnp.sum(lp[0:1] * lp[1:2], axis=-1, keepdims=True))
               - jnp.exp(jnp.sum(lp[2:3] * lp[3:4], axis=-1, keepdims=True)) + lam_init)
        a1, a2 = acc1[...], acc2[...]
        o_t = a1[:dv] * (1.0 / a1[dv:dv + 1]) - a2[:dv] * (lam / a2[dv:dv + 1])
        o = o_t.T
        o = o * lax.rsqrt(jnp.mean(o * o, axis=-1, keepdims=True) + RMS_EPS) * nw_ref[...]
        o_ref[...] = (o * (1.0 - lam_init)).astype(o_ref.dtype)

    @pl.when(odd)
    def _():
        consume(True, qi, s_a, ms)
        scores(0, s_a, qs_next)
        finish()

    @pl.when(jnp.logical_not(odd))
    def _():
        finish()


def _diff_attention(proj, bias, lam_params, norm_w, lam_init, bsz, seq, blk):
    t = proj.shape[0]
    nq = seq // blk
    nt = bias.shape[1]
    nb = nt - 1
    width = 2 * DIFF_DH
    ones_rows = 2 * SUBLANES
    return pl.pallas_call(
        functools.partial(_diff_body, blk, nb, lam_init),
        grid=(bsz, DIFF_HEADS, nq),
        in_specs=[pl.BlockSpec((blk, width), lambda b, h, i: (b * nq + i, OD_QC // width + h)),
                  pl.BlockSpec((blk, width),
                               lambda b, h, i: (b * nq + jnp.minimum(i + 1, nq - 1), OD_QC // width + h)),
                  pl.BlockSpec((seq, width), lambda b, h, i: (b, OD_KC // width + h)),
                  pl.BlockSpec((seq, width), lambda b, h, i: (b, OD_VC // width + h)),
                  pl.BlockSpec((1, nt, blk, blk), lambda b, h, i: (h, 0, 0, 0)),
                  pl.BlockSpec((4, DIFF_DH), lambda b, h, i: (0, 0)),
                  pl.BlockSpec((1, width), lambda b, h, i: (0, 0))],
        out_specs=pl.BlockSpec((blk, width), lambda b, h, i: (b * nq + i, h)),
        out_shape=jax.ShapeDtypeStruct((t, DIFF_HEADS * width), BF16),
        scratch_shapes=[pltpu.VMEM((width + ones_rows, seq), BF16),
                        pltpu.VMEM((width + ones_rows, blk), F32), pltpu.VMEM((width + ones_rows, blk), F32),
                        pltpu.VMEM((2, blk, blk), F32), pltpu.VMEM((2, blk, blk), F32)],
        compiler_params=_cparams(("arbitrary", "arbitrary", "arbitrary")),
        name="diff_attn",
    )(proj, proj, proj, proj, bias, lam_params, norm_w)


def _gla_body(q_ref, k_ref, gd_ref, v_ref, r_ref, wg_ref, bg_ref, nw_ref, o_ref, state, part):
    n = pl.program_id(1)

    @pl.when(n == 0)
    def _():
        state[...] = jnp.zeros_like(state)

    rows = GLA_GROUP
    nc = rows // CHUNK
    npair = GLA_HEADS // 2
    gate = jnp.dot(gd_ref[...], wg_ref[...], precision=HI, preferred_element_type=F32) + bg_ref[...]
    log_a = _log_sigmoid(gate) * (1.0 / GLA_TAU)

    ri = lax.broadcasted_iota(jnp.int32, (rows, rows), 0)
    ci = lax.broadcasted_iota(jnp.int32, (rows, rows), 1)
    causal = ((ri // CHUNK) == (ci // CHUNK)) & (ri >= ci)
    tri = causal.astype(BF16)
    hi = log_a.astype(BF16)
    rem = log_a - hi.astype(F32)
    mid = rem.astype(BF16)
    lo = (rem - mid.astype(F32)).astype(BF16)
    b_all = (jnp.dot(tri, hi, preferred_element_type=F32) + jnp.dot(tri, mid, preferred_element_type=F32)
             + jnp.dot(tri, lo, preferred_element_type=F32))

    lane = lax.broadcasted_iota(jnp.int32, (rows, LANES), 1)
    lane_c = lax.broadcasted_iota(jnp.int32, (CHUNK, LANES), 1)
    head_lanes = (lane < GLA_DK, lane >= GLA_DK)
    chunk_lanes = (lane_c < GLA_DK, lane_c >= GLA_DK)
    pairs = []
    for p in range(npair):
        cols = slice(p * LANES, (p + 1) * LANES)
        pairs.append(dict(
            b=b_all[:, cols],
            q=q_ref[:, cols].astype(F32) * GLA_DK ** -0.5,
            k=k_ref[:, cols].astype(F32),
            v=[v_ref[:, (2 * p + hd) * GLA_DV:(2 * p + hd + 1) * GLA_DV].astype(F32) for hd in range(2)]))

    for pr in pairs:
        pr["q_dec"] = pr["q"] * jnp.exp(pr["b"])
    for c in range(nc):
        sl = slice(c * CHUNK, (c + 1) * CHUNK)
        for p, pr in enumerate(pairs):
            bc = pr["b"][sl]
            b_last = bc[CHUNK - 1:CHUNK]
            k_dec = pr["k"][sl] * jnp.exp(b_last - bc)
            e_last = jnp.exp(b_last)
            for hd in range(2):
                h = 2 * p + hd
                st = state[h]
                part[h, sl, :] = _bdot_nt(jnp.where(chunk_lanes[hd], pr["q_dec"][sl], 0.0), st)
                state[h] = st * e_last + _bdot_tn(pr["v"][hd][sl], k_dec)

    def finish(intra):
        for h in range(GLA_HEADS):
            o = part[h] + intra[h]
            o = o * lax.rsqrt(jnp.mean(o * o, axis=-1, keepdims=True) + RMS_EPS) * nw_ref[...]
            gate_r = _silu(r_ref[:, h * GLA_DV:(h + 1) * GLA_DV].astype(F32))
            o_ref[:, h * GLA_DV:(h + 1) * GLA_DV] = (o * gate_r).astype(o_ref.dtype)

    def intra_whole_chunk():
        out = []
        for pr in pairs:
            k_inv = pr["k"] * jnp.exp(jnp.minimum(-pr["b"], GLA_MAX_DECAY))
            for hd in range(2):
                a = _bdot_nt(jnp.where(head_lanes[hd], pr["q_dec"], 0.0), k_inv)
                out.append(_bdot(jnp.where(causal, a, 0.0), pr["v"][hd]))
        return out

    def intra_exact():
        out = []
        for pr in pairs:
            off = _gla_intra_off_diagonal(pr["q"], pr["k"], pr["b"], pr["v"], nc)
            diag = _gla_intra_diagonal(pr["q"], pr["k"], pr["b"], pr["v"])
            out += [off[hd] + diag[hd] for hd in range(2)]
        return out

    chunk_decay = jnp.max(-b_all.reshape(nc, CHUNK, npair * LANES)[:, CHUNK - 1:CHUNK, :])

    @pl.when(chunk_decay <= GLA_MAX_DECAY)
    def _():
        finish(intra_whole_chunk())

    @pl.when(chunk_decay > GLA_MAX_DECAY)
    def _():
        finish(intra_exact())


def _gla_intra_off_diagonal(q, k, b, vs, nc):
    per_chunk = CHUNK // GLA_SUB
    lane = lax.broadcasted_iota(jnp.int32, (GLA_SUB, LANES), 1)
    sub_mask = (lane < GLA_DK, lane >= GLA_DK)
    kcol = lax.broadcasted_iota(jnp.int32, (GLA_SUB, CHUNK), 1)
    outs = ([], [])
    for c in range(nc):
        sl = slice(c * CHUNK, (c + 1) * CHUNK)
        bc, qc, kc = b[sl], q[sl], k[sl]
        a_rows = [[jnp.zeros((GLA_SUB, CHUNK), F32)] for _ in vs]
        for blk in range(1, per_chunk):
            r0 = blk * GLA_SUB
            bref = bc[r0:r0 + 1]
            qs = qc[r0:r0 + GLA_SUB] * jnp.exp(bc[r0:r0 + GLA_SUB] - bref)
            ks = kc * jnp.exp(jnp.minimum(bref - bc, 0.0))
            for hd in range(2):
                a = _bdot_nt(jnp.where(sub_mask[hd], qs, 0.0), ks)
                a_rows[hd].append(jnp.where(kcol < r0, a, 0.0))
        for hd in range(2):
            outs[hd].append(_bdot(jnp.concatenate(a_rows[hd], axis=0), vs[hd][sl]))
    return [jnp.concatenate(o, axis=0) for o in outs]


def _gla_intra_diagonal(q, k, b, vs):
    rows = q.shape[0]
    nsub = rows // GLA_SUB
    b3 = b.reshape(nsub, GLA_SUB, LANES)
    q3 = q.reshape(nsub, GLA_SUB, LANES)
    k3 = k.reshape(nsub, GLA_SUB, LANES)
    v3 = [v.reshape(nsub, GLA_SUB, LANES) for v in vs]
    row3 = lax.broadcasted_iota(jnp.int32, (nsub, GLA_SUB, LANES), 1)
    lane3 = lax.broadcasted_iota(jnp.int32, (nsub, GLA_SUB, LANES), 2)
    rowc = lax.broadcasted_iota(jnp.int32, (nsub, GLA_SUB, 1), 1)
    o3 = [jnp.zeros((nsub, GLA_SUB, LANES), F32) for _ in vs]
    for jj in range(GLA_SUB):
        e = jnp.exp(jnp.where(row3 >= jj, b3 - b3[:, jj:jj + 1, :], 0.0))
        t = q3 * k3[:, jj:jj + 1, :] * e
        w_all = jnp.sum(t, axis=-1, keepdims=True)
        w_a = jnp.sum(jnp.where(lane3 < GLA_DK, t, 0.0), axis=-1, keepdims=True)
        for hd, w in enumerate((w_a, w_all - w_a)):
            o3[hd] = o3[hd] + jnp.where(rowc >= jj, w, 0.0) * v3[hd][:, jj:jj + 1, :]
    return [o.reshape(rows, LANES) for o in o3]


def _gla(proj, gate_in, w_gate_pad, b_gate, norm_w, bsz, seq):
    t = proj.shape[0]
    rows = GLA_GROUP
    spb = seq // rows
    qk_w = GLA_HEADS * GLA_DK
    v_w = GLA_HEADS * GLA_DV

    def at(col):
        return lambda b, n: (b * spb + n, col)

    def whole(a):
        return pl.BlockSpec(a.shape, lambda b, n: (0, 0))

    return pl.pallas_call(
        _gla_body,
        grid=(bsz, spb),
        in_specs=[pl.BlockSpec((rows, qk_w), at(OD_QD // qk_w)),
                  pl.BlockSpec((rows, qk_w), at(OD_KD // qk_w)),
                  pl.BlockSpec((rows, LANES), at(0)),
                  pl.BlockSpec((rows, v_w), at(OD_VD // v_w)),
                  pl.BlockSpec((rows, v_w), at(OD_RD // v_w)),
                  whole(w_gate_pad), whole(b_gate), whole(norm_w)],
        out_specs=pl.BlockSpec((rows, v_w), at(0)),
        out_shape=jax.ShapeDtypeStruct((t, v_w), BF16),
        scratch_shapes=[pltpu.VMEM((GLA_HEADS, GLA_DV, LANES), F32), pltpu.VMEM((GLA_HEADS, rows, GLA_DV), F32)],
        compiler_params=_cparams(("parallel", "arbitrary")),
        name="gla",
    )(proj, proj, gate_in, proj, proj, w_gate_pad, b_gate, norm_w)


def _tail_body(tiles_per_seq, tc, ma_ref, mb_ref, h_ref, woa_ref, wob_ref,
               g1_ref, b1_ref, wu_ref, cw_ref, cb_ref, wd_ref, g2_ref, b2_ref, y_ref,
               act, scratch_g, scratch_v, carry):
    at_start = pl.program_id(0) % tiles_per_seq == 0
    mix = (jnp.dot(ma_ref[...], woa_ref[...], preferred_element_type=F32)
           + jnp.dot(mb_ref[...], wob_ref[...], preferred_element_type=F32))
    x = _layer_norm(DEEPNORM_ALPHA * h_ref[...] + mix, g1_ref[...], b1_ref[...])
    xb = x.astype(BF16)
    nchunk = D_FF // tc

    def branch(idx, scratch):
        lo = idx * tc
        cur = jnp.dot(xb, wu_ref[:, lo:lo + tc], preferred_element_type=F32)
        return _causal_conv(cur, cw_ref[:, lo:lo + tc], scratch, carry.at[idx], at_start) + cb_ref[:, lo:lo + tc]

    for c in range(nchunk):
        gate = branch(c, scratch_g)
        val = branch(nchunk + c, scratch_v)
        act[:, c * tc:(c + 1) * tc] = (_silu(gate) * val).astype(BF16)
    ffn = jnp.dot(act[...], wd_ref[...], preferred_element_type=F32)
    y_ref[...] = _layer_norm(DEEPNORM_ALPHA * x + ffn, g2_ref[...], b2_ref[...])


def _layer_tail(mix_a, mix_b, h, w_out_a, w_out_b, g1, b1, w_up, conv_w, conv_b, w_down, g2, b2,
                layer, seq):
    t = h.shape[0]
    tm, tc = TAIL_TM, FFN_TC

    def tile(a):
        return pl.BlockSpec((tm, a.shape[1]), lambda i: (i, 0))

    params = (w_out_a, w_out_b, g1, b1, w_up, conv_w, conv_b, w_down, g2, b2)
    param_specs = [_resident(w_out_a), _resident(w_out_b), _resident(g1), _resident(b1),
                   _resident(w_up, layer), _resident(conv_w), _resident(conv_b), _resident(w_down, layer),
                   _resident(g2), _resident(b2)]
    return pl.pallas_call(
        functools.partial(_tail_body, seq // tm, tc),
        grid=(t // tm,),
        in_specs=[tile(mix_a), tile(mix_b), tile(h)] + param_specs,
        out_specs=pl.BlockSpec((tm, D_MODEL), lambda i: (i, 0)),
        out_shape=jax.ShapeDtypeStruct((t, D_MODEL), F32),
        scratch_shapes=[pltpu.VMEM((tm, D_FF), BF16), pltpu.VMEM((HALO + tm, tc), F32),
                        pltpu.VMEM((HALO + tm, tc), F32), pltpu.VMEM((2 * D_FF // tc, HALO, tc), F32)],
        compiler_params=_cparams(("arbitrary",)),
        name="layer_tail",
    )(mix_a, mix_b, h, *params)


def _even_w_in(w):
    w_t = w.T
    a_end = 4 * EVA_PART
    gates = w_t[a_end:a_end + 2 * GDN_HEADS]
    qkv_b = w_t[a_end + 2 * GDN_HEADS:]
    pad = jnp.zeros((EVB_COLS - EVB_BA - 2 * GDN_HEADS, w.shape[0]), w.dtype)
    return w_t[:a_end].astype(BF16), jnp.concatenate([qkv_b, gates, pad], axis=0).astype(BF16)


def _odd_w_in(w):
    w_t = w.T
    pad = jnp.zeros((LANES - GLA_RANK, w.shape[0]), w.dtype)
    return w_t[:OD_COLS].astype(BF16), jnp.concatenate([w_t[OD_COLS:], pad], axis=0).astype(BF16)


def _even_mixer(h, rel_bias, w_in, conv_w, a_log, dt_bias, norm_w, w_out, bsz, seq):
    w_a, w_b = _even_w_in(w_in)
    proj_a, proj_b = _even_in_proj(h, w_a, w_b, conv_w, seq, PROJ_TM)
    gate_pad = jnp.zeros((1, LANES), F32)
    alog_pad = lax.dynamic_update_slice(gate_pad, a_log[None].astype(F32), (0, GDN_HEADS))
    dtb_pad = lax.dynamic_update_slice(gate_pad, dt_bias[None].astype(F32), (0, GDN_HEADS))
    o_a = _gdn(proj_a, proj_b, alog_pad, dtb_pad, norm_w[None], bsz, seq)
    tiles = _bias_tiles(rel_bias, *_swa_bias_tables(), scale=LOG2E)
    ng, two_blk = len(SWA_CONFIGS), 2 * SWA_BLOCK
    bias = tiles.reshape(SWA_HEADS // 2, 2, 2, ng, SWA_BLOCK, two_blk).transpose(3, 0, 2, 1, 4, 5)
    o_b = _swa(proj_b, bias.reshape(ng, SWA_HEADS // 2, 2, two_blk, two_blk), bsz, seq)
    w_out = w_out.astype(BF16)
    return o_a, o_b, w_out[:EVA_PART], w_out[EVA_PART:]


def _odd_mixer(h, rel_bias, w_in, lam_params, diff_norm_w, w_gate, b_gate, gla_norm_w, w_out,
               lam_init, bsz, seq):
    w_main, w_gd = _odd_w_in(w_in)
    proj, gate_in = _odd_in_proj(h, w_main, w_gd, PROJ_TM)
    blk = min(DIFF_BLOCK, seq)
    bucket, neg = _diff_bias_tables(blk)
    bias = _bias_tiles(rel_bias, bucket, neg, scale=LOG2E, base_bucket=NUM_BUCKETS - 1, toeplitz=True)
    o_c = _diff_attention(proj, bias, lam_params, diff_norm_w[None], lam_init, bsz, seq, blk)
    w_gate_pad = jnp.concatenate(
        [w_gate, jnp.zeros((LANES - GLA_RANK, w_gate.shape[1]), w_gate.dtype)], axis=0)
    o_d = _gla(proj, gate_in, w_gate_pad, b_gate[None], gla_norm_w[None], bsz, seq)
    diff_v = DIFF_HEADS * 2 * DIFF_DH
    w_out = w_out.astype(BF16)
    return o_c, o_d, w_out[:diff_v], w_out[diff_v:]


def kernel(x, rel_bias, w_in_even, gdn_conv_w, gdn_a_log, gdn_dt_bias, gdn_norm_w, w_out_even,
           w_in_odd, diff_lambda, diff_norm_w, gla_w_gate, gla_b_gate, gla_norm_w, w_out_odd,
           ffn_w_up, ffn_conv_w, ffn_conv_b, ffn_w_down, ln_g, ln_b):
    bsz, seq, d = x.shape
    h = x.reshape(bsz * seq, d)
    w_up, w_down = ffn_w_up.astype(BF16), ffn_w_down.astype(BF16)
    for layer in range(DEPTH):
        i = layer // 2
        if layer % 2 == 0:
            mixed = _even_mixer(h, rel_bias, w_in_even[i], gdn_conv_w[i], gdn_a_log[i], gdn_dt_bias[i],
                                gdn_norm_w[i], w_out_even[i], bsz, seq)
        else:
            lam_init = 0.8 - 0.6 * math.exp(-0.3 * layer)
            mixed = _odd_mixer(h, rel_bias, w_in_odd[i], diff_lambda[i], diff_norm_w[i], gla_w_gate[i],
                               gla_b_gate[i], gla_norm_w[i], w_out_odd[i], lam_init, bsz, seq)
        h = _layer_tail(*mixed[:2], h, *mixed[2:], ln_g[layer, 0][None], ln_b[layer, 0][None],
                        w_up, ffn_conv_w[layer], ffn_conv_b[layer][None], w_down,
                        ln_g[layer, 1][None], ln_b[layer, 1][None], layer, seq)
    return h.reshape(bsz, seq, d)
```

```python
import functools
import math

import numpy as np
import jax
import jax.numpy as jnp
from jax import lax
from jax.experimental import pallas as pl
from jax.experimental.pallas import tpu as pltpu

F32 = jnp.float32
BF16 = jnp.bfloat16
HI = lax.Precision.HIGHEST

D_MODEL = 1024
DEPTH = 2
DEEPNORM_ALPHA = (2 * DEPTH) ** 0.25
LN_EPS = 1e-5
RMS_EPS = 1e-6
NUM_BUCKETS = 32
REL_MAX_DIST = 2048
GDN_HEADS = 6
GDN_D = 128
CHUNK = 64
GDN_GROUP = 256
GDN_STEP_GROUPS = 2
SWA_CONFIGS = ((128, 1), (512, 4), (2048, 16))
SWA_HEADS = 4
SWA_DH = 64
SWA_BLOCK = 128
SWA_UNROLL = 8
SWA_SPAN = SWA_BLOCK * max(d for _, d in SWA_CONFIGS)
DIFF_HEADS = 4
DIFF_DH = 64
DIFF_BLOCK = 512
LOG2E = math.log2(math.e)
GLA_HEADS = 4
GLA_DK = 64
GLA_DV = 128
GLA_RANK = 16
GLA_TAU = 16.0
GLA_SUB = 16
GLA_GROUP = 256
GLA_STEP_GROUPS = 2
GLA_MAX_DECAY = 60.0
D_FF = 2816

LANES = 128
SUBLANES = 8
HALO = 8
VMEM_LIMIT = 56 * 1024 * 1024
NEG_BIG = -1e30

EVA_PART = GDN_HEADS * GDN_D
EVB_BA = 2304
EVB_COLS = 2560
EVB_TN = 512
OD_QC, OD_KC, OD_VC = 0, 512, 1024
OD_QD, OD_KD, OD_VD, OD_RD = 1536, 1792, 2048, 2560
OD_COLS = 3072
OD_TN = 768
PROJ_TM = 512
TAIL_TM = 512
FFN_TC = 256


def _cparams(sem):
    return pltpu.CompilerParams(dimension_semantics=sem, vmem_limit_bytes=VMEM_LIMIT)


def _bdot(a, b):
    return jnp.dot(a.astype(BF16), b.astype(BF16), preferred_element_type=F32)


def _dot_nt(a, b):
    return lax.dot_general(a, b, (((1,), (1,)), ((), ())), preferred_element_type=F32)


def _bdot_nt(a, b):
    return lax.dot_general(a.astype(BF16), b.astype(BF16), (((1,), (1,)), ((), ())),
                           preferred_element_type=F32)


def _bdot_tn(a, b):
    return lax.dot_general(a.astype(BF16), b.astype(BF16), (((0,), (0,)), ((), ())),
                           preferred_element_type=F32)


def _sigmoid(x):
    return 1.0 / (1.0 + jnp.exp(-x))


def _silu(x):
    return x * _sigmoid(x)


def _softplus(x):
    return jnp.maximum(x, 0.0) + jnp.log1p(jnp.exp(-jnp.abs(x)))


def _log_sigmoid(x):
    return -_softplus(-x)


def _resident(a, lead=None):
    if lead is None:
        return pl.BlockSpec(a.shape, lambda i: (0,) * a.ndim, pipeline_mode=pl.Buffered(1))
    return pl.BlockSpec((None,) + a.shape[1:], lambda i: (lead,) + (0,) * (a.ndim - 1),
                        pipeline_mode=pl.Buffered(1))


def _odd_in_body(x_ref, w_ref, wg_ref, o_ref, og_ref):
    xb = x_ref[...].astype(BF16)
    for c in range(OD_COLS // OD_TN):
        cols = slice(c * OD_TN, (c + 1) * OD_TN)
        o_ref[:, cols] = _dot_nt(xb, w_ref[cols, :]).astype(o_ref.dtype)
    og_ref[...] = _dot_nt(xb, wg_ref[...])


def _odd_in_proj(x, w, w_gate_in, tm):
    t, k = x.shape
    return pl.pallas_call(
        _odd_in_body,
        grid=(t // tm,),
        in_specs=[pl.BlockSpec((tm, k), lambda i: (i, 0)), _resident(w), _resident(w_gate_in)],
        out_specs=[pl.BlockSpec((tm, OD_COLS), lambda i: (i, 0)), pl.BlockSpec((tm, LANES), lambda i: (i, 0))],
        out_shape=[jax.ShapeDtypeStruct((t, OD_COLS), BF16), jax.ShapeDtypeStruct((t, LANES), F32)],
        compiler_params=_cparams(("parallel",)),
        name="odd_in_proj",
    )(x, w, w_gate_in)


def _rel_bucket_np(dist):
    max_exact = NUM_BUCKETS // 2
    d = np.maximum(dist, 1).astype(np.float32)
    large = max_exact + (np.log(d / max_exact) / math.log(REL_MAX_DIST / max_exact)
                         * (NUM_BUCKETS - max_exact)).astype(np.int32)
    large = np.minimum(large, NUM_BUCKETS - 1)
    return np.where(dist < max_exact, dist, large).astype(np.int32)


def _bias_body(scale, base_bucket, tile_buckets, toeplitz, rb_ref, bucket_ref, neg_ref, o_ref):
    h = pl.program_id(0)
    t = pl.program_id(1)
    base = 0.0 if base_bucket is None else rb_ref[base_bucket, h]
    for tile, present in enumerate(tile_buckets):
        @pl.when(t == tile)
        def _(present=present):
            bucket = bucket_ref[0]
            acc = neg_ref[0]
            for b in present:
                acc = acc + jnp.where(bucket == b, (rb_ref[b, h] - base) * scale, 0.0)
            if toeplitz:
                n = o_ref.shape[2]
                rows = jnp.broadcast_to(acc[0:1], (n, 2 * n))
                acc = pltpu.roll(rows, n, 1, stride=1, stride_axis=0)[:, :n]
            o_ref[0, 0] = acc


def _bias_tiles(rel_bias, bucket, neg, scale=1.0, base_bucket=None, toeplitz=False):
    nt, r, c = bucket.shape
    nh = rel_bias.shape[1]
    out_r, out_c = (c // 2, c // 2) if toeplitz else (r, c)
    tile_buckets = tuple(tuple(int(b) for b in np.unique(bucket[t][neg[t] == 0]) if b != base_bucket)
                         for t in range(nt))
    return pl.pallas_call(
        functools.partial(_bias_body, scale, base_bucket, tile_buckets, toeplitz),
        grid=(nh, nt),
        in_specs=[pl.BlockSpec(memory_space=pltpu.SMEM),
                  pl.BlockSpec((1, r, c), lambda h, t: (t, 0, 0)),
                  pl.BlockSpec((1, r, c), lambda h, t: (t, 0, 0))],
        out_specs=pl.BlockSpec((1, 1, out_r, out_c), lambda h, t: (h, t, 0, 0)),
        out_shape=jax.ShapeDtypeStruct((nh, nt, out_r, out_c), F32),
        compiler_params=_cparams(("parallel", "parallel")),
        name="rel_bias_tiles",
    )(rel_bias, jnp.asarray(bucket), jnp.asarray(neg))


def _causal_conv(cur, w, scratch, carry, at_start):
    width = w.shape[0]
    rows = cur.shape[0]
    scratch[0:HALO, :] = jnp.where(at_start, 0.0, carry[...])
    scratch[HALO:, :] = cur
    carry[...] = cur[rows - HALO:]
    y = w[width - 1:width, :] * cur
    for j in range(width - 1):
        back = width - 1 - j
        y = y + w[j:j + 1, :] * scratch[HALO - back:HALO - back + rows, :]
    return y


def _even_in_body(tiles_per_seq, x_ref, wa_ref, wb_ref, cw_ref, oa_ref, ob_ref, scratch, carry):
    at_start = pl.program_id(0) % tiles_per_seq == 0
    x = x_ref[...].astype(BF16)
    for part in range(3):
        cols = slice(part * EVA_PART, (part + 1) * EVA_PART)
        y = _dot_nt(x, wa_ref[cols, :])
        c = _silu(_causal_conv(y, cw_ref[:, cols], scratch, carry.at[part], at_start))
        for hd in range(GDN_HEADS):
            ch = c[:, hd * GDN_D:(hd + 1) * GDN_D]
            if part < 2:
                inv = lax.rsqrt(jnp.sum(ch * ch, axis=-1, keepdims=True) + RMS_EPS)
                ch = ch * (inv * GDN_D ** -0.5 if part == 0 else inv)
            lo = part * EVA_PART + hd * GDN_D
            oa_ref[:, lo:lo + GDN_D] = ch.astype(oa_ref.dtype)
    z_cols = slice(3 * EVA_PART, 4 * EVA_PART)
    oa_ref[:, z_cols] = _dot_nt(x, wa_ref[z_cols, :]).astype(oa_ref.dtype)
    for c in range(EVB_COLS // EVB_TN):
        cols = slice(c * EVB_TN, (c + 1) * EVB_TN)
        ob_ref[:, cols] = _dot_nt(x, wb_ref[cols, :])


def _even_in_proj(x, w_a, w_b, conv_w, seq, tm):
    t, k = x.shape
    return pl.pallas_call(
        functools.partial(_even_in_body, seq // tm),
        grid=(t // tm,),
        in_specs=[pl.BlockSpec((tm, k), lambda i: (i, 0)),
                  _resident(w_a), _resident(w_b), _resident(conv_w)],
        out_specs=[pl.BlockSpec((tm, w_a.shape[0]), lambda i: (i, 0)),
                   pl.BlockSpec((tm, EVB_COLS), lambda i: (i, 0))],
        out_shape=[jax.ShapeDtypeStruct((t, w_a.shape[0]), BF16), jax.ShapeDtypeStruct((t, EVB_COLS), F32)],
        scratch_shapes=[pltpu.VMEM((HALO + tm, EVA_PART), F32), pltpu.VMEM((3, HALO, EVA_PART), F32)],
        compiler_params=_cparams(("arbitrary",)),
        name="even_in_proj",
    )(x, w_a, w_b, conv_w)


def _gdn_body(q_ref, k_ref, v_ref, z_ref, ba_ref, alog_ref, dtb_ref, nw_ref, o_ref, state):
    n = pl.program_id(1)

    @pl.when(n == 0)
    def _():
        state[...] = jnp.zeros_like(state)

    grp = GDN_GROUP
    nc = grp // CHUNK
    nh = GDN_HEADS
    hs = range(nh)
    ri = lax.broadcasted_iota(jnp.int32, (grp, grp), 0)
    ci = lax.broadcasted_iota(jnp.int32, (grp, grp), 1)
    same = (ri // CHUNK) == (ci // CHUNK)
    incl = same & (ri >= ci)
    strict = same & (ri > ci)
    eye = (ri == ci).astype(F32)
    tri = incl.astype(BF16)

    def chunk_local(groups):
        units = [(g, h) for g in range(len(groups)) for h in hs]
        gam_all, gam_rows, beta_all = [], [], []
        for rows in groups:
            ba = ba_ref[rows, :]
            beta_all.append(_sigmoid(ba))
            g_all = -jnp.exp(alog_ref[...]) * _softplus(ba + dtb_ref[...])
            g_hi = g_all.astype(BF16)
            rem = g_all - g_hi.astype(F32)
            g_mid = rem.astype(BF16)
            g_lo = (rem - g_mid.astype(F32)).astype(BF16)
            gam_all.append(jnp.dot(tri, g_hi, preferred_element_type=F32)
                           + jnp.dot(tri, g_mid, preferred_element_type=F32)
                           + jnp.dot(tri, g_lo, preferred_element_type=F32))
            gam_rows.append(gam_all[-1].T)

        def head(ref, g, h):
            return ref[groups[g], h * GDN_D:(h + 1) * GDN_D]

        q = [head(q_ref, g, h) for g, h in units]
        k = [head(k_ref, g, h) for g, h in units]
        v = [head(v_ref, g, h) for g, h in units]
        gam = [jnp.broadcast_to(gam_all[g][:, nh + h:nh + h + 1], (grp, GDN_D)) for g, h in units]
        bcol = [beta_all[g][:, h:h + 1] for g, h in units]
        us = range(len(units))
        decay, x, inv = [], [], []
        for i, (g, h) in enumerate(units):
            diff = gam[i][:, 0:1] - gam_rows[g][nh + h:nh + h + 1, :]
            decay.append(jnp.where(incl, jnp.exp(jnp.where(incl, diff, 0.0)), 0.0))
            kk = _bdot_nt(k[i], k[i])
            x.append(-jnp.where(strict, bcol[i] * kk * decay[i], 0.0))
            inv.append(eye + x[i])
        for _ in range(5):
            for i in us:
                x[i] = _bdot(x[i], x[i])
                inv[i] = inv[i] + _bdot(inv[i], x[i])
        out = [tuple([] for _ in range(6)) for _ in groups]
        for i, (g, h) in enumerate(units):
            eg = jnp.exp(gam[i])
            uw = _bdot(inv[i], jnp.concatenate([v[i] * bcol[i], k[i] * (bcol[i] * eg)], axis=1))
            kd, gl = [], []
            for c in range(nc):
                last = gam[i][(c + 1) * CHUNK - 1:(c + 1) * CHUNK, :]
                kd.append(k[i][c * CHUNK:(c + 1) * CHUNK] * jnp.exp(last - gam[i][c * CHUNK:(c + 1) * CHUNK]))
                gl.append(jnp.exp(last))
            for lst, val in zip(out[g], (uw[:, :GDN_D], uw[:, GDN_D:], _bdot_nt(q[i], k[i]) * decay[i],
                                         q[i] * eg, kd, gl)):
                lst.append(val)
        return out

    def recurrence(rows, local, s):
        u, w, qk, q_dec, k_dec, g_last = local
        q_s = [[] for _ in hs]
        delta = [[] for _ in hs]
        for c in range(nc):
            sl = slice(c * CHUNK, (c + 1) * CHUNK)
            for h in hs:
                r = _bdot(jnp.concatenate([w[h][sl], q_dec[h][sl]], axis=0), s[h])
                d = u[h][sl] - r[:CHUNK]
                q_s[h].append(r[CHUNK:])
                delta[h].append(d)
                s[h] = g_last[h][c] * s[h] + _bdot_tn(k_dec[h][c], d)
        for h in hs:
            o = jnp.concatenate(q_s[h], axis=0) + _bdot(qk[h], jnp.concatenate(delta[h], axis=0))
            o = o * lax.rsqrt(jnp.mean(o * o, axis=-1, keepdims=True) + RMS_EPS) * nw_ref[...]
            z = z_ref[rows, h * GDN_D:(h + 1) * GDN_D].astype(F32)
            o_ref[rows, h * GDN_D:(h + 1) * GDN_D] = (o * _silu(z)).astype(o_ref.dtype)
        return s

    groups = [slice(g * grp, (g + 1) * grp) for g in range(GDN_STEP_GROUPS)]
    s = [state[h] for h in hs]
    for rows, local in zip(groups, chunk_local(groups)):
        s = recurrence(rows, local, s)
    for h in hs:
        state[h] = s[h]


def _gdn(proj_a, proj_b, alog_pad, dtb_pad, norm_w, bsz, seq):
    t = proj_a.shape[0]
    rows = GDN_GROUP * GDN_STEP_GROUPS
    spb = seq // rows
    width = GDN_HEADS * GDN_D

    def at(col):
        return lambda b, n: (b * spb + n, col)

    return pl.pallas_call(
        _gdn_body,
        grid=(bsz, spb),
        in_specs=[pl.BlockSpec((rows, width), at(0)),
                  pl.BlockSpec((rows, width), at(1)),
                  pl.BlockSpec((rows, width), at(2)),
                  pl.BlockSpec((rows, width), at(3)),
                  pl.BlockSpec((rows, LANES), at(EVB_BA // LANES)),
                  pl.BlockSpec((1, LANES), lambda b, n: (0, 0)),
                  pl.BlockSpec((1, LANES), lambda b, n: (0, 0)),
                  pl.BlockSpec((1, LANES), lambda b, n: (0, 0))],
        out_specs=pl.BlockSpec((rows, width), at(0)),
        out_shape=jax.ShapeDtypeStruct((t, width), BF16),
        scratch_shapes=[pltpu.VMEM((GDN_HEADS, GDN_D, GDN_D), F32)],
        compiler_params=_cparams(("parallel", "arbitrary")),
        name="gdn",
    )(proj_a, proj_a, proj_a, proj_a, proj_b, alog_pad, dtb_pad, norm_w)


def _swa_bias_tables():
    qi = np.arange(SWA_BLOCK)[:, None] + SWA_BLOCK
    kj = np.arange(2 * SWA_BLOCK)[None, :]
    rel = qi - kj
    buckets, negs = [], []
    for has_prev in (True, False):
        for window, dilation in SWA_CONFIGS:
            valid = (rel >= 0) & (rel <= window // dilation) & (has_prev | (kj >= SWA_BLOCK))
            buckets.append(_rel_bucket_np(np.maximum(rel, 0) * dilation))
            negs.append(np.where(valid, 0.0, NEG_BIG))
    return np.stack(buckets).astype(np.int32), np.stack(negs).astype(np.float32)


def _swa_body(*refs):
    ng = len(SWA_CONFIGS)
    ins, bias_ref, o_ref = refs[:5 * ng], refs[5 * ng], refs[5 * ng + 1]
    scratch = refs[5 * ng + 2:]
    kbufs, vbufs, o_scr, lse_scr = scratch[:ng], scratch[ng:2 * ng], scratch[2 * ng], scratch[2 * ng + 1]
    j = pl.program_id(1)
    blk = SWA_BLOCK
    first_head = lax.broadcasted_iota(jnp.int32, (blk, LANES), 1) < SWA_DH

    for g, (_, d) in enumerate(SWA_CONFIGS):
        q_ref, kc_ref, kp_ref, vc_ref, vp_ref = ins[5 * g:5 * g + 5]
        kbuf, vbuf = kbufs[g], vbufs[g]
        halo = blk * d
        kbuf[0:halo, :] = kp_ref[...]
        kbuf[halo:, :] = kc_ref[...]
        vbuf[0:halo, :] = vp_ref[...]
        vbuf[halo:, :] = vc_ref[...]

        def unit(u, carry, g=g, d=d, halo=halo, q_ref=q_ref, kbuf=kbuf, vbuf=vbuf):
            base = (u // d) * halo + u % d
            q = q_ref[pl.ds(base, blk, stride=d), :] * (SWA_DH ** -0.5 * LOG2E)
            k = kbuf[pl.ds(base, 2 * blk, stride=d), :]
            v = vbuf[pl.ds(base, 2 * blk, stride=d), :]
            lhs = jnp.concatenate([jnp.where(first_head, q, 0.0), jnp.where(first_head, 0.0, q)], axis=0)
            no_prev = jnp.where((j == 0) & (u < d), 1, 0)
            s = _bdot_nt(lhs, k) + bias_ref[g, 0, no_prev]
            m = jnp.max(s, axis=-1, keepdims=True)
            p = jnp.exp2(s - m)
            l = jnp.sum(p, axis=-1, keepdims=True)
            o2 = _bdot(p, v) * (1.0 / l)
            lse2 = m + jnp.log2(l)
            o_scr[g, pl.ds(base, blk, stride=d), :] = jnp.where(first_head, o2[:blk], o2[blk:])
            lse_scr[g, pl.ds(base, blk, stride=d), :] = jnp.where(first_head, lse2[:blk], lse2[blk:])
            return carry

        lax.fori_loop(0, SWA_SPAN // blk, unit, 0, unroll=SWA_UNROLL)

    def combine(c, carry):
        rows = pl.ds(pl.multiple_of(c * 2 * blk, 2 * blk), 2 * blk)
        lse = [lse_scr[g, rows, :] for g in range(ng)]
        m = functools.reduce(jnp.maximum, lse)
        e = [jnp.exp2(x - m) for x in lse]
        den = functools.reduce(lambda x, y: x + y, e)
        o_ref[rows, :] = functools.reduce(
            lambda x, y: x + y, [(e[g] / den) * o_scr[g, rows, :] for g in range(ng)]).astype(o_ref.dtype)
        return carry

    lax.fori_loop(0, SWA_SPAN // (2 * blk), combine, 0)


def _swa(proj, bias, bsz, seq):
    t = proj.shape[0]
    ng = len(SWA_CONFIGS)
    nspan = seq // SWA_SPAN
    npair = SWA_HEADS * SWA_DH // LANES
    group_cols = SWA_HEADS * SWA_DH // LANES
    in_specs, scratch_k = [], []
    for g, (_, d) in enumerate(SWA_CONFIGS):
        halo = SWA_BLOCK * d
        per_span = SWA_SPAN // halo

        def cur(which, g=g):
            col = (which * ng + g) * group_cols
            return lambda b, j, p: (b * nspan + j, col + p)

        def prev(which, g=g, per_span=per_span):
            col = (which * ng + g) * group_cols
            return lambda b, j, p: (jnp.maximum((b * nspan + j) * per_span - 1, 0), col + p)

        in_specs += [pl.BlockSpec((SWA_SPAN, LANES), cur(0)),
                     pl.BlockSpec((SWA_SPAN, LANES), cur(1)), pl.BlockSpec((halo, LANES), prev(1)),
                     pl.BlockSpec((SWA_SPAN, LANES), cur(2)), pl.BlockSpec((halo, LANES), prev(2))]
        scratch_k.append(pltpu.VMEM((halo + SWA_SPAN, LANES), F32))
    in_specs.append(pl.BlockSpec((ng, 1, 2, 2 * SWA_BLOCK, 2 * SWA_BLOCK), lambda b, j, p: (0, p, 0, 0, 0)))
    return pl.pallas_call(
        _swa_body,
        grid=(bsz, nspan, npair),
        in_specs=in_specs,
        out_specs=pl.BlockSpec((SWA_SPAN, LANES), lambda b, j, p: (b * nspan + j, p)),
        out_shape=jax.ShapeDtypeStruct((t, npair * LANES), BF16),
        scratch_shapes=scratch_k + scratch_k + [pltpu.VMEM((ng, SWA_SPAN, LANES), F32)] * 2,
        compiler_params=_cparams(("parallel", "parallel", "parallel")),
        name="swa",
    )(*([proj] * (5 * ng)), bias)


def _layer_norm(y, g, b):
    mu = jnp.mean(y, axis=-1, keepdims=True)
    yc = y - mu
    var = jnp.mean(yc * yc, axis=-1, keepdims=True)
    return yc * lax.rsqrt(var + LN_EPS) * g + b


def _diff_bias_tables(blk):
    buckets = _rel_bucket_np(np.arange(2 * REL_MAX_DIST))
    far = int(np.max(np.nonzero(buckets != NUM_BUCKETS - 1)[0])) + 1
    nb = -(-(far + blk - 1) // blk)
    dist = np.stack([t * blk - blk + np.arange(2 * blk) for t in range(nb + 1)])[:, None, :]
    dist = np.broadcast_to(dist, (nb + 1, SUBLANES, 2 * blk))
    bucket = _rel_bucket_np(np.maximum(dist, 0))
    neg = np.where(dist >= 0, 0.0, NEG_BIG).astype(np.float32)
    return bucket, neg


def _diff_body(blk, nb, lam_init, q_ref, qn_ref, k_ref, v_ref, bias_ref, lam_ref, nw_ref, o_ref,
               vt, acc1, acc2, s_a, s_b):
    qi = pl.program_id(2)
    dv = 2 * DIFF_DH
    seq = k_ref.shape[0]

    @pl.when(qi == 0)
    def _():
        vt[dv:, :] = jnp.ones((vt.shape[0] - dv, seq), BF16)

        def fill(c, carry):
            st = pl.multiple_of(c * blk, blk)
            vt[0:dv, pl.ds(st, blk)] = v_ref[pl.ds(st, blk), :].astype(F32).T.astype(BF16)
            return carry

        lax.fori_loop(0, seq // blk, fill, 0)

    feature = lax.broadcasted_iota(jnp.int32, (dv, blk), 0)

    def components(ref):
        q_t = (ref[...].astype(F32) * (DIFF_DH ** -0.5 * LOG2E)).T
        return (jnp.where(feature < DIFF_DH, q_t, 0.0).astype(BF16),
                jnp.where(feature >= DIFF_DH, q_t, 0.0).astype(BF16))

    qs = components(q_ref)
    qs_next = components(qn_ref)
    accs = (acc1, acc2)
    for acc in accs:
        acc[...] = jnp.zeros_like(acc)

    last = pl.num_programs(2) - 1

    def key_rows(kj):
        return pl.ds(pl.multiple_of(jnp.minimum(kj, last) * blk, blk), blk)

    def scores(kj, dst, queries=qs):
        k = k_ref[key_rows(kj), :]
        for c, qc in enumerate(queries):
            dst[c] = jnp.dot(k, qc, preferred_element_type=F32)

    def consume(biased, kj, src, ms):
        vtb = vt[:, key_rows(kj)]
        out = []
        for c, (m, acc) in enumerate(zip(ms, accs)):
            s = src[c]
            if biased:
                s = s + bias_ref[0, jnp.minimum(qi - kj, nb)]
            m_new = jnp.maximum(m, jnp.max(s, axis=0, keepdims=True))
            p = jnp.exp2(s - m_new).astype(BF16)
            acc[...] = jnp.exp2(m - m_new) * acc[...] + jnp.dot(vtb, p, preferred_element_type=F32)
            out.append(m_new)
        return tuple(out)

    nblocks = qi + 1
    trips = nblocks // 2
    odd = nblocks % 2 == 1
    far_trips = jnp.maximum(qi - nb + 1, 0) // 2

    def pair(biased, t, ms):
        kj = 2 * t
        scores(kj + 1, s_b)
        ms = consume(biased, kj, s_a, ms)
        if biased:
            hand_off = (t == trips - 1) & jnp.logical_not(odd)
            scores(jnp.where(hand_off, 0, kj + 2), s_a,
                   tuple(jnp.where(hand_off, qn, qc) for qc, qn in zip(qs, qs_next)))
        else:
            scores(kj + 2, s_a)
        return consume(biased, kj + 1, s_b, ms)

    @pl.when(qi == 0)
    def _():
        scores(0, s_a)

    m0 = jnp.full((1, blk), NEG_BIG, F32)
    ms = lax.fori_loop(0, far_trips, functools.partial(pair, False), (m0, m0))
    ms = lax.fori_loop(far_trips, trips, functools.partial(pair, True), ms)

    def finish():
        lp = lam_ref[...]
        lam = (jnp.exp(jnp.sum(lp[0:1] * lp[1:2], axis=-1, keepdims=True))
               - jnp.exp(jnp.sum(lp[2:3] * lp[3:4], axis=-1, keepdims=True)) + lam_init)
        a1, a2 = acc1[...], acc2[...]
        o_t = a1[:dv] * (1.0 / a1[dv:dv + 1]) - a2[:dv] * (lam / a2[dv:dv + 1])
        o = o_t.T
        o = o * lax.rsqrt(jnp.mean(o * o, axis=-1, keepdims=True) + RMS_EPS) * nw_ref[...]
        o_ref[...] = (o * (1.0 - lam_init)).astype(o_ref.dtype)

    @pl.when(odd)
    def _():
        consume(True, qi, s_a, ms)
        scores(0, s_a, qs_next)
        finish()

    @pl.when(jnp.logical_not(odd))
    def _():
        finish()


def _diff_attention(proj, bias, lam_params, norm_w, lam_init, bsz, seq, blk):
    t = proj.shape[0]
    nq = seq // blk
    nt = bias.shape[1]
    nb = nt - 1
    width = 2 * DIFF_DH
    ones_rows = 2 * SUBLANES
    return pl.pallas_call(
        functools.partial(_diff_body, blk, nb, lam_init),
        grid=(bsz, DIFF_HEADS, nq),
        in_specs=[pl.BlockSpec((blk, width), lambda b, h, i: (b * nq + i, OD_QC // width + h)),
                  pl.BlockSpec((blk, width),
                               lambda b, h, i: (b * nq + jnp.minimum(i + 1, nq - 1), OD_QC // width + h)),
                  pl.BlockSpec((seq, width), lambda b, h, i: (b, OD_KC // width + h)),
                  pl.BlockSpec((seq, width), lambda b, h, i: (b, OD_VC // width + h)),
                  pl.BlockSpec((1, nt, blk, blk), lambda b, h, i: (h, 0, 0, 0)),
                  pl.BlockSpec((4, DIFF_DH), lambda b, h, i: (0, 0)),
                  pl.BlockSpec((1, width), lambda b, h, i: (0, 0))],
        out_specs=pl.BlockSpec((blk, width), lambda b, h, i: (b * nq + i, h)),
        out_shape=jax.ShapeDtypeStruct((t, DIFF_HEADS * width), BF16),
        scratch_shapes=[pltpu.VMEM((width + ones_rows, seq), BF16),
                        pltpu.VMEM((width + ones_rows, blk), F32), pltpu.VMEM((width + ones_rows, blk), F32),
                        pltpu.VMEM((2, blk, blk), F32), pltpu.VMEM((2, blk, blk), F32)],
        compiler_params=_cparams(("arbitrary", "arbitrary", "arbitrary")),
        name="diff_attn",
    )(proj, proj, proj, proj, bias, lam_params, norm_w)


def _gla_body(q_ref, k_ref, gd_ref, v_ref, r_ref, wg_ref, bg_ref, nw_ref, o_ref, state, part):
    n = pl.program_id(1)

    @pl.when(n == 0)
    def _():
        state[...] = jnp.zeros_like(state)

    rows = GLA_GROUP
    nc = rows // CHUNK
    npair = GLA_HEADS // 2
    groups = [slice(g * rows, (g + 1) * rows) for g in range(GLA_STEP_GROUPS)]
    gate = jnp.dot(gd_ref[...], wg_ref[...], precision=HI, preferred_element_type=F32) + bg_ref[...]
    log_a = _log_sigmoid(gate) * (1.0 / GLA_TAU)

    ri = lax.broadcasted_iota(jnp.int32, (rows, rows), 0)
    ci = lax.broadcasted_iota(jnp.int32, (rows, rows), 1)
    causal = ((ri // CHUNK) == (ci // CHUNK)) & (ri >= ci)
    tri = causal.astype(BF16)
    hi = log_a.astype(BF16)
    rem = log_a - hi.astype(F32)
    mid = rem.astype(BF16)
    lo = (rem - mid.astype(F32)).astype(BF16)
    b_all = [jnp.dot(tri, hi[gr], preferred_element_type=F32) + jnp.dot(tri, mid[gr], preferred_element_type=F32)
             + jnp.dot(tri, lo[gr], preferred_element_type=F32) for gr in groups]

    lane = lax.broadcasted_iota(jnp.int32, (rows, LANES), 1)
    lane_c = lax.broadcasted_iota(jnp.int32, (CHUNK, LANES), 1)
    head_lanes = (lane < GLA_DK, lane >= GLA_DK)
    chunk_lanes = (lane_c < GLA_DK, lane_c >= GLA_DK)
    pairs = []
    for gr, b_g in zip(groups, b_all):
        pairs.append([])
        for p in range(npair):
            cols = slice(p * LANES, (p + 1) * LANES)
            q = q_ref[gr, cols].astype(F32) * GLA_DK ** -0.5
            pairs[-1].append(dict(
                b=b_g[:, cols], q=q, q_dec=q * jnp.exp(b_g[:, cols]), k=k_ref[gr, cols].astype(F32),
                v=[v_ref[gr, (2 * p + hd) * GLA_DV:(2 * p + hd + 1) * GLA_DV].astype(F32) for hd in range(2)]))

    for g, gr in enumerate(groups):
        for c in range(nc):
            sl = slice(c * CHUNK, (c + 1) * CHUNK)
            out_rows = slice(g * rows + c * CHUNK, g * rows + (c + 1) * CHUNK)
            for p, pr in enumerate(pairs[g]):
                bc = pr["b"][sl]
                b_last = bc[CHUNK - 1:CHUNK]
                k_dec = pr["k"][sl] * jnp.exp(b_last - bc)
                e_last = jnp.exp(b_last)
                for hd in range(2):
                    h = 2 * p + hd
                    st = state[h]
                    part[h, out_rows, :] = _bdot_nt(jnp.where(chunk_lanes[hd], pr["q_dec"][sl], 0.0), st)
                    state[h] = st * e_last + _bdot_tn(pr["v"][hd][sl], k_dec)

    def finish(intra):
        for gr, intra_g in zip(groups, intra):
            for h in range(GLA_HEADS):
                o = part[h, gr, :] + intra_g[h]
                o = o * lax.rsqrt(jnp.mean(o * o, axis=-1, keepdims=True) + RMS_EPS) * nw_ref[...]
                gate_r = _silu(r_ref[gr, h * GLA_DV:(h + 1) * GLA_DV].astype(F32))
                o_ref[gr, h * GLA_DV:(h + 1) * GLA_DV] = (o * gate_r).astype(o_ref.dtype)

    def intra_whole_chunk():
        out = []
        for pairs_g in pairs:
            out.append([])
            for pr in pairs_g:
                k_inv = pr["k"] * jnp.exp(jnp.minimum(-pr["b"], GLA_MAX_DECAY))
                for hd in range(2):
                    a = _bdot_nt(jnp.where(head_lanes[hd], pr["q_dec"], 0.0), k_inv)
                    out[-1].append(_bdot(jnp.where(causal, a, 0.0), pr["v"][hd]))
        return out

    def intra_exact():
        out = []
        for pairs_g in pairs:
            out.append([])
            for pr in pairs_g:
                off = _gla_intra_off_diagonal(pr["q"], pr["k"], pr["b"], pr["v"], nc)
                diag = _gla_intra_diagonal(pr["q"], pr["k"], pr["b"], pr["v"])
                out[-1] += [off[hd] + diag[hd] for hd in range(2)]
        return out

    chunk_decay = functools.reduce(jnp.maximum, [
        jnp.max(-b_g.reshape(nc, CHUNK, npair * LANES)[:, CHUNK - 1:CHUNK, :]) for b_g in b_all])

    @pl.when(chunk_decay <= GLA_MAX_DECAY)
    def _():
        finish(intra_whole_chunk())

    @pl.when(chunk_decay > GLA_MAX_DECAY)
    def _():
        finish(intra_exact())


def _gla_intra_off_diagonal(q, k, b, vs, nc):
    per_chunk = CHUNK // GLA_SUB
    lane = lax.broadcasted_iota(jnp.int32, (GLA_SUB, LANES), 1)
    sub_mask = (lane < GLA_DK, lane >= GLA_DK)
    kcol = lax.broadcasted_iota(jnp.int32, (GLA_SUB, CHUNK), 1)
    outs = ([], [])
    for c in range(nc):
        sl = slice(c * CHUNK, (c + 1) * CHUNK)
        bc, qc, kc = b[sl], q[sl], k[sl]
        a_rows = [[jnp.zeros((GLA_SUB, CHUNK), F32)] for _ in vs]
        for blk in range(1, per_chunk):
            r0 = blk * GLA_SUB
            bref = bc[r0:r0 + 1]
            qs = qc[r0:r0 + GLA_SUB] * jnp.exp(bc[r0:r0 + GLA_SUB] - bref)
            ks = kc * jnp.exp(jnp.minimum(bref - bc, 0.0))
            for hd in range(2):
                a = _bdot_nt(jnp.where(sub_mask[hd], qs, 0.0), ks)
                a_rows[hd].append(jnp.where(kcol < r0, a, 0.0))
        for hd in range(2):
            outs[hd].append(_bdot(jnp.concatenate(a_rows[hd], axis=0), vs[hd][sl]))
    return [jnp.concatenate(o, axis=0) for o in outs]


def _gla_intra_diagonal(q, k, b, vs):
    rows = q.shape[0]
    nsub = rows // GLA_SUB
    b3 = b.reshape(nsub, GLA_SUB, LANES)
    q3 = q.reshape(nsub, GLA_SUB, LANES)
    k3 = k.reshape(nsub, GLA_SUB, LANES)
    v3 = [v.reshape(nsub, GLA_SUB, LANES) for v in vs]
    row3 = lax.broadcasted_iota(jnp.int32, (nsub, GLA_SUB, LANES), 1)
    lane3 = lax.broadcasted_iota(jnp.int32, (nsub, GLA_SUB, LANES), 2)
    rowc = lax.broadcasted_iota(jnp.int32, (nsub, GLA_SUB, 1), 1)
    o3 = [jnp.zeros((nsub, GLA_SUB, LANES), F32) for _ in vs]
    for jj in range(GLA_SUB):
        e = jnp.exp(jnp.where(row3 >= jj, b3 - b3[:, jj:jj + 1, :], 0.0))
        t = q3 * k3[:, jj:jj + 1, :] * e
        w_all = jnp.sum(t, axis=-1, keepdims=True)
        w_a = jnp.sum(jnp.where(lane3 < GLA_DK, t, 0.0), axis=-1, keepdims=True)
        for hd, w in enumerate((w_a, w_all - w_a)):
            o3[hd] = o3[hd] + jnp.where(rowc >= jj, w, 0.0) * v3[hd][:, jj:jj + 1, :]
    return [o.reshape(rows, LANES) for o in o3]


def _gla(proj, gate_in, w_gate_pad, b_gate, norm_w, bsz, seq):
    t = proj.shape[0]
    rows = GLA_GROUP * GLA_STEP_GROUPS
    spb = seq // rows
    qk_w = GLA_HEADS * GLA_DK
    v_w = GLA_HEADS * GLA_DV

    def at(col):
        return lambda b, n: (b * spb + n, col)

    def whole(a):
        return pl.BlockSpec(a.shape, lambda b, n: (0, 0))

    return pl.pallas_call(
        _gla_body,
        grid=(bsz, spb),
        in_specs=[pl.BlockSpec((rows, qk_w), at(OD_QD // qk_w)),
                  pl.BlockSpec((rows, qk_w), at(OD_KD // qk_w)),
                  pl.BlockSpec((rows, LANES), at(0)),
                  pl.BlockSpec((rows, v_w), at(OD_VD // v_w)),
                  pl.BlockSpec((rows, v_w), at(OD_RD // v_w)),
                  whole(w_gate_pad), whole(b_gate), whole(norm_w)],
        out_specs=pl.BlockSpec((rows, v_w), at(0)),
        out_shape=jax.ShapeDtypeStruct((t, v_w), BF16),
        scratch_shapes=[pltpu.VMEM((GLA_HEADS, GLA_DV, LANES), F32), pltpu.VMEM((GLA_HEADS, rows, GLA_DV), F32)],
        compiler_params=_cparams(("parallel", "arbitrary")),
        name="gla",
    )(proj, proj, gate_in, proj, proj, w_gate_pad, b_gate, norm_w)


def _tail_body(tiles_per_seq, tc, ma_ref, mb_ref, h_ref, woa_ref, wob_ref,
               g1_ref, b1_ref, wu_ref, cw_ref, cb_ref, wd_ref, g2_ref, b2_ref, y_ref,
               act, scratch_g, scratch_v, carry):
    at_start = pl.program_id(0) % tiles_per_seq == 0
    mix = (jnp.dot(ma_ref[...], woa_ref[...], preferred_element_type=F32)
           + jnp.dot(mb_ref[...], wob_ref[...], preferred_element_type=F32))
    x = _layer_norm(DEEPNORM_ALPHA * h_ref[...] + mix, g1_ref[...], b1_ref[...])
    xb = x.astype(BF16)
    nchunk = D_FF // tc

    def branch(idx, scratch):
        lo = idx * tc
        cur = jnp.dot(xb, wu_ref[:, lo:lo + tc], preferred_element_type=F32)
        return _causal_conv(cur, cw_ref[:, lo:lo + tc], scratch, carry.at[idx], at_start) + cb_ref[:, lo:lo + tc]

    for c in range(nchunk):
        gate = branch(c, scratch_g)
        val = branch(nchunk + c, scratch_v)
        act[:, c * tc:(c + 1) * tc] = (_silu(gate) * val).astype(BF16)
    ffn = jnp.dot(act[...], wd_ref[...], preferred_element_type=F32)
    y_ref[...] = _layer_norm(DEEPNORM_ALPHA * x + ffn, g2_ref[...], b2_ref[...])


def _layer_tail(mix_a, mix_b, h, w_out_a, w_out_b, g1, b1, w_up, conv_w, conv_b, w_down, g2, b2,
                layer, seq):
    t = h.shape[0]
    tm, tc = TAIL_TM, FFN_TC

    def tile(a):
        return pl.BlockSpec((tm, a.shape[1]), lambda i: (i, 0))

    params = (w_out_a, w_out_b, g1, b1, w_up, conv_w, conv_b, w_down, g2, b2)
    param_specs = [_resident(w_out_a), _resident(w_out_b), _resident(g1), _resident(b1),
                   _resident(w_up, layer), _resident(conv_w), _resident(conv_b), _resident(w_down, layer),
                   _resident(g2), _resident(b2)]
    return pl.pallas_call(
        functools.partial(_tail_body, seq // tm, tc),
        grid=(t // tm,),
        in_specs=[tile(mix_a), tile(mix_b), tile(h)] + param_specs,
        out_specs=pl.BlockSpec((tm, D_MODEL), lambda i: (i, 0)),
        out_shape=jax.ShapeDtypeStruct((t, D_MODEL), F32),
        scratch_shapes=[pltpu.VMEM((tm, D_FF), BF16), pltpu.VMEM((HALO + tm, tc), F32),
                        pltpu.VMEM((HALO + tm, tc), F32), pltpu.VMEM((2 * D_FF // tc, HALO, tc), F32)],
        compiler_params=_cparams(("arbitrary",)),
        name="layer_tail",
    )(mix_a, mix_b, h, *params)


def _even_w_in(w):
    w_t = w.T
    a_end = 4 * EVA_PART
    gates = w_t[a_end:a_end + 2 * GDN_HEADS]
    qkv_b = w_t[a_end + 2 * GDN_HEADS:]
    pad = jnp.zeros((EVB_COLS - EVB_BA - 2 * GDN_HEADS, w.shape[0]), w.dtype)
    return w_t[:a_end].astype(BF16), jnp.concatenate([qkv_b, gates, pad], axis=0).astype(BF16)


def _odd_w_in(w):
    w_t = w.T
    pad = jnp.zeros((LANES - GLA_RANK, w.shape[0]), w.dtype)
    return w_t[:OD_COLS].astype(BF16), jnp.concatenate([w_t[OD_COLS:], pad], axis=0).astype(BF16)


def _even_mixer(h, rel_bias, w_in, conv_w, a_log, dt_bias, norm_w, w_out, bsz, seq):
    w_a, w_b = _even_w_in(w_in)
    proj_a, proj_b = _even_in_proj(h, w_a, w_b, conv_w, seq, PROJ_TM)
    gate_pad = jnp.zeros((1, LANES), F32)
    alog_pad = lax.dynamic_update_slice(gate_pad, a_log[None].astype(F32), (0, GDN_HEADS))
    dtb_pad = lax.dynamic_update_slice(gate_pad, dt_bias[None].astype(F32), (0, GDN_HEADS))
    o_a = _gdn(proj_a, proj_b, alog_pad, dtb_pad, norm_w[None], bsz, seq)
    tiles = _bias_tiles(rel_bias, *_swa_bias_tables(), scale=LOG2E)
    ng, two_blk = len(SWA_CONFIGS), 2 * SWA_BLOCK
    bias = tiles.reshape(SWA_HEADS // 2, 2, 2, ng, SWA_BLOCK, two_blk).transpose(3, 0, 2, 1, 4, 5)
    o_b = _swa(proj_b, bias.reshape(ng, SWA_HEADS // 2, 2, two_blk, two_blk), bsz, seq)
    w_out = w_out.astype(BF16)
    return o_a, o_b, w_out[:EVA_PART], w_out[EVA_PART:]


def _odd_mixer(h, rel_bias, w_in, lam_params, diff_norm_w, w_gate, b_gate, gla_norm_w, w_out,
               lam_init, bsz, seq):
    w_main, w_gd = _odd_w_in(w_in)
    proj, gate_in = _odd_in_proj(h, w_main, w_gd, PROJ_TM)
    blk = min(DIFF_BLOCK, seq)
    bucket, neg = _diff_bias_tables(blk)
    bias = _bias_tiles(rel_bias, bucket, neg, scale=LOG2E, base_bucket=NUM_BUCKETS - 1, toeplitz=True)
    o_c = _diff_attention(proj, bias, lam_params, diff_norm_w[None], lam_init, bsz, seq, blk)
    w_gate_pad = jnp.concatenate(
        [w_gate, jnp.zeros((LANES - GLA_RANK, w_gate.shape[1]), w_gate.dtype)], axis=0)
    o_d = _gla(proj, gate_in, w_gate_pad, b_gate[None], gla_norm_w[None], bsz, seq)
    diff_v = DIFF_HEADS * 2 * DIFF_DH
    w_out = w_out.astype(BF16)
    return o_c, o_d, w_out[:diff_v], w_out[diff_v:]


def kernel(x, rel_bias, w_in_even, gdn_conv_w, gdn_a_log, gdn_dt_bias, gdn_norm_w, w_out_even,
           w_in_odd, diff_lambda, diff_norm_w, gla_w_gate, gla_b_gate, gla_norm_w, w_out_odd,
           ffn_w_up, ffn_conv_w, ffn_conv_b, ffn_w_down, ln_g, ln_b):
    bsz, seq, d = x.shape
    h = x.reshape(bsz * seq, d)
    w_up, w_down = ffn_w_up.astype(BF16), ffn_w_down.astype(BF16)
    for layer in range(DEPTH):
        i = layer // 2
        if layer % 2 == 0:
            mixed = _even_mixer(h, rel_bias, w_in_even[i], gdn_conv_w[i], gdn_a_log[i], gdn_dt_bias[i],
                                gdn_norm_w[i], w_out_even[i], bsz, seq)
        else:
            lam_init = 0.8 - 0.6 * math.exp(-0.3 * layer)
            mixed = _odd_mixer(h, rel_bias, w_in_odd[i], diff_lambda[i], diff_norm_w[i], gla_w_gate[i],
                               gla_b_gate[i], gla_norm_w[i], w_out_odd[i], lam_init, bsz, seq)
        h = _layer_tail(*mixed[:2], h, *mixed[2:], ln_g[layer, 0][None], ln_b[layer, 0][None],
                        w_up, ffn_conv_w[layer], ffn_conv_b[layer][None], w_down,
                        ln_g[layer, 1][None], ln_b[layer, 1][None], layer, seq)
    return h.reshape(bsz, seq, d)
```

```python
import functools
import math

import numpy as np
import jax
import jax.numpy as jnp
from jax import lax
from jax.experimental import pallas as pl
from jax.experimental.pallas import tpu as pltpu

F32 = jnp.float32
BF16 = jnp.bfloat16
HI = lax.Precision.HIGHEST

D_MODEL = 1024
DEPTH = 2
DEEPNORM_ALPHA = (2 * DEPTH) ** 0.25
LN_EPS = 1e-5
RMS_EPS = 1e-6
NUM_BUCKETS = 32
REL_MAX_DIST = 2048
GDN_HEADS = 6
GDN_D = 128
CHUNK = 64
GDN_GROUP = 256
GDN_STEP_GROUPS = 2
SWA_CONFIGS = ((128, 1), (512, 4), (2048, 16))
SWA_HEADS = 4
SWA_DH = 64
SWA_BLOCK = 128
SWA_UNROLL = 8
SWA_SPAN = SWA_BLOCK * max(d for _, d in SWA_CONFIGS)
DIFF_HEADS = 4
DIFF_DH = 64
DIFF_BLOCK = 512
LOG2E = math.log2(math.e)
GLA_HEADS = 4
GLA_DK = 64
GLA_DV = 128
GLA_RANK = 16
GLA_TAU = 16.0
GLA_SUB = 16
GLA_GROUP = 256
GLA_STEP_GROUPS = 4
GLA_MAX_DECAY = 60.0
D_FF = 2816

LANES = 128
SUBLANES = 8
HALO = 8
VMEM_LIMIT = 56 * 1024 * 1024
NEG_BIG = -1e30

EVA_PART = GDN_HEADS * GDN_D
EVB_BA = 2304
EVB_COLS = 2560
EVB_TN = 512
OD_QC, OD_KC, OD_VC = 0, 512, 1024
OD_QD, OD_KD, OD_VD, OD_RD = 1536, 1792, 2048, 2560
OD_COLS = 3072
OD_TN = 768
PROJ_TM = 512
TAIL_TM = 512
FFN_TC = 256


def _cparams(sem):
    return pltpu.CompilerParams(dimension_semantics=sem, vmem_limit_bytes=VMEM_LIMIT)


def _bdot(a, b):
    return jnp.dot(a.astype(BF16), b.astype(BF16), preferred_element_type=F32)


def _dot_nt(a, b):
    return lax.dot_general(a, b, (((1,), (1,)), ((), ())), preferred_element_type=F32)


def _bdot_nt(a, b):
    return lax.dot_general(a.astype(BF16), b.astype(BF16), (((1,), (1,)), ((), ())),
                           preferred_element_type=F32)


def _bdot_tn(a, b):
    return lax.dot_general(a.astype(BF16), b.astype(BF16), (((0,), (0,)), ((), ())),
                           preferred_element_type=F32)


def _sigmoid(x):
    return 1.0 / (1.0 + jnp.exp(-x))


def _silu(x):
    return x * _sigmoid(x)


def _softplus(x):
    return jnp.maximum(x, 0.0) + jnp.log1p(jnp.exp(-jnp.abs(x)))


def _log_sigmoid(x):
    return -_softplus(-x)


def _resident(a, lead=None):
    if lead is None:
        return pl.BlockSpec(a.shape, lambda i: (0,) * a.ndim, pipeline_mode=pl.Buffered(1))
    return pl.BlockSpec((None,) + a.shape[1:], lambda i: (lead,) + (0,) * (a.ndim - 1),
                        pipeline_mode=pl.Buffered(1))


def _odd_in_body(x_ref, w_ref, wg_ref, o_ref, og_ref):
    xb = x_ref[...].astype(BF16)
    for c in range(OD_COLS // OD_TN):
        cols = slice(c * OD_TN, (c + 1) * OD_TN)
        o_ref[:, cols] = _dot_nt(xb, w_ref[cols, :]).astype(o_ref.dtype)
    og_ref[...] = _dot_nt(xb, wg_ref[...])


def _odd_in_proj(x, w, w_gate_in, tm):
    t, k = x.shape
    return pl.pallas_call(
        _odd_in_body,
        grid=(t // tm,),
        in_specs=[pl.BlockSpec((tm, k), lambda i: (i, 0)), _resident(w), _resident(w_gate_in)],
        out_specs=[pl.BlockSpec((tm, OD_COLS), lambda i: (i, 0)), pl.BlockSpec((tm, LANES), lambda i: (i, 0))],
        out_shape=[jax.ShapeDtypeStruct((t, OD_COLS), BF16), jax.ShapeDtypeStruct((t, LANES), F32)],
        compiler_params=_cparams(("parallel",)),
        name="odd_in_proj",
    )(x, w, w_gate_in)


def _rel_bucket_np(dist):
    max_exact = NUM_BUCKETS // 2
    d = np.maximum(dist, 1).astype(np.float32)
    large = max_exact + (np.log(d / max_exact) / math.log(REL_MAX_DIST / max_exact)
                         * (NUM_BUCKETS - max_exact)).astype(np.int32)
    large = np.minimum(large, NUM_BUCKETS - 1)
    return np.where(dist < max_exact, dist, large).astype(np.int32)


def _bias_body(scale, base_bucket, tile_buckets, toeplitz, rb_ref, bucket_ref, neg_ref, o_ref):
    h = pl.program_id(0)
    t = pl.program_id(1)
    base = 0.0 if base_bucket is None else rb_ref[base_bucket, h]
    for tile, present in enumerate(tile_buckets):
        @pl.when(t == tile)
        def _(present=present):
            bucket = bucket_ref[0]
            acc = neg_ref[0]
            for b in present:
                acc = acc + jnp.where(bucket == b, (rb_ref[b, h] - base) * scale, 0.0)
            if toeplitz:
                n = o_ref.shape[2]
                rows = jnp.broadcast_to(acc[0:1], (n, 2 * n))
                acc = pltpu.roll(rows, n, 1, stride=1, stride_axis=0)[:, :n]
            o_ref[0, 0] = acc


def _bias_tiles(rel_bias, bucket, neg, scale=1.0, base_bucket=None, toeplitz=False):
    nt, r, c = bucket.shape
    nh = rel_bias.shape[1]
    out_r, out_c = (c // 2, c // 2) if toeplitz else (r, c)
    tile_buckets = tuple(tuple(int(b) for b in np.unique(bucket[t][neg[t] == 0]) if b != base_bucket)
                         for t in range(nt))
    return pl.pallas_call(
        functools.partial(_bias_body, scale, base_bucket, tile_buckets, toeplitz),
        grid=(nh, nt),
        in_specs=[pl.BlockSpec(memory_space=pltpu.SMEM),
                  pl.BlockSpec((1, r, c), lambda h, t: (t, 0, 0)),
                  pl.BlockSpec((1, r, c), lambda h, t: (t, 0, 0))],
        out_specs=pl.BlockSpec((1, 1, out_r, out_c), lambda h, t: (h, t, 0, 0)),
        out_shape=jax.ShapeDtypeStruct((nh, nt, out_r, out_c), F32),
        compiler_params=_cparams(("parallel", "parallel")),
        name="rel_bias_tiles",
    )(rel_bias, jnp.asarray(bucket), jnp.asarray(neg))


def _causal_conv(cur, w, scratch, carry, at_start):
    width = w.shape[0]
    rows = cur.shape[0]
    scratch[0:HALO, :] = jnp.where(at_start, 0.0, carry[...])
    scratch[HALO:, :] = cur
    carry[...] = cur[rows - HALO:]
    y = w[width - 1:width, :] * cur
    for j in range(width - 1):
        back = width - 1 - j
        y = y + w[j:j + 1, :] * scratch[HALO - back:HALO - back + rows, :]
    return y


def _even_in_body(tiles_per_seq, x_ref, wa_ref, wb_ref, cw_ref, oa_ref, ob_ref, scratch, carry):
    at_start = pl.program_id(0) % tiles_per_seq == 0
    x = x_ref[...].astype(BF16)
    for part in range(3):
        cols = slice(part * EVA_PART, (part + 1) * EVA_PART)
        y = _dot_nt(x, wa_ref[cols, :])
        c = _silu(_causal_conv(y, cw_ref[:, cols], scratch, carry.at[part], at_start))
        for hd in range(GDN_HEADS):
            ch = c[:, hd * GDN_D:(hd + 1) * GDN_D]
            if part < 2:
                inv = lax.rsqrt(jnp.sum(ch * ch, axis=-1, keepdims=True) + RMS_EPS)
                ch = ch * (inv * GDN_D ** -0.5 if part == 0 else inv)
            lo = part * EVA_PART + hd * GDN_D
            oa_ref[:, lo:lo + GDN_D] = ch.astype(oa_ref.dtype)
    z_cols = slice(3 * EVA_PART, 4 * EVA_PART)
    oa_ref[:, z_cols] = _dot_nt(x, wa_ref[z_cols, :]).astype(oa_ref.dtype)
    for c in range(EVB_COLS // EVB_TN):
        cols = slice(c * EVB_TN, (c + 1) * EVB_TN)
        ob_ref[:, cols] = _dot_nt(x, wb_ref[cols, :])


def _even_in_proj(x, w_a, w_b, conv_w, seq, tm):
    t, k = x.shape
    return pl.pallas_call(
        functools.partial(_even_in_body, seq // tm),
        grid=(t // tm,),
        in_specs=[pl.BlockSpec((tm, k), lambda i: (i, 0)),
                  _resident(w_a), _resident(w_b), _resident(conv_w)],
        out_specs=[pl.BlockSpec((tm, w_a.shape[0]), lambda i: (i, 0)),
                   pl.BlockSpec((tm, EVB_COLS), lambda i: (i, 0))],
        out_shape=[jax.ShapeDtypeStruct((t, w_a.shape[0]), BF16), jax.ShapeDtypeStruct((t, EVB_COLS), F32)],
        scratch_shapes=[pltpu.VMEM((HALO + tm, EVA_PART), F32), pltpu.VMEM((3, HALO, EVA_PART), F32)],
        compiler_params=_cparams(("arbitrary",)),
        name="even_in_proj",
    )(x, w_a, w_b, conv_w)


def _gdn_body(q_ref, k_ref, v_ref, z_ref, ba_ref, alog_ref, dtb_ref, nw_ref, o_ref, state):
    n = pl.program_id(1)

    @pl.when(n == 0)
    def _():
        state[...] = jnp.zeros_like(state)

    grp = GDN_GROUP
    nc = grp // CHUNK
    nh = GDN_HEADS
    hs = range(nh)
    ri = lax.broadcasted_iota(jnp.int32, (grp, grp), 0)
    ci = lax.broadcasted_iota(jnp.int32, (grp, grp), 1)
    same = (ri // CHUNK) == (ci // CHUNK)
    incl = same & (ri >= ci)
    strict = same & (ri > ci)
    eye = (ri == ci).astype(F32)
    tri = incl.astype(BF16)

    def chunk_local(groups):
        units = [(g, h) for g in range(len(groups)) for h in hs]
        gam_all, gam_rows, beta_all = [], [], []
        for rows in groups:
            ba = ba_ref[rows, :]
            beta_all.append(_sigmoid(ba))
            g_all = -jnp.exp(alog_ref[...]) * _softplus(ba + dtb_ref[...])
            g_hi = g_all.astype(BF16)
            rem = g_all - g_hi.astype(F32)
            g_mid = rem.astype(BF16)
            g_lo = (rem - g_mid.astype(F32)).astype(BF16)
            gam_all.append(jnp.dot(tri, g_hi, preferred_element_type=F32)
                           + jnp.dot(tri, g_mid, preferred_element_type=F32)
                           + jnp.dot(tri, g_lo, preferred_element_type=F32))
            gam_rows.append(gam_all[-1].T)

        def head(ref, g, h):
            return ref[groups[g], h * GDN_D:(h + 1) * GDN_D]

        q = [head(q_ref, g, h) for g, h in units]
        k = [head(k_ref, g, h) for g, h in units]
        v = [head(v_ref, g, h) for g, h in units]
        gam = [jnp.broadcast_to(gam_all[g][:, nh + h:nh + h + 1], (grp, GDN_D)) for g, h in units]
        bcol = [beta_all[g][:, h:h + 1] for g, h in units]
        us = range(len(units))
        decay, x, inv = [], [], []
        for i, (g, h) in enumerate(units):
            diff = gam[i][:, 0:1] - gam_rows[g][nh + h:nh + h + 1, :]
            decay.append(jnp.where(incl, jnp.exp(jnp.where(incl, diff, 0.0)), 0.0))
            kk = _bdot_nt(k[i], k[i])
            x.append(-jnp.where(strict, bcol[i] * kk * decay[i], 0.0))
            inv.append(eye + x[i])
        for _ in range(5):
            for i in us:
                x[i] = _bdot(x[i], x[i])
                inv[i] = inv[i] + _bdot(inv[i], x[i])
        out = [tuple([] for _ in range(6)) for _ in groups]
        for i, (g, h) in enumerate(units):
            eg = jnp.exp(gam[i])
            uw = _bdot(inv[i], jnp.concatenate([v[i] * bcol[i], k[i] * (bcol[i] * eg)], axis=1))
            kd, gl = [], []
            for c in range(nc):
                last = gam[i][(c + 1) * CHUNK - 1:(c + 1) * CHUNK, :]
                kd.append(k[i][c * CHUNK:(c + 1) * CHUNK] * jnp.exp(last - gam[i][c * CHUNK:(c + 1) * CHUNK]))
                gl.append(jnp.exp(last))
            for lst, val in zip(out[g], (uw[:, :GDN_D], uw[:, GDN_D:], _bdot_nt(q[i], k[i]) * decay[i],
                                         q[i] * eg, kd, gl)):
                lst.append(val)
        return out

    def recurrence(rows, local, s):
        u, w, qk, q_dec, k_dec, g_last = local
        q_s = [[] for _ in hs]
        delta = [[] for _ in hs]
        for c in range(nc):
            sl = slice(c * CHUNK, (c + 1) * CHUNK)
            for h in hs:
                r = _bdot(jnp.concatenate([w[h][sl], q_dec[h][sl]], axis=0), s[h])
                d = u[h][sl] - r[:CHUNK]
                q_s[h].append(r[CHUNK:])
                delta[h].append(d)
                s[h] = g_last[h][c] * s[h] + _bdot_tn(k_dec[h][c], d)
        for h in hs:
            o = jnp.concatenate(q_s[h], axis=0) + _bdot(qk[h], jnp.concatenate(delta[h], axis=0))
            o = o * lax.rsqrt(jnp.mean(o * o, axis=-1, keepdims=True) + RMS_EPS) * nw_ref[...]
            z = z_ref[rows, h * GDN_D:(h + 1) * GDN_D].astype(F32)
            o_ref[rows, h * GDN_D:(h + 1) * GDN_D] = (o * _silu(z)).astype(o_ref.dtype)
        return s

    groups = [slice(g * grp, (g + 1) * grp) for g in range(GDN_STEP_GROUPS)]
    s = [state[h] for h in hs]
    for rows, local in zip(groups, chunk_local(groups)):
        s = recurrence(rows, local, s)
    for h in hs:
        state[h] = s[h]


def _gdn(proj_a, proj_b, alog_pad, dtb_pad, norm_w, bsz, seq):
    t = proj_a.shape[0]
    rows = GDN_GROUP * GDN_STEP_GROUPS
    spb = seq // rows
    width = GDN_HEADS * GDN_D

    def at(col):
        return lambda b, n: (b * spb + n, col)

    return pl.pallas_call(
        _gdn_body,
        grid=(bsz, spb),
        in_specs=[pl.BlockSpec((rows, width), at(0)),
                  pl.BlockSpec((rows, width), at(1)),
                  pl.BlockSpec((rows, width), at(2)),
                  pl.BlockSpec((rows, width), at(3)),
                  pl.BlockSpec((rows, LANES), at(EVB_BA // LANES)),
                  pl.BlockSpec((1, LANES), lambda b, n: (0, 0)),
                  pl.BlockSpec((1, LANES), lambda b, n: (0, 0)),
                  pl.BlockSpec((1, LANES), lambda b, n: (0, 0))],
        out_specs=pl.BlockSpec((rows, width), at(0)),
        out_shape=jax.ShapeDtypeStruct((t, width), BF16),
        scratch_shapes=[pltpu.VMEM((GDN_HEADS, GDN_D, GDN_D), F32)],
        compiler_params=_cparams(("parallel", "arbitrary")),
        name="gdn",
    )(proj_a, proj_a, proj_a, proj_a, proj_b, alog_pad, dtb_pad, norm_w)


def _swa_bias_tables():
    qi = np.arange(SWA_BLOCK)[:, None] + SWA_BLOCK
    kj = np.arange(2 * SWA_BLOCK)[None, :]
    rel = qi - kj
    buckets, negs = [], []
    for has_prev in (True, False):
        for window, dilation in SWA_CONFIGS:
            valid = (rel >= 0) & (rel <= window // dilation) & (has_prev | (kj >= SWA_BLOCK))
            buckets.append(_rel_bucket_np(np.maximum(rel, 0) * dilation))
            negs.append(np.where(valid, 0.0, NEG_BIG))
    return np.stack(buckets).astype(np.int32), np.stack(negs).astype(np.float32)


def _swa_body(*refs):
    ng = len(SWA_CONFIGS)
    ins, bias_ref, o_ref = refs[:5 * ng], refs[5 * ng], refs[5 * ng + 1]
    scratch = refs[5 * ng + 2:]
    kbufs, vbufs, o_scr, lse_scr = scratch[:ng], scratch[ng:2 * ng], scratch[2 * ng], scratch[2 * ng + 1]
    j = pl.program_id(1)
    blk = SWA_BLOCK
    first_head = lax.broadcasted_iota(jnp.int32, (blk, LANES), 1) < SWA_DH

    for g, (_, d) in enumerate(SWA_CONFIGS):
        q_ref, kc_ref, kp_ref, vc_ref, vp_ref = ins[5 * g:5 * g + 5]
        kbuf, vbuf = kbufs[g], vbufs[g]
        halo = blk * d
        kbuf[0:halo, :] = kp_ref[...]
        kbuf[halo:, :] = kc_ref[...]
        vbuf[0:halo, :] = vp_ref[...]
        vbuf[halo:, :] = vc_ref[...]

        def unit(u, carry, g=g, d=d, halo=halo, q_ref=q_ref, kbuf=kbuf, vbuf=vbuf):
            base = (u // d) * halo + u % d
            q = q_ref[pl.ds(base, blk, stride=d), :] * (SWA_DH ** -0.5 * LOG2E)
            k = kbuf[pl.ds(base, 2 * blk, stride=d), :]
            v = vbuf[pl.ds(base, 2 * blk, stride=d), :]
            lhs = jnp.concatenate([jnp.where(first_head, q, 0.0), jnp.where(first_head, 0.0, q)], axis=0)
            no_prev = jnp.where((j == 0) & (u < d), 1, 0)
            s = _bdot_nt(lhs, k) + bias_ref[g, 0, no_prev]
            m = jnp.max(s, axis=-1, keepdims=True)
            p = jnp.exp2(s - m)
            l = jnp.sum(p, axis=-1, keepdims=True)
            o2 = _bdot(p, v) * (1.0 / l)
            lse2 = m + jnp.log2(l)
            o_scr[g, pl.ds(base, blk, stride=d), :] = jnp.where(first_head, o2[:blk], o2[blk:])
            lse_scr[g, pl.ds(base, blk, stride=d), :] = jnp.where(first_head, lse2[:blk], lse2[blk:])
            return carry

        lax.fori_loop(0, SWA_SPAN // blk, unit, 0, unroll=SWA_UNROLL)

    def combine(c, carry):
        rows = pl.ds(pl.multiple_of(c * 2 * blk, 2 * blk), 2 * blk)
        lse = [lse_scr[g, rows, :] for g in range(ng)]
        m = functools.reduce(jnp.maximum, lse)
        e = [jnp.exp2(x - m) for x in lse]
        den = functools.reduce(lambda x, y: x + y, e)
        o_ref[rows, :] = functools.reduce(
            lambda x, y: x + y, [(e[g] / den) * o_scr[g, rows, :] for g in range(ng)]).astype(o_ref.dtype)
        return carry

    lax.fori_loop(0, SWA_SPAN // (2 * blk), combine, 0)


def _swa(proj, bias, bsz, seq):
    t = proj.shape[0]
    ng = len(SWA_CONFIGS)
    nspan = seq // SWA_SPAN
    npair = SWA_HEADS * SWA_DH // LANES
    group_cols = SWA_HEADS * SWA_DH // LANES
    in_specs, scratch_k = [], []
    for g, (_, d) in enumerate(SWA_CONFIGS):
        halo = SWA_BLOCK * d
        per_span = SWA_SPAN // halo

        def cur(which, g=g):
            col = (which * ng + g) * group_cols
            return lambda b, j, p: (b * nspan + j, col + p)

        def prev(which, g=g, per_span=per_span):
            col = (which * ng + g) * group_cols
            return lambda b, j, p: (jnp.maximum((b * nspan + j) * per_span - 1, 0), col + p)

        in_specs += [pl.BlockSpec((SWA_SPAN, LANES), cur(0)),
                     pl.BlockSpec((SWA_SPAN, LANES), cur(1)), pl.BlockSpec((halo, LANES), prev(1)),
                     pl.BlockSpec((SWA_SPAN, LANES), cur(2)), pl.BlockSpec((halo, LANES), prev(2))]
        scratch_k.append(pltpu.VMEM((halo + SWA_SPAN, LANES), F32))
    in_specs.append(pl.BlockSpec((ng, 1, 2, 2 * SWA_BLOCK, 2 * SWA_BLOCK), lambda b, j, p: (0, p, 0, 0, 0)))
    return pl.pallas_call(
        _swa_body,
        grid=(bsz, nspan, npair),
        in_specs=in_specs,
        out_specs=pl.BlockSpec((SWA_SPAN, LANES), lambda b, j, p: (b * nspan + j, p)),
        out_shape=jax.ShapeDtypeStruct((t, npair * LANES), BF16),
        scratch_shapes=scratch_k + scratch_k + [pltpu.VMEM((ng, SWA_SPAN, LANES), F32)] * 2,
        compiler_params=_cparams(("parallel", "parallel", "parallel")),
        name="swa",
    )(*([proj] * (5 * ng)), bias)


def _layer_norm(y, g, b):
    mu = jnp.mean(y, axis=-1, keepdims=True)
    yc = y - mu
    var = jnp.mean(yc * yc, axis=-1, keepdims=True)
    return yc * lax.rsqrt(var + LN_EPS) * g + b


def _diff_bias_tables(blk):
    buckets = _rel_bucket_np(np.arange(2 * REL_MAX_DIST))
    far = int(np.max(np.nonzero(buckets != NUM_BUCKETS - 1)[0])) + 1
    nb = -(-(far + blk - 1) // blk)
    dist = np.stack([t * blk - blk + np.arange(2 * blk) for t in range(nb + 1)])[:, None, :]
    dist = np.broadcast_to(dist, (nb + 1, SUBLANES, 2 * blk))
    bucket = _rel_bucket_np(np.maximum(dist, 0))
    neg = np.where(dist >= 0, 0.0, NEG_BIG).astype(np.float32)
    return bucket, neg


def _diff_body(blk, nb, lam_init, q_ref, qn_ref, k_ref, v_ref, bias_ref, lam_ref, nw_ref, o_ref,
               vt, acc1, acc2, s_a, s_b):
    qi = pl.program_id(2)
    dv = 2 * DIFF_DH
    seq = k_ref.shape[0]

    @pl.when(qi == 0)
    def _():
        vt[dv:, :] = jnp.ones((vt.shape[0] - dv, seq), BF16)

        def fill(c, carry):
            st = pl.multiple_of(c * blk, blk)
            vt[0:dv, pl.ds(st, blk)] = v_ref[pl.ds(st, blk), :].astype(F32).T.astype(BF16)
            return carry

        lax.fori_loop(0, seq // blk, fill, 0)

    feature = lax.broadcasted_iota(jnp.int32, (dv, blk), 0)

    def components(ref):
        q_t = (ref[...].astype(F32) * (DIFF_DH ** -0.5 * LOG2E)).T
        return (jnp.where(feature < DIFF_DH, q_t, 0.0).astype(BF16),
                jnp.where(feature >= DIFF_DH, q_t, 0.0).astype(BF16))

    qs = components(q_ref)
    qs_next = components(qn_ref)
    accs = (acc1, acc2)
    for acc in accs:
        acc[...] = jnp.zeros_like(acc)

    last = pl.num_programs(2) - 1

    def key_rows(kj):
        return pl.ds(pl.multiple_of(jnp.minimum(kj, last) * blk, blk), blk)

    def scores(kj, dst, queries=qs):
        k = k_ref[key_rows(kj), :]
        for c, qc in enumerate(queries):
            dst[c] = jnp.dot(k, qc, preferred_element_type=F32)

    def consume(biased, kj, src, ms):
        vtb = vt[:, key_rows(kj)]
        out = []
        for c, (m, acc) in enumerate(zip(ms, accs)):
            s = src[c]
            if biased:
                s = s + bias_ref[0, jnp.minimum(qi - kj, nb)]
            m_new = jnp.maximum(m, jnp.max(s, axis=0, keepdims=True))
            p = jnp.exp2(s - m_new).astype(BF16)
            acc[...] = jnp.exp2(m - m_new) * acc[...] + jnp.dot(vtb, p, preferred_element_type=F32)
            out.append(m_new)
        return tuple(out)

    nblocks = qi + 1
    trips = nblocks // 2
    odd = nblocks % 2 == 1
    far_trips = jnp.maximum(qi - nb + 1, 0) // 2

    def pair(biased, t, ms):
        kj = 2 * t
        scores(kj + 1, s_b)
        ms = consume(biased, kj, s_a, ms)
        if biased:
            hand_off = (t == trips - 1) & jnp.logical_not(odd)
            scores(jnp.where(hand_off, 0, kj + 2), s_a,
                   tuple(jnp.where(hand_off, qn, qc) for qc, qn in zip(qs, qs_next)))
        else:
            scores(kj + 2, s_a)
        return consume(biased, kj + 1, s_b, ms)

    @pl.when(qi == 0)
    def _():
        scores(0, s_a)

    m0 = jnp.full((1, blk), NEG_BIG, F32)
    ms = lax.fori_loop(0, far_trips, functools.partial(pair, False), (m0, m0))
    ms = lax.fori_loop(far_trips, trips, functools.partial(pair, True), ms)

    def finish():
        lp = lam_ref[...]
        lam = (jnp.exp(jnp.sum(lp[0:1] * lp[1:2], axis=-1, keepdims=True))
               - jnp.exp(jnp.sum(lp[2:3] * lp[3:4], axis=-1, keepdims=True)) + lam_init)
        a1, a2 = acc1[...], acc2[...]
        o_t = a1[:dv] * (1.0 / a1[dv:dv + 1]) - a2[:dv] * (lam / a2[dv:dv + 1])
        o = o_t.T
        o = o * lax.rsqrt(jnp.mean(o * o, axis=-1, keepdims=True) + RMS_EPS) * nw_ref[...]
        o_ref[...] = (o * (1.0 - lam_init)).astype(o_ref.dtype)

    @pl.when(odd)
    def _():
        consume(True, qi, s_a, ms)
        scores(0, s_a, qs_next)
        finish()

    @pl.when(jnp.logical_not(odd))
    def _():
        finish()


def _diff_attention(proj, bias, lam_params, norm_w, lam_init, bsz, seq, blk):
    t = proj.shape[0]
    nq = seq // blk
    nt = bias.shape[1]
    nb = nt - 1
    width = 2 * DIFF_DH
    ones_rows = 2 * SUBLANES
    return pl.pallas_call(
        functools.partial(_diff_body, blk, nb, lam_init),
        grid=(bsz, DIFF_HEADS, nq),
        in_specs=[pl.BlockSpec((blk, width), lambda b, h, i: (b * nq + i, OD_QC // width + h)),
                  pl.BlockSpec((blk, width),
                               lambda b, h, i: (b * nq + jnp.minimum(i + 1, nq - 1), OD_QC // width + h)),
                  pl.BlockSpec((seq, width), lambda b, h, i: (b, OD_KC // width + h)),
                  pl.BlockSpec((seq, width), lambda b, h, i: (b, OD_VC // width + h)),
                  pl.BlockSpec((1, nt, blk, blk), lambda b, h, i: (h, 0, 0, 0)),
                  pl.BlockSpec((4, DIFF_DH), lambda b, h, i: (0, 0)),
                  pl.BlockSpec((1, width), lambda b, h, i: (0, 0))],
        out_specs=pl.BlockSpec((blk, width), lambda b, h, i: (b * nq + i, h)),
        out_shape=jax.ShapeDtypeStruct((t, DIFF_HEADS * width), BF16),
        scratch_shapes=[pltpu.VMEM((width + ones_rows, seq), BF16),
                        pltpu.VMEM((width + ones_rows, blk), F32), pltpu.VMEM((width + ones_rows, blk), F32),
                        pltpu.VMEM((2, blk, blk), F32), pltpu.VMEM((2, blk, blk), F32)],
        compiler_params=_cparams(("arbitrary", "arbitrary", "arbitrary")),
        name="diff_attn",
    )(proj, proj, proj, proj, bias, lam_params, norm_w)


def _gla_body(q_ref, k_ref, gd_ref, v_ref, r_ref, wg_ref, bg_ref, nw_ref, o_ref, state, part):
    n = pl.program_id(1)

    @pl.when(n == 0)
    def _():
        state[...] = jnp.zeros_like(state)

    rows = GLA_GROUP
    nc = rows // CHUNK
    npair = GLA_HEADS // 2
    groups = [slice(g * rows, (g + 1) * rows) for g in range(GLA_STEP_GROUPS)]
    gate = jnp.dot(gd_ref[...], wg_ref[...], precision=HI, preferred_element_type=F32) + bg_ref[...]
    log_a = _log_sigmoid(gate) * (1.0 / GLA_TAU)

    ri = lax.broadcasted_iota(jnp.int32, (rows, rows), 0)
    ci = lax.broadcasted_iota(jnp.int32, (rows, rows), 1)
    causal = ((ri // CHUNK) == (ci // CHUNK)) & (ri >= ci)
    tri = causal.astype(BF16)
    hi = log_a.astype(BF16)
    rem = log_a - hi.astype(F32)
    mid = rem.astype(BF16)
    lo = (rem - mid.astype(F32)).astype(BF16)
    b_all = [jnp.dot(tri, hi[gr], preferred_element_type=F32) + jnp.dot(tri, mid[gr], preferred_element_type=F32)
             + jnp.dot(tri, lo[gr], preferred_element_type=F32) for gr in groups]

    lane = lax.broadcasted_iota(jnp.int32, (rows, LANES), 1)
    lane_c = lax.broadcasted_iota(jnp.int32, (CHUNK, LANES), 1)
    head_lanes = (lane < GLA_DK, lane >= GLA_DK)
    chunk_lanes = (lane_c < GLA_DK, lane_c >= GLA_DK)
    pairs = []
    for gr, b_g in zip(groups, b_all):
        pairs.append([])
        for p in range(npair):
            cols = slice(p * LANES, (p + 1) * LANES)
            q = q_ref[gr, cols].astype(F32) * GLA_DK ** -0.5
            pairs[-1].append(dict(
                b=b_g[:, cols], q=q, q_dec=q * jnp.exp(b_g[:, cols]), k=k_ref[gr, cols].astype(F32),
                v=[v_ref[gr, (2 * p + hd) * GLA_DV:(2 * p + hd + 1) * GLA_DV].astype(F32) for hd in range(2)]))

    for g, gr in enumerate(groups):
        for c in range(nc):
            sl = slice(c * CHUNK, (c + 1) * CHUNK)
            out_rows = slice(g * rows + c * CHUNK, g * rows + (c + 1) * CHUNK)
            for p, pr in enumerate(pairs[g]):
                bc = pr["b"][sl]
                b_last = bc[CHUNK - 1:CHUNK]
                k_dec = pr["k"][sl] * jnp.exp(b_last - bc)
                e_last = jnp.exp(b_last)
                for hd in range(2):
                    h = 2 * p + hd
                    st = state[h]
                    part[h, out_rows, :] = _bdot_nt(jnp.where(chunk_lanes[hd], pr["q_dec"][sl], 0.0), st)
                    state[h] = st * e_last + _bdot_tn(pr["v"][hd][sl], k_dec)

    def finish(intra):
        for gr, intra_g in zip(groups, intra):
            for h in range(GLA_HEADS):
                o = part[h, gr, :] + intra_g[h]
                o = o * lax.rsqrt(jnp.mean(o * o, axis=-1, keepdims=True) + RMS_EPS) * nw_ref[...]
                gate_r = _silu(r_ref[gr, h * GLA_DV:(h + 1) * GLA_DV].astype(F32))
                o_ref[gr, h * GLA_DV:(h + 1) * GLA_DV] = (o * gate_r).astype(o_ref.dtype)

    def intra_whole_chunk():
        out = []
        for pairs_g in pairs:
            out.append([])
            for pr in pairs_g:
                k_inv = pr["k"] * jnp.exp(jnp.minimum(-pr["b"], GLA_MAX_DECAY))
                for hd in range(2):
                    a = _bdot_nt(jnp.where(head_lanes[hd], pr["q_dec"], 0.0), k_inv)
                    out[-1].append(_bdot(jnp.where(causal, a, 0.0), pr["v"][hd]))
        return out

    def intra_exact():
        out = []
        for pairs_g in pairs:
            out.append([])
            for pr in pairs_g:
                off = _gla_intra_off_diagonal(pr["q"], pr["k"], pr["b"], pr["v"], nc)
                diag = _gla_intra_diagonal(pr["q"], pr["k"], pr["b"], pr["v"])
                out[-1] += [off[hd] + diag[hd] for hd in range(2)]
        return out

    chunk_decay = functools.reduce(jnp.maximum, [
        jnp.max(-b_g.reshape(nc, CHUNK, npair * LANES)[:, CHUNK - 1:CHUNK, :]) for b_g in b_all])

    @pl.when(chunk_decay <= GLA_MAX_DECAY)
    def _():
        finish(intra_whole_chunk())

    @pl.when(chunk_decay > GLA_MAX_DECAY)
    def _():
        finish(intra_exact())


def _gla_intra_off_diagonal(q, k, b, vs, nc):
    per_chunk = CHUNK // GLA_SUB
    lane = lax.broadcasted_iota(jnp.int32, (GLA_SUB, LANES), 1)
    sub_mask = (lane < GLA_DK, lane >= GLA_DK)
    kcol = lax.broadcasted_iota(jnp.int32, (GLA_SUB, CHUNK), 1)
    outs = ([], [])
    for c in range(nc):
        sl = slice(c * CHUNK, (c + 1) * CHUNK)
        bc, qc, kc = b[sl], q[sl], k[sl]
        a_rows = [[jnp.zeros((GLA_SUB, CHUNK), F32)] for _ in vs]
        for blk in range(1, per_chunk):
            r0 = blk * GLA_SUB
            bref = bc[r0:r0 + 1]
            qs = qc[r0:r0 + GLA_SUB] * jnp.exp(bc[r0:r0 + GLA_SUB] - bref)
            ks = kc * jnp.exp(jnp.minimum(bref - bc, 0.0))
            for hd in range(2):
                a = _bdot_nt(jnp.where(sub_mask[hd], qs, 0.0), ks)
                a_rows[hd].append(jnp.where(kcol < r0, a, 0.0))
        for hd in range(2):
            outs[hd].append(_bdot(jnp.concatenate(a_rows[hd], axis=0), vs[hd][sl]))
    return [jnp.concatenate(o, axis=0) for o in outs]


def _gla_intra_diagonal(q, k, b, vs):
    rows = q.shape[0]
    nsub = rows // GLA_SUB
    b3 = b.reshape(nsub, GLA_SUB, LANES)
    q3 = q.reshape(nsub, GLA_SUB, LANES)
    k3 = k.reshape(nsub, GLA_SUB, LANES)
    v3 = [v.reshape(nsub, GLA_SUB, LANES) for v in vs]
    row3 = lax.broadcasted_iota(jnp.int32, (nsub, GLA_SUB, LANES), 1)
    lane3 = lax.broadcasted_iota(jnp.int32, (nsub, GLA_SUB, LANES), 2)
    rowc = lax.broadcasted_iota(jnp.int32, (nsub, GLA_SUB, 1), 1)
    o3 = [jnp.zeros((nsub, GLA_SUB, LANES), F32) for _ in vs]
    for jj in range(GLA_SUB):
        e = jnp.exp(jnp.where(row3 >= jj, b3 - b3[:, jj:jj + 1, :], 0.0))
        t = q3 * k3[:, jj:jj + 1, :] * e
        w_all = jnp.sum(t, axis=-1, keepdims=True)
        w_a = jnp.sum(jnp.where(lane3 < GLA_DK, t, 0.0), axis=-1, keepdims=True)
        for hd, w in enumerate((w_a, w_all - w_a)):
            o3[hd] = o3[hd] + jnp.where(rowc >= jj, w, 0.0) * v3[hd][:, jj:jj + 1, :]
    return [o.reshape(rows, LANES) for o in o3]


def _gla(proj, gate_in, w_gate_pad, b_gate, norm_w, bsz, seq):
    t = proj.shape[0]
    rows = GLA_GROUP * GLA_STEP_GROUPS
    spb = seq // rows
    qk_w = GLA_HEADS * GLA_DK
    v_w = GLA_HEADS * GLA_DV

    def at(col):
        return lambda b, n: (b * spb + n, col)

    def whole(a):
        return pl.BlockSpec(a.shape, lambda b, n: (0, 0))

    return pl.pallas_call(
        _gla_body,
        grid=(bsz, spb),
        in_specs=[pl.BlockSpec((rows, qk_w), at(OD_QD // qk_w)),
                  pl.BlockSpec((rows, qk_w), at(OD_KD // qk_w)),
                  pl.BlockSpec((rows, LANES), at(0)),
                  pl.BlockSpec((rows, v_w), at(OD_VD // v_w)),
                  pl.BlockSpec((rows, v_w), at(OD_RD // v_w)),
                  whole(w_gate_pad), whole(b_gate), whole(norm_w)],
        out_specs=pl.BlockSpec((rows, v_w), at(0)),
        out_shape=jax.ShapeDtypeStruct((t, v_w), BF16),
        scratch_shapes=[pltpu.VMEM((GLA_HEADS, GLA_DV, LANES), F32), pltpu.VMEM((GLA_HEADS, rows, GLA_DV), F32)],
        compiler_params=_cparams(("parallel", "arbitrary")),
        name="gla",
    )(proj, proj, gate_in, proj, proj, w_gate_pad, b_gate, norm_w)


def _tail_body(tiles_per_seq, tc, ma_ref, mb_ref, h_ref, woa_ref, wob_ref,
               g1_ref, b1_ref, wu_ref, cw_ref, cb_ref, wd_ref, g2_ref, b2_ref, y_ref,
               act, scratch_g, scratch_v, carry):
    at_start = pl.program_id(0) % tiles_per_seq == 0
    mix = (jnp.dot(ma_ref[...], woa_ref[...], preferred_element_type=F32)
           + jnp.dot(mb_ref[...], wob_ref[...], preferred_element_type=F32))
    x = _layer_norm(DEEPNORM_ALPHA * h_ref[...] + mix, g1_ref[...], b1_ref[...])
    xb = x.astype(BF16)
    nchunk = D_FF // tc

    def branch(idx, scratch):
        lo = idx * tc
        cur = jnp.dot(xb, wu_ref[:, lo:lo + tc], preferred_element_type=F32)
        return _causal_conv(cur, cw_ref[:, lo:lo + tc], scratch, carry.at[idx], at_start) + cb_ref[:, lo:lo + tc]

    for c in range(nchunk):
        gate = branch(c, scratch_g)
        val = branch(nchunk + c, scratch_v)
        act[:, c * tc:(c + 1) * tc] = (_silu(gate) * val).astype(BF16)
    ffn = jnp.dot(act[...], wd_ref[...], preferred_element_type=F32)
    y_ref[...] = _layer_norm(DEEPNORM_ALPHA * x + ffn, g2_ref[...], b2_ref[...])


def _layer_tail(mix_a, mix_b, h, w_out_a, w_out_b, g1, b1, w_up, conv_w, conv_b, w_down, g2, b2,
                layer, seq):
    t = h.shape[0]
    tm, tc = TAIL_TM, FFN_TC

    def tile(a):
        return pl.BlockSpec((tm, a.shape[1]), lambda i: (i, 0))

    params = (w_out_a, w_out_b, g1, b1, w_up, conv_w, conv_b, w_down, g2, b2)
    param_specs = [_resident(w_out_a), _resident(w_out_b), _resident(g1), _resident(b1),
                   _resident(w_up, layer), _resident(conv_w), _resident(conv_b), _resident(w_down, layer),
                   _resident(g2), _resident(b2)]
    return pl.pallas_call(
        functools.partial(_tail_body, seq // tm, tc),
        grid=(t // tm,),
        in_specs=[tile(mix_a), tile(mix_b), tile(h)] + param_specs,
        out_specs=pl.BlockSpec((tm, D_MODEL), lambda i: (i, 0)),
        out_shape=jax.ShapeDtypeStruct((t, D_MODEL), F32),
        scratch_shapes=[pltpu.VMEM((tm, D_FF), BF16), pltpu.VMEM((HALO + tm, tc), F32),
                        pltpu.VMEM((HALO + tm, tc), F32), pltpu.VMEM((2 * D_FF // tc, HALO, tc), F32)],
        compiler_params=_cparams(("arbitrary",)),
        name="layer_tail",
    )(mix_a, mix_b, h, *params)


def _even_w_in(w):
    w_t = w.T
    a_end = 4 * EVA_PART
    gates = w_t[a_end:a_end + 2 * GDN_HEADS]
    qkv_b = w_t[a_end + 2 * GDN_HEADS:]
    pad = jnp.zeros((EVB_COLS - EVB_BA - 2 * GDN_HEADS, w.shape[0]), w.dtype)
    return w_t[:a_end].astype(BF16), jnp.concatenate([qkv_b, gates, pad], axis=0).astype(BF16)


def _odd_w_in(w):
    w_t = w.T
    pad = jnp.zeros((LANES - GLA_RANK, w.shape[0]), w.dtype)
    return w_t[:OD_COLS].astype(BF16), jnp.concatenate([w_t[OD_COLS:], pad], axis=0).astype(BF16)


def _even_mixer(h, rel_bias, w_in, conv_w, a_log, dt_bias, norm_w, w_out, bsz, seq):
    w_a, w_b = _even_w_in(w_in)
    proj_a, proj_b = _even_in_proj(h, w_a, w_b, conv_w, seq, PROJ_TM)
    gate_pad = jnp.zeros((1, LANES), F32)
    alog_pad = lax.dynamic_update_slice(gate_pad, a_log[None].astype(F32), (0, GDN_HEADS))
    dtb_pad = lax.dynamic_update_slice(gate_pad, dt_bias[None].astype(F32), (0, GDN_HEADS))
    o_a = _gdn(proj_a, proj_b, alog_pad, dtb_pad, norm_w[None], bsz, seq)
    tiles = _bias_tiles(rel_bias, *_swa_bias_tables(), scale=LOG2E)
    ng, two_blk = len(SWA_CONFIGS), 2 * SWA_BLOCK
    bias = tiles.reshape(SWA_HEADS // 2, 2, 2, ng, SWA_BLOCK, two_blk).transpose(3, 0, 2, 1, 4, 5)
    o_b = _swa(proj_b, bias.reshape(ng, SWA_HEADS // 2, 2, two_blk, two_blk), bsz, seq)
    w_out = w_out.astype(BF16)
    return o_a, o_b, w_out[:EVA_PART], w_out[EVA_PART:]


def _odd_mixer(h, rel_bias, w_in, lam_params, diff_norm_w, w_gate, b_gate, gla_norm_w, w_out,
               lam_init, bsz, seq):
    w_main, w_gd = _odd_w_in(w_in)
    proj, gate_in = _odd_in_proj(h, w_main, w_gd, PROJ_TM)
    blk = min(DIFF_BLOCK, seq)
    bucket, neg = _diff_bias_tables(blk)
    bias = _bias_tiles(rel_bias, bucket, neg, scale=LOG2E, base_bucket=NUM_BUCKETS - 1, toeplitz=True)
    o_c = _diff_attention(proj, bias, lam_params, diff_norm_w[None], lam_init, bsz, seq, blk)
    w_gate_pad = jnp.concatenate(
        [w_gate, jnp.zeros((LANES - GLA_RANK, w_gate.shape[1]), w_gate.dtype)], axis=0)
    o_d = _gla(proj, gate_in, w_gate_pad, b_gate[None], gla_norm_w[None], bsz, seq)
    diff_v = DIFF_HEADS * 2 * DIFF_DH
    w_out = w_out.astype(BF16)
    return o_c, o_d, w_out[:diff_v], w_out[diff_v:]


def kernel(x, rel_bias, w_in_even, gdn_conv_w, gdn_a_log, gdn_dt_bias, gdn_norm_w, w_out_even,
           w_in_odd, diff_lambda, diff_norm_w, gla_w_gate, gla_b_gate, gla_norm_w, w_out_odd,
           ffn_w_up, ffn_conv_w, ffn_conv_b, ffn_w_down, ln_g, ln_b):
    bsz, seq, d = x.shape
    h = x.reshape(bsz * seq, d)
    w_up, w_down = ffn_w_up.astype(BF16), ffn_w_down.astype(BF16)
    for layer in range(DEPTH):
        i = layer // 2
        if layer % 2 == 0:
            mixed = _even_mixer(h, rel_bias, w_in_even[i], gdn_conv_w[i], gdn_a_log[i], gdn_dt_bias[i],
                                gdn_norm_w[i], w_out_even[i], bsz, seq)
        else:
            lam_init = 0.8 - 0.6 * math.exp(-0.3 * layer)
            mixed = _odd_mixer(h, rel_bias, w_in_odd[i], diff_lambda[i], diff_norm_w[i], gla_w_gate[i],
                               gla_b_gate[i], gla_norm_w[i], w_out_odd[i], lam_init, bsz, seq)
        h = _layer_tail(*mixed[:2], h, *mixed[2:], ln_g[layer, 0][None], ln_b[layer, 0][None],
                        w_up, ffn_conv_w[layer], ffn_conv_b[layer][None], w_down,
                        ln_g[layer, 1][None], ln_b[layer, 1][None], layer, seq)
    return h.reshape(bsz, seq, d)
```

```python
import functools
import math

import numpy as np
import jax
import jax.numpy as jnp
from jax import lax
from jax.experimental import pallas as pl
from jax.experimental.pallas import tpu as pltpu

F32 = jnp.float32
BF16 = jnp.bfloat16
HI = lax.Precision.HIGHEST

D_MODEL = 1024
DEPTH = 2
DEEPNORM_ALPHA = (2 * DEPTH) ** 0.25
LN_EPS = 1e-5
RMS_EPS = 1e-6
NUM_BUCKETS = 32
REL_MAX_DIST = 2048
GDN_HEADS = 6
GDN_D = 128
CHUNK = 64
GDN_GROUP = 256
GDN_STEP_GROUPS = 2
SWA_CONFIGS = ((128, 1), (512, 4), (2048, 16))
SWA_HEADS = 4
SWA_DH = 64
SWA_BLOCK = 128
SWA_UNROLL = 8
SWA_SPAN = SWA_BLOCK * max(d for _, d in SWA_CONFIGS)
DIFF_HEADS = 4
DIFF_DH = 64
DIFF_BLOCK = 512
LOG2E = math.log2(math.e)
GLA_HEADS = 4
GLA_DK = 64
GLA_DV = 128
GLA_RANK = 16
GLA_TAU = 16.0
GLA_SUB = 16
GLA_GROUP = 256
GLA_STEP_GROUPS = 4
GLA_MAX_DECAY = 60.0
D_FF = 2816

LANES = 128
SUBLANES = 8
HALO = 8
VMEM_LIMIT = 56 * 1024 * 1024
NEG_BIG = -1e30

EVA_PART = GDN_HEADS * GDN_D
EVB_BA = 2304
EVB_COLS = 2560
EVB_TN = 512
OD_QC, OD_KC, OD_VC = 0, 512, 1024
OD_QD, OD_KD, OD_VD, OD_RD = 1536, 1792, 2048, 2560
OD_COLS = 3072
OD_TN = 768
PROJ_TM = 512
TAIL_TM = 512
FFN_TC = 256


def _cparams(sem):
    return pltpu.CompilerParams(dimension_semantics=sem, vmem_limit_bytes=VMEM_LIMIT)


def _bdot(a, b):
    return jnp.dot(a.astype(BF16), b.astype(BF16), preferred_element_type=F32)


def _dot_nt(a, b):
    return lax.dot_general(a, b, (((1,), (1,)), ((), ())), preferred_element_type=F32)


def _bdot_nt(a, b):
    return lax.dot_general(a.astype(BF16), b.astype(BF16), (((1,), (1,)), ((), ())),
                           preferred_element_type=F32)


def _bdot_tn(a, b):
    return lax.dot_general(a.astype(BF16), b.astype(BF16), (((0,), (0,)), ((), ())),
                           preferred_element_type=F32)


def _sigmoid(x):
    return 1.0 / (1.0 + jnp.exp(-x))


def _silu(x):
    return x * _sigmoid(x)


def _softplus(x):
    return jnp.maximum(x, 0.0) + jnp.log1p(jnp.exp(-jnp.abs(x)))


def _log_sigmoid(x):
    return -_softplus(-x)


def _resident(a, lead=None):
    if lead is None:
        return pl.BlockSpec(a.shape, lambda i: (0,) * a.ndim, pipeline_mode=pl.Buffered(1))
    return pl.BlockSpec((None,) + a.shape[1:], lambda i: (lead,) + (0,) * (a.ndim - 1),
                        pipeline_mode=pl.Buffered(1))


def _odd_in_body(x_ref, w_ref, wg_ref, o_ref, og_ref):
    xb = x_ref[...].astype(BF16)
    for c in range(OD_COLS // OD_TN):
        cols = slice(c * OD_TN, (c + 1) * OD_TN)
        o_ref[:, cols] = _dot_nt(xb, w_ref[cols, :]).astype(o_ref.dtype)
    og_ref[...] = _dot_nt(xb, wg_ref[...])


def _odd_in_proj(x, w, w_gate_in, tm):
    t, k = x.shape
    return pl.pallas_call(
        _odd_in_body,
        grid=(t // tm,),
        in_specs=[pl.BlockSpec((tm, k), lambda i: (i, 0)), _resident(w), _resident(w_gate_in)],
        out_specs=[pl.BlockSpec((tm, OD_COLS), lambda i: (i, 0)), pl.BlockSpec((tm, LANES), lambda i: (i, 0))],
        out_shape=[jax.ShapeDtypeStruct((t, OD_COLS), BF16), jax.ShapeDtypeStruct((t, LANES), F32)],
        compiler_params=_cparams(("parallel",)),
        name="odd_in_proj",
    )(x, w, w_gate_in)


def _rel_bucket_np(dist):
    max_exact = NUM_BUCKETS // 2
    d = np.maximum(dist, 1).astype(np.float32)
    large = max_exact + (np.log(d / max_exact) / math.log(REL_MAX_DIST / max_exact)
                         * (NUM_BUCKETS - max_exact)).astype(np.int32)
    large = np.minimum(large, NUM_BUCKETS - 1)
    return np.where(dist < max_exact, dist, large).astype(np.int32)


def _bias_body(scale, base_bucket, tile_buckets, toeplitz, rb_ref, bucket_ref, neg_ref, o_ref):
    h = pl.program_id(0)
    t = pl.program_id(1)
    base = 0.0 if base_bucket is None else rb_ref[base_bucket, h]
    for tile, present in enumerate(tile_buckets):
        @pl.when(t == tile)
        def _(present=present):
            bucket = bucket_ref[0]
            acc = neg_ref[0]
            for b in present:
                acc = acc + jnp.where(bucket == b, (rb_ref[b, h] - base) * scale, 0.0)
            if toeplitz:
                n = o_ref.shape[2]
                rows = jnp.broadcast_to(acc[0:1], (n, 2 * n))
                acc = pltpu.roll(rows, n, 1, stride=1, stride_axis=0)[:, :n]
            o_ref[0, 0] = acc


def _bias_tiles(rel_bias, bucket, neg, scale=1.0, base_bucket=None, toeplitz=False):
    nt, r, c = bucket.shape
    nh = rel_bias.shape[1]
    out_r, out_c = (c // 2, c // 2) if toeplitz else (r, c)
    tile_buckets = tuple(tuple(int(b) for b in np.unique(bucket[t][neg[t] == 0]) if b != base_bucket)
                         for t in range(nt))
    return pl.pallas_call(
        functools.partial(_bias_body, scale, base_bucket, tile_buckets, toeplitz),
        grid=(nh, nt),
        in_specs=[pl.BlockSpec(memory_space=pltpu.SMEM),
                  pl.BlockSpec((1, r, c), lambda h, t: (t, 0, 0)),
                  pl.BlockSpec((1, r, c), lambda h, t: (t, 0, 0))],
        out_specs=pl.BlockSpec((1, 1, out_r, out_c), lambda h, t: (h, t, 0, 0)),
        out_shape=jax.ShapeDtypeStruct((nh, nt, out_r, out_c), F32),
        compiler_params=_cparams(("parallel", "parallel")),
        name="rel_bias_tiles",
    )(rel_bias, jnp.asarray(bucket), jnp.asarray(neg))


def _causal_conv(cur, w, scratch, carry, at_start):
    width = w.shape[0]
    rows = cur.shape[0]
    scratch[0:HALO, :] = jnp.where(at_start, 0.0, carry[...])
    scratch[HALO:, :] = cur
    carry[...] = cur[rows - HALO:]
    y = w[width - 1:width, :] * cur
    for j in range(width - 1):
        back = width - 1 - j
        y = y + w[j:j + 1, :] * scratch[HALO - back:HALO - back + rows, :]
    return y


def _even_in_body(tiles_per_seq, x_ref, wa_ref, wb_ref, cw_ref, oa_ref, ob_ref, scratch, carry):
    at_start = pl.program_id(0) % tiles_per_seq == 0
    x = x_ref[...].astype(BF16)
    for part in range(3):
        cols = slice(part * EVA_PART, (part + 1) * EVA_PART)
        y = _dot_nt(x, wa_ref[cols, :])
        c = _silu(_causal_conv(y, cw_ref[:, cols], scratch, carry.at[part], at_start))
        for hd in range(GDN_HEADS):
            ch = c[:, hd * GDN_D:(hd + 1) * GDN_D]
            if part < 2:
                inv = lax.rsqrt(jnp.sum(ch * ch, axis=-1, keepdims=True) + RMS_EPS)
                ch = ch * (inv * GDN_D ** -0.5 if part == 0 else inv)
            lo = part * EVA_PART + hd * GDN_D
            oa_ref[:, lo:lo + GDN_D] = ch.astype(oa_ref.dtype)
    z_cols = slice(3 * EVA_PART, 4 * EVA_PART)
    oa_ref[:, z_cols] = _dot_nt(x, wa_ref[z_cols, :]).astype(oa_ref.dtype)
    for c in range(EVB_COLS // EVB_TN):
        cols = slice(c * EVB_TN, (c + 1) * EVB_TN)
        ob_ref[:, cols] = _dot_nt(x, wb_ref[cols, :])


def _even_in_proj(x, w_a, w_b, conv_w, seq, tm):
    t, k = x.shape
    return pl.pallas_call(
        functools.partial(_even_in_body, seq // tm),
        grid=(t // tm,),
        in_specs=[pl.BlockSpec((tm, k), lambda i: (i, 0)),
                  _resident(w_a), _resident(w_b), _resident(conv_w)],
        out_specs=[pl.BlockSpec((tm, w_a.shape[0]), lambda i: (i, 0)),
                   pl.BlockSpec((tm, EVB_COLS), lambda i: (i, 0))],
        out_shape=[jax.ShapeDtypeStruct((t, w_a.shape[0]), BF16), jax.ShapeDtypeStruct((t, EVB_COLS), F32)],
        scratch_shapes=[pltpu.VMEM((HALO + tm, EVA_PART), F32), pltpu.VMEM((3, HALO, EVA_PART), F32)],
        compiler_params=_cparams(("arbitrary",)),
        name="even_in_proj",
    )(x, w_a, w_b, conv_w)


def _gdn_body(q_ref, k_ref, v_ref, z_ref, ba_ref, alog_ref, dtb_ref, nw_ref, o_ref, state):
    n = pl.program_id(1)

    @pl.when(n == 0)
    def _():
        state[...] = jnp.zeros_like(state)

    grp = GDN_GROUP
    nc = grp // CHUNK
    nh = GDN_HEADS
    hs = range(nh)
    ri = lax.broadcasted_iota(jnp.int32, (grp, grp), 0)
    ci = lax.broadcasted_iota(jnp.int32, (grp, grp), 1)
    same = (ri // CHUNK) == (ci // CHUNK)
    incl = same & (ri >= ci)
    strict = same & (ri > ci)
    eye = (ri == ci).astype(F32)
    tri = incl.astype(BF16)

    def chunk_local(groups):
        units = [(g, h) for g in range(len(groups)) for h in hs]
        gam_all, gam_rows, beta_all = [], [], []
        for rows in groups:
            ba = ba_ref[rows, :]
            beta_all.append(_sigmoid(ba))
            g_all = -jnp.exp(alog_ref[...]) * _softplus(ba + dtb_ref[...])
            g_hi = g_all.astype(BF16)
            rem = g_all - g_hi.astype(F32)
            g_mid = rem.astype(BF16)
            g_lo = (rem - g_mid.astype(F32)).astype(BF16)
            gam_all.append(jnp.dot(tri, g_hi, preferred_element_type=F32)
                           + jnp.dot(tri, g_mid, preferred_element_type=F32)
                           + jnp.dot(tri, g_lo, preferred_element_type=F32))
            gam_rows.append(gam_all[-1].T)

        def head(ref, g, h):
            return ref[groups[g], h * GDN_D:(h + 1) * GDN_D]

        q = [head(q_ref, g, h) for g, h in units]
        k = [head(k_ref, g, h) for g, h in units]
        v = [head(v_ref, g, h) for g, h in units]
        gam = [jnp.broadcast_to(gam_all[g][:, nh + h:nh + h + 1], (grp, GDN_D)) for g, h in units]
        bcol = [beta_all[g][:, h:h + 1] for g, h in units]
        us = range(len(units))
        decay, x, inv = [], [], []
        for i, (g, h) in enumerate(units):
            diff = gam[i][:, 0:1] - gam_rows[g][nh + h:nh + h + 1, :]
            decay.append(jnp.where(incl, jnp.exp(jnp.where(incl, diff, 0.0)), 0.0))
            kk = _bdot_nt(k[i], k[i])
            x.append(-jnp.where(strict, bcol[i] * kk * decay[i], 0.0))
            inv.append(eye + x[i])
        for _ in range(5):
            for i in us:
                x[i] = _bdot(x[i], x[i])
                inv[i] = inv[i] + _bdot(inv[i], x[i])
        out = [tuple([] for _ in range(6)) for _ in groups]
        for i, (g, h) in enumerate(units):
            eg = jnp.exp(gam[i])
            uw = _bdot(inv[i], jnp.concatenate([v[i] * bcol[i], k[i] * (bcol[i] * eg)], axis=1))
            kd, gl = [], []
            for c in range(nc):
                last = gam[i][(c + 1) * CHUNK - 1:(c + 1) * CHUNK, :]
                kd.append(k[i][c * CHUNK:(c + 1) * CHUNK] * jnp.exp(last - gam[i][c * CHUNK:(c + 1) * CHUNK]))
                gl.append(jnp.exp(last))
            for lst, val in zip(out[g], (uw[:, :GDN_D], uw[:, GDN_D:], _bdot_nt(q[i], k[i]) * decay[i],
                                         q[i] * eg, kd, gl)):
                lst.append(val)
        return out

    def recurrence(rows, local, s):
        u, w, qk, q_dec, k_dec, g_last = local
        q_s = [[] for _ in hs]
        delta = [[] for _ in hs]
        for c in range(nc):
            sl = slice(c * CHUNK, (c + 1) * CHUNK)
            for h in hs:
                r = _bdot(jnp.concatenate([w[h][sl], q_dec[h][sl]], axis=0), s[h])
                d = u[h][sl] - r[:CHUNK]
                q_s[h].append(r[CHUNK:])
                delta[h].append(d)
                s[h] = g_last[h][c] * s[h] + _bdot_tn(k_dec[h][c], d)
        for h in hs:
            o = jnp.concatenate(q_s[h], axis=0) + _bdot(qk[h], jnp.concatenate(delta[h], axis=0))
            o = o * lax.rsqrt(jnp.mean(o * o, axis=-1, keepdims=True) + RMS_EPS) * nw_ref[...]
            z = z_ref[rows, h * GDN_D:(h + 1) * GDN_D].astype(F32)
            o_ref[rows, h * GDN_D:(h + 1) * GDN_D] = (o * _silu(z)).astype(o_ref.dtype)
        return s

    groups = [slice(g * grp, (g + 1) * grp) for g in range(GDN_STEP_GROUPS)]
    s = [state[h] for h in hs]
    for rows, local in zip(groups, chunk_local(groups)):
        s = recurrence(rows, local, s)
    for h in hs:
        state[h] = s[h]


def _gdn(proj_a, proj_b, alog_pad, dtb_pad, norm_w, bsz, seq):
    t = proj_a.shape[0]
    rows = GDN_GROUP * GDN_STEP_GROUPS
    spb = seq // rows
    width = GDN_HEADS * GDN_D

    def at(col):
        return lambda b, n: (b * spb + n, col)

    return pl.pallas_call(
        _gdn_body,
        grid=(bsz, spb),
        in_specs=[pl.BlockSpec((rows, width), at(0)),
                  pl.BlockSpec((rows, width), at(1)),
                  pl.BlockSpec((rows, width), at(2)),
                  pl.BlockSpec((rows, width), at(3)),
                  pl.BlockSpec((rows, LANES), at(EVB_BA // LANES)),
                  pl.BlockSpec((1, LANES), lambda b, n: (0, 0)),
                  pl.BlockSpec((1, LANES), lambda b, n: (0, 0)),
                  pl.BlockSpec((1, LANES), lambda b, n: (0, 0))],
        out_specs=pl.BlockSpec((rows, width), at(0)),
        out_shape=jax.ShapeDtypeStruct((t, width), BF16),
        scratch_shapes=[pltpu.VMEM((GDN_HEADS, GDN_D, GDN_D), F32)],
        compiler_params=_cparams(("parallel", "arbitrary")),
        name="gdn",
    )(proj_a, proj_a, proj_a, proj_a, proj_b, alog_pad, dtb_pad, norm_w)


def _swa_bias_tables():
    qi = np.arange(SWA_BLOCK)[:, None] + SWA_BLOCK
    kj = np.arange(2 * SWA_BLOCK)[None, :]
    rel = qi - kj
    buckets, negs = [], []
    for has_prev in (True, False):
        for window, dilation in SWA_CONFIGS:
            valid = (rel >= 0) & (rel <= window // dilation) & (has_prev | (kj >= SWA_BLOCK))
            buckets.append(_rel_bucket_np(np.maximum(rel, 0) * dilation))
            negs.append(np.where(valid, 0.0, NEG_BIG))
    return np.stack(buckets).astype(np.int32), np.stack(negs).astype(np.float32)


def _swa_body(*refs):
    ng = len(SWA_CONFIGS)
    ins, bias_ref, o_ref = refs[:5 * ng], refs[5 * ng], refs[5 * ng + 1]
    scratch = refs[5 * ng + 2:]
    kbufs, vbufs, o_scr, lse_scr = scratch[:ng], scratch[ng:2 * ng], scratch[2 * ng], scratch[2 * ng + 1]
    j = pl.program_id(1)
    blk = SWA_BLOCK
    first_head = lax.broadcasted_iota(jnp.int32, (blk, LANES), 1) < SWA_DH

    for g, (_, d) in enumerate(SWA_CONFIGS):
        q_ref, kc_ref, kp_ref, vc_ref, vp_ref = ins[5 * g:5 * g + 5]
        kbuf, vbuf = kbufs[g], vbufs[g]
        halo = blk * d
        kbuf[0:halo, :] = kp_ref[...]
        kbuf[halo:, :] = kc_ref[...]
        vbuf[0:halo, :] = vp_ref[...]
        vbuf[halo:, :] = vc_ref[...]

        def unit(u, carry, g=g, d=d, halo=halo, q_ref=q_ref, kbuf=kbuf, vbuf=vbuf):
            base = (u // d) * halo + u % d
            q = q_ref[pl.ds(base, blk, stride=d), :] * (SWA_DH ** -0.5 * LOG2E)
            k = kbuf[pl.ds(base, 2 * blk, stride=d), :]
            v = vbuf[pl.ds(base, 2 * blk, stride=d), :]
            lhs = jnp.concatenate([jnp.where(first_head, q, 0.0), jnp.where(first_head, 0.0, q)], axis=0)
            no_prev = jnp.where((j == 0) & (u < d), 1, 0)
            s = _bdot_nt(lhs, k) + bias_ref[g, 0, no_prev]
            m = jnp.max(s, axis=-1, keepdims=True)
            p = jnp.exp2(s - m)
            l = jnp.sum(p, axis=-1, keepdims=True)
            o2 = _bdot(p, v) * (1.0 / l)
            lse2 = m + jnp.log2(l)
            o_scr[g, pl.ds(base, blk, stride=d), :] = jnp.where(first_head, o2[:blk], o2[blk:])
            lse_scr[g, pl.ds(base, blk, stride=d), :] = jnp.where(first_head, lse2[:blk], lse2[blk:])
            return carry

        lax.fori_loop(0, SWA_SPAN // blk, unit, 0, unroll=SWA_UNROLL)

    def combine(c, carry):
        rows = pl.ds(pl.multiple_of(c * 2 * blk, 2 * blk), 2 * blk)
        lse = [lse_scr[g, rows, :] for g in range(ng)]
        m = functools.reduce(jnp.maximum, lse)
        e = [jnp.exp2(x - m) for x in lse]
        den = functools.reduce(lambda x, y: x + y, e)
        o_ref[rows, :] = functools.reduce(
            lambda x, y: x + y, [(e[g] / den) * o_scr[g, rows, :] for g in range(ng)]).astype(o_ref.dtype)
        return carry

    lax.fori_loop(0, SWA_SPAN // (2 * blk), combine, 0)


def _swa(proj, bias, bsz, seq):
    t = proj.shape[0]
    ng = len(SWA_CONFIGS)
    nspan = seq // SWA_SPAN
    npair = SWA_HEADS * SWA_DH // LANES
    group_cols = SWA_HEADS * SWA_DH // LANES
    in_specs, scratch_k = [], []
    for g, (_, d) in enumerate(SWA_CONFIGS):
        halo = SWA_BLOCK * d
        per_span = SWA_SPAN // halo

        def cur(which, g=g):
            col = (which * ng + g) * group_cols
            return lambda b, j, p: (b * nspan + j, col + p)

        def prev(which, g=g, per_span=per_span):
            col = (which * ng + g) * group_cols
            return lambda b, j, p: (jnp.maximum((b * nspan + j) * per_span - 1, 0), col + p)

        in_specs += [pl.BlockSpec((SWA_SPAN, LANES), cur(0)),
                     pl.BlockSpec((SWA_SPAN, LANES), cur(1)), pl.BlockSpec((halo, LANES), prev(1)),
                     pl.BlockSpec((SWA_SPAN, LANES), cur(2)), pl.BlockSpec((halo, LANES), prev(2))]
        scratch_k.append(pltpu.VMEM((halo + SWA_SPAN, LANES), F32))
    in_specs.append(pl.BlockSpec((ng, 1, 2, 2 * SWA_BLOCK, 2 * SWA_BLOCK), lambda b, j, p: (0, p, 0, 0, 0)))
    return pl.pallas_call(
        _swa_body,
        grid=(bsz, nspan, npair),
        in_specs=in_specs,
        out_specs=pl.BlockSpec((SWA_SPAN, LANES), lambda b, j, p: (b * nspan + j, p)),
        out_shape=jax.ShapeDtypeStruct((t, npair * LANES), BF16),
        scratch_shapes=scratch_k + scratch_k + [pltpu.VMEM((ng, SWA_SPAN, LANES), F32)] * 2,
        compiler_params=_cparams(("parallel", "parallel", "parallel")),
        name="swa",
    )(*([proj] * (5 * ng)), bias)


def _layer_norm(y, g, b):
    mu = jnp.mean(y, axis=-1, keepdims=True)
    yc = y - mu
    var = jnp.mean(yc * yc, axis=-1, keepdims=True)
    return yc * lax.rsqrt(var + LN_EPS) * g + b


def _diff_bias_tables(blk):
    buckets = _rel_bucket_np(np.arange(2 * REL_MAX_DIST))
    far = int(np.max(np.nonzero(buckets != NUM_BUCKETS - 1)[0])) + 1
    nb = -(-(far + blk - 1) // blk)
    dist = np.stack([t * blk - blk + np.arange(2 * blk) for t in range(nb + 1)])[:, None, :]
    dist = np.broadcast_to(dist, (nb + 1, SUBLANES, 2 * blk))
    bucket = _rel_bucket_np(np.maximum(dist, 0))
    neg = np.where(dist >= 0, 0.0, NEG_BIG).astype(np.float32)
    return bucket, neg


def _diff_body(blk, nb, lam_init, q_ref, qn_ref, k_ref, v_ref, bias_ref, lam_ref, nw_ref, o_ref,
               vt, acc1, acc2, s_a, s_b):
    qi = pl.program_id(2)
    dv = 2 * DIFF_DH
    seq = k_ref.shape[0]

    @pl.when(qi == 0)
    def _():
        vt[dv:, :] = jnp.ones((vt.shape[0] - dv, seq), BF16)

        def fill(c, carry):
            st = pl.multiple_of(c * blk, blk)
            vt[0:dv, pl.ds(st, blk)] = v_ref[pl.ds(st, blk), :].astype(F32).T.astype(BF16)
            return carry

        lax.fori_loop(0, seq // blk, fill, 0)

    feature = lax.broadcasted_iota(jnp.int32, (dv, blk), 0)

    def components(ref):
        q_t = (ref[...].astype(F32) * (DIFF_DH ** -0.5 * LOG2E)).T
        return (jnp.where(feature < DIFF_DH, q_t, 0.0).astype(BF16),
                jnp.where(feature >= DIFF_DH, q_t, 0.0).astype(BF16))

    qs = components(q_ref)
    qs_next = components(qn_ref)
    accs = (acc1, acc2)
    for acc in accs:
        acc[...] = jnp.zeros_like(acc)

    last = pl.num_programs(2) - 1

    def key_rows(kj):
        return pl.ds(pl.multiple_of(jnp.minimum(kj, last) * blk, blk), blk)

    def scores(kj, dst, queries=qs, offset=None):
        k = k_ref[key_rows(kj), :]
        for c, qc in enumerate(queries):
            s = jnp.dot(k, qc, preferred_element_type=F32)
            dst[c] = s if offset is None else s + bias_ref[0, jnp.minimum(offset, nb)]

    def consume(kj, src, ms):
        vtb = vt[:, key_rows(kj)]
        out = []
        for c, (m, acc) in enumerate(zip(ms, accs)):
            s = src[c]
            m_new = jnp.maximum(m, jnp.max(s, axis=0, keepdims=True))
            p = jnp.exp2(s - m_new).astype(BF16)
            acc[...] = jnp.exp2(m - m_new) * acc[...] + jnp.dot(vtb, p, preferred_element_type=F32)
            out.append(m_new)
        return tuple(out)

    nblocks = qi + 1
    trips = nblocks // 2
    odd = nblocks % 2 == 1
    far_trips = jnp.maximum(qi - nb, 0) // 2

    def pair(biased, t, ms):
        kj = 2 * t
        scores(kj + 1, s_b, offset=qi - (kj + 1) if biased else None)
        ms = consume(kj, s_a, ms)
        if biased:
            hand_off = (t == trips - 1) & jnp.logical_not(odd)
            scores(jnp.where(hand_off, 0, kj + 2), s_a,
                   tuple(jnp.where(hand_off, qn, qc) for qc, qn in zip(qs, qs_next)),
                   offset=jnp.where(hand_off, qi + 1, qi - (kj + 2)))
        else:
            scores(kj + 2, s_a)
        return consume(kj + 1, s_b, ms)

    @pl.when(qi == 0)
    def _():
        scores(0, s_a, offset=0)

    m0 = jnp.full((1, blk), NEG_BIG, F32)
    ms = lax.fori_loop(0, far_trips, functools.partial(pair, False), (m0, m0))
    ms = lax.fori_loop(far_trips, trips, functools.partial(pair, True), ms)

    def finish():
        lp = lam_ref[...]
        lam = (jnp.exp(jnp.sum(lp[0:1] * lp[1:2], axis=-1, keepdims=True))
               - jnp.exp(jnp.sum(lp[2:3] * lp[3:4], axis=-1, keepdims=True)) + lam_init)
        a1, a2 = acc1[...], acc2[...]
        o_t = a1[:dv] * (1.0 / a1[dv:dv + 1]) - a2[:dv] * (lam / a2[dv:dv + 1])
        o = o_t.T
        o = o * lax.rsqrt(jnp.mean(o * o, axis=-1, keepdims=True) + RMS_EPS) * nw_ref[...]
        o_ref[...] = (o * (1.0 - lam_init)).astype(o_ref.dtype)

    @pl.when(odd)
    def _():
        consume(qi, s_a, ms)
        scores(0, s_a, qs_next, offset=qi + 1)
        finish()

    @pl.when(jnp.logical_not(odd))
    def _():
        finish()


def _diff_attention(proj, bias, lam_params, norm_w, lam_init, bsz, seq, blk):
    t = proj.shape[0]
    nq = seq // blk
    nt = bias.shape[1]
    nb = nt - 1
    width = 2 * DIFF_DH
    ones_rows = 2 * SUBLANES
    return pl.pallas_call(
        functools.partial(_diff_body, blk, nb, lam_init),
        grid=(bsz, DIFF_HEADS, nq),
        in_specs=[pl.BlockSpec((blk, width), lambda b, h, i: (b * nq + i, OD_QC // width + h)),
                  pl.BlockSpec((blk, width),
                               lambda b, h, i: (b * nq + jnp.minimum(i + 1, nq - 1), OD_QC // width + h)),
                  pl.BlockSpec((seq, width), lambda b, h, i: (b, OD_KC // width + h)),
                  pl.BlockSpec((seq, width), lambda b, h, i: (b, OD_VC // width + h)),
                  pl.BlockSpec((1, nt, blk, blk), lambda b, h, i: (h, 0, 0, 0)),
                  pl.BlockSpec((4, DIFF_DH), lambda b, h, i: (0, 0)),
                  pl.BlockSpec((1, width), lambda b, h, i: (0, 0))],
        out_specs=pl.BlockSpec((blk, width), lambda b, h, i: (b * nq + i, h)),
        out_shape=jax.ShapeDtypeStruct((t, DIFF_HEADS * width), BF16),
        scratch_shapes=[pltpu.VMEM((width + ones_rows, seq), BF16),
                        pltpu.VMEM((width + ones_rows, blk), F32), pltpu.VMEM((width + ones_rows, blk), F32),
                        pltpu.VMEM((2, blk, blk), F32), pltpu.VMEM((2, blk, blk), F32)],
        compiler_params=_cparams(("arbitrary", "arbitrary", "arbitrary")),
        name="diff_attn",
    )(proj, proj, proj, proj, bias, lam_params, norm_w)


def _gla_body(q_ref, k_ref, gd_ref, v_ref, r_ref, wg_ref, bg_ref, nw_ref, o_ref, state, part):
    n = pl.program_id(1)

    @pl.when(n == 0)
    def _():
        state[...] = jnp.zeros_like(state)

    rows = GLA_GROUP
    nc = rows // CHUNK
    npair = GLA_HEADS // 2
    groups = [slice(g * rows, (g + 1) * rows) for g in range(GLA_STEP_GROUPS)]
    gate = jnp.dot(gd_ref[...], wg_ref[...], precision=HI, preferred_element_type=F32) + bg_ref[...]
    log_a = _log_sigmoid(gate) * (1.0 / GLA_TAU)

    ri = lax.broadcasted_iota(jnp.int32, (rows, rows), 0)
    ci = lax.broadcasted_iota(jnp.int32, (rows, rows), 1)
    causal = ((ri // CHUNK) == (ci // CHUNK)) & (ri >= ci)
    tri = causal.astype(BF16)
    hi = log_a.astype(BF16)
    rem = log_a - hi.astype(F32)
    mid = rem.astype(BF16)
    lo = (rem - mid.astype(F32)).astype(BF16)
    b_all = [jnp.dot(tri, hi[gr], preferred_element_type=F32) + jnp.dot(tri, mid[gr], preferred_element_type=F32)
             + jnp.dot(tri, lo[gr], preferred_element_type=F32) for gr in groups]

    lane = lax.broadcasted_iota(jnp.int32, (rows, LANES), 1)
    lane_c = lax.broadcasted_iota(jnp.int32, (CHUNK, LANES), 1)
    head_lanes = (lane < GLA_DK, lane >= GLA_DK)
    chunk_lanes = (lane_c < GLA_DK, lane_c >= GLA_DK)
    pairs = []
    for gr, b_g in zip(groups, b_all):
        pairs.append([])
        for p in range(npair):
            cols = slice(p * LANES, (p + 1) * LANES)
            q = q_ref[gr, cols].astype(F32) * GLA_DK ** -0.5
            pairs[-1].append(dict(
                b=b_g[:, cols], q=q, q_dec=q * jnp.exp(b_g[:, cols]), k=k_ref[gr, cols].astype(F32),
                v=[v_ref[gr, (2 * p + hd) * GLA_DV:(2 * p + hd + 1) * GLA_DV].astype(F32) for hd in range(2)]))

    for g, gr in enumerate(groups):
        for c in range(nc):
            sl = slice(c * CHUNK, (c + 1) * CHUNK)
            out_rows = slice(g * rows + c * CHUNK, g * rows + (c + 1) * CHUNK)
            for p, pr in enumerate(pairs[g]):
                bc = pr["b"][sl]
                b_last = bc[CHUNK - 1:CHUNK]
                k_dec = pr["k"][sl] * jnp.exp(b_last - bc)
                e_last = jnp.exp(b_last)
                for hd in range(2):
                    h = 2 * p + hd
                    st = state[h]
                    part[h, out_rows, :] = _bdot_nt(jnp.where(chunk_lanes[hd], pr["q_dec"][sl], 0.0), st)
                    state[h] = st * e_last + _bdot_tn(pr["v"][hd][sl], k_dec)

    def finish(intra):
        for gr, intra_g in zip(groups, intra):
            for h in range(GLA_HEADS):
                o = part[h, gr, :] + intra_g[h]
                o = o * lax.rsqrt(jnp.mean(o * o, axis=-1, keepdims=True) + RMS_EPS) * nw_ref[...]
                gate_r = _silu(r_ref[gr, h * GLA_DV:(h + 1) * GLA_DV].astype(F32))
                o_ref[gr, h * GLA_DV:(h + 1) * GLA_DV] = (o * gate_r).astype(o_ref.dtype)

    def intra_whole_chunk():
        out = []
        for pairs_g in pairs:
            out.append([])
            for pr in pairs_g:
                k_inv = pr["k"] * jnp.exp(jnp.minimum(-pr["b"], GLA_MAX_DECAY))
                for hd in range(2):
                    a = _bdot_nt(jnp.where(head_lanes[hd], pr["q_dec"], 0.0), k_inv)
                    out[-1].append(_bdot(jnp.where(causal, a, 0.0), pr["v"][hd]))
        return out

    def intra_exact():
        out = []
        for pairs_g in pairs:
            out.append([])
            for pr in pairs_g:
                off = _gla_intra_off_diagonal(pr["q"], pr["k"], pr["b"], pr["v"], nc)
                diag = _gla_intra_diagonal(pr["q"], pr["k"], pr["b"], pr["v"])
                out[-1] += [off[hd] + diag[hd] for hd in range(2)]
        return out

    chunk_decay = functools.reduce(jnp.maximum, [
        jnp.max(-b_g.reshape(nc, CHUNK, npair * LANES)[:, CHUNK - 1:CHUNK, :]) for b_g in b_all])

    @pl.when(chunk_decay <= GLA_MAX_DECAY)
    def _():
        finish(intra_whole_chunk())

    @pl.when(chunk_decay > GLA_MAX_DECAY)
    def _():
        finish(intra_exact())


def _gla_intra_off_diagonal(q, k, b, vs, nc):
    per_chunk = CHUNK // GLA_SUB
    lane = lax.broadcasted_iota(jnp.int32, (GLA_SUB, LANES), 1)
    sub_mask = (lane < GLA_DK, lane >= GLA_DK)
    kcol = lax.broadcasted_iota(jnp.int32, (GLA_SUB, CHUNK), 1)
    outs = ([], [])
    for c in range(nc):
        sl = slice(c * CHUNK, (c + 1) * CHUNK)
        bc, qc, kc = b[sl], q[sl], k[sl]
        a_rows = [[jnp.zeros((GLA_SUB, CHUNK), F32)] for _ in vs]
        for blk in range(1, per_chunk):
            r0 = blk * GLA_SUB
            bref = bc[r0:r0 + 1]
            qs = qc[r0:r0 + GLA_SUB] * jnp.exp(bc[r0:r0 + GLA_SUB] - bref)
            ks = kc * jnp.exp(jnp.minimum(bref - bc, 0.0))
            for hd in range(2):
                a = _bdot_nt(jnp.where(sub_mask[hd], qs, 0.0), ks)
                a_rows[hd].append(jnp.where(kcol < r0, a, 0.0))
        for hd in range(2):
            outs[hd].append(_bdot(jnp.concatenate(a_rows[hd], axis=0), vs[hd][sl]))
    return [jnp.concatenate(o, axis=0) for o in outs]


def _gla_intra_diagonal(q, k, b, vs):
    rows = q.shape[0]
    nsub = rows // GLA_SUB
    b3 = b.reshape(nsub, GLA_SUB, LANES)
    q3 = q.reshape(nsub, GLA_SUB, LANES)
    k3 = k.reshape(nsub, GLA_SUB, LANES)
    v3 = [v.reshape(nsub, GLA_SUB, LANES) for v in vs]
    row3 = lax.broadcasted_iota(jnp.int32, (nsub, GLA_SUB, LANES), 1)
    lane3 = lax.broadcasted_iota(jnp.int32, (nsub, GLA_SUB, LANES), 2)
    rowc = lax.broadcasted_iota(jnp.int32, (nsub, GLA_SUB, 1), 1)
    o3 = [jnp.zeros((nsub, GLA_SUB, LANES), F32) for _ in vs]
    for jj in range(GLA_SUB):
        e = jnp.exp(jnp.where(row3 >= jj, b3 - b3[:, jj:jj + 1, :], 0.0))
        t = q3 * k3[:, jj:jj + 1, :] * e
        w_all = jnp.sum(t, axis=-1, keepdims=True)
        w_a = jnp.sum(jnp.where(lane3 < GLA_DK, t, 0.0), axis=-1, keepdims=True)
        for hd, w in enumerate((w_a, w_all - w_a)):
            o3[hd] = o3[hd] + jnp.where(rowc >= jj, w, 0.0) * v3[hd][:, jj:jj + 1, :]
    return [o.reshape(rows, LANES) for o in o3]


def _gla(proj, gate_in, w_gate_pad, b_gate, norm_w, bsz, seq):
    t = proj.shape[0]
    rows = GLA_GROUP * GLA_STEP_GROUPS
    spb = seq // rows
    qk_w = GLA_HEADS * GLA_DK
    v_w = GLA_HEADS * GLA_DV

    def at(col):
        return lambda b, n: (b * spb + n, col)

    def whole(a):
        return pl.BlockSpec(a.shape, lambda b, n: (0, 0))

    return pl.pallas_call(
        _gla_body,
        grid=(bsz, spb),
        in_specs=[pl.BlockSpec((rows, qk_w), at(OD_QD // qk_w)),
                  pl.BlockSpec((rows, qk_w), at(OD_KD // qk_w)),
                  pl.BlockSpec((rows, LANES), at(0)),
                  pl.BlockSpec((rows, v_w), at(OD_VD // v_w)),
                  pl.BlockSpec((rows, v_w), at(OD_RD // v_w)),
                  whole(w_gate_pad), whole(b_gate), whole(norm_w)],
        out_specs=pl.BlockSpec((rows, v_w), at(0)),
        out_shape=jax.ShapeDtypeStruct((t, v_w), BF16),
        scratch_shapes=[pltpu.VMEM((GLA_HEADS, GLA_DV, LANES), F32), pltpu.VMEM((GLA_HEADS, rows, GLA_DV), F32)],
        compiler_params=_cparams(("parallel", "arbitrary")),
        name="gla",
    )(proj, proj, gate_in, proj, proj, w_gate_pad, b_gate, norm_w)


def _tail_body(tiles_per_seq, tc, ma_ref, mb_ref, h_ref, woa_ref, wob_ref,
               g1_ref, b1_ref, wu_ref, cw_ref, cb_ref, wd_ref, g2_ref, b2_ref, y_ref,
               act, scratch_g, scratch_v, carry):
    at_start = pl.program_id(0) % tiles_per_seq == 0
    mix = (jnp.dot(ma_ref[...], woa_ref[...], preferred_element_type=F32)
           + jnp.dot(mb_ref[...], wob_ref[...], preferred_element_type=F32))
    x = _layer_norm(DEEPNORM_ALPHA * h_ref[...] + mix, g1_ref[...], b1_ref[...])
    xb = x.astype(BF16)
    nchunk = D_FF // tc

    def branch(idx, scratch):
        lo = idx * tc
        cur = jnp.dot(xb, wu_ref[:, lo:lo + tc], preferred_element_type=F32)
        return _causal_conv(cur, cw_ref[:, lo:lo + tc], scratch, carry.at[idx], at_start) + cb_ref[:, lo:lo + tc]

    for c in range(nchunk):
        gate = branch(c, scratch_g)
        val = branch(nchunk + c, scratch_v)
        act[:, c * tc:(c + 1) * tc] = (_silu(gate) * val).astype(BF16)
    ffn = jnp.dot(act[...], wd_ref[...], preferred_element_type=F32)
    y_ref[...] = _layer_norm(DEEPNORM_ALPHA * x + ffn, g2_ref[...], b2_ref[...])


def _layer_tail(mix_a, mix_b, h, w_out_a, w_out_b, g1, b1, w_up, conv_w, conv_b, w_down, g2, b2,
                layer, seq):
    t = h.shape[0]
    tm, tc = TAIL_TM, FFN_TC

    def tile(a):
        return pl.BlockSpec((tm, a.shape[1]), lambda i: (i, 0))

    params = (w_out_a, w_out_b, g1, b1, w_up, conv_w, conv_b, w_down, g2, b2)
    param_specs = [_resident(w_out_a), _resident(w_out_b), _resident(g1), _resident(b1),
                   _resident(w_up, layer), _resident(conv_w), _resident(conv_b), _resident(w_down, layer),
                   _resident(g2), _resident(b2)]
    return pl.pallas_call(
        functools.partial(_tail_body, seq // tm, tc),
        grid=(t // tm,),
        in_specs=[tile(mix_a), tile(mix_b), tile(h)] + param_specs,
        out_specs=pl.BlockSpec((tm, D_MODEL), lambda i: (i, 0)),
        out_shape=jax.ShapeDtypeStruct((t, D_MODEL), F32),
        scratch_shapes=[pltpu.VMEM((tm, D_FF), BF16), pltpu.VMEM((HALO + tm, tc), F32),
                        pltpu.VMEM((HALO + tm, tc), F32), pltpu.VMEM((2 * D_FF // tc, HALO, tc), F32)],
        compiler_params=_cparams(("arbitrary",)),
        name="layer_tail",
    )(mix_a, mix_b, h, *params)


def _even_w_in(w):
    w_t = w.T
    a_end = 4 * EVA_PART
    gates = w_t[a_end:a_end + 2 * GDN_HEADS]
    qkv_b = w_t[a_end + 2 * GDN_HEADS:]
    pad = jnp.zeros((EVB_COLS - EVB_BA - 2 * GDN_HEADS, w.shape[0]), w.dtype)
    return w_t[:a_end].astype(BF16), jnp.concatenate([qkv_b, gates, pad], axis=0).astype(BF16)


def _odd_w_in(w):
    w_t = w.T
    pad = jnp.zeros((LANES - GLA_RANK, w.shape[0]), w.dtype)
    return w_t[:OD_COLS].astype(BF16), jnp.concatenate([w_t[OD_COLS:], pad], axis=0).astype(BF16)


def _even_mixer(h, rel_bias, w_in, conv_w, a_log, dt_bias, norm_w, w_out, bsz, seq):
    w_a, w_b = _even_w_in(w_in)
    proj_a, proj_b = _even_in_proj(h, w_a, w_b, conv_w, seq, PROJ_TM)
    gate_pad = jnp.zeros((1, LANES), F32)
    alog_pad = lax.dynamic_update_slice(gate_pad, a_log[None].astype(F32), (0, GDN_HEADS))
    dtb_pad = lax.dynamic_update_slice(gate_pad, dt_bias[None].astype(F32), (0, GDN_HEADS))
    o_a = _gdn(proj_a, proj_b, alog_pad, dtb_pad, norm_w[None], bsz, seq)
    tiles = _bias_tiles(rel_bias, *_swa_bias_tables(), scale=LOG2E)
    ng, two_blk = len(SWA_CONFIGS), 2 * SWA_BLOCK
    bias = tiles.reshape(SWA_HEADS // 2, 2, 2, ng, SWA_BLOCK, two_blk).transpose(3, 0, 2, 1, 4, 5)
    o_b = _swa(proj_b, bias.reshape(ng, SWA_HEADS // 2, 2, two_blk, two_blk), bsz, seq)
    w_out = w_out.astype(BF16)
    return o_a, o_b, w_out[:EVA_PART], w_out[EVA_PART:]


def _odd_mixer(h, rel_bias, w_in, lam_params, diff_norm_w, w_gate, b_gate, gla_norm_w, w_out,
               lam_init, bsz, seq):
    w_main, w_gd = _odd_w_in(w_in)
    proj, gate_in = _odd_in_proj(h, w_main, w_gd, PROJ_TM)
    blk = min(DIFF_BLOCK, seq)
    bucket, neg = _diff_bias_tables(blk)
    bias = _bias_tiles(rel_bias, bucket, neg, scale=LOG2E, base_bucket=NUM_BUCKETS - 1, toeplitz=True)
    o_c = _diff_attention(proj, bias, lam_params, diff_norm_w[None], lam_init, bsz, seq, blk)
    w_gate_pad = jnp.concatenate(
        [w_gate, jnp.zeros((LANES - GLA_RANK, w_gate.shape[1]), w_gate.dtype)], axis=0)
    o_d = _gla(proj, gate_in, w_gate_pad, b_gate[None], gla_norm_w[None], bsz, seq)
    diff_v = DIFF_HEADS * 2 * DIFF_DH
    w_out = w_out.astype(BF16)
    return o_c, o_d, w_out[:diff_v], w_out[diff_v:]


def kernel(x, rel_bias, w_in_even, gdn_conv_w, gdn_a_log, gdn_dt_bias, gdn_norm_w, w_out_even,
           w_in_odd, diff_lambda, diff_norm_w, gla_w_gate, gla_b_gate, gla_norm_w, w_out_odd,
           ffn_w_up, ffn_conv_w, ffn_conv_b, ffn_w_down, ln_g, ln_b):
    bsz, seq, d = x.shape
    h = x.reshape(bsz * seq, d)
    w_up, w_down = ffn_w_up.astype(BF16), ffn_w_down.astype(BF16)
    for layer in range(DEPTH):
        i = layer // 2
        if layer % 2 == 0:
            mixed = _even_mixer(h, rel_bias, w_in_even[i], gdn_conv_w[i], gdn_a_log[i], gdn_dt_bias[i],
                                gdn_norm_w[i], w_out_even[i], bsz, seq)
        else:
            lam_init = 0.8 - 0.6 * math.exp(-0.3 * layer)
            mixed = _odd_mixer(h, rel_bias, w_in_odd[i], diff_lambda[i], diff_norm_w[i], gla_w_gate[i],
                               gla_b_gate[i], gla_norm_w[i], w_out_odd[i], lam_init, bsz, seq)
        h = _layer_tail(*mixed[:2], h, *mixed[2:], ln_g[layer, 0][None], ln_b[layer, 0][None],
                        w_up, ffn_conv_w[layer], ffn_conv_b[layer][None], w_down,
                        ln_g[layer, 1][None], ln_b[layer, 1][None], layer, seq)
    return h.reshape(bsz, seq, d)
```

```python
import functools
import math

import numpy as np
import jax
import jax.numpy as jnp
from jax import lax
from jax.experimental import pallas as pl
from jax.experimental.pallas import tpu as pltpu

F32 = jnp.float32
BF16 = jnp.bfloat16
HI = lax.Precision.HIGHEST

D_MODEL = 1024
DEPTH = 2
DEEPNORM_ALPHA = (2 * DEPTH) ** 0.25
LN_EPS = 1e-5
RMS_EPS = 1e-6
NUM_BUCKETS = 32
REL_MAX_DIST = 2048
GDN_HEADS = 6
GDN_D = 128
CHUNK = 64
GDN_GROUP = 256
GDN_STEP_GROUPS = 2
SWA_CONFIGS = ((128, 1), (512, 4), (2048, 16))
SWA_HEADS = 4
SWA_DH = 64
SWA_BLOCK = 128
SWA_UNROLL = 8
SWA_SPAN = SWA_BLOCK * max(d for _, d in SWA_CONFIGS)
DIFF_HEADS = 4
DIFF_DH = 64
DIFF_BLOCK = 512
LOG2E = math.log2(math.e)
GLA_HEADS = 4
GLA_DK = 64
GLA_DV = 128
GLA_RANK = 16
GLA_TAU = 16.0
GLA_SUB = 16
GLA_GROUP = 256
GLA_STEP_GROUPS = 4
GLA_MAX_DECAY = 60.0
D_FF = 2816

LANES = 128
SUBLANES = 8
HALO = 8
VMEM_LIMIT = 56 * 1024 * 1024
NEG_BIG = -1e30

EVA_PART = GDN_HEADS * GDN_D
EVB_BA = 2304
EVB_COLS = 2560
EV_TN = 256
OD_QC, OD_KC, OD_VC = 0, 512, 1024
OD_QD, OD_KD, OD_VD, OD_RD = 1536, 1792, 2048, 2560
OD_COLS = 3072
OD_TN = 768
PROJ_TM = 512
TAIL_TM = 512
FFN_TC = 256


def _cparams(sem):
    return pltpu.CompilerParams(dimension_semantics=sem, vmem_limit_bytes=VMEM_LIMIT)


def _bdot(a, b):
    return jnp.dot(a.astype(BF16), b.astype(BF16), preferred_element_type=F32)


def _dot_nt(a, b):
    return lax.dot_general(a, b, (((1,), (1,)), ((), ())), preferred_element_type=F32)


def _bdot_nt(a, b):
    return lax.dot_general(a.astype(BF16), b.astype(BF16), (((1,), (1,)), ((), ())),
                           preferred_element_type=F32)


def _bdot_tn(a, b):
    return lax.dot_general(a.astype(BF16), b.astype(BF16), (((0,), (0,)), ((), ())),
                           preferred_element_type=F32)


def _sigmoid(x):
    return 1.0 / (1.0 + jnp.exp(-x))


def _silu(x):
    return x * _sigmoid(x)


def _softplus(x):
    return jnp.maximum(x, 0.0) + jnp.log1p(jnp.exp(-jnp.abs(x)))


def _log_sigmoid(x):
    return -_softplus(-x)


def _resident(a, lead=None):
    if lead is None:
        return pl.BlockSpec(a.shape, lambda i: (0,) * a.ndim, pipeline_mode=pl.Buffered(1))
    return pl.BlockSpec((None,) + a.shape[1:], lambda i: (lead,) + (0,) * (a.ndim - 1),
                        pipeline_mode=pl.Buffered(1))


def _odd_in_body(x_ref, w_ref, wg_ref, o_ref, og_ref):
    xb = x_ref[...].astype(BF16)
    for c in range(OD_COLS // OD_TN):
        cols = slice(c * OD_TN, (c + 1) * OD_TN)
        o_ref[:, cols] = _dot_nt(xb, w_ref[cols, :]).astype(o_ref.dtype)
    og_ref[...] = _dot_nt(xb, wg_ref[...])


def _odd_in_proj(x, w, w_gate_in, tm):
    t, k = x.shape
    return pl.pallas_call(
        _odd_in_body,
        grid=(t // tm,),
        in_specs=[pl.BlockSpec((tm, k), lambda i: (i, 0)), _resident(w), _resident(w_gate_in)],
        out_specs=[pl.BlockSpec((tm, OD_COLS), lambda i: (i, 0)), pl.BlockSpec((tm, LANES), lambda i: (i, 0))],
        out_shape=[jax.ShapeDtypeStruct((t, OD_COLS), BF16), jax.ShapeDtypeStruct((t, LANES), F32)],
        compiler_params=_cparams(("parallel",)),
        name="odd_in_proj",
    )(x, w, w_gate_in)


def _rel_bucket_np(dist):
    max_exact = NUM_BUCKETS // 2
    d = np.maximum(dist, 1).astype(np.float32)
    large = max_exact + (np.log(d / max_exact) / math.log(REL_MAX_DIST / max_exact)
                         * (NUM_BUCKETS - max_exact)).astype(np.int32)
    large = np.minimum(large, NUM_BUCKETS - 1)
    return np.where(dist < max_exact, dist, large).astype(np.int32)


def _bias_body(scale, base_bucket, tile_buckets, toeplitz, rb_ref, bucket_ref, neg_ref, o_ref):
    h = pl.program_id(0)
    t = pl.program_id(1)
    base = 0.0 if base_bucket is None else rb_ref[base_bucket, h]
    for tile, present in enumerate(tile_buckets):
        @pl.when(t == tile)
        def _(present=present):
            bucket = bucket_ref[0]
            acc = neg_ref[0]
            for b in present:
                acc = acc + jnp.where(bucket == b, (rb_ref[b, h] - base) * scale, 0.0)
            if toeplitz:
                n = o_ref.shape[2]
                rows = jnp.broadcast_to(acc[0:1], (n, 2 * n))
                acc = pltpu.roll(rows, n, 1, stride=1, stride_axis=0)[:, :n]
            o_ref[0, 0] = acc


def _bias_tiles(rel_bias, bucket, neg, scale=1.0, base_bucket=None, toeplitz=False):
    nt, r, c = bucket.shape
    nh = rel_bias.shape[1]
    out_r, out_c = (c // 2, c // 2) if toeplitz else (r, c)
    tile_buckets = tuple(tuple(int(b) for b in np.unique(bucket[t][neg[t] == 0]) if b != base_bucket)
                         for t in range(nt))
    return pl.pallas_call(
        functools.partial(_bias_body, scale, base_bucket, tile_buckets, toeplitz),
        grid=(nh, nt),
        in_specs=[pl.BlockSpec(memory_space=pltpu.SMEM),
                  pl.BlockSpec((1, r, c), lambda h, t: (t, 0, 0)),
                  pl.BlockSpec((1, r, c), lambda h, t: (t, 0, 0))],
        out_specs=pl.BlockSpec((1, 1, out_r, out_c), lambda h, t: (h, t, 0, 0)),
        out_shape=jax.ShapeDtypeStruct((nh, nt, out_r, out_c), F32),
        compiler_params=_cparams(("parallel", "parallel")),
        name="rel_bias_tiles",
    )(rel_bias, jnp.asarray(bucket), jnp.asarray(neg))


def _causal_conv(cur, w, scratch, carry, at_start):
    width = w.shape[0]
    rows = cur.shape[0]
    scratch[0:HALO, :] = jnp.where(at_start, 0.0, carry[...])
    scratch[HALO:, :] = cur
    carry[...] = cur[rows - HALO:]
    y = w[width - 1:width, :] * cur
    for j in range(width - 1):
        back = width - 1 - j
        y = y + w[j:j + 1, :] * scratch[HALO - back:HALO - back + rows, :]
    return y


def _even_in_body(tiles_per_seq, x_ref, wa_ref, wb_ref, cw_ref, oa_ref, ob_ref, conv_in):
    rows = x_ref.shape[0]
    width = cw_ref.shape[0]
    conv_cols = 3 * EVA_PART
    at_start = pl.program_id(0) % tiles_per_seq == 0
    conv_in[0:HALO, :] = jnp.where(at_start, 0.0, conv_in[rows:rows + HALO, :])
    x = x_ref[...].astype(BF16)

    def project_conv_in(c):
        cols = slice(c * EV_TN, (c + 1) * EV_TN)
        conv_in[HALO:, cols] = _dot_nt(x, wa_ref[cols, :])

    def project_z(c):
        cols = slice(conv_cols + c * EV_TN, conv_cols + (c + 1) * EV_TN)
        oa_ref[:, cols] = _dot_nt(x, wa_ref[cols, :]).astype(oa_ref.dtype)

    def project_b(c):
        cols = slice(c * EV_TN, (c + 1) * EV_TN)
        ob_ref[:, cols] = _dot_nt(x, wb_ref[cols, :])

    def finish_head(part, hd):
        cols = slice(part * EVA_PART + hd * GDN_D, part * EVA_PART + (hd + 1) * GDN_D)
        y = cw_ref[width - 1:width, cols] * conv_in[HALO:, cols]
        for j in range(width - 1):
            back = width - 1 - j
            y = y + cw_ref[j:j + 1, cols] * conv_in[HALO - back:HALO - back + rows, cols]
        ch = _silu(y)
        if part < 2:
            inv = lax.rsqrt(jnp.sum(ch * ch, axis=-1, keepdims=True) + RMS_EPS)
            ch = ch * (inv * GDN_D ** -0.5 if part == 0 else inv)
        oa_ref[:, cols] = ch.astype(oa_ref.dtype)

    plain = ([functools.partial(project_b, c) for c in range(EVB_COLS // EV_TN)]
             + [functools.partial(project_z, c) for c in range(EVA_PART // EV_TN)])
    heads_per_block = EV_TN // GDN_D
    n_conv = conv_cols // EV_TN
    project_conv_in(0)
    for c in range(n_conv):
        matmuls = ([functools.partial(project_conv_in, c + 1)] if c + 1 < n_conv else [])
        share = -(-len(plain) // (n_conv - c))
        matmuls, plain = matmuls + plain[:share], plain[share:]
        for i in range(max(len(matmuls), heads_per_block)):
            if i < len(matmuls):
                matmuls[i]()
            if i < heads_per_block:
                head = c * heads_per_block + i
                finish_head(head // GDN_HEADS, head % GDN_HEADS)


def _even_in_proj(x, w_a, w_b, conv_w, seq, tm):
    t, k = x.shape
    return pl.pallas_call(
        functools.partial(_even_in_body, seq // tm),
        grid=(t // tm,),
        in_specs=[pl.BlockSpec((tm, k), lambda i: (i, 0)),
                  _resident(w_a), _resident(w_b), _resident(conv_w)],
        out_specs=[pl.BlockSpec((tm, w_a.shape[0]), lambda i: (i, 0)),
                   pl.BlockSpec((tm, EVB_COLS), lambda i: (i, 0))],
        out_shape=[jax.ShapeDtypeStruct((t, w_a.shape[0]), BF16), jax.ShapeDtypeStruct((t, EVB_COLS), F32)],
        scratch_shapes=[pltpu.VMEM((HALO + tm, 3 * EVA_PART), F32)],
        compiler_params=_cparams(("arbitrary",)),
        name="even_in_proj",
    )(x, w_a, w_b, conv_w)


def _gdn_body(q_ref, k_ref, v_ref, z_ref, ba_ref, alog_ref, dtb_ref, nw_ref, o_ref, state):
    n = pl.program_id(1)

    @pl.when(n == 0)
    def _():
        state[...] = jnp.zeros_like(state)

    grp = GDN_GROUP
    nc = grp // CHUNK
    nh = GDN_HEADS
    hs = range(nh)
    ri = lax.broadcasted_iota(jnp.int32, (grp, grp), 0)
    ci = lax.broadcasted_iota(jnp.int32, (grp, grp), 1)
    same = (ri // CHUNK) == (ci // CHUNK)
    incl = same & (ri >= ci)
    strict = same & (ri > ci)
    eye = (ri == ci).astype(F32)
    tri = incl.astype(BF16)

    def chunk_local(groups):
        units = [(g, h) for g in range(len(groups)) for h in hs]
        gam_all, gam_rows, beta_all = [], [], []
        for rows in groups:
            ba = ba_ref[rows, :]
            beta_all.append(_sigmoid(ba))
            g_all = -jnp.exp(alog_ref[...]) * _softplus(ba + dtb_ref[...])
            g_hi = g_all.astype(BF16)
            rem = g_all - g_hi.astype(F32)
            g_mid = rem.astype(BF16)
            g_lo = (rem - g_mid.astype(F32)).astype(BF16)
            gam_all.append(jnp.dot(tri, g_hi, preferred_element_type=F32)
                           + jnp.dot(tri, g_mid, preferred_element_type=F32)
                           + jnp.dot(tri, g_lo, preferred_element_type=F32))
            gam_rows.append(gam_all[-1].T)

        def head(ref, g, h):
            return ref[groups[g], h * GDN_D:(h + 1) * GDN_D]

        q = [head(q_ref, g, h) for g, h in units]
        k = [head(k_ref, g, h) for g, h in units]
        v = [head(v_ref, g, h) for g, h in units]
        gam = [jnp.broadcast_to(gam_all[g][:, nh + h:nh + h + 1], (grp, GDN_D)) for g, h in units]
        bcol = [beta_all[g][:, h:h + 1] for g, h in units]
        us = range(len(units))
        decay, x, inv = [], [], []
        for i, (g, h) in enumerate(units):
            diff = gam[i][:, 0:1] - gam_rows[g][nh + h:nh + h + 1, :]
            decay.append(jnp.where(incl, jnp.exp(jnp.where(incl, diff, 0.0)), 0.0))
            kk = _bdot_nt(k[i], k[i])
            x.append(-jnp.where(strict, bcol[i] * kk * decay[i], 0.0))
            inv.append(eye + x[i])
        for _ in range(5):
            for i in us:
                x[i] = _bdot(x[i], x[i])
                inv[i] = inv[i] + _bdot(inv[i], x[i])
        out = [tuple([] for _ in range(6)) for _ in groups]
        for i, (g, h) in enumerate(units):
            eg = jnp.exp(gam[i])
            uw = _bdot(inv[i], jnp.concatenate([v[i] * bcol[i], k[i] * (bcol[i] * eg)], axis=1))
            kd, gl = [], []
            for c in range(nc):
                last = gam[i][(c + 1) * CHUNK - 1:(c + 1) * CHUNK, :]
                kd.append(k[i][c * CHUNK:(c + 1) * CHUNK] * jnp.exp(last - gam[i][c * CHUNK:(c + 1) * CHUNK]))
                gl.append(jnp.exp(last))
            for lst, val in zip(out[g], (uw[:, :GDN_D], uw[:, GDN_D:], _bdot_nt(q[i], k[i]) * decay[i],
                                         q[i] * eg, kd, gl)):
                lst.append(val)
        return out

    def recurrence(rows, local, s):
        u, w, qk, q_dec, k_dec, g_last = local
        q_s = [[] for _ in hs]
        delta = [[] for _ in hs]
        for c in range(nc):
            sl = slice(c * CHUNK, (c + 1) * CHUNK)
            for h in hs:
                r = _bdot(jnp.concatenate([w[h][sl], q_dec[h][sl]], axis=0), s[h])
                d = u[h][sl] - r[:CHUNK]
                q_s[h].append(r[CHUNK:])
                delta[h].append(d)
                s[h] = g_last[h][c] * s[h] + _bdot_tn(k_dec[h][c], d)
        for h in hs:
            o = jnp.concatenate(q_s[h], axis=0) + _bdot(qk[h], jnp.concatenate(delta[h], axis=0))
            o = o * lax.rsqrt(jnp.mean(o * o, axis=-1, keepdims=True) + RMS_EPS) * nw_ref[...]
            z = z_ref[rows, h * GDN_D:(h + 1) * GDN_D].astype(F32)
            o_ref[rows, h * GDN_D:(h + 1) * GDN_D] = (o * _silu(z)).astype(o_ref.dtype)
        return s

    groups = [slice(g * grp, (g + 1) * grp) for g in range(GDN_STEP_GROUPS)]
    s = [state[h] for h in hs]
    for rows, local in zip(groups, chunk_local(groups)):
        s = recurrence(rows, local, s)
    for h in hs:
        state[h] = s[h]


def _gdn(proj_a, proj_b, alog_pad, dtb_pad, norm_w, bsz, seq):
    t = proj_a.shape[0]
    rows = GDN_GROUP * GDN_STEP_GROUPS
    spb = seq // rows
    width = GDN_HEADS * GDN_D

    def at(col):
        return lambda b, n: (b * spb + n, col)

    return pl.pallas_call(
        _gdn_body,
        grid=(bsz, spb),
        in_specs=[pl.BlockSpec((rows, width), at(0)),
                  pl.BlockSpec((rows, width), at(1)),
                  pl.BlockSpec((rows, width), at(2)),
                  pl.BlockSpec((rows, width), at(3)),
                  pl.BlockSpec((rows, LANES), at(EVB_BA // LANES)),
                  pl.BlockSpec((1, LANES), lambda b, n: (0, 0)),
                  pl.BlockSpec((1, LANES), lambda b, n: (0, 0)),
                  pl.BlockSpec((1, LANES), lambda b, n: (0, 0))],
        out_specs=pl.BlockSpec((rows, width), at(0)),
        out_shape=jax.ShapeDtypeStruct((t, width), BF16),
        scratch_shapes=[pltpu.VMEM((GDN_HEADS, GDN_D, GDN_D), F32)],
        compiler_params=_cparams(("parallel", "arbitrary")),
        name="gdn",
    )(proj_a, proj_a, proj_a, proj_a, proj_b, alog_pad, dtb_pad, norm_w)


def _swa_bias_tables():
    qi = np.arange(SWA_BLOCK)[:, None] + SWA_BLOCK
    kj = np.arange(2 * SWA_BLOCK)[None, :]
    rel = qi - kj
    buckets, negs = [], []
    for has_prev in (True, False):
        for window, dilation in SWA_CONFIGS:
            valid = (rel >= 0) & (rel <= window // dilation) & (has_prev | (kj >= SWA_BLOCK))
            buckets.append(_rel_bucket_np(np.maximum(rel, 0) * dilation))
            negs.append(np.where(valid, 0.0, NEG_BIG))
    return np.stack(buckets).astype(np.int32), np.stack(negs).astype(np.float32)


def _swa_body(*refs):
    ng = len(SWA_CONFIGS)
    ins, bias_ref, o_ref = refs[:5 * ng], refs[5 * ng], refs[5 * ng + 1]
    scratch = refs[5 * ng + 2:]
    kbufs, vbufs, o_scr, lse_scr = scratch[:ng], scratch[ng:2 * ng], scratch[2 * ng], scratch[2 * ng + 1]
    j = pl.program_id(1)
    blk = SWA_BLOCK
    first_head = lax.broadcasted_iota(jnp.int32, (blk, LANES), 1) < SWA_DH

    for g, (_, d) in enumerate(SWA_CONFIGS):
        q_ref, kc_ref, kp_ref, vc_ref, vp_ref = ins[5 * g:5 * g + 5]
        kbuf, vbuf = kbufs[g], vbufs[g]
        halo = blk * d
        kbuf[0:halo, :] = kp_ref[...]
        kbuf[halo:, :] = kc_ref[...]
        vbuf[0:halo, :] = vp_ref[...]
        vbuf[halo:, :] = vc_ref[...]

        def unit(u, carry, g=g, d=d, halo=halo, q_ref=q_ref, kbuf=kbuf, vbuf=vbuf):
            base = (u // d) * halo + u % d
            q = q_ref[pl.ds(base, blk, stride=d), :] * (SWA_DH ** -0.5 * LOG2E)
            k = kbuf[pl.ds(base, 2 * blk, stride=d), :]
            v = vbuf[pl.ds(base, 2 * blk, stride=d), :]
            lhs = jnp.concatenate([jnp.where(first_head, q, 0.0), jnp.where(first_head, 0.0, q)], axis=0)
            no_prev = jnp.where((j == 0) & (u < d), 1, 0)
            s = _bdot_nt(lhs, k) + bias_ref[g, 0, no_prev]
            m = jnp.max(s, axis=-1, keepdims=True)
            p = jnp.exp2(s - m)
            l = jnp.sum(p, axis=-1, keepdims=True)
            o2 = _bdot(p, v) * (1.0 / l)
            lse2 = m + jnp.log2(l)
            o_scr[g, pl.ds(base, blk, stride=d), :] = jnp.where(first_head, o2[:blk], o2[blk:])
            lse_scr[g, pl.ds(base, blk, stride=d), :] = jnp.where(first_head, lse2[:blk], lse2[blk:])
            return carry

        lax.fori_loop(0, SWA_SPAN // blk, unit, 0, unroll=SWA_UNROLL)

    def combine(c, carry):
        rows = pl.ds(pl.multiple_of(c * 2 * blk, 2 * blk), 2 * blk)
        lse = [lse_scr[g, rows, :] for g in range(ng)]
        m = functools.reduce(jnp.maximum, lse)
        e = [jnp.exp2(x - m) for x in lse]
        den = functools.reduce(lambda x, y: x + y, e)
        o_ref[rows, :] = functools.reduce(
            lambda x, y: x + y, [(e[g] / den) * o_scr[g, rows, :] for g in range(ng)]).astype(o_ref.dtype)
        return carry

    lax.fori_loop(0, SWA_SPAN // (2 * blk), combine, 0)


def _swa(proj, bias, bsz, seq):
    t = proj.shape[0]
    ng = len(SWA_CONFIGS)
    nspan = seq // SWA_SPAN
    npair = SWA_HEADS * SWA_DH // LANES
    group_cols = SWA_HEADS * SWA_DH // LANES
    in_specs, scratch_k = [], []
    for g, (_, d) in enumerate(SWA_CONFIGS):
        halo = SWA_BLOCK * d
        per_span = SWA_SPAN // halo

        def cur(which, g=g):
            col = (which * ng + g) * group_cols
            return lambda b, j, p: (b * nspan + j, col + p)

        def prev(which, g=g, per_span=per_span):
            col = (which * ng + g) * group_cols
            return lambda b, j, p: (jnp.maximum((b * nspan + j) * per_span - 1, 0), col + p)

        in_specs += [pl.BlockSpec((SWA_SPAN, LANES), cur(0)),
                     pl.BlockSpec((SWA_SPAN, LANES), cur(1)), pl.BlockSpec((halo, LANES), prev(1)),
                     pl.BlockSpec((SWA_SPAN, LANES), cur(2)), pl.BlockSpec((halo, LANES), prev(2))]
        scratch_k.append(pltpu.VMEM((halo + SWA_SPAN, LANES), F32))
    in_specs.append(pl.BlockSpec((ng, 1, 2, 2 * SWA_BLOCK, 2 * SWA_BLOCK), lambda b, j, p: (0, p, 0, 0, 0)))
    return pl.pallas_call(
        _swa_body,
        grid=(bsz, nspan, npair),
        in_specs=in_specs,
        out_specs=pl.BlockSpec((SWA_SPAN, LANES), lambda b, j, p: (b * nspan + j, p)),
        out_shape=jax.ShapeDtypeStruct((t, npair * LANES), BF16),
        scratch_shapes=scratch_k + scratch_k + [pltpu.VMEM((ng, SWA_SPAN, LANES), F32)] * 2,
        compiler_params=_cparams(("parallel", "parallel", "parallel")),
        name="swa",
    )(*([proj] * (5 * ng)), bias)


def _layer_norm(y, g, b):
    mu = jnp.mean(y, axis=-1, keepdims=True)
    yc = y - mu
    var = jnp.mean(yc * yc, axis=-1, keepdims=True)
    return yc * lax.rsqrt(var + LN_EPS) * g + b


def _diff_bias_tables(blk):
    buckets = _rel_bucket_np(np.arange(2 * REL_MAX_DIST))
    far = int(np.max(np.nonzero(buckets != NUM_BUCKETS - 1)[0])) + 1
    nb = -(-(far + blk - 1) // blk)
    dist = np.stack([t * blk - blk + np.arange(2 * blk) for t in range(nb + 1)])[:, None, :]
    dist = np.broadcast_to(dist, (nb + 1, SUBLANES, 2 * blk))
    bucket = _rel_bucket_np(np.maximum(dist, 0))
    neg = np.where(dist >= 0, 0.0, NEG_BIG).astype(np.float32)
    return bucket, neg


def _diff_body(blk, nb, lam_init, q_ref, qn_ref, k_ref, v_ref, bias_ref, lam_ref, nw_ref, o_ref,
               vt, acc1, acc2, s_a, s_b):
    qi = pl.program_id(2)
    dv = 2 * DIFF_DH
    seq = k_ref.shape[0]

    @pl.when(qi == 0)
    def _():
        vt[dv:, :] = jnp.ones((vt.shape[0] - dv, seq), BF16)

        def fill(c, carry):
            st = pl.multiple_of(c * blk, blk)
            vt[0:dv, pl.ds(st, blk)] = v_ref[pl.ds(st, blk), :].astype(F32).T.astype(BF16)
            return carry

        lax.fori_loop(0, seq // blk, fill, 0)

    feature = lax.broadcasted_iota(jnp.int32, (dv, blk), 0)

    def components(ref):
        q_t = (ref[...].astype(F32) * (DIFF_DH ** -0.5 * LOG2E)).T
        return (jnp.where(feature < DIFF_DH, q_t, 0.0).astype(BF16),
                jnp.where(feature >= DIFF_DH, q_t, 0.0).astype(BF16))

    qs = components(q_ref)
    qs_next = components(qn_ref)
    accs = (acc1, acc2)
    for acc in accs:
        acc[...] = jnp.zeros_like(acc)

    last = pl.num_programs(2) - 1

    def key_rows(kj):
        return pl.ds(pl.multiple_of(jnp.minimum(kj, last) * blk, blk), blk)

    def scores(kj, dst, queries=qs, offset=None):
        k = k_ref[key_rows(kj), :]
        for c, qc in enumerate(queries):
            s = jnp.dot(k, qc, preferred_element_type=F32)
            dst[c] = s if offset is None else s + bias_ref[0, jnp.minimum(offset, nb)]

    def consume(kj, src, ms):
        vtb = vt[:, key_rows(kj)]
        out = []
        for c, (m, acc) in enumerate(zip(ms, accs)):
            s = src[c]
            m_new = jnp.maximum(m, jnp.max(s, axis=0, keepdims=True))
            p = jnp.exp2(s - m_new).astype(BF16)
            acc[...] = jnp.exp2(m - m_new) * acc[...] + jnp.dot(vtb, p, preferred_element_type=F32)
            out.append(m_new)
        return tuple(out)

    nblocks = qi + 1
    trips = nblocks // 2
    odd = nblocks % 2 == 1
    far_trips = jnp.maximum(qi - nb, 0) // 2

    def pair(biased, t, ms):
        kj = 2 * t
        scores(kj + 1, s_b, offset=qi - (kj + 1) if biased else None)
        ms = consume(kj, s_a, ms)
        if biased:
            hand_off = (t == trips - 1) & jnp.logical_not(odd)
            scores(jnp.where(hand_off, 0, kj + 2), s_a,
                   tuple(jnp.where(hand_off, qn, qc) for qc, qn in zip(qs, qs_next)),
                   offset=jnp.where(hand_off, qi + 1, qi - (kj + 2)))
        else:
            scores(kj + 2, s_a)
        return consume(kj + 1, s_b, ms)

    @pl.when(qi == 0)
    def _():
        scores(0, s_a, offset=0)

    m0 = jnp.full((1, blk), NEG_BIG, F32)
    ms = lax.fori_loop(0, far_trips, functools.partial(pair, False), (m0, m0))
    ms = lax.fori_loop(far_trips, trips, functools.partial(pair, True), ms)

    def finish():
        lp = lam_ref[...]
        lam = (jnp.exp(jnp.sum(lp[0:1] * lp[1:2], axis=-1, keepdims=True))
               - jnp.exp(jnp.sum(lp[2:3] * lp[3:4], axis=-1, keepdims=True)) + lam_init)
        a1, a2 = acc1[...], acc2[...]
        o_t = a1[:dv] * (1.0 / a1[dv:dv + 1]) - a2[:dv] * (lam / a2[dv:dv + 1])
        o = o_t.T
        o = o * lax.rsqrt(jnp.mean(o * o, axis=-1, keepdims=True) + RMS_EPS) * nw_ref[...]
        o_ref[...] = (o * (1.0 - lam_init)).astype(o_ref.dtype)

    @pl.when(odd)
    def _():
        consume(qi, s_a, ms)
        scores(0, s_a, qs_next, offset=qi + 1)
        finish()

    @pl.when(jnp.logical_not(odd))
    def _():
        finish()


def _diff_attention(proj, bias, lam_params, norm_w, lam_init, bsz, seq, blk):
    t = proj.shape[0]
    nq = seq // blk
    nt = bias.shape[1]
    nb = nt - 1
    width = 2 * DIFF_DH
    ones_rows = 2 * SUBLANES
    return pl.pallas_call(
        functools.partial(_diff_body, blk, nb, lam_init),
        grid=(bsz, DIFF_HEADS, nq),
        in_specs=[pl.BlockSpec((blk, width), lambda b, h, i: (b * nq + i, OD_QC // width + h)),
                  pl.BlockSpec((blk, width),
                               lambda b, h, i: (b * nq + jnp.minimum(i + 1, nq - 1), OD_QC // width + h)),
                  pl.BlockSpec((seq, width), lambda b, h, i: (b, OD_KC // width + h)),
                  pl.BlockSpec((seq, width), lambda b, h, i: (b, OD_VC // width + h)),
                  pl.BlockSpec((1, nt, blk, blk), lambda b, h, i: (h, 0, 0, 0)),
                  pl.BlockSpec((4, DIFF_DH), lambda b, h, i: (0, 0)),
                  pl.BlockSpec((1, width), lambda b, h, i: (0, 0))],
        out_specs=pl.BlockSpec((blk, width), lambda b, h, i: (b * nq + i, h)),
        out_shape=jax.ShapeDtypeStruct((t, DIFF_HEADS * width), BF16),
        scratch_shapes=[pltpu.VMEM((width + ones_rows, seq), BF16),
                        pltpu.VMEM((width + ones_rows, blk), F32), pltpu.VMEM((width + ones_rows, blk), F32),
                        pltpu.VMEM((2, blk, blk), F32), pltpu.VMEM((2, blk, blk), F32)],
        compiler_params=_cparams(("arbitrary", "arbitrary", "arbitrary")),
        name="diff_attn",
    )(proj, proj, proj, proj, bias, lam_params, norm_w)


def _gla_body(q_ref, k_ref, gd_ref, v_ref, r_ref, wg_ref, bg_ref, nw_ref, o_ref, state, part):
    n = pl.program_id(1)

    @pl.when(n == 0)
    def _():
        state[...] = jnp.zeros_like(state)

    rows = GLA_GROUP
    nc = rows // CHUNK
    npair = GLA_HEADS // 2
    groups = [slice(g * rows, (g + 1) * rows) for g in range(GLA_STEP_GROUPS)]
    gate = jnp.dot(gd_ref[...], wg_ref[...], precision=HI, preferred_element_type=F32) + bg_ref[...]
    log_a = _log_sigmoid(gate) * (1.0 / GLA_TAU)

    ri = lax.broadcasted_iota(jnp.int32, (rows, rows), 0)
    ci = lax.broadcasted_iota(jnp.int32, (rows, rows), 1)
    causal = ((ri // CHUNK) == (ci // CHUNK)) & (ri >= ci)
    tri = causal.astype(BF16)
    hi = log_a.astype(BF16)
    rem = log_a - hi.astype(F32)
    mid = rem.astype(BF16)
    lo = (rem - mid.astype(F32)).astype(BF16)
    b_all = [jnp.dot(tri, hi[gr], preferred_element_type=F32) + jnp.dot(tri, mid[gr], preferred_element_type=F32)
             + jnp.dot(tri, lo[gr], preferred_element_type=F32) for gr in groups]

    lane = lax.broadcasted_iota(jnp.int32, (rows, LANES), 1)
    lane_c = lax.broadcasted_iota(jnp.int32, (CHUNK, LANES), 1)
    head_lanes = (lane < GLA_DK, lane >= GLA_DK)
    chunk_lanes = (lane_c < GLA_DK, lane_c >= GLA_DK)
    pairs = []
    for gr, b_g in zip(groups, b_all):
        pairs.append([])
        for p in range(npair):
            cols = slice(p * LANES, (p + 1) * LANES)
            q = q_ref[gr, cols].astype(F32) * GLA_DK ** -0.5
            pairs[-1].append(dict(
                b=b_g[:, cols], q=q, q_dec=q * jnp.exp(b_g[:, cols]), k=k_ref[gr, cols].astype(F32),
                v=[v_ref[gr, (2 * p + hd) * GLA_DV:(2 * p + hd + 1) * GLA_DV].astype(F32) for hd in range(2)]))

    for g, gr in enumerate(groups):
        for c in range(nc):
            sl = slice(c * CHUNK, (c + 1) * CHUNK)
            out_rows = slice(g * rows + c * CHUNK, g * rows + (c + 1) * CHUNK)
            for p, pr in enumerate(pairs[g]):
                bc = pr["b"][sl]
                b_last = bc[CHUNK - 1:CHUNK]
                k_dec = pr["k"][sl] * jnp.exp(b_last - bc)
                e_last = jnp.exp(b_last)
                for hd in range(2):
                    h = 2 * p + hd
                    st = state[h]
                    part[h, out_rows, :] = _bdot_nt(jnp.where(chunk_lanes[hd], pr["q_dec"][sl], 0.0), st)
                    state[h] = st * e_last + _bdot_tn(pr["v"][hd][sl], k_dec)

    def finish(intra):
        for gr, intra_g in zip(groups, intra):
            for h in range(GLA_HEADS):
                o = part[h, gr, :] + intra_g[h]
                o = o * lax.rsqrt(jnp.mean(o * o, axis=-1, keepdims=True) + RMS_EPS) * nw_ref[...]
                gate_r = _silu(r_ref[gr, h * GLA_DV:(h + 1) * GLA_DV].astype(F32))
                o_ref[gr, h * GLA_DV:(h + 1) * GLA_DV] = (o * gate_r).astype(o_ref.dtype)

    def intra_whole_chunk():
        out = []
        for pairs_g in pairs:
            out.append([])
            for pr in pairs_g:
                k_inv = pr["k"] * jnp.exp(jnp.minimum(-pr["b"], GLA_MAX_DECAY))
                for hd in range(2):
                    a = _bdot_nt(jnp.where(head_lanes[hd], pr["q_dec"], 0.0), k_inv)
                    out[-1].append(_bdot(jnp.where(causal, a, 0.0), pr["v"][hd]))
        return out

    def intra_exact():
        out = []
        for pairs_g in pairs:
            out.append([])
            for pr in pairs_g:
                off = _gla_intra_off_diagonal(pr["q"], pr["k"], pr["b"], pr["v"], nc)
                diag = _gla_intra_diagonal(pr["q"], pr["k"], pr["b"], pr["v"])
                out[-1] += [off[hd] + diag[hd] for hd in range(2)]
        return out

    chunk_decay = functools.reduce(jnp.maximum, [
        jnp.max(-b_g.reshape(nc, CHUNK, npair * LANES)[:, CHUNK - 1:CHUNK, :]) for b_g in b_all])

    @pl.when(chunk_decay <= GLA_MAX_DECAY)
    def _():
        finish(intra_whole_chunk())

    @pl.when(chunk_decay > GLA_MAX_DECAY)
    def _():
        finish(intra_exact())


def _gla_intra_off_diagonal(q, k, b, vs, nc):
    per_chunk = CHUNK // GLA_SUB
    lane = lax.broadcasted_iota(jnp.int32, (GLA_SUB, LANES), 1)
    sub_mask = (lane < GLA_DK, lane >= GLA_DK)
    kcol = lax.broadcasted_iota(jnp.int32, (GLA_SUB, CHUNK), 1)
    outs = ([], [])
    for c in range(nc):
        sl = slice(c * CHUNK, (c + 1) * CHUNK)
        bc, qc, kc = b[sl], q[sl], k[sl]
        a_rows = [[jnp.zeros((GLA_SUB, CHUNK), F32)] for _ in vs]
        for blk in range(1, per_chunk):
            r0 = blk * GLA_SUB
            bref = bc[r0:r0 + 1]
            qs = qc[r0:r0 + GLA_SUB] * jnp.exp(bc[r0:r0 + GLA_SUB] - bref)
            ks = kc * jnp.exp(jnp.minimum(bref - bc, 0.0))
            for hd in range(2):
                a = _bdot_nt(jnp.where(sub_mask[hd], qs, 0.0), ks)
                a_rows[hd].append(jnp.where(kcol < r0, a, 0.0))
        for hd in range(2):
            outs[hd].append(_bdot(jnp.concatenate(a_rows[hd], axis=0), vs[hd][sl]))
    return [jnp.concatenate(o, axis=0) for o in outs]


def _gla_intra_diagonal(q, k, b, vs):
    rows = q.shape[0]
    nsub = rows // GLA_SUB
    b3 = b.reshape(nsub, GLA_SUB, LANES)
    q3 = q.reshape(nsub, GLA_SUB, LANES)
    k3 = k.reshape(nsub, GLA_SUB, LANES)
    v3 = [v.reshape(nsub, GLA_SUB, LANES) for v in vs]
    row3 = lax.broadcasted_iota(jnp.int32, (nsub, GLA_SUB, LANES), 1)
    lane3 = lax.broadcasted_iota(jnp.int32, (nsub, GLA_SUB, LANES), 2)
    rowc = lax.broadcasted_iota(jnp.int32, (nsub, GLA_SUB, 1), 1)
    o3 = [jnp.zeros((nsub, GLA_SUB, LANES), F32) for _ in vs]
    for jj in range(GLA_SUB):
        e = jnp.exp(jnp.where(row3 >= jj, b3 - b3[:, jj:jj + 1, :], 0.0))
        t = q3 * k3[:, jj:jj + 1, :] * e
        w_all = jnp.sum(t, axis=-1, keepdims=True)
        w_a = jnp.sum(jnp.where(lane3 < GLA_DK, t, 0.0), axis=-1, keepdims=True)
        for hd, w in enumerate((w_a, w_all - w_a)):
            o3[hd] = o3[hd] + jnp.where(rowc >= jj, w, 0.0) * v3[hd][:, jj:jj + 1, :]
    return [o.reshape(rows, LANES) for o in o3]


def _gla(proj, gate_in, w_gate_pad, b_gate, norm_w, bsz, seq):
    t = proj.shape[0]
    rows = GLA_GROUP * GLA_STEP_GROUPS
    spb = seq // rows
    qk_w = GLA_HEADS * GLA_DK
    v_w = GLA_HEADS * GLA_DV

    def at(col):
        return lambda b, n: (b * spb + n, col)

    def whole(a):
        return pl.BlockSpec(a.shape, lambda b, n: (0, 0))

    return pl.pallas_call(
        _gla_body,
        grid=(bsz, spb),
        in_specs=[pl.BlockSpec((rows, qk_w), at(OD_QD // qk_w)),
                  pl.BlockSpec((rows, qk_w), at(OD_KD // qk_w)),
                  pl.BlockSpec((rows, LANES), at(0)),
                  pl.BlockSpec((rows, v_w), at(OD_VD // v_w)),
                  pl.BlockSpec((rows, v_w), at(OD_RD // v_w)),
                  whole(w_gate_pad), whole(b_gate), whole(norm_w)],
        out_specs=pl.BlockSpec((rows, v_w), at(0)),
        out_shape=jax.ShapeDtypeStruct((t, v_w), BF16),
        scratch_shapes=[pltpu.VMEM((GLA_HEADS, GLA_DV, LANES), F32), pltpu.VMEM((GLA_HEADS, rows, GLA_DV), F32)],
        compiler_params=_cparams(("parallel", "arbitrary")),
        name="gla",
    )(proj, proj, gate_in, proj, proj, w_gate_pad, b_gate, norm_w)


def _tail_body(tiles_per_seq, tc, ma_ref, mb_ref, h_ref, woa_ref, wob_ref,
               g1_ref, b1_ref, wu_ref, cw_ref, cb_ref, wd_ref, g2_ref, b2_ref, y_ref,
               act, scratch_g, scratch_v, carry):
    at_start = pl.program_id(0) % tiles_per_seq == 0
    mix = (jnp.dot(ma_ref[...], woa_ref[...], preferred_element_type=F32)
           + jnp.dot(mb_ref[...], wob_ref[...], preferred_element_type=F32))
    x = _layer_norm(DEEPNORM_ALPHA * h_ref[...] + mix, g1_ref[...], b1_ref[...])
    xb = x.astype(BF16)
    nchunk = D_FF // tc

    def branch(idx, scratch):
        lo = idx * tc
        cur = jnp.dot(xb, wu_ref[:, lo:lo + tc], preferred_element_type=F32)
        return _causal_conv(cur, cw_ref[:, lo:lo + tc], scratch, carry.at[idx], at_start) + cb_ref[:, lo:lo + tc]

    for c in range(nchunk):
        gate = branch(c, scratch_g)
        val = branch(nchunk + c, scratch_v)
        act[:, c * tc:(c + 1) * tc] = (_silu(gate) * val).astype(BF16)
    ffn = jnp.dot(act[...], wd_ref[...], preferred_element_type=F32)
    y_ref[...] = _layer_norm(DEEPNORM_ALPHA * x + ffn, g2_ref[...], b2_ref[...])


def _layer_tail(mix_a, mix_b, h, w_out_a, w_out_b, g1, b1, w_up, conv_w, conv_b, w_down, g2, b2,
                layer, seq):
    t = h.shape[0]
    tm, tc = TAIL_TM, FFN_TC

    def tile(a):
        return pl.BlockSpec((tm, a.shape[1]), lambda i: (i, 0))

    params = (w_out_a, w_out_b, g1, b1, w_up, conv_w, conv_b, w_down, g2, b2)
    param_specs = [_resident(w_out_a), _resident(w_out_b), _resident(g1), _resident(b1),
                   _resident(w_up, layer), _resident(conv_w), _resident(conv_b), _resident(w_down, layer),
                   _resident(g2), _resident(b2)]
    return pl.pallas_call(
        functools.partial(_tail_body, seq // tm, tc),
        grid=(t // tm,),
        in_specs=[tile(mix_a), tile(mix_b), tile(h)] + param_specs,
        out_specs=pl.BlockSpec((tm, D_MODEL), lambda i: (i, 0)),
        out_shape=jax.ShapeDtypeStruct((t, D_MODEL), F32),
        scratch_shapes=[pltpu.VMEM((tm, D_FF), BF16), pltpu.VMEM((HALO + tm, tc), F32),
                        pltpu.VMEM((HALO + tm, tc), F32), pltpu.VMEM((2 * D_FF // tc, HALO, tc), F32)],
        compiler_params=_cparams(("arbitrary",)),
        name="layer_tail",
    )(mix_a, mix_b, h, *params)


def _even_w_in(w):
    w_t = w.T
    a_end = 4 * EVA_PART
    gates = w_t[a_end:a_end + 2 * GDN_HEADS]
    qkv_b = w_t[a_end + 2 * GDN_HEADS:]
    pad = jnp.zeros((EVB_COLS - EVB_BA - 2 * GDN_HEADS, w.shape[0]), w.dtype)
    return w_t[:a_end].astype(BF16), jnp.concatenate([qkv_b, gates, pad], axis=0).astype(BF16)


def _odd_w_in(w):
    w_t = w.T
    pad = jnp.zeros((LANES - GLA_RANK, w.shape[0]), w.dtype)
    return w_t[:OD_COLS].astype(BF16), jnp.concatenate([w_t[OD_COLS:], pad], axis=0).astype(BF16)


def _even_mixer(h, rel_bias, w_in, conv_w, a_log, dt_bias, norm_w, w_out, bsz, seq):
    w_a, w_b = _even_w_in(w_in)
    proj_a, proj_b = _even_in_proj(h, w_a, w_b, conv_w, seq, PROJ_TM)
    gate_pad = jnp.zeros((1, LANES), F32)
    alog_pad = lax.dynamic_update_slice(gate_pad, a_log[None].astype(F32), (0, GDN_HEADS))
    dtb_pad = lax.dynamic_update_slice(gate_pad, dt_bias[None].astype(F32), (0, GDN_HEADS))
    o_a = _gdn(proj_a, proj_b, alog_pad, dtb_pad, norm_w[None], bsz, seq)
    tiles = _bias_tiles(rel_bias, *_swa_bias_tables(), scale=LOG2E)
    ng, two_blk = len(SWA_CONFIGS), 2 * SWA_BLOCK
    bias = tiles.reshape(SWA_HEADS // 2, 2, 2, ng, SWA_BLOCK, two_blk).transpose(3, 0, 2, 1, 4, 5)
    o_b = _swa(proj_b, bias.reshape(ng, SWA_HEADS // 2, 2, two_blk, two_blk), bsz, seq)
    w_out = w_out.astype(BF16)
    return o_a, o_b, w_out[:EVA_PART], w_out[EVA_PART:]


def _odd_mixer(h, rel_bias, w_in, lam_params, diff_norm_w, w_gate, b_gate, gla_norm_w, w_out,
               lam_init, bsz, seq):
    w_main, w_gd = _odd_w_in(w_in)
    proj, gate_in = _odd_in_proj(h, w_main, w_gd, PROJ_TM)
    blk = min(DIFF_BLOCK, seq)
    bucket, neg = _diff_bias_tables(blk)
    bias = _bias_tiles(rel_bias, bucket, neg, scale=LOG2E, base_bucket=NUM_BUCKETS - 1, toeplitz=True)
    o_c = _diff_attention(proj, bias, lam_params, diff_norm_w[None], lam_init, bsz, seq, blk)
    w_gate_pad = jnp.concatenate(
        [w_gate, jnp.zeros((LANES - GLA_RANK, w_gate.shape[1]), w_gate.dtype)], axis=0)
    o_d = _gla(proj, gate_in, w_gate_pad, b_gate[None], gla_norm_w[None], bsz, seq)
    diff_v = DIFF_HEADS * 2 * DIFF_DH
    w_out = w_out.astype(BF16)
    return o_c, o_d, w_out[:diff_v], w_out[diff_v:]


def kernel(x, rel_bias, w_in_even, gdn_conv_w, gdn_a_log, gdn_dt_bias, gdn_norm_w, w_out_even,
           w_in_odd, diff_lambda, diff_norm_w, gla_w_gate, gla_b_gate, gla_norm_w, w_out_odd,
           ffn_w_up, ffn_conv_w, ffn_conv_b, ffn_w_down, ln_g, ln_b):
    bsz, seq, d = x.shape
    h = x.reshape(bsz * seq, d)
    w_up, w_down = ffn_w_up.astype(BF16), ffn_w_down.astype(BF16)
    for layer in range(DEPTH):
        i = layer // 2
        if layer % 2 == 0:
            mixed = _even_mixer(h, rel_bias, w_in_even[i], gdn_conv_w[i], gdn_a_log[i], gdn_dt_bias[i],
                                gdn_norm_w[i], w_out_even[i], bsz, seq)
        else:
            lam_init = 0.8 - 0.6 * math.exp(-0.3 * layer)
            mixed = _odd_mixer(h, rel_bias, w_in_odd[i], diff_lambda[i], diff_norm_w[i], gla_w_gate[i],
                               gla_b_gate[i], gla_norm_w[i], w_out_odd[i], lam_init, bsz, seq)
        h = _layer_tail(*mixed[:2], h, *mixed[2:], ln_g[layer, 0][None], ln_b[layer, 0][None],
                        w_up, ffn_conv_w[layer], ffn_conv_b[layer][None], w_down,
                        ln_g[layer, 1][None], ln_b[layer, 1][None], layer, seq)
    return h.reshape(bsz, seq, d)
```

```python
import functools
import math

import numpy as np
import jax
import jax.numpy as jnp
from jax import lax
from jax.experimental import pallas as pl
from jax.experimental.pallas import tpu as pltpu

F32 = jnp.float32
BF16 = jnp.bfloat16
HI = lax.Precision.HIGHEST

D_MODEL = 1024
DEPTH = 2
DEEPNORM_ALPHA = (2 * DEPTH) ** 0.25
LN_EPS = 1e-5
RMS_EPS = 1e-6
NUM_BUCKETS = 32
REL_MAX_DIST = 2048
GDN_HEADS = 6
GDN_D = 128
CHUNK = 64
GDN_GROUP = 256
GDN_STEP_GROUPS = 2
SWA_CONFIGS = ((128, 1), (512, 4), (2048, 16))
SWA_HEADS = 4
SWA_DH = 64
SWA_BLOCK = 128
SWA_UNROLL = 8
SWA_SPAN = SWA_BLOCK * max(d for _, d in SWA_CONFIGS)
DIFF_HEADS = 4
DIFF_DH = 64
DIFF_BLOCK = 512
LOG2E = math.log2(math.e)
GLA_HEADS = 4
GLA_DK = 64
GLA_DV = 128
GLA_RANK = 16
GLA_TAU = 16.0
GLA_SUB = 16
GLA_GROUP = 256
GLA_STEP_GROUPS = 4
GLA_MAX_DECAY = 60.0
D_FF = 2816

LANES = 128
SUBLANES = 8
HALO = 8
VMEM_LIMIT = 56 * 1024 * 1024
NEG_BIG = -1e30

EVA_PART = GDN_HEADS * GDN_D
EVB_BA = 2304
EVB_COLS = 2560
EV_TN = 256
OD_QC, OD_KC, OD_VC = 0, 512, 1024
OD_QD, OD_KD, OD_VD, OD_RD = 1536, 1792, 2048, 2560
OD_COLS = 3072
OD_TN = 768
PROJ_TM = 512
TAIL_TM = 512
FFN_TC = 256


def _cparams(sem):
    return pltpu.CompilerParams(dimension_semantics=sem, vmem_limit_bytes=VMEM_LIMIT)


def _bdot(a, b):
    return jnp.dot(a.astype(BF16), b.astype(BF16), preferred_element_type=F32)


def _dot_nt(a, b):
    return lax.dot_general(a, b, (((1,), (1,)), ((), ())), preferred_element_type=F32)


def _bdot_nt(a, b):
    return lax.dot_general(a.astype(BF16), b.astype(BF16), (((1,), (1,)), ((), ())),
                           preferred_element_type=F32)


def _bdot_tn(a, b):
    return lax.dot_general(a.astype(BF16), b.astype(BF16), (((0,), (0,)), ((), ())),
                           preferred_element_type=F32)


def _sigmoid(x):
    return 1.0 / (1.0 + jnp.exp(-x))


def _silu(x):
    return x * _sigmoid(x)


def _softplus(x):
    return jnp.maximum(x, 0.0) + jnp.log1p(jnp.exp(-jnp.abs(x)))


def _log_sigmoid(x):
    return -_softplus(-x)


def _resident(a, lead=None):
    if lead is None:
        return pl.BlockSpec(a.shape, lambda i: (0,) * a.ndim, pipeline_mode=pl.Buffered(1))
    return pl.BlockSpec((None,) + a.shape[1:], lambda i: (lead,) + (0,) * (a.ndim - 1),
                        pipeline_mode=pl.Buffered(1))


def _odd_in_body(x_ref, w_ref, wg_ref, o_ref, og_ref):
    xb = x_ref[...].astype(BF16)
    for c in range(OD_COLS // OD_TN):
        cols = slice(c * OD_TN, (c + 1) * OD_TN)
        o_ref[:, cols] = _dot_nt(xb, w_ref[cols, :]).astype(o_ref.dtype)
    og_ref[...] = _dot_nt(xb, wg_ref[...])


def _odd_in_proj(x, w, w_gate_in, tm):
    t, k = x.shape
    return pl.pallas_call(
        _odd_in_body,
        grid=(t // tm,),
        in_specs=[pl.BlockSpec((tm, k), lambda i: (i, 0)), _resident(w), _resident(w_gate_in)],
        out_specs=[pl.BlockSpec((tm, OD_COLS), lambda i: (i, 0)), pl.BlockSpec((tm, LANES), lambda i: (i, 0))],
        out_shape=[jax.ShapeDtypeStruct((t, OD_COLS), BF16), jax.ShapeDtypeStruct((t, LANES), F32)],
        compiler_params=_cparams(("parallel",)),
        name="odd_in_proj",
    )(x, w, w_gate_in)


def _rel_bucket_np(dist):
    max_exact = NUM_BUCKETS // 2
    d = np.maximum(dist, 1).astype(np.float32)
    large = max_exact + (np.log(d / max_exact) / math.log(REL_MAX_DIST / max_exact)
                         * (NUM_BUCKETS - max_exact)).astype(np.int32)
    large = np.minimum(large, NUM_BUCKETS - 1)
    return np.where(dist < max_exact, dist, large).astype(np.int32)


def _bias_body(scale, base_bucket, tile_buckets, toeplitz, rb_ref, bucket_ref, neg_ref, o_ref):
    h = pl.program_id(0)
    t = pl.program_id(1)
    base = 0.0 if base_bucket is None else rb_ref[base_bucket, h]
    for tile, present in enumerate(tile_buckets):
        @pl.when(t == tile)
        def _(present=present):
            bucket = bucket_ref[0]
            acc = neg_ref[0]
            for b in present:
                acc = acc + jnp.where(bucket == b, (rb_ref[b, h] - base) * scale, 0.0)
            if toeplitz:
                n = o_ref.shape[2]
                rows = jnp.broadcast_to(acc[0:1], (n, 2 * n))
                acc = pltpu.roll(rows, n, 1, stride=1, stride_axis=0)[:, :n]
            o_ref[0, 0] = acc


def _bias_tiles(rel_bias, bucket, neg, scale=1.0, base_bucket=None, toeplitz=False):
    nt, r, c = bucket.shape
    nh = rel_bias.shape[1]
    out_r, out_c = (c // 2, c // 2) if toeplitz else (r, c)
    tile_buckets = tuple(tuple(int(b) for b in np.unique(bucket[t][neg[t] == 0]) if b != base_bucket)
                         for t in range(nt))
    return pl.pallas_call(
        functools.partial(_bias_body, scale, base_bucket, tile_buckets, toeplitz),
        grid=(nh, nt),
        in_specs=[pl.BlockSpec(memory_space=pltpu.SMEM),
                  pl.BlockSpec((1, r, c), lambda h, t: (t, 0, 0)),
                  pl.BlockSpec((1, r, c), lambda h, t: (t, 0, 0))],
        out_specs=pl.BlockSpec((1, 1, out_r, out_c), lambda h, t: (h, t, 0, 0)),
        out_shape=jax.ShapeDtypeStruct((nh, nt, out_r, out_c), F32),
        compiler_params=_cparams(("parallel", "parallel")),
        name="rel_bias_tiles",
    )(rel_bias, jnp.asarray(bucket), jnp.asarray(neg))


def _causal_conv(cur, w, scratch, carry, at_start):
    width = w.shape[0]
    rows = cur.shape[0]
    scratch[0:HALO, :] = jnp.where(at_start, 0.0, carry[...])
    scratch[HALO:, :] = cur
    carry[...] = cur[rows - HALO:]
    y = w[width - 1:width, :] * cur
    for j in range(width - 1):
        back = width - 1 - j
        y = y + w[j:j + 1, :] * scratch[HALO - back:HALO - back + rows, :]
    return y


def _even_in_body(tiles_per_seq, x_ref, wa_ref, wb_ref, cw_ref, oa_ref, ob_ref, conv_in):
    rows = x_ref.shape[0]
    width = cw_ref.shape[0]
    conv_cols = 3 * EVA_PART
    at_start = pl.program_id(0) % tiles_per_seq == 0
    conv_in[0:HALO, :] = jnp.where(at_start, 0.0, conv_in[rows:rows + HALO, :])
    x = x_ref[...].astype(BF16)

    def project_conv_in(c):
        cols = slice(c * EV_TN, (c + 1) * EV_TN)
        conv_in[HALO:, cols] = _dot_nt(x, wa_ref[cols, :])

    def project_z(c):
        cols = slice(conv_cols + c * EV_TN, conv_cols + (c + 1) * EV_TN)
        oa_ref[:, cols] = _dot_nt(x, wa_ref[cols, :]).astype(oa_ref.dtype)

    def project_b(c):
        cols = slice(c * EV_TN, (c + 1) * EV_TN)
        ob_ref[:, cols] = _dot_nt(x, wb_ref[cols, :])

    def finish_head(part, hd):
        cols = slice(part * EVA_PART + hd * GDN_D, part * EVA_PART + (hd + 1) * GDN_D)
        y = cw_ref[width - 1:width, cols] * conv_in[HALO:, cols]
        for j in range(width - 1):
            back = width - 1 - j
            y = y + cw_ref[j:j + 1, cols] * conv_in[HALO - back:HALO - back + rows, cols]
        ch = _silu(y)
        if part < 2:
            inv = lax.rsqrt(jnp.sum(ch * ch, axis=-1, keepdims=True) + RMS_EPS)
            ch = ch * (inv * GDN_D ** -0.5 if part == 0 else inv)
        oa_ref[:, cols] = ch.astype(oa_ref.dtype)

    plain = ([functools.partial(project_b, c) for c in range(EVB_COLS // EV_TN)]
             + [functools.partial(project_z, c) for c in range(EVA_PART // EV_TN)])
    heads_per_block = EV_TN // GDN_D
    n_conv = conv_cols // EV_TN
    project_conv_in(0)
    for c in range(n_conv):
        matmuls = ([functools.partial(project_conv_in, c + 1)] if c + 1 < n_conv else [])
        share = -(-len(plain) // (n_conv - c))
        matmuls, plain = matmuls + plain[:share], plain[share:]
        for i in range(max(len(matmuls), heads_per_block)):
            if i < len(matmuls):
                matmuls[i]()
            if i < heads_per_block:
                head = c * heads_per_block + i
                finish_head(head // GDN_HEADS, head % GDN_HEADS)


def _even_in_proj(x, w_a, w_b, conv_w, seq, tm):
    t, k = x.shape
    return pl.pallas_call(
        functools.partial(_even_in_body, seq // tm),
        grid=(t // tm,),
        in_specs=[pl.BlockSpec((tm, k), lambda i: (i, 0)),
                  _resident(w_a), _resident(w_b), _resident(conv_w)],
        out_specs=[pl.BlockSpec((tm, w_a.shape[0]), lambda i: (i, 0)),
                   pl.BlockSpec((tm, EVB_COLS), lambda i: (i, 0))],
        out_shape=[jax.ShapeDtypeStruct((t, w_a.shape[0]), BF16), jax.ShapeDtypeStruct((t, EVB_COLS), F32)],
        scratch_shapes=[pltpu.VMEM((HALO + tm, 3 * EVA_PART), F32)],
        compiler_params=_cparams(("arbitrary",)),
        name="even_in_proj",
    )(x, w_a, w_b, conv_w)


def _gdn_body(q_ref, k_ref, v_ref, z_ref, ba_ref, alog_ref, dtb_ref, nw_ref, o_ref, state):
    n = pl.program_id(1)

    @pl.when(n == 0)
    def _():
        state[...] = jnp.zeros_like(state)

    grp = GDN_GROUP
    nc = grp // CHUNK
    nh = GDN_HEADS
    hs = range(nh)
    ri = lax.broadcasted_iota(jnp.int32, (grp, grp), 0)
    ci = lax.broadcasted_iota(jnp.int32, (grp, grp), 1)
    same = (ri // CHUNK) == (ci // CHUNK)
    incl = same & (ri >= ci)
    strict = same & (ri > ci)
    eye = (ri == ci).astype(F32)
    tri = incl.astype(BF16)

    def chunk_local(groups):
        units = [(g, h) for g in range(len(groups)) for h in hs]
        gam_all, gam_rows, beta_all = [], [], []
        for rows in groups:
            ba = ba_ref[rows, :]
            beta_all.append(_sigmoid(ba))
            g_all = -jnp.exp(alog_ref[...]) * _softplus(ba + dtb_ref[...])
            g_hi = g_all.astype(BF16)
            rem = g_all - g_hi.astype(F32)
            g_mid = rem.astype(BF16)
            g_lo = (rem - g_mid.astype(F32)).astype(BF16)
            gam_all.append(jnp.dot(tri, g_hi, preferred_element_type=F32)
                           + jnp.dot(tri, g_mid, preferred_element_type=F32)
                           + jnp.dot(tri, g_lo, preferred_element_type=F32))
            gam_rows.append(gam_all[-1].T)

        def head(ref, g, h):
            return ref[groups[g], h * GDN_D:(h + 1) * GDN_D]

        q = [head(q_ref, g, h) for g, h in units]
        k = [head(k_ref, g, h) for g, h in units]
        v = [head(v_ref, g, h) for g, h in units]
        gam = [jnp.broadcast_to(gam_all[g][:, nh + h:nh + h + 1], (grp, GDN_D)) for g, h in units]
        bcol = [beta_all[g][:, h:h + 1] for g, h in units]
        us = range(len(units))
        decay, x, inv = [], [], []
        for i, (g, h) in enumerate(units):
            diff = gam[i][:, 0:1] - gam_rows[g][nh + h:nh + h + 1, :]
            decay.append(jnp.where(incl, jnp.exp(jnp.where(incl, diff, 0.0)), 0.0))
            kk = _bdot_nt(k[i], k[i])
            x.append(-jnp.where(strict, bcol[i] * kk * decay[i], 0.0))
            inv.append(eye + x[i])
        for _ in range(5):
            for i in us:
                x[i] = _bdot(x[i], x[i])
                inv[i] = inv[i] + _bdot(inv[i], x[i])
        out = [tuple([] for _ in range(6)) for _ in groups]
        for i, (g, h) in enumerate(units):
            eg = jnp.exp(gam[i])
            uw = _bdot(inv[i], jnp.concatenate([v[i] * bcol[i], k[i] * (bcol[i] * eg)], axis=1))
            kd, gl = [], []
            for c in range(nc):
                last = gam[i][(c + 1) * CHUNK - 1:(c + 1) * CHUNK, :]
                kd.append(k[i][c * CHUNK:(c + 1) * CHUNK] * jnp.exp(last - gam[i][c * CHUNK:(c + 1) * CHUNK]))
                gl.append(jnp.exp(last))
            for lst, val in zip(out[g], (uw[:, :GDN_D], uw[:, GDN_D:], _bdot_nt(q[i], k[i]) * decay[i],
                                         q[i] * eg, kd, gl)):
                lst.append(val)
        return out

    def recurrence(rows, local, s):
        u, w, qk, q_dec, k_dec, g_last = local
        q_s = [[] for _ in hs]
        delta = [[] for _ in hs]
        for c in range(nc):
            sl = slice(c * CHUNK, (c + 1) * CHUNK)
            for h in hs:
                r = _bdot(jnp.concatenate([w[h][sl], q_dec[h][sl]], axis=0), s[h])
                d = u[h][sl] - r[:CHUNK]
                q_s[h].append(r[CHUNK:])
                delta[h].append(d)
                s[h] = g_last[h][c] * s[h] + _bdot_tn(k_dec[h][c], d)
        for h in hs:
            o = jnp.concatenate(q_s[h], axis=0) + _bdot(qk[h], jnp.concatenate(delta[h], axis=0))
            o = o * lax.rsqrt(jnp.mean(o * o, axis=-1, keepdims=True) + RMS_EPS) * nw_ref[...]
            z = z_ref[rows, h * GDN_D:(h + 1) * GDN_D].astype(F32)
            o_ref[rows, h * GDN_D:(h + 1) * GDN_D] = (o * _silu(z)).astype(o_ref.dtype)
        return s

    groups = [slice(g * grp, (g + 1) * grp) for g in range(GDN_STEP_GROUPS)]
    s = [state[h] for h in hs]
    for rows, local in zip(groups, chunk_local(groups)):
        s = recurrence(rows, local, s)
    for h in hs:
        state[h] = s[h]


def _gdn(proj_a, proj_b, alog_pad, dtb_pad, norm_w, bsz, seq):
    t = proj_a.shape[0]
    rows = GDN_GROUP * GDN_STEP_GROUPS
    spb = seq // rows
    width = GDN_HEADS * GDN_D

    def at(col):
        return lambda b, n: (b * spb + n, col)

    return pl.pallas_call(
        _gdn_body,
        grid=(bsz, spb),
        in_specs=[pl.BlockSpec((rows, width), at(0)),
                  pl.BlockSpec((rows, width), at(1)),
                  pl.BlockSpec((rows, width), at(2)),
                  pl.BlockSpec((rows, width), at(3)),
                  pl.BlockSpec((rows, LANES), at(EVB_BA // LANES)),
                  pl.BlockSpec((1, LANES), lambda b, n: (0, 0)),
                  pl.BlockSpec((1, LANES), lambda b, n: (0, 0)),
                  pl.BlockSpec((1, LANES), lambda b, n: (0, 0))],
        out_specs=pl.BlockSpec((rows, width), at(0)),
        out_shape=jax.ShapeDtypeStruct((t, width), BF16),
        scratch_shapes=[pltpu.VMEM((GDN_HEADS, GDN_D, GDN_D), F32)],
        compiler_params=_cparams(("parallel", "arbitrary")),
        name="gdn",
    )(proj_a, proj_a, proj_a, proj_a, proj_b, alog_pad, dtb_pad, norm_w)


def _swa_bias_tables():
    qi = np.arange(SWA_BLOCK)[:, None] + SWA_BLOCK
    kj = np.arange(2 * SWA_BLOCK)[None, :]
    rel = qi - kj
    buckets, negs = [], []
    for has_prev in (True, False):
        for window, dilation in SWA_CONFIGS:
            valid = (rel >= 0) & (rel <= window // dilation) & (has_prev | (kj >= SWA_BLOCK))
            buckets.append(_rel_bucket_np(np.maximum(rel, 0) * dilation))
            negs.append(np.where(valid, 0.0, NEG_BIG))
    return np.stack(buckets).astype(np.int32), np.stack(negs).astype(np.float32)


def _swa_body(*refs):
    ng = len(SWA_CONFIGS)
    ins, bias_ref, o_ref = refs[:5 * ng], refs[5 * ng], refs[5 * ng + 1]
    scratch = refs[5 * ng + 2:]
    kbufs, vbufs, o_scr, lse_scr = scratch[:ng], scratch[ng:2 * ng], scratch[2 * ng], scratch[2 * ng + 1]
    j = pl.program_id(1)
    blk = SWA_BLOCK
    first_head = lax.broadcasted_iota(jnp.int32, (blk, LANES), 1) < SWA_DH

    for g, (_, d) in enumerate(SWA_CONFIGS):
        q_ref, kc_ref, kp_ref, vc_ref, vp_ref = ins[5 * g:5 * g + 5]
        kbuf, vbuf = kbufs[g], vbufs[g]
        halo = blk * d
        kbuf[0:halo, :] = kp_ref[...]
        kbuf[halo:, :] = kc_ref[...]
        vbuf[0:halo, :] = vp_ref[...]
        vbuf[halo:, :] = vc_ref[...]

        def unit(u, carry, g=g, d=d, halo=halo, q_ref=q_ref, kbuf=kbuf, vbuf=vbuf):
            base = (u // d) * halo + u % d
            q = q_ref[pl.ds(base, blk, stride=d), :] * (SWA_DH ** -0.5 * LOG2E)
            k = kbuf[pl.ds(base, 2 * blk, stride=d), :]
            v = vbuf[pl.ds(base, 2 * blk, stride=d), :]
            lhs = jnp.concatenate([jnp.where(first_head, q, 0.0), jnp.where(first_head, 0.0, q)], axis=0)
            no_prev = jnp.where((j == 0) & (u < d), 1, 0)
            s = _bdot_nt(lhs, k) + bias_ref[g, 0, no_prev]
            m = jnp.max(s, axis=-1, keepdims=True)
            p = jnp.exp2(s - m)
            l = jnp.sum(p, axis=-1, keepdims=True)
            o2 = _bdot(p, v) * (1.0 / l)
            lse2 = m + jnp.log2(l)
            o_scr[g, pl.ds(base, blk, stride=d), :] = jnp.where(first_head, o2[:blk], o2[blk:])
            lse_scr[g, pl.ds(base, blk, stride=d), :] = jnp.where(first_head, lse2[:blk], lse2[blk:])
            return carry

        lax.fori_loop(0, SWA_SPAN // blk, unit, 0, unroll=SWA_UNROLL)

    def combine(c, carry):
        rows = pl.ds(pl.multiple_of(c * 2 * blk, 2 * blk), 2 * blk)
        lse = [lse_scr[g, rows, :] for g in range(ng)]
        m = functools.reduce(jnp.maximum, lse)
        e = [jnp.exp2(x - m) for x in lse]
        den = functools.reduce(lambda x, y: x + y, e)
        o_ref[rows, :] = functools.reduce(
            lambda x, y: x + y, [(e[g] / den) * o_scr[g, rows, :] for g in range(ng)]).astype(o_ref.dtype)
        return carry

    lax.fori_loop(0, SWA_SPAN // (2 * blk), combine, 0)


def _swa(proj, bias, bsz, seq):
    t = proj.shape[0]
    ng = len(SWA_CONFIGS)
    nspan = seq // SWA_SPAN
    npair = SWA_HEADS * SWA_DH // LANES
    group_cols = SWA_HEADS * SWA_DH // LANES
    in_specs, scratch_k = [], []
    for g, (_, d) in enumerate(SWA_CONFIGS):
        halo = SWA_BLOCK * d
        per_span = SWA_SPAN // halo

        def cur(which, g=g):
            col = (which * ng + g) * group_cols
            return lambda b, j, p: (b * nspan + j, col + p)

        def prev(which, g=g, per_span=per_span):
            col = (which * ng + g) * group_cols
            return lambda b, j, p: (jnp.maximum((b * nspan + j) * per_span - 1, 0), col + p)

        in_specs += [pl.BlockSpec((SWA_SPAN, LANES), cur(0)),
                     pl.BlockSpec((SWA_SPAN, LANES), cur(1)), pl.BlockSpec((halo, LANES), prev(1)),
                     pl.BlockSpec((SWA_SPAN, LANES), cur(2)), pl.BlockSpec((halo, LANES), prev(2))]
        scratch_k.append(pltpu.VMEM((halo + SWA_SPAN, LANES), F32))
    in_specs.append(pl.BlockSpec((ng, 1, 2, 2 * SWA_BLOCK, 2 * SWA_BLOCK), lambda b, j, p: (0, p, 0, 0, 0)))
    return pl.pallas_call(
        _swa_body,
        grid=(bsz, nspan, npair),
        in_specs=in_specs,
        out_specs=pl.BlockSpec((SWA_SPAN, LANES), lambda b, j, p: (b * nspan + j, p)),
        out_shape=jax.ShapeDtypeStruct((t, npair * LANES), BF16),
        scratch_shapes=scratch_k + scratch_k + [pltpu.VMEM((ng, SWA_SPAN, LANES), F32)] * 2,
        compiler_params=_cparams(("parallel", "parallel", "parallel")),
        name="swa",
    )(*([proj] * (5 * ng)), bias)


def _layer_norm(y, g, b):
    mu = jnp.mean(y, axis=-1, keepdims=True)
    yc = y - mu
    var = jnp.mean(yc * yc, axis=-1, keepdims=True)
    return yc * lax.rsqrt(var + LN_EPS) * g + b


def _diff_bias_tables(blk):
    buckets = _rel_bucket_np(np.arange(2 * REL_MAX_DIST))
    far = int(np.max(np.nonzero(buckets != NUM_BUCKETS - 1)[0])) + 1
    nb = -(-(far + blk - 1) // blk)
    dist = np.stack([t * blk - blk + np.arange(2 * blk) for t in range(nb + 1)])[:, None, :]
    dist = np.broadcast_to(dist, (nb + 1, SUBLANES, 2 * blk))
    bucket = _rel_bucket_np(np.maximum(dist, 0))
    neg = np.where(dist >= 0, 0.0, NEG_BIG).astype(np.float32)
    return bucket, neg


def _diff_body(blk, nb, lam_init, q_ref, qn_ref, k_ref, v_ref, bias_ref, lam_ref, nw_ref, o_ref,
               vt, acc1, acc2, s_a, s_b):
    qi = pl.program_id(2)
    dv = 2 * DIFF_DH
    seq = k_ref.shape[0]

    @pl.when(qi == 0)
    def _():
        vt[dv:, :] = jnp.ones((vt.shape[0] - dv, seq), BF16)

        def fill(c, carry):
            st = pl.multiple_of(c * blk, blk)
            vt[0:dv, pl.ds(st, blk)] = v_ref[pl.ds(st, blk), :].astype(F32).T.astype(BF16)
            return carry

        lax.fori_loop(0, seq // blk, fill, 0)

    feature = lax.broadcasted_iota(jnp.int32, (dv, blk), 0)

    def components(ref):
        q_t = (ref[...].astype(F32) * (DIFF_DH ** -0.5 * LOG2E)).T
        return (jnp.where(feature < DIFF_DH, q_t, 0.0).astype(BF16),
                jnp.where(feature >= DIFF_DH, q_t, 0.0).astype(BF16))

    qs = components(q_ref)
    qs_next = components(qn_ref)
    accs = (acc1, acc2)
    for acc in accs:
        acc[...] = jnp.zeros_like(acc)

    last = pl.num_programs(2) - 1

    def key_rows(kj):
        return pl.ds(pl.multiple_of(jnp.minimum(kj, last) * blk, blk), blk)

    def scores(kj, dst, queries=qs, offset=None):
        k = k_ref[key_rows(kj), :]
        for c, qc in enumerate(queries):
            s = jnp.dot(k, qc, preferred_element_type=F32)
            dst[c] = s if offset is None else s + bias_ref[0, jnp.minimum(offset, nb)]

    def consume(kj, src, ms):
        vtb = vt[:, key_rows(kj)]
        out = []
        for c, (m, acc) in enumerate(zip(ms, accs)):
            halves = []
            for h in range(2):
                cols = slice(h * blk // 2, (h + 1) * blk // 2)
                s = src[c, :, cols]
                m_new = jnp.maximum(m[:, cols], jnp.max(s, axis=0, keepdims=True))
                p = jnp.exp2(s - m_new).astype(BF16)
                acc[:, cols] = (jnp.exp2(m[:, cols] - m_new) * acc[:, cols]
                                + jnp.dot(vtb, p, preferred_element_type=F32))
                halves.append(m_new)
            out.append(jnp.concatenate(halves, axis=1))
        return tuple(out)

    nblocks = qi + 1
    trips = nblocks // 2
    odd = nblocks % 2 == 1
    far_trips = jnp.maximum(qi - nb, 0) // 2

    def pair(biased, t, ms):
        kj = 2 * t
        scores(kj + 1, s_b, offset=qi - (kj + 1) if biased else None)
        ms = consume(kj, s_a, ms)
        if biased:
            hand_off = (t == trips - 1) & jnp.logical_not(odd)
            scores(jnp.where(hand_off, 0, kj + 2), s_a,
                   tuple(jnp.where(hand_off, qn, qc) for qc, qn in zip(qs, qs_next)),
                   offset=jnp.where(hand_off, qi + 1, qi - (kj + 2)))
        else:
            scores(kj + 2, s_a)
        return consume(kj + 1, s_b, ms)

    @pl.when(qi == 0)
    def _():
        scores(0, s_a, offset=0)

    m0 = jnp.full((1, blk), NEG_BIG, F32)
    ms = lax.fori_loop(0, far_trips, functools.partial(pair, False), (m0, m0))
    ms = lax.fori_loop(far_trips, trips, functools.partial(pair, True), ms)

    def finish():
        lp = lam_ref[...]
        lam = (jnp.exp(jnp.sum(lp[0:1] * lp[1:2], axis=-1, keepdims=True))
               - jnp.exp(jnp.sum(lp[2:3] * lp[3:4], axis=-1, keepdims=True)) + lam_init)
        a1, a2 = acc1[...], acc2[...]
        o_t = a1[:dv] * (1.0 / a1[dv:dv + 1]) - a2[:dv] * (lam / a2[dv:dv + 1])
        o = o_t.T
        o = o * lax.rsqrt(jnp.mean(o * o, axis=-1, keepdims=True) + RMS_EPS) * nw_ref[...]
        o_ref[...] = (o * (1.0 - lam_init)).astype(o_ref.dtype)

    @pl.when(odd)
    def _():
        consume(qi, s_a, ms)
        scores(0, s_a, qs_next, offset=qi + 1)
        finish()

    @pl.when(jnp.logical_not(odd))
    def _():
        finish()


def _diff_attention(proj, bias, lam_params, norm_w, lam_init, bsz, seq, blk):
    t = proj.shape[0]
    nq = seq // blk
    nt = bias.shape[1]
    nb = nt - 1
    width = 2 * DIFF_DH
    ones_rows = 2 * SUBLANES
    return pl.pallas_call(
        functools.partial(_diff_body, blk, nb, lam_init),
        grid=(bsz, DIFF_HEADS, nq),
        in_specs=[pl.BlockSpec((blk, width), lambda b, h, i: (b * nq + i, OD_QC // width + h)),
                  pl.BlockSpec((blk, width),
                               lambda b, h, i: (b * nq + jnp.minimum(i + 1, nq - 1), OD_QC // width + h)),
                  pl.BlockSpec((seq, width), lambda b, h, i: (b, OD_KC // width + h)),
                  pl.BlockSpec((seq, width), lambda b, h, i: (b, OD_VC // width + h)),
                  pl.BlockSpec((1, nt, blk, blk), lambda b, h, i: (h, 0, 0, 0)),
                  pl.BlockSpec((4, DIFF_DH), lambda b, h, i: (0, 0)),
                  pl.BlockSpec((1, width), lambda b, h, i: (0, 0))],
        out_specs=pl.BlockSpec((blk, width), lambda b, h, i: (b * nq + i, h)),
        out_shape=jax.ShapeDtypeStruct((t, DIFF_HEADS * width), BF16),
        scratch_shapes=[pltpu.VMEM((width + ones_rows, seq), BF16),
                        pltpu.VMEM((width + ones_rows, blk), F32), pltpu.VMEM((width + ones_rows, blk), F32),
                        pltpu.VMEM((2, blk, blk), F32), pltpu.VMEM((2, blk, blk), F32)],
        compiler_params=_cparams(("arbitrary", "arbitrary", "arbitrary")),
        name="diff_attn",
    )(proj, proj, proj, proj, bias, lam_params, norm_w)


def _gla_body(q_ref, k_ref, gd_ref, v_ref, r_ref, wg_ref, bg_ref, nw_ref, o_ref, state, part):
    n = pl.program_id(1)

    @pl.when(n == 0)
    def _():
        state[...] = jnp.zeros_like(state)

    rows = GLA_GROUP
    nc = rows // CHUNK
    npair = GLA_HEADS // 2
    groups = [slice(g * rows, (g + 1) * rows) for g in range(GLA_STEP_GROUPS)]
    gate = jnp.dot(gd_ref[...], wg_ref[...], precision=HI, preferred_element_type=F32) + bg_ref[...]
    log_a = _log_sigmoid(gate) * (1.0 / GLA_TAU)

    ri = lax.broadcasted_iota(jnp.int32, (rows, rows), 0)
    ci = lax.broadcasted_iota(jnp.int32, (rows, rows), 1)
    causal = ((ri // CHUNK) == (ci // CHUNK)) & (ri >= ci)
    tri = causal.astype(BF16)
    hi = log_a.astype(BF16)
    rem = log_a - hi.astype(F32)
    mid = rem.astype(BF16)
    lo = (rem - mid.astype(F32)).astype(BF16)
    b_all = [jnp.dot(tri, hi[gr], preferred_element_type=F32) + jnp.dot(tri, mid[gr], preferred_element_type=F32)
             + jnp.dot(tri, lo[gr], preferred_element_type=F32) for gr in groups]

    lane = lax.broadcasted_iota(jnp.int32, (rows, LANES), 1)
    lane_c = lax.broadcasted_iota(jnp.int32, (CHUNK, LANES), 1)
    head_lanes = (lane < GLA_DK, lane >= GLA_DK)
    chunk_lanes = (lane_c < GLA_DK, lane_c >= GLA_DK)
    pairs = []
    for gr, b_g in zip(groups, b_all):
        pairs.append([])
        for p in range(npair):
            cols = slice(p * LANES, (p + 1) * LANES)
            q = q_ref[gr, cols].astype(F32) * GLA_DK ** -0.5
            pairs[-1].append(dict(
                b=b_g[:, cols], q=q, q_dec=q * jnp.exp(b_g[:, cols]), k=k_ref[gr, cols].astype(F32),
                v=[v_ref[gr, (2 * p + hd) * GLA_DV:(2 * p + hd + 1) * GLA_DV].astype(F32) for hd in range(2)]))

    for g, gr in enumerate(groups):
        for c in range(nc):
            sl = slice(c * CHUNK, (c + 1) * CHUNK)
            out_rows = slice(g * rows + c * CHUNK, g * rows + (c + 1) * CHUNK)
            for p, pr in enumerate(pairs[g]):
                bc = pr["b"][sl]
                b_last = bc[CHUNK - 1:CHUNK]
                k_dec = pr["k"][sl] * jnp.exp(b_last - bc)
                e_last = jnp.exp(b_last)
                for hd in range(2):
                    h = 2 * p + hd
                    st = state[h]
                    part[h, out_rows, :] = _bdot_nt(jnp.where(chunk_lanes[hd], pr["q_dec"][sl], 0.0), st)
                    state[h] = st * e_last + _bdot_tn(pr["v"][hd][sl], k_dec)

    def finish(intra):
        for gr, intra_g in zip(groups, intra):
            for h in range(GLA_HEADS):
                o = part[h, gr, :] + intra_g[h]
                o = o * lax.rsqrt(jnp.mean(o * o, axis=-1, keepdims=True) + RMS_EPS) * nw_ref[...]
                gate_r = _silu(r_ref[gr, h * GLA_DV:(h + 1) * GLA_DV].astype(F32))
                o_ref[gr, h * GLA_DV:(h + 1) * GLA_DV] = (o * gate_r).astype(o_ref.dtype)

    def intra_whole_chunk():
        out = []
        for pairs_g in pairs:
            out.append([])
            for pr in pairs_g:
                k_inv = pr["k"] * jnp.exp(jnp.minimum(-pr["b"], GLA_MAX_DECAY))
                for hd in range(2):
                    a = _bdot_nt(jnp.where(head_lanes[hd], pr["q_dec"], 0.0), k_inv)
                    out[-1].append(_bdot(jnp.where(causal, a, 0.0), pr["v"][hd]))
        return out

    def intra_exact():
        out = []
        for pairs_g in pairs:
            out.append([])
            for pr in pairs_g:
                off = _gla_intra_off_diagonal(pr["q"], pr["k"], pr["b"], pr["v"], nc)
                diag = _gla_intra_diagonal(pr["q"], pr["k"], pr["b"], pr["v"])
                out[-1] += [off[hd] + diag[hd] for hd in range(2)]
        return out

    chunk_decay = functools.reduce(jnp.maximum, [
        jnp.max(-b_g.reshape(nc, CHUNK, npair * LANES)[:, CHUNK - 1:CHUNK, :]) for b_g in b_all])

    @pl.when(chunk_decay <= GLA_MAX_DECAY)
    def _():
        finish(intra_whole_chunk())

    @pl.when(chunk_decay > GLA_MAX_DECAY)
    def _():
        finish(intra_exact())


def _gla_intra_off_diagonal(q, k, b, vs, nc):
    per_chunk = CHUNK // GLA_SUB
    lane = lax.broadcasted_iota(jnp.int32, (GLA_SUB, LANES), 1)
    sub_mask = (lane < GLA_DK, lane >= GLA_DK)
    kcol = lax.broadcasted_iota(jnp.int32, (GLA_SUB, CHUNK), 1)
    outs = ([], [])
    for c in range(nc):
        sl = slice(c * CHUNK, (c + 1) * CHUNK)
        bc, qc, kc = b[sl], q[sl], k[sl]
        a_rows = [[jnp.zeros((GLA_SUB, CHUNK), F32)] for _ in vs]
        for blk in range(1, per_chunk):
            r0 = blk * GLA_SUB
            bref = bc[r0:r0 + 1]
            qs = qc[r0:r0 + GLA_SUB] * jnp.exp(bc[r0:r0 + GLA_SUB] - bref)
            ks = kc * jnp.exp(jnp.minimum(bref - bc, 0.0))
            for hd in range(2):
                a = _bdot_nt(jnp.where(sub_mask[hd], qs, 0.0), ks)
                a_rows[hd].append(jnp.where(kcol < r0, a, 0.0))
        for hd in range(2):
            outs[hd].append(_bdot(jnp.concatenate(a_rows[hd], axis=0), vs[hd][sl]))
    return [jnp.concatenate(o, axis=0) for o in outs]


def _gla_intra_diagonal(q, k, b, vs):
    rows = q.shape[0]
    nsub = rows // GLA_SUB
    b3 = b.reshape(nsub, GLA_SUB, LANES)
    q3 = q.reshape(nsub, GLA_SUB, LANES)
    k3 = k.reshape(nsub, GLA_SUB, LANES)
    v3 = [v.reshape(nsub, GLA_SUB, LANES) for v in vs]
    row3 = lax.broadcasted_iota(jnp.int32, (nsub, GLA_SUB, LANES), 1)
    lane3 = lax.broadcasted_iota(jnp.int32, (nsub, GLA_SUB, LANES), 2)
    rowc = lax.broadcasted_iota(jnp.int32, (nsub, GLA_SUB, 1), 1)
    o3 = [jnp.zeros((nsub, GLA_SUB, LANES), F32) for _ in vs]
    for jj in range(GLA_SUB):
        e = jnp.exp(jnp.where(row3 >= jj, b3 - b3[:, jj:jj + 1, :], 0.0))
        t = q3 * k3[:, jj:jj + 1, :] * e
        w_all = jnp.sum(t, axis=-1, keepdims=True)
        w_a = jnp.sum(jnp.where(lane3 < GLA_DK, t, 0.0), axis=-1, keepdims=True)
        for hd, w in enumerate((w_a, w_all - w_a)):
            o3[hd] = o3[hd] + jnp.where(rowc >= jj, w, 0.0) * v3[hd][:, jj:jj + 1, :]
    return [o.reshape(rows, LANES) for o in o3]


def _gla(proj, gate_in, w_gate_pad, b_gate, norm_w, bsz, seq):
    t = proj.shape[0]
    rows = GLA_GROUP * GLA_STEP_GROUPS
    spb = seq // rows
    qk_w = GLA_HEADS * GLA_DK
    v_w = GLA_HEADS * GLA_DV

    def at(col):
        return lambda b, n: (b * spb + n, col)

    def whole(a):
        return pl.BlockSpec(a.shape, lambda b, n: (0, 0))

    return pl.pallas_call(
        _gla_body,
        grid=(bsz, spb),
        in_specs=[pl.BlockSpec((rows, qk_w), at(OD_QD // qk_w)),
                  pl.BlockSpec((rows, qk_w), at(OD_KD // qk_w)),
                  pl.BlockSpec((rows, LANES), at(0)),
                  pl.BlockSpec((rows, v_w), at(OD_VD // v_w)),
                  pl.BlockSpec((rows, v_w), at(OD_RD // v_w)),
                  whole(w_gate_pad), whole(b_gate), whole(norm_w)],
        out_specs=pl.BlockSpec((rows, v_w), at(0)),
        out_shape=jax.ShapeDtypeStruct((t, v_w), BF16),
        scratch_shapes=[pltpu.VMEM((GLA_HEADS, GLA_DV, LANES), F32), pltpu.VMEM((GLA_HEADS, rows, GLA_DV), F32)],
        compiler_params=_cparams(("parallel", "arbitrary")),
        name="gla",
    )(proj, proj, gate_in, proj, proj, w_gate_pad, b_gate, norm_w)


def _tail_body(tiles_per_seq, tc, ma_ref, mb_ref, h_ref, woa_ref, wob_ref,
               g1_ref, b1_ref, wu_ref, cw_ref, cb_ref, wd_ref, g2_ref, b2_ref, y_ref,
               act, scratch_g, scratch_v, carry):
    at_start = pl.program_id(0) % tiles_per_seq == 0
    mix = (jnp.dot(ma_ref[...], woa_ref[...], preferred_element_type=F32)
           + jnp.dot(mb_ref[...], wob_ref[...], preferred_element_type=F32))
    x = _layer_norm(DEEPNORM_ALPHA * h_ref[...] + mix, g1_ref[...], b1_ref[...])
    xb = x.astype(BF16)
    nchunk = D_FF // tc

    def branch(idx, scratch):
        lo = idx * tc
        cur = jnp.dot(xb, wu_ref[:, lo:lo + tc], preferred_element_type=F32)
        return _causal_conv(cur, cw_ref[:, lo:lo + tc], scratch, carry.at[idx], at_start) + cb_ref[:, lo:lo + tc]

    for c in range(nchunk):
        gate = branch(c, scratch_g)
        val = branch(nchunk + c, scratch_v)
        act[:, c * tc:(c + 1) * tc] = (_silu(gate) * val).astype(BF16)
    ffn = jnp.dot(act[...], wd_ref[...], preferred_element_type=F32)
    y_ref[...] = _layer_norm(DEEPNORM_ALPHA * x + ffn, g2_ref[...], b2_ref[...])


def _layer_tail(mix_a, mix_b, h, w_out_a, w_out_b, g1, b1, w_up, conv_w, conv_b, w_down, g2, b2,
                layer, seq):
    t = h.shape[0]
    tm, tc = TAIL_TM, FFN_TC

    def tile(a):
        return pl.BlockSpec((tm, a.shape[1]), lambda i: (i, 0))

    params = (w_out_a, w_out_b, g1, b1, w_up, conv_w, conv_b, w_down, g2, b2)
    param_specs = [_resident(w_out_a), _resident(w_out_b), _resident(g1), _resident(b1),
                   _resident(w_up, layer), _resident(conv_w), _resident(conv_b), _resident(w_down, layer),
                   _resident(g2), _resident(b2)]
    return pl.pallas_call(
        functools.partial(_tail_body, seq // tm, tc),
        grid=(t // tm,),
        in_specs=[tile(mix_a), tile(mix_b), tile(h)] + param_specs,
        out_specs=pl.BlockSpec((tm, D_MODEL), lambda i: (i, 0)),
        out_shape=jax.ShapeDtypeStruct((t, D_MODEL), F32),
        scratch_shapes=[pltpu.VMEM((tm, D_FF), BF16), pltpu.VMEM((HALO + tm, tc), F32),
                        pltpu.VMEM((HALO + tm, tc), F32), pltpu.VMEM((2 * D_FF // tc, HALO, tc), F32)],
        compiler_params=_cparams(("arbitrary",)),
        name="layer_tail",
    )(mix_a, mix_b, h, *params)


def _even_w_in(w):
    w_t = w.T
    a_end = 4 * EVA_PART
    gates = w_t[a_end:a_end + 2 * GDN_HEADS]
    qkv_b = w_t[a_end + 2 * GDN_HEADS:]
    pad = jnp.zeros((EVB_COLS - EVB_BA - 2 * GDN_HEADS, w.shape[0]), w.dtype)
    return w_t[:a_end].astype(BF16), jnp.concatenate([qkv_b, gates, pad], axis=0).astype(BF16)


def _odd_w_in(w):
    w_t = w.T
    pad = jnp.zeros((LANES - GLA_RANK, w.shape[0]), w.dtype)
    return w_t[:OD_COLS].astype(BF16), jnp.concatenate([w_t[OD_COLS:], pad], axis=0).astype(BF16)


def _even_mixer(h, rel_bias, w_in, conv_w, a_log, dt_bias, norm_w, w_out, bsz, seq):
    w_a, w_b = _even_w_in(w_in)
    proj_a, proj_b = _even_in_proj(h, w_a, w_b, conv_w, seq, PROJ_TM)
    gate_pad = jnp.zeros((1, LANES), F32)
    alog_pad = lax.dynamic_update_slice(gate_pad, a_log[None].astype(F32), (0, GDN_HEADS))
    dtb_pad = lax.dynamic_update_slice(gate_pad, dt_bias[None].astype(F32), (0, GDN_HEADS))
    o_a = _gdn(proj_a, proj_b, alog_pad, dtb_pad, norm_w[None], bsz, seq)
    tiles = _bias_tiles(rel_bias, *_swa_bias_tables(), scale=LOG2E)
    ng, two_blk = len(SWA_CONFIGS), 2 * SWA_BLOCK
    bias = tiles.reshape(SWA_HEADS // 2, 2, 2, ng, SWA_BLOCK, two_blk).transpose(3, 0, 2, 1, 4, 5)
    o_b = _swa(proj_b, bias.reshape(ng, SWA_HEADS // 2, 2, two_blk, two_blk), bsz, seq)
    w_out = w_out.astype(BF16)
    return o_a, o_b, w_out[:EVA_PART], w_out[EVA_PART:]


def _odd_mixer(h, rel_bias, w_in, lam_params, diff_norm_w, w_gate, b_gate, gla_norm_w, w_out,
               lam_init, bsz, seq):
    w_main, w_gd = _odd_w_in(w_in)
    proj, gate_in = _odd_in_proj(h, w_main, w_gd, PROJ_TM)
    blk = min(DIFF_BLOCK, seq)
    bucket, neg = _diff_bias_tables(blk)
    bias = _bias_tiles(rel_bias, bucket, neg, scale=LOG2E, base_bucket=NUM_BUCKETS - 1, toeplitz=True)
    o_c = _diff_attention(proj, bias, lam_params, diff_norm_w[None], lam_init, bsz, seq, blk)
    w_gate_pad = jnp.concatenate(
        [w_gate, jnp.zeros((LANES - GLA_RANK, w_gate.shape[1]), w_gate.dtype)], axis=0)
    o_d = _gla(proj, gate_in, w_gate_pad, b_gate[None], gla_norm_w[None], bsz, seq)
    diff_v = DIFF_HEADS * 2 * DIFF_DH
    w_out = w_out.astype(BF16)
    return o_c, o_d, w_out[:diff_v], w_out[diff_v:]


def kernel(x, rel_bias, w_in_even, gdn_conv_w, gdn_a_log, gdn_dt_bias, gdn_norm_w, w_out_even,
           w_in_odd, diff_lambda, diff_norm_w, gla_w_gate, gla_b_gate, gla_norm_w, w_out_odd,
           ffn_w_up, ffn_conv_w, ffn_conv_b, ffn_w_down, ln_g, ln_b):
    bsz, seq, d = x.shape
    h = x.reshape(bsz * seq, d)
    w_up, w_down = ffn_w_up.astype(BF16), ffn_w_down.astype(BF16)
    for layer in range(DEPTH):
        i = layer // 2
        if layer % 2 == 0:
            mixed = _even_mixer(h, rel_bias, w_in_even[i], gdn_conv_w[i], gdn_a_log[i], gdn_dt_bias[i],
                                gdn_norm_w[i], w_out_even[i], bsz, seq)
        else:
            lam_init = 0.8 - 0.6 * math.exp(-0.3 * layer)
            mixed = _odd_mixer(h, rel_bias, w_in_odd[i], diff_lambda[i], diff_norm_w[i], gla_w_gate[i],
                               gla_b_gate[i], gla_norm_w[i], w_out_odd[i], lam_init, bsz, seq)
        h = _layer_tail(*mixed[:2], h, *mixed[2:], ln_g[layer, 0][None], ln_b[layer, 0][None],
                        w_up, ffn_conv_w[layer], ffn_conv_b[layer][None], w_down,
                        ln_g[layer, 1][None], ln_b[layer, 1][None], layer, seq)
    return h.reshape(bsz, seq, d)
```

```python
import functools
import math

import numpy as np
import jax
import jax.numpy as jnp
from jax import lax
from jax.experimental import pallas as pl
from jax.experimental.pallas import tpu as pltpu

F32 = jnp.float32
BF16 = jnp.bfloat16
HI = lax.Precision.HIGHEST

D_MODEL = 1024
DEPTH = 2
DEEPNORM_ALPHA = (2 * DEPTH) ** 0.25
LN_EPS = 1e-5
RMS_EPS = 1e-6
NUM_BUCKETS = 32
REL_MAX_DIST = 2048
GDN_HEADS = 6
GDN_D = 128
CHUNK = 64
GDN_GROUP = 256
GDN_STEP_GROUPS = 2
SWA_CONFIGS = ((128, 1), (512, 4), (2048, 16))
SWA_HEADS = 4
SWA_DH = 64
SWA_BLOCK = 128
SWA_UNROLL = 8
SWA_SPAN = SWA_BLOCK * max(d for _, d in SWA_CONFIGS)
DIFF_HEADS = 4
DIFF_DH = 64
DIFF_BLOCK = 512
LOG2E = math.log2(math.e)
GLA_HEADS = 4
GLA_DK = 64
GLA_DV = 128
GLA_RANK = 16
GLA_TAU = 16.0
GLA_SUB = 16
GLA_GROUP = 256
GLA_STEP_GROUPS = 4
GLA_MAX_DECAY = 60.0
D_FF = 2816

LANES = 128
SUBLANES = 8
HALO = 8
VMEM_LIMIT = 56 * 1024 * 1024
NEG_BIG = -1e30

EVA_PART = GDN_HEADS * GDN_D
EVB_BA = 2304
EVB_COLS = 2560
EV_TN = 256
OD_QC, OD_KC, OD_VC = 0, 512, 1024
OD_QD, OD_KD, OD_VD, OD_RD = 1536, 1792, 2048, 2560
OD_COLS = 3072
OD_TN = 768
PROJ_TM = 512
TAIL_TM = 512
FFN_TC = 256


def _cparams(sem):
    return pltpu.CompilerParams(dimension_semantics=sem, vmem_limit_bytes=VMEM_LIMIT)


def _bdot(a, b):
    return jnp.dot(a.astype(BF16), b.astype(BF16), preferred_element_type=F32)


def _dot_nt(a, b):
    return lax.dot_general(a, b, (((1,), (1,)), ((), ())), preferred_element_type=F32)


def _bdot_nt(a, b):
    return lax.dot_general(a.astype(BF16), b.astype(BF16), (((1,), (1,)), ((), ())),
                           preferred_element_type=F32)


def _bdot_tn(a, b):
    return lax.dot_general(a.astype(BF16), b.astype(BF16), (((0,), (0,)), ((), ())),
                           preferred_element_type=F32)


def _sigmoid(x):
    return 1.0 / (1.0 + jnp.exp(-x))


def _silu(x):
    return x * _sigmoid(x)


def _softplus(x):
    return jnp.maximum(x, 0.0) + jnp.log1p(jnp.exp(-jnp.abs(x)))


def _log_sigmoid(x):
    return -_softplus(-x)


def _resident(a, lead=None):
    if lead is None:
        return pl.BlockSpec(a.shape, lambda i: (0,) * a.ndim, pipeline_mode=pl.Buffered(1))
    return pl.BlockSpec((None,) + a.shape[1:], lambda i: (lead,) + (0,) * (a.ndim - 1),
                        pipeline_mode=pl.Buffered(1))


def _odd_in_body(x_ref, w_ref, wg_ref, o_ref, og_ref):
    xb = x_ref[...].astype(BF16)
    for c in range(OD_COLS // OD_TN):
        cols = slice(c * OD_TN, (c + 1) * OD_TN)
        o_ref[:, cols] = _dot_nt(xb, w_ref[cols, :]).astype(o_ref.dtype)
    og_ref[...] = _dot_nt(xb, wg_ref[...])


def _odd_in_proj(x, w, w_gate_in, tm):
    t, k = x.shape
    return pl.pallas_call(
        _odd_in_body,
        grid=(t // tm,),
        in_specs=[pl.BlockSpec((tm, k), lambda i: (i, 0)), _resident(w), _resident(w_gate_in)],
        out_specs=[pl.BlockSpec((tm, OD_COLS), lambda i: (i, 0)), pl.BlockSpec((tm, LANES), lambda i: (i, 0))],
        out_shape=[jax.ShapeDtypeStruct((t, OD_COLS), BF16), jax.ShapeDtypeStruct((t, LANES), F32)],
        compiler_params=_cparams(("parallel",)),
        name="odd_in_proj",
    )(x, w, w_gate_in)


def _rel_bucket_np(dist):
    max_exact = NUM_BUCKETS // 2
    d = np.maximum(dist, 1).astype(np.float32)
    large = max_exact + (np.log(d / max_exact) / math.log(REL_MAX_DIST / max_exact)
                         * (NUM_BUCKETS - max_exact)).astype(np.int32)
    large = np.minimum(large, NUM_BUCKETS - 1)
    return np.where(dist < max_exact, dist, large).astype(np.int32)


def _bias_body(scale, base_bucket, tile_buckets, toeplitz, rb_ref, bucket_ref, neg_ref, o_ref):
    h = pl.program_id(0)
    t = pl.program_id(1)
    base = 0.0 if base_bucket is None else rb_ref[base_bucket, h]
    for tile, present in enumerate(tile_buckets):
        @pl.when(t == tile)
        def _(present=present):
            bucket = bucket_ref[0]
            acc = neg_ref[0]
            for b in present:
                acc = acc + jnp.where(bucket == b, (rb_ref[b, h] - base) * scale, 0.0)
            if toeplitz:
                n = o_ref.shape[2]
                rows = jnp.broadcast_to(acc[0:1], (n, 2 * n))
                acc = pltpu.roll(rows, n, 1, stride=1, stride_axis=0)[:, :n]
            o_ref[0, 0] = acc


def _bias_tiles(rel_bias, bucket, neg, scale=1.0, base_bucket=None, toeplitz=False):
    nt, r, c = bucket.shape
    nh = rel_bias.shape[1]
    out_r, out_c = (c // 2, c // 2) if toeplitz else (r, c)
    tile_buckets = tuple(tuple(int(b) for b in np.unique(bucket[t][neg[t] == 0]) if b != base_bucket)
                         for t in range(nt))
    return pl.pallas_call(
        functools.partial(_bias_body, scale, base_bucket, tile_buckets, toeplitz),
        grid=(nh, nt),
        in_specs=[pl.BlockSpec(memory_space=pltpu.SMEM),
                  pl.BlockSpec((1, r, c), lambda h, t: (t, 0, 0)),
                  pl.BlockSpec((1, r, c), lambda h, t: (t, 0, 0))],
        out_specs=pl.BlockSpec((1, 1, out_r, out_c), lambda h, t: (h, t, 0, 0)),
        out_shape=jax.ShapeDtypeStruct((nh, nt, out_r, out_c), F32),
        compiler_params=_cparams(("parallel", "parallel")),
        name="rel_bias_tiles",
    )(rel_bias, jnp.asarray(bucket), jnp.asarray(neg))


def _causal_conv(cur, w, scratch, carry, at_start):
    width = w.shape[0]
    rows = cur.shape[0]
    scratch[0:HALO, :] = jnp.where(at_start, 0.0, carry[...])
    scratch[HALO:, :] = cur
    carry[...] = cur[rows - HALO:]
    y = w[width - 1:width, :] * cur
    for j in range(width - 1):
        back = width - 1 - j
        y = y + w[j:j + 1, :] * scratch[HALO - back:HALO - back + rows, :]
    return y


def _even_in_body(tiles_per_seq, x_ref, wa_ref, wb_ref, cw_ref, oa_ref, ob_ref, conv_in):
    rows = x_ref.shape[0]
    width = cw_ref.shape[0]
    conv_cols = 3 * EVA_PART
    at_start = pl.program_id(0) % tiles_per_seq == 0
    conv_in[0:HALO, :] = jnp.where(at_start, 0.0, conv_in[rows:rows + HALO, :])
    x = x_ref[...].astype(BF16)

    def project_conv_in(c):
        cols = slice(c * EV_TN, (c + 1) * EV_TN)
        conv_in[HALO:, cols] = _dot_nt(x, wa_ref[cols, :])

    def project_z(c):
        cols = slice(conv_cols + c * EV_TN, conv_cols + (c + 1) * EV_TN)
        oa_ref[:, cols] = _dot_nt(x, wa_ref[cols, :]).astype(oa_ref.dtype)

    def project_b(c):
        cols = slice(c * EV_TN, (c + 1) * EV_TN)
        ob_ref[:, cols] = _dot_nt(x, wb_ref[cols, :])

    def finish_head(part, hd):
        cols = slice(part * EVA_PART + hd * GDN_D, part * EVA_PART + (hd + 1) * GDN_D)
        y = cw_ref[width - 1:width, cols] * conv_in[HALO:, cols]
        for j in range(width - 1):
            back = width - 1 - j
            y = y + cw_ref[j:j + 1, cols] * conv_in[HALO - back:HALO - back + rows, cols]
        ch = _silu(y)
        if part < 2:
            inv = lax.rsqrt(jnp.sum(ch * ch, axis=-1, keepdims=True) + RMS_EPS)
            ch = ch * (inv * GDN_D ** -0.5 if part == 0 else inv)
        oa_ref[:, cols] = ch.astype(oa_ref.dtype)

    plain = ([functools.partial(project_b, c) for c in range(EVB_COLS // EV_TN)]
             + [functools.partial(project_z, c) for c in range(EVA_PART // EV_TN)])
    heads_per_block = EV_TN // GDN_D
    n_conv = conv_cols // EV_TN
    project_conv_in(0)
    for c in range(n_conv):
        matmuls = ([functools.partial(project_conv_in, c + 1)] if c + 1 < n_conv else [])
        share = -(-len(plain) // (n_conv - c))
        matmuls, plain = matmuls + plain[:share], plain[share:]
        for i in range(max(len(matmuls), heads_per_block)):
            if i < len(matmuls):
                matmuls[i]()
            if i < heads_per_block:
                head = c * heads_per_block + i
                finish_head(head // GDN_HEADS, head % GDN_HEADS)


def _even_in_proj(x, w_a, w_b, conv_w, seq, tm):
    t, k = x.shape
    return pl.pallas_call(
        functools.partial(_even_in_body, seq // tm),
        grid=(t // tm,),
        in_specs=[pl.BlockSpec((tm, k), lambda i: (i, 0)),
                  _resident(w_a), _resident(w_b), _resident(conv_w)],
        out_specs=[pl.BlockSpec((tm, w_a.shape[0]), lambda i: (i, 0)),
                   pl.BlockSpec((tm, EVB_COLS), lambda i: (i, 0))],
        out_shape=[jax.ShapeDtypeStruct((t, w_a.shape[0]), BF16), jax.ShapeDtypeStruct((t, EVB_COLS), F32)],
        scratch_shapes=[pltpu.VMEM((HALO + tm, 3 * EVA_PART), F32)],
        compiler_params=_cparams(("arbitrary",)),
        name="even_in_proj",
    )(x, w_a, w_b, conv_w)


def _gdn_body(q_ref, k_ref, v_ref, z_ref, ba_ref, alog_ref, dtb_ref, nw_ref, o_ref, state):
    n = pl.program_id(1)

    @pl.when(n == 0)
    def _():
        state[...] = jnp.zeros_like(state)

    grp = GDN_GROUP
    nc = grp // CHUNK
    nh = GDN_HEADS
    hs = range(nh)
    ri = lax.broadcasted_iota(jnp.int32, (grp, grp), 0)
    ci = lax.broadcasted_iota(jnp.int32, (grp, grp), 1)
    same = (ri // CHUNK) == (ci // CHUNK)
    incl = same & (ri >= ci)
    strict = same & (ri > ci)
    eye = (ri == ci).astype(F32)
    tri = incl.astype(BF16)

    def chunk_local(groups):
        units = [(g, h) for g in range(len(groups)) for h in hs]
        gam_all, gam_rows, beta_all = [], [], []
        for rows in groups:
            ba = ba_ref[rows, :]
            beta_all.append(_sigmoid(ba))
            g_all = -jnp.exp(alog_ref[...]) * _softplus(ba + dtb_ref[...])
            g_hi = g_all.astype(BF16)
            rem = g_all - g_hi.astype(F32)
            g_mid = rem.astype(BF16)
            g_lo = (rem - g_mid.astype(F32)).astype(BF16)
            gam_all.append(jnp.dot(tri, g_hi, preferred_element_type=F32)
                           + jnp.dot(tri, g_mid, preferred_element_type=F32)
                           + jnp.dot(tri, g_lo, preferred_element_type=F32))
            gam_rows.append(gam_all[-1].T)

        def head(ref, g, h):
            return ref[groups[g], h * GDN_D:(h + 1) * GDN_D]

        q = [head(q_ref, g, h) for g, h in units]
        k = [head(k_ref, g, h) for g, h in units]
        v = [head(v_ref, g, h) for g, h in units]
        gam = [jnp.broadcast_to(gam_all[g][:, nh + h:nh + h + 1], (grp, GDN_D)) for g, h in units]
        bcol = [beta_all[g][:, h:h + 1] for g, h in units]
        us = range(len(units))
        decay, x, inv = [], [], []
        for i, (g, h) in enumerate(units):
            diff = gam[i][:, 0:1] - gam_rows[g][nh + h:nh + h + 1, :]
            decay.append(jnp.where(incl, jnp.exp(jnp.where(incl, diff, 0.0)), 0.0))
            kk = _bdot_nt(k[i], k[i])
            x.append(-jnp.where(strict, bcol[i] * kk * decay[i], 0.0))
            inv.append(eye + x[i])
        for _ in range(5):
            for i in us:
                x[i] = _bdot(x[i], x[i])
                inv[i] = inv[i] + _bdot(inv[i], x[i])
        out = [tuple([] for _ in range(6)) for _ in groups]
        for i, (g, h) in enumerate(units):
            eg = jnp.exp(gam[i])
            uw = _bdot(inv[i], jnp.concatenate([v[i] * bcol[i], k[i] * (bcol[i] * eg)], axis=1))
            kd, gl = [], []
            for c in range(nc):
                last = gam[i][(c + 1) * CHUNK - 1:(c + 1) * CHUNK, :]
                kd.append(k[i][c * CHUNK:(c + 1) * CHUNK] * jnp.exp(last - gam[i][c * CHUNK:(c + 1) * CHUNK]))
                gl.append(jnp.exp(last))
            for lst, val in zip(out[g], (uw[:, :GDN_D], uw[:, GDN_D:], _bdot_nt(q[i], k[i]) * decay[i],
                                         q[i] * eg, kd, gl)):
                lst.append(val)
        return out

    def recurrence(rows, local, s):
        u, w, qk, q_dec, k_dec, g_last = local
        q_s = [[] for _ in hs]
        delta = [[] for _ in hs]
        for c in range(nc):
            sl = slice(c * CHUNK, (c + 1) * CHUNK)
            for h in hs:
                r = _bdot(jnp.concatenate([w[h][sl], q_dec[h][sl]], axis=0), s[h])
                d = u[h][sl] - r[:CHUNK]
                q_s[h].append(r[CHUNK:])
                delta[h].append(d)
                s[h] = g_last[h][c] * s[h] + _bdot_tn(k_dec[h][c], d)
        for h in hs:
            o = jnp.concatenate(q_s[h], axis=0) + _bdot(qk[h], jnp.concatenate(delta[h], axis=0))
            o = o * lax.rsqrt(jnp.mean(o * o, axis=-1, keepdims=True) + RMS_EPS) * nw_ref[...]
            z = z_ref[rows, h * GDN_D:(h + 1) * GDN_D].astype(F32)
            o_ref[rows, h * GDN_D:(h + 1) * GDN_D] = (o * _silu(z)).astype(o_ref.dtype)
        return s

    groups = [slice(g * grp, (g + 1) * grp) for g in range(GDN_STEP_GROUPS)]
    s = [state[h] for h in hs]
    for rows, local in zip(groups, chunk_local(groups)):
        s = recurrence(rows, local, s)
    for h in hs:
        state[h] = s[h]


def _gdn(proj_a, proj_b, alog_pad, dtb_pad, norm_w, bsz, seq):
    t = proj_a.shape[0]
    rows = GDN_GROUP * GDN_STEP_GROUPS
    spb = seq // rows
    width = GDN_HEADS * GDN_D

    def at(col):
        return lambda b, n: (b * spb + n, col)

    return pl.pallas_call(
        _gdn_body,
        grid=(bsz, spb),
        in_specs=[pl.BlockSpec((rows, width), at(0)),
                  pl.BlockSpec((rows, width), at(1)),
                  pl.BlockSpec((rows, width), at(2)),
                  pl.BlockSpec((rows, width), at(3)),
                  pl.BlockSpec((rows, LANES), at(EVB_BA // LANES)),
                  pl.BlockSpec((1, LANES), lambda b, n: (0, 0)),
                  pl.BlockSpec((1, LANES), lambda b, n: (0, 0)),
                  pl.BlockSpec((1, LANES), lambda b, n: (0, 0))],
        out_specs=pl.BlockSpec((rows, width), at(0)),
        out_shape=jax.ShapeDtypeStruct((t, width), BF16),
        scratch_shapes=[pltpu.VMEM((GDN_HEADS, GDN_D, GDN_D), F32)],
        compiler_params=_cparams(("parallel", "arbitrary")),
        name="gdn",
    )(proj_a, proj_a, proj_a, proj_a, proj_b, alog_pad, dtb_pad, norm_w)


def _swa_bias_tables():
    qi = np.arange(SWA_BLOCK)[:, None] + SWA_BLOCK
    kj = np.arange(2 * SWA_BLOCK)[None, :]
    rel = qi - kj
    buckets, negs = [], []
    for has_prev in (True, False):
        for window, dilation in SWA_CONFIGS:
            valid = (rel >= 0) & (rel <= window // dilation) & (has_prev | (kj >= SWA_BLOCK))
            buckets.append(_rel_bucket_np(np.maximum(rel, 0) * dilation))
            negs.append(np.where(valid, 0.0, NEG_BIG))
    return np.stack(buckets).astype(np.int32), np.stack(negs).astype(np.float32)


def _swa_body(*refs):
    ng = len(SWA_CONFIGS)
    ins, bias_ref, o_ref = refs[:5 * ng], refs[5 * ng], refs[5 * ng + 1]
    scratch = refs[5 * ng + 2:]
    kbufs, vbufs, o_scr, lse_scr = scratch[:ng], scratch[ng:2 * ng], scratch[2 * ng], scratch[2 * ng + 1]
    j = pl.program_id(1)
    blk = SWA_BLOCK
    first_head = lax.broadcasted_iota(jnp.int32, (blk, LANES), 1) < SWA_DH

    for g, (_, d) in enumerate(SWA_CONFIGS):
        q_ref, kc_ref, kp_ref, vc_ref, vp_ref = ins[5 * g:5 * g + 5]
        kbuf, vbuf = kbufs[g], vbufs[g]
        halo = blk * d
        kbuf[0:halo, :] = kp_ref[...]
        kbuf[halo:, :] = kc_ref[...]
        vbuf[0:halo, :] = vp_ref[...]
        vbuf[halo:, :] = vc_ref[...]

        def unit(u, carry, g=g, d=d, halo=halo, q_ref=q_ref, kbuf=kbuf, vbuf=vbuf):
            base = (u // d) * halo + u % d
            q = q_ref[pl.ds(base, blk, stride=d), :] * (SWA_DH ** -0.5 * LOG2E)
            k = kbuf[pl.ds(base, 2 * blk, stride=d), :]
            v = vbuf[pl.ds(base, 2 * blk, stride=d), :]
            lhs = jnp.concatenate([jnp.where(first_head, q, 0.0), jnp.where(first_head, 0.0, q)], axis=0)
            no_prev = jnp.where((j == 0) & (u < d), 1, 0)
            s = _bdot_nt(lhs, k) + bias_ref[g, 0, no_prev]
            m = jnp.max(s, axis=-1, keepdims=True)
            p = jnp.exp2(s - m)
            l = jnp.sum(p, axis=-1, keepdims=True)
            o2 = _bdot(p, v) * (1.0 / l)
            lse2 = m + jnp.log2(l)
            o_scr[g, pl.ds(base, blk, stride=d), :] = jnp.where(first_head, o2[:blk], o2[blk:])
            lse_scr[g, pl.ds(base, blk, stride=d), :] = jnp.where(first_head, lse2[:blk], lse2[blk:])
            return carry

        lax.fori_loop(0, SWA_SPAN // blk, unit, 0, unroll=SWA_UNROLL)

    def combine(c, carry):
        rows = pl.ds(pl.multiple_of(c * 2 * blk, 2 * blk), 2 * blk)
        lse = [lse_scr[g, rows, :] for g in range(ng)]
        m = functools.reduce(jnp.maximum, lse)
        e = [jnp.exp2(x - m) for x in lse]
        den = functools.reduce(lambda x, y: x + y, e)
        o_ref[rows, :] = functools.reduce(
            lambda x, y: x + y, [(e[g] / den) * o_scr[g, rows, :] for g in range(ng)]).astype(o_ref.dtype)
        return carry

    lax.fori_loop(0, SWA_SPAN // (2 * blk), combine, 0)


def _swa(proj, bias, bsz, seq):
    t = proj.shape[0]
    ng = len(SWA_CONFIGS)
    nspan = seq // SWA_SPAN
    npair = SWA_HEADS * SWA_DH // LANES
    group_cols = SWA_HEADS * SWA_DH // LANES
    in_specs, scratch_k = [], []
    for g, (_, d) in enumerate(SWA_CONFIGS):
        halo = SWA_BLOCK * d
        per_span = SWA_SPAN // halo

        def cur(which, g=g):
            col = (which * ng + g) * group_cols
            return lambda b, j, p: (b * nspan + j, col + p)

        def prev(which, g=g, per_span=per_span):
            col = (which * ng + g) * group_cols
            return lambda b, j, p: (jnp.maximum((b * nspan + j) * per_span - 1, 0), col + p)

        in_specs += [pl.BlockSpec((SWA_SPAN, LANES), cur(0)),
                     pl.BlockSpec((SWA_SPAN, LANES), cur(1)), pl.BlockSpec((halo, LANES), prev(1)),
                     pl.BlockSpec((SWA_SPAN, LANES), cur(2)), pl.BlockSpec((halo, LANES), prev(2))]
        scratch_k.append(pltpu.VMEM((halo + SWA_SPAN, LANES), F32))
    in_specs.append(pl.BlockSpec((ng, 1, 2, 2 * SWA_BLOCK, 2 * SWA_BLOCK), lambda b, j, p: (0, p, 0, 0, 0)))
    return pl.pallas_call(
        _swa_body,
        grid=(bsz, nspan, npair),
        in_specs=in_specs,
        out_specs=pl.BlockSpec((SWA_SPAN, LANES), lambda b, j, p: (b * nspan + j, p)),
        out_shape=jax.ShapeDtypeStruct((t, npair * LANES), BF16),
        scratch_shapes=scratch_k + scratch_k + [pltpu.VMEM((ng, SWA_SPAN, LANES), F32)] * 2,
        compiler_params=_cparams(("parallel", "parallel", "parallel")),
        name="swa",
    )(*([proj] * (5 * ng)), bias)


def _layer_norm(y, g, b):
    mu = jnp.mean(y, axis=-1, keepdims=True)
    yc = y - mu
    var = jnp.mean(yc * yc, axis=-1, keepdims=True)
    return yc * lax.rsqrt(var + LN_EPS) * g + b


def _diff_bias_tables(blk):
    buckets = _rel_bucket_np(np.arange(2 * REL_MAX_DIST))
    far = int(np.max(np.nonzero(buckets != NUM_BUCKETS - 1)[0])) + 1
    nb = -(-(far + blk - 1) // blk)
    dist = np.stack([t * blk - blk + np.arange(2 * blk) for t in range(nb + 1)])[:, None, :]
    dist = np.broadcast_to(dist, (nb + 1, SUBLANES, 2 * blk))
    bucket = _rel_bucket_np(np.maximum(dist, 0))
    neg = np.where(dist >= 0, 0.0, NEG_BIG).astype(np.float32)
    return bucket, neg


def _diff_body(blk, nb, lam_init, q_ref, qn_ref, k_ref, v_ref, bias_ref, lam_ref, nw_ref, o_ref,
               vt, acc1, acc2, s_a, s_b):
    qi = pl.program_id(2)
    dv = 2 * DIFF_DH
    seq = k_ref.shape[0]

    @pl.when(qi == 0)
    def _():
        vt[dv:, :] = jnp.ones((vt.shape[0] - dv, seq), BF16)

        def fill(c, carry):
            st = pl.multiple_of(c * blk, blk)
            vt[0:dv, pl.ds(st, blk)] = v_ref[pl.ds(st, blk), :].astype(F32).T.astype(BF16)
            return carry

        lax.fori_loop(0, seq // blk, fill, 0)

    feature = lax.broadcasted_iota(jnp.int32, (dv, blk), 0)

    def components(ref):
        q_t = (ref[...].astype(F32) * (DIFF_DH ** -0.5 * LOG2E)).T
        return (jnp.where(feature < DIFF_DH, q_t, 0.0).astype(BF16),
                jnp.where(feature >= DIFF_DH, q_t, 0.0).astype(BF16))

    qs = components(q_ref)
    qs_next = components(qn_ref)
    accs = (acc1, acc2)
    for acc in accs:
        acc[...] = jnp.zeros_like(acc)

    last = pl.num_programs(2) - 1

    def key_rows(kj):
        return pl.ds(pl.multiple_of(jnp.minimum(kj, last) * blk, blk), blk)

    def scores(kj, dst, queries=qs, offset=None):
        k = k_ref[key_rows(kj), :]
        for c, qc in enumerate(queries):
            s = jnp.dot(k, qc, preferred_element_type=F32)
            dst[c] = s if offset is None else s + bias_ref[0, jnp.minimum(offset, nb)]

    def consume(kj, src, ms):
        vtb = vt[:, key_rows(kj)]
        out = []
        for c, (m, acc) in enumerate(zip(ms, accs)):
            halves = []
            for h in range(2):
                cols = slice(h * blk // 2, (h + 1) * blk // 2)
                s = src[c, :, cols]
                m_new = jnp.maximum(m[:, cols], jnp.max(s, axis=0, keepdims=True))
                p = jnp.exp2(s - m_new).astype(BF16)
                acc[:, cols] = (jnp.exp2(m[:, cols] - m_new) * acc[:, cols]
                                + jnp.dot(vtb, p, preferred_element_type=F32))
                halves.append(m_new)
            out.append(jnp.concatenate(halves, axis=1))
        return tuple(out)

    nblocks = qi + 1
    trips = nblocks // 2
    odd = nblocks % 2 == 1
    far_trips = jnp.maximum(qi - nb, 0) // 2

    def pair(biased, t, ms):
        kj = 2 * t
        scores(kj + 1, s_b, offset=qi - (kj + 1) if biased else None)
        ms = consume(kj, s_a, ms)
        if biased:
            hand_off = (t == trips - 1) & jnp.logical_not(odd)
            scores(jnp.where(hand_off, 0, kj + 2), s_a,
                   tuple(jnp.where(hand_off, qn, qc) for qc, qn in zip(qs, qs_next)),
                   offset=jnp.where(hand_off, qi + 1, qi - (kj + 2)))
        else:
            scores(kj + 2, s_a)
        return consume(kj + 1, s_b, ms)

    @pl.when(qi == 0)
    def _():
        scores(0, s_a, offset=0)

    m0 = jnp.full((1, blk), NEG_BIG, F32)
    ms = lax.fori_loop(0, far_trips, functools.partial(pair, False), (m0, m0))
    ms = lax.fori_loop(far_trips, trips, functools.partial(pair, True), ms)

    def finish():
        lp = lam_ref[...]
        lam = (jnp.exp(jnp.sum(lp[0:1] * lp[1:2], axis=-1, keepdims=True))
               - jnp.exp(jnp.sum(lp[2:3] * lp[3:4], axis=-1, keepdims=True)) + lam_init)
        a1, a2 = acc1[...], acc2[...]
        o_t = a1[:dv] * (1.0 / a1[dv:dv + 1]) - a2[:dv] * (lam / a2[dv:dv + 1])
        o = o_t.T
        o = o * lax.rsqrt(jnp.mean(o * o, axis=-1, keepdims=True) + RMS_EPS) * nw_ref[...]
        o_ref[...] = (o * (1.0 - lam_init)).astype(o_ref.dtype)

    @pl.when(odd)
    def _():
        consume(qi, s_a, ms)
        scores(0, s_a, qs_next, offset=qi + 1)
        finish()

    @pl.when(jnp.logical_not(odd))
    def _():
        finish()


def _diff_attention(proj, bias, lam_params, norm_w, lam_init, bsz, seq, blk):
    t = proj.shape[0]
    nq = seq // blk
    nt = bias.shape[1]
    nb = nt - 1
    width = 2 * DIFF_DH
    ones_rows = 2 * SUBLANES
    return pl.pallas_call(
        functools.partial(_diff_body, blk, nb, lam_init),
        grid=(bsz, DIFF_HEADS, nq),
        in_specs=[pl.BlockSpec((blk, width), lambda b, h, i: (b * nq + i, OD_QC // width + h)),
                  pl.BlockSpec((blk, width),
                               lambda b, h, i: (b * nq + jnp.minimum(i + 1, nq - 1), OD_QC // width + h)),
                  pl.BlockSpec((seq, width), lambda b, h, i: (b, OD_KC // width + h)),
                  pl.BlockSpec((seq, width), lambda b, h, i: (b, OD_VC // width + h)),
                  pl.BlockSpec((1, nt, blk, blk), lambda b, h, i: (h, 0, 0, 0)),
                  pl.BlockSpec((4, DIFF_DH), lambda b, h, i: (0, 0)),
                  pl.BlockSpec((1, width), lambda b, h, i: (0, 0))],
        out_specs=pl.BlockSpec((blk, width), lambda b, h, i: (b * nq + i, h)),
        out_shape=jax.ShapeDtypeStruct((t, DIFF_HEADS * width), BF16),
        scratch_shapes=[pltpu.VMEM((width + ones_rows, seq), BF16),
                        pltpu.VMEM((width + ones_rows, blk), F32), pltpu.VMEM((width + ones_rows, blk), F32),
                        pltpu.VMEM((2, blk, blk), F32), pltpu.VMEM((2, blk, blk), F32)],
        compiler_params=_cparams(("arbitrary", "arbitrary", "arbitrary")),
        name="diff_attn",
    )(proj, proj, proj, proj, bias, lam_params, norm_w)


def _gla_body(q_ref, k_ref, gd_ref, v_ref, r_ref, wg_ref, bg_ref, nw_ref, o_ref, state, part):
    n = pl.program_id(1)

    @pl.when(n == 0)
    def _():
        state[...] = jnp.zeros_like(state)

    rows = GLA_GROUP
    nc = rows // CHUNK
    npair = GLA_HEADS // 2
    groups = [slice(g * rows, (g + 1) * rows) for g in range(GLA_STEP_GROUPS)]
    gate = jnp.dot(gd_ref[...], wg_ref[...], precision=HI, preferred_element_type=F32) + bg_ref[...]
    log_a = _log_sigmoid(gate) * (1.0 / GLA_TAU)

    ri = lax.broadcasted_iota(jnp.int32, (rows, rows), 0)
    ci = lax.broadcasted_iota(jnp.int32, (rows, rows), 1)
    causal = ((ri // CHUNK) == (ci // CHUNK)) & (ri >= ci)
    tri = causal.astype(BF16)
    hi = log_a.astype(BF16)
    rem = log_a - hi.astype(F32)
    mid = rem.astype(BF16)
    lo = (rem - mid.astype(F32)).astype(BF16)
    b_all = [jnp.dot(tri, hi[gr], preferred_element_type=F32) + jnp.dot(tri, mid[gr], preferred_element_type=F32)
             + jnp.dot(tri, lo[gr], preferred_element_type=F32) for gr in groups]

    lane = lax.broadcasted_iota(jnp.int32, (rows, LANES), 1)
    lane_c = lax.broadcasted_iota(jnp.int32, (CHUNK, LANES), 1)
    head_lanes = (lane < GLA_DK, lane >= GLA_DK)
    chunk_lanes = (lane_c < GLA_DK, lane_c >= GLA_DK)
    pairs = []
    for gr, b_g in zip(groups, b_all):
        pairs.append([])
        for p in range(npair):
            cols = slice(p * LANES, (p + 1) * LANES)
            q = q_ref[gr, cols].astype(F32) * GLA_DK ** -0.5
            pairs[-1].append(dict(
                b=b_g[:, cols], q=q, q_dec=q * jnp.exp(b_g[:, cols]), k=k_ref[gr, cols].astype(F32),
                v=[v_ref[gr, (2 * p + hd) * GLA_DV:(2 * p + hd + 1) * GLA_DV].astype(F32) for hd in range(2)]))

    for g, gr in enumerate(groups):
        for c in range(nc):
            sl = slice(c * CHUNK, (c + 1) * CHUNK)
            out_rows = slice(g * rows + c * CHUNK, g * rows + (c + 1) * CHUNK)
            for p, pr in enumerate(pairs[g]):
                bc = pr["b"][sl]
                b_last = bc[CHUNK - 1:CHUNK]
                k_dec = pr["k"][sl] * jnp.exp(b_last - bc)
                e_last = jnp.exp(b_last)
                for hd in range(2):
                    h = 2 * p + hd
                    st = state[h]
                    part[h, out_rows, :] = _bdot_nt(jnp.where(chunk_lanes[hd], pr["q_dec"][sl], 0.0), st)
                    state[h] = st * e_last + _bdot_tn(pr["v"][hd][sl], k_dec)

    def finish(intra):
        for gr, intra_g in zip(groups, intra):
            for h in range(GLA_HEADS):
                o = part[h, gr, :] + intra_g[h]
                o = o * lax.rsqrt(jnp.mean(o * o, axis=-1, keepdims=True) + RMS_EPS) * nw_ref[...]
                gate_r = _silu(r_ref[gr, h * GLA_DV:(h + 1) * GLA_DV].astype(F32))
                o_ref[gr, h * GLA_DV:(h + 1) * GLA_DV] = (o * gate_r).astype(o_ref.dtype)

    def intra_whole_chunk():
        out = []
        for pairs_g in pairs:
            out.append([])
            for pr in pairs_g:
                k_inv = pr["k"] * jnp.exp(jnp.minimum(-pr["b"], GLA_MAX_DECAY))
                for hd in range(2):
                    a = _bdot_nt(jnp.where(head_lanes[hd], pr["q_dec"], 0.0), k_inv)
                    out[-1].append(_bdot(jnp.where(causal, a, 0.0), pr["v"][hd]))
        return out

    def intra_exact():
        out = []
        for pairs_g in pairs:
            out.append([])
            for pr in pairs_g:
                off = _gla_intra_off_diagonal(pr["q"], pr["k"], pr["b"], pr["v"], nc)
                diag = _gla_intra_diagonal(pr["q"], pr["k"], pr["b"], pr["v"])
                out[-1] += [off[hd] + diag[hd] for hd in range(2)]
        return out

    chunk_decay = functools.reduce(jnp.maximum, [
        jnp.max(-b_g.reshape(nc, CHUNK, npair * LANES)[:, CHUNK - 1:CHUNK, :]) for b_g in b_all])

    @pl.when(chunk_decay <= GLA_MAX_DECAY)
    def _():
        finish(intra_whole_chunk())

    @pl.when(chunk_decay > GLA_MAX_DECAY)
    def _():
        finish(intra_exact())


def _gla_intra_off_diagonal(q, k, b, vs, nc):
    per_chunk = CHUNK // GLA_SUB
    lane = lax.broadcasted_iota(jnp.int32, (GLA_SUB, LANES), 1)
    sub_mask = (lane < GLA_DK, lane >= GLA_DK)
    kcol = lax.broadcasted_iota(jnp.int32, (GLA_SUB, CHUNK), 1)
    outs = ([], [])
    for c in range(nc):
        sl = slice(c * CHUNK, (c + 1) * CHUNK)
        bc, qc, kc = b[sl], q[sl], k[sl]
        a_rows = [[jnp.zeros((GLA_SUB, CHUNK), F32)] for _ in vs]
        for blk in range(1, per_chunk):
            r0 = blk * GLA_SUB
            bref = bc[r0:r0 + 1]
            qs = qc[r0:r0 + GLA_SUB] * jnp.exp(bc[r0:r0 + GLA_SUB] - bref)
            ks = kc * jnp.exp(jnp.minimum(bref - bc, 0.0))
            for hd in range(2):
                a = _bdot_nt(jnp.where(sub_mask[hd], qs, 0.0), ks)
                a_rows[hd].append(jnp.where(kcol < r0, a, 0.0))
        for hd in range(2):
            outs[hd].append(_bdot(jnp.concatenate(a_rows[hd], axis=0), vs[hd][sl]))
    return [jnp.concatenate(o, axis=0) for o in outs]


def _gla_intra_diagonal(q, k, b, vs):
    rows = q.shape[0]
    nsub = rows // GLA_SUB
    b3 = b.reshape(nsub, GLA_SUB, LANES)
    q3 = q.reshape(nsub, GLA_SUB, LANES)
    k3 = k.reshape(nsub, GLA_SUB, LANES)
    v3 = [v.reshape(nsub, GLA_SUB, LANES) for v in vs]
    row3 = lax.broadcasted_iota(jnp.int32, (nsub, GLA_SUB, LANES), 1)
    lane3 = lax.broadcasted_iota(jnp.int32, (nsub, GLA_SUB, LANES), 2)
    rowc = lax.broadcasted_iota(jnp.int32, (nsub, GLA_SUB, 1), 1)
    o3 = [jnp.zeros((nsub, GLA_SUB, LANES), F32) for _ in vs]
    for jj in range(GLA_SUB):
        e = jnp.exp(jnp.where(row3 >= jj, b3 - b3[:, jj:jj + 1, :], 0.0))
        t = q3 * k3[:, jj:jj + 1, :] * e
        w_all = jnp.sum(t, axis=-1, keepdims=True)
        w_a = jnp.sum(jnp.where(lane3 < GLA_DK, t, 0.0), axis=-1, keepdims=True)
        for hd, w in enumerate((w_a, w_all - w_a)):
            o3[hd] = o3[hd] + jnp.where(rowc >= jj, w, 0.0) * v3[hd][:, jj:jj + 1, :]
    return [o.reshape(rows, LANES) for o in o3]


def _gla(proj, gate_in, w_gate_pad, b_gate, norm_w, bsz, seq):
    t = proj.shape[0]
    rows = GLA_GROUP * GLA_STEP_GROUPS
    spb = seq // rows
    qk_w = GLA_HEADS * GLA_DK
    v_w = GLA_HEADS * GLA_DV

    def at(col):
        return lambda b, n: (b * spb + n, col)

    def whole(a):
        return pl.BlockSpec(a.shape, lambda b, n: (0, 0))

    return pl.pallas_call(
        _gla_body,
        grid=(bsz, spb),
        in_specs=[pl.BlockSpec((rows, qk_w), at(OD_QD // qk_w)),
                  pl.BlockSpec((rows, qk_w), at(OD_KD // qk_w)),
                  pl.BlockSpec((rows, LANES), at(0)),
                  pl.BlockSpec((rows, v_w), at(OD_VD // v_w)),
                  pl.BlockSpec((rows, v_w), at(OD_RD // v_w)),
                  whole(w_gate_pad), whole(b_gate), whole(norm_w)],
        out_specs=pl.BlockSpec((rows, v_w), at(0)),
        out_shape=jax.ShapeDtypeStruct((t, v_w), BF16),
        scratch_shapes=[pltpu.VMEM((GLA_HEADS, GLA_DV, LANES), F32), pltpu.VMEM((GLA_HEADS, rows, GLA_DV), F32)],
        compiler_params=_cparams(("parallel", "arbitrary")),
        name="gla",
    )(proj, proj, gate_in, proj, proj, w_gate_pad, b_gate, norm_w)


def _tail_body(tiles_per_seq, tc, ma_ref, mb_ref, h_ref, woa_ref, wob_ref,
               g1_ref, b1_ref, wu_ref, cw_ref, cb_ref, wd_ref, g2_ref, b2_ref, y_ref,
               act, scratch_g, scratch_v, carry):
    at_start = pl.program_id(0) % tiles_per_seq == 0
    mix = (jnp.dot(ma_ref[...], woa_ref[...], preferred_element_type=F32)
           + jnp.dot(mb_ref[...], wob_ref[...], preferred_element_type=F32))
    x = _layer_norm(DEEPNORM_ALPHA * h_ref[...] + mix, g1_ref[...], b1_ref[...])
    xb = x.astype(BF16)
    nchunk = D_FF // tc

    def project(idx):
        return jnp.dot(xb, wu_ref[:, idx * tc:(idx + 1) * tc], preferred_element_type=F32)

    def branch(cur, idx, scratch, h):
        lo, cols = idx * tc + h * LANES, slice(h * LANES, (h + 1) * LANES)
        half = cur[:, cols]
        rows, width = half.shape[0], cw_ref.shape[0]
        scratch[0:HALO, cols] = jnp.where(at_start, 0.0, carry[idx, :, cols])
        scratch[HALO:, cols] = half
        carry[idx, :, cols] = half[rows - HALO:]
        y = cw_ref[width - 1:width, lo:lo + LANES] * half
        for j in range(width - 1):
            back = width - 1 - j
            y = y + cw_ref[j:j + 1, lo:lo + LANES] * scratch[HALO - back:HALO - back + rows, cols]
        return y + cb_ref[:, lo:lo + LANES]

    for c in range(nchunk):
        gate_in, val_in = project(c), project(nchunk + c)
        for h in range(tc // LANES):
            gate = branch(gate_in, c, scratch_g, h)
            val = branch(val_in, nchunk + c, scratch_v, h)
            lo = c * tc + h * LANES
            act[:, lo:lo + LANES] = (_silu(gate) * val).astype(BF16)
    ffn = jnp.dot(act[...], wd_ref[...], preferred_element_type=F32)
    y_ref[...] = _layer_norm(DEEPNORM_ALPHA * x + ffn, g2_ref[...], b2_ref[...])


def _layer_tail(mix_a, mix_b, h, w_out_a, w_out_b, g1, b1, w_up, conv_w, conv_b, w_down, g2, b2,
                layer, seq):
    t = h.shape[0]
    tm, tc = TAIL_TM, FFN_TC

    def tile(a):
        return pl.BlockSpec((tm, a.shape[1]), lambda i: (i, 0))

    params = (w_out_a, w_out_b, g1, b1, w_up, conv_w, conv_b, w_down, g2, b2)
    param_specs = [_resident(w_out_a), _resident(w_out_b), _resident(g1), _resident(b1),
                   _resident(w_up, layer), _resident(conv_w), _resident(conv_b), _resident(w_down, layer),
                   _resident(g2), _resident(b2)]
    return pl.pallas_call(
        functools.partial(_tail_body, seq // tm, tc),
        grid=(t // tm,),
        in_specs=[tile(mix_a), tile(mix_b), tile(h)] + param_specs,
        out_specs=pl.BlockSpec((tm, D_MODEL), lambda i: (i, 0)),
        out_shape=jax.ShapeDtypeStruct((t, D_MODEL), F32),
        scratch_shapes=[pltpu.VMEM((tm, D_FF), BF16), pltpu.VMEM((HALO + tm, tc), F32),
                        pltpu.VMEM((HALO + tm, tc), F32), pltpu.VMEM((2 * D_FF // tc, HALO, tc), F32)],
        compiler_params=_cparams(("arbitrary",)),
        name="layer_tail",
    )(mix_a, mix_b, h, *params)


def _even_w_in(w):
    w_t = w.T
    a_end = 4 * EVA_PART
    gates = w_t[a_end:a_end + 2 * GDN_HEADS]
    qkv_b = w_t[a_end + 2 * GDN_HEADS:]
    pad = jnp.zeros((EVB_COLS - EVB_BA - 2 * GDN_HEADS, w.shape[0]), w.dtype)
    return w_t[:a_end].astype(BF16), jnp.concatenate([qkv_b, gates, pad], axis=0).astype(BF16)


def _odd_w_in(w):
    w_t = w.T
    pad = jnp.zeros((LANES - GLA_RANK, w.shape[0]), w.dtype)
    return w_t[:OD_COLS].astype(BF16), jnp.concatenate([w_t[OD_COLS:], pad], axis=0).astype(BF16)


def _even_mixer(h, rel_bias, w_in, conv_w, a_log, dt_bias, norm_w, w_out, bsz, seq):
    w_a, w_b = _even_w_in(w_in)
    proj_a, proj_b = _even_in_proj(h, w_a, w_b, conv_w, seq, PROJ_TM)
    gate_pad = jnp.zeros((1, LANES), F32)
    alog_pad = lax.dynamic_update_slice(gate_pad, a_log[None].astype(F32), (0, GDN_HEADS))
    dtb_pad = lax.dynamic_update_slice(gate_pad, dt_bias[None].astype(F32), (0, GDN_HEADS))
    o_a = _gdn(proj_a, proj_b, alog_pad, dtb_pad, norm_w[None], bsz, seq)
    tiles = _bias_tiles(rel_bias, *_swa_bias_tables(), scale=LOG2E)
    ng, two_blk = len(SWA_CONFIGS), 2 * SWA_BLOCK
    bias = tiles.reshape(SWA_HEADS // 2, 2, 2, ng, SWA_BLOCK, two_blk).transpose(3, 0, 2, 1, 4, 5)
    o_b = _swa(proj_b, bias.reshape(ng, SWA_HEADS // 2, 2, two_blk, two_blk), bsz, seq)
    w_out = w_out.astype(BF16)
    return o_a, o_b, w_out[:EVA_PART], w_out[EVA_PART:]


def _odd_mixer(h, rel_bias, w_in, lam_params, diff_norm_w, w_gate, b_gate, gla_norm_w, w_out,
               lam_init, bsz, seq):
    w_main, w_gd = _odd_w_in(w_in)
    proj, gate_in = _odd_in_proj(h, w_main, w_gd, PROJ_TM)
    blk = min(DIFF_BLOCK, seq)
    bucket, neg = _diff_bias_tables(blk)
    bias = _bias_tiles(rel_bias, bucket, neg, scale=LOG2E, base_bucket=NUM_BUCKETS - 1, toeplitz=True)
    o_c = _diff_attention(proj, bias, lam_params, diff_norm_w[None], lam_init, bsz, seq, blk)
    w_gate_pad = jnp.concatenate(
        [w_gate, jnp.zeros((LANES - GLA_RANK, w_gate.shape[1]), w_gate.dtype)], axis=0)
    o_d = _gla(proj, gate_in, w_gate_pad, b_gate[None], gla_norm_w[None], bsz, seq)
    diff_v = DIFF_HEADS * 2 * DIFF_DH
    w_out = w_out.astype(BF16)
    return o_c, o_d, w_out[:diff_v], w_out[diff_v:]


def kernel(x, rel_bias, w_in_even, gdn_conv_w, gdn_a_log, gdn_dt_bias, gdn_norm_w, w_out_even,
           w_in_odd, diff_lambda, diff_norm_w, gla_w_gate, gla_b_gate, gla_norm_w, w_out_odd,
           ffn_w_up, ffn_conv_w, ffn_conv_b, ffn_w_down, ln_g, ln_b):
    bsz, seq, d = x.shape
    h = x.reshape(bsz * seq, d)
    w_up, w_down = ffn_w_up.astype(BF16), ffn_w_down.astype(BF16)
    for layer in range(DEPTH):
        i = layer // 2
        if layer % 2 == 0:
            mixed = _even_mixer(h, rel_bias, w_in_even[i], gdn_conv_w[i], gdn_a_log[i], gdn_dt_bias[i],
                                gdn_norm_w[i], w_out_even[i], bsz, seq)
        else:
            lam_init = 0.8 - 0.6 * math.exp(-0.3 * layer)
            mixed = _odd_mixer(h, rel_bias, w_in_odd[i], diff_lambda[i], diff_norm_w[i], gla_w_gate[i],
                               gla_b_gate[i], gla_norm_w[i], w_out_odd[i], lam_init, bsz, seq)
        h = _layer_tail(*mixed[:2], h, *mixed[2:], ln_g[layer, 0][None], ln_b[layer, 0][None],
                        w_up, ffn_conv_w[layer], ffn_conv_b[layer][None], w_down,
                        ln_g[layer, 1][None], ln_b[layer, 1][None], layer, seq)
    return h.reshape(bsz, seq, d)
```
